```python
import jax, jax.numpy as jnp
from jax import lax
import numpy as np

D_MODEL = 1024
BATCH = 8
SEQ = 8192
DEPTH = 2

GM_WIDTH = D_MODEL
GM_GROUPS = 8
GM_GROUP_DIM = GM_WIDTH // GM_GROUPS
GM_CHUNK = 128
SSM_D_INNER = 2 * D_MODEL
SSM_HEAD_DIM = 64
SSM_HEADS = SSM_D_INNER // SSM_HEAD_DIM
SSM_GROUPS = 8
SSM_HEADS_PER_GROUP = SSM_HEADS // SSM_GROUPS
SSM_STATE = 128
SSM_CONV = 4
SSM_CHUNK = 128
SSM_CONV_DIM = SSM_D_INNER + 2 * SSM_GROUPS * SSM_STATE
N_BRANCH = 2
SPLITS = (
    GM_WIDTH,
    2 * GM_WIDTH,
    3 * GM_WIDTH,
    3 * GM_WIDTH + SSM_D_INNER,
    3 * GM_WIDTH + SSM_D_INNER + SSM_CONV_DIM,
    3 * GM_WIDTH + SSM_D_INNER + SSM_CONV_DIM + SSM_HEADS,
    3 * GM_WIDTH + SSM_D_INNER + SSM_CONV_DIM + SSM_HEADS + D_MODEL,
)
N_IN = 3 * GM_WIDTH + SSM_D_INNER + SSM_CONV_DIM + SSM_HEADS + N_BRANCH * D_MODEL
EPS = 1e-6

kernel_name = "hybrid_gmlp_ssd_gated_merge"


def rms_norm(x, w):
    xf = x.astype(jnp.float32)
    y = xf * lax.rsqrt(jnp.mean(xf * xf, axis=-1, keepdims=True) + EPS)
    return (y * w.astype(jnp.float32)).astype(x.dtype)


def layer_norm(x, w, b):
    xf = x.astype(jnp.float32)
    mu = jnp.mean(xf, axis=-1, keepdims=True)
    var = jnp.mean(jnp.square(xf - mu), axis=-1, keepdims=True)
    y = (xf - mu) * lax.rsqrt(var + EPS)
    return (y * w.astype(jnp.float32) + b.astype(jnp.float32)).astype(x.dtype)


def spatial_gating(u, v, ln_w, ln_b, w_s, b_s):
    b, s, _ = v.shape
    v = layer_norm(v, ln_w, ln_b)
    v = v.reshape(b, s // GM_CHUNK, GM_CHUNK, GM_GROUPS, GM_GROUP_DIM)
    mask = jnp.tril(jnp.ones((GM_CHUNK, GM_CHUNK), dtype=bool))
    w = jnp.where(mask[None], w_s, jnp.zeros_like(w_s))
    mixed = jnp.einsum('gts,bnsgd->bntgd', w, v) + b_s.T[None, None, :, :, None]
    return u * mixed.reshape(b, s, GM_WIDTH)


def causal_depthwise_conv(x, w, bias):
    k, ch = w.shape
    out = lax.conv_general_dilated(
        x, w[:, None, :], window_strides=(1,), padding=[(k - 1, 0)],
        dimension_numbers=('NWC', 'WIO', 'NWC'), feature_group_count=ch)
    return out + bias


def segsum_exp(a):
    t = a.shape[-1]
    cs = jnp.cumsum(a, axis=-1)
    diff = cs[..., :, None] - cs[..., None, :]
    mask = jnp.tril(jnp.ones((t, t), dtype=bool))
    return jnp.exp(jnp.where(mask, diff, -jnp.inf))


def ssd_scan(x, dt, a, bmat, cmat):
    b, s, h, p = x.shape
    q = SSM_CHUNK
    nc = s // q
    g, r, n = SSM_GROUPS, SSM_HEADS_PER_GROUP, SSM_STATE
    xd = (x * dt[..., None]).reshape(b, nc, q, g, r, p)
    adt = jnp.moveaxis((dt * a).astype(jnp.float32).reshape(b, nc, q, g, r), 2, -1)
    a_cs = jnp.cumsum(adt, axis=-1)
    bc = bmat.reshape(b, nc, q, g, n)
    cc = cmat.reshape(b, nc, q, g, n)
    decay = segsum_exp(adt)
    cb = jnp.einsum('bclgn,bcsgn->bcgls', cc, bc)
    y_diag = jnp.einsum('bcgls,bcgrls,bcsgrp->bclgrp', cb, decay, xd)
    decay_to_end = jnp.exp(a_cs[..., -1:] - a_cs)
    chunk_states = jnp.einsum('bcsgn,bcgrs,bcsgrp->bcgrpn', bc, decay_to_end, xd).astype(jnp.float32)
    chunk_decay = jnp.exp(a_cs[..., -1])

    def step(state, inp):
        cs, cd = inp
        return state * cd[..., None, None] + cs, state

    init = jnp.zeros((b, g, r, p, n), jnp.float32)
    _, prev_states = lax.scan(step, init, (jnp.moveaxis(chunk_states, 1, 0), jnp.moveaxis(chunk_decay, 1, 0)))
    prev_states = jnp.moveaxis(prev_states, 0, 1)
    y_off = jnp.einsum('bclgn,bcgrpn,bcgrl->bclgrp', cc, prev_states, jnp.exp(a_cs))
    return (y_diag + y_off).reshape(b, s, h, p).astype(x.dtype)


def hybrid_layer(x, c, ada_w, ada_b, norm_w, w_in, gm_ln_w, gm_ln_b, gm_ws, gm_bs,
                 conv_w, conv_b, dt_bias, a_log, d_skip, ssm_norm_w, w_proj_a, w_proj_b, w_out):
    b, s, _ = x.shape
    mod = jax.nn.silu(c) @ ada_w + ada_b
    shift, scale, gate = jnp.split(mod, 3, axis=-1)
    h = rms_norm(x, norm_w) * (1 + scale[:, None, :]) + shift[:, None, :]
    proj = h @ w_in
    gm_u, gm_v, gm_z, ssm_z, xbc, dt_raw, g_a, g_b = jnp.split(proj, SPLITS, axis=-1)
    y_a = spatial_gating(jax.nn.gelu(gm_u), jax.nn.gelu(gm_v), gm_ln_w, gm_ln_b, gm_ws, gm_bs) * jax.nn.silu(gm_z)
    xbc = jax.nn.silu(causal_depthwise_conv(xbc, conv_w, conv_b))
    xs, bm, cm = jnp.split(xbc, (SSM_D_INNER, SSM_D_INNER + SSM_GROUPS * SSM_STATE), axis=-1)
    dt = jax.nn.softplus((dt_raw + dt_bias).astype(jnp.float32))
    a = -jnp.exp(a_log.astype(jnp.float32))
    xh = xs.reshape(b, s, SSM_HEADS, SSM_HEAD_DIM)
    y_b = ssd_scan(xh, dt, a, bm.reshape(b, s, SSM_GROUPS, SSM_STATE), cm.reshape(b, s, SSM_GROUPS, SSM_STATE))
    y_b = y_b + xh * d_skip[:, None]
    yz = (y_b.reshape(b, s, SSM_D_INNER) * jax.nn.silu(ssm_z)).reshape(b, s, SSM_GROUPS, SSM_D_INNER // SSM_GROUPS)
    y_b = rms_norm(yz, ssm_norm_w.reshape(SSM_GROUPS, -1)).reshape(b, s, SSM_D_INNER)
    merged = jax.nn.sigmoid(g_a) * (y_a @ w_proj_a) + jax.nn.sigmoid(g_b) * (y_b @ w_proj_b)
    return x + gate[:, None, :] * (merged @ w_out)


def _fwd_setup_inputs(seed: int = 0) -> dict:
    key = jax.random.key(seed)
    ks = jax.random.split(key, 24)
    nrm = jax.random.normal
    L, D = DEPTH, D_MODEL
    dt0 = jnp.exp(jax.random.uniform(ks[10], (L, SSM_HEADS), minval=np.log(1e-3), maxval=np.log(1e-1)))
    return {
        "x": nrm(ks[0], (BATCH, SEQ, D), jnp.float32),
        "c": nrm(ks[1], (BATCH, D), jnp.float32),
        "ada_w": nrm(ks[2], (L, D, 3 * D), jnp.float32) * D ** -0.5,
        "ada_b": 0.01 * nrm(ks[3], (L, 3 * D), jnp.float32),
        "norm_w": 1.0 + 0.1 * nrm(ks[4], (L, D), jnp.float32),
        "w_in": nrm(ks[5], (L, D, N_IN), jnp.float32) * D ** -0.5,
        "gm_ln_w": 1.0 + 0.1 * nrm(ks[6], (L, GM_WIDTH), jnp.float32),
        "gm_ln_b": 0.01 * nrm(ks[7], (L, GM_WIDTH), jnp.float32),
        "gm_ws": nrm(ks[8], (L, GM_GROUPS, GM_CHUNK, GM_CHUNK), jnp.float32) * GM_CHUNK ** -0.5,
        "gm_bs": 1.0 + 0.1 * nrm(ks[9], (L, GM_GROUPS, GM_CHUNK), jnp.float32),
        "conv_w": nrm(ks[11], (L, SSM_CONV, SSM_CONV_DIM), jnp.float32) * SSM_CONV ** -0.5,
        "conv_b": 0.01 * nrm(ks[12], (L, SSM_CONV_DIM), jnp.float32),
        "dt_bias": dt0 + jnp.log(-jnp.expm1(-dt0)),
        "a_log": jnp.log(jax.random.uniform(ks[13], (L, SSM_HEADS), minval=1.0, maxval=16.0)),
        "d_skip": 1.0 + 0.1 * nrm(ks[14], (L, SSM_HEADS), jnp.float32),
        "ssm_norm_w": 1.0 + 0.1 * nrm(ks[15], (L, SSM_D_INNER), jnp.float32),
        "w_proj_a": nrm(ks[16], (L, GM_WIDTH, D), jnp.float32) * GM_WIDTH ** -0.5,
        "w_proj_b": nrm(ks[17], (L, SSM_D_INNER, D), jnp.float32) * SSM_D_INNER ** -0.5,
        "w_out": nrm(ks[18], (L, D, D), jnp.float32) * D ** -0.5,
        "final_norm_w": 1.0 + 0.1 * nrm(ks[19], (D,), jnp.float32),
    }


def _fwd_reference(x, c, ada_w, ada_b, norm_w, w_in, gm_ln_w, gm_ln_b, gm_ws, gm_bs, conv_w, conv_b,
              dt_bias, a_log, d_skip, ssm_norm_w, w_proj_a, w_proj_b, w_out, final_norm_w):
    for i in range(DEPTH):
        x = hybrid_layer(x, c, ada_w[i], ada_b[i], norm_w[i], w_in[i], gm_ln_w[i], gm_ln_b[i],
                         gm_ws[i], gm_bs[i], conv_w[i], conv_b[i], dt_bias[i], a_log[i], d_skip[i],
                         ssm_norm_w[i], w_proj_a[i], w_proj_b[i], w_out[i])
    return rms_norm(x, final_norm_w)


import jax as _jax
import jax.numpy as _jnp

TWIN_FORMAT = 'train_step'
FWD_PARAMS = ['x', 'c', 'ada_w', 'ada_b', 'norm_w', 'w_in', 'gm_ln_w', 'gm_ln_b', 'gm_ws', 'gm_bs', 'conv_w', 'conv_b', 'dt_bias', 'a_log', 'd_skip', 'ssm_norm_w', 'w_proj_a', 'w_proj_b', 'w_out', 'final_norm_w']
TWIN_WEIGHTS = ['ada_w', 'ada_b', 'norm_w', 'w_in', 'gm_ln_w', 'gm_ln_b', 'gm_ws', 'gm_bs', 'conv_w', 'conv_b', 'dt_bias', 'a_log', 'd_skip', 'ssm_norm_w', 'w_proj_a', 'w_proj_b', 'w_out', 'final_norm_w']
TWIN_DIFF_INPUT = 'x'
TWIN_INPUTS = ['x', 'c', 'ada_w', 'ada_b', 'norm_w', 'w_in', 'gm_ln_w', 'gm_ln_b', 'gm_ws', 'gm_bs', 'conv_w', 'conv_b', 'dt_bias', 'a_log', 'd_skip', 'ssm_norm_w', 'w_proj_a', 'w_proj_b', 'w_out', 'final_norm_w', 'loss_target', 'm_ada_w', 'm_ada_b', 'm_norm_w', 'm_w_in', 'm_gm_ln_w', 'm_gm_ln_b', 'm_gm_ws', 'm_gm_bs', 'm_conv_w', 'm_conv_b', 'm_dt_bias', 'm_a_log', 'm_d_skip', 'm_ssm_norm_w', 'm_w_proj_a', 'm_w_proj_b', 'm_w_out', 'm_final_norm_w', 'v_ada_w', 'v_ada_b', 'v_norm_w', 'v_w_in', 'v_gm_ln_w', 'v_gm_ln_b', 'v_gm_ws', 'v_gm_bs', 'v_conv_w', 'v_conv_b', 'v_dt_bias', 'v_a_log', 'v_d_skip', 'v_ssm_norm_w', 'v_w_proj_a', 'v_w_proj_b', 'v_w_out', 'v_final_norm_w']
TWIN_OUTPUTS = ['loss', 'grad_x', 'grad_ada_w', 'grad_ada_b', 'grad_norm_w', 'grad_w_in', 'grad_gm_ln_w', 'grad_gm_ln_b', 'grad_gm_ws', 'grad_gm_bs', 'grad_conv_w', 'grad_conv_b', 'grad_dt_bias', 'grad_a_log', 'grad_d_skip', 'grad_ssm_norm_w', 'grad_w_proj_a', 'grad_w_proj_b', 'grad_w_out', 'grad_final_norm_w', 'delta_ada_w', 'delta_ada_b', 'delta_norm_w', 'delta_w_in', 'delta_gm_ln_w', 'delta_gm_ln_b', 'delta_gm_ws', 'delta_gm_bs', 'delta_conv_w', 'delta_conv_b', 'delta_dt_bias', 'delta_a_log', 'delta_d_skip', 'delta_ssm_norm_w', 'delta_w_proj_a', 'delta_w_proj_b', 'delta_w_out', 'delta_final_norm_w', 'new_m_ada_w', 'new_m_ada_b', 'new_m_norm_w', 'new_m_w_in', 'new_m_gm_ln_w', 'new_m_gm_ln_b', 'new_m_gm_ws', 'new_m_gm_bs', 'new_m_conv_w', 'new_m_conv_b', 'new_m_dt_bias', 'new_m_a_log', 'new_m_d_skip', 'new_m_ssm_norm_w', 'new_m_w_proj_a', 'new_m_w_proj_b', 'new_m_w_out', 'new_m_final_norm_w', 'new_v_ada_w', 'new_v_ada_b', 'new_v_norm_w', 'new_v_w_in', 'new_v_gm_ln_w', 'new_v_gm_ln_b', 'new_v_gm_ws', 'new_v_gm_bs', 'new_v_conv_w', 'new_v_conv_b', 'new_v_dt_bias', 'new_v_a_log', 'new_v_d_skip', 'new_v_ssm_norm_w', 'new_v_w_proj_a', 'new_v_w_proj_b', 'new_v_w_out', 'new_v_final_norm_w']
TWIN_LEAF_KINDS = {'loss': 'loss', 'grad_x': 'grad_x', 'grad_ada_w': 'grad_w', 'grad_ada_b': 'grad_w', 'grad_norm_w': 'grad_w', 'grad_w_in': 'grad_w', 'grad_gm_ln_w': 'grad_w', 'grad_gm_ln_b': 'grad_w', 'grad_gm_ws': 'grad_w', 'grad_gm_bs': 'grad_w', 'grad_conv_w': 'grad_w', 'grad_conv_b': 'grad_w', 'grad_dt_bias': 'grad_w', 'grad_a_log': 'grad_w', 'grad_d_skip': 'grad_w', 'grad_ssm_norm_w': 'grad_w', 'grad_w_proj_a': 'grad_w', 'grad_w_proj_b': 'grad_w', 'grad_w_out': 'grad_w', 'grad_final_norm_w': 'grad_w', 'delta_ada_w': 'delta_w', 'delta_ada_b': 'delta_w', 'delta_norm_w': 'delta_w', 'delta_w_in': 'delta_w', 'delta_gm_ln_w': 'delta_w', 'delta_gm_ln_b': 'delta_w', 'delta_gm_ws': 'delta_w', 'delta_gm_bs': 'delta_w', 'delta_conv_w': 'delta_w', 'delta_conv_b': 'delta_w', 'delta_dt_bias': 'delta_w', 'delta_a_log': 'delta_w', 'delta_d_skip': 'delta_w', 'delta_ssm_norm_w': 'delta_w', 'delta_w_proj_a': 'delta_w', 'delta_w_proj_b': 'delta_w', 'delta_w_out': 'delta_w', 'delta_final_norm_w': 'delta_w', 'new_m_ada_w': 'new_m', 'new_m_ada_b': 'new_m', 'new_m_norm_w': 'new_m', 'new_m_w_in': 'new_m', 'new_m_gm_ln_w': 'new_m', 'new_m_gm_ln_b': 'new_m', 'new_m_gm_ws': 'new_m', 'new_m_gm_bs': 'new_m', 'new_m_conv_w': 'new_m', 'new_m_conv_b': 'new_m', 'new_m_dt_bias': 'new_m', 'new_m_a_log': 'new_m', 'new_m_d_skip': 'new_m', 'new_m_ssm_norm_w': 'new_m', 'new_m_w_proj_a': 'new_m', 'new_m_w_proj_b': 'new_m', 'new_m_w_out': 'new_m', 'new_m_final_norm_w': 'new_m', 'new_v_ada_w': 'new_v', 'new_v_ada_b': 'new_v', 'new_v_norm_w': 'new_v', 'new_v_w_in': 'new_v', 'new_v_gm_ln_w': 'new_v', 'new_v_gm_ln_b': 'new_v', 'new_v_gm_ws': 'new_v', 'new_v_gm_bs': 'new_v', 'new_v_conv_w': 'new_v', 'new_v_conv_b': 'new_v', 'new_v_dt_bias': 'new_v', 'new_v_a_log': 'new_v', 'new_v_d_skip': 'new_v', 'new_v_ssm_norm_w': 'new_v', 'new_v_w_proj_a': 'new_v', 'new_v_w_proj_b': 'new_v', 'new_v_w_out': 'new_v', 'new_v_final_norm_w': 'new_v'}


def _forward(args):
    return _fwd_reference(*[args[k] for k in FWD_PARAMS])


def _output_shape():
    def fwd():
        inp = _fwd_setup_inputs(0)
        return _fwd_reference(*[inp[k] for k in FWD_PARAMS])
    out = _jax.eval_shape(fwd)
    return out.shape, out.dtype

N_MICROBATCH = 1
ADAM_LR = 0.001
ADAM_B1 = 0.9
ADAM_B2 = 0.999
ADAM_EPS = 1e-08
ADAM_WD = 0.01
ADAM_STEP = 10
PER_EXAMPLE_BATCH_AXIS = {'x': 0, 'c': 0, 'loss_target': 0}
SHARED_INPUTS = []
_WEIGHT_DTYPES = {'ada_w': _jnp.float32, 'ada_b': _jnp.float32, 'norm_w': _jnp.float32, 'w_in': _jnp.float32, 'gm_ln_w': _jnp.float32, 'gm_ln_b': _jnp.float32, 'gm_ws': _jnp.float32, 'gm_bs': _jnp.float32, 'conv_w': _jnp.float32, 'conv_b': _jnp.float32, 'dt_bias': _jnp.float32, 'a_log': _jnp.float32, 'd_skip': _jnp.float32, 'ssm_norm_w': _jnp.float32, 'w_proj_a': _jnp.float32, 'w_proj_b': _jnp.float32, 'w_out': _jnp.float32, 'final_norm_w': _jnp.float32}
MOMENT_SCALE = {'ada_w': 2.971616e-01, 'ada_b': 6.525716e-01, 'norm_w': 1.557037e-01, 'w_in': 5.830260e-02, 'gm_ln_w': 4.323821e-02, 'gm_ln_b': 4.718897e-02, 'gm_ws': 4.126610e-02, 'gm_bs': 5.800489e-02, 'conv_w': 5.372165e-02, 'conv_b': 6.178022e-02, 'dt_bias': 1.363026e-01, 'a_log': 2.429426e-01, 'd_skip': 1.827782e-01, 'ssm_norm_w': 8.073361e-02, 'w_proj_a': 8.357595e-02, 'w_proj_b': 1.122209e-01, 'w_out': 1.418969e-01, 'final_norm_w': 6.445441e+01}


def _to_microbatches(a, axis):
    t = _jnp.moveaxis(a, axis, 0)
    t = t.reshape((N_MICROBATCH, t.shape[0] // N_MICROBATCH) + t.shape[1:])
    return _jnp.moveaxis(t, 1, axis + 1)


def setup_inputs(seed: int = 0) -> dict:
    inp = _fwd_setup_inputs(seed)
    key = _jax.random.fold_in(_jax.random.key(seed), 7919)
    shape, _ = _output_shape()
    out = dict(inp)
    out["loss_target"] = _jax.random.normal(_jax.random.fold_in(key, 0), shape, _jnp.float32)
    for i, name in enumerate(TWIN_WEIGHTS):
        w = inp[name].astype(_jnp.float32)
        if MOMENT_SCALE is None:
            s = _jnp.sqrt(_jnp.mean(_jnp.square(w)) + 1e-30)
        else:
            s = MOMENT_SCALE[name]
        km, kv = _jax.random.split(_jax.random.fold_in(key, i + 1))
        out[name] = w
        out["m_" + name] = s * _jax.random.normal(km, w.shape, _jnp.float32)
        out["v_" + name] = (s * s) * _jax.random.uniform(kv, w.shape, _jnp.float32, 0.5, 1.5)
    if N_MICROBATCH > 1:
        for name, axis in PER_EXAMPLE_BATCH_AXIS.items():
            out[name] = _to_microbatches(out[name], axis)
    return {'x': out['x'], 'c': out['c'], 'ada_w': out['ada_w'], 'ada_b': out['ada_b'], 'norm_w': out['norm_w'], 'w_in': out['w_in'], 'gm_ln_w': out['gm_ln_w'], 'gm_ln_b': out['gm_ln_b'], 'gm_ws': out['gm_ws'], 'gm_bs': out['gm_bs'], 'conv_w': out['conv_w'], 'conv_b': out['conv_b'], 'dt_bias': out['dt_bias'], 'a_log': out['a_log'], 'd_skip': out['d_skip'], 'ssm_norm_w': out['ssm_norm_w'], 'w_proj_a': out['w_proj_a'], 'w_proj_b': out['w_proj_b'], 'w_out': out['w_out'], 'final_norm_w': out['final_norm_w'], 'loss_target': out['loss_target'], 'm_ada_w': out['m_ada_w'], 'm_ada_b': out['m_ada_b'], 'm_norm_w': out['m_norm_w'], 'm_w_in': out['m_w_in'], 'm_gm_ln_w': out['m_gm_ln_w'], 'm_gm_ln_b': out['m_gm_ln_b'], 'm_gm_ws': out['m_gm_ws'], 'm_gm_bs': out['m_gm_bs'], 'm_conv_w': out['m_conv_w'], 'm_conv_b': out['m_conv_b'], 'm_dt_bias': out['m_dt_bias'], 'm_a_log': out['m_a_log'], 'm_d_skip': out['m_d_skip'], 'm_ssm_norm_w': out['m_ssm_norm_w'], 'm_w_proj_a': out['m_w_proj_a'], 'm_w_proj_b': out['m_w_proj_b'], 'm_w_out': out['m_w_out'], 'm_final_norm_w': out['m_final_norm_w'], 'v_ada_w': out['v_ada_w'], 'v_ada_b': out['v_ada_b'], 'v_norm_w': out['v_norm_w'], 'v_w_in': out['v_w_in'], 'v_gm_ln_w': out['v_gm_ln_w'], 'v_gm_ln_b': out['v_gm_ln_b'], 'v_gm_ws': out['v_gm_ws'], 'v_gm_bs': out['v_gm_bs'], 'v_conv_w': out['v_conv_w'], 'v_conv_b': out['v_conv_b'], 'v_dt_bias': out['v_dt_bias'], 'v_a_log': out['v_a_log'], 'v_d_skip': out['v_d_skip'], 'v_ssm_norm_w': out['v_ssm_norm_w'], 'v_w_proj_a': out['v_w_proj_a'], 'v_w_proj_b': out['v_w_proj_b'], 'v_w_out': out['v_w_out'], 'v_final_norm_w': out['v_final_norm_w']}


def _loss(weights, diff, rest, loss_target):
    with _jax.named_scope("forward"):
        args = {**rest, TWIN_DIFF_INPUT: diff, **{k: w.astype(_WEIGHT_DTYPES[k]) for k, w in weights.items()}}
        y = _forward(args)
    with _jax.named_scope("loss_head"):
        err = _jnp.square(y.astype(_jnp.float32) - loss_target)
        return 0.5 * _jnp.sum(_jnp.mean(err, axis=-1)) if err.ndim else 0.5 * err


def _adamw(w, g, m, v):
    m = ADAM_B1 * m + (1.0 - ADAM_B1) * g
    v = ADAM_B2 * v + (1.0 - ADAM_B2) * _jnp.square(g)
    m_hat = m / (1.0 - ADAM_B1 ** ADAM_STEP)
    v_hat = v / (1.0 - ADAM_B2 ** ADAM_STEP)
    delta = -ADAM_LR * (m_hat / (_jnp.sqrt(v_hat) + ADAM_EPS) + ADAM_WD * w)
    return delta, m, v


def reference(x, c, ada_w, ada_b, norm_w, w_in, gm_ln_w, gm_ln_b, gm_ws, gm_bs, conv_w, conv_b, dt_bias, a_log, d_skip, ssm_norm_w, w_proj_a, w_proj_b, w_out, final_norm_w, loss_target, m_ada_w, m_ada_b, m_norm_w, m_w_in, m_gm_ln_w, m_gm_ln_b, m_gm_ws, m_gm_bs, m_conv_w, m_conv_b, m_dt_bias, m_a_log, m_d_skip, m_ssm_norm_w, m_w_proj_a, m_w_proj_b, m_w_out, m_final_norm_w, v_ada_w, v_ada_b, v_norm_w, v_w_in, v_gm_ln_w, v_gm_ln_b, v_gm_ws, v_gm_bs, v_conv_w, v_conv_b, v_dt_bias, v_a_log, v_d_skip, v_ssm_norm_w, v_w_proj_a, v_w_proj_b, v_w_out, v_final_norm_w):
    given = dict(x=x, c=c, ada_w=ada_w, ada_b=ada_b, norm_w=norm_w, w_in=w_in, gm_ln_w=gm_ln_w, gm_ln_b=gm_ln_b, gm_ws=gm_ws, gm_bs=gm_bs, conv_w=conv_w, conv_b=conv_b, dt_bias=dt_bias, a_log=a_log, d_skip=d_skip, ssm_norm_w=ssm_norm_w, w_proj_a=w_proj_a, w_proj_b=w_proj_b, w_out=w_out, final_norm_w=final_norm_w, loss_target=loss_target, m_ada_w=m_ada_w, m_ada_b=m_ada_b, m_norm_w=m_norm_w, m_w_in=m_w_in, m_gm_ln_w=m_gm_ln_w, m_gm_ln_b=m_gm_ln_b, m_gm_ws=m_gm_ws, m_gm_bs=m_gm_bs, m_conv_w=m_conv_w, m_conv_b=m_conv_b, m_dt_bias=m_dt_bias, m_a_log=m_a_log, m_d_skip=m_d_skip, m_ssm_norm_w=m_ssm_norm_w, m_w_proj_a=m_w_proj_a, m_w_proj_b=m_w_proj_b, m_w_out=m_w_out, m_final_norm_w=m_final_norm_w, v_ada_w=v_ada_w, v_ada_b=v_ada_b, v_norm_w=v_norm_w, v_w_in=v_w_in, v_gm_ln_w=v_gm_ln_w, v_gm_ln_b=v_gm_ln_b, v_gm_ws=v_gm_ws, v_gm_bs=v_gm_bs, v_conv_w=v_conv_w, v_conv_b=v_conv_b, v_dt_bias=v_dt_bias, v_a_log=v_a_log, v_d_skip=v_d_skip, v_ssm_norm_w=v_ssm_norm_w, v_w_proj_a=v_w_proj_a, v_w_proj_b=v_w_proj_b, v_w_out=v_w_out, v_final_norm_w=v_final_norm_w)
    weights = {n: given[n] for n in TWIN_WEIGHTS}
    shared = {n: given[n] for n in SHARED_INPUTS}
    per_example = {n: given[n] for n in ['x', 'c']}
    grad_fn = _jax.value_and_grad(_loss, argnums=(0, 1))

    def one_microbatch(ex, loss_target):
        ex = dict(ex)
        diff = ex.pop(TWIN_DIFF_INPUT)
        return grad_fn(weights, diff, {**shared, **ex}, loss_target)

    if N_MICROBATCH == 1:
        loss, (grad_w, grad_x) = one_microbatch(per_example, given["loss_target"])
    else:
        def body(carry, xs):
            loss_sum, grad_sum = carry
            l_k, (gw_k, gx_k) = one_microbatch(xs[0], xs[1])
            with _jax.named_scope("update"):
                return (loss_sum + l_k, _jax.tree.map(_jnp.add, grad_sum, gw_k)), gx_k

        init = (_jnp.zeros((), _jnp.float32), _jax.tree.map(_jnp.zeros_like, weights))
        (loss, grad_w), grad_x = _jax.lax.scan(body, init, (per_example, given["loss_target"]))
    with _jax.named_scope("update"):
        delta_w, new_m, new_v = {}, {}, {}
        for n in TWIN_WEIGHTS:
            delta_w[n], new_m[n], new_v[n] = _adamw(weights[n], grad_w[n], given["m_" + n], given["v_" + n])
    return (loss, grad_x, *[grad_w[n] for n in TWIN_WEIGHTS], *[delta_w[n] for n in TWIN_WEIGHTS],
            *[new_m[n] for n in TWIN_WEIGHTS], *[new_v[n] for n in TWIN_WEIGHTS])
```

```python
import jax
import jax.numpy as jnp
from jax import lax
from jax.experimental import pallas as pl
from jax.experimental.pallas import tpu as pltpu

f32 = jnp.float32
bf16 = jnp.bfloat16

D = 1024
DEPTH = 2
EPS = 1e-6
CH = 128
NG = 8
HPG = 4
HD = 64
NH = NG * HPG
NST = 128
DIN = 2048
CONVD = 4096
GW = DIN // NG
PB = CONVD + DIN + 256
PA = 3 * D
PG = 2 * D
N_IN = 11296
NSHARD = 4
V7X_VMEM_BYTES = 64 * 2 ** 20
VMEM_LIMIT = V7X_VMEM_BYTES - 8 * 2 ** 20

ADAM_LR, ADAM_B1, ADAM_B2, ADAM_EPS, ADAM_WD, ADAM_STEP = 0.001, 0.9, 0.999, 1e-08, 0.01, 10

_HI = lax.Precision.HIGHEST


def _cp(sem):
    return pltpu.CompilerParams(dimension_semantics=sem, vmem_limit_bytes=VMEM_LIMIT)


def _sigmoid(x):
    return 1.0 / (1.0 + jnp.exp(-x))


def _silu_and_grad(x):
    s = _sigmoid(x)
    return x * s, s * (1.0 + x * (1.0 - s))


_GELU_K = 0.7978845608028654
_GELU_C = 0.044715


def _gelu_and_grad(x):
    x2 = x * x
    t = jnp.tanh(_GELU_K * (x + _GELU_C * x * x2))
    g = 0.5 * x * (1.0 + t)
    dg = 0.5 * (1.0 + t) + 0.5 * x * (1.0 - t * t) * _GELU_K * (1.0 + 3.0 * _GELU_C * x2)
    return g, dg


def _gelu(x):
    t = jnp.tanh(_GELU_K * (x + _GELU_C * x * x * x))
    return 0.5 * x * (1.0 + t)


def _softplus(x):
    return jnp.maximum(x, 0.0) + jnp.log(1.0 + jnp.exp(-jnp.abs(x)))


def _dot(a, b):
    return jnp.dot(a, b, preferred_element_type=f32)


def _dot_nt(a, b):
    return lax.dot_general(a, b, (((1,), (1,)), ((), ())), preferred_element_type=f32)


def _dot_tn(a, b):
    return lax.dot_general(a, b, (((0,), (0,)), ((), ())), preferred_element_type=f32)


def _dot_hi(a, b):
    return jnp.dot(a, b, precision=_HI, preferred_element_type=f32)


def _rmsmod_fwd(x, nw, scale, shift, name):
    s = x.shape[0]
    ts = min(512, s)

    def body(x_ref, nw_ref, sc_ref, sh_ref, h_ref):
        xv = x_ref[...]
        r = lax.rsqrt(jnp.mean(xv * xv, axis=-1, keepdims=True) + EPS)
        h_ref[...] = ((xv * r) * nw_ref[...] * (1.0 + sc_ref[...]) + sh_ref[...]).astype(bf16)

    row = pl.BlockSpec((1, D), lambda i: (0, 0))
    tile = pl.BlockSpec((ts, D), lambda i: (i, 0))
    return pl.pallas_call(
        body, grid=(s // ts,), in_specs=[tile, row, row, row], out_specs=tile,
        out_shape=jax.ShapeDtypeStruct((s, D), bf16), name=name, compiler_params=_cp(("parallel",)),
    )(x, nw, scale, shift)


def _rmsmod_bwd(dh, x, dres, nw, scale, name):
    s = x.shape[0]
    ts = min(512, s)

    def body(dh_ref, x_ref, dres_ref, nw_ref, sc_ref, dx_ref, dsc_ref, dsh_ref, dnw_ref):
        @pl.when(pl.program_id(0) == 0)
        def _():
            dsc_ref[...] = jnp.zeros_like(dsc_ref)
            dsh_ref[...] = jnp.zeros_like(dsh_ref)
            dnw_ref[...] = jnp.zeros_like(dnw_ref)
        xv = x_ref[...]
        dhv = dh_ref[...]
        r = lax.rsqrt(jnp.mean(xv * xv, axis=-1, keepdims=True) + EPS)
        xn = xv * r
        one_sc = 1.0 + sc_ref[...]
        dsc_ref[...] += jnp.sum(dhv * xn * nw_ref[...], axis=0, keepdims=True)
        dsh_ref[...] += jnp.sum(dhv, axis=0, keepdims=True)
        dnw_ref[...] += jnp.sum(dhv * xn * one_sc, axis=0, keepdims=True)
        dxn = dhv * (nw_ref[...] * one_sc)
        dx_ref[...] = r * (dxn - xn * jnp.mean(dxn * xn, axis=-1, keepdims=True)) + dres_ref[...]

    row = pl.BlockSpec((1, D), lambda i: (0, 0))
    tile = pl.BlockSpec((ts, D), lambda i: (i, 0))
    vec = jax.ShapeDtypeStruct((1, D), f32)
    return pl.pallas_call(
        body, grid=(s // ts,), in_specs=[tile, tile, tile, row, row], out_specs=[tile, row, row, row],
        out_shape=[jax.ShapeDtypeStruct((s, D), f32), vec, vec, vec], name=name, compiler_params=_cp(("arbitrary",)),
    )(dh, x, dres, nw, scale)


def _pick(n, prefs):
    for p in prefs:
        if n % p == 0:
            return p
    return n


def _mm(a, b, out_dtype, name, c_in=None):
    m, k = a.shape
    n = b.shape[1]
    tm = _pick(m, (1024, 512, 256))
    tn = _pick(n, (1280, 1024, 512))
    tk = _pick(k, (1280, 1024, 512))
    nk = k // tk

    def body(*refs):
        if c_in is not None:
            a_ref, b_ref, c_ref, o_ref, acc = refs
        else:
            a_ref, b_ref, o_ref, acc = refs
        kk = pl.program_id(2)

        @pl.when(kk == 0)
        def _():
            if c_in is not None:
                acc[...] = c_ref[...]
            else:
                acc[...] = jnp.zeros_like(acc)
        acc[...] += _dot(a_ref[...], b_ref[...])

        @pl.when(kk == nk - 1)
        def _():
            o_ref[...] = acc[...].astype(out_dtype)

    in_specs = [pl.BlockSpec((tm, tk), lambda j, i, kk: (i, kk)), pl.BlockSpec((tk, tn), lambda j, i, kk: (kk, j))]
    args = [a, b]
    if c_in is not None:
        in_specs.append(pl.BlockSpec((tm, tn), lambda j, i, kk: (i, j)))
        args.append(c_in)
    return pl.pallas_call(
        body, grid=(n // tn, m // tm, nk), in_specs=in_specs, out_specs=pl.BlockSpec((tm, tn), lambda j, i, kk: (i, j)),
        out_shape=jax.ShapeDtypeStruct((m, n), out_dtype), scratch_shapes=[pltpu.VMEM((tm, tn), f32)], name=name,
        compiler_params=_cp(("parallel", "parallel", "arbitrary")),
    )(*args)


def _mm_tn(a, b, name):
    t, k1 = a.shape
    n = b.shape[1]
    t1 = _pick(k1, (1024, 512))
    tn = _pick(n, (1280, 1024, 512))
    tt = _pick(t, (512, 256))
    nt = t // tt

    def body(a_ref, b_ref, o_ref):
        tt_i = pl.program_id(2)

        @pl.when(tt_i == 0)
        def _():
            o_ref[...] = jnp.zeros_like(o_ref)
        o_ref[...] += _dot_tn(a_ref[...], b_ref[...])

    return pl.pallas_call(
        body, grid=(k1 // t1, n // tn, nt),
        in_specs=[pl.BlockSpec((tt, t1), lambda i, j, tt_i: (tt_i, i)), pl.BlockSpec((tt, tn), lambda i, j, tt_i: (tt_i, j))],
        out_specs=pl.BlockSpec((t1, tn), lambda i, j, tt_i: (i, j)),
        out_shape=jax.ShapeDtypeStruct((k1, n), f32), name=name,
        compiler_params=_cp(("parallel", "parallel", "arbitrary")),
    )(a, b)


def _ln_stats(v):
    mu = jnp.mean(v, axis=-1, keepdims=True)
    vc = v - mu
    rstd = lax.rsqrt(jnp.mean(vc * vc, axis=-1, keepdims=True) + EPS)
    return vc * rstd, rstd


def _mix(w_ref, vl):
    return jnp.concatenate([_dot(w_ref[g], vl[:, g * CH:(g + 1) * CH]) for g in range(NG)], axis=1)


def _branch_a_fwd(proj_a, lnw, lnb, wsm, bsf, name):
    s = proj_a.shape[0]
    ta = min(256, s)

    def body(pu_ref, pv_ref, pz_ref, lnw_ref, lnb_ref, w_ref, bs_ref, ya_ref):
        for c in range(ta // CH):
            rows = pl.ds(c * CH, CH)
            vh, _ = _ln_stats(_gelu(pv_ref[rows, :]))
            vl = (vh * lnw_ref[...] + lnb_ref[...]).astype(bf16)
            mixed = _mix(w_ref, vl) + bs_ref[...]
            pz = pz_ref[rows, :]
            ya_ref[rows, :] = (_gelu(pu_ref[rows, :]) * mixed * (pz * _sigmoid(pz))).astype(bf16)

    row = pl.BlockSpec((1, D), lambda i: (0, 0))
    return pl.pallas_call(
        body, grid=(s // ta,),
        in_specs=[pl.BlockSpec((ta, D), lambda i: (i, 0)), pl.BlockSpec((ta, D), lambda i: (i, 1)),
                  pl.BlockSpec((ta, D), lambda i: (i, 2)), row, row,
                  pl.BlockSpec((NG, CH, CH), lambda i: (0, 0, 0)), pl.BlockSpec((CH, D), lambda i: (0, 0))],
        out_specs=pl.BlockSpec((ta, D), lambda i: (i, 0)),
        out_shape=jax.ShapeDtypeStruct((s, D), bf16), name=name, compiler_params=_cp(("parallel",)),
    )(proj_a, proj_a, proj_a, lnw, lnb, wsm, bsf)


def _branch_a_bwd(proj_a, dya, lnw, lnb, wsm, wsm_t, bsf, name):
    s = proj_a.shape[0]
    ta = min(256, s)

    def body(pu_ref, pv_ref, pz_ref, dya_ref, lnw_ref, lnb_ref, w_ref, wt_ref, bs_ref,
             dp_ref, dws_ref, dbs_ref, dlnw_ref, dlnb_ref):
        @pl.when(pl.program_id(0) == 0)
        def _():
            dws_ref[...] = jnp.zeros_like(dws_ref)
            dbs_ref[...] = jnp.zeros_like(dbs_ref)
            dlnw_ref[...] = jnp.zeros_like(dlnw_ref)
            dlnb_ref[...] = jnp.zeros_like(dlnb_ref)
        for c in range(ta // CH):
            rows = pl.ds(c * CH, CH)
            u, du = _gelu_and_grad(pu_ref[rows, :])
            v, dv_act = _gelu_and_grad(pv_ref[rows, :])
            zg, dzg = _silu_and_grad(pz_ref[rows, :])
            vh, rstd = _ln_stats(v)
            vl = (vh * lnw_ref[...] + lnb_ref[...]).astype(bf16)
            mixed = _mix(w_ref, vl) + bs_ref[...]
            dy = dya_ref[rows, :].astype(f32)
            dmixed = dy * u * zg
            dp_ref[rows, 0:D] = (dy * mixed * zg * du).astype(bf16)
            dp_ref[rows, 2 * D:3 * D] = (dy * u * mixed * dzg).astype(bf16)
            dmb = dmixed.astype(bf16)
            dbs_ref[...] += dmixed
            dvl = _mix(wt_ref, dmb)
            for g in range(NG):
                cols = slice(g * CH, (g + 1) * CH)
                dws_ref[g] += _dot_nt(dmb[:, cols], vl[:, cols])
            dlnw_ref[...] += jnp.sum(dvl * vh, axis=0, keepdims=True)
            dlnb_ref[...] += jnp.sum(dvl, axis=0, keepdims=True)
            dvh = dvl * lnw_ref[...]
            dv = rstd * (dvh - jnp.mean(dvh, axis=-1, keepdims=True) - vh * jnp.mean(dvh * vh, axis=-1, keepdims=True))
            dp_ref[rows, D:2 * D] = (dv * dv_act).astype(bf16)

    row = pl.BlockSpec((1, D), lambda i: (0, 0))
    wspec = pl.BlockSpec((NG, CH, CH), lambda i: (0, 0, 0))
    full = pl.BlockSpec((CH, D), lambda i: (0, 0))
    return pl.pallas_call(
        body, grid=(s // ta,),
        in_specs=[pl.BlockSpec((ta, D), lambda i: (i, 0)), pl.BlockSpec((ta, D), lambda i: (i, 1)),
                  pl.BlockSpec((ta, D), lambda i: (i, 2)), pl.BlockSpec((ta, D), lambda i: (i, 0)),
                  row, row, wspec, wspec, full],
        out_specs=[pl.BlockSpec((ta, PA), lambda i: (i, 0)), wspec, full, row, row],
        out_shape=[jax.ShapeDtypeStruct((s, PA), bf16), jax.ShapeDtypeStruct((NG, CH, CH), f32),
                   jax.ShapeDtypeStruct((CH, D), f32), jax.ShapeDtypeStruct((1, D), f32), jax.ShapeDtypeStruct((1, D), f32)],
        name=name, compiler_params=_cp(("arbitrary",)),
    )(proj_a, proj_a, proj_a, dya, lnw, lnb, wsm, wsm_t, bsf)


def _shift_rows(x, prev8, j):
    xr = pltpu.roll(x, j, 0)
    fix = pltpu.roll(prev8, j, 0)
    rid = lax.broadcasted_iota(jnp.int32, (8, x.shape[1]), 0)
    top = jnp.where(rid < j, fix, xr[0:8])
    return jnp.concatenate([top, xr[8:]], axis=0)


def _conv_pre(x, prev8, cw_ref, cb_ref):
    shifted = [_shift_rows(x, prev8, j) for j in (1, 2, 3)]
    conv = cb_ref[...] + cw_ref[3:4, :] * x
    for j in (1, 2, 3):
        conv = conv + cw_ref[3 - j:4 - j, :] * shifted[j - 1]
    return conv, shifted


def _tril_mask():
    return lax.broadcasted_iota(jnp.int32, (CH, CH), 0) >= lax.broadcasted_iota(jnp.int32, (CH, CH), 1)


def _branch_b_fwd(proj_b, cw, cb, sel, dtb_g, a_g, dk_g, snw_g, name):
    s = proj_b.shape[0]
    nc = s // CH

    def body(xbc_ref, sz_ref, dtr_ref, cw_ref, cb_ref, sel_ref, dtb_ref, a_ref, dk_ref, snw_ref,
             yb_ref, y_ref, st_ref, prev8, state, xs_s, b_s, c_s, y_s, yb_s):
        @pl.when(pl.program_id(0) == 0)
        def _():
            prev8[...] = jnp.zeros_like(prev8)
            state[...] = jnp.zeros_like(state)
        x = xbc_ref[...]
        conv, _ = _conv_pre(x, prev8[...], cw_ref, cb_ref)
        prev8[...] = x[CH - 8:CH]
        xc = conv * _sigmoid(conv)
        for g in range(NG):
            xs_s[g] = xc[:, g * GW:(g + 1) * GW]
            b_s[g] = xc[:, DIN + g * NST:DIN + (g + 1) * NST].astype(bf16)
            c_s[g] = xc[:, DIN + NG * NST + g * NST:DIN + NG * NST + (g + 1) * NST].astype(bf16)
        st_ref[0] = state[...].astype(bf16)
        raw = dtr_ref[:, 0:CH]
        mask = _tril_mask()
        tri = mask.astype(f32)

        def group(g, carry):
            dt = _softplus(_dot_hi(raw, sel_ref[g]) + dtb_ref[g])
            acs = _dot_hi(tri, dt * a_ref[g])
            acs_t = acs.T
            bg = b_s[g]
            cg = c_s[g]
            cbm = _dot_nt(cg, bg)
            xs = xs_s[g]
            dk = dk_ref[g]
            for r in range(HPG):
                col = acs[:, r:r + 1]
                row = acs_t[r:r + 1, :]
                lmat = jnp.exp(jnp.where(mask, col - row, -jnp.inf))
                xr = xs[:, r * HD:(r + 1) * HD]
                xd = xr * dt[:, r:r + 1]
                sp = state[g * HPG + r]
                alast = acs[CH - 1:CH, r:r + 1]
                y_r = _dot((cbm * lmat).astype(bf16), xd.astype(bf16))
                y_r = y_r + jnp.exp(col) * _dot_nt(cg, sp.astype(bf16))
                y_r = y_r + xr * dk[:, r:r + 1]
                cs = _dot_tn((xd * jnp.exp(alast - col)).astype(bf16), bg)
                state[g * HPG + r] = jnp.exp(alast) * sp + cs
                y_s[g, :, r * HD:(r + 1) * HD] = y_r
            return carry

        lax.fori_loop(0, NG, group, 0)
        for g in range(NG):
            cols = slice(g * GW, (g + 1) * GW)
            y = y_s[g]
            szv = sz_ref[:, cols]
            yz = y * (szv * _sigmoid(szv))
            rr = lax.rsqrt(jnp.mean(yz * yz, axis=-1, keepdims=True) + EPS)
            yb_ref[:, cols] = (yz * rr * snw_ref[g]).astype(bf16)
            y_ref[:, cols] = y.astype(bf16)

    const3 = lambda c: (0, 0, 0)
    return pl.pallas_call(
        body, grid=(nc,),
        in_specs=[pl.BlockSpec((CH, CONVD), lambda c: (c, 0)), pl.BlockSpec((CH, DIN), lambda c: (c, CONVD // DIN)),
                  pl.BlockSpec((CH, 256), lambda c: (c, (CONVD + DIN) // 256)),
                  pl.BlockSpec((4, CONVD), lambda c: (0, 0)), pl.BlockSpec((1, CONVD), lambda c: (0, 0)),
                  pl.BlockSpec((NG, CH, CH), const3), pl.BlockSpec((NG, 1, CH), const3), pl.BlockSpec((NG, 1, CH), const3),
                  pl.BlockSpec((NG, 1, CH), const3), pl.BlockSpec((NG, 1, GW), const3)],
        out_specs=[pl.BlockSpec((CH, DIN), lambda c: (c, 0)), pl.BlockSpec((CH, DIN), lambda c: (c, 0)),
                   pl.BlockSpec((1, NH, HD, NST), lambda c: (c, 0, 0, 0))],
        out_shape=[jax.ShapeDtypeStruct((s, DIN), bf16), jax.ShapeDtypeStruct((s, DIN), bf16),
                   jax.ShapeDtypeStruct((nc, NH, HD, NST), bf16)],
        scratch_shapes=[pltpu.VMEM((8, CONVD), f32), pltpu.VMEM((NH, HD, NST), f32), pltpu.VMEM((NG, CH, GW), f32),
                        pltpu.VMEM((NG, CH, NST), bf16), pltpu.VMEM((NG, CH, NST), bf16), pltpu.VMEM((NG, CH, GW), f32),
                        pltpu.VMEM((NG, CH, GW), f32)],
        name=name, compiler_params=_cp(("arbitrary",)),
    )(proj_b, proj_b, proj_b, cw, cb, sel, dtb_g, a_g, dk_g, snw_g)


def _branch_b_bwd(proj_b, dyb, y_sv, states, cw, cb, sel, dtb_g, a_g, dk_g, snw_g, name):
    s = proj_b.shape[0]
    nc = s // CH

    def body(xbc_ref, xprev_ref, sz_ref, dtr_ref, dyb_ref, y_ref, st_ref, cw_ref, cb_ref, sel_ref, dtb_ref, a_ref, dk_ref,
             snw_ref, dp_ref, dcw_ref, dcb_ref, ddtb_ref, dal_ref, ddk_ref, dsnw_ref,
             dstate, dnext8, xs_s, b_s, c_s, dy_s, db_s, dc_s, ddt32):
        step = pl.program_id(0)
        cc = nc - 1 - step

        @pl.when(step == 0)
        def _():
            dstate[...] = jnp.zeros_like(dstate)
            dnext8[...] = jnp.zeros_like(dnext8)
            dcw_ref[...] = jnp.zeros_like(dcw_ref)
            dcb_ref[...] = jnp.zeros_like(dcb_ref)
            ddtb_ref[...] = jnp.zeros_like(ddtb_ref)
            dal_ref[...] = jnp.zeros_like(dal_ref)
            ddk_ref[...] = jnp.zeros_like(ddk_ref)
            dsnw_ref[...] = jnp.zeros_like(dsnw_ref)

        x = xbc_ref[...]
        p8 = xprev_ref[...] * jnp.where(cc > 0, 1.0, 0.0)
        conv, shifted = _conv_pre(x, p8, cw_ref, cb_ref)
        sg = _sigmoid(conv)
        xc = conv * sg
        for g in range(NG):
            xs_s[g] = xc[:, g * GW:(g + 1) * GW]
            b_s[g] = xc[:, DIN + g * NST:DIN + (g + 1) * NST].astype(bf16)
            c_s[g] = xc[:, DIN + NG * NST + g * NST:DIN + NG * NST + (g + 1) * NST].astype(bf16)
        ddt32[...] = jnp.zeros_like(ddt32)
        raw = dtr_ref[:, 0:CH]
        mask = _tril_mask()
        tri = mask.astype(f32)
        tri_t = (lax.broadcasted_iota(jnp.int32, (CH, CH), 0) <= lax.broadcasted_iota(jnp.int32, (CH, CH), 1)).astype(f32)
        lane = lax.broadcasted_iota(jnp.int32, (CH, CH), 1)
        sub = lax.broadcasted_iota(jnp.int32, (CH, CH), 0)
        lane1 = lax.broadcasted_iota(jnp.int32, (1, CH), 1)
        is_last = lax.broadcasted_iota(jnp.int32, (CH, 1), 0) == CH - 1

        for g in range(NG):
            cols = slice(g * GW, (g + 1) * GW)
            y = y_ref[:, cols].astype(f32)
            silu_sz, dsilu_sz = _silu_and_grad(sz_ref[:, cols])
            yz = y * silu_sz
            rr = lax.rsqrt(jnp.mean(yz * yz, axis=-1, keepdims=True) + EPS)
            dyb_g = dyb_ref[:, cols].astype(f32)
            w = dyb_g * snw_ref[g]
            dsnw_ref[g] += jnp.sum(dyb_g * yz * rr, axis=0, keepdims=True)
            dyz = rr * w - yz * (rr * rr * rr) * jnp.mean(w * yz, axis=-1, keepdims=True)
            dp_ref[:, CONVD + g * GW:CONVD + (g + 1) * GW] = (dyz * y * dsilu_sz).astype(bf16)
            dy_s[g] = dyz * silu_sz

        def group(g, carry):
            z = _dot_hi(raw, sel_ref[g]) + dtb_ref[g]
            dt = _softplus(z)
            a_vec = a_ref[g]
            adt = dt * a_vec
            acs = _dot_hi(tri, adt)
            acs_t = acs.T
            bg = b_s[g]
            cg = c_s[g]
            cbm = _dot_nt(cg, bg)
            xs = xs_s[g]
            dk = dk_ref[g]
            dy_g = dy_s[g]
            d_cb = jnp.zeros((CH, CH), f32)
            d_bg = jnp.zeros((CH, NST), f32)
            d_cg = jnp.zeros((CH, NST), f32)
            dacs_col = jnp.zeros((CH, CH), f32)
            dacs_row = jnp.zeros((CH, CH), f32)
            ddt_x = jnp.zeros((CH, CH), f32)
            ddk = jnp.zeros((1, CH), f32)
            dxs_parts = []
            for r in range(HPG):
                h = g * HPG + r
                col = acs[:, r:r + 1]
                row = acs_t[r:r + 1, :]
                lmat = jnp.exp(jnp.where(mask, col - row, -jnp.inf))
                mmat = cbm * lmat
                dtc = dt[:, r:r + 1]
                xr = xs[:, r * HD:(r + 1) * HD]
                xd = xr * dtc
                ea = jnp.exp(col)
                alast = acs[CH - 1:CH, r:r + 1]
                dte = jnp.exp(alast - col)
                cd = jnp.exp(alast)
                sp = st_ref[0, h]
                dsn = dstate[h]
                dsn_b = dsn.astype(bf16)
                dyr = dy_g[:, r * HD:(r + 1) * HD]
                dyr_b = dyr.astype(bf16)
                dye_b = (dyr * ea).astype(bf16)
                d_cg = d_cg + _dot(dye_b, sp)
                ds_off = _dot_tn(dye_b, cg)
                cst = _dot_nt(cg, sp)
                dacs_l = jnp.sum(dyr * ea * cst, axis=1, keepdims=True)
                dcd = jnp.sum(jnp.sum(dsn * sp.astype(f32), axis=1, keepdims=True), axis=0, keepdims=True)
                dxde = _dot_nt(bg, dsn_b)
                ddte = jnp.sum(dxde * xd, axis=1, keepdims=True)
                d_bg = d_bg + _dot((xd * dte).astype(bf16), dsn_b)
                xd_b = xd.astype(bf16)
                dm = _dot_nt(dyr_b, xd_b)
                dxd = dxde * dte + _dot_tn(mmat.astype(bf16), dyr_b)
                d_cb = d_cb + dm * lmat
                gm = dm * mmat
                rs = jnp.sum(gm, axis=1, keepdims=True)
                cs = jnp.sum(gm, axis=0, keepdims=True)
                dd = ddte * dte
                last_add = dcd * cd + jnp.sum(dd, axis=0, keepdims=True)
                colvec = dacs_l + rs - dd + jnp.where(is_last, last_add, 0.0)
                dacs_col = jnp.where(lane == r, colvec, dacs_col)
                dacs_row = jnp.where(sub == r, -cs, dacs_row)
                dstate[h] = cd * dsn + ds_off
                dxs_parts.append(dxd * dtc + dyr * dk[:, r:r + 1])
                ddt_x = jnp.where(lane == r, jnp.sum(dxd * xr, axis=1, keepdims=True), ddt_x)
                ddk = jnp.where(lane1 == r, jnp.sum(jnp.sum(dyr * xr, axis=1, keepdims=True), axis=0, keepdims=True), ddk)
            d_cb_b = d_cb.astype(bf16)
            d_cg = d_cg + _dot(d_cb_b, bg)
            d_bg = d_bg + _dot_tn(d_cb_b, cg)
            dacs = dacs_col + dacs_row.T
            dadt = _dot_hi(tri_t, dacs)
            ddt = dadt * a_vec + ddt_x
            dal_ref[g] += jnp.sum(dadt * adt, axis=0, keepdims=True)
            ddz = ddt * _sigmoid(z)
            ddtb_ref[g] += jnp.sum(ddz, axis=0, keepdims=True)
            ddk_ref[g] += ddk
            ddt32[...] += lax.dot_general(ddz, sel_ref[g], (((1,), (1,)), ((), ())), precision=_HI, preferred_element_type=f32)
            dy_s[g] = jnp.concatenate(dxs_parts, axis=1)
            db_s[g] = d_bg
            dc_s[g] = d_cg
            return carry

        lax.fori_loop(0, NG, group, 0)

        dxc = jnp.concatenate([dy_s[g] for g in range(NG)] + [db_s[g] for g in range(NG)] + [dc_s[g] for g in range(NG)], axis=1)
        dconv = dxc * (sg * (1.0 + conv * (1.0 - sg)))
        dcb_ref[...] += jnp.sum(dconv, axis=0, keepdims=True)
        dcw_ref[3:4, :] += jnp.sum(dconv * x, axis=0, keepdims=True)
        for j in (1, 2, 3):
            dcw_ref[3 - j:4 - j, :] += jnp.sum(dconv * shifted[j - 1], axis=0, keepdims=True)
        dx = cw_ref[3:4, :] * dconv
        nxt = dnext8[...]
        rid = lax.broadcasted_iota(jnp.int32, (8, CONVD), 0)
        for j in (1, 2, 3):
            dr = pltpu.roll(dconv, CH - j, 0)
            fix = pltpu.roll(nxt, 8 - j, 0)
            bot = jnp.where(rid >= 8 - j, fix, dr[CH - 8:CH])
            dx = dx + cw_ref[3 - j:4 - j, :] * jnp.concatenate([dr[0:CH - 8], bot], axis=0)
        dnext8[...] = dconv[0:8]
        dp_ref[:, 0:CONVD] = dx.astype(bf16)
        dp_ref[:, CONVD + DIN:CONVD + DIN + CH] = ddt32[...].astype(bf16)
        dp_ref[:, CONVD + DIN + CH:PB] = jnp.zeros((CH, PB - CONVD - DIN - CH), bf16)

    const3 = lambda c: (0, 0, 0)
    const2 = lambda c: (0, 0)
    rev = lambda c: (nc - 1 - c, 0)
    return pl.pallas_call(
        body, grid=(nc,),
        in_specs=[pl.BlockSpec((CH, CONVD), rev),
                  pl.BlockSpec((8, CONVD), lambda c: (jnp.maximum((nc - 1 - c) * (CH // 8) - 1, 0), 0)),
                  pl.BlockSpec((CH, DIN), lambda c: (nc - 1 - c, CONVD // DIN)),
                  pl.BlockSpec((CH, 256), lambda c: (nc - 1 - c, (CONVD + DIN) // 256)),
                  pl.BlockSpec((CH, DIN), rev), pl.BlockSpec((CH, DIN), rev),
                  pl.BlockSpec((1, NH, HD, NST), lambda c: (nc - 1 - c, 0, 0, 0)),
                  pl.BlockSpec((4, CONVD), const2), pl.BlockSpec((1, CONVD), const2),
                  pl.BlockSpec((NG, CH, CH), const3), pl.BlockSpec((NG, 1, CH), const3), pl.BlockSpec((NG, 1, CH), const3),
                  pl.BlockSpec((NG, 1, CH), const3), pl.BlockSpec((NG, 1, GW), const3)],
        out_specs=[pl.BlockSpec((CH, PB), rev), pl.BlockSpec((4, CONVD), const2), pl.BlockSpec((1, CONVD), const2),
                   pl.BlockSpec((NG, 1, CH), const3), pl.BlockSpec((NG, 1, CH), const3), pl.BlockSpec((NG, 1, CH), const3),
                   pl.BlockSpec((NG, 1, GW), const3)],
        out_shape=[jax.ShapeDtypeStruct((s, PB), bf16), jax.ShapeDtypeStruct((4, CONVD), f32),
                   jax.ShapeDtypeStruct((1, CONVD), f32), jax.ShapeDtypeStruct((NG, 1, CH), f32),
                   jax.ShapeDtypeStruct((NG, 1, CH), f32), jax.ShapeDtypeStruct((NG, 1, CH), f32),
                   jax.ShapeDtypeStruct((NG, 1, GW), f32)],
        scratch_shapes=[pltpu.VMEM((NH, HD, NST), f32), pltpu.VMEM((8, CONVD), f32), pltpu.VMEM((NG, CH, GW), f32),
                        pltpu.VMEM((NG, CH, NST), bf16), pltpu.VMEM((NG, CH, NST), bf16), pltpu.VMEM((NG, CH, GW), f32),
                        pltpu.VMEM((NG, CH, NST), f32), pltpu.VMEM((NG, CH, NST), f32), pltpu.VMEM((CH, CH), f32)],
        name=name, compiler_params=_cp(("arbitrary",)),
    )(proj_b, proj_b, proj_b, proj_b, dyb, y_sv, states, cw, cb, sel, dtb_g, a_g, dk_g, snw_g)


def _merge_fwd(ya, yb, proj_g, x, gate, wa, wb, wo, name):
    s = x.shape[0]
    ts = min(512, s)

    def body(ya_ref, yb_ref, ga_ref, gb_ref, x_ref, gate_ref, wa_ref, wb_ref, wo_ref, xo_ref, pa_ref, pb_ref, mg_ref, o_ref):
        pa = _dot(ya_ref[...], wa_ref[...])
        pb = _dot(yb_ref[...], wb_ref[...])
        mg = (_sigmoid(ga_ref[...]) * pa + _sigmoid(gb_ref[...]) * pb).astype(bf16)
        o = _dot(mg, wo_ref[...])
        xo_ref[...] = x_ref[...] + gate_ref[...] * o
        pa_ref[...] = pa.astype(bf16)
        pb_ref[...] = pb.astype(bf16)
        mg_ref[...] = mg
        o_ref[...] = o.astype(bf16)

    tile = pl.BlockSpec((ts, D), lambda i: (i, 0))
    const = lambda i: (0, 0)
    act = jax.ShapeDtypeStruct((s, D), bf16)
    return pl.pallas_call(
        body, grid=(s // ts,),
        in_specs=[tile, pl.BlockSpec((ts, DIN), lambda i: (i, 0)), tile, pl.BlockSpec((ts, D), lambda i: (i, 1)), tile,
                  pl.BlockSpec((1, D), const), pl.BlockSpec((D, D), const), pl.BlockSpec((DIN, D), const),
                  pl.BlockSpec((D, D), const)],
        out_specs=[tile, tile, tile, tile, tile],
        out_shape=[jax.ShapeDtypeStruct((s, D), f32), act, act, act, act],
        name=name, compiler_params=_cp(("parallel",)),
    )(ya, yb, proj_g, proj_g, x, gate, wa, wb, wo)


def _merge_bwd(dxo, gate, o_sv, pa_sv, pb_sv, proj_g, wo_t, wa_t, wb_t, name):
    s = dxo.shape[0]
    ts = min(512, s)

    def body(dxo_ref, gate_ref, o_ref, pa_ref, pb_ref, ga_ref, gb_ref, wot_ref, wat_ref, wbt_ref,
             do_ref, dpa_ref, dpb_ref, dg_ref, dya_ref, dyb_ref, dgate_ref):
        @pl.when(pl.program_id(0) == 0)
        def _():
            dgate_ref[...] = jnp.zeros_like(dgate_ref)
        dxo_v = dxo_ref[...]
        dgate_ref[...] += jnp.sum(dxo_v * o_ref[...].astype(f32), axis=0, keepdims=True)
        do = (dxo_v * gate_ref[...]).astype(bf16)
        do_ref[...] = do
        dmg = _dot(do, wot_ref[...])
        sa = _sigmoid(ga_ref[...])
        sb = _sigmoid(gb_ref[...])
        dpa = (dmg * sa).astype(bf16)
        dpb = (dmg * sb).astype(bf16)
        dpa_ref[...] = dpa
        dpb_ref[...] = dpb
        dg_ref[:, 0:D] = (dmg * pa_ref[...].astype(f32) * sa * (1.0 - sa)).astype(bf16)
        dg_ref[:, D:2 * D] = (dmg * pb_ref[...].astype(f32) * sb * (1.0 - sb)).astype(bf16)
        dya_ref[...] = _dot(dpa, wat_ref[...]).astype(bf16)
        dyb_ref[...] = _dot(dpb, wbt_ref[...]).astype(bf16)

    tile = pl.BlockSpec((ts, D), lambda i: (i, 0))
    const = lambda i: (0, 0)
    act = jax.ShapeDtypeStruct((s, D), bf16)
    return pl.pallas_call(
        body, grid=(s // ts,),
        in_specs=[tile, pl.BlockSpec((1, D), const), tile, tile, tile, tile, pl.BlockSpec((ts, D), lambda i: (i, 1)),
                  pl.BlockSpec((D, D), const), pl.BlockSpec((D, D), const), pl.BlockSpec((D, DIN), const)],
        out_specs=[tile, tile, tile, pl.BlockSpec((ts, PG), lambda i: (i, 0)), tile, pl.BlockSpec((ts, DIN), lambda i: (i, 0)),
                   pl.BlockSpec((1, D), const)],
        out_shape=[act, act, act, jax.ShapeDtypeStruct((s, PG), bf16), act, jax.ShapeDtypeStruct((s, DIN), bf16),
                   jax.ShapeDtypeStruct((1, D), f32)],
        name=name, compiler_params=_cp(("arbitrary",)),
    )(dxo, gate, o_sv, pa_sv, pb_sv, proj_g, proj_g, wo_t, wa_t, wb_t)


def _final_loss(x, target, fnw, name):
    s = x.shape[0]
    ts = min(512, s)

    def body(x_ref, t_ref, w_ref, loss_ref, dx_ref, dw_ref):
        @pl.when(pl.program_id(0) == 0)
        def _():
            loss_ref[...] = jnp.zeros_like(loss_ref)
            dw_ref[...] = jnp.zeros_like(dw_ref)
        xv = x_ref[...]
        r = lax.rsqrt(jnp.mean(xv * xv, axis=-1, keepdims=True) + EPS)
        xn = xv * r
        err = xn * w_ref[...] - t_ref[...]
        part = jnp.sum(err * err, axis=0, keepdims=True)
        acc = part[:, 0:128]
        for k in range(1, D // 128):
            acc = acc + part[:, k * 128:(k + 1) * 128]
        loss_ref[0:1, :] += acc * (0.5 / D)
        dy = err * (1.0 / D)
        dw_ref[...] += jnp.sum(dy * xn, axis=0, keepdims=True)
        dxn = dy * w_ref[...]
        dx_ref[...] = r * (dxn - xn * jnp.mean(dxn * xn, axis=-1, keepdims=True))

    tile = pl.BlockSpec((ts, D), lambda i: (i, 0))
    row = pl.BlockSpec((1, D), lambda i: (0, 0))
    return pl.pallas_call(
        body, grid=(s // ts,), in_specs=[tile, tile, row],
        out_specs=[pl.BlockSpec((8, 128), lambda i: (0, 0)), tile, row],
        out_shape=[jax.ShapeDtypeStruct((8, 128), f32), jax.ShapeDtypeStruct((s, D), f32), jax.ShapeDtypeStruct((1, D), f32)],
        name=name, compiler_params=_cp(("arbitrary",)),
    )(x, target, fnw)


def _layer_operands(w_in, conv_w, wa, wb, wo, norm_w, gm_ln_w, gm_ln_b, gm_ws, gm_bs, conv_b, dt_bias, a_log, d_skip, ssm_norm_w):
    w_b = jnp.concatenate([w_in[:, 5120:9216], w_in[:, 3072:5120], w_in[:, 9216:9248], jnp.zeros((D, 224), bf16)], axis=1)
    w_a = w_in[:, 0:3072]
    w_g = w_in[:, 9248:N_IN]
    tril = jnp.tril(jnp.ones((CH, CH), bool))
    wsm = jnp.where(tril[None], gm_ws, 0.0).astype(bf16)
    j = jnp.arange(CH)[None, :, None]
    r = jnp.arange(CH)[None, None, :]
    g = jnp.arange(NG)[:, None, None]
    sel = ((j == HPG * g + r) & (r < HPG)).astype(f32)

    def per_group(v):
        return jnp.pad(v.reshape(NG, 1, HPG), ((0, 0), (0, 0), (0, CH - HPG)))

    return dict(
        w_b=w_b, w_a=w_a, w_g=w_g, w_b_t=w_b.T, w_a_t=w_a.T, w_g_t=w_g.T,
        wa=wa, wb=wb, wo=wo, wa_t=wa.T, wb_t=wb.T, wo_t=wo.T,
        norm_w=norm_w.reshape(1, D), lnw=gm_ln_w.reshape(1, D), lnb=gm_ln_b.reshape(1, D),
        wsm=wsm, wsm_t=jnp.swapaxes(wsm, 1, 2), bsf=jnp.repeat(gm_bs.T, CH, axis=1),
        cw=conv_w, cb=conv_b.reshape(1, CONVD), sel=sel, dtb_g=per_group(dt_bias), a_g=per_group(-jnp.exp(a_log)),
        dk_g=per_group(d_skip), snw_g=ssm_norm_w.reshape(NG, 1, GW),
    )


def _layer_fwd(x, shift, scale, gate, p, tag):
    h = _rmsmod_fwd(x, p["norm_w"], scale, shift, f"rmsmod_fwd{tag}")
    proj_b = _mm(h, p["w_b"], f32, f"proj_b{tag}")
    proj_a = _mm(h, p["w_a"], f32, f"proj_a{tag}")
    proj_g = _mm(h, p["w_g"], f32, f"proj_g{tag}")
    ya = _branch_a_fwd(proj_a, p["lnw"], p["lnb"], p["wsm"], p["bsf"], f"branch_a_fwd{tag}")
    yb, y_sv, states = _branch_b_fwd(proj_b, p["cw"], p["cb"], p["sel"], p["dtb_g"], p["a_g"], p["dk_g"], p["snw_g"],
                                     f"branch_b_fwd{tag}")
    x_out, pa, pb, mg, o = _merge_fwd(ya, yb, proj_g, x, gate, p["wa"], p["wb"], p["wo"], f"merge_fwd{tag}")
    saved = dict(x=x, h=h, proj_b=proj_b, proj_a=proj_a, proj_g=proj_g, ya=ya, yb=yb, y=y_sv, states=states,
                 pa=pa, pb=pb, mg=mg, o=o, scale=scale, gate=gate)
    return x_out, saved


def _layer_bwd(dxo, sv, p, tag):
    do, dpa, dpb, dg, dya, dyb, dgate = _merge_bwd(dxo, sv["gate"], sv["o"], sv["pa"], sv["pb"], sv["proj_g"],
                                                   p["wo_t"], p["wa_t"], p["wb_t"], f"merge_bwd{tag}")
    d_wo = _mm_tn(sv["mg"], do, f"d_wo{tag}")
    d_wa = _mm_tn(sv["ya"], dpa, f"d_wa{tag}")
    d_wb = _mm_tn(sv["yb"], dpb, f"d_wb{tag}")
    da, dws, dbs, dlnw, dlnb = _branch_a_bwd(sv["proj_a"], dya, p["lnw"], p["lnb"], p["wsm"], p["wsm_t"], p["bsf"],
                                             f"branch_a_bwd{tag}")
    db, dcw, dcb, ddtb, dal, ddk, dsnw = _branch_b_bwd(sv["proj_b"], dyb, sv["y"], sv["states"], p["cw"], p["cb"], p["sel"],
                                                       p["dtb_g"], p["a_g"], p["dk_g"], p["snw_g"], f"branch_b_bwd{tag}")
    dh = _mm(db, p["w_b_t"], f32, f"dh_b{tag}")
    dh = _mm(da, p["w_a_t"], f32, f"dh_a{tag}", c_in=dh)
    dh = _mm(dg, p["w_g_t"], f32, f"dh_g{tag}", c_in=dh)
    d_w_b = _mm_tn(sv["h"], db, f"d_w_b{tag}")
    d_w_a = _mm_tn(sv["h"], da, f"d_w_a{tag}")
    d_w_g = _mm_tn(sv["h"], dg, f"d_w_g{tag}")
    dx, dscale, dshift, dnw = _rmsmod_bwd(dh, sv["x"], dxo, p["norm_w"], sv["scale"], f"rmsmod_bwd{tag}")
    d_w_in = jnp.concatenate([d_w_a, d_w_b[:, CONVD:CONVD + DIN], d_w_b[:, 0:CONVD], d_w_b[:, CONVD + DIN:CONVD + DIN + NH], d_w_g],
                             axis=1)
    tril = jnp.tril(jnp.ones((CH, CH), bool))
    heads = lambda v: v[:, 0, 0:HPG].reshape(NH)
    grads = dict(
        w_in=d_w_in, w_proj_a=d_wa, w_proj_b=d_wb, w_out=d_wo, conv_w=dcw, conv_b=dcb.reshape(CONVD),
        norm_w=dnw.reshape(D), gm_ln_w=dlnw.reshape(D), gm_ln_b=dlnb.reshape(D),
        gm_ws=jnp.where(tril[None], dws, 0.0), gm_bs=dbs.reshape(CH, NG, CH).sum(-1).T,
        dt_bias=heads(ddtb), a_log=heads(dal), d_skip=heads(ddk), ssm_norm_w=dsnw.reshape(DIN),
        mod=jnp.concatenate([dshift, dscale, dgate], axis=1).reshape(3 * D),
    )
    return dx, grads


def _local_step(x, target, mods, layer_ops, fnw):
    saved = []
    for l in range(DEPTH):
        shift, scale, gate = mods[l]
        x, sv = _layer_fwd(x, shift, scale, gate, layer_ops[l], f"_l{l}")
        saved.append(sv)
    loss_parts, dx, dfnw = _final_loss(x, target, fnw.reshape(1, D), "final_loss")
    grads = [None] * DEPTH
    for l in reversed(range(DEPTH)):
        dx, grads[l] = _layer_bwd(dx, saved[l], layer_ops[l], f"_l{l}")
    return jnp.sum(loss_parts), dx, grads, dfnw.reshape(D)


ADA_COLS = 3 * D // NSHARD


def _ada_fwd(c_all, ada_w, ada_b_cols, name):
    def body(c_ref, w_ref, b_ref, o_ref):
        cv = c_ref[...]
        sc = cv * _sigmoid(cv)
        for l in range(DEPTH):
            o_ref[l] = _dot_hi(sc, w_ref[l]) + b_ref[l]

    return pl.pallas_call(body, out_shape=jax.ShapeDtypeStruct((DEPTH, 8, ADA_COLS), f32), name=name,
                          compiler_params=_cp(None))(c_all, ada_w, ada_b_cols)


def _adam_math(w, g, m, v):
    m = ADAM_B1 * m + (1.0 - ADAM_B1) * g
    v = ADAM_B2 * v + (1.0 - ADAM_B2) * (g * g)
    m_hat = m / (1.0 - ADAM_B1 ** ADAM_STEP)
    v_hat = v / (1.0 - ADAM_B2 ** ADAM_STEP)
    delta = -ADAM_LR * (m_hat / (jnp.sqrt(v_hat) + ADAM_EPS) + ADAM_WD * w)
    return delta, m, v


def _ada_bwd_adamw(c_all, dmod_cols, w, m, v, name):
    tr = 256

    def body(c_ref, dm_ref, w_ref, m_ref, v_ref, g_ref, d_ref, nm_ref, nv_ref):
        cv = c_ref[...]
        sc = cv * _sigmoid(cv)
        g = lax.dot_general(sc, dm_ref[0], (((0,), (0,)), ((), ())), precision=_HI, preferred_element_type=f32)
        g_ref[0] = g
        d_ref[0], nm_ref[0], nv_ref[0] = _adam_math(w_ref[0], g, m_ref[0], v_ref[0])

    blk = pl.BlockSpec((1, tr, ADA_COLS), lambda l, i: (l, i, 0))
    shp = jax.ShapeDtypeStruct((DEPTH, D, ADA_COLS), f32)
    return pl.pallas_call(
        body, grid=(DEPTH, D // tr),
        in_specs=[pl.BlockSpec((8, tr), lambda l, i: (0, i)), pl.BlockSpec((1, 8, ADA_COLS), lambda l, i: (l, 0, 0)), blk, blk, blk],
        out_specs=[blk, blk, blk, blk], out_shape=[shp, shp, shp, shp], name=name, compiler_params=_cp(("parallel", "parallel")),
    )(c_all, dmod_cols, w, m, v)


def _adamw(w, g, m, v, name):
    r, c = w.shape
    tr = _pick(r, (256, 128, 64, 32, 16, 8))

    def body(w_ref, g_ref, m_ref, v_ref, d_ref, nm_ref, nv_ref):
        d_ref[...], nm_ref[...], nv_ref[...] = _adam_math(w_ref[...], g_ref[...], m_ref[...], v_ref[...])

    blk = pl.BlockSpec((tr, c), lambda i: (i, 0))
    shp = jax.ShapeDtypeStruct((r, c), f32)
    return pl.pallas_call(body, grid=(r // tr,), in_specs=[blk] * 4, out_specs=[blk] * 3, out_shape=[shp] * 3, name=name,
                          compiler_params=_cp(("parallel",)))(w, g, m, v)


_MESH = pl.DeviceIdType.MESH
_AXES = ("x", "y", "c")
_HBM = pl.BlockSpec(memory_space=pltpu.HBM)


def _my_place():
    return tuple(lax.axis_index(a) for a in _AXES)


def _allreduce(buf, axes, name):
    r = buf.shape[0]
    n = len(axes)

    def body(x_ref, o_ref, rbuf, ssem, rsem):
        me = dict(zip(_AXES, _my_place()))
        o_ref[...] = x_ref[...]
        for k, ax in enumerate(axes):
            peer = tuple(1 - me[a] if a == ax else me[a] for a in _AXES)
            cp = pltpu.make_async_remote_copy(src_ref=o_ref, dst_ref=rbuf.at[k], send_sem=ssem.at[k], recv_sem=rsem.at[k],
                                              device_id=peer, device_id_type=_MESH)
            cp.start()
            cp.wait()
            o_ref[...] = o_ref[...] + rbuf[k]

    vm = pl.BlockSpec(memory_space=pltpu.VMEM)
    return pl.pallas_call(
        body, out_shape=jax.ShapeDtypeStruct((r, 128), f32), in_specs=[vm], out_specs=vm,
        scratch_shapes=[pltpu.VMEM((n, r, 128), f32), pltpu.SemaphoreType.DMA((n,)), pltpu.SemaphoreType.DMA((n,))],
        name=name, compiler_params=pltpu.CompilerParams(vmem_limit_bytes=VMEM_LIMIT),
    )(buf)


def _other_chips(x, y):
    return [(1 - x, y), (x, 1 - y), (1 - x, 1 - y)]


def _gather_weights(shards, name):
    na = len(shards)

    def body(*refs):
        ins, outs = refs[:na], refs[na:2 * na]
        ssem, rsem, lsem = refs[2 * na:]
        x, y, c = _my_place()
        k_me = 2 * x + y
        sibling = (x, y, 1 - c)
        chips = _other_chips(x, y)

        def rcopy(a, src, k, layer, to, idx):
            return pltpu.make_async_remote_copy(src_ref=src, dst_ref=outs[a].at[k, layer], send_sem=ssem.at[idx],
                                                recv_sem=rsem.at[idx], device_id=to, device_id_type=_MESH)

        local = [pltpu.make_async_copy(ins[a], outs[a].at[k_me], lsem.at[a]) for a in range(na)]
        for cp in local:
            cp.start()
        sent = []
        for j, chip in enumerate(chips):
            for a in range(na):
                cp = rcopy(a, ins[a].at[c], k_me, c, (*chip, c), j * na + a)
                cp.start()
                sent.append(cp)
        for j, chip in enumerate(chips):
            kj = 2 * chip[0] + chip[1]
            for a in range(na):
                rcopy(a, ins[a].at[c], kj, c, (*chip, c), j * na + a).wait_recv()
                cp = rcopy(a, outs[a].at[kj, c], kj, c, sibling, (3 + j) * na + a)
                cp.start()
                sent.append(cp)
        for j, chip in enumerate(chips):
            kj = 2 * chip[0] + chip[1]
            for a in range(na):
                rcopy(a, ins[a].at[c], kj, 1 - c, sibling, (3 + j) * na + a).wait_recv()
        for cp in sent:
            cp.wait_send()
        for cp in local:
            cp.wait()

    out_shape = [jax.ShapeDtypeStruct((NSHARD,) + s.shape, s.dtype) for s in shards]
    return pl.pallas_call(
        body, out_shape=out_shape, in_specs=[_HBM] * na, out_specs=[_HBM] * na,
        scratch_shapes=[pltpu.SemaphoreType.DMA((6 * na,)), pltpu.SemaphoreType.DMA((6 * na,)), pltpu.SemaphoreType.DMA((na,))],
        name=name,
    )(*shards)


def _swap_layers(parts, name):
    na = len(parts)

    def body(*refs):
        ins, outs = refs[:na], refs[na:2 * na]
        ssem, rsem = refs[2 * na:]
        x, y, c = _my_place()
        cps = [pltpu.make_async_remote_copy(src_ref=ins[a].at[1 - c], dst_ref=outs[a], send_sem=ssem.at[a], recv_sem=rsem.at[a],
                                            device_id=(x, y, 1 - c), device_id_type=_MESH) for a in range(na)]
        for cp in cps:
            cp.start()
        for cp in cps:
            cp.wait()

    return pl.pallas_call(
        body, out_shape=[jax.ShapeDtypeStruct(p.shape[1:], p.dtype) for p in parts], in_specs=[_HBM] * na, out_specs=[_HBM] * na,
        scratch_shapes=[pltpu.SemaphoreType.DMA((na,)), pltpu.SemaphoreType.DMA((na,))], name=name,
    )(*parts)


def _scatter_shards(sums, name):
    na = len(sums)

    def body(*refs):
        ins, outs = refs[:na], refs[na:2 * na]
        ssem, rsem = refs[2 * na:]
        x, y, c = _my_place()
        cps = []
        for j, chip in enumerate(_other_chips(x, y)):
            kj = 2 * chip[0] + chip[1]
            for a in range(na):
                cps.append(pltpu.make_async_remote_copy(
                    src_ref=ins[a].at[kj], dst_ref=outs[a].at[j], send_sem=ssem.at[j * na + a], recv_sem=rsem.at[j * na + a],
                    device_id=(*chip, c), device_id_type=_MESH))
        for cp in cps:
            cp.start()
        for cp in cps:
            cp.wait()

    return pl.pallas_call(
        body, out_shape=[jax.ShapeDtypeStruct((3,) + p.shape[1:], p.dtype) for p in sums], in_specs=[_HBM] * na,
        out_specs=[_HBM] * na, scratch_shapes=[pltpu.SemaphoreType.DMA((3 * na,)), pltpu.SemaphoreType.DMA((3 * na,))], name=name,
    )(*sums)


def _share_layers(finals, name):
    na = len(finals)

    def body(*refs):
        ins, outs = refs[:na], refs[na:2 * na]
        ssem, rsem, lsem = refs[2 * na:]
        x, y, c = _my_place()
        local = [pltpu.make_async_copy(ins[a], outs[a].at[c], lsem.at[a]) for a in range(na)]
        cps = [pltpu.make_async_remote_copy(src_ref=ins[a], dst_ref=outs[a].at[c], send_sem=ssem.at[a], recv_sem=rsem.at[a],
                                            device_id=(x, y, 1 - c), device_id_type=_MESH) for a in range(na)]
        for cp in local + cps:
            cp.start()
        for cp in cps:
            cp.wait()
        for cp in local:
            cp.wait()

    return pl.pallas_call(
        body, out_shape=[jax.ShapeDtypeStruct((DEPTH,) + p.shape, p.dtype) for p in finals], in_specs=[_HBM] * na,
        out_specs=[_HBM] * na,
        scratch_shapes=[pltpu.SemaphoreType.DMA((na,)), pltpu.SemaphoreType.DMA((na,)), pltpu.SemaphoreType.DMA((na,))], name=name,
    )(*finals)


def _add_own_layer(parts, recv, c_idx, name):
    _, ns, r, c = parts.shape
    tr = _pick(r, (256, 128, 64, 32, 16, 8, 4))

    def body(ci_ref, p_ref, r_ref, o_ref):
        o_ref[...] = p_ref[0] + r_ref[...]

    blk = pl.BlockSpec((1, tr, c), lambda k, i, ci: (k, i, 0))
    return pl.pallas_call(
        body,
        grid_spec=pltpu.PrefetchScalarGridSpec(
            num_scalar_prefetch=1, grid=(ns, r // tr),
            in_specs=[pl.BlockSpec((1, 1, tr, c), lambda k, i, ci: (ci[0], k, i, 0)), blk], out_specs=blk),
        out_shape=jax.ShapeDtypeStruct((ns, r, c), f32), name=name, compiler_params=_cp(("parallel", "parallel")),
    )(c_idx, parts, recv)


def _add_own_shard(sums, recv, k_idx, name):
    _, r, c = sums.shape
    tr = _pick(r, (256, 128, 64, 32, 16, 8, 4))

    def body(ki_ref, s_ref, r_ref, o_ref):
        o_ref[...] = ((s_ref[0] + r_ref[0]) + r_ref[1]) + r_ref[2]

    return pl.pallas_call(
        body,
        grid_spec=pltpu.PrefetchScalarGridSpec(
            num_scalar_prefetch=1, grid=(r // tr,),
            in_specs=[pl.BlockSpec((1, tr, c), lambda i, ki: (ki[0], i, 0)), pl.BlockSpec((3, tr, c), lambda i, ki: (0, i, 0))],
            out_specs=pl.BlockSpec((tr, c), lambda i, ki: (i, 0))),
        out_shape=jax.ShapeDtypeStruct((r, c), f32), name=name, compiler_params=_cp(("parallel",)),
    )(k_idx, sums, recv)


def _reduce_scatter(parts, tag):
    x, y, c = _my_place()
    c_idx = jnp.reshape(c, (1,)).astype(jnp.int32)
    k_idx = jnp.reshape(2 * x + y, (1,)).astype(jnp.int32)
    na = len(parts)
    recv = _swap_layers(parts, f"rs_swap{tag}")
    sums = [_add_own_layer(parts[a], recv[a], c_idx, f"rs_add_layer{tag}_{a}") for a in range(na)]
    recv = _scatter_shards(sums, f"rs_scatter{tag}")
    finals = [_add_own_shard(sums[a], recv[a], k_idx, f"rs_add_shard{tag}_{a}") for a in range(na)]
    return _share_layers(finals, f"rs_share{tag}")


_SMALL = [("ada_b", (DEPTH, 3 * D)), ("norm_w", (DEPTH, D)), ("gm_ln_w", (DEPTH, D)), ("gm_ln_b", (DEPTH, D)),
          ("gm_ws", (DEPTH, NG, CH, CH)), ("gm_bs", (DEPTH, NG, CH)), ("conv_b", (DEPTH, CONVD)), ("dt_bias", (DEPTH, NH)),
          ("a_log", (DEPTH, NH)), ("d_skip", (DEPTH, NH)), ("ssm_norm_w", (DEPTH, DIN)), ("final_norm_w", (D,))]


def _rows_of(shape):
    n = 1
    for d in shape:
        n *= d
    return -(-n // 1024) * 8


def _pack(arrays):
    rows = []
    for a in arrays:
        flat = a.reshape(-1)
        r = _rows_of(a.shape)
        rows.append(jnp.pad(flat, (0, r * 128 - flat.shape[0])).reshape(r, 128))
    return jnp.concatenate(rows, axis=0)


def _unpack(buf, shapes):
    out, at = [], 0
    for shp in shapes:
        r = _rows_of(shp)
        n = 1
        for d in shp:
            n *= d
        out.append(buf[at:at + r].reshape(-1)[:n].reshape(shp))
        at += r
    return out


def kernel(x, c, ada_w, ada_b, norm_w, w_in, gm_ln_w, gm_ln_b, gm_ws, gm_bs, conv_w, conv_b, dt_bias, a_log, d_skip, ssm_norm_w, w_proj_a, w_proj_b, w_out, final_norm_w, loss_target, m_ada_w, m_ada_b, m_norm_w, m_w_in, m_gm_ln_w, m_gm_ln_b, m_gm_ws, m_gm_bs, m_conv_w, m_conv_b, m_dt_bias, m_a_log, m_d_skip, m_ssm_norm_w, m_w_proj_a, m_w_proj_b, m_w_out, m_final_norm_w, v_ada_w, v_ada_b, v_norm_w, v_w_in, v_gm_ln_w, v_gm_ln_b, v_gm_ws, v_gm_bs, v_conv_w, v_conv_b, v_dt_bias, v_a_log, v_d_skip, v_ssm_norm_w, v_w_proj_a, v_w_proj_b, v_w_out, v_final_norm_w):
    xi, yi, ci = _my_place()
    k_me = 2 * xi + yi
    b_me = 4 * xi + 2 * yi + ci
    w = dict(ada_b=ada_b, norm_w=norm_w, gm_ln_w=gm_ln_w, gm_ln_b=gm_ln_b, gm_ws=gm_ws, gm_bs=gm_bs, conv_b=conv_b, dt_bias=dt_bias,
             a_log=a_log, d_skip=d_skip, ssm_norm_w=ssm_norm_w, final_norm_w=final_norm_w)
    m = dict(ada_b=m_ada_b, norm_w=m_norm_w, gm_ln_w=m_gm_ln_w, gm_ln_b=m_gm_ln_b, gm_ws=m_gm_ws, gm_bs=m_gm_bs, conv_b=m_conv_b,
             dt_bias=m_dt_bias, a_log=m_a_log, d_skip=m_d_skip, ssm_norm_w=m_ssm_norm_w, final_norm_w=m_final_norm_w)
    v = dict(ada_b=v_ada_b, norm_w=v_norm_w, gm_ln_w=v_gm_ln_w, gm_ln_b=v_gm_ln_b, gm_ws=v_gm_ws, gm_bs=v_gm_bs, conv_b=v_conv_b,
             dt_bias=v_dt_bias, a_log=v_a_log, d_skip=v_d_skip, ssm_norm_w=v_ssm_norm_w, final_norm_w=v_final_norm_w)

    c_slot = lax.dynamic_update_slice(jnp.zeros((8, D), f32), c, (b_me, 0))
    c_all = _allreduce(c_slot.reshape(64, 128), _AXES, "gather_c").reshape(8, D)
    ada_b_cols = lax.dynamic_slice(ada_b, (0, k_me * ADA_COLS), (DEPTH, ADA_COLS)).reshape(DEPTH, 1, ADA_COLS)
    mod_cols = _ada_fwd(c_all, ada_w, ada_b_cols, "ada_fwd")
    mod_slot = lax.dynamic_update_slice(jnp.zeros((DEPTH, 8, 3 * D), f32), mod_cols, (0, 0, k_me * ADA_COLS))
    mod_all = _allreduce(mod_slot.reshape(-1, 128), ("x", "y"), "gather_mod").reshape(DEPTH, 8, 3 * D)
    mod_me = lax.dynamic_slice(mod_all, (0, b_me, 0), (DEPTH, 1, 3 * D))
    mods = [(mod_me[l, :, 0:D], mod_me[l, :, D:2 * D], mod_me[l, :, 2 * D:3 * D]) for l in range(DEPTH)]

    rows_sh = jnp.concatenate([w_proj_a, w_proj_b, w_out], axis=1).astype(bf16)
    win_all, rows_all, conv_all = _gather_weights([w_in.astype(bf16), rows_sh, conv_w], "gather_weights")
    ops = []
    for l in range(DEPTH):
        w_in_l = jnp.concatenate([win_all[k, l] for k in range(NSHARD)], axis=1)
        wa_l = jnp.concatenate([rows_all[k, l, 0:256] for k in range(NSHARD)], axis=0)
        wb_l = jnp.concatenate([rows_all[k, l, 256:768] for k in range(NSHARD)], axis=0)
        wo_l = jnp.concatenate([rows_all[k, l, 768:1024] for k in range(NSHARD)], axis=0)
        cw_l = jnp.concatenate([conv_all[k, l] for k in range(NSHARD)], axis=1)
        ops.append(_layer_operands(w_in_l, cw_l, wa_l, wb_l, wo_l, norm_w[l], gm_ln_w[l], gm_ln_b[l], gm_ws[l], gm_bs[l], conv_b[l],
                                   dt_bias[l], a_log[l], d_skip[l], ssm_norm_w[l]))

    loss_me, dx, grads, dfnw = _local_step(x[0], loss_target[0], mods, ops, final_norm_w)
    loss = lax.psum(loss_me, _AXES)

    s_in = (N_IN // NSHARD)
    g_in = jnp.stack([grads[l]["w_in"].reshape(D, NSHARD, s_in).transpose(1, 0, 2) for l in range(DEPTH)])
    g_rows = jnp.stack([jnp.concatenate([grads[l]["w_proj_a"].reshape(NSHARD, 256, D), grads[l]["w_proj_b"].reshape(NSHARD, 512, D),
                                         grads[l]["w_out"].reshape(NSHARD, 256, D)], axis=1) for l in range(DEPTH)])
    g_conv = jnp.stack([grads[l]["conv_w"].reshape(4, NSHARD, D).transpose(1, 0, 2) for l in range(DEPTH)])
    gr_in, gr_rows, gr_conv = _reduce_scatter([g_in, g_rows, g_conv], "")

    def upd2(wv, gv, mv, vv, name):
        shp = wv.shape
        flat = lambda t: t.reshape(-1, shp[-1])
        d_, m_, v_ = _adamw(flat(wv), flat(gv), flat(mv), flat(vv), name)
        return d_.reshape(shp), m_.reshape(shp), v_.reshape(shp)

    d_in, nm_in, nv_in = upd2(w_in, gr_in, m_w_in, v_w_in, "adamw_w_in")
    cat = lambda a, b, c_: jnp.concatenate([a, b, c_], axis=1)
    d_rows, nm_rows, nv_rows = upd2(cat(w_proj_a, w_proj_b, w_out), gr_rows, cat(m_w_proj_a, m_w_proj_b, m_w_out),
                                    cat(v_w_proj_a, v_w_proj_b, v_w_out), "adamw_rows")
    split = lambda t: (t[:, 0:256], t[:, 256:768], t[:, 768:1024])

    dmod_slot = lax.dynamic_update_slice(jnp.zeros((DEPTH, 8, 3 * D), f32),
                                         jnp.stack([grads[l]["mod"] for l in range(DEPTH)]).reshape(DEPTH, 1, 3 * D), (0, b_me, 0))
    small_g = {n: (dfnw if n == "final_norm_w" else jnp.stack([grads[l]["mod" if n == "ada_b" else n] for l in range(DEPTH)]))
               for n, _ in _SMALL}
    packed = _allreduce(_pack([small_g[n] for n, _ in _SMALL] + [dmod_slot]), _AXES, "allreduce_small")
    n_small = sum(_rows_of(s) for _, s in _SMALL)
    g_small = packed[0:n_small]
    dmod_all = packed[n_small:].reshape(-1)[:DEPTH * 8 * 3 * D].reshape(DEPTH, 8, 3 * D)
    small_gw = jnp.concatenate([g_small, _pack([gr_conv])], axis=0)
    d_s, nm_s, nv_s = _adamw(_pack([w[n] for n, _ in _SMALL] + [conv_w]), small_gw,
                             _pack([m[n] for n, _ in _SMALL] + [m_conv_w]), _pack([v[n] for n, _ in _SMALL] + [v_conv_w]), "adamw_small")
    shapes = [s for _, s in _SMALL] + [conv_w.shape]
    names = [n for n, _ in _SMALL] + ["conv_w"]
    g_d = dict(zip(names, _unpack(small_gw, shapes)))
    d_d = dict(zip(names, _unpack(d_s, shapes)))
    nm_d = dict(zip(names, _unpack(nm_s, shapes)))
    nv_d = dict(zip(names, _unpack(nv_s, shapes)))

    dmod_cols = lax.dynamic_slice(dmod_all, (0, 0, k_me * ADA_COLS), (DEPTH, 8, ADA_COLS))
    g_ada, d_ada, nm_ada, nv_ada = _ada_bwd_adamw(c_all, dmod_cols, ada_w, m_ada_w, v_ada_w, "ada_bwd_adamw")

    def by_name(big, small):
        ga, gb, go = split(big[1])
        return dict(small, ada_w=big[2], w_in=big[0], w_proj_a=ga, w_proj_b=gb, w_out=go)

    order = ["ada_w", "ada_b", "norm_w", "w_in", "gm_ln_w", "gm_ln_b", "gm_ws", "gm_bs", "conv_w", "conv_b", "dt_bias", "a_log",
             "d_skip", "ssm_norm_w", "w_proj_a", "w_proj_b", "w_out", "final_norm_w"]
    outs = []
    for big, small in (((gr_in, gr_rows, g_ada), g_d), ((d_in, d_rows, d_ada), d_d), ((nm_in, nm_rows, nm_ada), nm_d),
                       ((nv_in, nv_rows, nv_ada), nv_d)):
        t = by_name(big, small)
        outs += [t[n] for n in order]
    return (loss, dx.reshape(1, -1, D), *outs)
```

```python
import jax
import jax.numpy as jnp
from jax import lax
from jax.experimental import pallas as pl
from jax.experimental.pallas import tpu as pltpu

f32 = jnp.float32
bf16 = jnp.bfloat16

D = 1024
DEPTH = 2
EPS = 1e-6
CH = 128
NG = 8
HPG = 4
HD = 64
NH = NG * HPG
NST = 128
DIN = 2048
CONVD = 4096
GW = DIN // NG
PB = CONVD + DIN + 256
PA = 3 * D
PG = 2 * D
N_IN = 11296
NSHARD = 4
V7X_VMEM_BYTES = 64 * 2 ** 20
VMEM_LIMIT = V7X_VMEM_BYTES - 8 * 2 ** 20

ADAM_LR, ADAM_B1, ADAM_B2, ADAM_EPS, ADAM_WD, ADAM_STEP = 0.001, 0.9, 0.999, 1e-08, 0.01, 10

_HI = lax.Precision.HIGHEST


def _cp(sem):
    return pltpu.CompilerParams(dimension_semantics=sem, vmem_limit_bytes=VMEM_LIMIT)


def _sigmoid(x):
    return 0.5 * jnp.tanh(0.5 * x) + 0.5


def _silu_and_grad(x):
    s = _sigmoid(x)
    return x * s, s * (1.0 + x * (1.0 - s))


_GELU_K = 0.7978845608028654
_GELU_C = 0.044715


def _gelu_and_grad(x):
    x2 = x * x
    t = jnp.tanh(_GELU_K * (x + _GELU_C * x * x2))
    g = 0.5 * x * (1.0 + t)
    dg = 0.5 * (1.0 + t) + 0.5 * x * (1.0 - t * t) * _GELU_K * (1.0 + 3.0 * _GELU_C * x2)
    return g, dg


def _gelu(x):
    t = jnp.tanh(_GELU_K * (x + _GELU_C * x * x * x))
    return 0.5 * x * (1.0 + t)


def _softplus(x):
    return jnp.maximum(x, 0.0) + jnp.log(1.0 + jnp.exp(-jnp.abs(x)))


def _dot(a, b):
    return jnp.dot(a, b, preferred_element_type=f32)


def _dot_nt(a, b):
    return lax.dot_general(a, b, (((1,), (1,)), ((), ())), preferred_element_type=f32)


def _dot_tn(a, b):
    return lax.dot_general(a, b, (((0,), (0,)), ((), ())), preferred_element_type=f32)


def _dot_hi(a, b):
    return jnp.dot(a, b, precision=_HI, preferred_element_type=f32)


def _rmsmod_fwd(x, nw, scale, shift, name):
    s = x.shape[0]
    ts = min(512, s)

    def body(x_ref, nw_ref, sc_ref, sh_ref, h_ref):
        xv = x_ref[...]
        r = lax.rsqrt(jnp.mean(xv * xv, axis=-1, keepdims=True) + EPS)
        h_ref[...] = ((xv * r) * nw_ref[...] * (1.0 + sc_ref[...]) + sh_ref[...]).astype(bf16)

    row = pl.BlockSpec((1, D), lambda i: (0, 0))
    tile = pl.BlockSpec((ts, D), lambda i: (i, 0))
    return pl.pallas_call(
        body, grid=(s // ts,), in_specs=[tile, row, row, row], out_specs=tile,
        out_shape=jax.ShapeDtypeStruct((s, D), bf16), name=name, compiler_params=_cp(("parallel",)),
    )(x, nw, scale, shift)


def _rmsmod_bwd(dh, x, dres, nw, scale, name):
    s = x.shape[0]
    ts = min(512, s)

    def body(dh_ref, x_ref, dres_ref, nw_ref, sc_ref, dx_ref, dsc_ref, dsh_ref, dnw_ref):
        @pl.when(pl.program_id(0) == 0)
        def _():
            dsc_ref[...] = jnp.zeros_like(dsc_ref)
            dsh_ref[...] = jnp.zeros_like(dsh_ref)
            dnw_ref[...] = jnp.zeros_like(dnw_ref)
        xv = x_ref[...]
        dhv = dh_ref[...]
        r = lax.rsqrt(jnp.mean(xv * xv, axis=-1, keepdims=True) + EPS)
        xn = xv * r
        one_sc = 1.0 + sc_ref[...]
        dsc_ref[...] += jnp.sum(dhv * xn * nw_ref[...], axis=0, keepdims=True)
        dsh_ref[...] += jnp.sum(dhv, axis=0, keepdims=True)
        dnw_ref[...] += jnp.sum(dhv * xn * one_sc, axis=0, keepdims=True)
        dxn = dhv * (nw_ref[...] * one_sc)
        dx_ref[...] = r * (dxn - xn * jnp.mean(dxn * xn, axis=-1, keepdims=True)) + dres_ref[...]

    row = pl.BlockSpec((1, D), lambda i: (0, 0))
    tile = pl.BlockSpec((ts, D), lambda i: (i, 0))
    vec = jax.ShapeDtypeStruct((1, D), f32)
    return pl.pallas_call(
        body, grid=(s // ts,), in_specs=[tile, tile, tile, row, row], out_specs=[tile, row, row, row],
        out_shape=[jax.ShapeDtypeStruct((s, D), f32), vec, vec, vec], name=name, compiler_params=_cp(("arbitrary",)),
    )(dh, x, dres, nw, scale)


def _pick(n, prefs):
    for p in prefs:
        if n % p == 0:
            return p
    return n


def _mm(a, b, out_dtype, name, c_in=None):
    m, k = a.shape
    n = b.shape[1]
    tm = _pick(m, (1024, 512, 256))
    tn = _pick(n, (1280, 1024, 512))
    tk = _pick(k, (1280, 1024, 512))
    nk = k // tk

    def body(*refs):
        if c_in is not None:
            a_ref, b_ref, c_ref, o_ref, acc = refs
        else:
            a_ref, b_ref, o_ref, acc = refs
        kk = pl.program_id(2)

        @pl.when(kk == 0)
        def _():
            if c_in is not None:
                acc[...] = c_ref[...]
            else:
                acc[...] = jnp.zeros_like(acc)
        acc[...] += _dot(a_ref[...], b_ref[...])

        @pl.when(kk == nk - 1)
        def _():
            o_ref[...] = acc[...].astype(out_dtype)

    in_specs = [pl.BlockSpec((tm, tk), lambda j, i, kk: (i, kk)), pl.BlockSpec((tk, tn), lambda j, i, kk: (kk, j))]
    args = [a, b]
    if c_in is not None:
        in_specs.append(pl.BlockSpec((tm, tn), lambda j, i, kk: (i, j)))
        args.append(c_in)
    return pl.pallas_call(
        body, grid=(n // tn, m // tm, nk), in_specs=in_specs, out_specs=pl.BlockSpec((tm, tn), lambda j, i, kk: (i, j)),
        out_shape=jax.ShapeDtypeStruct((m, n), out_dtype), scratch_shapes=[pltpu.VMEM((tm, tn), f32)], name=name,
        compiler_params=_cp(("parallel", "parallel", "arbitrary")),
    )(*args)


def _mm_tn(a, b, name):
    t, k1 = a.shape
    n = b.shape[1]
    t1 = _pick(k1, (1024, 512))
    tn = _pick(n, (1280, 1024, 512))
    tt = _pick(t, (512, 256))
    nt = t // tt

    def body(a_ref, b_ref, o_ref):
        tt_i = pl.program_id(2)

        @pl.when(tt_i == 0)
        def _():
            o_ref[...] = jnp.zeros_like(o_ref)
        o_ref[...] += _dot_tn(a_ref[...], b_ref[...])

    return pl.pallas_call(
        body, grid=(k1 // t1, n // tn, nt),
        in_specs=[pl.BlockSpec((tt, t1), lambda i, j, tt_i: (tt_i, i)), pl.BlockSpec((tt, tn), lambda i, j, tt_i: (tt_i, j))],
        out_specs=pl.BlockSpec((t1, tn), lambda i, j, tt_i: (i, j)),
        out_shape=jax.ShapeDtypeStruct((k1, n), f32), name=name,
        compiler_params=_cp(("parallel", "parallel", "arbitrary")),
    )(a, b)


def _ln_stats(v):
    mu = jnp.mean(v, axis=-1, keepdims=True)
    vc = v - mu
    rstd = lax.rsqrt(jnp.mean(vc * vc, axis=-1, keepdims=True) + EPS)
    return vc * rstd, rstd


def _mix(w_ref, vl):
    return jnp.concatenate([_dot(w_ref[g], vl[:, g * CH:(g + 1) * CH]) for g in range(NG)], axis=1)


def _branch_a_fwd(proj_a, lnw, lnb, wsm, bsf, name):
    s = proj_a.shape[0]
    ta = min(256, s)

    def body(pu_ref, pv_ref, pz_ref, lnw_ref, lnb_ref, w_ref, bs_ref, ya_ref):
        for c in range(ta // CH):
            rows = pl.ds(c * CH, CH)
            vh, _ = _ln_stats(_gelu(pv_ref[rows, :]))
            vl = (vh * lnw_ref[...] + lnb_ref[...]).astype(bf16)
            mixed = _mix(w_ref, vl) + bs_ref[...]
            pz = pz_ref[rows, :]
            ya_ref[rows, :] = (_gelu(pu_ref[rows, :]) * mixed * (pz * _sigmoid(pz))).astype(bf16)

    row = pl.BlockSpec((1, D), lambda i: (0, 0))
    return pl.pallas_call(
        body, grid=(s // ta,),
        in_specs=[pl.BlockSpec((ta, D), lambda i: (i, 0)), pl.BlockSpec((ta, D), lambda i: (i, 1)),
                  pl.BlockSpec((ta, D), lambda i: (i, 2)), row, row,
                  pl.BlockSpec((NG, CH, CH), lambda i: (0, 0, 0)), pl.BlockSpec((CH, D), lambda i: (0, 0))],
        out_specs=pl.BlockSpec((ta, D), lambda i: (i, 0)),
        out_shape=jax.ShapeDtypeStruct((s, D), bf16), name=name, compiler_params=_cp(("parallel",)),
    )(proj_a, proj_a, proj_a, lnw, lnb, wsm, bsf)


def _branch_a_bwd(proj_a, dya, lnw, lnb, wsm, wsm_t, bsf, name):
    s = proj_a.shape[0]
    ta = min(256, s)

    def body(pu_ref, pv_ref, pz_ref, dya_ref, lnw_ref, lnb_ref, w_ref, wt_ref, bs_ref,
             dp_ref, dws_ref, dbs_ref, dlnw_ref, dlnb_ref):
        @pl.when(pl.program_id(0) == 0)
        def _():
            dws_ref[...] = jnp.zeros_like(dws_ref)
            dbs_ref[...] = jnp.zeros_like(dbs_ref)
            dlnw_ref[...] = jnp.zeros_like(dlnw_ref)
            dlnb_ref[...] = jnp.zeros_like(dlnb_ref)
        for c in range(ta // CH):
            rows = pl.ds(c * CH, CH)
            u, du = _gelu_and_grad(pu_ref[rows, :])
            v, dv_act = _gelu_and_grad(pv_ref[rows, :])
            zg, dzg = _silu_and_grad(pz_ref[rows, :])
            vh, rstd = _ln_stats(v)
            vl = (vh * lnw_ref[...] + lnb_ref[...]).astype(bf16)
            mixed = _mix(w_ref, vl) + bs_ref[...]
            dy = dya_ref[rows, :].astype(f32)
            dmixed = dy * u * zg
            dp_ref[rows, 0:D] = (dy * mixed * zg * du).astype(bf16)
            dp_ref[rows, 2 * D:3 * D] = (dy * u * mixed * dzg).astype(bf16)
            dmb = dmixed.astype(bf16)
            dbs_ref[...] += dmixed
            dvl = _mix(wt_ref, dmb)
            for g in range(NG):
                cols = slice(g * CH, (g + 1) * CH)
                dws_ref[g] += _dot_nt(dmb[:, cols], vl[:, cols])
            dlnw_ref[...] += jnp.sum(dvl * vh, axis=0, keepdims=True)
            dlnb_ref[...] += jnp.sum(dvl, axis=0, keepdims=True)
            dvh = dvl * lnw_ref[...]
            dv = rstd * (dvh - jnp.mean(dvh, axis=-1, keepdims=True) - vh * jnp.mean(dvh * vh, axis=-1, keepdims=True))
            dp_ref[rows, D:2 * D] = (dv * dv_act).astype(bf16)

    row = pl.BlockSpec((1, D), lambda i: (0, 0))
    wspec = pl.BlockSpec((NG, CH, CH), lambda i: (0, 0, 0))
    full = pl.BlockSpec((CH, D), lambda i: (0, 0))
    return pl.pallas_call(
        body, grid=(s // ta,),
        in_specs=[pl.BlockSpec((ta, D), lambda i: (i, 0)), pl.BlockSpec((ta, D), lambda i: (i, 1)),
                  pl.BlockSpec((ta, D), lambda i: (i, 2)), pl.BlockSpec((ta, D), lambda i: (i, 0)),
                  row, row, wspec, wspec, full],
        out_specs=[pl.BlockSpec((ta, PA), lambda i: (i, 0)), wspec, full, row, row],
        out_shape=[jax.ShapeDtypeStruct((s, PA), bf16), jax.ShapeDtypeStruct((NG, CH, CH), f32),
                   jax.ShapeDtypeStruct((CH, D), f32), jax.ShapeDtypeStruct((1, D), f32), jax.ShapeDtypeStruct((1, D), f32)],
        name=name, compiler_params=_cp(("arbitrary",)),
    )(proj_a, proj_a, proj_a, dya, lnw, lnb, wsm, wsm_t, bsf)


GB = GW + 2 * NST


def _group_major(xs, b, c):
    lead = xs.shape[:-1]
    return jnp.concatenate([xs.reshape(lead + (NG, GW)), b.reshape(lead + (NG, NST)), c.reshape(lead + (NG, NST))],
                           axis=-1).reshape(lead + (CONVD,))


def _from_group_major(t):
    lead = t.shape[:-1]
    t = t.reshape(lead + (NG, GB))
    return jnp.concatenate([t[..., 0:GW].reshape(lead + (DIN,)), t[..., GW:GW + NST].reshape(lead + (NG * NST,)),
                            t[..., GW + NST:GB].reshape(lead + (NG * NST,))], axis=-1)


def _shift_rows(x, prev8, j):
    xr = pltpu.roll(x, j, 0)
    fix = pltpu.roll(prev8, j, 0)
    rid = lax.broadcasted_iota(jnp.int32, (8, x.shape[1]), 0)
    top = jnp.where(rid < j, fix, xr[0:8])
    return jnp.concatenate([top, xr[8:]], axis=0)


def _shift_rows_up(d, next8, j):
    dr = pltpu.roll(d, CH - j, 0)
    fix = pltpu.roll(next8, 8 - j, 0)
    rid = lax.broadcasted_iota(jnp.int32, (8, d.shape[1]), 0)
    bot = jnp.where(rid >= 8 - j, fix, dr[CH - 8:CH])
    return jnp.concatenate([dr[0:CH - 8], bot], axis=0)


def _conv_pre(x, prev8, cw_ref, cb_ref, cols):
    shifted = [_shift_rows(x, prev8, j) for j in (1, 2, 3)]
    conv = cb_ref[:, cols] + cw_ref[3:4, cols] * x
    for j in (1, 2, 3):
        conv = conv + cw_ref[3 - j:4 - j, cols] * shifted[j - 1]
    return conv, shifted


def _tril_mask():
    return lax.broadcasted_iota(jnp.int32, (CH, CH), 0) >= lax.broadcasted_iota(jnp.int32, (CH, CH), 1)


def _sum_all(v):
    return jnp.sum(jnp.sum(v, axis=0, keepdims=True), axis=1, keepdims=True)


def _lanes(g, width, base=0):
    return pl.ds(pl.multiple_of(base + g * width, width), width)


def _branch_b_fwd(proj_b, cw, cb, dtb, a_row, dkc, snw, name):
    s = proj_b.shape[0]
    nc = s // CH

    def body(xbc_ref, sz_ref, dtr_ref, cw_ref, cb_ref, dtb_ref, a_ref, dkc_ref, snw_ref,
             yb_ref, y_ref, st_ref, prev8, state, acst_s):
        @pl.when(pl.program_id(0) == 0)
        def _():
            prev8[...] = jnp.zeros_like(prev8)
            state[...] = jnp.zeros_like(state)
        st_ref[0] = state[...].astype(bf16)
        mask = _tril_mask()
        dt_all = _softplus(dtr_ref[:, 0:CH] + dtb_ref[...])
        acs_all = _dot_hi(mask.astype(f32), dt_all * a_ref[...])
        acst_s[...] = acs_all.T

        def group(g, carry):
            cols = _lanes(g, GB)
            gcols = _lanes(g, GW)
            x = xbc_ref[:, cols]
            conv, _ = _conv_pre(x, prev8[:, cols], cw_ref, cb_ref, cols)
            prev8[:, cols] = x[CH - 8:CH]
            xc = conv * _sigmoid(conv)
            xs = xc[:, 0:GW]
            bg = xc[:, GW:GW + NST].astype(bf16)
            cg = xc[:, GW + NST:GB].astype(bf16)
            back = lax.rem(CH - HPG * g, CH)
            dt = pltpu.roll(dt_all, back, 1)
            acs = pltpu.roll(acs_all, back, 1)
            cbm = _dot_nt(cg, bg)
            dkc_g = dkc_ref[:, gcols]
            y_parts = []
            for r in range(HPG):
                colb = jnp.broadcast_to(acs[:, r:r + 1], (CH, CH))
                row = acst_s[pl.ds(g * HPG + r, 1), :]
                lmat = jnp.exp(jnp.where(mask, colb - row, -jnp.inf))
                xr = xs[:, r * HD:(r + 1) * HD]
                xd = xr * dt[:, r:r + 1]
                sp = state[g * HPG + r]
                col = colb[:, 0:HD]
                alast = colb[CH - 1:CH, 0:HD]
                y_r = _dot((cbm * lmat).astype(bf16), xd.astype(bf16))
                y_r = y_r + jnp.exp(col) * _dot_nt(cg, sp.astype(bf16))
                y_parts.append(y_r + xr * dkc_g[:, r * HD:(r + 1) * HD])
                cs = _dot_tn((xd * jnp.exp(alast - col)).astype(bf16), bg)
                state[g * HPG + r] = jnp.exp(colb[CH - 1:CH, :]) * sp + cs
            y = jnp.concatenate(y_parts, axis=1)
            szv = sz_ref[:, gcols]
            yz = y * (szv * _sigmoid(szv))
            rr = lax.rsqrt(jnp.mean(yz * yz, axis=-1, keepdims=True) + EPS)
            yb_ref[:, gcols] = (yz * rr * snw_ref[:, gcols]).astype(bf16)
            y_ref[:, gcols] = y.astype(bf16)
            return carry

        lax.fori_loop(0, NG, group, 0)

    const2 = lambda c: (0, 0)
    return pl.pallas_call(
        body, grid=(nc,),
        in_specs=[pl.BlockSpec((CH, CONVD), lambda c: (c, 0)), pl.BlockSpec((CH, DIN), lambda c: (c, CONVD // DIN)),
                  pl.BlockSpec((CH, 256), lambda c: (c, (CONVD + DIN) // 256)),
                  pl.BlockSpec((4, CONVD), const2), pl.BlockSpec((1, CONVD), const2),
                  pl.BlockSpec((1, CH), const2), pl.BlockSpec((1, CH), const2),
                  pl.BlockSpec((1, DIN), const2), pl.BlockSpec((1, DIN), const2)],
        out_specs=[pl.BlockSpec((CH, DIN), lambda c: (c, 0)), pl.BlockSpec((CH, DIN), lambda c: (c, 0)),
                   pl.BlockSpec((1, NH, HD, NST), lambda c: (c, 0, 0, 0))],
        out_shape=[jax.ShapeDtypeStruct((s, DIN), bf16), jax.ShapeDtypeStruct((s, DIN), bf16),
                   jax.ShapeDtypeStruct((nc, NH, HD, NST), bf16)],
        scratch_shapes=[pltpu.VMEM((8, CONVD), f32), pltpu.VMEM((NH, HD, NST), f32), pltpu.VMEM((CH, CH), f32)],
        name=name, compiler_params=_cp(("arbitrary",)),
    )(proj_b, proj_b, proj_b, cw, cb, dtb, a_row, dkc, snw)


def _branch_b_bwd(proj_b, dyb, y_sv, states, cw, cb, dtb, a_row, dkc, snw, ind, name):
    s = proj_b.shape[0]
    nc = s // CH

    def body(xbc_ref, xprev_ref, sz_ref, dtr_ref, dyb_ref, y_ref, st_ref, cw_ref, cb_ref, dtb_ref, a_ref, dkc_ref,
             snw_ref, ind_ref, dp_ref, dcw_ref, dcb_ref, ddtb_ref, dal_ref, ddk_ref, dsnw_ref,
             dstate, dnext8, acst_s, dacs_acc, q2_acc):
        step = pl.program_id(0)
        cc = nc - 1 - step

        @pl.when(step == 0)
        def _():
            dstate[...] = jnp.zeros_like(dstate)
            dnext8[...] = jnp.zeros_like(dnext8)
            dcw_ref[...] = jnp.zeros_like(dcw_ref)
            dcb_ref[...] = jnp.zeros_like(dcb_ref)
            ddtb_ref[...] = jnp.zeros_like(ddtb_ref)
            dal_ref[...] = jnp.zeros_like(dal_ref)
            ddk_ref[...] = jnp.zeros_like(ddk_ref)
            dsnw_ref[...] = jnp.zeros_like(dsnw_ref)

        first = jnp.where(cc > 0, 1.0, 0.0)
        dacs_acc[...] = jnp.zeros_like(dacs_acc)
        q2_acc[...] = jnp.zeros_like(q2_acc)
        mask = _tril_mask()
        tri_t = (lax.broadcasted_iota(jnp.int32, (CH, CH), 0) <= lax.broadcasted_iota(jnp.int32, (CH, CH), 1)).astype(f32)
        lane1 = lax.broadcasted_iota(jnp.int32, (1, CH), 1)
        is_last = lax.broadcasted_iota(jnp.int32, (CH, 1), 0) == CH - 1
        z_all = dtr_ref[:, 0:CH] + dtb_ref[...]
        dt_all = _softplus(z_all)
        adt_all = dt_all * a_ref[...]
        acs_all = _dot_hi(mask.astype(f32), adt_all)
        acst_s[...] = acs_all.T

        def ind_sum(v):
            hi = v.astype(bf16)
            lo = (v - hi.astype(f32)).astype(bf16)
            return _dot(hi, ind_ref[...]) + _dot(lo, ind_ref[...])

        def group(g, carry):
            cols = _lanes(g, GB)
            gcols = _lanes(g, GW)
            x = xbc_ref[:, cols]
            conv, shifted = _conv_pre(x, xprev_ref[:, cols] * first, cw_ref, cb_ref, cols)
            sg = _sigmoid(conv)
            xc = conv * sg
            xs = xc[:, 0:GW]
            bg = xc[:, GW:GW + NST].astype(bf16)
            cg = xc[:, GW + NST:GB].astype(bf16)

            y = y_ref[:, gcols].astype(f32)
            silu_sz, dsilu_sz = _silu_and_grad(sz_ref[:, gcols])
            yz = y * silu_sz
            rr = lax.rsqrt(jnp.mean(yz * yz, axis=-1, keepdims=True) + EPS)
            dyb_g = dyb_ref[:, gcols].astype(f32)
            w = dyb_g * snw_ref[:, gcols]
            dsnw_ref[:, gcols] += jnp.sum(dyb_g * yz * rr, axis=0, keepdims=True)
            dyz = rr * w - yz * (rr * rr * rr) * jnp.mean(w * yz, axis=-1, keepdims=True)
            dp_ref[:, _lanes(g, GW, CONVD)] = (dyz * y * dsilu_sz).astype(bf16)
            dy_g = dyz * silu_sz
            ddk_ref[:, gcols] += jnp.sum(dy_g * xs, axis=0, keepdims=True)

            back = lax.rem(CH - HPG * g, CH)
            dt = pltpu.roll(dt_all, back, 1)
            acs = pltpu.roll(acs_all, back, 1)
            cbm = _dot_nt(cg, bg)
            d_cb = jnp.zeros((CH, CH), f32)
            d_bg = jnp.zeros((CH, NST), f32)
            d_cg = jnp.zeros((CH, NST), f32)
            lastrow = jnp.zeros((1, CH), f32)
            dxd_parts, dxs_parts, t_parts = [], [], []
            for r in range(HPG):
                h = g * HPG + r
                colb = jnp.broadcast_to(acs[:, r:r + 1], (CH, CH))
                row = acst_s[pl.ds(h, 1), :]
                lmat = jnp.exp(jnp.where(mask, colb - row, -jnp.inf))
                mmat_b = (cbm * lmat).astype(bf16)
                dtc = dt[:, r:r + 1]
                xd = xs[:, r * HD:(r + 1) * HD] * dtc
                col = colb[:, 0:HD]
                alast = colb[CH - 1:CH, 0:HD]
                dte = jnp.exp(alast - col)
                ea = jnp.exp(col)
                cd = jnp.exp(colb[CH - 1:CH, :])
                sp = st_ref[0, h]
                dsn = dstate[h]
                dsn_b = dsn.astype(bf16)
                dyr = dy_g[:, r * HD:(r + 1) * HD]
                dyr_b = dyr.astype(bf16)
                dye_b = (dyr * ea).astype(bf16)
                d_cg = d_cg + _dot(dye_b, sp)
                dxde = _dot_nt(bg, dsn_b)
                xdte = xd * dte
                xd_b = xd.astype(bf16)
                d_bg = d_bg + _dot(xdte.astype(bf16), dsn_b)
                dxd_diag = _dot_tn(mmat_b, dyr_b)
                dxd = dxde * dte + dxd_diag
                d_cb = d_cb + _dot_nt(dyr_b, xd_b) * lmat
                t_parts.append(dyr_b.astype(f32) * _dot(mmat_b, xd_b) + dyr * (ea * _dot_nt(cg, sp))
                               - xd_b.astype(f32) * dxd_diag - dxde * xdte)
                lastrow = jnp.where(lane1 == r, _sum_all(dsn * sp.astype(f32)) * cd + _sum_all(dxde * xdte), lastrow)
                dstate[h] = cd * dsn + _dot_tn(dye_b, cg)
                dxd_parts.append(dxd)
                dxs_parts.append(dxd * dtc)
            d_cb_b = d_cb.astype(bf16)
            d_bg = d_bg + _dot_tn(d_cb_b, cg)
            d_cg = d_cg + _dot(d_cb_b, bg)
            q2 = ind_sum(jnp.concatenate(dxd_parts, axis=1) * xs)
            dacs = ind_sum(jnp.concatenate(t_parts, axis=1)) + jnp.where(is_last, lastrow, 0.0)
            dacs_acc[...] += pltpu.roll(dacs, HPG * g, 1)
            q2_acc[...] += pltpu.roll(q2, HPG * g, 1)
            dxs = jnp.concatenate(dxs_parts, axis=1) + dy_g * dkc_ref[:, gcols]

            dconv = jnp.concatenate([dxs, d_bg, d_cg], axis=1) * (sg * (1.0 + conv * (1.0 - sg)))
            dcb_ref[:, cols] += jnp.sum(dconv, axis=0, keepdims=True)
            dcw_ref[3:4, cols] += jnp.sum(dconv * x, axis=0, keepdims=True)
            dx = cw_ref[3:4, cols] * dconv
            nxt = dnext8[:, cols]
            for j in (1, 2, 3):
                dcw_ref[3 - j:4 - j, cols] += jnp.sum(dconv * shifted[j - 1], axis=0, keepdims=True)
                dx = dx + cw_ref[3 - j:4 - j, cols] * _shift_rows_up(dconv, nxt, j)
            dnext8[:, cols] = dconv[0:8]
            dp_ref[:, cols] = dx.astype(bf16)
            return carry

        lax.fori_loop(0, NG, group, 0)

        dadt = _dot_hi(tri_t, dacs_acc[...])
        dal_ref[...] += jnp.sum(dadt * adt_all, axis=0, keepdims=True)
        ddz = (dadt * a_ref[...] + q2_acc[...]) * _sigmoid(z_all)
        ddtb_ref[...] += jnp.sum(ddz, axis=0, keepdims=True)
        dp_ref[:, CONVD + DIN:CONVD + DIN + CH] = ddz.astype(bf16)
        dp_ref[:, CONVD + DIN + CH:PB] = jnp.zeros((CH, PB - CONVD - DIN - CH), bf16)

    const2 = lambda c: (0, 0)
    rev = lambda c: (nc - 1 - c, 0)
    return pl.pallas_call(
        body, grid=(nc,),
        in_specs=[pl.BlockSpec((CH, CONVD), rev),
                  pl.BlockSpec((8, CONVD), lambda c: (jnp.maximum((nc - 1 - c) * (CH // 8) - 1, 0), 0)),
                  pl.BlockSpec((CH, DIN), lambda c: (nc - 1 - c, CONVD // DIN)),
                  pl.BlockSpec((CH, 256), lambda c: (nc - 1 - c, (CONVD + DIN) // 256)),
                  pl.BlockSpec((CH, DIN), rev), pl.BlockSpec((CH, DIN), rev),
                  pl.BlockSpec((1, NH, HD, NST), lambda c: (nc - 1 - c, 0, 0, 0)),
                  pl.BlockSpec((4, CONVD), const2), pl.BlockSpec((1, CONVD), const2),
                  pl.BlockSpec((1, CH), const2), pl.BlockSpec((1, CH), const2),
                  pl.BlockSpec((1, DIN), const2), pl.BlockSpec((1, DIN), const2), pl.BlockSpec((GW, CH), const2)],
        out_specs=[pl.BlockSpec((CH, PB), rev), pl.BlockSpec((4, CONVD), const2), pl.BlockSpec((1, CONVD), const2),
                   pl.BlockSpec((1, CH), const2), pl.BlockSpec((1, CH), const2), pl.BlockSpec((1, DIN), const2),
                   pl.BlockSpec((1, DIN), const2)],
        out_shape=[jax.ShapeDtypeStruct((s, PB), bf16), jax.ShapeDtypeStruct((4, CONVD), f32),
                   jax.ShapeDtypeStruct((1, CONVD), f32), jax.ShapeDtypeStruct((1, CH), f32),
                   jax.ShapeDtypeStruct((1, CH), f32), jax.ShapeDtypeStruct((1, DIN), f32),
                   jax.ShapeDtypeStruct((1, DIN), f32)],
        scratch_shapes=[pltpu.VMEM((NH, HD, NST), f32), pltpu.VMEM((8, CONVD), f32), pltpu.VMEM((CH, CH), f32),
                        pltpu.VMEM((CH, CH), f32), pltpu.VMEM((CH, CH), f32)],
        name=name, compiler_params=_cp(("arbitrary",)),
    )(proj_b, proj_b, proj_b, proj_b, dyb, y_sv, states, cw, cb, dtb, a_row, dkc, snw, ind)


def _merge_fwd(ya, yb, proj_g, x, gate, wa, wb, wo, name):
    s = x.shape[0]
    ts = min(512, s)

    def body(ya_ref, yb_ref, ga_ref, gb_ref, x_ref, gate_ref, wa_ref, wb_ref, wo_ref, xo_ref, pa_ref, pb_ref, mg_ref, o_ref):
        pa = _dot(ya_ref[...], wa_ref[...])
        pb = _dot(yb_ref[...], wb_ref[...])
        mg = (_sigmoid(ga_ref[...]) * pa + _sigmoid(gb_ref[...]) * pb).astype(bf16)
        o = _dot(mg, wo_ref[...])
        xo_ref[...] = x_ref[...] + gate_ref[...] * o
        pa_ref[...] = pa.astype(bf16)
        pb_ref[...] = pb.astype(bf16)
        mg_ref[...] = mg
        o_ref[...] = o.astype(bf16)

    tile = pl.BlockSpec((ts, D), lambda i: (i, 0))
    const = lambda i: (0, 0)
    act = jax.ShapeDtypeStruct((s, D), bf16)
    return pl.pallas_call(
        body, grid=(s // ts,),
        in_specs=[tile, pl.BlockSpec((ts, DIN), lambda i: (i, 0)), tile, pl.BlockSpec((ts, D), lambda i: (i, 1)), tile,
                  pl.BlockSpec((1, D), const), pl.BlockSpec((D, D), const), pl.BlockSpec((DIN, D), const),
                  pl.BlockSpec((D, D), const)],
        out_specs=[tile, tile, tile, tile, tile],
        out_shape=[jax.ShapeDtypeStruct((s, D), f32), act, act, act, act],
        name=name, compiler_params=_cp(("parallel",)),
    )(ya, yb, proj_g, proj_g, x, gate, wa, wb, wo)


def _merge_bwd(dxo, gate, o_sv, pa_sv, pb_sv, proj_g, wo_t, wa_t, wb_t, name):
    s = dxo.shape[0]
    ts = min(512, s)

    def body(dxo_ref, gate_ref, o_ref, pa_ref, pb_ref, ga_ref, gb_ref, wot_ref, wat_ref, wbt_ref,
             do_ref, dpa_ref, dpb_ref, dg_ref, dya_ref, dyb_ref, dgate_ref):
        @pl.when(pl.program_id(0) == 0)
        def _():
            dgate_ref[...] = jnp.zeros_like(dgate_ref)
        dxo_v = dxo_ref[...]
        dgate_ref[...] += jnp.sum(dxo_v * o_ref[...].astype(f32), axis=0, keepdims=True)
        do = (dxo_v * gate_ref[...]).astype(bf16)
        do_ref[...] = do
        dmg = _dot(do, wot_ref[...])
        sa = _sigmoid(ga_ref[...])
        sb = _sigmoid(gb_ref[...])
        dpa = (dmg * sa).astype(bf16)
        dpb = (dmg * sb).astype(bf16)
        dpa_ref[...] = dpa
        dpb_ref[...] = dpb
        dg_ref[:, 0:D] = (dmg * pa_ref[...].astype(f32) * sa * (1.0 - sa)).astype(bf16)
        dg_ref[:, D:2 * D] = (dmg * pb_ref[...].astype(f32) * sb * (1.0 - sb)).astype(bf16)
        dya_ref[...] = _dot(dpa, wat_ref[...]).astype(bf16)
        dyb_ref[...] = _dot(dpb, wbt_ref[...]).astype(bf16)

    tile = pl.BlockSpec((ts, D), lambda i: (i, 0))
    const = lambda i: (0, 0)
    act = jax.ShapeDtypeStruct((s, D), bf16)
    return pl.pallas_call(
        body, grid=(s // ts,),
        in_specs=[tile, pl.BlockSpec((1, D), const), tile, tile, tile, tile, pl.BlockSpec((ts, D), lambda i: (i, 1)),
                  pl.BlockSpec((D, D), const), pl.BlockSpec((D, D), const), pl.BlockSpec((D, DIN), const)],
        out_specs=[tile, tile, tile, pl.BlockSpec((ts, PG), lambda i: (i, 0)), tile, pl.BlockSpec((ts, DIN), lambda i: (i, 0)),
                   pl.BlockSpec((1, D), const)],
        out_shape=[act, act, act, jax.ShapeDtypeStruct((s, PG), bf16), act, jax.ShapeDtypeStruct((s, DIN), bf16),
                   jax.ShapeDtypeStruct((1, D), f32)],
        name=name, compiler_params=_cp(("arbitrary",)),
    )(dxo, gate, o_sv, pa_sv, pb_sv, proj_g, proj_g, wo_t, wa_t, wb_t)


def _final_loss(x, target, fnw, name):
    s = x.shape[0]
    ts = min(512, s)

    def body(x_ref, t_ref, w_ref, loss_ref, dx_ref, dw_ref):
        @pl.when(pl.program_id(0) == 0)
        def _():
            loss_ref[...] = jnp.zeros_like(loss_ref)
            dw_ref[...] = jnp.zeros_like(dw_ref)
        xv = x_ref[...]
        r = lax.rsqrt(jnp.mean(xv * xv, axis=-1, keepdims=True) + EPS)
        xn = xv * r
        err = xn * w_ref[...] - t_ref[...]
        part = jnp.sum(err * err, axis=0, keepdims=True)
        acc = part[:, 0:128]
        for k in range(1, D // 128):
            acc = acc + part[:, k * 128:(k + 1) * 128]
        loss_ref[0:1, :] += acc * (0.5 / D)
        dy = err * (1.0 / D)
        dw_ref[...] += jnp.sum(dy * xn, axis=0, keepdims=True)
        dxn = dy * w_ref[...]
        dx_ref[...] = r * (dxn - xn * jnp.mean(dxn * xn, axis=-1, keepdims=True))

    tile = pl.BlockSpec((ts, D), lambda i: (i, 0))
    row = pl.BlockSpec((1, D), lambda i: (0, 0))
    return pl.pallas_call(
        body, grid=(s // ts,), in_specs=[tile, tile, row],
        out_specs=[pl.BlockSpec((8, 128), lambda i: (0, 0)), tile, row],
        out_shape=[jax.ShapeDtypeStruct((8, 128), f32), jax.ShapeDtypeStruct((s, D), f32), jax.ShapeDtypeStruct((1, D), f32)],
        name=name, compiler_params=_cp(("arbitrary",)),
    )(x, target, fnw)


def _layer_operands(w_in, conv_w, wa, wb, wo, norm_w, gm_ln_w, gm_ln_b, gm_ws, gm_bs, conv_b, dt_bias, a_log, d_skip, ssm_norm_w):
    w_xbc = _group_major(w_in[:, 5120:7168], w_in[:, 7168:8192], w_in[:, 8192:9216])
    w_b = jnp.concatenate([w_xbc, w_in[:, 3072:5120], w_in[:, 9216:9248], jnp.zeros((D, 224), bf16)], axis=1)
    w_a = w_in[:, 0:3072]
    w_g = w_in[:, 9248:N_IN]
    tril = jnp.tril(jnp.ones((CH, CH), bool))
    wsm = jnp.where(tril[None], gm_ws, 0.0).astype(bf16)

    def heads_row(v):
        return jnp.pad(v, (0, CH - NH)).reshape(1, CH)

    return dict(
        w_b=w_b, w_a=w_a, w_g=w_g, w_b_t=w_b.T, w_a_t=w_a.T, w_g_t=w_g.T,
        wa=wa, wb=wb, wo=wo, wa_t=wa.T, wb_t=wb.T, wo_t=wo.T,
        norm_w=norm_w.reshape(1, D), lnw=gm_ln_w.reshape(1, D), lnb=gm_ln_b.reshape(1, D),
        wsm=wsm, wsm_t=jnp.swapaxes(wsm, 1, 2), bsf=jnp.repeat(gm_bs.T, CH, axis=1),
        cw=_group_major(conv_w[:, 0:DIN], conv_w[:, DIN:DIN + NG * NST], conv_w[:, DIN + NG * NST:CONVD]),
        cb=_group_major(conv_b[0:DIN], conv_b[DIN:DIN + NG * NST], conv_b[DIN + NG * NST:CONVD]).reshape(1, CONVD),
        dtb=heads_row(dt_bias), a_row=heads_row(-jnp.exp(a_log)),
        snw=ssm_norm_w.reshape(1, DIN), dkc=jnp.repeat(d_skip, HD).reshape(1, DIN),
        ind=(jnp.arange(GW)[:, None] // HD == jnp.arange(CH)[None, :]).astype(bf16),
    )


def _layer_fwd(x, shift, scale, gate, p, tag):
    h = _rmsmod_fwd(x, p["norm_w"], scale, shift, f"rmsmod_fwd{tag}")
    proj_b = _mm(h, p["w_b"], f32, f"proj_b{tag}")
    proj_a = _mm(h, p["w_a"], f32, f"proj_a{tag}")
    proj_g = _mm(h, p["w_g"], f32, f"proj_g{tag}")
    ya = _branch_a_fwd(proj_a, p["lnw"], p["lnb"], p["wsm"], p["bsf"], f"branch_a_fwd{tag}")
    yb, y_sv, states = _branch_b_fwd(proj_b, p["cw"], p["cb"], p["dtb"], p["a_row"], p["dkc"], p["snw"], f"branch_b_fwd{tag}")
    x_out, pa, pb, mg, o = _merge_fwd(ya, yb, proj_g, x, gate, p["wa"], p["wb"], p["wo"], f"merge_fwd{tag}")
    saved = dict(x=x, h=h, proj_b=proj_b, proj_a=proj_a, proj_g=proj_g, ya=ya, yb=yb, y=y_sv, states=states,
                 pa=pa, pb=pb, mg=mg, o=o, scale=scale, gate=gate)
    return x_out, saved


def _layer_bwd(dxo, sv, p, tag):
    do, dpa, dpb, dg, dya, dyb, dgate = _merge_bwd(dxo, sv["gate"], sv["o"], sv["pa"], sv["pb"], sv["proj_g"],
                                                   p["wo_t"], p["wa_t"], p["wb_t"], f"merge_bwd{tag}")
    d_wo = _mm_tn(sv["mg"], do, f"d_wo{tag}")
    d_wa = _mm_tn(sv["ya"], dpa, f"d_wa{tag}")
    d_wb = _mm_tn(sv["yb"], dpb, f"d_wb{tag}")
    da, dws, dbs, dlnw, dlnb = _branch_a_bwd(sv["proj_a"], dya, p["lnw"], p["lnb"], p["wsm"], p["wsm_t"], p["bsf"],
                                             f"branch_a_bwd{tag}")
    db, dcw, dcb, ddtb, dal, ddk, dsnw = _branch_b_bwd(sv["proj_b"], dyb, sv["y"], sv["states"], p["cw"], p["cb"], p["dtb"],
                                                       p["a_row"], p["dkc"], p["snw"], p["ind"], f"branch_b_bwd{tag}")
    dh = _mm(db, p["w_b_t"], f32, f"dh_b{tag}")
    dh = _mm(da, p["w_a_t"], f32, f"dh_a{tag}", c_in=dh)
    dh = _mm(dg, p["w_g_t"], f32, f"dh_g{tag}", c_in=dh)
    d_w_b = _mm_tn(sv["h"], db, f"d_w_b{tag}")
    d_w_a = _mm_tn(sv["h"], da, f"d_w_a{tag}")
    d_w_g = _mm_tn(sv["h"], dg, f"d_w_g{tag}")
    dx, dscale, dshift, dnw = _rmsmod_bwd(dh, sv["x"], dxo, p["norm_w"], sv["scale"], f"rmsmod_bwd{tag}")
    d_w_in = jnp.concatenate([d_w_a, d_w_b[:, CONVD:CONVD + DIN], _from_group_major(d_w_b[:, 0:CONVD]),
                              d_w_b[:, CONVD + DIN:CONVD + DIN + NH], d_w_g], axis=1)
    tril = jnp.tril(jnp.ones((CH, CH), bool))
    heads = lambda v: v[0, 0:NH]
    grads = dict(
        w_in=d_w_in, w_proj_a=d_wa, w_proj_b=d_wb, w_out=d_wo, conv_w=_from_group_major(dcw), conv_b=_from_group_major(dcb).reshape(CONVD),
        norm_w=dnw.reshape(D), gm_ln_w=dlnw.reshape(D), gm_ln_b=dlnb.reshape(D),
        gm_ws=jnp.where(tril[None], dws, 0.0), gm_bs=dbs.reshape(CH, NG, CH).sum(-1).T,
        dt_bias=heads(ddtb), a_log=heads(dal), d_skip=ddk.reshape(NH, HD).sum(-1), ssm_norm_w=dsnw.reshape(DIN),
        mod=jnp.concatenate([dshift, dscale, dgate], axis=1).reshape(3 * D),
    )
    return dx, grads


def _local_step(x, target, mods, layer_ops, fnw):
    saved = []
    for l in range(DEPTH):
        shift, scale, gate = mods[l]
        x, sv = _layer_fwd(x, shift, scale, gate, layer_ops[l], f"_l{l}")
        saved.append(sv)
    loss_parts, dx, dfnw = _final_loss(x, target, fnw.reshape(1, D), "final_loss")
    grads = [None] * DEPTH
    for l in reversed(range(DEPTH)):
        dx, grads[l] = _layer_bwd(dx, saved[l], layer_ops[l], f"_l{l}")
    return jnp.sum(loss_parts), dx, grads, dfnw.reshape(D)


ADA_COLS = 3 * D // NSHARD


def _ada_fwd(c_all, ada_w, ada_b_cols, name):
    def body(c_ref, w_ref, b_ref, o_ref):
        cv = c_ref[...]
        sc = cv * _sigmoid(cv)
        for l in range(DEPTH):
            o_ref[l] = _dot_hi(sc, w_ref[l]) + b_ref[l]

    return pl.pallas_call(body, out_shape=jax.ShapeDtypeStruct((DEPTH, 8, ADA_COLS), f32), name=name,
                          compiler_params=_cp(None))(c_all, ada_w, ada_b_cols)


def _adam_math(w, g, m, v):
    m = ADAM_B1 * m + (1.0 - ADAM_B1) * g
    v = ADAM_B2 * v + (1.0 - ADAM_B2) * (g * g)
    m_hat = m / (1.0 - ADAM_B1 ** ADAM_STEP)
    v_hat = v / (1.0 - ADAM_B2 ** ADAM_STEP)
    delta = -ADAM_LR * (m_hat / (jnp.sqrt(v_hat) + ADAM_EPS) + ADAM_WD * w)
    return delta, m, v


def _ada_bwd_adamw(c_all, dmod_cols, w, m, v, name):
    tr = 256

    def body(c_ref, dm_ref, w_ref, m_ref, v_ref, g_ref, d_ref, nm_ref, nv_ref):
        cv = c_ref[...]
        sc = cv * _sigmoid(cv)
        g = lax.dot_general(sc, dm_ref[0], (((0,), (0,)), ((), ())), precision=_HI, preferred_element_type=f32)
        g_ref[0] = g
        d_ref[0], nm_ref[0], nv_ref[0] = _adam_math(w_ref[0], g, m_ref[0], v_ref[0])

    blk = pl.BlockSpec((1, tr, ADA_COLS), lambda l, i: (l, i, 0))
    shp = jax.ShapeDtypeStruct((DEPTH, D, ADA_COLS), f32)
    return pl.pallas_call(
        body, grid=(DEPTH, D // tr),
        in_specs=[pl.BlockSpec((8, tr), lambda l, i: (0, i)), pl.BlockSpec((1, 8, ADA_COLS), lambda l, i: (l, 0, 0)), blk, blk, blk],
        out_specs=[blk, blk, blk, blk], out_shape=[shp, shp, shp, shp], name=name, compiler_params=_cp(("parallel", "parallel")),
    )(c_all, dmod_cols, w, m, v)


def _adamw(w, g, m, v, name):
    r, c = w.shape
    tr = _pick(r, (256, 128, 64, 32, 16, 8))

    def body(w_ref, g_ref, m_ref, v_ref, d_ref, nm_ref, nv_ref):
        d_ref[...], nm_ref[...], nv_ref[...] = _adam_math(w_ref[...], g_ref[...], m_ref[...], v_ref[...])

    blk = pl.BlockSpec((tr, c), lambda i: (i, 0))
    shp = jax.ShapeDtypeStruct((r, c), f32)
    return pl.pallas_call(body, grid=(r // tr,), in_specs=[blk] * 4, out_specs=[blk] * 3, out_shape=[shp] * 3, name=name,
                          compiler_params=_cp(("parallel",)))(w, g, m, v)


_MESH = pl.DeviceIdType.MESH
_AXES = ("x", "y", "c")
_HBM = pl.BlockSpec(memory_space=pltpu.HBM)


def _my_place():
    return tuple(lax.axis_index(a) for a in _AXES)


def _allreduce(buf, axes, name):
    r = buf.shape[0]
    n = len(axes)

    def body(x_ref, o_ref, rbuf, ssem, rsem):
        me = dict(zip(_AXES, _my_place()))
        o_ref[...] = x_ref[...]
        for k, ax in enumerate(axes):
            peer = tuple(1 - me[a] if a == ax else me[a] for a in _AXES)
            cp = pltpu.make_async_remote_copy(src_ref=o_ref, dst_ref=rbuf.at[k], send_sem=ssem.at[k], recv_sem=rsem.at[k],
                                              device_id=peer, device_id_type=_MESH)
            cp.start()
            cp.wait()
            o_ref[...] = o_ref[...] + rbuf[k]

    vm = pl.BlockSpec(memory_space=pltpu.VMEM)
    return pl.pallas_call(
        body, out_shape=jax.ShapeDtypeStruct((r, 128), f32), in_specs=[vm], out_specs=vm,
        scratch_shapes=[pltpu.VMEM((n, r, 128), f32), pltpu.SemaphoreType.DMA((n,)), pltpu.SemaphoreType.DMA((n,))],
        name=name, compiler_params=pltpu.CompilerParams(vmem_limit_bytes=VMEM_LIMIT),
    )(buf)


def _other_chips(x, y):
    return [(1 - x, y), (x, 1 - y), (1 - x, 1 - y)]


def _gather_weights(shards, name):
    na = len(shards)

    def body(*refs):
        ins, outs = refs[:na], refs[na:2 * na]
        ssem, rsem, lsem = refs[2 * na:]
        x, y, c = _my_place()
        k_me = 2 * x + y
        sibling = (x, y, 1 - c)
        chips = _other_chips(x, y)

        def rcopy(a, src, k, layer, to, idx):
            return pltpu.make_async_remote_copy(src_ref=src, dst_ref=outs[a].at[k, layer], send_sem=ssem.at[idx],
                                                recv_sem=rsem.at[idx], device_id=to, device_id_type=_MESH)

        local = [pltpu.make_async_copy(ins[a], outs[a].at[k_me], lsem.at[a]) for a in range(na)]
        for cp in local:
            cp.start()
        sent = []
        for j, chip in enumerate(chips):
            for a in range(na):
                cp = rcopy(a, ins[a].at[c], k_me, c, (*chip, c), j * na + a)
                cp.start()
                sent.append(cp)
        for j, chip in enumerate(chips):
            kj = 2 * chip[0] + chip[1]
            for a in range(na):
                rcopy(a, ins[a].at[c], kj, c, (*chip, c), j * na + a).wait_recv()
                cp = rcopy(a, outs[a].at[kj, c], kj, c, sibling, (3 + j) * na + a)
                cp.start()
                sent.append(cp)
        for j, chip in enumerate(chips):
            kj = 2 * chip[0] + chip[1]
            for a in range(na):
                rcopy(a, ins[a].at[c], kj, 1 - c, sibling, (3 + j) * na + a).wait_recv()
        for cp in sent:
            cp.wait_send()
        for cp in local:
            cp.wait()

    out_shape = [jax.ShapeDtypeStruct((NSHARD,) + s.shape, s.dtype) for s in shards]
    return pl.pallas_call(
        body, out_shape=out_shape, in_specs=[_HBM] * na, out_specs=[_HBM] * na,
        scratch_shapes=[pltpu.SemaphoreType.DMA((6 * na,)), pltpu.SemaphoreType.DMA((6 * na,)), pltpu.SemaphoreType.DMA((na,))],
        name=name,
    )(*shards)


def _swap_layers(parts, name):
    na = len(parts)

    def body(*refs):
        ins, outs = refs[:na], refs[na:2 * na]
        ssem, rsem = refs[2 * na:]
        x, y, c = _my_place()
        cps = [pltpu.make_async_remote_copy(src_ref=ins[a].at[1 - c], dst_ref=outs[a], send_sem=ssem.at[a], recv_sem=rsem.at[a],
                                            device_id=(x, y, 1 - c), device_id_type=_MESH) for a in range(na)]
        for cp in cps:
            cp.start()
        for cp in cps:
            cp.wait()

    return pl.pallas_call(
        body, out_shape=[jax.ShapeDtypeStruct(p.shape[1:], p.dtype) for p in parts], in_specs=[_HBM] * na, out_specs=[_HBM] * na,
        scratch_shapes=[pltpu.SemaphoreType.DMA((na,)), pltpu.SemaphoreType.DMA((na,))], name=name,
    )(*parts)


def _scatter_shards(sums, name):
    na = len(sums)

    def body(*refs):
        ins, outs = refs[:na], refs[na:2 * na]
        ssem, rsem = refs[2 * na:]
        x, y, c = _my_place()
        cps = []
        for j, chip in enumerate(_other_chips(x, y)):
            kj = 2 * chip[0] + chip[1]
            for a in range(na):
                cps.append(pltpu.make_async_remote_copy(
                    src_ref=ins[a].at[kj], dst_ref=outs[a].at[j], send_sem=ssem.at[j * na + a], recv_sem=rsem.at[j * na + a],
                    device_id=(*chip, c), device_id_type=_MESH))
        for cp in cps:
            cp.start()
        for cp in cps:
            cp.wait()

    return pl.pallas_call(
        body, out_shape=[jax.ShapeDtypeStruct((3,) + p.shape[1:], p.dtype) for p in sums], in_specs=[_HBM] * na,
        out_specs=[_HBM] * na, scratch_shapes=[pltpu.SemaphoreType.DMA((3 * na,)), pltpu.SemaphoreType.DMA((3 * na,))], name=name,
    )(*sums)


def _share_layers(finals, name):
    na = len(finals)

    def body(*refs):
        ins, outs = refs[:na], refs[na:2 * na]
        ssem, rsem, lsem = refs[2 * na:]
        x, y, c = _my_place()
        local = [pltpu.make_async_copy(ins[a], outs[a].at[c], lsem.at[a]) for a in range(na)]
        cps = [pltpu.make_async_remote_copy(src_ref=ins[a], dst_ref=outs[a].at[c], send_sem=ssem.at[a], recv_sem=rsem.at[a],
                                            device_id=(x, y, 1 - c), device_id_type=_MESH) for a in range(na)]
        for cp in local + cps:
            cp.start()
        for cp in cps:
            cp.wait()
        for cp in local:
            cp.wait()

    return pl.pallas_call(
        body, out_shape=[jax.ShapeDtypeStruct((DEPTH,) + p.shape, p.dtype) for p in finals], in_specs=[_HBM] * na,
        out_specs=[_HBM] * na,
        scratch_shapes=[pltpu.SemaphoreType.DMA((na,)), pltpu.SemaphoreType.DMA((na,)), pltpu.SemaphoreType.DMA((na,))], name=name,
    )(*finals)


def _add_own_layer(parts, recv, c_idx, name):
    _, ns, r, c = parts.shape
    tr = _pick(r, (256, 128, 64, 32, 16, 8, 4))

    def body(ci_ref, p_ref, r_ref, o_ref, ob_ref):
        t = p_ref[0] + r_ref[...]
        o_ref[...] = t
        ob_ref[...] = t.astype(bf16)

    blk = pl.BlockSpec((1, tr, c), lambda k, i, ci: (k, i, 0))
    return pl.pallas_call(
        body,
        grid_spec=pltpu.PrefetchScalarGridSpec(
            num_scalar_prefetch=1, grid=(ns, r // tr),
            in_specs=[pl.BlockSpec((1, 1, tr, c), lambda k, i, ci: (ci[0], k, i, 0)), blk], out_specs=[blk, blk]),
        out_shape=[jax.ShapeDtypeStruct((ns, r, c), f32), jax.ShapeDtypeStruct((ns, r, c), bf16)], name=name,
        compiler_params=_cp(("parallel", "parallel")),
    )(c_idx, parts, recv)


def _add_own_shard(sums, recv, k_idx, name):
    _, r, c = sums.shape
    tr = _pick(r, (256, 128, 64, 32, 16, 8, 4))

    def body(ki_ref, s_ref, r_ref, o_ref):
        o_ref[...] = ((s_ref[0] + r_ref[0].astype(f32)) + r_ref[1].astype(f32)) + r_ref[2].astype(f32)

    return pl.pallas_call(
        body,
        grid_spec=pltpu.PrefetchScalarGridSpec(
            num_scalar_prefetch=1, grid=(r // tr,),
            in_specs=[pl.BlockSpec((1, tr, c), lambda i, ki: (ki[0], i, 0)), pl.BlockSpec((3, tr, c), lambda i, ki: (0, i, 0))],
            out_specs=pl.BlockSpec((tr, c), lambda i, ki: (i, 0))),
        out_shape=jax.ShapeDtypeStruct((r, c), f32), name=name, compiler_params=_cp(("parallel",)),
    )(k_idx, sums, recv)


def _reduce_scatter(parts, tag):
    x, y, c = _my_place()
    c_idx = jnp.reshape(c, (1,)).astype(jnp.int32)
    k_idx = jnp.reshape(2 * x + y, (1,)).astype(jnp.int32)
    na = len(parts)
    recv = _swap_layers(parts, f"rs_swap{tag}")
    sums = [_add_own_layer(parts[a], recv[a], c_idx, f"rs_add_layer{tag}_{a}") for a in range(na)]
    recv = _scatter_shards([sb for _, sb in sums], f"rs_scatter{tag}")
    finals = [_add_own_shard(sums[a][0], recv[a], k_idx, f"rs_add_shard{tag}_{a}") for a in range(na)]
    return _share_layers(finals, f"rs_share{tag}")


_SMALL = [("ada_b", (DEPTH, 3 * D)), ("norm_w", (DEPTH, D)), ("gm_ln_w", (DEPTH, D)), ("gm_ln_b", (DEPTH, D)),
          ("gm_ws", (DEPTH, NG, CH, CH)), ("gm_bs", (DEPTH, NG, CH)), ("conv_b", (DEPTH, CONVD)), ("dt_bias", (DEPTH, NH)),
          ("a_log", (DEPTH, NH)), ("d_skip", (DEPTH, NH)), ("ssm_norm_w", (DEPTH, DIN)), ("final_norm_w", (D,))]


def _rows_of(shape):
    n = 1
    for d in shape:
        n *= d
    return -(-n // 1024) * 8


def _pack(arrays):
    rows = []
    for a in arrays:
        flat = a.reshape(-1)
        r = _rows_of(a.shape)
        rows.append(jnp.pad(flat, (0, r * 128 - flat.shape[0])).reshape(r, 128))
    return jnp.concatenate(rows, axis=0)


def _unpack(buf, shapes):
    out, at = [], 0
    for shp in shapes:
        r = _rows_of(shp)
        n = 1
        for d in shp:
            n *= d
        out.append(buf[at:at + r].reshape(-1)[:n].reshape(shp))
        at += r
    return out


def kernel(x, c, ada_w, ada_b, norm_w, w_in, gm_ln_w, gm_ln_b, gm_ws, gm_bs, conv_w, conv_b, dt_bias, a_log, d_skip, ssm_norm_w, w_proj_a, w_proj_b, w_out, final_norm_w, loss_target, m_ada_w, m_ada_b, m_norm_w, m_w_in, m_gm_ln_w, m_gm_ln_b, m_gm_ws, m_gm_bs, m_conv_w, m_conv_b, m_dt_bias, m_a_log, m_d_skip, m_ssm_norm_w, m_w_proj_a, m_w_proj_b, m_w_out, m_final_norm_w, v_ada_w, v_ada_b, v_norm_w, v_w_in, v_gm_ln_w, v_gm_ln_b, v_gm_ws, v_gm_bs, v_conv_w, v_conv_b, v_dt_bias, v_a_log, v_d_skip, v_ssm_norm_w, v_w_proj_a, v_w_proj_b, v_w_out, v_final_norm_w):
    xi, yi, ci = _my_place()
    k_me = 2 * xi + yi
    b_me = 4 * xi + 2 * yi + ci
    w = dict(ada_b=ada_b, norm_w=norm_w, gm_ln_w=gm_ln_w, gm_ln_b=gm_ln_b, gm_ws=gm_ws, gm_bs=gm_bs, conv_b=conv_b, dt_bias=dt_bias,
             a_log=a_log, d_skip=d_skip, ssm_norm_w=ssm_norm_w, final_norm_w=final_norm_w)
    m = dict(ada_b=m_ada_b, norm_w=m_norm_w, gm_ln_w=m_gm_ln_w, gm_ln_b=m_gm_ln_b, gm_ws=m_gm_ws, gm_bs=m_gm_bs, conv_b=m_conv_b,
             dt_bias=m_dt_bias, a_log=m_a_log, d_skip=m_d_skip, ssm_norm_w=m_ssm_norm_w, final_norm_w=m_final_norm_w)
    v = dict(ada_b=v_ada_b, norm_w=v_norm_w, gm_ln_w=v_gm_ln_w, gm_ln_b=v_gm_ln_b, gm_ws=v_gm_ws, gm_bs=v_gm_bs, conv_b=v_conv_b,
             dt_bias=v_dt_bias, a_log=v_a_log, d_skip=v_d_skip, ssm_norm_w=v_ssm_norm_w, final_norm_w=v_final_norm_w)

    c_slot = lax.dynamic_update_slice(jnp.zeros((8, D), f32), c, (b_me, 0))
    c_all = _allreduce(c_slot.reshape(64, 128), _AXES, "gather_c").reshape(8, D)
    ada_b_cols = lax.dynamic_slice(ada_b, (0, k_me * ADA_COLS), (DEPTH, ADA_COLS)).reshape(DEPTH, 1, ADA_COLS)
    mod_cols = _ada_fwd(c_all, ada_w, ada_b_cols, "ada_fwd")
    mod_slot = lax.dynamic_update_slice(jnp.zeros((DEPTH, 8, 3 * D), f32), mod_cols, (0, 0, k_me * ADA_COLS))
    mod_all = _allreduce(mod_slot.reshape(-1, 128), ("x", "y"), "gather_mod").reshape(DEPTH, 8, 3 * D)
    mod_me = lax.dynamic_slice(mod_all, (0, b_me, 0), (DEPTH, 1, 3 * D))
    mods = [(mod_me[l, :, 0:D], mod_me[l, :, D:2 * D], mod_me[l, :, 2 * D:3 * D]) for l in range(DEPTH)]

    rows_sh = jnp.concatenate([w_proj_a, w_proj_b, w_out], axis=1).astype(bf16)
    win_all, rows_all, conv_all = _gather_weights([w_in.astype(bf16), rows_sh, conv_w], "gather_weights")
    ops = []
    for l in range(DEPTH):
        w_in_l = jnp.concatenate([win_all[k, l] for k in range(NSHARD)], axis=1)
        wa_l = jnp.concatenate([rows_all[k, l, 0:256] for k in range(NSHARD)], axis=0)
        wb_l = jnp.concatenate([rows_all[k, l, 256:768] for k in range(NSHARD)], axis=0)
        wo_l = jnp.concatenate([rows_all[k, l, 768:1024] for k in range(NSHARD)], axis=0)
        cw_l = jnp.concatenate([conv_all[k, l] for k in range(NSHARD)], axis=1)
        ops.append(_layer_operands(w_in_l, cw_l, wa_l, wb_l, wo_l, norm_w[l], gm_ln_w[l], gm_ln_b[l], gm_ws[l], gm_bs[l], conv_b[l],
                                   dt_bias[l], a_log[l], d_skip[l], ssm_norm_w[l]))

    loss_me, dx, grads, dfnw = _local_step(x[0], loss_target[0], mods, ops, final_norm_w)
    loss = lax.psum(loss_me, _AXES)

    s_in = (N_IN // NSHARD)
    g_in = jnp.stack([grads[l]["w_in"].reshape(D, NSHARD, s_in).transpose(1, 0, 2) for l in range(DEPTH)])
    g_rows = jnp.stack([jnp.concatenate([grads[l]["w_proj_a"].reshape(NSHARD, 256, D), grads[l]["w_proj_b"].reshape(NSHARD, 512, D),
                                         grads[l]["w_out"].reshape(NSHARD, 256, D)], axis=1) for l in range(DEPTH)])
    g_conv = jnp.stack([grads[l]["conv_w"].reshape(4, NSHARD, D).transpose(1, 0, 2) for l in range(DEPTH)])
    gr_in, gr_rows, gr_conv = _reduce_scatter([g_in, g_rows, g_conv], "")

    def upd2(wv, gv, mv, vv, name):
        shp = wv.shape
        flat = lambda t: t.reshape(-1, shp[-1])
        d_, m_, v_ = _adamw(flat(wv), flat(gv), flat(mv), flat(vv), name)
        return d_.reshape(shp), m_.reshape(shp), v_.reshape(shp)

    d_in, nm_in, nv_in = upd2(w_in, gr_in, m_w_in, v_w_in, "adamw_w_in")
    cat = lambda a, b, c_: jnp.concatenate([a, b, c_], axis=1)
    d_rows, nm_rows, nv_rows = upd2(cat(w_proj_a, w_proj_b, w_out), gr_rows, cat(m_w_proj_a, m_w_proj_b, m_w_out),
                                    cat(v_w_proj_a, v_w_proj_b, v_w_out), "adamw_rows")
    split = lambda t: (t[:, 0:256], t[:, 256:768], t[:, 768:1024])

    dmod_slot = lax.dynamic_update_slice(jnp.zeros((DEPTH, 8, 3 * D), f32),
                                         jnp.stack([grads[l]["mod"] for l in range(DEPTH)]).reshape(DEPTH, 1, 3 * D), (0, b_me, 0))
    small_g = {n: (dfnw if n == "final_norm_w" else jnp.stack([grads[l]["mod" if n == "ada_b" else n] for l in range(DEPTH)]))
               for n, _ in _SMALL}
    packed = _allreduce(_pack([small_g[n] for n, _ in _SMALL] + [dmod_slot]), _AXES, "allreduce_small")
    n_small = sum(_rows_of(s) for _, s in _SMALL)
    g_small = packed[0:n_small]
    dmod_all = packed[n_small:].reshape(-1)[:DEPTH * 8 * 3 * D].reshape(DEPTH, 8, 3 * D)
    small_gw = jnp.concatenate([g_small, _pack([gr_conv])], axis=0)
    d_s, nm_s, nv_s = _adamw(_pack([w[n] for n, _ in _SMALL] + [conv_w]), small_gw,
                             _pack([m[n] for n, _ in _SMALL] + [m_conv_w]), _pack([v[n] for n, _ in _SMALL] + [v_conv_w]), "adamw_small")
    shapes = [s for _, s in _SMALL] + [conv_w.shape]
    names = [n for n, _ in _SMALL] + ["conv_w"]
    g_d = dict(zip(names, _unpack(small_gw, shapes)))
    d_d = dict(zip(names, _unpack(d_s, shapes)))
    nm_d = dict(zip(names, _unpack(nm_s, shapes)))
    nv_d = dict(zip(names, _unpack(nv_s, shapes)))

    dmod_cols = lax.dynamic_slice(dmod_all, (0, 0, k_me * ADA_COLS), (DEPTH, 8, ADA_COLS))
    g_ada, d_ada, nm_ada, nv_ada = _ada_bwd_adamw(c_all, dmod_cols, ada_w, m_ada_w, v_ada_w, "ada_bwd_adamw")

    def by_name(big, small):
        ga, gb, go = split(big[1])
        return dict(small, ada_w=big[2], w_in=big[0], w_proj_a=ga, w_proj_b=gb, w_out=go)

    order = ["ada_w", "ada_b", "norm_w", "w_in", "gm_ln_w", "gm_ln_b", "gm_ws", "gm_bs", "conv_w", "conv_b", "dt_bias", "a_log",
             "d_skip", "ssm_norm_w", "w_proj_a", "w_proj_b", "w_out", "final_norm_w"]
    outs = []
    for big, small in (((gr_in, gr_rows, g_ada), g_d), ((d_in, d_rows, d_ada), d_d), ((nm_in, nm_rows, nm_ada), nm_d),
                       ((nv_in, nv_rows, nv_ada), nv_d)):
        t = by_name(big, small)
        outs += [t[n] for n in order]
    return (loss, dx.reshape(1, -1, D), *outs)
```

```python
import jax
import jax.numpy as jnp
from jax import lax
from jax.experimental import pallas as pl
from jax.experimental.pallas import tpu as pltpu

f32 = jnp.float32
bf16 = jnp.bfloat16

D = 1024
DEPTH = 2
EPS = 1e-6
CH = 128
NG = 8
HPG = 4
HD = 64
NH = NG * HPG
NST = 128
DIN = 2048
CONVD = 4096
GW = DIN // NG
PB = CONVD + DIN + 256
PA = 3 * D
PG = 2 * D
N_IN = 11296
NSHARD = 4
V7X_VMEM_BYTES = 64 * 2 ** 20
VMEM_LIMIT = V7X_VMEM_BYTES - 8 * 2 ** 20

ADAM_LR, ADAM_B1, ADAM_B2, ADAM_EPS, ADAM_WD, ADAM_STEP = 0.001, 0.9, 0.999, 1e-08, 0.01, 10

_HI = lax.Precision.HIGHEST


def _cp(sem):
    return pltpu.CompilerParams(dimension_semantics=sem, vmem_limit_bytes=VMEM_LIMIT)


def _sigmoid(x):
    return 0.5 * jnp.tanh(0.5 * x) + 0.5


def _silu_and_grad(x):
    s = _sigmoid(x)
    return x * s, s * (1.0 + x * (1.0 - s))


_GELU_K = 0.7978845608028654
_GELU_C = 0.044715


def _gelu_and_grad(x):
    x2 = x * x
    t = jnp.tanh(_GELU_K * (x + _GELU_C * x * x2))
    g = 0.5 * x * (1.0 + t)
    dg = 0.5 * (1.0 + t) + 0.5 * x * (1.0 - t * t) * _GELU_K * (1.0 + 3.0 * _GELU_C * x2)
    return g, dg


def _gelu(x):
    t = jnp.tanh(_GELU_K * (x + _GELU_C * x * x * x))
    return 0.5 * x * (1.0 + t)


def _softplus(x):
    return jnp.maximum(x, 0.0) + jnp.log(1.0 + jnp.exp(-jnp.abs(x)))


def _dot(a, b):
    return jnp.dot(a, b, preferred_element_type=f32)


def _dot_nt(a, b):
    return lax.dot_general(a, b, (((1,), (1,)), ((), ())), preferred_element_type=f32)


def _dot_tn(a, b):
    return lax.dot_general(a, b, (((0,), (0,)), ((), ())), preferred_element_type=f32)


def _dot_hi(a, b):
    return jnp.dot(a, b, precision=_HI, preferred_element_type=f32)


def _rmsmod_fwd(x, nw, scale, shift, name):
    s = x.shape[0]
    ts = min(512, s)

    def body(x_ref, nw_ref, sc_ref, sh_ref, h_ref):
        xv = x_ref[...]
        r = lax.rsqrt(jnp.mean(xv * xv, axis=-1, keepdims=True) + EPS)
        h_ref[...] = ((xv * r) * nw_ref[...] * (1.0 + sc_ref[...]) + sh_ref[...]).astype(bf16)

    row = pl.BlockSpec((1, D), lambda i: (0, 0))
    tile = pl.BlockSpec((ts, D), lambda i: (i, 0))
    return pl.pallas_call(
        body, grid=(s // ts,), in_specs=[tile, row, row, row], out_specs=tile,
        out_shape=jax.ShapeDtypeStruct((s, D), bf16), name=name, compiler_params=_cp(("parallel",)),
    )(x, nw, scale, shift)


def _rmsmod_bwd(dh, x, dres, nw, scale, name):
    s = x.shape[0]
    ts = min(512, s)

    def body(dh_ref, x_ref, dres_ref, nw_ref, sc_ref, dx_ref, dsc_ref, dsh_ref, dnw_ref):
        @pl.when(pl.program_id(0) == 0)
        def _():
            dsc_ref[...] = jnp.zeros_like(dsc_ref)
            dsh_ref[...] = jnp.zeros_like(dsh_ref)
            dnw_ref[...] = jnp.zeros_like(dnw_ref)
        xv = x_ref[...]
        dhv = dh_ref[...]
        r = lax.rsqrt(jnp.mean(xv * xv, axis=-1, keepdims=True) + EPS)
        xn = xv * r
        one_sc = 1.0 + sc_ref[...]
        dsc_ref[...] += jnp.sum(dhv * xn * nw_ref[...], axis=0, keepdims=True)
        dsh_ref[...] += jnp.sum(dhv, axis=0, keepdims=True)
        dnw_ref[...] += jnp.sum(dhv * xn * one_sc, axis=0, keepdims=True)
        dxn = dhv * (nw_ref[...] * one_sc)
        dx_ref[...] = r * (dxn - xn * jnp.mean(dxn * xn, axis=-1, keepdims=True)) + dres_ref[...]

    row = pl.BlockSpec((1, D), lambda i: (0, 0))
    tile = pl.BlockSpec((ts, D), lambda i: (i, 0))
    vec = jax.ShapeDtypeStruct((1, D), f32)
    return pl.pallas_call(
        body, grid=(s // ts,), in_specs=[tile, tile, tile, row, row], out_specs=[tile, row, row, row],
        out_shape=[jax.ShapeDtypeStruct((s, D), f32), vec, vec, vec], name=name, compiler_params=_cp(("arbitrary",)),
    )(dh, x, dres, nw, scale)


def _pick(n, prefs):
    for p in prefs:
        if n % p == 0:
            return p
    return n


def _mm(a, b, out_dtype, name, c_in=None):
    m, k = a.shape
    n = b.shape[1]
    tm = _pick(m, (1024, 512, 256))
    tn = _pick(n, (1280, 1024, 512))
    tk = _pick(k, (1280, 1024, 512))
    nk = k // tk

    def body(*refs):
        if c_in is not None:
            a_ref, b_ref, c_ref, o_ref, acc = refs
        else:
            a_ref, b_ref, o_ref, acc = refs
        kk = pl.program_id(2)

        @pl.when(kk == 0)
        def _():
            if c_in is not None:
                acc[...] = c_ref[...]
            else:
                acc[...] = jnp.zeros_like(acc)
        acc[...] += _dot(a_ref[...], b_ref[...])

        @pl.when(kk == nk - 1)
        def _():
            o_ref[...] = acc[...].astype(out_dtype)

    in_specs = [pl.BlockSpec((tm, tk), lambda j, i, kk: (i, kk)), pl.BlockSpec((tk, tn), lambda j, i, kk: (kk, j))]
    args = [a, b]
    if c_in is not None:
        in_specs.append(pl.BlockSpec((tm, tn), lambda j, i, kk: (i, j)))
        args.append(c_in)
    return pl.pallas_call(
        body, grid=(n // tn, m // tm, nk), in_specs=in_specs, out_specs=pl.BlockSpec((tm, tn), lambda j, i, kk: (i, j)),
        out_shape=jax.ShapeDtypeStruct((m, n), out_dtype), scratch_shapes=[pltpu.VMEM((tm, tn), f32)], name=name,
        compiler_params=_cp(("parallel", "parallel", "arbitrary")),
    )(*args)


def _mm_tn(a, b, name):
    t, k1 = a.shape
    n = b.shape[1]
    t1 = _pick(k1, (1024, 512))
    tn = _pick(n, (1280, 1024, 512))
    tt = _pick(t, (512, 256))
    nt = t // tt

    def body(a_ref, b_ref, o_ref):
        tt_i = pl.program_id(2)

        @pl.when(tt_i == 0)
        def _():
            o_ref[...] = jnp.zeros_like(o_ref)
        o_ref[...] += _dot_tn(a_ref[...], b_ref[...])

    return pl.pallas_call(
        body, grid=(k1 // t1, n // tn, nt),
        in_specs=[pl.BlockSpec((tt, t1), lambda i, j, tt_i: (tt_i, i)), pl.BlockSpec((tt, tn), lambda i, j, tt_i: (tt_i, j))],
        out_specs=pl.BlockSpec((t1, tn), lambda i, j, tt_i: (i, j)),
        out_shape=jax.ShapeDtypeStruct((k1, n), f32), name=name,
        compiler_params=_cp(("parallel", "parallel", "arbitrary")),
    )(a, b)


def _ln_stats(v):
    mu = jnp.mean(v, axis=-1, keepdims=True)
    vc = v - mu
    rstd = lax.rsqrt(jnp.mean(vc * vc, axis=-1, keepdims=True) + EPS)
    return vc * rstd, rstd


def _mix(w_ref, vl):
    return jnp.concatenate([_dot(w_ref[g], vl[:, g * CH:(g + 1) * CH]) for g in range(NG)], axis=1)


def _branch_a_fwd(proj_a, lnw, lnb, wsm, bsf, name):
    s = proj_a.shape[0]
    ta = min(256, s)

    def body(pu_ref, pv_ref, pz_ref, lnw_ref, lnb_ref, w_ref, bs_ref, ya_ref):
        for c in range(ta // CH):
            rows = pl.ds(c * CH, CH)
            vh, _ = _ln_stats(_gelu(pv_ref[rows, :]))
            vl = (vh * lnw_ref[...] + lnb_ref[...]).astype(bf16)
            mixed = _mix(w_ref, vl) + bs_ref[...]
            pz = pz_ref[rows, :]
            ya_ref[rows, :] = (_gelu(pu_ref[rows, :]) * mixed * (pz * _sigmoid(pz))).astype(bf16)

    row = pl.BlockSpec((1, D), lambda i: (0, 0))
    return pl.pallas_call(
        body, grid=(s // ta,),
        in_specs=[pl.BlockSpec((ta, D), lambda i: (i, 0)), pl.BlockSpec((ta, D), lambda i: (i, 1)),
                  pl.BlockSpec((ta, D), lambda i: (i, 2)), row, row,
                  pl.BlockSpec((NG, CH, CH), lambda i: (0, 0, 0)), pl.BlockSpec((CH, D), lambda i: (0, 0))],
        out_specs=pl.BlockSpec((ta, D), lambda i: (i, 0)),
        out_shape=jax.ShapeDtypeStruct((s, D), bf16), name=name, compiler_params=_cp(("parallel",)),
    )(proj_a, proj_a, proj_a, lnw, lnb, wsm, bsf)


def _branch_a_bwd(proj_a, dya, lnw, lnb, wsm, wsm_t, bsf, name):
    s = proj_a.shape[0]
    ta = min(256, s)

    def body(pu_ref, pv_ref, pz_ref, dya_ref, lnw_ref, lnb_ref, w_ref, wt_ref, bs_ref,
             dp_ref, dws_ref, dbs_ref, dlnw_ref, dlnb_ref):
        @pl.when(pl.program_id(0) == 0)
        def _():
            dws_ref[...] = jnp.zeros_like(dws_ref)
            dbs_ref[...] = jnp.zeros_like(dbs_ref)
            dlnw_ref[...] = jnp.zeros_like(dlnw_ref)
            dlnb_ref[...] = jnp.zeros_like(dlnb_ref)
        for c in range(ta // CH):
            rows = pl.ds(c * CH, CH)
            u, du = _gelu_and_grad(pu_ref[rows, :])
            v, dv_act = _gelu_and_grad(pv_ref[rows, :])
            zg, dzg = _silu_and_grad(pz_ref[rows, :])
            vh, rstd = _ln_stats(v)
            vl = (vh * lnw_ref[...] + lnb_ref[...]).astype(bf16)
            mixed = _mix(w_ref, vl) + bs_ref[...]
            dy = dya_ref[rows, :].astype(f32)
            dmixed = dy * u * zg
            dp_ref[rows, 0:D] = (dy * mixed * zg * du).astype(bf16)
            dp_ref[rows, 2 * D:3 * D] = (dy * u * mixed * dzg).astype(bf16)
            dmb = dmixed.astype(bf16)
            dbs_ref[...] += dmixed
            dvl = _mix(wt_ref, dmb)
            for g in range(NG):
                cols = slice(g * CH, (g + 1) * CH)
                dws_ref[g] += _dot_nt(dmb[:, cols], vl[:, cols])
            dlnw_ref[...] += jnp.sum(dvl * vh, axis=0, keepdims=True)
            dlnb_ref[...] += jnp.sum(dvl, axis=0, keepdims=True)
            dvh = dvl * lnw_ref[...]
            dv = rstd * (dvh - jnp.mean(dvh, axis=-1, keepdims=True) - vh * jnp.mean(dvh * vh, axis=-1, keepdims=True))
            dp_ref[rows, D:2 * D] = (dv * dv_act).astype(bf16)

    row = pl.BlockSpec((1, D), lambda i: (0, 0))
    wspec = pl.BlockSpec((NG, CH, CH), lambda i: (0, 0, 0))
    full = pl.BlockSpec((CH, D), lambda i: (0, 0))
    return pl.pallas_call(
        body, grid=(s // ta,),
        in_specs=[pl.BlockSpec((ta, D), lambda i: (i, 0)), pl.BlockSpec((ta, D), lambda i: (i, 1)),
                  pl.BlockSpec((ta, D), lambda i: (i, 2)), pl.BlockSpec((ta, D), lambda i: (i, 0)),
                  row, row, wspec, wspec, full],
        out_specs=[pl.BlockSpec((ta, PA), lambda i: (i, 0)), wspec, full, row, row],
        out_shape=[jax.ShapeDtypeStruct((s, PA), bf16), jax.ShapeDtypeStruct((NG, CH, CH), f32),
                   jax.ShapeDtypeStruct((CH, D), f32), jax.ShapeDtypeStruct((1, D), f32), jax.ShapeDtypeStruct((1, D), f32)],
        name=name, compiler_params=_cp(("arbitrary",)),
    )(proj_a, proj_a, proj_a, dya, lnw, lnb, wsm, wsm_t, bsf)


GB = GW + 2 * NST


def _group_major(xs, b, c):
    lead = xs.shape[:-1]
    return jnp.concatenate([xs.reshape(lead + (NG, GW)), b.reshape(lead + (NG, NST)), c.reshape(lead + (NG, NST))],
                           axis=-1).reshape(lead + (CONVD,))


def _from_group_major(t):
    lead = t.shape[:-1]
    t = t.reshape(lead + (NG, GB))
    return jnp.concatenate([t[..., 0:GW].reshape(lead + (DIN,)), t[..., GW:GW + NST].reshape(lead + (NG * NST,)),
                            t[..., GW + NST:GB].reshape(lead + (NG * NST,))], axis=-1)


def _shift_rows(x, prev8, j):
    xr = pltpu.roll(x, j, 0)
    fix = pltpu.roll(prev8, j, 0)
    rid = lax.broadcasted_iota(jnp.int32, (8, x.shape[1]), 0)
    top = jnp.where(rid < j, fix, xr[0:8])
    return jnp.concatenate([top, xr[8:]], axis=0)


def _shift_rows_up(d, next8, j):
    dr = pltpu.roll(d, CH - j, 0)
    fix = pltpu.roll(next8, 8 - j, 0)
    rid = lax.broadcasted_iota(jnp.int32, (8, d.shape[1]), 0)
    bot = jnp.where(rid >= 8 - j, fix, dr[CH - 8:CH])
    return jnp.concatenate([dr[0:CH - 8], bot], axis=0)


def _conv_pre(x, prev8, cw_ref, cb_ref, cols):
    shifted = [_shift_rows(x, prev8, j) for j in (1, 2, 3)]
    conv = cb_ref[:, cols] + cw_ref[3:4, cols] * x
    for j in (1, 2, 3):
        conv = conv + cw_ref[3 - j:4 - j, cols] * shifted[j - 1]
    return conv, shifted


def _tril_mask():
    return lax.broadcasted_iota(jnp.int32, (CH, CH), 0) >= lax.broadcasted_iota(jnp.int32, (CH, CH), 1)


def _sum_all(v):
    return jnp.sum(jnp.sum(v, axis=0, keepdims=True), axis=1, keepdims=True)


def _lanes(g, width, base=0):
    return pl.ds(pl.multiple_of(base + g * width, width), width)


def _branch_b_fwd(proj_b, cw, cb, dtb, a_row, dkc, snw, name):
    s = proj_b.shape[0]
    nc = s // CH

    def body(xbc_ref, sz_ref, dtr_ref, cw_ref, cb_ref, dtb_ref, a_ref, dkc_ref, snw_ref,
             yb_ref, y_ref, st_ref, prev8, state, acst_s):
        @pl.when(pl.program_id(0) == 0)
        def _():
            prev8[...] = jnp.zeros_like(prev8)
            state[...] = jnp.zeros_like(state)
        st_ref[0] = state[...].astype(bf16)
        mask = _tril_mask()
        dt_all = _softplus(dtr_ref[:, 0:CH] + dtb_ref[...])
        acs_all = _dot_hi(mask.astype(f32), dt_all * a_ref[...])
        acst_s[...] = acs_all.T

        def group(g, carry):
            cols = _lanes(g, GB)
            gcols = _lanes(g, GW)
            x = xbc_ref[:, cols]
            conv, _ = _conv_pre(x, prev8[:, cols], cw_ref, cb_ref, cols)
            prev8[:, cols] = x[CH - 8:CH]
            xc = conv * _sigmoid(conv)
            xs = xc[:, 0:GW]
            bg = xc[:, GW:GW + NST].astype(bf16)
            cg = xc[:, GW + NST:GB].astype(bf16)
            back = lax.rem(CH - HPG * g, CH)
            dt = pltpu.roll(dt_all, back, 1)
            acs = pltpu.roll(acs_all, back, 1)
            cbm = _dot_nt(cg, bg)
            dkc_g = dkc_ref[:, gcols]
            y_parts = []
            for r in range(HPG):
                colb = jnp.broadcast_to(acs[:, r:r + 1], (CH, CH))
                row = acst_s[pl.ds(g * HPG + r, 1), :]
                lmat = jnp.exp(jnp.where(mask, colb - row, -jnp.inf))
                xr = xs[:, r * HD:(r + 1) * HD]
                xd = xr * dt[:, r:r + 1]
                sp = state[g * HPG + r]
                col = colb[:, 0:HD]
                alast = colb[CH - 1:CH, 0:HD]
                y_r = _dot((cbm * lmat).astype(bf16), xd.astype(bf16))
                y_r = y_r + jnp.exp(col) * _dot_nt(cg, sp.astype(bf16))
                y_parts.append(y_r + xr * dkc_g[:, r * HD:(r + 1) * HD])
                cs = _dot_tn((xd * jnp.exp(alast - col)).astype(bf16), bg)
                state[g * HPG + r] = jnp.exp(colb[CH - 1:CH, :]) * sp + cs
            y = jnp.concatenate(y_parts, axis=1)
            szv = sz_ref[:, gcols]
            yz = y * (szv * _sigmoid(szv))
            rr = lax.rsqrt(jnp.mean(yz * yz, axis=-1, keepdims=True) + EPS)
            yb_ref[:, gcols] = (yz * rr * snw_ref[:, gcols]).astype(bf16)
            y_ref[:, gcols] = y.astype(bf16)
            return carry

        lax.fori_loop(0, NG, group, 0)

    const2 = lambda c: (0, 0)
    return pl.pallas_call(
        body, grid=(nc,),
        in_specs=[pl.BlockSpec((CH, CONVD), lambda c: (c, 0)), pl.BlockSpec((CH, DIN), lambda c: (c, CONVD // DIN)),
                  pl.BlockSpec((CH, 256), lambda c: (c, (CONVD + DIN) // 256)),
                  pl.BlockSpec((4, CONVD), const2), pl.BlockSpec((1, CONVD), const2),
                  pl.BlockSpec((1, CH), const2), pl.BlockSpec((1, CH), const2),
                  pl.BlockSpec((1, DIN), const2), pl.BlockSpec((1, DIN), const2)],
        out_specs=[pl.BlockSpec((CH, DIN), lambda c: (c, 0)), pl.BlockSpec((CH, DIN), lambda c: (c, 0)),
                   pl.BlockSpec((1, NH, HD, NST), lambda c: (c, 0, 0, 0))],
        out_shape=[jax.ShapeDtypeStruct((s, DIN), bf16), jax.ShapeDtypeStruct((s, DIN), bf16),
                   jax.ShapeDtypeStruct((nc, NH, HD, NST), bf16)],
        scratch_shapes=[pltpu.VMEM((8, CONVD), f32), pltpu.VMEM((NH, HD, NST), f32), pltpu.VMEM((CH, CH), f32)],
        name=name, compiler_params=_cp(("arbitrary",)),
    )(proj_b, proj_b, proj_b, cw, cb, dtb, a_row, dkc, snw)


def _branch_b_bwd(proj_b, dyb, y_sv, states, cw, cb, dtb, a_row, dkc, snw, ind, name):
    s = proj_b.shape[0]
    nc = s // CH

    def body(xbc_ref, xprev_ref, sz_ref, dtr_ref, dyb_ref, y_ref, st_ref, cw_ref, cb_ref, dtb_ref, a_ref, dkc_ref,
             snw_ref, ind_ref, dp_ref, dcw_ref, dcb_ref, ddtb_ref, dal_ref, ddk_ref, dsnw_ref,
             dstate, dnext8, acst_s, dacs_acc, q2_acc):
        step = pl.program_id(0)
        cc = nc - 1 - step

        @pl.when(step == 0)
        def _():
            dstate[...] = jnp.zeros_like(dstate)
            dnext8[...] = jnp.zeros_like(dnext8)
            dcw_ref[...] = jnp.zeros_like(dcw_ref)
            dcb_ref[...] = jnp.zeros_like(dcb_ref)
            ddtb_ref[...] = jnp.zeros_like(ddtb_ref)
            dal_ref[...] = jnp.zeros_like(dal_ref)
            ddk_ref[...] = jnp.zeros_like(ddk_ref)
            dsnw_ref[...] = jnp.zeros_like(dsnw_ref)

        first = jnp.where(cc > 0, 1.0, 0.0)
        dacs_acc[...] = jnp.zeros_like(dacs_acc)
        q2_acc[...] = jnp.zeros_like(q2_acc)
        mask = _tril_mask()
        tri_t = (lax.broadcasted_iota(jnp.int32, (CH, CH), 0) <= lax.broadcasted_iota(jnp.int32, (CH, CH), 1)).astype(f32)
        lane1 = lax.broadcasted_iota(jnp.int32, (1, CH), 1)
        is_last = lax.broadcasted_iota(jnp.int32, (CH, 1), 0) == CH - 1
        z_all = dtr_ref[:, 0:CH] + dtb_ref[...]
        dt_all = _softplus(z_all)
        adt_all = dt_all * a_ref[...]
        acs_all = _dot_hi(mask.astype(f32), adt_all)
        acst_s[...] = acs_all.T

        def ind_sum(v):
            hi = v.astype(bf16)
            lo = (v - hi.astype(f32)).astype(bf16)
            return _dot(hi, ind_ref[...]) + _dot(lo, ind_ref[...])

        def group(g, carry):
            cols = _lanes(g, GB)
            gcols = _lanes(g, GW)
            x = xbc_ref[:, cols]
            conv, shifted = _conv_pre(x, xprev_ref[:, cols] * first, cw_ref, cb_ref, cols)
            sg = _sigmoid(conv)
            xc = conv * sg
            xs = xc[:, 0:GW]
            bg = xc[:, GW:GW + NST].astype(bf16)
            cg = xc[:, GW + NST:GB].astype(bf16)

            y = y_ref[:, gcols].astype(f32)
            silu_sz, dsilu_sz = _silu_and_grad(sz_ref[:, gcols])
            yz = y * silu_sz
            rr = lax.rsqrt(jnp.mean(yz * yz, axis=-1, keepdims=True) + EPS)
            dyb_g = dyb_ref[:, gcols].astype(f32)
            w = dyb_g * snw_ref[:, gcols]
            dsnw_ref[:, gcols] += jnp.sum(dyb_g * yz * rr, axis=0, keepdims=True)
            dyz = rr * w - yz * (rr * rr * rr) * jnp.mean(w * yz, axis=-1, keepdims=True)
            dp_ref[:, _lanes(g, GW, CONVD)] = (dyz * y * dsilu_sz).astype(bf16)
            dy_g = dyz * silu_sz
            ddk_ref[:, gcols] += jnp.sum(dy_g * xs, axis=0, keepdims=True)

            back = lax.rem(CH - HPG * g, CH)
            dt = pltpu.roll(dt_all, back, 1)
            acs = pltpu.roll(acs_all, back, 1)
            cbm = _dot_nt(cg, bg)
            d_cb = jnp.zeros((CH, CH), f32)
            d_bg = jnp.zeros((CH, NST), f32)
            d_cg = jnp.zeros((CH, NST), f32)
            lastrow = jnp.zeros((1, CH), f32)
            dxd_parts, dxs_parts, t_parts = [], [], []
            for r in range(HPG):
                h = g * HPG + r
                colb = jnp.broadcast_to(acs[:, r:r + 1], (CH, CH))
                row = acst_s[pl.ds(h, 1), :]
                lmat = jnp.exp(jnp.where(mask, colb - row, -jnp.inf))
                mmat_b = (cbm * lmat).astype(bf16)
                dtc = dt[:, r:r + 1]
                xd = xs[:, r * HD:(r + 1) * HD] * dtc
                col = colb[:, 0:HD]
                alast = colb[CH - 1:CH, 0:HD]
                dte = jnp.exp(alast - col)
                ea = jnp.exp(col)
                cd = jnp.exp(colb[CH - 1:CH, :])
                sp = st_ref[0, h]
                dsn = dstate[h]
                dsn_b = dsn.astype(bf16)
                dyr = dy_g[:, r * HD:(r + 1) * HD]
                dyr_b = dyr.astype(bf16)
                dye_b = (dyr * ea).astype(bf16)
                d_cg = d_cg + _dot(dye_b, sp)
                dxde = _dot_nt(bg, dsn_b)
                xdte = xd * dte
                xd_b = xd.astype(bf16)
                d_bg = d_bg + _dot(xdte.astype(bf16), dsn_b)
                dxd_diag = _dot_tn(mmat_b, dyr_b)
                dxd = dxde * dte + dxd_diag
                d_cb = d_cb + _dot_nt(dyr_b, xd_b) * lmat
                t_parts.append(dyr_b.astype(f32) * _dot(mmat_b, xd_b) + dyr * (ea * _dot_nt(cg, sp))
                               - xd_b.astype(f32) * dxd_diag - dxde * xdte)
                lastrow = jnp.where(lane1 == r, _sum_all(dsn * sp.astype(f32)) * cd + _sum_all(dxde * xdte), lastrow)
                dstate[h] = cd * dsn + _dot_tn(dye_b, cg)
                dxd_parts.append(dxd)
                dxs_parts.append(dxd * dtc)
            d_cb_b = d_cb.astype(bf16)
            d_bg = d_bg + _dot_tn(d_cb_b, cg)
            d_cg = d_cg + _dot(d_cb_b, bg)
            q2 = ind_sum(jnp.concatenate(dxd_parts, axis=1) * xs)
            dacs = ind_sum(jnp.concatenate(t_parts, axis=1)) + jnp.where(is_last, lastrow, 0.0)
            dacs_acc[...] += pltpu.roll(dacs, HPG * g, 1)
            q2_acc[...] += pltpu.roll(q2, HPG * g, 1)
            dxs = jnp.concatenate(dxs_parts, axis=1) + dy_g * dkc_ref[:, gcols]

            dconv = jnp.concatenate([dxs, d_bg, d_cg], axis=1) * (sg * (1.0 + conv * (1.0 - sg)))
            dcb_ref[:, cols] += jnp.sum(dconv, axis=0, keepdims=True)
            dcw_ref[3:4, cols] += jnp.sum(dconv * x, axis=0, keepdims=True)
            dx = cw_ref[3:4, cols] * dconv
            nxt = dnext8[:, cols]
            for j in (1, 2, 3):
                dcw_ref[3 - j:4 - j, cols] += jnp.sum(dconv * shifted[j - 1], axis=0, keepdims=True)
                dx = dx + cw_ref[3 - j:4 - j, cols] * _shift_rows_up(dconv, nxt, j)
            dnext8[:, cols] = dconv[0:8]
            dp_ref[:, cols] = dx.astype(bf16)
            return carry

        lax.fori_loop(0, NG, group, 0)

        dadt = _dot_hi(tri_t, dacs_acc[...])
        dal_ref[...] += jnp.sum(dadt * adt_all, axis=0, keepdims=True)
        ddz = (dadt * a_ref[...] + q2_acc[...]) * _sigmoid(z_all)
        ddtb_ref[...] += jnp.sum(ddz, axis=0, keepdims=True)
        dp_ref[:, CONVD + DIN:CONVD + DIN + CH] = ddz.astype(bf16)
        dp_ref[:, CONVD + DIN + CH:PB] = jnp.zeros((CH, PB - CONVD - DIN - CH), bf16)

    const2 = lambda c: (0, 0)
    rev = lambda c: (nc - 1 - c, 0)
    return pl.pallas_call(
        body, grid=(nc,),
        in_specs=[pl.BlockSpec((CH, CONVD), rev),
                  pl.BlockSpec((8, CONVD), lambda c: (jnp.maximum((nc - 1 - c) * (CH // 8) - 1, 0), 0)),
                  pl.BlockSpec((CH, DIN), lambda c: (nc - 1 - c, CONVD // DIN)),
                  pl.BlockSpec((CH, 256), lambda c: (nc - 1 - c, (CONVD + DIN) // 256)),
                  pl.BlockSpec((CH, DIN), rev), pl.BlockSpec((CH, DIN), rev),
                  pl.BlockSpec((1, NH, HD, NST), lambda c: (nc - 1 - c, 0, 0, 0)),
                  pl.BlockSpec((4, CONVD), const2), pl.BlockSpec((1, CONVD), const2),
                  pl.BlockSpec((1, CH), const2), pl.BlockSpec((1, CH), const2),
                  pl.BlockSpec((1, DIN), const2), pl.BlockSpec((1, DIN), const2), pl.BlockSpec((GW, CH), const2)],
        out_specs=[pl.BlockSpec((CH, PB), rev), pl.BlockSpec((4, CONVD), const2), pl.BlockSpec((1, CONVD), const2),
                   pl.BlockSpec((1, CH), const2), pl.BlockSpec((1, CH), const2), pl.BlockSpec((1, DIN), const2),
                   pl.BlockSpec((1, DIN), const2)],
        out_shape=[jax.ShapeDtypeStruct((s, PB), bf16), jax.ShapeDtypeStruct((4, CONVD), f32),
                   jax.ShapeDtypeStruct((1, CONVD), f32), jax.ShapeDtypeStruct((1, CH), f32),
                   jax.ShapeDtypeStruct((1, CH), f32), jax.ShapeDtypeStruct((1, DIN), f32),
                   jax.ShapeDtypeStruct((1, DIN), f32)],
        scratch_shapes=[pltpu.VMEM((NH, HD, NST), f32), pltpu.VMEM((8, CONVD), f32), pltpu.VMEM((CH, CH), f32),
                        pltpu.VMEM((CH, CH), f32), pltpu.VMEM((CH, CH), f32)],
        name=name, compiler_params=_cp(("arbitrary",)),
    )(proj_b, proj_b, proj_b, proj_b, dyb, y_sv, states, cw, cb, dtb, a_row, dkc, snw, ind)


def _merge_fwd(ya, yb, proj_g, x, gate, wa, wb, wo, name):
    s = x.shape[0]
    ts = min(512, s)

    def body(ya_ref, yb_ref, ga_ref, gb_ref, x_ref, gate_ref, wa_ref, wb_ref, wo_ref, xo_ref, pa_ref, pb_ref, mg_ref, o_ref):
        pa = _dot(ya_ref[...], wa_ref[...])
        pb = _dot(yb_ref[...], wb_ref[...])
        mg = (_sigmoid(ga_ref[...]) * pa + _sigmoid(gb_ref[...]) * pb).astype(bf16)
        o = _dot(mg, wo_ref[...])
        xo_ref[...] = x_ref[...] + gate_ref[...] * o
        pa_ref[...] = pa.astype(bf16)
        pb_ref[...] = pb.astype(bf16)
        mg_ref[...] = mg
        o_ref[...] = o.astype(bf16)

    tile = pl.BlockSpec((ts, D), lambda i: (i, 0))
    const = lambda i: (0, 0)
    act = jax.ShapeDtypeStruct((s, D), bf16)
    return pl.pallas_call(
        body, grid=(s // ts,),
        in_specs=[tile, pl.BlockSpec((ts, DIN), lambda i: (i, 0)), tile, pl.BlockSpec((ts, D), lambda i: (i, 1)), tile,
                  pl.BlockSpec((1, D), const), pl.BlockSpec((D, D), const), pl.BlockSpec((DIN, D), const),
                  pl.BlockSpec((D, D), const)],
        out_specs=[tile, tile, tile, tile, tile],
        out_shape=[jax.ShapeDtypeStruct((s, D), f32), act, act, act, act],
        name=name, compiler_params=_cp(("parallel",)),
    )(ya, yb, proj_g, proj_g, x, gate, wa, wb, wo)


def _merge_bwd(dxo, gate, o_sv, pa_sv, pb_sv, proj_g, wo_t, wa_t, wb_t, name):
    s = dxo.shape[0]
    ts = min(512, s)

    def body(dxo_ref, gate_ref, o_ref, pa_ref, pb_ref, ga_ref, gb_ref, wot_ref, wat_ref, wbt_ref,
             do_ref, dpa_ref, dpb_ref, dg_ref, dya_ref, dyb_ref, dgate_ref):
        @pl.when(pl.program_id(0) == 0)
        def _():
            dgate_ref[...] = jnp.zeros_like(dgate_ref)
        dxo_v = dxo_ref[...]
        dgate_ref[...] += jnp.sum(dxo_v * o_ref[...].astype(f32), axis=0, keepdims=True)
        do = (dxo_v * gate_ref[...]).astype(bf16)
        do_ref[...] = do
        dmg = _dot(do, wot_ref[...])
        sa = _sigmoid(ga_ref[...])
        sb = _sigmoid(gb_ref[...])
        dpa = (dmg * sa).astype(bf16)
        dpb = (dmg * sb).astype(bf16)
        dpa_ref[...] = dpa
        dpb_ref[...] = dpb
        dg_ref[:, 0:D] = (dmg * pa_ref[...].astype(f32) * sa * (1.0 - sa)).astype(bf16)
        dg_ref[:, D:2 * D] = (dmg * pb_ref[...].astype(f32) * sb * (1.0 - sb)).astype(bf16)
        dya_ref[...] = _dot(dpa, wat_ref[...]).astype(bf16)
        dyb_ref[...] = _dot(dpb, wbt_ref[...]).astype(bf16)

    tile = pl.BlockSpec((ts, D), lambda i: (i, 0))
    const = lambda i: (0, 0)
    act = jax.ShapeDtypeStruct((s, D), bf16)
    return pl.pallas_call(
        body, grid=(s // ts,),
        in_specs=[tile, pl.BlockSpec((1, D), const), tile, tile, tile, tile, pl.BlockSpec((ts, D), lambda i: (i, 1)),
                  pl.BlockSpec((D, D), const), pl.BlockSpec((D, D), const), pl.BlockSpec((D, DIN), const)],
        out_specs=[tile, tile, tile, pl.BlockSpec((ts, PG), lambda i: (i, 0)), tile, pl.BlockSpec((ts, DIN), lambda i: (i, 0)),
                   pl.BlockSpec((1, D), const)],
        out_shape=[act, act, act, jax.ShapeDtypeStruct((s, PG), bf16), act, jax.ShapeDtypeStruct((s, DIN), bf16),
                   jax.ShapeDtypeStruct((1, D), f32)],
        name=name, compiler_params=_cp(("arbitrary",)),
    )(dxo, gate, o_sv, pa_sv, pb_sv, proj_g, proj_g, wo_t, wa_t, wb_t)


def _final_loss(x, target, fnw, name):
    s = x.shape[0]
    ts = min(512, s)

    def body(x_ref, t_ref, w_ref, loss_ref, dx_ref, dw_ref):
        @pl.when(pl.program_id(0) == 0)
        def _():
            loss_ref[...] = jnp.zeros_like(loss_ref)
            dw_ref[...] = jnp.zeros_like(dw_ref)
        xv = x_ref[...]
        r = lax.rsqrt(jnp.mean(xv * xv, axis=-1, keepdims=True) + EPS)
        xn = xv * r
        err = xn * w_ref[...] - t_ref[...]
        part = jnp.sum(err * err, axis=0, keepdims=True)
        acc = part[:, 0:128]
        for k in range(1, D // 128):
            acc = acc + part[:, k * 128:(k + 1) * 128]
        loss_ref[0:1, :] += acc * (0.5 / D)
        dy = err * (1.0 / D)
        dw_ref[...] += jnp.sum(dy * xn, axis=0, keepdims=True)
        dxn = dy * w_ref[...]
        dx_ref[...] = r * (dxn - xn * jnp.mean(dxn * xn, axis=-1, keepdims=True))

    tile = pl.BlockSpec((ts, D), lambda i: (i, 0))
    row = pl.BlockSpec((1, D), lambda i: (0, 0))
    return pl.pallas_call(
        body, grid=(s // ts,), in_specs=[tile, tile, row],
        out_specs=[pl.BlockSpec((8, 128), lambda i: (0, 0)), tile, row],
        out_shape=[jax.ShapeDtypeStruct((8, 128), f32), jax.ShapeDtypeStruct((s, D), f32), jax.ShapeDtypeStruct((1, D), f32)],
        name=name, compiler_params=_cp(("arbitrary",)),
    )(x, target, fnw)


def _layer_operands(w_in, conv_w, wa, wb, wo, norm_w, gm_ln_w, gm_ln_b, gm_ws, gm_bs, conv_b, dt_bias, a_log, d_skip, ssm_norm_w):
    w_xbc = _group_major(w_in[:, 5120:7168], w_in[:, 7168:8192], w_in[:, 8192:9216])
    w_b = jnp.concatenate([w_xbc, w_in[:, 3072:5120], w_in[:, 9216:9248], jnp.zeros((D, 224), bf16)], axis=1)
    w_a = w_in[:, 0:3072]
    w_g = w_in[:, 9248:N_IN]
    tril = jnp.tril(jnp.ones((CH, CH), bool))
    wsm = jnp.where(tril[None], gm_ws, 0.0).astype(bf16)

    def heads_row(v):
        return jnp.pad(v, (0, CH - NH)).reshape(1, CH)

    return dict(
        w_b=w_b, w_a=w_a, w_g=w_g, w_b_t=w_b.T, w_a_t=w_a.T, w_g_t=w_g.T,
        wa=wa, wb=wb, wo=wo, wa_t=wa.T, wb_t=wb.T, wo_t=wo.T,
        norm_w=norm_w.reshape(1, D), lnw=gm_ln_w.reshape(1, D), lnb=gm_ln_b.reshape(1, D),
        wsm=wsm, wsm_t=jnp.swapaxes(wsm, 1, 2), bsf=jnp.repeat(gm_bs.T, CH, axis=1),
        cw=_group_major(conv_w[:, 0:DIN], conv_w[:, DIN:DIN + NG * NST], conv_w[:, DIN + NG * NST:CONVD]),
        cb=_group_major(conv_b[0:DIN], conv_b[DIN:DIN + NG * NST], conv_b[DIN + NG * NST:CONVD]).reshape(1, CONVD),
        dtb=heads_row(dt_bias), a_row=heads_row(-jnp.exp(a_log)),
        snw=ssm_norm_w.reshape(1, DIN), dkc=jnp.repeat(d_skip, HD).reshape(1, DIN),
        ind=(jnp.arange(GW)[:, None] // HD == jnp.arange(CH)[None, :]).astype(bf16),
    )


def _layer_fwd(x, shift, scale, gate, p, tag):
    h = _rmsmod_fwd(x, p["norm_w"], scale, shift, f"rmsmod_fwd{tag}")
    proj_b = _mm(h, p["w_b"], f32, f"proj_b{tag}")
    proj_a = _mm(h, p["w_a"], f32, f"proj_a{tag}")
    proj_g = _mm(h, p["w_g"], f32, f"proj_g{tag}")
    ya = _branch_a_fwd(proj_a, p["lnw"], p["lnb"], p["wsm"], p["bsf"], f"branch_a_fwd{tag}")
    yb, y_sv, states = _branch_b_fwd(proj_b, p["cw"], p["cb"], p["dtb"], p["a_row"], p["dkc"], p["snw"], f"branch_b_fwd{tag}")
    x_out, pa, pb, mg, o = _merge_fwd(ya, yb, proj_g, x, gate, p["wa"], p["wb"], p["wo"], f"merge_fwd{tag}")
    saved = dict(x=x, h=h, proj_b=proj_b, proj_a=proj_a, proj_g=proj_g, ya=ya, yb=yb, y=y_sv, states=states,
                 pa=pa, pb=pb, mg=mg, o=o, scale=scale, gate=gate)
    return x_out, saved


def _layer_bwd(dxo, sv, p, tag):
    do, dpa, dpb, dg, dya, dyb, dgate = _merge_bwd(dxo, sv["gate"], sv["o"], sv["pa"], sv["pb"], sv["proj_g"],
                                                   p["wo_t"], p["wa_t"], p["wb_t"], f"merge_bwd{tag}")
    d_wo = _mm_tn(sv["mg"], do, f"d_wo{tag}")
    d_wa = _mm_tn(sv["ya"], dpa, f"d_wa{tag}")
    d_wb = _mm_tn(sv["yb"], dpb, f"d_wb{tag}")
    da, dws, dbs, dlnw, dlnb = _branch_a_bwd(sv["proj_a"], dya, p["lnw"], p["lnb"], p["wsm"], p["wsm_t"], p["bsf"],
                                             f"branch_a_bwd{tag}")
    db, dcw, dcb, ddtb, dal, ddk, dsnw = _branch_b_bwd(sv["proj_b"], dyb, sv["y"], sv["states"], p["cw"], p["cb"], p["dtb"],
                                                       p["a_row"], p["dkc"], p["snw"], p["ind"], f"branch_b_bwd{tag}")
    dh = _mm(db, p["w_b_t"], f32, f"dh_b{tag}")
    dh = _mm(da, p["w_a_t"], f32, f"dh_a{tag}", c_in=dh)
    dh = _mm(dg, p["w_g_t"], f32, f"dh_g{tag}", c_in=dh)
    d_w_b = _mm_tn(sv["h"], db, f"d_w_b{tag}")
    d_w_a = _mm_tn(sv["h"], da, f"d_w_a{tag}")
    d_w_g = _mm_tn(sv["h"], dg, f"d_w_g{tag}")
    dx, dscale, dshift, dnw = _rmsmod_bwd(dh, sv["x"], dxo, p["norm_w"], sv["scale"], f"rmsmod_bwd{tag}")
    d_w_in = jnp.concatenate([d_w_a, d_w_b[:, CONVD:CONVD + DIN], _from_group_major(d_w_b[:, 0:CONVD]),
                              d_w_b[:, CONVD + DIN:CONVD + DIN + NH], d_w_g], axis=1)
    tril = jnp.tril(jnp.ones((CH, CH), bool))
    heads = lambda v: v[0, 0:NH]
    grads = dict(
        w_in=d_w_in, w_proj_a=d_wa, w_proj_b=d_wb, w_out=d_wo, conv_w=_from_group_major(dcw), conv_b=_from_group_major(dcb).reshape(CONVD),
        norm_w=dnw.reshape(D), gm_ln_w=dlnw.reshape(D), gm_ln_b=dlnb.reshape(D),
        gm_ws=jnp.where(tril[None], dws, 0.0), gm_bs=dbs.reshape(CH, NG, CH).sum(-1).T,
        dt_bias=heads(ddtb), a_log=heads(dal), d_skip=ddk.reshape(NH, HD).sum(-1), ssm_norm_w=dsnw.reshape(DIN),
        mod=jnp.concatenate([dshift, dscale, dgate], axis=1).reshape(3 * D),
    )
    return dx, grads


def _local_step(x, target, mods, layer_ops, fnw):
    saved = []
    for l in range(DEPTH):
        shift, scale, gate = mods[l]
        x, sv = _layer_fwd(x, shift, scale, gate, layer_ops[l], f"_l{l}")
        saved.append(sv)
    loss_parts, dx, dfnw = _final_loss(x, target, fnw.reshape(1, D), "final_loss")
    grads = [None] * DEPTH
    for l in reversed(range(DEPTH)):
        dx, grads[l] = _layer_bwd(dx, saved[l], layer_ops[l], f"_l{l}")
    return jnp.sum(loss_parts), dx, grads, dfnw.reshape(D)


ADA_COLS = 3 * D // NSHARD


def _ada_fwd(c_all, ada_w, ada_b_cols, name):
    def body(c_ref, w_ref, b_ref, o_ref):
        cv = c_ref[...]
        sc = cv * _sigmoid(cv)
        for l in range(DEPTH):
            o_ref[l] = _dot_hi(sc, w_ref[l]) + b_ref[l]

    return pl.pallas_call(body, out_shape=jax.ShapeDtypeStruct((DEPTH, 8, ADA_COLS), f32), name=name,
                          compiler_params=_cp(None))(c_all, ada_w, ada_b_cols)


def _adam_math(w, g, m, v):
    m = ADAM_B1 * m + (1.0 - ADAM_B1) * g
    v = ADAM_B2 * v + (1.0 - ADAM_B2) * (g * g)
    m_hat = m / (1.0 - ADAM_B1 ** ADAM_STEP)
    v_hat = v / (1.0 - ADAM_B2 ** ADAM_STEP)
    delta = -ADAM_LR * (m_hat / (jnp.sqrt(v_hat) + ADAM_EPS) + ADAM_WD * w)
    return delta, m, v


def _ada_bwd_adamw(c_all, dmod_cols, w, m, v, name):
    tr = 256

    def body(c_ref, dm_ref, w_ref, m_ref, v_ref, g_ref, d_ref, nm_ref, nv_ref):
        cv = c_ref[...]
        sc = cv * _sigmoid(cv)
        g = lax.dot_general(sc, dm_ref[0], (((0,), (0,)), ((), ())), precision=_HI, preferred_element_type=f32)
        g_ref[0] = g
        d_ref[0], nm_ref[0], nv_ref[0] = _adam_math(w_ref[0], g, m_ref[0], v_ref[0])

    blk = pl.BlockSpec((1, tr, ADA_COLS), lambda l, i: (l, i, 0))
    shp = jax.ShapeDtypeStruct((DEPTH, D, ADA_COLS), f32)
    return pl.pallas_call(
        body, grid=(DEPTH, D // tr),
        in_specs=[pl.BlockSpec((8, tr), lambda l, i: (0, i)), pl.BlockSpec((1, 8, ADA_COLS), lambda l, i: (l, 0, 0)), blk, blk, blk],
        out_specs=[blk, blk, blk, blk], out_shape=[shp, shp, shp, shp], name=name, compiler_params=_cp(("parallel", "parallel")),
    )(c_all, dmod_cols, w, m, v)


def _adamw(w, g, m, v, name):
    r, c = w.shape
    tr = _pick(r, (256, 128, 64, 32, 16, 8))

    def body(w_ref, g_ref, m_ref, v_ref, d_ref, nm_ref, nv_ref):
        d_ref[...], nm_ref[...], nv_ref[...] = _adam_math(w_ref[...], g_ref[...], m_ref[...], v_ref[...])

    blk = pl.BlockSpec((tr, c), lambda i: (i, 0))
    shp = jax.ShapeDtypeStruct((r, c), f32)
    return pl.pallas_call(body, grid=(r // tr,), in_specs=[blk] * 4, out_specs=[blk] * 3, out_shape=[shp] * 3, name=name,
                          compiler_params=_cp(("parallel",)))(w, g, m, v)


def _adamw_layers(w, g_mine, g_other, m, v, c_idx, name):
    _, r, c = w.shape
    tr = _pick(r, (128,))

    def body(ci_ref, w_ref, gm_ref, go_ref, m_ref, v_ref, g_ref, d_ref, nm_ref, nv_ref):
        def update(g):
            g_ref[0] = g
            d_ref[0], nm_ref[0], nv_ref[0] = _adam_math(w_ref[0], g, m_ref[0], v_ref[0])

        mine = pl.program_id(0) == ci_ref[0]

        @pl.when(mine)
        def _():
            update(gm_ref[...])

        @pl.when(jnp.logical_not(mine))
        def _():
            update(go_ref[...])

    blk = pl.BlockSpec((1, tr, c), lambda l, i, ci: (l, i, 0))
    gblk = pl.BlockSpec((tr, c), lambda l, i, ci: (i, 0))
    shp = jax.ShapeDtypeStruct(w.shape, f32)
    return pl.pallas_call(
        body,
        grid_spec=pltpu.PrefetchScalarGridSpec(num_scalar_prefetch=1, grid=(DEPTH, r // tr), in_specs=[blk, gblk, gblk, blk, blk],
                                               out_specs=[blk, blk, blk, blk]),
        out_shape=[shp, shp, shp, shp], name=name, compiler_params=_cp(("parallel", "parallel")),
    )(c_idx, w, g_mine, g_other, m, v)


_MESH = pl.DeviceIdType.MESH
_AXES = ("x", "y", "c")
_HBM = pl.BlockSpec(memory_space=pltpu.HBM)


def _my_place():
    return tuple(lax.axis_index(a) for a in _AXES)


def _allreduce(buf, axes, name):
    r = buf.shape[0]
    n = len(axes)

    def body(x_ref, o_ref, rbuf, ssem, rsem):
        me = dict(zip(_AXES, _my_place()))
        o_ref[...] = x_ref[...]
        for k, ax in enumerate(axes):
            peer = tuple(1 - me[a] if a == ax else me[a] for a in _AXES)
            cp = pltpu.make_async_remote_copy(src_ref=o_ref, dst_ref=rbuf.at[k], send_sem=ssem.at[k], recv_sem=rsem.at[k],
                                              device_id=peer, device_id_type=_MESH)
            cp.start()
            cp.wait()
            o_ref[...] = o_ref[...] + rbuf[k]

    vm = pl.BlockSpec(memory_space=pltpu.VMEM)
    return pl.pallas_call(
        body, out_shape=jax.ShapeDtypeStruct((r, 128), f32), in_specs=[vm], out_specs=vm,
        scratch_shapes=[pltpu.VMEM((n, r, 128), f32), pltpu.SemaphoreType.DMA((n,)), pltpu.SemaphoreType.DMA((n,))],
        name=name, compiler_params=pltpu.CompilerParams(vmem_limit_bytes=VMEM_LIMIT),
    )(buf)


def _other_chips(x, y):
    return [(1 - x, y), (x, 1 - y), (1 - x, 1 - y)]


def _gather_weights(shards, name):
    na = len(shards)

    def body(*refs):
        ins, outs = refs[:na], refs[na:2 * na]
        ssem, rsem = refs[2 * na:]
        x, y, c = _my_place()
        sibling = (x, y, 1 - c)
        chips = _other_chips(x, y)

        def rcopy(a, src, j, layer, to, idx):
            return pltpu.make_async_remote_copy(src_ref=src, dst_ref=outs[a].at[j, layer], send_sem=ssem.at[idx],
                                                recv_sem=rsem.at[idx], device_id=to, device_id_type=_MESH)

        sent = []
        for j, chip in enumerate(chips):
            for a in range(na):
                cp = rcopy(a, ins[a].at[c], j, c, (*chip, c), j * na + a)
                cp.start()
                sent.append(cp)
        for j, chip in enumerate(chips):
            for a in range(na):
                rcopy(a, ins[a].at[c], j, c, (*chip, c), j * na + a).wait_recv()
                cp = rcopy(a, outs[a].at[j, c], j, c, sibling, (3 + j) * na + a)
                cp.start()
                sent.append(cp)
        for j in range(3):
            for a in range(na):
                rcopy(a, ins[a].at[c], j, 1 - c, sibling, (3 + j) * na + a).wait_recv()
        for cp in sent:
            cp.wait_send()

    out_shape = [jax.ShapeDtypeStruct((3,) + s.shape, s.dtype) for s in shards]
    return pl.pallas_call(
        body, out_shape=out_shape, in_specs=[_HBM] * na, out_specs=[_HBM] * na,
        scratch_shapes=[pltpu.SemaphoreType.DMA((6 * na,)), pltpu.SemaphoreType.DMA((6 * na,))], name=name,
    )(*shards)


def _with_own(own, others, k):
    xi, yi, _ = _my_place()
    d = k ^ (2 * xi + yi)
    j = jnp.where(d == 2, 0, jnp.where(d == 1, 1, 2))
    return jnp.where(d == 0, own, lax.dynamic_index_in_dim(others, j, axis=0, keepdims=False))


def _swap_layers(parts, name):
    na = len(parts)

    def body(*refs):
        ins, outs = refs[:na], refs[na:2 * na]
        ssem, rsem = refs[2 * na:]
        x, y, c = _my_place()
        cps = [pltpu.make_async_remote_copy(src_ref=ins[a].at[1 - c], dst_ref=outs[a], send_sem=ssem.at[a], recv_sem=rsem.at[a],
                                            device_id=(x, y, 1 - c), device_id_type=_MESH) for a in range(na)]
        for cp in cps:
            cp.start()
        for cp in cps:
            cp.wait()

    return pl.pallas_call(
        body, out_shape=[jax.ShapeDtypeStruct(p.shape[1:], p.dtype) for p in parts], in_specs=[_HBM] * na, out_specs=[_HBM] * na,
        scratch_shapes=[pltpu.SemaphoreType.DMA((na,)), pltpu.SemaphoreType.DMA((na,))], name=name,
    )(*parts)


def _scatter_shards(sums, name):
    na = len(sums)

    def body(*refs):
        ins, outs = refs[:na], refs[na:2 * na]
        ssem, rsem = refs[2 * na:]
        x, y, c = _my_place()
        cps = []
        for j, chip in enumerate(_other_chips(x, y)):
            kj = 2 * chip[0] + chip[1]
            for a in range(na):
                cps.append(pltpu.make_async_remote_copy(
                    src_ref=ins[a].at[kj], dst_ref=outs[a].at[j], send_sem=ssem.at[j * na + a], recv_sem=rsem.at[j * na + a],
                    device_id=(*chip, c), device_id_type=_MESH))
        for cp in cps:
            cp.start()
        for cp in cps:
            cp.wait()

    return pl.pallas_call(
        body, out_shape=[jax.ShapeDtypeStruct((3,) + p.shape[1:], p.dtype) for p in sums], in_specs=[_HBM] * na,
        out_specs=[_HBM] * na, scratch_shapes=[pltpu.SemaphoreType.DMA((3 * na,)), pltpu.SemaphoreType.DMA((3 * na,))], name=name,
    )(*sums)


def _share_layers(finals, name):
    na = len(finals)

    def body(*refs):
        ins, outs = refs[:na], refs[na:2 * na]
        ssem, rsem = refs[2 * na:]
        x, y, c = _my_place()
        cps = [pltpu.make_async_remote_copy(src_ref=ins[a], dst_ref=outs[a], send_sem=ssem.at[a], recv_sem=rsem.at[a],
                                            device_id=(x, y, 1 - c), device_id_type=_MESH) for a in range(na)]
        for cp in cps:
            cp.start()
        for cp in cps:
            cp.wait()

    return pl.pallas_call(
        body, out_shape=[jax.ShapeDtypeStruct(p.shape, p.dtype) for p in finals], in_specs=[_HBM] * na, out_specs=[_HBM] * na,
        scratch_shapes=[pltpu.SemaphoreType.DMA((na,)), pltpu.SemaphoreType.DMA((na,))], name=name,
    )(*finals)


def _add_own_layer(parts, recv, c_idx, name):
    _, ns, r, c = parts.shape
    tr = _pick(r, (256, 128, 64, 32, 16, 8, 4))

    def body(ci_ref, p_ref, r_ref, o_ref, ob_ref):
        t = p_ref[0] + r_ref[...]
        o_ref[...] = t
        ob_ref[...] = t.astype(bf16)

    blk = pl.BlockSpec((1, tr, c), lambda k, i, ci: (k, i, 0))
    return pl.pallas_call(
        body,
        grid_spec=pltpu.PrefetchScalarGridSpec(
            num_scalar_prefetch=1, grid=(ns, r // tr),
            in_specs=[pl.BlockSpec((1, 1, tr, c), lambda k, i, ci: (ci[0], k, i, 0)), blk], out_specs=[blk, blk]),
        out_shape=[jax.ShapeDtypeStruct((ns, r, c), f32), jax.ShapeDtypeStruct((ns, r, c), bf16)], name=name,
        compiler_params=_cp(("parallel", "parallel")),
    )(c_idx, parts, recv)


def _add_own_shard(sums, recv, k_idx, name):
    _, r, c = sums.shape
    tr = _pick(r, (256, 128, 64, 32, 16, 8, 4))

    def body(ki_ref, s_ref, r_ref, o_ref):
        o_ref[...] = ((s_ref[0] + r_ref[0].astype(f32)) + r_ref[1].astype(f32)) + r_ref[2].astype(f32)

    return pl.pallas_call(
        body,
        grid_spec=pltpu.PrefetchScalarGridSpec(
            num_scalar_prefetch=1, grid=(r // tr,),
            in_specs=[pl.BlockSpec((1, tr, c), lambda i, ki: (ki[0], i, 0)), pl.BlockSpec((3, tr, c), lambda i, ki: (0, i, 0))],
            out_specs=pl.BlockSpec((tr, c), lambda i, ki: (i, 0))),
        out_shape=jax.ShapeDtypeStruct((r, c), f32), name=name, compiler_params=_cp(("parallel",)),
    )(k_idx, sums, recv)


def _reduce_scatter(parts, tag):
    x, y, c = _my_place()
    c_idx = jnp.reshape(c, (1,)).astype(jnp.int32)
    k_idx = jnp.reshape(2 * x + y, (1,)).astype(jnp.int32)
    na = len(parts)
    recv = _swap_layers(parts, f"rs_swap{tag}")
    sums = [_add_own_layer(parts[a], recv[a], c_idx, f"rs_add_layer{tag}_{a}") for a in range(na)]
    recv = _scatter_shards([sb for _, sb in sums], f"rs_scatter{tag}")
    finals = [_add_own_shard(sums[a][0], recv[a], k_idx, f"rs_add_shard{tag}_{a}") for a in range(na)]
    return finals, _share_layers(finals, f"rs_share{tag}"), c_idx


_SMALL = [("ada_b", (DEPTH, 3 * D)), ("norm_w", (DEPTH, D)), ("gm_ln_w", (DEPTH, D)), ("gm_ln_b", (DEPTH, D)),
          ("gm_ws", (DEPTH, NG, CH, CH)), ("gm_bs", (DEPTH, NG, CH)), ("conv_b", (DEPTH, CONVD)), ("dt_bias", (DEPTH, NH)),
          ("a_log", (DEPTH, NH)), ("d_skip", (DEPTH, NH)), ("ssm_norm_w", (DEPTH, DIN)), ("final_norm_w", (D,))]


def _rows_of(shape):
    n = 1
    for d in shape:
        n *= d
    return -(-n // 1024) * 8


def _pack(arrays):
    rows = []
    for a in arrays:
        flat = a.reshape(-1)
        r = _rows_of(a.shape)
        rows.append(jnp.pad(flat, (0, r * 128 - flat.shape[0])).reshape(r, 128))
    return jnp.concatenate(rows, axis=0)


def _unpack(buf, shapes):
    out, at = [], 0
    for shp in shapes:
        r = _rows_of(shp)
        n = 1
        for d in shp:
            n *= d
        out.append(buf[at:at + r].reshape(-1)[:n].reshape(shp))
        at += r
    return out


def kernel(x, c, ada_w, ada_b, norm_w, w_in, gm_ln_w, gm_ln_b, gm_ws, gm_bs, conv_w, conv_b, dt_bias, a_log, d_skip, ssm_norm_w, w_proj_a, w_proj_b, w_out, final_norm_w, loss_target, m_ada_w, m_ada_b, m_norm_w, m_w_in, m_gm_ln_w, m_gm_ln_b, m_gm_ws, m_gm_bs, m_conv_w, m_conv_b, m_dt_bias, m_a_log, m_d_skip, m_ssm_norm_w, m_w_proj_a, m_w_proj_b, m_w_out, m_final_norm_w, v_ada_w, v_ada_b, v_norm_w, v_w_in, v_gm_ln_w, v_gm_ln_b, v_gm_ws, v_gm_bs, v_conv_w, v_conv_b, v_dt_bias, v_a_log, v_d_skip, v_ssm_norm_w, v_w_proj_a, v_w_proj_b, v_w_out, v_final_norm_w):
    xi, yi, ci = _my_place()
    k_me = 2 * xi + yi
    b_me = 4 * xi + 2 * yi + ci
    w = dict(ada_b=ada_b, norm_w=norm_w, gm_ln_w=gm_ln_w, gm_ln_b=gm_ln_b, gm_ws=gm_ws, gm_bs=gm_bs, conv_b=conv_b, dt_bias=dt_bias,
             a_log=a_log, d_skip=d_skip, ssm_norm_w=ssm_norm_w, final_norm_w=final_norm_w)
    m = dict(ada_b=m_ada_b, norm_w=m_norm_w, gm_ln_w=m_gm_ln_w, gm_ln_b=m_gm_ln_b, gm_ws=m_gm_ws, gm_bs=m_gm_bs, conv_b=m_conv_b,
             dt_bias=m_dt_bias, a_log=m_a_log, d_skip=m_d_skip, ssm_norm_w=m_ssm_norm_w, final_norm_w=m_final_norm_w)
    v = dict(ada_b=v_ada_b, norm_w=v_norm_w, gm_ln_w=v_gm_ln_w, gm_ln_b=v_gm_ln_b, gm_ws=v_gm_ws, gm_bs=v_gm_bs, conv_b=v_conv_b,
             dt_bias=v_dt_bias, a_log=v_a_log, d_skip=v_d_skip, ssm_norm_w=v_ssm_norm_w, final_norm_w=v_final_norm_w)

    c_slot = lax.dynamic_update_slice(jnp.zeros((8, D), f32), c, (b_me, 0))
    c_all = _allreduce(c_slot.reshape(64, 128), _AXES, "gather_c").reshape(8, D)
    ada_b_cols = lax.dynamic_slice(ada_b, (0, k_me * ADA_COLS), (DEPTH, ADA_COLS)).reshape(DEPTH, 1, ADA_COLS)
    mod_cols = _ada_fwd(c_all, ada_w, ada_b_cols, "ada_fwd")
    mod_slot = lax.dynamic_update_slice(jnp.zeros((DEPTH, 8, 3 * D), f32), mod_cols, (0, 0, k_me * ADA_COLS))
    mod_all = _allreduce(mod_slot.reshape(-1, 128), ("x", "y"), "gather_mod").reshape(DEPTH, 8, 3 * D)
    mod_me = lax.dynamic_slice(mod_all, (0, b_me, 0), (DEPTH, 1, 3 * D))
    mods = [(mod_me[l, :, 0:D], mod_me[l, :, D:2 * D], mod_me[l, :, 2 * D:3 * D]) for l in range(DEPTH)]

    rows_sh = jnp.concatenate([w_proj_a, w_proj_b, w_out], axis=1).astype(bf16)
    win_sh = w_in.astype(bf16)
    win_oth, rows_oth, conv_oth = _gather_weights([win_sh, rows_sh, conv_w], "gather_weights")
    ops = []
    for l in range(DEPTH):
        rows_l = [_with_own(rows_sh[l], rows_oth[:, l], k) for k in range(NSHARD)]
        w_in_l = jnp.concatenate([_with_own(win_sh[l], win_oth[:, l], k) for k in range(NSHARD)], axis=1)
        wa_l = jnp.concatenate([t[0:256] for t in rows_l], axis=0)
        wb_l = jnp.concatenate([t[256:768] for t in rows_l], axis=0)
        wo_l = jnp.concatenate([t[768:1024] for t in rows_l], axis=0)
        cw_l = jnp.concatenate([_with_own(conv_w[l], conv_oth[:, l], k) for k in range(NSHARD)], axis=1)
        ops.append(_layer_operands(w_in_l, cw_l, wa_l, wb_l, wo_l, norm_w[l], gm_ln_w[l], gm_ln_b[l], gm_ws[l], gm_bs[l], conv_b[l],
                                   dt_bias[l], a_log[l], d_skip[l], ssm_norm_w[l]))

    loss_me, dx, grads, dfnw = _local_step(x[0], loss_target[0], mods, ops, final_norm_w)
    loss = lax.psum(loss_me, _AXES)

    s_in = (N_IN // NSHARD)
    g_in = jnp.stack([grads[l]["w_in"].reshape(D, NSHARD, s_in).transpose(1, 0, 2) for l in range(DEPTH)])
    g_rows = jnp.stack([jnp.concatenate([grads[l]["w_proj_a"].reshape(NSHARD, 256, D), grads[l]["w_proj_b"].reshape(NSHARD, 512, D),
                                         grads[l]["w_out"].reshape(NSHARD, 256, D)], axis=1) for l in range(DEPTH)])
    g_conv = jnp.stack([grads[l]["conv_w"].reshape(4, NSHARD, D).transpose(1, 0, 2) for l in range(DEPTH)])
    (f_in, f_rows, f_conv), (o_in, o_rows, o_conv), c_idx = _reduce_scatter([g_in, g_rows, g_conv], "")
    gr_in, d_in, nm_in, nv_in = _adamw_layers(w_in, f_in, o_in, m_w_in, v_w_in, c_idx, "adamw_w_in")
    cat = lambda a, b, c_: jnp.concatenate([a, b, c_], axis=1)
    gr_rows, d_rows, nm_rows, nv_rows = _adamw_layers(cat(w_proj_a, w_proj_b, w_out), f_rows, o_rows, cat(m_w_proj_a, m_w_proj_b, m_w_out),
                                                      cat(v_w_proj_a, v_w_proj_b, v_w_out), c_idx, "adamw_rows")
    gr_conv = jnp.where(ci == 0, jnp.stack([f_conv, o_conv]), jnp.stack([o_conv, f_conv]))
    split = lambda t: (t[:, 0:256], t[:, 256:768], t[:, 768:1024])

    dmod_slot = lax.dynamic_update_slice(jnp.zeros((DEPTH, 8, 3 * D), f32),
                                         jnp.stack([grads[l]["mod"] for l in range(DEPTH)]).reshape(DEPTH, 1, 3 * D), (0, b_me, 0))
    small_g = {n: (dfnw if n == "final_norm_w" else jnp.stack([grads[l]["mod" if n == "ada_b" else n] for l in range(DEPTH)]))
               for n, _ in _SMALL}
    packed = _allreduce(_pack([small_g[n] for n, _ in _SMALL] + [dmod_slot]), _AXES, "allreduce_small")
    n_small = sum(_rows_of(s) for _, s in _SMALL)
    g_small = packed[0:n_small]
    dmod_all = packed[n_small:].reshape(-1)[:DEPTH * 8 * 3 * D].reshape(DEPTH, 8, 3 * D)
    small_gw = jnp.concatenate([g_small, _pack([gr_conv])], axis=0)
    d_s, nm_s, nv_s = _adamw(_pack([w[n] for n, _ in _SMALL] + [conv_w]), small_gw,
                             _pack([m[n] for n, _ in _SMALL] + [m_conv_w]), _pack([v[n] for n, _ in _SMALL] + [v_conv_w]), "adamw_small")
    shapes = [s for _, s in _SMALL] + [conv_w.shape]
    names = [n for n, _ in _SMALL] + ["conv_w"]
    g_d = dict(zip(names, _unpack(small_gw, shapes)))
    d_d = dict(zip(names, _unpack(d_s, shapes)))
    nm_d = dict(zip(names, _unpack(nm_s, shapes)))
    nv_d = dict(zip(names, _unpack(nv_s, shapes)))

    dmod_cols = lax.dynamic_slice(dmod_all, (0, 0, k_me * ADA_COLS), (DEPTH, 8, ADA_COLS))
    g_ada, d_ada, nm_ada, nv_ada = _ada_bwd_adamw(c_all, dmod_cols, ada_w, m_ada_w, v_ada_w, "ada_bwd_adamw")

    def by_name(big, small):
        ga, gb, go = split(big[1])
        return dict(small, ada_w=big[2], w_in=big[0], w_proj_a=ga, w_proj_b=gb, w_out=go)

    order = ["ada_w", "ada_b", "norm_w", "w_in", "gm_ln_w", "gm_ln_b", "gm_ws", "gm_bs", "conv_w", "conv_b", "dt_bias", "a_log",
             "d_skip", "ssm_norm_w", "w_proj_a", "w_proj_b", "w_out", "final_norm_w"]
    outs = []
    for big, small in (((gr_in, gr_rows, g_ada), g_d), ((d_in, d_rows, d_ada), d_d), ((nm_in, nm_rows, nm_ada), nm_d),
                       ((nv_in, nv_rows, nv_ada), nv_d)):
        t = by_name(big, small)
        outs += [t[n] for n in order]
    return (loss, dx.reshape(1, -1, D), *outs)
```

```python
import jax
import jax.numpy as jnp
from jax import lax
from jax.experimental import pallas as pl
from jax.experimental.pallas import tpu as pltpu

f32 = jnp.float32
bf16 = jnp.bfloat16

D = 1024
DEPTH = 2
EPS = 1e-6
CH = 128
NG = 8
HPG = 4
HD = 64
NH = NG * HPG
NST = 128
DIN = 2048
CONVD = 4096
GW = DIN // NG
PB = CONVD + DIN + 256
PA = 3 * D
PG = 2 * D
N_IN = 11296
NSHARD = 4
V7X_VMEM_BYTES = 64 * 2 ** 20
VMEM_LIMIT = V7X_VMEM_BYTES - 8 * 2 ** 20

ADAM_LR, ADAM_B1, ADAM_B2, ADAM_EPS, ADAM_WD, ADAM_STEP = 0.001, 0.9, 0.999, 1e-08, 0.01, 10

_HI = lax.Precision.HIGHEST


def _cp(sem):
    return pltpu.CompilerParams(dimension_semantics=sem, vmem_limit_bytes=VMEM_LIMIT)


def _sigmoid(x):
    return 0.5 * jnp.tanh(0.5 * x) + 0.5


def _silu_and_grad(x):
    s = _sigmoid(x)
    return x * s, s * (1.0 + x * (1.0 - s))


_GELU_K = 0.7978845608028654
_GELU_C = 0.044715


def _gelu_and_grad(x):
    x2 = x * x
    t = jnp.tanh(_GELU_K * (x + _GELU_C * x * x2))
    g = 0.5 * x * (1.0 + t)
    dg = 0.5 * (1.0 + t) + 0.5 * x * (1.0 - t * t) * _GELU_K * (1.0 + 3.0 * _GELU_C * x2)
    return g, dg


def _gelu(x):
    t = jnp.tanh(_GELU_K * (x + _GELU_C * x * x * x))
    return 0.5 * x * (1.0 + t)


def _softplus(x):
    return jnp.maximum(x, 0.0) + jnp.log(1.0 + jnp.exp(-jnp.abs(x)))


def _dot(a, b):
    return jnp.dot(a, b, preferred_element_type=f32)


def _dot_nt(a, b):
    return lax.dot_general(a, b, (((1,), (1,)), ((), ())), preferred_element_type=f32)


def _dot_tn(a, b):
    return lax.dot_general(a, b, (((0,), (0,)), ((), ())), preferred_element_type=f32)


def _dot_hi(a, b):
    return jnp.dot(a, b, precision=_HI, preferred_element_type=f32)


def _rmsmod_fwd(x, nw, scale, shift, name):
    s = x.shape[0]
    ts = min(512, s)

    def body(x_ref, nw_ref, sc_ref, sh_ref, h_ref):
        xv = x_ref[...]
        r = lax.rsqrt(jnp.mean(xv * xv, axis=-1, keepdims=True) + EPS)
        h_ref[...] = ((xv * r) * nw_ref[...] * (1.0 + sc_ref[...]) + sh_ref[...]).astype(bf16)

    row = pl.BlockSpec((1, D), lambda i: (0, 0))
    tile = pl.BlockSpec((ts, D), lambda i: (i, 0))
    return pl.pallas_call(
        body, grid=(s // ts,), in_specs=[tile, row, row, row], out_specs=tile,
        out_shape=jax.ShapeDtypeStruct((s, D), bf16), name=name, compiler_params=_cp(("parallel",)),
    )(x, nw, scale, shift)


def _rmsmod_bwd(dh, x, dres, nw, scale, name):
    s = x.shape[0]
    ts = min(512, s)

    def body(dh_ref, x_ref, dres_ref, nw_ref, sc_ref, dx_ref, dsc_ref, dsh_ref, dnw_ref):
        @pl.when(pl.program_id(0) == 0)
        def _():
            dsc_ref[...] = jnp.zeros_like(dsc_ref)
            dsh_ref[...] = jnp.zeros_like(dsh_ref)
            dnw_ref[...] = jnp.zeros_like(dnw_ref)
        xv = x_ref[...]
        dhv = dh_ref[...]
        r = lax.rsqrt(jnp.mean(xv * xv, axis=-1, keepdims=True) + EPS)
        xn = xv * r
        one_sc = 1.0 + sc_ref[...]
        dsc_ref[...] += jnp.sum(dhv * xn * nw_ref[...], axis=0, keepdims=True)
        dsh_ref[...] += jnp.sum(dhv, axis=0, keepdims=True)
        dnw_ref[...] += jnp.sum(dhv * xn * one_sc, axis=0, keepdims=True)
        dxn = dhv * (nw_ref[...] * one_sc)
        dx_ref[...] = r * (dxn - xn * jnp.mean(dxn * xn, axis=-1, keepdims=True)) + dres_ref[...]

    row = pl.BlockSpec((1, D), lambda i: (0, 0))
    tile = pl.BlockSpec((ts, D), lambda i: (i, 0))
    vec = jax.ShapeDtypeStruct((1, D), f32)
    return pl.pallas_call(
        body, grid=(s // ts,), in_specs=[tile, tile, tile, row, row], out_specs=[tile, row, row, row],
        out_shape=[jax.ShapeDtypeStruct((s, D), f32), vec, vec, vec], name=name, compiler_params=_cp(("arbitrary",)),
    )(dh, x, dres, nw, scale)


def _pick(n, prefs):
    for p in prefs:
        if n % p == 0:
            return p
    return n


def _mm(a, b, out_dtype, name, c_in=None):
    m, k = a.shape
    n = b.shape[1]
    tm = _pick(m, (1024, 512, 256))
    tn = _pick(n, (1280, 1024, 512))
    tk = _pick(k, (1280, 1024, 512))
    nk = k // tk

    def body(*refs):
        if c_in is not None:
            a_ref, b_ref, c_ref, o_ref, acc = refs
        else:
            a_ref, b_ref, o_ref, acc = refs
        kk = pl.program_id(2)

        @pl.when(kk == 0)
        def _():
            if c_in is not None:
                acc[...] = c_ref[...]
            else:
                acc[...] = jnp.zeros_like(acc)
        acc[...] += _dot(a_ref[...], b_ref[...])

        @pl.when(kk == nk - 1)
        def _():
            o_ref[...] = acc[...].astype(out_dtype)

    in_specs = [pl.BlockSpec((tm, tk), lambda j, i, kk: (i, kk)), pl.BlockSpec((tk, tn), lambda j, i, kk: (kk, j))]
    args = [a, b]
    if c_in is not None:
        in_specs.append(pl.BlockSpec((tm, tn), lambda j, i, kk: (i, j)))
        args.append(c_in)
    return pl.pallas_call(
        body, grid=(n // tn, m // tm, nk), in_specs=in_specs, out_specs=pl.BlockSpec((tm, tn), lambda j, i, kk: (i, j)),
        out_shape=jax.ShapeDtypeStruct((m, n), out_dtype), scratch_shapes=[pltpu.VMEM((tm, tn), f32)], name=name,
        compiler_params=_cp(("parallel", "parallel", "arbitrary")),
    )(*args)


def _mm_tn(a, b, name):
    t, k1 = a.shape
    n = b.shape[1]
    t1 = _pick(k1, (1280, 1024, 512))
    tn = _pick(n, (1280, 1024, 512))
    tt = _pick(t, (512, 256))
    nt = t // tt

    def body(a_ref, b_ref, o_ref):
        tt_i = pl.program_id(2)

        @pl.when(tt_i == 0)
        def _():
            o_ref[...] = jnp.zeros_like(o_ref)
        o_ref[...] += _dot_tn(a_ref[...], b_ref[...])

    return pl.pallas_call(
        body, grid=(k1 // t1, n // tn, nt),
        in_specs=[pl.BlockSpec((tt, t1), lambda i, j, tt_i: (tt_i, i)), pl.BlockSpec((tt, tn), lambda i, j, tt_i: (tt_i, j))],
        out_specs=pl.BlockSpec((t1, tn), lambda i, j, tt_i: (i, j)),
        out_shape=jax.ShapeDtypeStruct((k1, n), f32), name=name,
        compiler_params=_cp(("parallel", "parallel", "arbitrary")),
    )(a, b)


def _ln_stats(v):
    mu = jnp.mean(v, axis=-1, keepdims=True)
    vc = v - mu
    rstd = lax.rsqrt(jnp.mean(vc * vc, axis=-1, keepdims=True) + EPS)
    return vc * rstd, rstd


def _mix(w_ref, vl):
    return jnp.concatenate([_dot(w_ref[g], vl[:, g * CH:(g + 1) * CH]) for g in range(NG)], axis=1)


def _branch_a_fwd(proj_a, lnw, lnb, wsm, bsf, name):
    s = proj_a.shape[0]
    ta = min(256, s)

    def body(pu_ref, pv_ref, pz_ref, lnw_ref, lnb_ref, w_ref, bs_ref, ya_ref):
        for c in range(ta // CH):
            rows = pl.ds(c * CH, CH)
            vh, _ = _ln_stats(_gelu(pv_ref[rows, :]))
            vl = (vh * lnw_ref[...] + lnb_ref[...]).astype(bf16)
            mixed = _mix(w_ref, vl) + bs_ref[...]
            pz = pz_ref[rows, :]
            ya_ref[rows, :] = (_gelu(pu_ref[rows, :]) * mixed * (pz * _sigmoid(pz))).astype(bf16)

    row = pl.BlockSpec((1, D), lambda i: (0, 0))
    return pl.pallas_call(
        body, grid=(s // ta,),
        in_specs=[pl.BlockSpec((ta, D), lambda i: (i, 0)), pl.BlockSpec((ta, D), lambda i: (i, 1)),
                  pl.BlockSpec((ta, D), lambda i: (i, 2)), row, row,
                  pl.BlockSpec((NG, CH, CH), lambda i: (0, 0, 0)), pl.BlockSpec((CH, D), lambda i: (0, 0))],
        out_specs=pl.BlockSpec((ta, D), lambda i: (i, 0)),
        out_shape=jax.ShapeDtypeStruct((s, D), bf16), name=name, compiler_params=_cp(("parallel",)),
    )(proj_a, proj_a, proj_a, lnw, lnb, wsm, bsf)


def _branch_a_bwd(proj_a, dya, lnw, lnb, wsm, wsm_t, bsf, name):
    s = proj_a.shape[0]
    ta = min(256, s)

    def body(pu_ref, pv_ref, pz_ref, dya_ref, lnw_ref, lnb_ref, w_ref, wt_ref, bs_ref,
             dp_ref, dws_ref, dbs_ref, dlnw_ref, dlnb_ref):
        @pl.when(pl.program_id(0) == 0)
        def _():
            dws_ref[...] = jnp.zeros_like(dws_ref)
            dbs_ref[...] = jnp.zeros_like(dbs_ref)
            dlnw_ref[...] = jnp.zeros_like(dlnw_ref)
            dlnb_ref[...] = jnp.zeros_like(dlnb_ref)
        for c in range(ta // CH):
            rows = pl.ds(c * CH, CH)
            u, du = _gelu_and_grad(pu_ref[rows, :])
            v, dv_act = _gelu_and_grad(pv_ref[rows, :])
            zg, dzg = _silu_and_grad(pz_ref[rows, :])
            vh, rstd = _ln_stats(v)
            vl = (vh * lnw_ref[...] + lnb_ref[...]).astype(bf16)
            mixed = _mix(w_ref, vl) + bs_ref[...]
            dy = dya_ref[rows, :].astype(f32)
            dmixed = dy * u * zg
            dp_ref[rows, 0:D] = (dy * mixed * zg * du).astype(bf16)
            dp_ref[rows, 2 * D:3 * D] = (dy * u * mixed * dzg).astype(bf16)
            dmb = dmixed.astype(bf16)
            dbs_ref[...] += dmixed
            dvl = _mix(wt_ref, dmb)
            for g in range(NG):
                cols = slice(g * CH, (g + 1) * CH)
                dws_ref[g] += _dot_nt(dmb[:, cols], vl[:, cols])
            dlnw_ref[...] += jnp.sum(dvl * vh, axis=0, keepdims=True)
            dlnb_ref[...] += jnp.sum(dvl, axis=0, keepdims=True)
            dvh = dvl * lnw_ref[...]
            dv = rstd * (dvh - jnp.mean(dvh, axis=-1, keepdims=True) - vh * jnp.mean(dvh * vh, axis=-1, keepdims=True))
            dp_ref[rows, D:2 * D] = (dv * dv_act).astype(bf16)

    row = pl.BlockSpec((1, D), lambda i: (0, 0))
    wspec = pl.BlockSpec((NG, CH, CH), lambda i: (0, 0, 0))
    full = pl.BlockSpec((CH, D), lambda i: (0, 0))
    return pl.pallas_call(
        body, grid=(s // ta,),
        in_specs=[pl.BlockSpec((ta, D), lambda i: (i, 0)), pl.BlockSpec((ta, D), lambda i: (i, 1)),
                  pl.BlockSpec((ta, D), lambda i: (i, 2)), pl.BlockSpec((ta, D), lambda i: (i, 0)),
                  row, row, wspec, wspec, full],
        out_specs=[pl.BlockSpec((ta, PA), lambda i: (i, 0)), wspec, full, row, row],
        out_shape=[jax.ShapeDtypeStruct((s, PA), bf16), jax.ShapeDtypeStruct((NG, CH, CH), f32),
                   jax.ShapeDtypeStruct((CH, D), f32), jax.ShapeDtypeStruct((1, D), f32), jax.ShapeDtypeStruct((1, D), f32)],
        name=name, compiler_params=_cp(("arbitrary",)),
    )(proj_a, proj_a, proj_a, dya, lnw, lnb, wsm, wsm_t, bsf)


GB = GW + 2 * NST


def _group_major(xs, b, c):
    lead = xs.shape[:-1]
    return jnp.concatenate([xs.reshape(lead + (NG, GW)), b.reshape(lead + (NG, NST)), c.reshape(lead + (NG, NST))],
                           axis=-1).reshape(lead + (CONVD,))


def _from_group_major(t):
    lead = t.shape[:-1]
    t = t.reshape(lead + (NG, GB))
    return jnp.concatenate([t[..., 0:GW].reshape(lead + (DIN,)), t[..., GW:GW + NST].reshape(lead + (NG * NST,)),
                            t[..., GW + NST:GB].reshape(lead + (NG * NST,))], axis=-1)


def _rows_from_group_major(t):
    t = t.reshape(NG, GB, t.shape[-1])
    return jnp.concatenate([t[:, 0:GW].reshape(DIN, -1), t[:, GW:GW + NST].reshape(NG * NST, -1),
                            t[:, GW + NST:GB].reshape(NG * NST, -1)], axis=0)


def _shift_rows(x, prev8, j):
    xr = pltpu.roll(x, j, 0)
    fix = pltpu.roll(prev8, j, 0)
    rid = lax.broadcasted_iota(jnp.int32, (8, x.shape[1]), 0)
    top = jnp.where(rid < j, fix, xr[0:8])
    return jnp.concatenate([top, xr[8:]], axis=0)


def _shift_rows_up(d, next8, j):
    dr = pltpu.roll(d, CH - j, 0)
    fix = pltpu.roll(next8, 8 - j, 0)
    rid = lax.broadcasted_iota(jnp.int32, (8, d.shape[1]), 0)
    bot = jnp.where(rid >= 8 - j, fix, dr[CH - 8:CH])
    return jnp.concatenate([dr[0:CH - 8], bot], axis=0)


def _conv_pre(x, prev8, cw_ref, cb_ref, cols):
    shifted = [_shift_rows(x, prev8, j) for j in (1, 2, 3)]
    conv = cb_ref[:, cols] + cw_ref[3:4, cols] * x
    for j in (1, 2, 3):
        conv = conv + cw_ref[3 - j:4 - j, cols] * shifted[j - 1]
    return conv, shifted


def _tril_mask():
    return lax.broadcasted_iota(jnp.int32, (CH, CH), 0) >= lax.broadcasted_iota(jnp.int32, (CH, CH), 1)


def _sum_all(v):
    return jnp.sum(jnp.sum(v, axis=0, keepdims=True), axis=1, keepdims=True)


def _lanes(g, width, base=0):
    return pl.ds(pl.multiple_of(base + g * width, width), width)


def _branch_b_fwd(proj_b, cw, cb, dtb, a_row, dkc, snw, name):
    s = proj_b.shape[0]
    nc = s // CH

    def body(xbc_ref, sz_ref, dtr_ref, cw_ref, cb_ref, dtb_ref, a_ref, dkc_ref, snw_ref,
             yb_ref, y_ref, st_ref, prev8, state, acst_s):
        @pl.when(pl.program_id(0) == 0)
        def _():
            prev8[...] = jnp.zeros_like(prev8)
            state[...] = jnp.zeros_like(state)
        st_ref[0] = state[...].astype(bf16)
        mask = _tril_mask()
        dt_all = _softplus(dtr_ref[:, 0:CH] + dtb_ref[...])
        acs_all = _dot_hi(mask.astype(f32), dt_all * a_ref[...])
        acst_s[...] = acs_all.T

        def group(g, carry):
            cols = _lanes(g, GB)
            gcols = _lanes(g, GW)
            x = xbc_ref[:, cols]
            conv, _ = _conv_pre(x, prev8[:, cols], cw_ref, cb_ref, cols)
            prev8[:, cols] = x[CH - 8:CH]
            xc = conv * _sigmoid(conv)
            xs = xc[:, 0:GW]
            bg = xc[:, GW:GW + NST].astype(bf16)
            cg = xc[:, GW + NST:GB].astype(bf16)
            back = lax.rem(CH - HPG * g, CH)
            dt = pltpu.roll(dt_all, back, 1)
            acs = pltpu.roll(acs_all, back, 1)
            cbm = _dot_nt(cg, bg)
            dkc_g = dkc_ref[:, gcols]
            y_parts = []
            for r in range(HPG):
                colb = jnp.broadcast_to(acs[:, r:r + 1], (CH, CH))
                row = acst_s[pl.ds(g * HPG + r, 1), :]
                lmat = jnp.exp(jnp.where(mask, colb - row, -jnp.inf))
                xr = xs[:, r * HD:(r + 1) * HD]
                xd = xr * dt[:, r:r + 1]
                sp = state[g * HPG + r]
                col = colb[:, 0:HD]
                alast = colb[CH - 1:CH, 0:HD]
                y_r = _dot((cbm * lmat).astype(bf16), xd.astype(bf16))
                y_r = y_r + jnp.exp(col) * _dot_nt(cg, sp.astype(bf16))
                y_parts.append(y_r + xr * dkc_g[:, r * HD:(r + 1) * HD])
                cs = _dot_tn((xd * jnp.exp(alast - col)).astype(bf16), bg)
                state[g * HPG + r] = jnp.exp(colb[CH - 1:CH, :]) * sp + cs
            y = jnp.concatenate(y_parts, axis=1)
            szv = sz_ref[:, gcols]
            yz = y * (szv * _sigmoid(szv))
            rr = lax.rsqrt(jnp.mean(yz * yz, axis=-1, keepdims=True) + EPS)
            yb_ref[:, gcols] = (yz * rr * snw_ref[:, gcols]).astype(bf16)
            y_ref[:, gcols] = y.astype(bf16)
            return carry

        lax.fori_loop(0, NG, group, 0)

    const2 = lambda c: (0, 0)
    return pl.pallas_call(
        body, grid=(nc,),
        in_specs=[pl.BlockSpec((CH, CONVD), lambda c: (c, 0)), pl.BlockSpec((CH, DIN), lambda c: (c, CONVD // DIN)),
                  pl.BlockSpec((CH, 256), lambda c: (c, (CONVD + DIN) // 256)),
                  pl.BlockSpec((4, CONVD), const2), pl.BlockSpec((1, CONVD), const2),
                  pl.BlockSpec((1, CH), const2), pl.BlockSpec((1, CH), const2),
                  pl.BlockSpec((1, DIN), const2), pl.BlockSpec((1, DIN), const2)],
        out_specs=[pl.BlockSpec((CH, DIN), lambda c: (c, 0)), pl.BlockSpec((CH, DIN), lambda c: (c, 0)),
                   pl.BlockSpec((1, NH, HD, NST), lambda c: (c, 0, 0, 0))],
        out_shape=[jax.ShapeDtypeStruct((s, DIN), bf16), jax.ShapeDtypeStruct((s, DIN), bf16),
                   jax.ShapeDtypeStruct((nc, NH, HD, NST), bf16)],
        scratch_shapes=[pltpu.VMEM((8, CONVD), f32), pltpu.VMEM((NH, HD, NST), f32), pltpu.VMEM((CH, CH), f32)],
        name=name, compiler_params=_cp(("arbitrary",)),
    )(proj_b, proj_b, proj_b, cw, cb, dtb, a_row, dkc, snw)


def _branch_b_bwd(proj_b, dyb, y_sv, states, cw, cb, dtb, a_row, dkc, snw, ind, name):
    s = proj_b.shape[0]
    nc = s // CH

    def body(xbc_ref, xprev_ref, sz_ref, dtr_ref, dyb_ref, y_ref, st_ref, cw_ref, cb_ref, dtb_ref, a_ref, dkc_ref,
             snw_ref, ind_ref, dp_ref, dcw_ref, dcb_ref, ddtb_ref, dal_ref, ddk_ref, dsnw_ref,
             dstate, dnext8, acst_s, dacs_acc, q2_acc):
        step = pl.program_id(0)
        cc = nc - 1 - step

        @pl.when(step == 0)
        def _():
            dstate[...] = jnp.zeros_like(dstate)
            dnext8[...] = jnp.zeros_like(dnext8)
            dcw_ref[...] = jnp.zeros_like(dcw_ref)
            dcb_ref[...] = jnp.zeros_like(dcb_ref)
            ddtb_ref[...] = jnp.zeros_like(ddtb_ref)
            dal_ref[...] = jnp.zeros_like(dal_ref)
            ddk_ref[...] = jnp.zeros_like(ddk_ref)
            dsnw_ref[...] = jnp.zeros_like(dsnw_ref)

        first = jnp.where(cc > 0, 1.0, 0.0)
        dacs_acc[...] = jnp.zeros_like(dacs_acc)
        q2_acc[...] = jnp.zeros_like(q2_acc)
        mask = _tril_mask()
        tri_t = (lax.broadcasted_iota(jnp.int32, (CH, CH), 0) <= lax.broadcasted_iota(jnp.int32, (CH, CH), 1)).astype(f32)
        lane1 = lax.broadcasted_iota(jnp.int32, (1, CH), 1)
        is_last = lax.broadcasted_iota(jnp.int32, (CH, 1), 0) == CH - 1
        z_all = dtr_ref[:, 0:CH] + dtb_ref[...]
        dt_all = _softplus(z_all)
        adt_all = dt_all * a_ref[...]
        acs_all = _dot_hi(mask.astype(f32), adt_all)
        acst_s[...] = acs_all.T

        def ind_sum(v):
            hi = v.astype(bf16)
            lo = (v - hi.astype(f32)).astype(bf16)
            return _dot(hi, ind_ref[...]) + _dot(lo, ind_ref[...])

        def group(g, carry):
            cols = _lanes(g, GB)
            gcols = _lanes(g, GW)
            x = xbc_ref[:, cols]
            conv, shifted = _conv_pre(x, xprev_ref[:, cols] * first, cw_ref, cb_ref, cols)
            sg = _sigmoid(conv)
            xc = conv * sg
            xs = xc[:, 0:GW]
            bg = xc[:, GW:GW + NST].astype(bf16)
            cg = xc[:, GW + NST:GB].astype(bf16)

            y = y_ref[:, gcols].astype(f32)
            silu_sz, dsilu_sz = _silu_and_grad(sz_ref[:, gcols])
            yz = y * silu_sz
            rr = lax.rsqrt(jnp.mean(yz * yz, axis=-1, keepdims=True) + EPS)
            dyb_g = dyb_ref[:, gcols].astype(f32)
            w = dyb_g * snw_ref[:, gcols]
            dsnw_ref[:, gcols] += jnp.sum(dyb_g * yz * rr, axis=0, keepdims=True)
            dyz = rr * w - yz * (rr * rr * rr) * jnp.mean(w * yz, axis=-1, keepdims=True)
            dp_ref[:, _lanes(g, GW, CONVD)] = (dyz * y * dsilu_sz).astype(bf16)
            dy_g = dyz * silu_sz
            ddk_ref[:, gcols] += jnp.sum(dy_g * xs, axis=0, keepdims=True)

            back = lax.rem(CH - HPG * g, CH)
            dt = pltpu.roll(dt_all, back, 1)
            acs = pltpu.roll(acs_all, back, 1)
            cbm = _dot_nt(cg, bg)
            d_cb = jnp.zeros((CH, CH), f32)
            d_bg = jnp.zeros((CH, NST), f32)
            d_cg = jnp.zeros((CH, NST), f32)
            lastrow = jnp.zeros((1, CH), f32)
            dxd_parts, dxs_parts, t_parts = [], [], []
            for r in range(HPG):
                h = g * HPG + r
                colb = jnp.broadcast_to(acs[:, r:r + 1], (CH, CH))
                row = acst_s[pl.ds(h, 1), :]
                lmat = jnp.exp(jnp.where(mask, colb - row, -jnp.inf))
                mmat_b = (cbm * lmat).astype(bf16)
                dtc = dt[:, r:r + 1]
                xd = xs[:, r * HD:(r + 1) * HD] * dtc
                col = colb[:, 0:HD]
                alast = colb[CH - 1:CH, 0:HD]
                dte = jnp.exp(alast - col)
                ea = jnp.exp(col)
                cd = jnp.exp(colb[CH - 1:CH, :])
                sp = st_ref[0, h]
                dsn = dstate[h]
                dsn_b = dsn.astype(bf16)
                dyr = dy_g[:, r * HD:(r + 1) * HD]
                dyr_b = dyr.astype(bf16)
                dye_b = (dyr * ea).astype(bf16)
                d_cg = d_cg + _dot(dye_b, sp)
                dxde = _dot_nt(bg, dsn_b)
                xdte = xd * dte
                xd_b = xd.astype(bf16)
                d_bg = d_bg + _dot(xdte.astype(bf16), dsn_b)
                dxd_diag = _dot_tn(mmat_b, dyr_b)
                dxd = dxde * dte + dxd_diag
                d_cb = d_cb + _dot_nt(dyr_b, xd_b) * lmat
                t_parts.append(dyr_b.astype(f32) * _dot(mmat_b, xd_b) + dyr * (ea * _dot_nt(cg, sp))
                               - xd_b.astype(f32) * dxd_diag - dxde * xdte)
                lastrow = jnp.where(lane1 == r, _sum_all(dsn * sp.astype(f32)) * cd + _sum_all(dxde * xdte), lastrow)
                dstate[h] = cd * dsn + _dot_tn(dye_b, cg)
                dxd_parts.append(dxd)
                dxs_parts.append(dxd * dtc)
            d_cb_b = d_cb.astype(bf16)
            d_bg = d_bg + _dot_tn(d_cb_b, cg)
            d_cg = d_cg + _dot(d_cb_b, bg)
            q2 = ind_sum(jnp.concatenate(dxd_parts, axis=1) * xs)
            dacs = ind_sum(jnp.concatenate(t_parts, axis=1)) + jnp.where(is_last, lastrow, 0.0)
            dacs_acc[...] += pltpu.roll(dacs, HPG * g, 1)
            q2_acc[...] += pltpu.roll(q2, HPG * g, 1)
            dxs = jnp.concatenate(dxs_parts, axis=1) + dy_g * dkc_ref[:, gcols]

            dconv = jnp.concatenate([dxs, d_bg, d_cg], axis=1) * (sg * (1.0 + conv * (1.0 - sg)))
            dcb_ref[:, cols] += jnp.sum(dconv, axis=0, keepdims=True)
            dcw_ref[3:4, cols] += jnp.sum(dconv * x, axis=0, keepdims=True)
            dx = cw_ref[3:4, cols] * dconv
            nxt = dnext8[:, cols]
            for j in (1, 2, 3):
                dcw_ref[3 - j:4 - j, cols] += jnp.sum(dconv * shifted[j - 1], axis=0, keepdims=True)
                dx = dx + cw_ref[3 - j:4 - j, cols] * _shift_rows_up(dconv, nxt, j)
            dnext8[:, cols] = dconv[0:8]
            dp_ref[:, cols] = dx.astype(bf16)
            return carry

        lax.fori_loop(0, NG, group, 0)

        dadt = _dot_hi(tri_t, dacs_acc[...])
        dal_ref[...] += jnp.sum(dadt * adt_all, axis=0, keepdims=True)
        ddz = (dadt * a_ref[...] + q2_acc[...]) * _sigmoid(z_all)
        ddtb_ref[...] += jnp.sum(ddz, axis=0, keepdims=True)
        dp_ref[:, CONVD + DIN:CONVD + DIN + CH] = ddz.astype(bf16)
        dp_ref[:, CONVD + DIN + CH:PB] = jnp.zeros((CH, PB - CONVD - DIN - CH), bf16)

    const2 = lambda c: (0, 0)
    rev = lambda c: (nc - 1 - c, 0)
    return pl.pallas_call(
        body, grid=(nc,),
        in_specs=[pl.BlockSpec((CH, CONVD), rev),
                  pl.BlockSpec((8, CONVD), lambda c: (jnp.maximum((nc - 1 - c) * (CH // 8) - 1, 0), 0)),
                  pl.BlockSpec((CH, DIN), lambda c: (nc - 1 - c, CONVD // DIN)),
                  pl.BlockSpec((CH, 256), lambda c: (nc - 1 - c, (CONVD + DIN) // 256)),
                  pl.BlockSpec((CH, DIN), rev), pl.BlockSpec((CH, DIN), rev),
                  pl.BlockSpec((1, NH, HD, NST), lambda c: (nc - 1 - c, 0, 0, 0)),
                  pl.BlockSpec((4, CONVD), const2), pl.BlockSpec((1, CONVD), const2),
                  pl.BlockSpec((1, CH), const2), pl.BlockSpec((1, CH), const2),
                  pl.BlockSpec((1, DIN), const2), pl.BlockSpec((1, DIN), const2), pl.BlockSpec((GW, CH), const2)],
        out_specs=[pl.BlockSpec((CH, PB), rev), pl.BlockSpec((4, CONVD), const2), pl.BlockSpec((1, CONVD), const2),
                   pl.BlockSpec((1, CH), const2), pl.BlockSpec((1, CH), const2), pl.BlockSpec((1, DIN), const2),
                   pl.BlockSpec((1, DIN), const2)],
        out_shape=[jax.ShapeDtypeStruct((s, PB), bf16), jax.ShapeDtypeStruct((4, CONVD), f32),
                   jax.ShapeDtypeStruct((1, CONVD), f32), jax.ShapeDtypeStruct((1, CH), f32),
                   jax.ShapeDtypeStruct((1, CH), f32), jax.ShapeDtypeStruct((1, DIN), f32),
                   jax.ShapeDtypeStruct((1, DIN), f32)],
        scratch_shapes=[pltpu.VMEM((NH, HD, NST), f32), pltpu.VMEM((8, CONVD), f32), pltpu.VMEM((CH, CH), f32),
                        pltpu.VMEM((CH, CH), f32), pltpu.VMEM((CH, CH), f32)],
        name=name, compiler_params=_cp(("arbitrary",)),
    )(proj_b, proj_b, proj_b, proj_b, dyb, y_sv, states, cw, cb, dtb, a_row, dkc, snw, ind)


def _merge_fwd(ya, yb, proj_g, x, gate, wa, wb, wo, name):
    s = x.shape[0]
    ts = min(512, s)

    def body(ya_ref, yb_ref, ga_ref, gb_ref, x_ref, gate_ref, wa_ref, wb_ref, wo_ref, xo_ref, pa_ref, pb_ref, mg_ref, o_ref):
        pa = _dot(ya_ref[...], wa_ref[...])
        pb = _dot(yb_ref[...], wb_ref[...])
        mg = (_sigmoid(ga_ref[...]) * pa + _sigmoid(gb_ref[...]) * pb).astype(bf16)
        o = _dot(mg, wo_ref[...])
        xo_ref[...] = x_ref[...] + gate_ref[...] * o
        pa_ref[...] = pa.astype(bf16)
        pb_ref[...] = pb.astype(bf16)
        mg_ref[...] = mg
        o_ref[...] = o.astype(bf16)

    tile = pl.BlockSpec((ts, D), lambda i: (i, 0))
    const = lambda i: (0, 0)
    act = jax.ShapeDtypeStruct((s, D), bf16)
    return pl.pallas_call(
        body, grid=(s // ts,),
        in_specs=[tile, pl.BlockSpec((ts, DIN), lambda i: (i, 0)), tile, pl.BlockSpec((ts, D), lambda i: (i, 1)), tile,
                  pl.BlockSpec((1, D), const), pl.BlockSpec((D, D), const), pl.BlockSpec((DIN, D), const),
                  pl.BlockSpec((D, D), const)],
        out_specs=[tile, tile, tile, tile, tile],
        out_shape=[jax.ShapeDtypeStruct((s, D), f32), act, act, act, act],
        name=name, compiler_params=_cp(("parallel",)),
    )(ya, yb, proj_g, proj_g, x, gate, wa, wb, wo)


def _merge_bwd(dxo, gate, o_sv, pa_sv, pb_sv, proj_g, wo_t, wa_t, wb_t, name):
    s = dxo.shape[0]
    ts = min(512, s)

    def body(dxo_ref, gate_ref, o_ref, pa_ref, pb_ref, ga_ref, gb_ref, wot_ref, wat_ref, wbt_ref,
             do_ref, dpa_ref, dpb_ref, dg_ref, dya_ref, dyb_ref, dgate_ref):
        @pl.when(pl.program_id(0) == 0)
        def _():
            dgate_ref[...] = jnp.zeros_like(dgate_ref)
        dxo_v = dxo_ref[...]
        dgate_ref[...] += jnp.sum(dxo_v * o_ref[...].astype(f32), axis=0, keepdims=True)
        do = (dxo_v * gate_ref[...]).astype(bf16)
        do_ref[...] = do
        dmg = _dot(do, wot_ref[...])
        sa = _sigmoid(ga_ref[...])
        sb = _sigmoid(gb_ref[...])
        dpa = (dmg * sa).astype(bf16)
        dpb = (dmg * sb).astype(bf16)
        dpa_ref[...] = dpa
        dpb_ref[...] = dpb
        dg_ref[:, 0:D] = (dmg * pa_ref[...].astype(f32) * sa * (1.0 - sa)).astype(bf16)
        dg_ref[:, D:2 * D] = (dmg * pb_ref[...].astype(f32) * sb * (1.0 - sb)).astype(bf16)
        dya_ref[...] = _dot(dpa, wat_ref[...]).astype(bf16)
        dyb_ref[...] = _dot(dpb, wbt_ref[...]).astype(bf16)

    tile = pl.BlockSpec((ts, D), lambda i: (i, 0))
    const = lambda i: (0, 0)
    act = jax.ShapeDtypeStruct((s, D), bf16)
    return pl.pallas_call(
        body, grid=(s // ts,),
        in_specs=[tile, pl.BlockSpec((1, D), const), tile, tile, tile, tile, pl.BlockSpec((ts, D), lambda i: (i, 1)),
                  pl.BlockSpec((D, D), const), pl.BlockSpec((D, D), const), pl.BlockSpec((D, DIN), const)],
        out_specs=[tile, tile, tile, pl.BlockSpec((ts, PG), lambda i: (i, 0)), tile, pl.BlockSpec((ts, DIN), lambda i: (i, 0)),
                   pl.BlockSpec((1, D), const)],
        out_shape=[act, act, act, jax.ShapeDtypeStruct((s, PG), bf16), act, jax.ShapeDtypeStruct((s, DIN), bf16),
                   jax.ShapeDtypeStruct((1, D), f32)],
        name=name, compiler_params=_cp(("arbitrary",)),
    )(dxo, gate, o_sv, pa_sv, pb_sv, proj_g, proj_g, wo_t, wa_t, wb_t)


def _final_loss(x, target, fnw, name):
    s = x.shape[0]
    ts = min(512, s)

    def body(x_ref, t_ref, w_ref, loss_ref, dx_ref, dw_ref):
        @pl.when(pl.program_id(0) == 0)
        def _():
            loss_ref[...] = jnp.zeros_like(loss_ref)
            dw_ref[...] = jnp.zeros_like(dw_ref)
        xv = x_ref[...]
        r = lax.rsqrt(jnp.mean(xv * xv, axis=-1, keepdims=True) + EPS)
        xn = xv * r
        err = xn * w_ref[...] - t_ref[...]
        part = jnp.sum(err * err, axis=0, keepdims=True)
        acc = part[:, 0:128]
        for k in range(1, D // 128):
            acc = acc + part[:, k * 128:(k + 1) * 128]
        loss_ref[0:1, :] += acc * (0.5 / D)
        dy = err * (1.0 / D)
        dw_ref[...] += jnp.sum(dy * xn, axis=0, keepdims=True)
        dxn = dy * w_ref[...]
        dx_ref[...] = r * (dxn - xn * jnp.mean(dxn * xn, axis=-1, keepdims=True))

    tile = pl.BlockSpec((ts, D), lambda i: (i, 0))
    row = pl.BlockSpec((1, D), lambda i: (0, 0))
    return pl.pallas_call(
        body, grid=(s // ts,), in_specs=[tile, tile, row],
        out_specs=[pl.BlockSpec((8, 128), lambda i: (0, 0)), tile, row],
        out_shape=[jax.ShapeDtypeStruct((8, 128), f32), jax.ShapeDtypeStruct((s, D), f32), jax.ShapeDtypeStruct((1, D), f32)],
        name=name, compiler_params=_cp(("arbitrary",)),
    )(x, target, fnw)


def _layer_operands(w_in, conv_w, wa, wb, wo, norm_w, gm_ln_w, gm_ln_b, gm_ws, gm_bs, conv_b, dt_bias, a_log, d_skip, ssm_norm_w):
    w_xbc = _group_major(w_in[:, 5120:7168], w_in[:, 7168:8192], w_in[:, 8192:9216])
    w_b = jnp.concatenate([w_xbc, w_in[:, 3072:5120], w_in[:, 9216:9248], jnp.zeros((D, 224), bf16)], axis=1)
    w_a = w_in[:, 0:3072]
    w_g = w_in[:, 9248:N_IN]
    tril = jnp.tril(jnp.ones((CH, CH), bool))
    wsm = jnp.where(tril[None], gm_ws, 0.0).astype(bf16)

    def heads_row(v):
        return jnp.pad(v, (0, CH - NH)).reshape(1, CH)

    return dict(
        w_b=w_b, w_a=w_a, w_g=w_g, w_b_t=w_b.T, w_a_t=w_a.T, w_g_t=w_g.T,
        wa=wa, wb=wb, wo=wo, wa_t=wa.T, wb_t=wb.T, wo_t=wo.T,
        norm_w=norm_w.reshape(1, D), lnw=gm_ln_w.reshape(1, D), lnb=gm_ln_b.reshape(1, D),
        wsm=wsm, wsm_t=jnp.swapaxes(wsm, 1, 2), bsf=jnp.repeat(gm_bs.T, CH, axis=1),
        cw=_group_major(conv_w[:, 0:DIN], conv_w[:, DIN:DIN + NG * NST], conv_w[:, DIN + NG * NST:CONVD]),
        cb=_group_major(conv_b[0:DIN], conv_b[DIN:DIN + NG * NST], conv_b[DIN + NG * NST:CONVD]).reshape(1, CONVD),
        dtb=heads_row(dt_bias), a_row=heads_row(-jnp.exp(a_log)),
        snw=ssm_norm_w.reshape(1, DIN), dkc=jnp.repeat(d_skip, HD).reshape(1, DIN),
        ind=(jnp.arange(GW)[:, None] // HD == jnp.arange(CH)[None, :]).astype(bf16),
    )


def _layer_fwd(x, shift, scale, gate, p, tag):
    h = _rmsmod_fwd(x, p["norm_w"], scale, shift, f"rmsmod_fwd{tag}")
    proj_b = _mm(h, p["w_b"], f32, f"proj_b{tag}")
    proj_a = _mm(h, p["w_a"], f32, f"proj_a{tag}")
    proj_g = _mm(h, p["w_g"], f32, f"proj_g{tag}")
    ya = _branch_a_fwd(proj_a, p["lnw"], p["lnb"], p["wsm"], p["bsf"], f"branch_a_fwd{tag}")
    yb, y_sv, states = _branch_b_fwd(proj_b, p["cw"], p["cb"], p["dtb"], p["a_row"], p["dkc"], p["snw"], f"branch_b_fwd{tag}")
    x_out, pa, pb, mg, o = _merge_fwd(ya, yb, proj_g, x, gate, p["wa"], p["wb"], p["wo"], f"merge_fwd{tag}")
    saved = dict(x=x, h=h, proj_b=proj_b, proj_a=proj_a, proj_g=proj_g, ya=ya, yb=yb, y=y_sv, states=states,
                 pa=pa, pb=pb, mg=mg, o=o, scale=scale, gate=gate)
    return x_out, saved


def _layer_bwd(dxo, sv, p, tag):
    do, dpa, dpb, dg, dya, dyb, dgate = _merge_bwd(dxo, sv["gate"], sv["o"], sv["pa"], sv["pb"], sv["proj_g"],
                                                   p["wo_t"], p["wa_t"], p["wb_t"], f"merge_bwd{tag}")
    d_wo = _mm_tn(sv["mg"], do, f"d_wo{tag}")
    d_wa = _mm_tn(sv["ya"], dpa, f"d_wa{tag}")
    d_wb = _mm_tn(sv["yb"], dpb, f"d_wb{tag}")
    da, dws, dbs, dlnw, dlnb = _branch_a_bwd(sv["proj_a"], dya, p["lnw"], p["lnb"], p["wsm"], p["wsm_t"], p["bsf"],
                                             f"branch_a_bwd{tag}")
    db, dcw, dcb, ddtb, dal, ddk, dsnw = _branch_b_bwd(sv["proj_b"], dyb, sv["y"], sv["states"], p["cw"], p["cb"], p["dtb"],
                                                       p["a_row"], p["dkc"], p["snw"], p["ind"], f"branch_b_bwd{tag}")
    dh = _mm(db, p["w_b_t"], f32, f"dh_b{tag}")
    dh = _mm(da, p["w_a_t"], f32, f"dh_a{tag}", c_in=dh)
    dh = _mm(dg, p["w_g_t"], f32, f"dh_g{tag}", c_in=dh)
    d_w_b_t = _mm_tn(db, sv["h"], f"d_w_b{tag}")
    d_w_a_t = _mm_tn(da, sv["h"], f"d_w_a{tag}")
    d_w_g_t = _mm_tn(dg, sv["h"], f"d_w_g{tag}")
    dx, dscale, dshift, dnw = _rmsmod_bwd(dh, sv["x"], dxo, p["norm_w"], sv["scale"], f"rmsmod_bwd{tag}")
    d_w_in_t = jnp.concatenate([d_w_a_t, d_w_b_t[CONVD:CONVD + DIN], _rows_from_group_major(d_w_b_t[0:CONVD]),
                                d_w_b_t[CONVD + DIN:CONVD + DIN + NH], d_w_g_t], axis=0)
    tril = jnp.tril(jnp.ones((CH, CH), bool))
    heads = lambda v: v[0, 0:NH]
    grads = dict(
        w_in_t=d_w_in_t, w_proj_a=d_wa, w_proj_b=d_wb, w_out=d_wo, conv_w=_from_group_major(dcw), conv_b=_from_group_major(dcb).reshape(CONVD),
        norm_w=dnw.reshape(D), gm_ln_w=dlnw.reshape(D), gm_ln_b=dlnb.reshape(D),
        gm_ws=jnp.where(tril[None], dws, 0.0), gm_bs=dbs.reshape(CH, NG, CH).sum(-1).T,
        dt_bias=heads(ddtb), a_log=heads(dal), d_skip=ddk.reshape(NH, HD).sum(-1), ssm_norm_w=dsnw.reshape(DIN),
        mod=jnp.concatenate([dshift, dscale, dgate], axis=1).reshape(3 * D),
    )
    return dx, grads


def _local_step(x, target, mods, layer_ops, fnw):
    saved = []
    for l in range(DEPTH):
        shift, scale, gate = mods[l]
        x, sv = _layer_fwd(x, shift, scale, gate, layer_ops[l], f"_l{l}")
        saved.append(sv)
    loss_parts, dx, dfnw = _final_loss(x, target, fnw.reshape(1, D), "final_loss")
    grads = [None] * DEPTH
    for l in reversed(range(DEPTH)):
        dx, grads[l] = _layer_bwd(dx, saved[l], layer_ops[l], f"_l{l}")
    return jnp.sum(loss_parts), dx, grads, dfnw.reshape(D)


ADA_COLS = 3 * D // NSHARD


def _ada_fwd(c_all, ada_w, ada_b_cols, name):
    def body(c_ref, w_ref, b_ref, o_ref):
        cv = c_ref[...]
        sc = cv * _sigmoid(cv)
        for l in range(DEPTH):
            o_ref[l] = _dot_hi(sc, w_ref[l]) + b_ref[l]

    return pl.pallas_call(body, out_shape=jax.ShapeDtypeStruct((DEPTH, 8, ADA_COLS), f32), name=name,
                          compiler_params=_cp(None))(c_all, ada_w, ada_b_cols)


def _adam_math(w, g, m, v):
    m = ADAM_B1 * m + (1.0 - ADAM_B1) * g
    v = ADAM_B2 * v + (1.0 - ADAM_B2) * (g * g)
    m_hat = m / (1.0 - ADAM_B1 ** ADAM_STEP)
    v_hat = v / (1.0 - ADAM_B2 ** ADAM_STEP)
    delta = -ADAM_LR * (m_hat / (jnp.sqrt(v_hat) + ADAM_EPS) + ADAM_WD * w)
    return delta, m, v


def _ada_bwd_adamw(c_all, dmod_cols, w, m, v, name):
    tr = 256

    def body(c_ref, dm_ref, w_ref, m_ref, v_ref, g_ref, d_ref, nm_ref, nv_ref):
        cv = c_ref[...]
        sc = cv * _sigmoid(cv)
        g = lax.dot_general(sc, dm_ref[0], (((0,), (0,)), ((), ())), precision=_HI, preferred_element_type=f32)
        g_ref[0] = g
        d_ref[0], nm_ref[0], nv_ref[0] = _adam_math(w_ref[0], g, m_ref[0], v_ref[0])

    blk = pl.BlockSpec((1, tr, ADA_COLS), lambda l, i: (l, i, 0))
    shp = jax.ShapeDtypeStruct((DEPTH, D, ADA_COLS), f32)
    return pl.pallas_call(
        body, grid=(DEPTH, D // tr),
        in_specs=[pl.BlockSpec((8, tr), lambda l, i: (0, i)), pl.BlockSpec((1, 8, ADA_COLS), lambda l, i: (l, 0, 0)), blk, blk, blk],
        out_specs=[blk, blk, blk, blk], out_shape=[shp, shp, shp, shp], name=name, compiler_params=_cp(("parallel", "parallel")),
    )(c_all, dmod_cols, w, m, v)


def _adamw(w, g, m, v, name):
    r, c = w.shape
    tr = _pick(r, (256, 128, 64, 32, 16, 8))

    def body(w_ref, g_ref, m_ref, v_ref, d_ref, nm_ref, nv_ref):
        d_ref[...], nm_ref[...], nv_ref[...] = _adam_math(w_ref[...], g_ref[...], m_ref[...], v_ref[...])

    blk = pl.BlockSpec((tr, c), lambda i: (i, 0))
    shp = jax.ShapeDtypeStruct((r, c), f32)
    return pl.pallas_call(body, grid=(r // tr,), in_specs=[blk] * 4, out_specs=[blk] * 3, out_shape=[shp] * 3, name=name,
                          compiler_params=_cp(("parallel",)))(w, g, m, v)


def _tile2(r, c):
    return _pick(r, (256, 128, 64, 32, 16, 8)), _pick(c, (1024,))


def _adamw_layers(w, g_mine, g_other, m, v, c_idx, name):
    _, r, c = w.shape
    tr, tc = _tile2(r, c)

    def body(ci_ref, w_ref, gm_ref, go_ref, m_ref, v_ref, g_ref, d_ref, nm_ref, nv_ref):
        def update(g):
            g_ref[0] = g
            d_ref[0], nm_ref[0], nv_ref[0] = _adam_math(w_ref[0], g, m_ref[0], v_ref[0])

        mine = pl.program_id(0) == ci_ref[0]

        @pl.when(mine)
        def _():
            update(gm_ref[...])

        @pl.when(jnp.logical_not(mine))
        def _():
            update(go_ref[...])

    blk = pl.BlockSpec((1, tr, tc), lambda l, i, j, ci: (l, i, j))
    gblk = pl.BlockSpec((tr, tc), lambda l, i, j, ci: (i, j))
    shp = jax.ShapeDtypeStruct(w.shape, f32)
    return pl.pallas_call(
        body,
        grid_spec=pltpu.PrefetchScalarGridSpec(num_scalar_prefetch=1, grid=(DEPTH, r // tr, c // tc),
                                               in_specs=[blk, gblk, gblk, blk, blk], out_specs=[blk, blk, blk, blk]),
        out_shape=[shp, shp, shp, shp], name=name, compiler_params=_cp(("parallel", "parallel", "parallel")),
    )(c_idx, w, g_mine, g_other, m, v)


_MESH = pl.DeviceIdType.MESH
_AXES = ("x", "y", "c")
_HBM = pl.BlockSpec(memory_space=pltpu.HBM)


def _my_place():
    return tuple(lax.axis_index(a) for a in _AXES)


def _allreduce(buf, axes, name):
    r = buf.shape[0]
    n = len(axes)

    def body(x_ref, o_ref, rbuf, ssem, rsem):
        me = dict(zip(_AXES, _my_place()))
        o_ref[...] = x_ref[...]
        for k, ax in enumerate(axes):
            peer = tuple(1 - me[a] if a == ax else me[a] for a in _AXES)
            cp = pltpu.make_async_remote_copy(src_ref=o_ref, dst_ref=rbuf.at[k], send_sem=ssem.at[k], recv_sem=rsem.at[k],
                                              device_id=peer, device_id_type=_MESH)
            cp.start()
            cp.wait()
            o_ref[...] = o_ref[...] + rbuf[k]

    vm = pl.BlockSpec(memory_space=pltpu.VMEM)
    return pl.pallas_call(
        body, out_shape=jax.ShapeDtypeStruct((r, 128), f32), in_specs=[vm], out_specs=vm,
        scratch_shapes=[pltpu.VMEM((n, r, 128), f32), pltpu.SemaphoreType.DMA((n,)), pltpu.SemaphoreType.DMA((n,))],
        name=name, compiler_params=pltpu.CompilerParams(vmem_limit_bytes=VMEM_LIMIT),
    )(buf)


def _other_chips(x, y):
    return [(1 - x, y), (x, 1 - y), (1 - x, 1 - y)]


def _gather_weights(shards, name):
    na = len(shards)

    def body(*refs):
        ins, outs = refs[:na], refs[na:2 * na]
        ssem, rsem = refs[2 * na:]
        x, y, c = _my_place()
        sibling = (x, y, 1 - c)
        chips = _other_chips(x, y)

        def rcopy(a, src, j, layer, to, idx):
            return pltpu.make_async_remote_copy(src_ref=src, dst_ref=outs[a].at[j, layer], send_sem=ssem.at[idx],
                                                recv_sem=rsem.at[idx], device_id=to, device_id_type=_MESH)

        sent = []
        for j, chip in enumerate(chips):
            for a in range(na):
                cp = rcopy(a, ins[a].at[c], j, c, (*chip, c), j * na + a)
                cp.start()
                sent.append(cp)
        for j, chip in enumerate(chips):
            for a in range(na):
                rcopy(a, ins[a].at[c], j, c, (*chip, c), j * na + a).wait_recv()
                cp = rcopy(a, outs[a].at[j, c], j, c, sibling, (3 + j) * na + a)
                cp.start()
                sent.append(cp)
        for j in range(3):
            for a in range(na):
                rcopy(a, ins[a].at[c], j, 1 - c, sibling, (3 + j) * na + a).wait_recv()
        for cp in sent:
            cp.wait_send()

    out_shape = [jax.ShapeDtypeStruct((3,) + s.shape, s.dtype) for s in shards]
    return pl.pallas_call(
        body, out_shape=out_shape, in_specs=[_HBM] * na, out_specs=[_HBM] * na,
        scratch_shapes=[pltpu.SemaphoreType.DMA((6 * na,)), pltpu.SemaphoreType.DMA((6 * na,))], name=name,
    )(*shards)


def _with_own(own, others, k):
    xi, yi, _ = _my_place()
    d = k ^ (2 * xi + yi)
    j = jnp.where(d == 2, 0, jnp.where(d == 1, 1, 2))
    return jnp.where(d == 0, own, lax.dynamic_index_in_dim(others, j, axis=0, keepdims=False))


def _swap_layers(parts, name):
    na = len(parts)

    def body(*refs):
        ins, outs = refs[:na], refs[na:2 * na]
        ssem, rsem = refs[2 * na:]
        x, y, c = _my_place()
        cps = [pltpu.make_async_remote_copy(src_ref=ins[a].at[1 - c], dst_ref=outs[a], send_sem=ssem.at[a], recv_sem=rsem.at[a],
                                            device_id=(x, y, 1 - c), device_id_type=_MESH) for a in range(na)]
        for cp in cps:
            cp.start()
        for cp in cps:
            cp.wait()

    return pl.pallas_call(
        body, out_shape=[jax.ShapeDtypeStruct(p.shape[1:], p.dtype) for p in parts], in_specs=[_HBM] * na, out_specs=[_HBM] * na,
        scratch_shapes=[pltpu.SemaphoreType.DMA((na,)), pltpu.SemaphoreType.DMA((na,))], name=name,
    )(*parts)


def _scatter_shards(sums, name):
    na = len(sums)

    def body(*refs):
        ins, outs = refs[:na], refs[na:2 * na]
        ssem, rsem = refs[2 * na:]
        x, y, c = _my_place()
        cps = []
        for j, chip in enumerate(_other_chips(x, y)):
            kj = 2 * chip[0] + chip[1]
            for a in range(na):
                cps.append(pltpu.make_async_remote_copy(
                    src_ref=ins[a].at[kj], dst_ref=outs[a].at[j], send_sem=ssem.at[j * na + a], recv_sem=rsem.at[j * na + a],
                    device_id=(*chip, c), device_id_type=_MESH))
        for cp in cps:
            cp.start()
        for cp in cps:
            cp.wait()

    return pl.pallas_call(
        body, out_shape=[jax.ShapeDtypeStruct((3,) + p.shape[1:], p.dtype) for p in sums], in_specs=[_HBM] * na,
        out_specs=[_HBM] * na, scratch_shapes=[pltpu.SemaphoreType.DMA((3 * na,)), pltpu.SemaphoreType.DMA((3 * na,))], name=name,
    )(*sums)


def _share_layers(finals, name):
    na = len(finals)

    def body(*refs):
        ins, outs = refs[:na], refs[na:2 * na]
        ssem, rsem = refs[2 * na:]
        x, y, c = _my_place()
        cps = [pltpu.make_async_remote_copy(src_ref=ins[a], dst_ref=outs[a], send_sem=ssem.at[a], recv_sem=rsem.at[a],
                                            device_id=(x, y, 1 - c), device_id_type=_MESH) for a in range(na)]
        for cp in cps:
            cp.start()
        for cp in cps:
            cp.wait()

    return pl.pallas_call(
        body, out_shape=[jax.ShapeDtypeStruct(p.shape, p.dtype) for p in finals], in_specs=[_HBM] * na, out_specs=[_HBM] * na,
        scratch_shapes=[pltpu.SemaphoreType.DMA((na,)), pltpu.SemaphoreType.DMA((na,))], name=name,
    )(*finals)


def _add_own_layer(parts, recv, c_idx, name):
    _, ns, r, c = parts.shape
    tr, tc = _tile2(r, c)

    def body(ci_ref, p_ref, r_ref, o_ref, ob_ref):
        t = p_ref[0] + r_ref[...]
        o_ref[...] = t
        ob_ref[...] = t.astype(bf16)

    blk = pl.BlockSpec((1, tr, tc), lambda k, i, j, ci: (k, i, j))
    return pl.pallas_call(
        body,
        grid_spec=pltpu.PrefetchScalarGridSpec(
            num_scalar_prefetch=1, grid=(ns, r // tr, c // tc),
            in_specs=[pl.BlockSpec((1, 1, tr, tc), lambda k, i, j, ci: (ci[0], k, i, j)), blk], out_specs=[blk, blk]),
        out_shape=[jax.ShapeDtypeStruct((ns, r, c), f32), jax.ShapeDtypeStruct((ns, r, c), bf16)], name=name,
        compiler_params=_cp(("parallel", "parallel", "parallel")),
    )(c_idx, parts, recv)


def _add_own_shard(sums, recv, k_idx, name):
    _, r, c = sums.shape
    tr, tc = _tile2(r, c)

    def body(ki_ref, s_ref, r_ref, o_ref):
        o_ref[...] = ((s_ref[0] + r_ref[0].astype(f32)) + r_ref[1].astype(f32)) + r_ref[2].astype(f32)

    return pl.pallas_call(
        body,
        grid_spec=pltpu.PrefetchScalarGridSpec(
            num_scalar_prefetch=1, grid=(r // tr, c // tc),
            in_specs=[pl.BlockSpec((1, tr, tc), lambda i, j, ki: (ki[0], i, j)), pl.BlockSpec((3, tr, tc), lambda i, j, ki: (0, i, j))],
            out_specs=pl.BlockSpec((tr, tc), lambda i, j, ki: (i, j))),
        out_shape=jax.ShapeDtypeStruct((r, c), f32), name=name, compiler_params=_cp(("parallel", "parallel")),
    )(k_idx, sums, recv)


def _reduce_scatter(parts, tag):
    x, y, c = _my_place()
    c_idx = jnp.reshape(c, (1,)).astype(jnp.int32)
    k_idx = jnp.reshape(2 * x + y, (1,)).astype(jnp.int32)
    na = len(parts)
    recv = _swap_layers(parts, f"rs_swap{tag}")
    sums = [_add_own_layer(parts[a], recv[a], c_idx, f"rs_add_layer{tag}_{a}") for a in range(na)]
    recv = _scatter_shards([sb for _, sb in sums], f"rs_scatter{tag}")
    finals = [_add_own_shard(sums[a][0], recv[a], k_idx, f"rs_add_shard{tag}_{a}") for a in range(na)]
    return finals, _share_layers(finals, f"rs_share{tag}"), c_idx


_SMALL = [("ada_b", (DEPTH, 3 * D)), ("norm_w", (DEPTH, D)), ("gm_ln_w", (DEPTH, D)), ("gm_ln_b", (DEPTH, D)),
          ("gm_ws", (DEPTH, NG, CH, CH)), ("gm_bs", (DEPTH, NG, CH)), ("conv_b", (DEPTH, CONVD)), ("dt_bias", (DEPTH, NH)),
          ("a_log", (DEPTH, NH)), ("d_skip", (DEPTH, NH)), ("ssm_norm_w", (DEPTH, DIN)), ("final_norm_w", (D,))]


def _rows_of(shape):
    n = 1
    for d in shape:
        n *= d
    return -(-n // 1024) * 8


def _pack(arrays):
    rows = []
    for a in arrays:
        flat = a.reshape(-1)
        r = _rows_of(a.shape)
        rows.append(jnp.pad(flat, (0, r * 128 - flat.shape[0])).reshape(r, 128))
    return jnp.concatenate(rows, axis=0)


def _unpack(buf, shapes):
    out, at = [], 0
    for shp in shapes:
        r = _rows_of(shp)
        n = 1
        for d in shp:
            n *= d
        out.append(buf[at:at + r].reshape(-1)[:n].reshape(shp))
        at += r
    return out


def kernel(x, c, ada_w, ada_b, norm_w, w_in, gm_ln_w, gm_ln_b, gm_ws, gm_bs, conv_w, conv_b, dt_bias, a_log, d_skip, ssm_norm_w, w_proj_a, w_proj_b, w_out, final_norm_w, loss_target, m_ada_w, m_ada_b, m_norm_w, m_w_in, m_gm_ln_w, m_gm_ln_b, m_gm_ws, m_gm_bs, m_conv_w, m_conv_b, m_dt_bias, m_a_log, m_d_skip, m_ssm_norm_w, m_w_proj_a, m_w_proj_b, m_w_out, m_final_norm_w, v_ada_w, v_ada_b, v_norm_w, v_w_in, v_gm_ln_w, v_gm_ln_b, v_gm_ws, v_gm_bs, v_conv_w, v_conv_b, v_dt_bias, v_a_log, v_d_skip, v_ssm_norm_w, v_w_proj_a, v_w_proj_b, v_w_out, v_final_norm_w):
    xi, yi, ci = _my_place()
    k_me = 2 * xi + yi
    b_me = 4 * xi + 2 * yi + ci
    w = dict(ada_b=ada_b, norm_w=norm_w, gm_ln_w=gm_ln_w, gm_ln_b=gm_ln_b, gm_ws=gm_ws, gm_bs=gm_bs, conv_b=conv_b, dt_bias=dt_bias,
             a_log=a_log, d_skip=d_skip, ssm_norm_w=ssm_norm_w, final_norm_w=final_norm_w)
    m = dict(ada_b=m_ada_b, norm_w=m_norm_w, gm_ln_w=m_gm_ln_w, gm_ln_b=m_gm_ln_b, gm_ws=m_gm_ws, gm_bs=m_gm_bs, conv_b=m_conv_b,
             dt_bias=m_dt_bias, a_log=m_a_log, d_skip=m_d_skip, ssm_norm_w=m_ssm_norm_w, final_norm_w=m_final_norm_w)
    v = dict(ada_b=v_ada_b, norm_w=v_norm_w, gm_ln_w=v_gm_ln_w, gm_ln_b=v_gm_ln_b, gm_ws=v_gm_ws, gm_bs=v_gm_bs, conv_b=v_conv_b,
             dt_bias=v_dt_bias, a_log=v_a_log, d_skip=v_d_skip, ssm_norm_w=v_ssm_norm_w, final_norm_w=v_final_norm_w)

    c_slot = lax.dynamic_update_slice(jnp.zeros((8, D), f32), c, (b_me, 0))
    c_all = _allreduce(c_slot.reshape(64, 128), _AXES, "gather_c").reshape(8, D)
    ada_b_cols = lax.dynamic_slice(ada_b, (0, k_me * ADA_COLS), (DEPTH, ADA_COLS)).reshape(DEPTH, 1, ADA_COLS)
    mod_cols = _ada_fwd(c_all, ada_w, ada_b_cols, "ada_fwd")
    mod_slot = lax.dynamic_update_slice(jnp.zeros((DEPTH, 8, 3 * D), f32), mod_cols, (0, 0, k_me * ADA_COLS))
    mod_all = _allreduce(mod_slot.reshape(-1, 128), ("x", "y"), "gather_mod").reshape(DEPTH, 8, 3 * D)
    mod_me = lax.dynamic_slice(mod_all, (0, b_me, 0), (DEPTH, 1, 3 * D))
    mods = [(mod_me[l, :, 0:D], mod_me[l, :, D:2 * D], mod_me[l, :, 2 * D:3 * D]) for l in range(DEPTH)]

    rows_sh = jnp.concatenate([w_proj_a, w_proj_b, w_out], axis=1).astype(bf16)
    win_sh = w_in.astype(bf16)
    win_oth, rows_oth, conv_oth = _gather_weights([win_sh, rows_sh, conv_w], "gather_weights")
    ops = []
    for l in range(DEPTH):
        rows_l = [_with_own(rows_sh[l], rows_oth[:, l], k) for k in range(NSHARD)]
        w_in_l = jnp.concatenate([_with_own(win_sh[l], win_oth[:, l], k) for k in range(NSHARD)], axis=1)
        wa_l = jnp.concatenate([t[0:256] for t in rows_l], axis=0)
        wb_l = jnp.concatenate([t[256:768] for t in rows_l], axis=0)
        wo_l = jnp.concatenate([t[768:1024] for t in rows_l], axis=0)
        cw_l = jnp.concatenate([_with_own(conv_w[l], conv_oth[:, l], k) for k in range(NSHARD)], axis=1)
        ops.append(_layer_operands(w_in_l, cw_l, wa_l, wb_l, wo_l, norm_w[l], gm_ln_w[l], gm_ln_b[l], gm_ws[l], gm_bs[l], conv_b[l],
                                   dt_bias[l], a_log[l], d_skip[l], ssm_norm_w[l]))

    loss_me, dx, grads, dfnw = _local_step(x[0], loss_target[0], mods, ops, final_norm_w)
    loss = lax.psum(loss_me, _AXES)

    s_in = N_IN // NSHARD
    wide = lambda t: t.reshape(t.shape[:-2] + (s_in * D // 8192, 8192))
    to_wide = lambda t: wide(jnp.swapaxes(t, 1, 2))
    from_wide = lambda t: jnp.swapaxes(t.reshape(DEPTH, s_in, D), 1, 2)
    g_in = jnp.stack([wide(grads[l]["w_in_t"].reshape(NSHARD, s_in, D)) for l in range(DEPTH)])
    g_rows = jnp.stack([jnp.concatenate([grads[l]["w_proj_a"].reshape(NSHARD, 256, D), grads[l]["w_proj_b"].reshape(NSHARD, 512, D),
                                         grads[l]["w_out"].reshape(NSHARD, 256, D)], axis=1) for l in range(DEPTH)])
    g_conv = jnp.stack([grads[l]["conv_w"].reshape(4, NSHARD, D).transpose(1, 0, 2) for l in range(DEPTH)])
    (f_in, f_rows, f_conv), (o_in, o_rows, o_conv), c_idx = _reduce_scatter([g_in, g_rows, g_conv], "")
    gr_in, d_in, nm_in, nv_in = [from_wide(t) for t in _adamw_layers(to_wide(w_in), f_in, o_in, to_wide(m_w_in), to_wide(v_w_in),
                                                                     c_idx, "adamw_w_in")]
    cat = lambda a, b, c_: jnp.concatenate([a, b, c_], axis=1)
    gr_rows, d_rows, nm_rows, nv_rows = _adamw_layers(cat(w_proj_a, w_proj_b, w_out), f_rows, o_rows, cat(m_w_proj_a, m_w_proj_b, m_w_out),
                                                      cat(v_w_proj_a, v_w_proj_b, v_w_out), c_idx, "adamw_rows")
    gr_conv = jnp.where(ci == 0, jnp.stack([f_conv, o_conv]), jnp.stack([o_conv, f_conv]))
    split = lambda t: (t[:, 0:256], t[:, 256:768], t[:, 768:1024])

    dmod_slot = lax.dynamic_update_slice(jnp.zeros((DEPTH, 8, 3 * D), f32),
                                         jnp.stack([grads[l]["mod"] for l in range(DEPTH)]).reshape(DEPTH, 1, 3 * D), (0, b_me, 0))
    small_g = {n: (dfnw if n == "final_norm_w" else jnp.stack([grads[l]["mod" if n == "ada_b" else n] for l in range(DEPTH)]))
               for n, _ in _SMALL}
    packed = _allreduce(_pack([small_g[n] for n, _ in _SMALL] + [dmod_slot]), _AXES, "allreduce_small")
    n_small = sum(_rows_of(s) for _, s in _SMALL)
    g_small = packed[0:n_small]
    dmod_all = packed[n_small:].reshape(-1)[:DEPTH * 8 * 3 * D].reshape(DEPTH, 8, 3 * D)
    small_gw = jnp.concatenate([g_small, _pack([gr_conv])], axis=0)
    d_s, nm_s, nv_s = _adamw(_pack([w[n] for n, _ in _SMALL] + [conv_w]), small_gw,
                             _pack([m[n] for n, _ in _SMALL] + [m_conv_w]), _pack([v[n] for n, _ in _SMALL] + [v_conv_w]), "adamw_small")
    shapes = [s for _, s in _SMALL] + [conv_w.shape]
    names = [n for n, _ in _SMALL] + ["conv_w"]
    g_d = dict(zip(names, _unpack(small_gw, shapes)))
    d_d = dict(zip(names, _unpack(d_s, shapes)))
    nm_d = dict(zip(names, _unpack(nm_s, shapes)))
    nv_d = dict(zip(names, _unpack(nv_s, shapes)))

    dmod_cols = lax.dynamic_slice(dmod_all, (0, 0, k_me * ADA_COLS), (DEPTH, 8, ADA_COLS))
    g_ada, d_ada, nm_ada, nv_ada = _ada_bwd_adamw(c_all, dmod_cols, ada_w, m_ada_w, v_ada_w, "ada_bwd_adamw")

    def by_name(big, small):
        ga, gb, go = split(big[1])
        return dict(small, ada_w=big[2], w_in=big[0], w_proj_a=ga, w_proj_b=gb, w_out=go)

    order = ["ada_w", "ada_b", "norm_w", "w_in", "gm_ln_w", "gm_ln_b", "gm_ws", "gm_bs", "conv_w", "conv_b", "dt_bias", "a_log",
             "d_skip", "ssm_norm_w", "w_proj_a", "w_proj_b", "w_out", "final_norm_w"]
    outs = []
    for big, small in (((gr_in, gr_rows, g_ada), g_d), ((d_in, d_rows, d_ada), d_d), ((nm_in, nm_rows, nm_ada), nm_d),
                       ((nv_in, nv_rows, nv_ada), nv_d)):
        t = by_name(big, small)
        outs += [t[n] for n in order]
    return (loss, dx.reshape(1, -1, D), *outs)
```

```python
import jax
import jax.numpy as jnp
from jax import lax
from jax.experimental import pallas as pl
from jax.experimental.pallas import tpu as pltpu

f32 = jnp.float32
bf16 = jnp.bfloat16

D = 1024
DEPTH = 2
EPS = 1e-6
CH = 128
NG = 8
HPG = 4
HD = 64
NH = NG * HPG
NST = 128
DIN = 2048
CONVD = 4096
GW = DIN // NG
PB = CONVD + DIN + 256
PA = 3 * D
PG = 2 * D
N_IN = 11296
NSHARD = 4
V7X_VMEM_BYTES = 64 * 2 ** 20
VMEM_LIMIT = V7X_VMEM_BYTES - 8 * 2 ** 20

ADAM_LR, ADAM_B1, ADAM_B2, ADAM_EPS, ADAM_WD, ADAM_STEP = 0.001, 0.9, 0.999, 1e-08, 0.01, 10

_HI = lax.Precision.HIGHEST


def _cp(sem):
    return pltpu.CompilerParams(dimension_semantics=sem, vmem_limit_bytes=VMEM_LIMIT)


def _sigmoid(x):
    return 0.5 * jnp.tanh(0.5 * x) + 0.5


def _silu_and_grad(x):
    s = _sigmoid(x)
    return x * s, s * (1.0 + x * (1.0 - s))


_GELU_K = 0.7978845608028654
_GELU_C = 0.044715


def _gelu_and_grad(x):
    x2 = x * x
    t = jnp.tanh(_GELU_K * (x + _GELU_C * x * x2))
    g = 0.5 * x * (1.0 + t)
    dg = 0.5 * (1.0 + t) + 0.5 * x * (1.0 - t * t) * _GELU_K * (1.0 + 3.0 * _GELU_C * x2)
    return g, dg


def _gelu(x):
    t = jnp.tanh(_GELU_K * (x + _GELU_C * x * x * x))
    return 0.5 * x * (1.0 + t)


def _softplus(x):
    return jnp.maximum(x, 0.0) + jnp.log(1.0 + jnp.exp(-jnp.abs(x)))


def _dot(a, b):
    return jnp.dot(a, b, preferred_element_type=f32)


def _dot_nt(a, b):
    return lax.dot_general(a, b, (((1,), (1,)), ((), ())), preferred_element_type=f32)


def _dot_tn(a, b):
    return lax.dot_general(a, b, (((0,), (0,)), ((), ())), preferred_element_type=f32)


def _dot_hi(a, b):
    return jnp.dot(a, b, precision=_HI, preferred_element_type=f32)


def _rmsmod_fwd(x, nw, scale, shift, name):
    s = x.shape[0]
    ts = min(512, s)

    def body(x_ref, nw_ref, sc_ref, sh_ref, h_ref):
        xv = x_ref[...]
        r = lax.rsqrt(jnp.mean(xv * xv, axis=-1, keepdims=True) + EPS)
        h_ref[...] = ((xv * r) * nw_ref[...] * (1.0 + sc_ref[...]) + sh_ref[...]).astype(bf16)

    row = pl.BlockSpec((1, D), lambda i: (0, 0))
    tile = pl.BlockSpec((ts, D), lambda i: (i, 0))
    return pl.pallas_call(
        body, grid=(s // ts,), in_specs=[tile, row, row, row], out_specs=tile,
        out_shape=jax.ShapeDtypeStruct((s, D), bf16), name=name, compiler_params=_cp(("parallel",)),
    )(x, nw, scale, shift)


def _rmsmod_bwd(dh, x, dres, nw, scale, name):
    s = x.shape[0]
    ts = min(512, s)

    def body(dh_ref, x_ref, dres_ref, nw_ref, sc_ref, dx_ref, dsc_ref, dsh_ref, dnw_ref):
        @pl.when(pl.program_id(0) == 0)
        def _():
            dsc_ref[...] = jnp.zeros_like(dsc_ref)
            dsh_ref[...] = jnp.zeros_like(dsh_ref)
            dnw_ref[...] = jnp.zeros_like(dnw_ref)
        xv = x_ref[...]
        dhv = dh_ref[...]
        r = lax.rsqrt(jnp.mean(xv * xv, axis=-1, keepdims=True) + EPS)
        xn = xv * r
        one_sc = 1.0 + sc_ref[...]
        dsc_ref[...] += jnp.sum(dhv * xn * nw_ref[...], axis=0, keepdims=True)
        dsh_ref[...] += jnp.sum(dhv, axis=0, keepdims=True)
        dnw_ref[...] += jnp.sum(dhv * xn * one_sc, axis=0, keepdims=True)
        dxn = dhv * (nw_ref[...] * one_sc)
        dx_ref[...] = r * (dxn - xn * jnp.mean(dxn * xn, axis=-1, keepdims=True)) + dres_ref[...]

    row = pl.BlockSpec((1, D), lambda i: (0, 0))
    tile = pl.BlockSpec((ts, D), lambda i: (i, 0))
    vec = jax.ShapeDtypeStruct((1, D), f32)
    return pl.pallas_call(
        body, grid=(s // ts,), in_specs=[tile, tile, tile, row, row], out_specs=[tile, row, row, row],
        out_shape=[jax.ShapeDtypeStruct((s, D), f32), vec, vec, vec], name=name, compiler_params=_cp(("arbitrary",)),
    )(dh, x, dres, nw, scale)


def _pick(n, prefs):
    for p in prefs:
        if n % p == 0:
            return p
    return n


def _mm(a, b, out_dtype, name, c_in=None):
    m, k = a.shape
    n = b.shape[1]
    tm = _pick(m, (1024, 512, 256))
    tn = _pick(n, (1280, 1024, 512))
    tk = _pick(k, (1280, 1024, 512))
    nk = k // tk

    def body(*refs):
        if c_in is not None:
            a_ref, b_ref, c_ref, o_ref, acc = refs
        else:
            a_ref, b_ref, o_ref, acc = refs
        kk = pl.program_id(2)

        @pl.when(kk == 0)
        def _():
            if c_in is not None:
                acc[...] = c_ref[...]
            else:
                acc[...] = jnp.zeros_like(acc)
        acc[...] += _dot(a_ref[...], b_ref[...])

        @pl.when(kk == nk - 1)
        def _():
            o_ref[...] = acc[...].astype(out_dtype)

    in_specs = [pl.BlockSpec((tm, tk), lambda j, i, kk: (i, kk)), pl.BlockSpec((tk, tn), lambda j, i, kk: (kk, j))]
    args = [a, b]
    if c_in is not None:
        in_specs.append(pl.BlockSpec((tm, tn), lambda j, i, kk: (i, j)))
        args.append(c_in)
    return pl.pallas_call(
        body, grid=(n // tn, m // tm, nk), in_specs=in_specs, out_specs=pl.BlockSpec((tm, tn), lambda j, i, kk: (i, j)),
        out_shape=jax.ShapeDtypeStruct((m, n), out_dtype), scratch_shapes=[pltpu.VMEM((tm, tn), f32)], name=name,
        compiler_params=_cp(("parallel", "parallel", "arbitrary")),
    )(*args)


def _mm_tn(a, b, name):
    t, k1 = a.shape
    n = b.shape[1]
    t1 = _pick(k1, (1280, 1024, 512))
    tn = _pick(n, (1280, 1024, 512))
    tt = _pick(t, (2048, 1024, 512, 256))
    nt = t // tt

    def body(a_ref, b_ref, o_ref):
        tt_i = pl.program_id(2)

        @pl.when(tt_i == 0)
        def _():
            o_ref[...] = jnp.zeros_like(o_ref)
        o_ref[...] += _dot_tn(a_ref[...], b_ref[...])

    return pl.pallas_call(
        body, grid=(k1 // t1, n // tn, nt),
        in_specs=[pl.BlockSpec((tt, t1), lambda i, j, tt_i: (tt_i, i)), pl.BlockSpec((tt, tn), lambda i, j, tt_i: (tt_i, j))],
        out_specs=pl.BlockSpec((t1, tn), lambda i, j, tt_i: (i, j)),
        out_shape=jax.ShapeDtypeStruct((k1, n), f32), name=name,
        compiler_params=_cp(("parallel", "parallel", "arbitrary")),
    )(a, b)


def _ln_stats(v):
    mu = jnp.mean(v, axis=-1, keepdims=True)
    vc = v - mu
    rstd = lax.rsqrt(jnp.mean(vc * vc, axis=-1, keepdims=True) + EPS)
    return vc * rstd, rstd


def _mix(w_ref, vl):
    return jnp.concatenate([_dot(w_ref[g], vl[:, g * CH:(g + 1) * CH]) for g in range(NG)], axis=1)


def _branch_a_fwd(proj_a, lnw, lnb, wsm, bsf, name):
    s = proj_a.shape[0]
    ta = min(256, s)

    def body(pu_ref, pv_ref, pz_ref, lnw_ref, lnb_ref, w_ref, bs_ref, ya_ref):
        for c in range(ta // CH):
            rows = pl.ds(c * CH, CH)
            vh, _ = _ln_stats(_gelu(pv_ref[rows, :]))
            vl = (vh * lnw_ref[...] + lnb_ref[...]).astype(bf16)
            mixed = _mix(w_ref, vl) + bs_ref[...]
            pz = pz_ref[rows, :]
            ya_ref[rows, :] = (_gelu(pu_ref[rows, :]) * mixed * (pz * _sigmoid(pz))).astype(bf16)

    row = pl.BlockSpec((1, D), lambda i: (0, 0))
    return pl.pallas_call(
        body, grid=(s // ta,),
        in_specs=[pl.BlockSpec((ta, D), lambda i: (i, 0)), pl.BlockSpec((ta, D), lambda i: (i, 1)),
                  pl.BlockSpec((ta, D), lambda i: (i, 2)), row, row,
                  pl.BlockSpec((NG, CH, CH), lambda i: (0, 0, 0)), pl.BlockSpec((CH, D), lambda i: (0, 0))],
        out_specs=pl.BlockSpec((ta, D), lambda i: (i, 0)),
        out_shape=jax.ShapeDtypeStruct((s, D), bf16), name=name, compiler_params=_cp(("parallel",)),
    )(proj_a, proj_a, proj_a, lnw, lnb, wsm, bsf)


def _branch_a_bwd(proj_a, dya, lnw, lnb, wsm, wsm_t, bsf, name):
    s = proj_a.shape[0]
    ta = min(256, s)

    def body(pu_ref, pv_ref, pz_ref, dya_ref, lnw_ref, lnb_ref, w_ref, wt_ref, bs_ref,
             dp_ref, dws_ref, dbs_ref, dlnw_ref, dlnb_ref):
        @pl.when(pl.program_id(0) == 0)
        def _():
            dws_ref[...] = jnp.zeros_like(dws_ref)
            dbs_ref[...] = jnp.zeros_like(dbs_ref)
            dlnw_ref[...] = jnp.zeros_like(dlnw_ref)
            dlnb_ref[...] = jnp.zeros_like(dlnb_ref)
        for c in range(ta // CH):
            rows = pl.ds(c * CH, CH)
            u, du = _gelu_and_grad(pu_ref[rows, :])
            v, dv_act = _gelu_and_grad(pv_ref[rows, :])
            zg, dzg = _silu_and_grad(pz_ref[rows, :])
            vh, rstd = _ln_stats(v)
            vl = (vh * lnw_ref[...] + lnb_ref[...]).astype(bf16)
            mixed = _mix(w_ref, vl) + bs_ref[...]
            dy = dya_ref[rows, :].astype(f32)
            dmixed = dy * u * zg
            dp_ref[rows, 0:D] = (dy * mixed * zg * du).astype(bf16)
            dp_ref[rows, 2 * D:3 * D] = (dy * u * mixed * dzg).astype(bf16)
            dmb = dmixed.astype(bf16)
            dbs_ref[...] += dmixed
            dvl = _mix(wt_ref, dmb)
            for g in range(NG):
                cols = slice(g * CH, (g + 1) * CH)
                dws_ref[g] += _dot_nt(dmb[:, cols], vl[:, cols])
            dlnw_ref[...] += jnp.sum(dvl * vh, axis=0, keepdims=True)
            dlnb_ref[...] += jnp.sum(dvl, axis=0, keepdims=True)
            dvh = dvl * lnw_ref[...]
            dv = rstd * (dvh - jnp.mean(dvh, axis=-1, keepdims=True) - vh * jnp.mean(dvh * vh, axis=-1, keepdims=True))
            dp_ref[rows, D:2 * D] = (dv * dv_act).astype(bf16)

    row = pl.BlockSpec((1, D), lambda i: (0, 0))
    wspec = pl.BlockSpec((NG, CH, CH), lambda i: (0, 0, 0))
    full = pl.BlockSpec((CH, D), lambda i: (0, 0))
    return pl.pallas_call(
        body, grid=(s // ta,),
        in_specs=[pl.BlockSpec((ta, D), lambda i: (i, 0)), pl.BlockSpec((ta, D), lambda i: (i, 1)),
                  pl.BlockSpec((ta, D), lambda i: (i, 2)), pl.BlockSpec((ta, D), lambda i: (i, 0)),
                  row, row, wspec, wspec, full],
        out_specs=[pl.BlockSpec((ta, PA), lambda i: (i, 0)), wspec, full, row, row],
        out_shape=[jax.ShapeDtypeStruct((s, PA), bf16), jax.ShapeDtypeStruct((NG, CH, CH), f32),
                   jax.ShapeDtypeStruct((CH, D), f32), jax.ShapeDtypeStruct((1, D), f32), jax.ShapeDtypeStruct((1, D), f32)],
        name=name, compiler_params=_cp(("arbitrary",)),
    )(proj_a, proj_a, proj_a, dya, lnw, lnb, wsm, wsm_t, bsf)


GB = GW + 2 * NST


def _group_major(xs, b, c):
    lead = xs.shape[:-1]
    return jnp.concatenate([xs.reshape(lead + (NG, GW)), b.reshape(lead + (NG, NST)), c.reshape(lead + (NG, NST))],
                           axis=-1).reshape(lead + (CONVD,))


def _from_group_major(t):
    lead = t.shape[:-1]
    t = t.reshape(lead + (NG, GB))
    return jnp.concatenate([t[..., 0:GW].reshape(lead + (DIN,)), t[..., GW:GW + NST].reshape(lead + (NG * NST,)),
                            t[..., GW + NST:GB].reshape(lead + (NG * NST,))], axis=-1)


def _rows_from_group_major(t):
    t = t.reshape(NG, GB, t.shape[-1])
    return jnp.concatenate([t[:, 0:GW].reshape(DIN, -1), t[:, GW:GW + NST].reshape(NG * NST, -1),
                            t[:, GW + NST:GB].reshape(NG * NST, -1)], axis=0)


def _shift_rows(x, prev8, j):
    xr = pltpu.roll(x, j, 0)
    fix = pltpu.roll(prev8, j, 0)
    rid = lax.broadcasted_iota(jnp.int32, (8, x.shape[1]), 0)
    top = jnp.where(rid < j, fix, xr[0:8])
    return jnp.concatenate([top, xr[8:]], axis=0)


def _shift_rows_up(d, next8, j):
    dr = pltpu.roll(d, CH - j, 0)
    fix = pltpu.roll(next8, 8 - j, 0)
    rid = lax.broadcasted_iota(jnp.int32, (8, d.shape[1]), 0)
    bot = jnp.where(rid >= 8 - j, fix, dr[CH - 8:CH])
    return jnp.concatenate([dr[0:CH - 8], bot], axis=0)


def _conv_pre(x, prev8, cw_ref, cb_ref, cols):
    shifted = [_shift_rows(x, prev8, j) for j in (1, 2, 3)]
    conv = cb_ref[:, cols] + cw_ref[3:4, cols] * x
    for j in (1, 2, 3):
        conv = conv + cw_ref[3 - j:4 - j, cols] * shifted[j - 1]
    return conv, shifted


def _tril_mask():
    return lax.broadcasted_iota(jnp.int32, (CH, CH), 0) >= lax.broadcasted_iota(jnp.int32, (CH, CH), 1)


def _sum_all(v):
    return jnp.sum(jnp.sum(v, axis=0, keepdims=True), axis=1, keepdims=True)


def _lanes(g, width, base=0):
    return pl.ds(pl.multiple_of(base + g * width, width), width)


def _branch_b_fwd(proj_b, cw, cb, dtb, a_row, dkc, snw, name):
    s = proj_b.shape[0]
    nc = s // CH

    def body(xbc_ref, sz_ref, dtr_ref, cw_ref, cb_ref, dtb_ref, a_ref, dkc_ref, snw_ref,
             yb_ref, y_ref, st_ref, cv_ref, prev8, state, acst_s):
        @pl.when(pl.program_id(0) == 0)
        def _():
            prev8[...] = jnp.zeros_like(prev8)
            state[...] = jnp.zeros_like(state)
        st_ref[0] = state[...].astype(bf16)
        mask = _tril_mask()
        dt_all = _softplus(dtr_ref[:, 0:CH] + dtb_ref[...])
        acs_all = _dot_hi(mask.astype(f32), dt_all * a_ref[...])
        acst_s[...] = acs_all.T

        def group(g, carry):
            cols = _lanes(g, GB)
            gcols = _lanes(g, GW)
            x = xbc_ref[:, cols]
            conv, _ = _conv_pre(x, prev8[:, cols], cw_ref, cb_ref, cols)
            prev8[:, cols] = x[CH - 8:CH]
            cv_ref[:, cols] = conv.astype(bf16)
            xc = conv * _sigmoid(conv)
            xs = xc[:, 0:GW]
            bg = xc[:, GW:GW + NST].astype(bf16)
            cg = xc[:, GW + NST:GB].astype(bf16)
            back = lax.rem(CH - HPG * g, CH)
            dt = pltpu.roll(dt_all, back, 1)
            acs = pltpu.roll(acs_all, back, 1)
            cbm = _dot_nt(cg, bg)
            dkc_g = dkc_ref[:, gcols]
            y_parts = []
            for r in range(HPG):
                colb = jnp.broadcast_to(acs[:, r:r + 1], (CH, CH))
                row = acst_s[pl.ds(g * HPG + r, 1), :]
                lmat = jnp.exp(jnp.where(mask, colb - row, -jnp.inf))
                xr = xs[:, r * HD:(r + 1) * HD]
                xd = xr * dt[:, r:r + 1]
                sp = state[g * HPG + r]
                col = colb[:, 0:HD]
                alast = colb[CH - 1:CH, 0:HD]
                y_r = _dot((cbm * lmat).astype(bf16), xd.astype(bf16))
                y_r = y_r + jnp.exp(col) * _dot_nt(cg, sp.astype(bf16))
                y_parts.append(y_r + xr * dkc_g[:, r * HD:(r + 1) * HD])
                cs = _dot_tn((xd * jnp.exp(alast - col)).astype(bf16), bg)
                state[g * HPG + r] = jnp.exp(colb[CH - 1:CH, :]) * sp + cs
            y = jnp.concatenate(y_parts, axis=1)
            szv = sz_ref[:, gcols]
            yz = y * (szv * _sigmoid(szv))
            rr = lax.rsqrt(jnp.mean(yz * yz, axis=-1, keepdims=True) + EPS)
            yb_ref[:, gcols] = (yz * rr * snw_ref[:, gcols]).astype(bf16)
            y_ref[:, gcols] = y.astype(bf16)
            return carry

        lax.fori_loop(0, NG, group, 0)

    const2 = lambda c: (0, 0)
    return pl.pallas_call(
        body, grid=(nc,),
        in_specs=[pl.BlockSpec((CH, CONVD), lambda c: (c, 0)), pl.BlockSpec((CH, DIN), lambda c: (c, CONVD // DIN)),
                  pl.BlockSpec((CH, 256), lambda c: (c, (CONVD + DIN) // 256)),
                  pl.BlockSpec((4, CONVD), const2), pl.BlockSpec((1, CONVD), const2),
                  pl.BlockSpec((1, CH), const2), pl.BlockSpec((1, CH), const2),
                  pl.BlockSpec((1, DIN), const2), pl.BlockSpec((1, DIN), const2)],
        out_specs=[pl.BlockSpec((CH, DIN), lambda c: (c, 0)), pl.BlockSpec((CH, DIN), lambda c: (c, 0)),
                   pl.BlockSpec((1, NH, HD, NST), lambda c: (c, 0, 0, 0)), pl.BlockSpec((CH, CONVD), lambda c: (c, 0))],
        out_shape=[jax.ShapeDtypeStruct((s, DIN), bf16), jax.ShapeDtypeStruct((s, DIN), bf16),
                   jax.ShapeDtypeStruct((nc, NH, HD, NST), bf16), jax.ShapeDtypeStruct((s, CONVD), bf16)],
        scratch_shapes=[pltpu.VMEM((8, CONVD), f32), pltpu.VMEM((NH, HD, NST), f32), pltpu.VMEM((CH, CH), f32)],
        name=name, compiler_params=_cp(("arbitrary",)),
    )(proj_b, proj_b, proj_b, cw, cb, dtb, a_row, dkc, snw)


def _branch_b_bwd(proj_b, conv_sv, dyb, y_sv, states, cw, dtb, a_row, dkc, snw, ind, name):
    s = proj_b.shape[0]
    nc = s // CH

    def body(xbc_ref, cv_ref, sz_ref, dtr_ref, dyb_ref, y_ref, st_ref, cw_ref, dtb_ref, a_ref, dkc_ref,
             snw_ref, ind_ref, dp_ref, dcw_ref, dcb_ref, ddtb_ref, dal_ref, ddk_ref, dsnw_ref,
             dstate, dnext8, acst_s, dacs_acc, q2_acc):
        @pl.when(pl.program_id(0) == 0)
        def _():
            dstate[...] = jnp.zeros_like(dstate)
            dnext8[...] = jnp.zeros_like(dnext8)
            dcw_ref[...] = jnp.zeros_like(dcw_ref)
            dcb_ref[...] = jnp.zeros_like(dcb_ref)
            ddtb_ref[...] = jnp.zeros_like(ddtb_ref)
            dal_ref[...] = jnp.zeros_like(dal_ref)
            ddk_ref[...] = jnp.zeros_like(ddk_ref)
            dsnw_ref[...] = jnp.zeros_like(dsnw_ref)

        dacs_acc[...] = jnp.zeros_like(dacs_acc)
        q2_acc[...] = jnp.zeros_like(q2_acc)
        mask = _tril_mask()
        tri_t = (lax.broadcasted_iota(jnp.int32, (CH, CH), 0) <= lax.broadcasted_iota(jnp.int32, (CH, CH), 1)).astype(f32)
        lane1 = lax.broadcasted_iota(jnp.int32, (1, CH), 1)
        is_last = lax.broadcasted_iota(jnp.int32, (CH, 1), 0) == CH - 1
        z_all = dtr_ref[:, 0:CH] + dtb_ref[...]
        dt_all = _softplus(z_all)
        adt_all = dt_all * a_ref[...]
        acs_all = _dot_hi(mask.astype(f32), adt_all)
        acst_s[...] = acs_all.T

        def ind_sum(v):
            hi = v.astype(bf16)
            lo = (v - hi.astype(f32)).astype(bf16)
            return _dot(hi, ind_ref[...]) + _dot(lo, ind_ref[...])

        def group(g, carry):
            cols = _lanes(g, GB)
            gcols = _lanes(g, GW)
            conv = cv_ref[:, cols].astype(f32)
            sg = _sigmoid(conv)
            xc = conv * sg
            xs = xc[:, 0:GW]
            bg = xc[:, GW:GW + NST].astype(bf16)
            cg = xc[:, GW + NST:GB].astype(bf16)

            y = y_ref[:, gcols].astype(f32)
            silu_sz, dsilu_sz = _silu_and_grad(sz_ref[:, gcols])
            yz = y * silu_sz
            rr = lax.rsqrt(jnp.mean(yz * yz, axis=-1, keepdims=True) + EPS)
            dyb_g = dyb_ref[:, gcols].astype(f32)
            w = dyb_g * snw_ref[:, gcols]
            dsnw_ref[:, gcols] += jnp.sum(dyb_g * yz * rr, axis=0, keepdims=True)
            dyz = rr * w - yz * (rr * rr * rr) * jnp.mean(w * yz, axis=-1, keepdims=True)
            dp_ref[:, _lanes(g, GW, CONVD)] = (dyz * y * dsilu_sz).astype(bf16)
            dy_g = dyz * silu_sz
            ddk_ref[:, gcols] += jnp.sum(dy_g * xs, axis=0, keepdims=True)

            back = lax.rem(CH - HPG * g, CH)
            dt = pltpu.roll(dt_all, back, 1)
            acs = pltpu.roll(acs_all, back, 1)
            cbm = _dot_nt(cg, bg)
            d_cb = jnp.zeros((CH, CH), f32)
            d_bg = jnp.zeros((CH, NST), f32)
            d_cg = jnp.zeros((CH, NST), f32)
            lastrow = jnp.zeros((1, CH), f32)
            dxd_parts, dxs_parts, t_parts = [], [], []
            for r in range(HPG):
                h = g * HPG + r
                colb = jnp.broadcast_to(acs[:, r:r + 1], (CH, CH))
                row = acst_s[pl.ds(h, 1), :]
                lmat = jnp.exp(jnp.where(mask, colb - row, -jnp.inf))
                mmat_b = (cbm * lmat).astype(bf16)
                dtc = dt[:, r:r + 1]
                xd = xs[:, r * HD:(r + 1) * HD] * dtc
                col = colb[:, 0:HD]
                alast = colb[CH - 1:CH, 0:HD]
                dte = jnp.exp(alast - col)
                ea = jnp.exp(col)
                cd = jnp.exp(colb[CH - 1:CH, :])
                sp = st_ref[0, h]
                dsn = dstate[h]
                dsn_b = dsn.astype(bf16)
                dyr = dy_g[:, r * HD:(r + 1) * HD]
                dyr_b = dyr.astype(bf16)
                dye_b = (dyr * ea).astype(bf16)
                d_cg = d_cg + _dot(dye_b, sp)
                dxde = _dot_nt(bg, dsn_b)
                xdte = xd * dte
                xd_b = xd.astype(bf16)
                d_bg = d_bg + _dot(xdte.astype(bf16), dsn_b)
                dxd_diag = _dot_tn(mmat_b, dyr_b)
                dxd = dxde * dte + dxd_diag
                d_cb = d_cb + _dot_nt(dyr_b, xd_b) * lmat
                t_parts.append(dyr_b.astype(f32) * _dot(mmat_b, xd_b) + dyr * (ea * _dot_nt(cg, sp))
                               - xd_b.astype(f32) * dxd_diag - dxde * xdte)
                lastrow = jnp.where(lane1 == r, _sum_all(dsn * sp.astype(f32)) * cd + _sum_all(dxde * xdte), lastrow)
                dstate[h] = cd * dsn + _dot_tn(dye_b, cg)
                dxd_parts.append(dxd)
                dxs_parts.append(dxd * dtc)
            d_cb_b = d_cb.astype(bf16)
            d_bg = d_bg + _dot_tn(d_cb_b, cg)
            d_cg = d_cg + _dot(d_cb_b, bg)
            q2 = ind_sum(jnp.concatenate(dxd_parts, axis=1) * xs)
            dacs = ind_sum(jnp.concatenate(t_parts, axis=1)) + jnp.where(is_last, lastrow, 0.0)
            dacs_acc[...] += pltpu.roll(dacs, HPG * g, 1)
            q2_acc[...] += pltpu.roll(q2, HPG * g, 1)
            dxs = jnp.concatenate(dxs_parts, axis=1) + dy_g * dkc_ref[:, gcols]

            dconv = jnp.concatenate([dxs, d_bg, d_cg], axis=1) * (sg * (1.0 + conv * (1.0 - sg)))
            x = xbc_ref[:, cols]
            dcb_ref[:, cols] += jnp.sum(dconv, axis=0, keepdims=True)
            dcw_ref[3:4, cols] += jnp.sum(dconv * x, axis=0, keepdims=True)
            dx = cw_ref[3:4, cols] * dconv
            nxt = dnext8[:, cols]
            for j in (1, 2, 3):
                up = _shift_rows_up(dconv, nxt, j)
                dcw_ref[3 - j:4 - j, cols] += jnp.sum(up * x, axis=0, keepdims=True)
                dx = dx + cw_ref[3 - j:4 - j, cols] * up
            dnext8[:, cols] = dconv[0:8]
            dp_ref[:, cols] = dx.astype(bf16)
            return carry

        lax.fori_loop(0, NG, group, 0)

        dadt = _dot_hi(tri_t, dacs_acc[...])
        dal_ref[...] += jnp.sum(dadt * adt_all, axis=0, keepdims=True)
        ddz = (dadt * a_ref[...] + q2_acc[...]) * _sigmoid(z_all)
        ddtb_ref[...] += jnp.sum(ddz, axis=0, keepdims=True)
        dp_ref[:, CONVD + DIN:CONVD + DIN + CH] = ddz.astype(bf16)
        dp_ref[:, CONVD + DIN + CH:PB] = jnp.zeros((CH, PB - CONVD - DIN - CH), bf16)

    const2 = lambda c: (0, 0)
    rev = lambda c: (nc - 1 - c, 0)
    return pl.pallas_call(
        body, grid=(nc,),
        in_specs=[pl.BlockSpec((CH, CONVD), rev), pl.BlockSpec((CH, CONVD), rev),
                  pl.BlockSpec((CH, DIN), lambda c: (nc - 1 - c, CONVD // DIN)),
                  pl.BlockSpec((CH, 256), lambda c: (nc - 1 - c, (CONVD + DIN) // 256)),
                  pl.BlockSpec((CH, DIN), rev), pl.BlockSpec((CH, DIN), rev),
                  pl.BlockSpec((1, NH, HD, NST), lambda c: (nc - 1 - c, 0, 0, 0)),
                  pl.BlockSpec((4, CONVD), const2), pl.BlockSpec((1, CH), const2), pl.BlockSpec((1, CH), const2),
                  pl.BlockSpec((1, DIN), const2), pl.BlockSpec((1, DIN), const2), pl.BlockSpec((GW, CH), const2)],
        out_specs=[pl.BlockSpec((CH, PB), rev), pl.BlockSpec((4, CONVD), const2), pl.BlockSpec((1, CONVD), const2),
                   pl.BlockSpec((1, CH), const2), pl.BlockSpec((1, CH), const2), pl.BlockSpec((1, DIN), const2),
                   pl.BlockSpec((1, DIN), const2)],
        out_shape=[jax.ShapeDtypeStruct((s, PB), bf16), jax.ShapeDtypeStruct((4, CONVD), f32),
                   jax.ShapeDtypeStruct((1, CONVD), f32), jax.ShapeDtypeStruct((1, CH), f32),
                   jax.ShapeDtypeStruct((1, CH), f32), jax.ShapeDtypeStruct((1, DIN), f32),
                   jax.ShapeDtypeStruct((1, DIN), f32)],
        scratch_shapes=[pltpu.VMEM((NH, HD, NST), f32), pltpu.VMEM((8, CONVD), f32), pltpu.VMEM((CH, CH), f32),
                        pltpu.VMEM((CH, CH), f32), pltpu.VMEM((CH, CH), f32)],
        name=name, compiler_params=_cp(("arbitrary",)),
    )(proj_b, conv_sv, proj_b, proj_b, dyb, y_sv, states, cw, dtb, a_row, dkc, snw, ind)


def _merge_fwd(ya, yb, proj_g, x, gate, wa, wb, wo, name):
    s = x.shape[0]
    ts = min(512, s)

    def body(ya_ref, yb_ref, ga_ref, gb_ref, x_ref, gate_ref, wa_ref, wb_ref, wo_ref, xo_ref, pa_ref, pb_ref, mg_ref, o_ref):
        pa = _dot(ya_ref[...], wa_ref[...])
        pb = _dot(yb_ref[...], wb_ref[...])
        mg = (_sigmoid(ga_ref[...]) * pa + _sigmoid(gb_ref[...]) * pb).astype(bf16)
        o = _dot(mg, wo_ref[...])
        xo_ref[...] = x_ref[...] + gate_ref[...] * o
        pa_ref[...] = pa.astype(bf16)
        pb_ref[...] = pb.astype(bf16)
        mg_ref[...] = mg
        o_ref[...] = o.astype(bf16)

    tile = pl.BlockSpec((ts, D), lambda i: (i, 0))
    const = lambda i: (0, 0)
    act = jax.ShapeDtypeStruct((s, D), bf16)
    return pl.pallas_call(
        body, grid=(s // ts,),
        in_specs=[tile, pl.BlockSpec((ts, DIN), lambda i: (i, 0)), tile, pl.BlockSpec((ts, D), lambda i: (i, 1)), tile,
                  pl.BlockSpec((1, D), const), pl.BlockSpec((D, D), const), pl.BlockSpec((DIN, D), const),
                  pl.BlockSpec((D, D), const)],
        out_specs=[tile, tile, tile, tile, tile],
        out_shape=[jax.ShapeDtypeStruct((s, D), f32), act, act, act, act],
        name=name, compiler_params=_cp(("parallel",)),
    )(ya, yb, proj_g, proj_g, x, gate, wa, wb, wo)


def _merge_bwd(dxo, gate, o_sv, pa_sv, pb_sv, proj_g, wo_t, wa_t, wb_t, name):
    s = dxo.shape[0]
    ts = min(512, s)

    def body(dxo_ref, gate_ref, o_ref, pa_ref, pb_ref, ga_ref, gb_ref, wot_ref, wat_ref, wbt_ref,
             do_ref, dpa_ref, dpb_ref, dg_ref, dya_ref, dyb_ref, dgate_ref):
        @pl.when(pl.program_id(0) == 0)
        def _():
            dgate_ref[...] = jnp.zeros_like(dgate_ref)
        dxo_v = dxo_ref[...]
        dgate_ref[...] += jnp.sum(dxo_v * o_ref[...].astype(f32), axis=0, keepdims=True)
        do = (dxo_v * gate_ref[...]).astype(bf16)
        do_ref[...] = do
        dmg = _dot(do, wot_ref[...])
        sa = _sigmoid(ga_ref[...])
        sb = _sigmoid(gb_ref[...])
        dpa = (dmg * sa).astype(bf16)
        dpb = (dmg * sb).astype(bf16)
        dpa_ref[...] = dpa
        dpb_ref[...] = dpb
        dg_ref[:, 0:D] = (dmg * pa_ref[...].astype(f32) * sa * (1.0 - sa)).astype(bf16)
        dg_ref[:, D:2 * D] = (dmg * pb_ref[...].astype(f32) * sb * (1.0 - sb)).astype(bf16)
        dya_ref[...] = _dot(dpa, wat_ref[...]).astype(bf16)
        dyb_ref[...] = _dot(dpb, wbt_ref[...]).astype(bf16)

    tile = pl.BlockSpec((ts, D), lambda i: (i, 0))
    const = lambda i: (0, 0)
    act = jax.ShapeDtypeStruct((s, D), bf16)
    return pl.pallas_call(
        body, grid=(s // ts,),
        in_specs=[tile, pl.BlockSpec((1, D), const), tile, tile, tile, tile, pl.BlockSpec((ts, D), lambda i: (i, 1)),
                  pl.BlockSpec((D, D), const), pl.BlockSpec((D, D), const), pl.BlockSpec((D, DIN), const)],
        out_specs=[tile, tile, tile, pl.BlockSpec((ts, PG), lambda i: (i, 0)), tile, pl.BlockSpec((ts, DIN), lambda i: (i, 0)),
                   pl.BlockSpec((1, D), const)],
        out_shape=[act, act, act, jax.ShapeDtypeStruct((s, PG), bf16), act, jax.ShapeDtypeStruct((s, DIN), bf16),
                   jax.ShapeDtypeStruct((1, D), f32)],
        name=name, compiler_params=_cp(("arbitrary",)),
    )(dxo, gate, o_sv, pa_sv, pb_sv, proj_g, proj_g, wo_t, wa_t, wb_t)


def _final_loss(x, target, fnw, name):
    s = x.shape[0]
    ts = min(512, s)

    def body(x_ref, t_ref, w_ref, loss_ref, dx_ref, dw_ref):
        @pl.when(pl.program_id(0) == 0)
        def _():
            loss_ref[...] = jnp.zeros_like(loss_ref)
            dw_ref[...] = jnp.zeros_like(dw_ref)
        xv = x_ref[...]
        r = lax.rsqrt(jnp.mean(xv * xv, axis=-1, keepdims=True) + EPS)
        xn = xv * r
        err = xn * w_ref[...] - t_ref[...]
        part = jnp.sum(err * err, axis=0, keepdims=True)
        acc = part[:, 0:128]
        for k in range(1, D // 128):
            acc = acc + part[:, k * 128:(k + 1) * 128]
        loss_ref[0:1, :] += acc * (0.5 / D)
        dy = err * (1.0 / D)
        dw_ref[...] += jnp.sum(dy * xn, axis=0, keepdims=True)
        dxn = dy * w_ref[...]
        dx_ref[...] = r * (dxn - xn * jnp.mean(dxn * xn, axis=-1, keepdims=True))

    tile = pl.BlockSpec((ts, D), lambda i: (i, 0))
    row = pl.BlockSpec((1, D), lambda i: (0, 0))
    return pl.pallas_call(
        body, grid=(s // ts,), in_specs=[tile, tile, row],
        out_specs=[pl.BlockSpec((8, 128), lambda i: (0, 0)), tile, row],
        out_shape=[jax.ShapeDtypeStruct((8, 128), f32), jax.ShapeDtypeStruct((s, D), f32), jax.ShapeDtypeStruct((1, D), f32)],
        name=name, compiler_params=_cp(("arbitrary",)),
    )(x, target, fnw)


def _layer_operands(w_in, conv_w, wa, wb, wo, norm_w, gm_ln_w, gm_ln_b, gm_ws, gm_bs, conv_b, dt_bias, a_log, d_skip, ssm_norm_w):
    w_xbc = _group_major(w_in[:, 5120:7168], w_in[:, 7168:8192], w_in[:, 8192:9216])
    w_b = jnp.concatenate([w_xbc, w_in[:, 3072:5120], w_in[:, 9216:9248], jnp.zeros((D, 224), bf16)], axis=1)
    w_a = w_in[:, 0:3072]
    w_g = w_in[:, 9248:N_IN]
    tril = jnp.tril(jnp.ones((CH, CH), bool))
    wsm = jnp.where(tril[None], gm_ws, 0.0).astype(bf16)

    def heads_row(v):
        return jnp.pad(v, (0, CH - NH)).reshape(1, CH)

    return dict(
        w_b=w_b, w_a=w_a, w_g=w_g, w_b_t=w_b.T, w_a_t=w_a.T, w_g_t=w_g.T,
        wa=wa, wb=wb, wo=wo, wa_t=wa.T, wb_t=wb.T, wo_t=wo.T,
        norm_w=norm_w.reshape(1, D), lnw=gm_ln_w.reshape(1, D), lnb=gm_ln_b.reshape(1, D),
        wsm=wsm, wsm_t=jnp.swapaxes(wsm, 1, 2), bsf=jnp.repeat(gm_bs.T, CH, axis=1),
        cw=_group_major(conv_w[:, 0:DIN], conv_w[:, DIN:DIN + NG * NST], conv_w[:, DIN + NG * NST:CONVD]),
        cb=_group_major(conv_b[0:DIN], conv_b[DIN:DIN + NG * NST], conv_b[DIN + NG * NST:CONVD]).reshape(1, CONVD),
        dtb=heads_row(dt_bias), a_row=heads_row(-jnp.exp(a_log)),
        snw=ssm_norm_w.reshape(1, DIN), dkc=jnp.repeat(d_skip, HD).reshape(1, DIN),
        ind=(jnp.arange(GW)[:, None] // HD == jnp.arange(CH)[None, :]).astype(bf16),
    )


def _layer_fwd(x, shift, scale, gate, p, tag):
    h = _rmsmod_fwd(x, p["norm_w"], scale, shift, f"rmsmod_fwd{tag}")
    proj_b = _mm(h, p["w_b"], f32, f"proj_b{tag}")
    proj_a = _mm(h, p["w_a"], f32, f"proj_a{tag}")
    proj_g = _mm(h, p["w_g"], f32, f"proj_g{tag}")
    ya = _branch_a_fwd(proj_a, p["lnw"], p["lnb"], p["wsm"], p["bsf"], f"branch_a_fwd{tag}")
    yb, y_sv, states, conv_sv = _branch_b_fwd(proj_b, p["cw"], p["cb"], p["dtb"], p["a_row"], p["dkc"], p["snw"], f"branch_b_fwd{tag}")
    x_out, pa, pb, mg, o = _merge_fwd(ya, yb, proj_g, x, gate, p["wa"], p["wb"], p["wo"], f"merge_fwd{tag}")
    saved = dict(x=x, h=h, proj_b=proj_b, proj_a=proj_a, proj_g=proj_g, ya=ya, yb=yb, y=y_sv, states=states, conv=conv_sv,
                 pa=pa, pb=pb, mg=mg, o=o, scale=scale, gate=gate)
    return x_out, saved


def _layer_bwd(dxo, sv, p, tag):
    do, dpa, dpb, dg, dya, dyb, dgate = _merge_bwd(dxo, sv["gate"], sv["o"], sv["pa"], sv["pb"], sv["proj_g"],
                                                   p["wo_t"], p["wa_t"], p["wb_t"], f"merge_bwd{tag}")
    d_wo = _mm_tn(sv["mg"], do, f"d_wo{tag}")
    d_wa = _mm_tn(sv["ya"], dpa, f"d_wa{tag}")
    d_wb = _mm_tn(sv["yb"], dpb, f"d_wb{tag}")
    da, dws, dbs, dlnw, dlnb = _branch_a_bwd(sv["proj_a"], dya, p["lnw"], p["lnb"], p["wsm"], p["wsm_t"], p["bsf"],
                                             f"branch_a_bwd{tag}")
    db, dcw, dcb, ddtb, dal, ddk, dsnw = _branch_b_bwd(sv["proj_b"], sv["conv"], dyb, sv["y"], sv["states"], p["cw"], p["dtb"],
                                                       p["a_row"], p["dkc"], p["snw"], p["ind"], f"branch_b_bwd{tag}")
    dh = _mm(db, p["w_b_t"], f32, f"dh_b{tag}")
    dh = _mm(da, p["w_a_t"], f32, f"dh_a{tag}", c_in=dh)
    dh = _mm(dg, p["w_g_t"], f32, f"dh_g{tag}", c_in=dh)
    d_w_b_t = _mm_tn(db, sv["h"], f"d_w_b{tag}")
    d_w_a_t = _mm_tn(da, sv["h"], f"d_w_a{tag}")
    d_w_g_t = _mm_tn(dg, sv["h"], f"d_w_g{tag}")
    dx, dscale, dshift, dnw = _rmsmod_bwd(dh, sv["x"], dxo, p["norm_w"], sv["scale"], f"rmsmod_bwd{tag}")
    d_w_in_t = jnp.concatenate([d_w_a_t, d_w_b_t[CONVD:CONVD + DIN], _rows_from_group_major(d_w_b_t[0:CONVD]),
                                d_w_b_t[CONVD + DIN:CONVD + DIN + NH], d_w_g_t], axis=0)
    tril = jnp.tril(jnp.ones((CH, CH), bool))
    heads = lambda v: v[0, 0:NH]
    grads = dict(
        w_in_t=d_w_in_t, w_proj_a=d_wa, w_proj_b=d_wb, w_out=d_wo, conv_w=_from_group_major(dcw), conv_b=_from_group_major(dcb).reshape(CONVD),
        norm_w=dnw.reshape(D), gm_ln_w=dlnw.reshape(D), gm_ln_b=dlnb.reshape(D),
        gm_ws=jnp.where(tril[None], dws, 0.0), gm_bs=dbs.reshape(CH, NG, CH).sum(-1).T,
        dt_bias=heads(ddtb), a_log=heads(dal), d_skip=ddk.reshape(NH, HD).sum(-1), ssm_norm_w=dsnw.reshape(DIN),
        mod=jnp.concatenate([dshift, dscale, dgate], axis=1).reshape(3 * D),
    )
    return dx, grads


def _local_step(x, target, mods, layer_ops, fnw):
    saved = []
    for l in range(DEPTH):
        shift, scale, gate = mods[l]
        x, sv = _layer_fwd(x, shift, scale, gate, layer_ops[l], f"_l{l}")
        saved.append(sv)
    loss_parts, dx, dfnw = _final_loss(x, target, fnw.reshape(1, D), "final_loss")
    grads = [None] * DEPTH
    for l in reversed(range(DEPTH)):
        dx, grads[l] = _layer_bwd(dx, saved[l], layer_ops[l], f"_l{l}")
    return jnp.sum(loss_parts), dx, grads, dfnw.reshape(D)


ADA_COLS = 3 * D // NSHARD


def _ada_fwd(c_all, ada_w, ada_b_cols, name):
    def body(c_ref, w_ref, b_ref, o_ref):
        cv = c_ref[...]
        sc = cv * _sigmoid(cv)
        for l in range(DEPTH):
            o_ref[l] = _dot_hi(sc, w_ref[l]) + b_ref[l]

    return pl.pallas_call(body, out_shape=jax.ShapeDtypeStruct((DEPTH, 8, ADA_COLS), f32), name=name,
                          compiler_params=_cp(None))(c_all, ada_w, ada_b_cols)


def _adam_math(w, g, m, v):
    m = ADAM_B1 * m + (1.0 - ADAM_B1) * g
    v = ADAM_B2 * v + (1.0 - ADAM_B2) * (g * g)
    m_hat = m / (1.0 - ADAM_B1 ** ADAM_STEP)
    v_hat = v / (1.0 - ADAM_B2 ** ADAM_STEP)
    delta = -ADAM_LR * (m_hat / (jnp.sqrt(v_hat) + ADAM_EPS) + ADAM_WD * w)
    return delta, m, v


def _ada_bwd_adamw(c_all, dmod_cols, w, m, v, name):
    tr = 256

    def body(c_ref, dm_ref, w_ref, m_ref, v_ref, g_ref, d_ref, nm_ref, nv_ref):
        cv = c_ref[...]
        sc = cv * _sigmoid(cv)
        g = lax.dot_general(sc, dm_ref[0], (((0,), (0,)), ((), ())), precision=_HI, preferred_element_type=f32)
        g_ref[0] = g
        d_ref[0], nm_ref[0], nv_ref[0] = _adam_math(w_ref[0], g, m_ref[0], v_ref[0])

    blk = pl.BlockSpec((1, tr, ADA_COLS), lambda l, i: (l, i, 0))
    shp = jax.ShapeDtypeStruct((DEPTH, D, ADA_COLS), f32)
    return pl.pallas_call(
        body, grid=(DEPTH, D // tr),
        in_specs=[pl.BlockSpec((8, tr), lambda l, i: (0, i)), pl.BlockSpec((1, 8, ADA_COLS), lambda l, i: (l, 0, 0)), blk, blk, blk],
        out_specs=[blk, blk, blk, blk], out_shape=[shp, shp, shp, shp], name=name, compiler_params=_cp(("parallel", "parallel")),
    )(c_all, dmod_cols, w, m, v)


def _adamw(w, g, m, v, name):
    r, c = w.shape
    tr = _pick(r, (256, 128, 64, 32, 16, 8))

    def body(w_ref, g_ref, m_ref, v_ref, d_ref, nm_ref, nv_ref):
        d_ref[...], nm_ref[...], nv_ref[...] = _adam_math(w_ref[...], g_ref[...], m_ref[...], v_ref[...])

    blk = pl.BlockSpec((tr, c), lambda i: (i, 0))
    shp = jax.ShapeDtypeStruct((r, c), f32)
    return pl.pallas_call(body, grid=(r // tr,), in_specs=[blk] * 4, out_specs=[blk] * 3, out_shape=[shp] * 3, name=name,
                          compiler_params=_cp(("parallel",)))(w, g, m, v)


def _tile2(r, c):
    return _pick(r, (256, 128, 64, 32, 16, 8)), _pick(c, (1024,))


def _adamw_layers(w, g_mine, g_other, m, v, c_idx, name):
    _, r, c = w.shape
    tr, tc = _tile2(r, c)

    def body(ci_ref, w_ref, gm_ref, go_ref, m_ref, v_ref, g_ref, d_ref, nm_ref, nv_ref):
        def update(g):
            g_ref[0] = g
            d_ref[0], nm_ref[0], nv_ref[0] = _adam_math(w_ref[0], g, m_ref[0], v_ref[0])

        mine = pl.program_id(0) == ci_ref[0]

        @pl.when(mine)
        def _():
            update(gm_ref[...])

        @pl.when(jnp.logical_not(mine))
        def _():
            update(go_ref[...])

    blk = pl.BlockSpec((1, tr, tc), lambda l, i, j, ci: (l, i, j))
    gblk = pl.BlockSpec((tr, tc), lambda l, i, j, ci: (i, j))
    shp = jax.ShapeDtypeStruct(w.shape, f32)
    return pl.pallas_call(
        body,
        grid_spec=pltpu.PrefetchScalarGridSpec(num_scalar_prefetch=1, grid=(DEPTH, r // tr, c // tc),
                                               in_specs=[blk, gblk, gblk, blk, blk], out_specs=[blk, blk, blk, blk]),
        out_shape=[shp, shp, shp, shp], name=name, compiler_params=_cp(("parallel", "parallel", "parallel")),
    )(c_idx, w, g_mine, g_other, m, v)


_MESH = pl.DeviceIdType.MESH
_AXES = ("x", "y", "c")
_HBM = pl.BlockSpec(memory_space=pltpu.HBM)


def _my_place():
    return tuple(lax.axis_index(a) for a in _AXES)


def _allreduce(buf, axes, name):
    r = buf.shape[0]
    n = len(axes)

    def body(x_ref, o_ref, rbuf, ssem, rsem):
        me = dict(zip(_AXES, _my_place()))
        o_ref[...] = x_ref[...]
        for k, ax in enumerate(axes):
            peer = tuple(1 - me[a] if a == ax else me[a] for a in _AXES)
            cp = pltpu.make_async_remote_copy(src_ref=o_ref, dst_ref=rbuf.at[k], send_sem=ssem.at[k], recv_sem=rsem.at[k],
                                              device_id=peer, device_id_type=_MESH)
            cp.start()
            cp.wait()
            o_ref[...] = o_ref[...] + rbuf[k]

    vm = pl.BlockSpec(memory_space=pltpu.VMEM)
    return pl.pallas_call(
        body, out_shape=jax.ShapeDtypeStruct((r, 128), f32), in_specs=[vm], out_specs=vm,
        scratch_shapes=[pltpu.VMEM((n, r, 128), f32), pltpu.SemaphoreType.DMA((n,)), pltpu.SemaphoreType.DMA((n,))],
        name=name, compiler_params=pltpu.CompilerParams(vmem_limit_bytes=VMEM_LIMIT),
    )(buf)


def _other_chips(x, y):
    return [(1 - x, y), (x, 1 - y), (1 - x, 1 - y)]


def _gather_weights(shards, name):
    na = len(shards)

    def body(*refs):
        ins, outs = refs[:na], refs[na:2 * na]
        ssem, rsem = refs[2 * na:]
        x, y, c = _my_place()
        sibling = (x, y, 1 - c)
        chips = _other_chips(x, y)

        def rcopy(a, src, j, layer, to, idx):
            return pltpu.make_async_remote_copy(src_ref=src, dst_ref=outs[a].at[j, layer], send_sem=ssem.at[idx],
                                                recv_sem=rsem.at[idx], device_id=to, device_id_type=_MESH)

        sent = []
        for j, chip in enumerate(chips):
            for a in range(na):
                cp = rcopy(a, ins[a].at[c], j, c, (*chip, c), j * na + a)
                cp.start()
                sent.append(cp)
        for j, chip in enumerate(chips):
            for a in range(na):
                rcopy(a, ins[a].at[c], j, c, (*chip, c), j * na + a).wait_recv()
                cp = rcopy(a, outs[a].at[j, c], j, c, sibling, (3 + j) * na + a)
                cp.start()
                sent.append(cp)
        for j in range(3):
            for a in range(na):
                rcopy(a, ins[a].at[c], j, 1 - c, sibling, (3 + j) * na + a).wait_recv()
        for cp in sent:
            cp.wait_send()

    out_shape = [jax.ShapeDtypeStruct((3,) + s.shape, s.dtype) for s in shards]
    return pl.pallas_call(
        body, out_shape=out_shape, in_specs=[_HBM] * na, out_specs=[_HBM] * na,
        scratch_shapes=[pltpu.SemaphoreType.DMA((6 * na,)), pltpu.SemaphoreType.DMA((6 * na,))], name=name,
    )(*shards)


def _with_own(own, others, k):
    xi, yi, _ = _my_place()
    d = k ^ (2 * xi + yi)
    j = jnp.where(d == 2, 0, jnp.where(d == 1, 1, 2))
    return jnp.where(d == 0, own, lax.dynamic_index_in_dim(others, j, axis=0, keepdims=False))


def _swap_layers(parts, name):
    na = len(parts)

    def body(*refs):
        ins, outs = refs[:2 * na], refs[2 * na:3 * na]
        ssem, rsem = refs[3 * na:]
        x, y, c = _my_place()

        def copy(a, layer):
            return pltpu.make_async_remote_copy(src_ref=ins[2 * a + layer], dst_ref=outs[a], send_sem=ssem.at[a], recv_sem=rsem.at[a],
                                                device_id=(x, y, 1 - c), device_id_type=_MESH)

        for layer in range(DEPTH):
            @pl.when(c == 1 - layer)
            def _():
                for a in range(na):
                    copy(a, layer).start()
        for a in range(na):
            copy(a, 0).wait()

    flat = [p for pair in parts for p in pair]
    return pl.pallas_call(
        body, out_shape=[jax.ShapeDtypeStruct(p0.shape, p0.dtype) for p0, _ in parts], in_specs=[_HBM] * (2 * na),
        out_specs=[_HBM] * na, scratch_shapes=[pltpu.SemaphoreType.DMA((na,)), pltpu.SemaphoreType.DMA((na,))], name=name,
    )(*flat)


def _scatter_shards(sums, name):
    na = len(sums)

    def body(*refs):
        ins, outs = refs[:na], refs[na:2 * na]
        ssem, rsem = refs[2 * na:]
        x, y, c = _my_place()
        cps = []
        for j, chip in enumerate(_other_chips(x, y)):
            kj = 2 * chip[0] + chip[1]
            for a in range(na):
                cps.append(pltpu.make_async_remote_copy(
                    src_ref=ins[a].at[kj], dst_ref=outs[a].at[j], send_sem=ssem.at[j * na + a], recv_sem=rsem.at[j * na + a],
                    device_id=(*chip, c), device_id_type=_MESH))
        for cp in cps:
            cp.start()
        for cp in cps:
            cp.wait()

    return pl.pallas_call(
        body, out_shape=[jax.ShapeDtypeStruct((3,) + p.shape[1:], p.dtype) for p in sums], in_specs=[_HBM] * na,
        out_specs=[_HBM] * na, scratch_shapes=[pltpu.SemaphoreType.DMA((3 * na,)), pltpu.SemaphoreType.DMA((3 * na,))], name=name,
    )(*sums)


def _share_layers(finals, name):
    na = len(finals)

    def body(*refs):
        ins, outs = refs[:na], refs[na:2 * na]
        ssem, rsem = refs[2 * na:]
        x, y, c = _my_place()
        cps = [pltpu.make_async_remote_copy(src_ref=ins[a], dst_ref=outs[a], send_sem=ssem.at[a], recv_sem=rsem.at[a],
                                            device_id=(x, y, 1 - c), device_id_type=_MESH) for a in range(na)]
        for cp in cps:
            cp.start()
        for cp in cps:
            cp.wait()

    return pl.pallas_call(
        body, out_shape=[jax.ShapeDtypeStruct(p.shape, p.dtype) for p in finals], in_specs=[_HBM] * na, out_specs=[_HBM] * na,
        scratch_shapes=[pltpu.SemaphoreType.DMA((na,)), pltpu.SemaphoreType.DMA((na,))], name=name,
    )(*finals)


def _add_own_layer(part0, part1, recv, c_idx, name):
    ns, r, c = recv.shape
    tr, tc = _tile2(r, c)

    def body(ci_ref, p0_ref, p1_ref, r_ref, o_ref, ob_ref):
        def add(p_ref):
            t = p_ref[...] + r_ref[...]
            o_ref[...] = t
            ob_ref[...] = t.astype(bf16)

        @pl.when(ci_ref[0] == 0)
        def _():
            add(p0_ref)

        @pl.when(ci_ref[0] == 1)
        def _():
            add(p1_ref)

    blk = pl.BlockSpec((1, tr, tc), lambda k, i, j, ci: (k, i, j))
    blk0 = pl.BlockSpec((1, tr, tc), lambda k, i, j, ci: (k * (1 - ci[0]), i * (1 - ci[0]), j * (1 - ci[0])))
    blk1 = pl.BlockSpec((1, tr, tc), lambda k, i, j, ci: (k * ci[0], i * ci[0], j * ci[0]))
    return pl.pallas_call(
        body,
        grid_spec=pltpu.PrefetchScalarGridSpec(num_scalar_prefetch=1, grid=(ns, r // tr, c // tc), in_specs=[blk0, blk1, blk],
                                               out_specs=[blk, blk]),
        out_shape=[jax.ShapeDtypeStruct((ns, r, c), f32), jax.ShapeDtypeStruct((ns, r, c), bf16)], name=name,
        compiler_params=_cp(("arbitrary", "arbitrary", "arbitrary")),
    )(c_idx, part0, part1, recv)


def _add_own_shard(sums, recv, k_idx, name):
    _, r, c = sums.shape
    tr, tc = _tile2(r, c)

    def body(ki_ref, s_ref, r_ref, o_ref):
        o_ref[...] = ((s_ref[0] + r_ref[0].astype(f32)) + r_ref[1].astype(f32)) + r_ref[2].astype(f32)

    return pl.pallas_call(
        body,
        grid_spec=pltpu.PrefetchScalarGridSpec(
            num_scalar_prefetch=1, grid=(r // tr, c // tc),
            in_specs=[pl.BlockSpec((1, tr, tc), lambda i, j, ki: (ki[0], i, j)), pl.BlockSpec((3, tr, tc), lambda i, j, ki: (0, i, j))],
            out_specs=pl.BlockSpec((tr, tc), lambda i, j, ki: (i, j))),
        out_shape=jax.ShapeDtypeStruct((r, c), f32), name=name, compiler_params=_cp(("parallel", "parallel")),
    )(k_idx, sums, recv)


def _reduce_scatter(parts, tag):
    x, y, c = _my_place()
    c_idx = jnp.reshape(c, (1,)).astype(jnp.int32)
    k_idx = jnp.reshape(2 * x + y, (1,)).astype(jnp.int32)
    na = len(parts)
    recv = _swap_layers(parts, f"rs_swap{tag}")
    sums = [_add_own_layer(parts[a][0], parts[a][1], recv[a], c_idx, f"rs_add_layer{tag}_{a}") for a in range(na)]
    recv = _scatter_shards([sb for _, sb in sums], f"rs_scatter{tag}")
    finals = [_add_own_shard(sums[a][0], recv[a], k_idx, f"rs_add_shard{tag}_{a}") for a in range(na)]
    return finals, _share_layers(finals, f"rs_share{tag}"), c_idx


_SMALL = [("ada_b", (DEPTH, 3 * D)), ("norm_w", (DEPTH, D)), ("gm_ln_w", (DEPTH, D)), ("gm_ln_b", (DEPTH, D)),
          ("gm_ws", (DEPTH, NG, CH, CH)), ("gm_bs", (DEPTH, NG, CH)), ("conv_b", (DEPTH, CONVD)), ("dt_bias", (DEPTH, NH)),
          ("a_log", (DEPTH, NH)), ("d_skip", (DEPTH, NH)), ("ssm_norm_w", (DEPTH, DIN)), ("final_norm_w", (D,))]


def _rows_of(shape):
    n = 1
    for d in shape:
        n *= d
    return -(-n // 1024) * 8


def _pack(arrays):
    rows = []
    for a in arrays:
        flat = a.reshape(-1)
        r = _rows_of(a.shape)
        rows.append(jnp.pad(flat, (0, r * 128 - flat.shape[0])).reshape(r, 128))
    return jnp.concatenate(rows, axis=0)


def _unpack(buf, shapes):
    out, at = [], 0
    for shp in shapes:
        r = _rows_of(shp)
        n = 1
        for d in shp:
            n *= d
        out.append(buf[at:at + r].reshape(-1)[:n].reshape(shp))
        at += r
    return out


def kernel(x, c, ada_w, ada_b, norm_w, w_in, gm_ln_w, gm_ln_b, gm_ws, gm_bs, conv_w, conv_b, dt_bias, a_log, d_skip, ssm_norm_w, w_proj_a, w_proj_b, w_out, final_norm_w, loss_target, m_ada_w, m_ada_b, m_norm_w, m_w_in, m_gm_ln_w, m_gm_ln_b, m_gm_ws, m_gm_bs, m_conv_w, m_conv_b, m_dt_bias, m_a_log, m_d_skip, m_ssm_norm_w, m_w_proj_a, m_w_proj_b, m_w_out, m_final_norm_w, v_ada_w, v_ada_b, v_norm_w, v_w_in, v_gm_ln_w, v_gm_ln_b, v_gm_ws, v_gm_bs, v_conv_w, v_conv_b, v_dt_bias, v_a_log, v_d_skip, v_ssm_norm_w, v_w_proj_a, v_w_proj_b, v_w_out, v_final_norm_w):
    xi, yi, ci = _my_place()
    k_me = 2 * xi + yi
    b_me = 4 * xi + 2 * yi + ci
    w = dict(ada_b=ada_b, norm_w=norm_w, gm_ln_w=gm_ln_w, gm_ln_b=gm_ln_b, gm_ws=gm_ws, gm_bs=gm_bs, conv_b=conv_b, dt_bias=dt_bias,
             a_log=a_log, d_skip=d_skip, ssm_norm_w=ssm_norm_w, final_norm_w=final_norm_w)
    m = dict(ada_b=m_ada_b, norm_w=m_norm_w, gm_ln_w=m_gm_ln_w, gm_ln_b=m_gm_ln_b, gm_ws=m_gm_ws, gm_bs=m_gm_bs, conv_b=m_conv_b,
             dt_bias=m_dt_bias, a_log=m_a_log, d_skip=m_d_skip, ssm_norm_w=m_ssm_norm_w, final_norm_w=m_final_norm_w)
    v = dict(ada_b=v_ada_b, norm_w=v_norm_w, gm_ln_w=v_gm_ln_w, gm_ln_b=v_gm_ln_b, gm_ws=v_gm_ws, gm_bs=v_gm_bs, conv_b=v_conv_b,
             dt_bias=v_dt_bias, a_log=v_a_log, d_skip=v_d_skip, ssm_norm_w=v_ssm_norm_w, final_norm_w=v_final_norm_w)

    c_slot = lax.dynamic_update_slice(jnp.zeros((8, D), f32), c, (b_me, 0))
    c_all = _allreduce(c_slot.reshape(64, 128), _AXES, "gather_c").reshape(8, D)
    ada_b_cols = lax.dynamic_slice(ada_b, (0, k_me * ADA_COLS), (DEPTH, ADA_COLS)).reshape(DEPTH, 1, ADA_COLS)
    mod_cols = _ada_fwd(c_all, ada_w, ada_b_cols, "ada_fwd")
    mod_slot = lax.dynamic_update_slice(jnp.zeros((DEPTH, 8, 3 * D), f32), mod_cols, (0, 0, k_me * ADA_COLS))
    mod_all = _allreduce(mod_slot.reshape(-1, 128), ("x", "y"), "gather_mod").reshape(DEPTH, 8, 3 * D)
    mod_me = lax.dynamic_slice(mod_all, (0, b_me, 0), (DEPTH, 1, 3 * D))
    mods = [(mod_me[l, :, 0:D], mod_me[l, :, D:2 * D], mod_me[l, :, 2 * D:3 * D]) for l in range(DEPTH)]

    rows_sh = jnp.concatenate([w_proj_a, w_proj_b, w_out], axis=1).astype(bf16)
    win_sh = w_in.astype(bf16)
    win_oth, rows_oth, conv_oth = _gather_weights([win_sh, rows_sh, conv_w], "gather_weights")
    ops = []
    for l in range(DEPTH):
        rows_l = [_with_own(rows_sh[l], rows_oth[:, l], k) for k in range(NSHARD)]
        w_in_l = jnp.concatenate([_with_own(win_sh[l], win_oth[:, l], k) for k in range(NSHARD)], axis=1)
        wa_l = jnp.concatenate([t[0:256] for t in rows_l], axis=0)
        wb_l = jnp.concatenate([t[256:768] for t in rows_l], axis=0)
        wo_l = jnp.concatenate([t[768:1024] for t in rows_l], axis=0)
        cw_l = jnp.concatenate([_with_own(conv_w[l], conv_oth[:, l], k) for k in range(NSHARD)], axis=1)
        ops.append(_layer_operands(w_in_l, cw_l, wa_l, wb_l, wo_l, norm_w[l], gm_ln_w[l], gm_ln_b[l], gm_ws[l], gm_bs[l], conv_b[l],
                                   dt_bias[l], a_log[l], d_skip[l], ssm_norm_w[l]))

    loss_me, dx, grads, dfnw = _local_step(x[0], loss_target[0], mods, ops, final_norm_w)
    loss = lax.psum(loss_me, _AXES)

    s_in = N_IN // NSHARD
    wide = lambda t: t.reshape(t.shape[:-2] + (s_in * D // 8192, 8192))
    to_wide = lambda t: wide(jnp.swapaxes(t, 1, 2))
    from_wide = lambda t: jnp.swapaxes(t.reshape(DEPTH, s_in, D), 1, 2)
    g_in = [wide(grads[l]["w_in_t"].reshape(NSHARD, s_in, D)) for l in range(DEPTH)]
    g_rows = [jnp.concatenate([grads[l]["w_proj_a"].reshape(NSHARD, 256, D), grads[l]["w_proj_b"].reshape(NSHARD, 512, D),
                               grads[l]["w_out"].reshape(NSHARD, 256, D)], axis=1) for l in range(DEPTH)]
    g_conv = [grads[l]["conv_w"].reshape(4, NSHARD, D).transpose(1, 0, 2) for l in range(DEPTH)]
    (f_in, f_rows, f_conv), (o_in, o_rows, o_conv), c_idx = _reduce_scatter([g_in, g_rows, g_conv], "")
    gr_in, d_in, nm_in, nv_in = [from_wide(t) for t in _adamw_layers(to_wide(w_in), f_in, o_in, to_wide(m_w_in), to_wide(v_w_in),
                                                                     c_idx, "adamw_w_in")]
    cat = lambda a, b, c_: jnp.concatenate([a, b, c_], axis=1)
    gr_rows, d_rows, nm_rows, nv_rows = _adamw_layers(cat(w_proj_a, w_proj_b, w_out), f_rows, o_rows, cat(m_w_proj_a, m_w_proj_b, m_w_out),
                                                      cat(v_w_proj_a, v_w_proj_b, v_w_out), c_idx, "adamw_rows")
    gr_conv = jnp.where(ci == 0, jnp.stack([f_conv, o_conv]), jnp.stack([o_conv, f_conv]))
    split = lambda t: (t[:, 0:256], t[:, 256:768], t[:, 768:1024])

    dmod_slot = lax.dynamic_update_slice(jnp.zeros((DEPTH, 8, 3 * D), f32),
                                         jnp.stack([grads[l]["mod"] for l in range(DEPTH)]).reshape(DEPTH, 1, 3 * D), (0, b_me, 0))
    small_g = {n: (dfnw if n == "final_norm_w" else jnp.stack([grads[l]["mod" if n == "ada_b" else n] for l in range(DEPTH)]))
               for n, _ in _SMALL}
    packed = _allreduce(_pack([small_g[n] for n, _ in _SMALL] + [dmod_slot]), _AXES, "allreduce_small")
    n_small = sum(_rows_of(s) for _, s in _SMALL)
    g_small = packed[0:n_small]
    dmod_all = packed[n_small:].reshape(-1)[:DEPTH * 8 * 3 * D].reshape(DEPTH, 8, 3 * D)
    small_gw = jnp.concatenate([g_small, _pack([gr_conv])], axis=0)
    d_s, nm_s, nv_s = _adamw(_pack([w[n] for n, _ in _SMALL] + [conv_w]), small_gw,
                             _pack([m[n] for n, _ in _SMALL] + [m_conv_w]), _pack([v[n] for n, _ in _SMALL] + [v_conv_w]), "adamw_small")
    shapes = [s for _, s in _SMALL] + [conv_w.shape]
    names = [n for n, _ in _SMALL] + ["conv_w"]
    g_d = dict(zip(names, _unpack(small_gw, shapes)))
    d_d = dict(zip(names, _unpack(d_s, shapes)))
    nm_d = dict(zip(names, _unpack(nm_s, shapes)))
    nv_d = dict(zip(names, _unpack(nv_s, shapes)))

    dmod_cols = lax.dynamic_slice(dmod_all, (0, 0, k_me * ADA_COLS), (DEPTH, 8, ADA_COLS))
    g_ada, d_ada, nm_ada, nv_ada = _ada_bwd_adamw(c_all, dmod_cols, ada_w, m_ada_w, v_ada_w, "ada_bwd_adamw")

    def by_name(big, small):
        ga, gb, go = split(big[1])
        return dict(small, ada_w=big[2], w_in=big[0], w_proj_a=ga, w_proj_b=gb, w_out=go)

    order = ["ada_w", "ada_b", "norm_w", "w_in", "gm_ln_w", "gm_ln_b", "gm_ws", "gm_bs", "conv_w", "conv_b", "dt_bias", "a_log",
             "d_skip", "ssm_norm_w", "w_proj_a", "w_proj_b", "w_out", "final_norm_w"]
    outs = []
    for big, small in (((gr_in, gr_rows, g_ada), g_d), ((d_in, d_rows, d_ada), d_d), ((nm_in, nm_rows, nm_ada), nm_d),
                       ((nv_in, nv_rows, nv_ada), nv_d)):
        t = by_name(big, small)
        outs += [t[n] for n in order]
    return (loss, dx.reshape(1, -1, D), *outs)
```

```python
import jax
import jax.numpy as jnp
from jax import lax
from jax.experimental import pallas as pl
from jax.experimental.pallas import tpu as pltpu

f32 = jnp.float32
bf16 = jnp.bfloat16

D = 1024
DEPTH = 2
EPS = 1e-6
CH = 128
NG = 8
HPG = 4
HD = 64
NH = NG * HPG
NST = 128
DIN = 2048
CONVD = 4096
GW = DIN // NG
PB = CONVD + DIN + 256
PA = 3 * D
PG = 2 * D
N_IN = 11296
NSHARD = 4
V7X_VMEM_BYTES = 64 * 2 ** 20
VMEM_LIMIT = V7X_VMEM_BYTES - 8 * 2 ** 20

ADAM_LR, ADAM_B1, ADAM_B2, ADAM_EPS, ADAM_WD, ADAM_STEP = 0.001, 0.9, 0.999, 1e-08, 0.01, 10

_HI = lax.Precision.HIGHEST


def _cp(sem):
    return pltpu.CompilerParams(dimension_semantics=sem, vmem_limit_bytes=VMEM_LIMIT)


def _sigmoid(x):
    return 0.5 * jnp.tanh(0.5 * x) + 0.5


def _silu_and_grad(x):
    s = _sigmoid(x)
    return x * s, s * (1.0 + x * (1.0 - s))


_GELU_K = 0.7978845608028654
_GELU_C = 0.044715


def _gelu_and_grad(x):
    x2 = x * x
    t = jnp.tanh(_GELU_K * (x + _GELU_C * x * x2))
    g = 0.5 * x * (1.0 + t)
    dg = 0.5 * (1.0 + t) + 0.5 * x * (1.0 - t * t) * _GELU_K * (1.0 + 3.0 * _GELU_C * x2)
    return g, dg


def _gelu(x):
    t = jnp.tanh(_GELU_K * (x + _GELU_C * x * x * x))
    return 0.5 * x * (1.0 + t)


def _softplus(x):
    return jnp.maximum(x, 0.0) + jnp.log(1.0 + jnp.exp(-jnp.abs(x)))


def _dot(a, b):
    return jnp.dot(a, b, preferred_element_type=f32)


def _dot_nt(a, b):
    return lax.dot_general(a, b, (((1,), (1,)), ((), ())), preferred_element_type=f32)


def _dot_tn(a, b):
    return lax.dot_general(a, b, (((0,), (0,)), ((), ())), preferred_element_type=f32)


def _dot_hi(a, b):
    return jnp.dot(a, b, precision=_HI, preferred_element_type=f32)


def _rmsmod_fwd(x, nw, scale, shift, name):
    s = x.shape[0]
    ts = min(512, s)

    def body(x_ref, nw_ref, sc_ref, sh_ref, h_ref):
        xv = x_ref[...]
        r = lax.rsqrt(jnp.mean(xv * xv, axis=-1, keepdims=True) + EPS)
        h_ref[...] = ((xv * r) * nw_ref[...] * (1.0 + sc_ref[...]) + sh_ref[...]).astype(bf16)

    row = pl.BlockSpec((1, D), lambda i: (0, 0))
    tile = pl.BlockSpec((ts, D), lambda i: (i, 0))
    return pl.pallas_call(
        body, grid=(s // ts,), in_specs=[tile, row, row, row], out_specs=tile,
        out_shape=jax.ShapeDtypeStruct((s, D), bf16), name=name, compiler_params=_cp(("parallel",)),
    )(x, nw, scale, shift)


def _rmsmod_bwd(dh, x, dres, nw, scale, name):
    s = x.shape[0]
    ts = min(512, s)

    def body(dh_ref, x_ref, dres_ref, nw_ref, sc_ref, dx_ref, dsc_ref, dsh_ref, dnw_ref):
        @pl.when(pl.program_id(0) == 0)
        def _():
            dsc_ref[...] = jnp.zeros_like(dsc_ref)
            dsh_ref[...] = jnp.zeros_like(dsh_ref)
            dnw_ref[...] = jnp.zeros_like(dnw_ref)
        xv = x_ref[...]
        dhv = dh_ref[...]
        r = lax.rsqrt(jnp.mean(xv * xv, axis=-1, keepdims=True) + EPS)
        xn = xv * r
        one_sc = 1.0 + sc_ref[...]
        dsc_ref[...] += jnp.sum(dhv * xn * nw_ref[...], axis=0, keepdims=True)
        dsh_ref[...] += jnp.sum(dhv, axis=0, keepdims=True)
        dnw_ref[...] += jnp.sum(dhv * xn * one_sc, axis=0, keepdims=True)
        dxn = dhv * (nw_ref[...] * one_sc)
        dx_ref[...] = r * (dxn - xn * jnp.mean(dxn * xn, axis=-1, keepdims=True)) + dres_ref[...]

    row = pl.BlockSpec((1, D), lambda i: (0, 0))
    tile = pl.BlockSpec((ts, D), lambda i: (i, 0))
    vec = jax.ShapeDtypeStruct((1, D), f32)
    return pl.pallas_call(
        body, grid=(s // ts,), in_specs=[tile, tile, tile, row, row], out_specs=[tile, row, row, row],
        out_shape=[jax.ShapeDtypeStruct((s, D), f32), vec, vec, vec], name=name, compiler_params=_cp(("arbitrary",)),
    )(dh, x, dres, nw, scale)


def _pick(n, prefs):
    for p in prefs:
        if n % p == 0:
            return p
    return n


def _mm(a, b, out_dtype, name, c_in=None):
    m, k = a.shape
    n = b.shape[1]
    tm = _pick(m, (1024, 512, 256))
    tn = _pick(n, (1280, 1024, 512))
    tk = _pick(k, (1280, 1024, 512))
    nk = k // tk

    def body(*refs):
        if c_in is not None:
            a_ref, b_ref, c_ref, o_ref, acc = refs
        else:
            a_ref, b_ref, o_ref, acc = refs
        kk = pl.program_id(2)

        @pl.when(kk == 0)
        def _():
            if c_in is not None:
                acc[...] = c_ref[...]
            else:
                acc[...] = jnp.zeros_like(acc)
        acc[...] += _dot(a_ref[...], b_ref[...])

        @pl.when(kk == nk - 1)
        def _():
            o_ref[...] = acc[...].astype(out_dtype)

    in_specs = [pl.BlockSpec((tm, tk), lambda j, i, kk: (i, kk)), pl.BlockSpec((tk, tn), lambda j, i, kk: (kk, j))]
    args = [a, b]
    if c_in is not None:
        in_specs.append(pl.BlockSpec((tm, tn), lambda j, i, kk: (i, j)))
        args.append(c_in)
    return pl.pallas_call(
        body, grid=(n // tn, m // tm, nk), in_specs=in_specs, out_specs=pl.BlockSpec((tm, tn), lambda j, i, kk: (i, j)),
        out_shape=jax.ShapeDtypeStruct((m, n), out_dtype), scratch_shapes=[pltpu.VMEM((tm, tn), f32)], name=name,
        compiler_params=_cp(("parallel", "parallel", "arbitrary")),
    )(*args)


def _mm_tn(a, b, name):
    t, k1 = a.shape
    n = b.shape[1]
    t1 = _pick(k1, (1280, 1024, 512))
    tn = _pick(n, (1280, 1024, 512))
    tt = _pick(t, (2048, 1024, 512, 256))
    nt = t // tt

    def body(a_ref, b_ref, o_ref):
        tt_i = pl.program_id(2)

        @pl.when(tt_i == 0)
        def _():
            o_ref[...] = jnp.zeros_like(o_ref)
        o_ref[...] += _dot_tn(a_ref[...], b_ref[...])

    return pl.pallas_call(
        body, grid=(k1 // t1, n // tn, nt),
        in_specs=[pl.BlockSpec((tt, t1), lambda i, j, tt_i: (tt_i, i)), pl.BlockSpec((tt, tn), lambda i, j, tt_i: (tt_i, j))],
        out_specs=pl.BlockSpec((t1, tn), lambda i, j, tt_i: (i, j)),
        out_shape=jax.ShapeDtypeStruct((k1, n), f32), name=name,
        compiler_params=_cp(("parallel", "parallel", "arbitrary")),
    )(a, b)


def _ln_stats(v):
    mu = jnp.mean(v, axis=-1, keepdims=True)
    vc = v - mu
    rstd = lax.rsqrt(jnp.mean(vc * vc, axis=-1, keepdims=True) + EPS)
    return vc * rstd, rstd


def _mix(w_ref, vl):
    return jnp.concatenate([_dot(w_ref[g], vl[:, g * CH:(g + 1) * CH]) for g in range(NG)], axis=1)


def _branch_a_fwd(proj_a, lnw, lnb, wsm, bsf, name):
    s = proj_a.shape[0]
    ta = min(256, s)

    def body(pu_ref, pv_ref, pz_ref, lnw_ref, lnb_ref, w_ref, bs_ref, ya_ref):
        for c in range(ta // CH):
            rows = pl.ds(c * CH, CH)
            vh, _ = _ln_stats(_gelu(pv_ref[rows, :]))
            vl = (vh * lnw_ref[...] + lnb_ref[...]).astype(bf16)
            mixed = _mix(w_ref, vl) + bs_ref[...]
            pz = pz_ref[rows, :]
            ya_ref[rows, :] = (_gelu(pu_ref[rows, :]) * mixed * (pz * _sigmoid(pz))).astype(bf16)

    row = pl.BlockSpec((1, D), lambda i: (0, 0))
    return pl.pallas_call(
        body, grid=(s // ta,),
        in_specs=[pl.BlockSpec((ta, D), lambda i: (i, 0)), pl.BlockSpec((ta, D), lambda i: (i, 1)),
                  pl.BlockSpec((ta, D), lambda i: (i, 2)), row, row,
                  pl.BlockSpec((NG, CH, CH), lambda i: (0, 0, 0)), pl.BlockSpec((CH, D), lambda i: (0, 0))],
        out_specs=pl.BlockSpec((ta, D), lambda i: (i, 0)),
        out_shape=jax.ShapeDtypeStruct((s, D), bf16), name=name, compiler_params=_cp(("parallel",)),
    )(proj_a, proj_a, proj_a, lnw, lnb, wsm, bsf)


def _branch_a_bwd(proj_a, dya, lnw, lnb, wsm, wsm_t, bsf, name):
    s = proj_a.shape[0]
    ta = min(256, s)

    def body(pu_ref, pv_ref, pz_ref, dya_ref, lnw_ref, lnb_ref, w_ref, wt_ref, bs_ref,
             dp_ref, dws_ref, dbs_ref, dlnw_ref, dlnb_ref):
        @pl.when(pl.program_id(0) == 0)
        def _():
            dws_ref[...] = jnp.zeros_like(dws_ref)
            dbs_ref[...] = jnp.zeros_like(dbs_ref)
            dlnw_ref[...] = jnp.zeros_like(dlnw_ref)
            dlnb_ref[...] = jnp.zeros_like(dlnb_ref)
        for c in range(ta // CH):
            rows = pl.ds(c * CH, CH)
            u, du = _gelu_and_grad(pu_ref[rows, :])
            v, dv_act = _gelu_and_grad(pv_ref[rows, :])
            zg, dzg = _silu_and_grad(pz_ref[rows, :])
            vh, rstd = _ln_stats(v)
            vl = (vh * lnw_ref[...] + lnb_ref[...]).astype(bf16)
            mixed = _mix(w_ref, vl) + bs_ref[...]
            dy = dya_ref[rows, :].astype(f32)
            dmixed = dy * u * zg
            dp_ref[rows, 0:D] = (dy * mixed * zg * du).astype(bf16)
            dp_ref[rows, 2 * D:3 * D] = (dy * u * mixed * dzg).astype(bf16)
            dmb = dmixed.astype(bf16)
            dbs_ref[...] += dmixed
            dvl = _mix(wt_ref, dmb)
            for g in range(NG):
                cols = slice(g * CH, (g + 1) * CH)
                dws_ref[g] += _dot_nt(dmb[:, cols], vl[:, cols])
            dlnw_ref[...] += jnp.sum(dvl * vh, axis=0, keepdims=True)
            dlnb_ref[...] += jnp.sum(dvl, axis=0, keepdims=True)
            dvh = dvl * lnw_ref[...]
            dv = rstd * (dvh - jnp.mean(dvh, axis=-1, keepdims=True) - vh * jnp.mean(dvh * vh, axis=-1, keepdims=True))
            dp_ref[rows, D:2 * D] = (dv * dv_act).astype(bf16)

    row = pl.BlockSpec((1, D), lambda i: (0, 0))
    wspec = pl.BlockSpec((NG, CH, CH), lambda i: (0, 0, 0))
    full = pl.BlockSpec((CH, D), lambda i: (0, 0))
    return pl.pallas_call(
        body, grid=(s // ta,),
        in_specs=[pl.BlockSpec((ta, D), lambda i: (i, 0)), pl.BlockSpec((ta, D), lambda i: (i, 1)),
                  pl.BlockSpec((ta, D), lambda i: (i, 2)), pl.BlockSpec((ta, D), lambda i: (i, 0)),
                  row, row, wspec, wspec, full],
        out_specs=[pl.BlockSpec((ta, PA), lambda i: (i, 0)), wspec, full, row, row],
        out_shape=[jax.ShapeDtypeStruct((s, PA), bf16), jax.ShapeDtypeStruct((NG, CH, CH), f32),
                   jax.ShapeDtypeStruct((CH, D), f32), jax.ShapeDtypeStruct((1, D), f32), jax.ShapeDtypeStruct((1, D), f32)],
        name=name, compiler_params=_cp(("arbitrary",)),
    )(proj_a, proj_a, proj_a, dya, lnw, lnb, wsm, wsm_t, bsf)


GB = GW + 2 * NST


def _group_major(xs, b, c):
    lead = xs.shape[:-1]
    return jnp.concatenate([xs.reshape(lead + (NG, GW)), b.reshape(lead + (NG, NST)), c.reshape(lead + (NG, NST))],
                           axis=-1).reshape(lead + (CONVD,))


def _from_group_major(t):
    lead = t.shape[:-1]
    t = t.reshape(lead + (NG, GB))
    return jnp.concatenate([t[..., 0:GW].reshape(lead + (DIN,)), t[..., GW:GW + NST].reshape(lead + (NG * NST,)),
                            t[..., GW + NST:GB].reshape(lead + (NG * NST,))], axis=-1)


def _rows_from_group_major(t):
    t = t.reshape(NG, GB, t.shape[-1])
    return jnp.concatenate([t[:, 0:GW].reshape(DIN, -1), t[:, GW:GW + NST].reshape(NG * NST, -1),
                            t[:, GW + NST:GB].reshape(NG * NST, -1)], axis=0)


def _shift_rows(x, prev8, j):
    xr = pltpu.roll(x, j, 0)
    fix = pltpu.roll(prev8, j, 0)
    rid = lax.broadcasted_iota(jnp.int32, (8, x.shape[1]), 0)
    top = jnp.where(rid < j, fix, xr[0:8])
    return jnp.concatenate([top, xr[8:]], axis=0)


def _shift_rows_up(d, next8, j):
    dr = pltpu.roll(d, CH - j, 0)
    fix = pltpu.roll(next8, 8 - j, 0)
    rid = lax.broadcasted_iota(jnp.int32, (8, d.shape[1]), 0)
    bot = jnp.where(rid >= 8 - j, fix, dr[CH - 8:CH])
    return jnp.concatenate([dr[0:CH - 8], bot], axis=0)


def _conv_pre(x, prev8, cw_ref, cb_ref, cols):
    shifted = [_shift_rows(x, prev8, j) for j in (1, 2, 3)]
    conv = cb_ref[:, cols] + cw_ref[3:4, cols] * x
    for j in (1, 2, 3):
        conv = conv + cw_ref[3 - j:4 - j, cols] * shifted[j - 1]
    return conv, shifted


def _tril_mask():
    return lax.broadcasted_iota(jnp.int32, (CH, CH), 0) >= lax.broadcasted_iota(jnp.int32, (CH, CH), 1)


def _sum_all(v):
    return jnp.sum(jnp.sum(v, axis=0, keepdims=True), axis=1, keepdims=True)


def _lanes(g, width, base=0):
    return pl.ds(pl.multiple_of(base + g * width, width), width)


def _branch_b_fwd(proj_b, cw, cb, dtb, a_row, dkc, snw, name):
    s = proj_b.shape[0]
    nc = s // CH

    def body(xbc_ref, sz_ref, dtr_ref, cw_ref, cb_ref, dtb_ref, a_ref, dkc_ref, snw_ref,
             yb_ref, y_ref, st_ref, cv_ref, prev8, state, acst_s):
        @pl.when(pl.program_id(0) == 0)
        def _():
            prev8[...] = jnp.zeros_like(prev8)
            state[...] = jnp.zeros_like(state)
        st_ref[0] = state[...].astype(bf16)
        mask = _tril_mask()
        dt_all = _softplus(dtr_ref[:, 0:CH] + dtb_ref[...])
        acs_all = _dot_hi(mask.astype(f32), dt_all * a_ref[...])
        acst_s[...] = acs_all.T

        def group(g, carry):
            cols = _lanes(g, GB)
            gcols = _lanes(g, GW)
            x = xbc_ref[:, cols]
            conv, _ = _conv_pre(x, prev8[:, cols], cw_ref, cb_ref, cols)
            prev8[:, cols] = x[CH - 8:CH]
            cv_ref[:, cols] = conv.astype(bf16)
            xc = conv * _sigmoid(conv)
            xs = xc[:, 0:GW]
            bg = xc[:, GW:GW + NST].astype(bf16)
            cg = xc[:, GW + NST:GB].astype(bf16)
            back = lax.rem(CH - HPG * g, CH)
            dt = pltpu.roll(dt_all, back, 1)
            acs = pltpu.roll(acs_all, back, 1)
            cbm = _dot_nt(cg, bg)
            dkc_g = dkc_ref[:, gcols]
            y_parts = []
            for r in range(HPG):
                colb = jnp.broadcast_to(acs[:, r:r + 1], (CH, CH))
                row = acst_s[pl.ds(g * HPG + r, 1), :]
                lmat = jnp.exp(jnp.where(mask, colb - row, -jnp.inf))
                xr = xs[:, r * HD:(r + 1) * HD]
                xd = xr * dt[:, r:r + 1]
                sp = state[g * HPG + r]
                col = colb[:, 0:HD]
                alast = colb[CH - 1:CH, 0:HD]
                y_r = _dot((cbm * lmat).astype(bf16), xd.astype(bf16))
                y_r = y_r + jnp.exp(col) * _dot_nt(cg, sp.astype(bf16))
                y_parts.append(y_r + xr * dkc_g[:, r * HD:(r + 1) * HD])
                cs = _dot_tn((xd * jnp.exp(alast - col)).astype(bf16), bg)
                state[g * HPG + r] = jnp.exp(colb[CH - 1:CH, :]) * sp + cs
            y = jnp.concatenate(y_parts, axis=1)
            szv = sz_ref[:, gcols]
            yz = y * (szv * _sigmoid(szv))
            rr = lax.rsqrt(jnp.mean(yz * yz, axis=-1, keepdims=True) + EPS)
            yb_ref[:, gcols] = (yz * rr * snw_ref[:, gcols]).astype(bf16)
            y_ref[:, gcols] = y.astype(bf16)
            return carry

        lax.fori_loop(0, NG, group, 0)

    const2 = lambda c: (0, 0)
    return pl.pallas_call(
        body, grid=(nc,),
        in_specs=[pl.BlockSpec((CH, CONVD), lambda c: (c, 0)), pl.BlockSpec((CH, DIN), lambda c: (c, CONVD // DIN)),
                  pl.BlockSpec((CH, 256), lambda c: (c, (CONVD + DIN) // 256)),
                  pl.BlockSpec((4, CONVD), const2), pl.BlockSpec((1, CONVD), const2),
                  pl.BlockSpec((1, CH), const2), pl.BlockSpec((1, CH), const2),
                  pl.BlockSpec((1, DIN), const2), pl.BlockSpec((1, DIN), const2)],
        out_specs=[pl.BlockSpec((CH, DIN), lambda c: (c, 0)), pl.BlockSpec((CH, DIN), lambda c: (c, 0)),
                   pl.BlockSpec((1, NH, HD, NST), lambda c: (c, 0, 0, 0)), pl.BlockSpec((CH, CONVD), lambda c: (c, 0))],
        out_shape=[jax.ShapeDtypeStruct((s, DIN), bf16), jax.ShapeDtypeStruct((s, DIN), bf16),
                   jax.ShapeDtypeStruct((nc, NH, HD, NST), bf16), jax.ShapeDtypeStruct((s, CONVD), bf16)],
        scratch_shapes=[pltpu.VMEM((8, CONVD), f32), pltpu.VMEM((NH, HD, NST), f32), pltpu.VMEM((CH, CH), f32)],
        name=name, compiler_params=_cp(("arbitrary",)),
    )(proj_b, proj_b, proj_b, cw, cb, dtb, a_row, dkc, snw)


def _branch_b_bwd(proj_b, conv_sv, dyb, y_sv, states, cw, dtb, a_row, dkc, snw, ind, name):
    s = proj_b.shape[0]
    nc = s // CH

    def body(xbc_ref, cv_ref, sz_ref, dtr_ref, dyb_ref, y_ref, st_ref, cw_ref, dtb_ref, a_ref, dkc_ref,
             snw_ref, ind_ref, dp_ref, dcw_ref, dcb_ref, ddtb_ref, dal_ref, ddk_ref, dsnw_ref,
             dstate, dnext8, acst_s, dacs_acc, q2_acc):
        @pl.when(pl.program_id(0) == 0)
        def _():
            dstate[...] = jnp.zeros_like(dstate)
            dnext8[...] = jnp.zeros_like(dnext8)
            dcw_ref[...] = jnp.zeros_like(dcw_ref)
            dcb_ref[...] = jnp.zeros_like(dcb_ref)
            ddtb_ref[...] = jnp.zeros_like(ddtb_ref)
            dal_ref[...] = jnp.zeros_like(dal_ref)
            ddk_ref[...] = jnp.zeros_like(ddk_ref)
            dsnw_ref[...] = jnp.zeros_like(dsnw_ref)

        dacs_acc[...] = jnp.zeros_like(dacs_acc)
        q2_acc[...] = jnp.zeros_like(q2_acc)
        mask = _tril_mask()
        tri_t = (lax.broadcasted_iota(jnp.int32, (CH, CH), 0) <= lax.broadcasted_iota(jnp.int32, (CH, CH), 1)).astype(f32)
        lane1 = lax.broadcasted_iota(jnp.int32, (1, CH), 1)
        is_last = lax.broadcasted_iota(jnp.int32, (CH, 1), 0) == CH - 1
        z_all = dtr_ref[:, 0:CH] + dtb_ref[...]
        dt_all = _softplus(z_all)
        adt_all = dt_all * a_ref[...]
        acs_all = _dot_hi(mask.astype(f32), adt_all)
        acst_s[...] = acs_all.T

        def ind_sum(v):
            hi = v.astype(bf16)
            lo = (v - hi.astype(f32)).astype(bf16)
            return _dot(hi, ind_ref[...]) + _dot(lo, ind_ref[...])

        def group(g, carry):
            cols = _lanes(g, GB)
            gcols = _lanes(g, GW)
            conv = cv_ref[:, cols].astype(f32)
            sg = _sigmoid(conv)
            xc = conv * sg
            xs = xc[:, 0:GW]
            bg = xc[:, GW:GW + NST].astype(bf16)
            cg = xc[:, GW + NST:GB].astype(bf16)

            y = y_ref[:, gcols].astype(f32)
            silu_sz, dsilu_sz = _silu_and_grad(sz_ref[:, gcols])
            yz = y * silu_sz
            rr = lax.rsqrt(jnp.mean(yz * yz, axis=-1, keepdims=True) + EPS)
            dyb_g = dyb_ref[:, gcols].astype(f32)
            w = dyb_g * snw_ref[:, gcols]
            dsnw_ref[:, gcols] += jnp.sum(dyb_g * yz * rr, axis=0, keepdims=True)
            dyz = rr * w - yz * (rr * rr * rr) * jnp.mean(w * yz, axis=-1, keepdims=True)
            dp_ref[:, _lanes(g, GW, CONVD)] = (dyz * y * dsilu_sz).astype(bf16)
            dy_g = dyz * silu_sz
            ddk_ref[:, gcols] += jnp.sum(dy_g * xs, axis=0, keepdims=True)

            back = lax.rem(CH - HPG * g, CH)
            dt = pltpu.roll(dt_all, back, 1)
            acs = pltpu.roll(acs_all, back, 1)
            cbm = _dot_nt(cg, bg)
            d_cb = jnp.zeros((CH, CH), f32)
            d_bg = jnp.zeros((CH, NST), f32)
            d_cg = jnp.zeros((CH, NST), f32)
            lastrow = jnp.zeros((1, CH), f32)
            dxd_parts, dxs_parts, t_parts = [], [], []
            for r in range(HPG):
                h = g * HPG + r
                colb = jnp.broadcast_to(acs[:, r:r + 1], (CH, CH))
                row = acst_s[pl.ds(h, 1), :]
                lmat = jnp.exp(jnp.where(mask, colb - row, -jnp.inf))
                mmat_b = (cbm * lmat).astype(bf16)
                dtc = dt[:, r:r + 1]
                xd = xs[:, r * HD:(r + 1) * HD] * dtc
                col = colb[:, 0:HD]
                alast = colb[CH - 1:CH, 0:HD]
                dte = jnp.exp(alast - col)
                ea = jnp.exp(col)
                cd = jnp.exp(colb[CH - 1:CH, :])
                sp = st_ref[0, h]
                dsn = dstate[h]
                dsn_b = dsn.astype(bf16)
                dyr = dy_g[:, r * HD:(r + 1) * HD]
                dyr_b = dyr.astype(bf16)
                dye_b = (dyr * ea).astype(bf16)
                d_cg = d_cg + _dot(dye_b, sp)
                dxde = _dot_nt(bg, dsn_b)
                xdte = xd * dte
                xd_b = xd.astype(bf16)
                d_bg = d_bg + _dot(xdte.astype(bf16), dsn_b)
                dxd_diag = _dot_tn(mmat_b, dyr_b)
                dxd = dxde * dte + dxd_diag
                d_cb = d_cb + _dot_nt(dyr_b, xd_b) * lmat
                t_parts.append(dyr_b.astype(f32) * _dot(mmat_b, xd_b) + dyr * (ea * _dot_nt(cg, sp))
                               - xd_b.astype(f32) * dxd_diag - dxde * xdte)
                lastrow = jnp.where(lane1 == r, _sum_all(dsn * sp.astype(f32)) * cd + _sum_all(dxde * xdte), lastrow)
                dstate[h] = cd * dsn + _dot_tn(dye_b, cg)
                dxd_parts.append(dxd)
                dxs_parts.append(dxd * dtc)
            d_cb_b = d_cb.astype(bf16)
            d_bg = d_bg + _dot_tn(d_cb_b, cg)
            d_cg = d_cg + _dot(d_cb_b, bg)
            q2 = ind_sum(jnp.concatenate(dxd_parts, axis=1) * xs)
            dacs = ind_sum(jnp.concatenate(t_parts, axis=1)) + jnp.where(is_last, lastrow, 0.0)
            dacs_acc[...] += pltpu.roll(dacs, HPG * g, 1)
            q2_acc[...] += pltpu.roll(q2, HPG * g, 1)
            dxs = jnp.concatenate(dxs_parts, axis=1) + dy_g * dkc_ref[:, gcols]

            dconv = jnp.concatenate([dxs, d_bg, d_cg], axis=1) * (sg * (1.0 + conv * (1.0 - sg)))
            x = xbc_ref[:, cols]
            dcb_ref[:, cols] += jnp.sum(dconv, axis=0, keepdims=True)
            dcw_ref[3:4, cols] += jnp.sum(dconv * x, axis=0, keepdims=True)
            dx = cw_ref[3:4, cols] * dconv
            nxt = dnext8[:, cols]
            for j in (1, 2, 3):
                up = _shift_rows_up(dconv, nxt, j)
                dcw_ref[3 - j:4 - j, cols] += jnp.sum(up * x, axis=0, keepdims=True)
                dx = dx + cw_ref[3 - j:4 - j, cols] * up
            dnext8[:, cols] = dconv[0:8]
            dp_ref[:, cols] = dx.astype(bf16)
            return carry

        lax.fori_loop(0, NG, group, 0)

        dadt = _dot_hi(tri_t, dacs_acc[...])
        dal_ref[...] += jnp.sum(dadt * adt_all, axis=0, keepdims=True)
        ddz = (dadt * a_ref[...] + q2_acc[...]) * _sigmoid(z_all)
        ddtb_ref[...] += jnp.sum(ddz, axis=0, keepdims=True)
        dp_ref[:, CONVD + DIN:CONVD + DIN + CH] = ddz.astype(bf16)
        dp_ref[:, CONVD + DIN + CH:PB] = jnp.zeros((CH, PB - CONVD - DIN - CH), bf16)

    const2 = lambda c: (0, 0)
    rev = lambda c: (nc - 1 - c, 0)
    return pl.pallas_call(
        body, grid=(nc,),
        in_specs=[pl.BlockSpec((CH, CONVD), rev), pl.BlockSpec((CH, CONVD), rev),
                  pl.BlockSpec((CH, DIN), lambda c: (nc - 1 - c, CONVD // DIN)),
                  pl.BlockSpec((CH, 256), lambda c: (nc - 1 - c, (CONVD + DIN) // 256)),
                  pl.BlockSpec((CH, DIN), rev), pl.BlockSpec((CH, DIN), rev),
                  pl.BlockSpec((1, NH, HD, NST), lambda c: (nc - 1 - c, 0, 0, 0)),
                  pl.BlockSpec((4, CONVD), const2), pl.BlockSpec((1, CH), const2), pl.BlockSpec((1, CH), const2),
                  pl.BlockSpec((1, DIN), const2), pl.BlockSpec((1, DIN), const2), pl.BlockSpec((GW, CH), const2)],
        out_specs=[pl.BlockSpec((CH, PB), rev), pl.BlockSpec((4, CONVD), const2), pl.BlockSpec((1, CONVD), const2),
                   pl.BlockSpec((1, CH), const2), pl.BlockSpec((1, CH), const2), pl.BlockSpec((1, DIN), const2),
                   pl.BlockSpec((1, DIN), const2)],
        out_shape=[jax.ShapeDtypeStruct((s, PB), bf16), jax.ShapeDtypeStruct((4, CONVD), f32),
                   jax.ShapeDtypeStruct((1, CONVD), f32), jax.ShapeDtypeStruct((1, CH), f32),
                   jax.ShapeDtypeStruct((1, CH), f32), jax.ShapeDtypeStruct((1, DIN), f32),
                   jax.ShapeDtypeStruct((1, DIN), f32)],
        scratch_shapes=[pltpu.VMEM((NH, HD, NST), f32), pltpu.VMEM((8, CONVD), f32), pltpu.VMEM((CH, CH), f32),
                        pltpu.VMEM((CH, CH), f32), pltpu.VMEM((CH, CH), f32)],
        name=name, compiler_params=_cp(("arbitrary",)),
    )(proj_b, conv_sv, proj_b, proj_b, dyb, y_sv, states, cw, dtb, a_row, dkc, snw, ind)


def _merge_fwd(ya, yb, proj_g, x, gate, wa, wb, wo, name):
    s = x.shape[0]
    ts = min(512, s)

    def body(ya_ref, yb_ref, ga_ref, gb_ref, x_ref, gate_ref, wa_ref, wb_ref, wo_ref, xo_ref, pa_ref, pb_ref, mg_ref, o_ref):
        pa = _dot(ya_ref[...], wa_ref[...])
        pb = _dot(yb_ref[...], wb_ref[...])
        mg = (_sigmoid(ga_ref[...]) * pa + _sigmoid(gb_ref[...]) * pb).astype(bf16)
        o = _dot(mg, wo_ref[...])
        xo_ref[...] = x_ref[...] + gate_ref[...] * o
        pa_ref[...] = pa.astype(bf16)
        pb_ref[...] = pb.astype(bf16)
        mg_ref[...] = mg
        o_ref[...] = o.astype(bf16)

    tile = pl.BlockSpec((ts, D), lambda i: (i, 0))
    const = lambda i: (0, 0)
    act = jax.ShapeDtypeStruct((s, D), bf16)
    return pl.pallas_call(
        body, grid=(s // ts,),
        in_specs=[tile, pl.BlockSpec((ts, DIN), lambda i: (i, 0)), tile, pl.BlockSpec((ts, D), lambda i: (i, 1)), tile,
                  pl.BlockSpec((1, D), const), pl.BlockSpec((D, D), const), pl.BlockSpec((DIN, D), const),
                  pl.BlockSpec((D, D), const)],
        out_specs=[tile, tile, tile, tile, tile],
        out_shape=[jax.ShapeDtypeStruct((s, D), f32), act, act, act, act],
        name=name, compiler_params=_cp(("parallel",)),
    )(ya, yb, proj_g, proj_g, x, gate, wa, wb, wo)


def _merge_bwd(dxo, gate, o_sv, pa_sv, pb_sv, proj_g, wo_t, wa_t, wb_t, name):
    s = dxo.shape[0]
    ts = min(512, s)

    def body(dxo_ref, gate_ref, o_ref, pa_ref, pb_ref, ga_ref, gb_ref, wot_ref, wat_ref, wbt_ref,
             do_ref, dpa_ref, dpb_ref, dg_ref, dya_ref, dyb_ref, dgate_ref):
        @pl.when(pl.program_id(0) == 0)
        def _():
            dgate_ref[...] = jnp.zeros_like(dgate_ref)
        dxo_v = dxo_ref[...]
        dgate_ref[...] += jnp.sum(dxo_v * o_ref[...].astype(f32), axis=0, keepdims=True)
        do = (dxo_v * gate_ref[...]).astype(bf16)
        do_ref[...] = do
        dmg = _dot(do, wot_ref[...])
        sa = _sigmoid(ga_ref[...])
        sb = _sigmoid(gb_ref[...])
        dpa = (dmg * sa).astype(bf16)
        dpb = (dmg * sb).astype(bf16)
        dpa_ref[...] = dpa
        dpb_ref[...] = dpb
        dg_ref[:, 0:D] = (dmg * pa_ref[...].astype(f32) * sa * (1.0 - sa)).astype(bf16)
        dg_ref[:, D:2 * D] = (dmg * pb_ref[...].astype(f32) * sb * (1.0 - sb)).astype(bf16)
        dya_ref[...] = _dot(dpa, wat_ref[...]).astype(bf16)
        dyb_ref[...] = _dot(dpb, wbt_ref[...]).astype(bf16)

    tile = pl.BlockSpec((ts, D), lambda i: (i, 0))
    const = lambda i: (0, 0)
    act = jax.ShapeDtypeStruct((s, D), bf16)
    return pl.pallas_call(
        body, grid=(s // ts,),
        in_specs=[tile, pl.BlockSpec((1, D), const), tile, tile, tile, tile, pl.BlockSpec((ts, D), lambda i: (i, 1)),
                  pl.BlockSpec((D, D), const), pl.BlockSpec((D, D), const), pl.BlockSpec((D, DIN), const)],
        out_specs=[tile, tile, tile, pl.BlockSpec((ts, PG), lambda i: (i, 0)), tile, pl.BlockSpec((ts, DIN), lambda i: (i, 0)),
                   pl.BlockSpec((1, D), const)],
        out_shape=[act, act, act, jax.ShapeDtypeStruct((s, PG), bf16), act, jax.ShapeDtypeStruct((s, DIN), bf16),
                   jax.ShapeDtypeStruct((1, D), f32)],
        name=name, compiler_params=_cp(("arbitrary",)),
    )(dxo, gate, o_sv, pa_sv, pb_sv, proj_g, proj_g, wo_t, wa_t, wb_t)


def _final_loss(x, target, fnw, name):
    s = x.shape[0]
    ts = min(512, s)

    def body(x_ref, t_ref, w_ref, loss_ref, dx_ref, dw_ref):
        @pl.when(pl.program_id(0) == 0)
        def _():
            loss_ref[...] = jnp.zeros_like(loss_ref)
            dw_ref[...] = jnp.zeros_like(dw_ref)
        xv = x_ref[...]
        r = lax.rsqrt(jnp.mean(xv * xv, axis=-1, keepdims=True) + EPS)
        xn = xv * r
        err = xn * w_ref[...] - t_ref[...]
        part = jnp.sum(err * err, axis=0, keepdims=True)
        acc = part[:, 0:128]
        for k in range(1, D // 128):
            acc = acc + part[:, k * 128:(k + 1) * 128]
        loss_ref[0:1, :] += acc * (0.5 / D)
        dy = err * (1.0 / D)
        dw_ref[...] += jnp.sum(dy * xn, axis=0, keepdims=True)
        dxn = dy * w_ref[...]
        dx_ref[...] = r * (dxn - xn * jnp.mean(dxn * xn, axis=-1, keepdims=True))

    tile = pl.BlockSpec((ts, D), lambda i: (i, 0))
    row = pl.BlockSpec((1, D), lambda i: (0, 0))
    return pl.pallas_call(
        body, grid=(s // ts,), in_specs=[tile, tile, row],
        out_specs=[pl.BlockSpec((8, 128), lambda i: (0, 0)), tile, row],
        out_shape=[jax.ShapeDtypeStruct((8, 128), f32), jax.ShapeDtypeStruct((s, D), f32), jax.ShapeDtypeStruct((1, D), f32)],
        name=name, compiler_params=_cp(("arbitrary",)),
    )(x, target, fnw)


def _layer_operands(w_in, conv_w, wa, wb, wo, norm_w, gm_ln_w, gm_ln_b, gm_ws, gm_bs, conv_b, dt_bias, a_log, d_skip, ssm_norm_w):
    w_xbc = _group_major(w_in[:, 5120:7168], w_in[:, 7168:8192], w_in[:, 8192:9216])
    w_b = jnp.concatenate([w_xbc, w_in[:, 3072:5120], w_in[:, 9216:9248], jnp.zeros((D, 224), bf16)], axis=1)
    w_a = w_in[:, 0:3072]
    w_g = w_in[:, 9248:N_IN]
    tril = jnp.tril(jnp.ones((CH, CH), bool))
    wsm = jnp.where(tril[None], gm_ws, 0.0).astype(bf16)

    def heads_row(v):
        return jnp.pad(v, (0, CH - NH)).reshape(1, CH)

    return dict(
        w_b=w_b, w_a=w_a, w_g=w_g, w_b_t=w_b.T, w_a_t=w_a.T, w_g_t=w_g.T,
        wa=wa, wb=wb, wo=wo, wa_t=wa.T, wb_t=wb.T, wo_t=wo.T,
        norm_w=norm_w.reshape(1, D), lnw=gm_ln_w.reshape(1, D), lnb=gm_ln_b.reshape(1, D),
        wsm=wsm, wsm_t=jnp.swapaxes(wsm, 1, 2), bsf=jnp.repeat(gm_bs.T, CH, axis=1),
        cw=_group_major(conv_w[:, 0:DIN], conv_w[:, DIN:DIN + NG * NST], conv_w[:, DIN + NG * NST:CONVD]),
        cb=_group_major(conv_b[0:DIN], conv_b[DIN:DIN + NG * NST], conv_b[DIN + NG * NST:CONVD]).reshape(1, CONVD),
        dtb=heads_row(dt_bias), a_row=heads_row(-jnp.exp(a_log)),
        snw=ssm_norm_w.reshape(1, DIN), dkc=jnp.repeat(d_skip, HD).reshape(1, DIN),
        ind=(jnp.arange(GW)[:, None] // HD == jnp.arange(CH)[None, :]).astype(bf16),
    )


def _layer_fwd(x, shift, scale, gate, p, tag):
    h = _rmsmod_fwd(x, p["norm_w"], scale, shift, f"rmsmod_fwd{tag}")
    proj_b = _mm(h, p["w_b"], f32, f"proj_b{tag}")
    proj_a = _mm(h, p["w_a"], f32, f"proj_a{tag}")
    proj_g = _mm(h, p["w_g"], f32, f"proj_g{tag}")
    ya = _branch_a_fwd(proj_a, p["lnw"], p["lnb"], p["wsm"], p["bsf"], f"branch_a_fwd{tag}")
    yb, y_sv, states, conv_sv = _branch_b_fwd(proj_b, p["cw"], p["cb"], p["dtb"], p["a_row"], p["dkc"], p["snw"], f"branch_b_fwd{tag}")
    x_out, pa, pb, mg, o = _merge_fwd(ya, yb, proj_g, x, gate, p["wa"], p["wb"], p["wo"], f"merge_fwd{tag}")
    saved = dict(x=x, h=h, proj_b=proj_b, proj_a=proj_a, proj_g=proj_g, ya=ya, yb=yb, y=y_sv, states=states, conv=conv_sv,
                 pa=pa, pb=pb, mg=mg, o=o, scale=scale, gate=gate)
    return x_out, saved


def _layer_bwd(dxo, sv, p, tag):
    do, dpa, dpb, dg, dya, dyb, dgate = _merge_bwd(dxo, sv["gate"], sv["o"], sv["pa"], sv["pb"], sv["proj_g"],
                                                   p["wo_t"], p["wa_t"], p["wb_t"], f"merge_bwd{tag}")
    d_wo = _mm_tn(sv["mg"], do, f"d_wo{tag}")
    d_wa = _mm_tn(sv["ya"], dpa, f"d_wa{tag}")
    d_wb = _mm_tn(sv["yb"], dpb, f"d_wb{tag}")
    da, dws, dbs, dlnw, dlnb = _branch_a_bwd(sv["proj_a"], dya, p["lnw"], p["lnb"], p["wsm"], p["wsm_t"], p["bsf"],
                                             f"branch_a_bwd{tag}")
    db, dcw, dcb, ddtb, dal, ddk, dsnw = _branch_b_bwd(sv["proj_b"], sv["conv"], dyb, sv["y"], sv["states"], p["cw"], p["dtb"],
                                                       p["a_row"], p["dkc"], p["snw"], p["ind"], f"branch_b_bwd{tag}")
    dh = _mm(db, p["w_b_t"], f32, f"dh_b{tag}")
    dh = _mm(da, p["w_a_t"], f32, f"dh_a{tag}", c_in=dh)
    dh = _mm(dg, p["w_g_t"], f32, f"dh_g{tag}", c_in=dh)
    d_w_b_t = _mm_tn(db, sv["h"], f"d_w_b{tag}")
    d_w_a_t = _mm_tn(da, sv["h"], f"d_w_a{tag}")
    d_w_g_t = _mm_tn(dg, sv["h"], f"d_w_g{tag}")
    dx, dscale, dshift, dnw = _rmsmod_bwd(dh, sv["x"], dxo, p["norm_w"], sv["scale"], f"rmsmod_bwd{tag}")
    d_w_in_t = jnp.concatenate([d_w_a_t, d_w_b_t[CONVD:CONVD + DIN], _rows_from_group_major(d_w_b_t[0:CONVD]),
                                d_w_b_t[CONVD + DIN:CONVD + DIN + NH], d_w_g_t], axis=0)
    tril = jnp.tril(jnp.ones((CH, CH), bool))
    heads = lambda v: v[0, 0:NH]
    grads = dict(
        w_in_t=d_w_in_t, w_proj_a=d_wa, w_proj_b=d_wb, w_out=d_wo, conv_w=_from_group_major(dcw), conv_b=_from_group_major(dcb).reshape(CONVD),
        norm_w=dnw.reshape(D), gm_ln_w=dlnw.reshape(D), gm_ln_b=dlnb.reshape(D),
        gm_ws=jnp.where(tril[None], dws, 0.0), gm_bs=dbs.reshape(CH, NG, CH).sum(-1).T,
        dt_bias=heads(ddtb), a_log=heads(dal), d_skip=ddk.reshape(NH, HD).sum(-1), ssm_norm_w=dsnw.reshape(DIN),
        mod=jnp.concatenate([dshift, dscale, dgate], axis=1).reshape(3 * D),
    )
    return dx, grads


def _local_step(x, target, mods, layer_ops, fnw):
    saved = []
    for l in range(DEPTH):
        shift, scale, gate = mods[l]
        x, sv = _layer_fwd(x, shift, scale, gate, layer_ops[l], f"_l{l}")
        saved.append(sv)
    loss_parts, dx, dfnw = _final_loss(x, target, fnw.reshape(1, D), "final_loss")
    grads = [None] * DEPTH
    for l in reversed(range(DEPTH)):
        dx, grads[l] = _layer_bwd(dx, saved[l], layer_ops[l], f"_l{l}")
    return loss_parts, dx, grads, dfnw.reshape(D)


ADA_COLS = 3 * D // NSHARD


def _ada_fwd(c_all, ada_w, ada_b_cols, name):
    def body(c_ref, w_ref, b_ref, o_ref):
        cv = c_ref[...]
        sc = cv * _sigmoid(cv)
        for l in range(DEPTH):
            o_ref[l] = _dot_hi(sc, w_ref[l]) + b_ref[l]

    return pl.pallas_call(body, out_shape=jax.ShapeDtypeStruct((DEPTH, 8, ADA_COLS), f32), name=name,
                          compiler_params=_cp(None))(c_all, ada_w, ada_b_cols)


def _adam_math(w, g, m, v):
    m = ADAM_B1 * m + (1.0 - ADAM_B1) * g
    v = ADAM_B2 * v + (1.0 - ADAM_B2) * (g * g)
    m_hat = m / (1.0 - ADAM_B1 ** ADAM_STEP)
    v_hat = v / (1.0 - ADAM_B2 ** ADAM_STEP)
    delta = -ADAM_LR * (m_hat / (jnp.sqrt(v_hat) + ADAM_EPS) + ADAM_WD * w)
    return delta, m, v


def _ada_bwd_adamw(c_all, dmod_cols, w, m, v, name):
    tr = 256

    def body(c_ref, dm_ref, w_ref, m_ref, v_ref, g_ref, d_ref, nm_ref, nv_ref):
        cv = c_ref[...]
        sc = cv * _sigmoid(cv)
        g = lax.dot_general(sc, dm_ref[0], (((0,), (0,)), ((), ())), precision=_HI, preferred_element_type=f32)
        g_ref[0] = g
        d_ref[0], nm_ref[0], nv_ref[0] = _adam_math(w_ref[0], g, m_ref[0], v_ref[0])

    blk = pl.BlockSpec((1, tr, ADA_COLS), lambda l, i: (l, i, 0))
    shp = jax.ShapeDtypeStruct((DEPTH, D, ADA_COLS), f32)
    return pl.pallas_call(
        body, grid=(DEPTH, D // tr),
        in_specs=[pl.BlockSpec((8, tr), lambda l, i: (0, i)), pl.BlockSpec((1, 8, ADA_COLS), lambda l, i: (l, 0, 0)), blk, blk, blk],
        out_specs=[blk, blk, blk, blk], out_shape=[shp, shp, shp, shp], name=name, compiler_params=_cp(("parallel", "parallel")),
    )(c_all, dmod_cols, w, m, v)


def _adamw(w, g, m, v, name):
    def body(w_ref, g_ref, m_ref, v_ref, d_ref, nm_ref, nv_ref):
        d_ref[...], nm_ref[...], nv_ref[...] = _adam_math(w_ref[...], g_ref[...], m_ref[...], v_ref[...])

    shp = jax.ShapeDtypeStruct(w.shape, f32)
    return pl.pallas_call(body, out_shape=[shp] * 3, name=name, compiler_params=_cp(None))(w, g, m, v)


def _tile2(r, c):
    if r <= 256 or r % 256 == 0:
        return _pick(r, (256,)), _pick(c, (1024,))
    return r, 128


def _adamw_layers(w, g_mine, g_other, m, v, c_idx, name):
    _, r, c = w.shape
    tr, tc = _tile2(r, c)

    def body(ci_ref, w_ref, gm_ref, go_ref, m_ref, v_ref, g_ref, d_ref, nm_ref, nv_ref):
        def update(g):
            g_ref[0] = g
            d_ref[0], nm_ref[0], nv_ref[0] = _adam_math(w_ref[0], g, m_ref[0], v_ref[0])

        mine = pl.program_id(0) == ci_ref[0]

        @pl.when(mine)
        def _():
            update(gm_ref[...])

        @pl.when(jnp.logical_not(mine))
        def _():
            update(go_ref[...])

    blk = pl.BlockSpec((1, tr, tc), lambda l, i, j, ci: (l, i, j))
    gblk = pl.BlockSpec((tr, tc), lambda l, i, j, ci: (i, j))
    shp = jax.ShapeDtypeStruct(w.shape, f32)
    return pl.pallas_call(
        body,
        grid_spec=pltpu.PrefetchScalarGridSpec(num_scalar_prefetch=1, grid=(DEPTH, r // tr, c // tc),
                                               in_specs=[blk, gblk, gblk, blk, blk], out_specs=[blk, blk, blk, blk]),
        out_shape=[shp, shp, shp, shp], name=name, compiler_params=_cp(("parallel", "parallel", "parallel")),
    )(c_idx, w, g_mine, g_other, m, v)


_MESH = pl.DeviceIdType.MESH
_AXES = ("x", "y", "c")
_HBM = pl.BlockSpec(memory_space=pltpu.HBM)


def _my_place():
    return tuple(lax.axis_index(a) for a in _AXES)


def _allreduce(buf, axes, name):
    r = buf.shape[0]
    n = len(axes)

    def body(x_ref, o_ref, rbuf, ssem, rsem):
        me = dict(zip(_AXES, _my_place()))
        o_ref[...] = x_ref[...]
        for k, ax in enumerate(axes):
            peer = tuple(1 - me[a] if a == ax else me[a] for a in _AXES)
            cp = pltpu.make_async_remote_copy(src_ref=o_ref, dst_ref=rbuf.at[k], send_sem=ssem.at[k], recv_sem=rsem.at[k],
                                              device_id=peer, device_id_type=_MESH)
            cp.start()
            cp.wait()
            o_ref[...] = o_ref[...] + rbuf[k]

    vm = pl.BlockSpec(memory_space=pltpu.VMEM)
    return pl.pallas_call(
        body, out_shape=jax.ShapeDtypeStruct((r, 128), f32), in_specs=[vm], out_specs=vm,
        scratch_shapes=[pltpu.VMEM((n, r, 128), f32), pltpu.SemaphoreType.DMA((n,)), pltpu.SemaphoreType.DMA((n,))],
        name=name, compiler_params=pltpu.CompilerParams(vmem_limit_bytes=VMEM_LIMIT),
    )(buf)


def _other_chips(x, y):
    return [(1 - x, y), (x, 1 - y), (1 - x, 1 - y)]


def _gather_weights(shards, name):
    na = len(shards)

    def body(*refs):
        ins, outs = refs[:na], refs[na:2 * na]
        ssem, rsem = refs[2 * na:]
        x, y, c = _my_place()
        sibling = (x, y, 1 - c)
        chips = _other_chips(x, y)

        def rcopy(a, src, j, layer, to, idx):
            return pltpu.make_async_remote_copy(src_ref=src, dst_ref=outs[a].at[j, layer], send_sem=ssem.at[idx],
                                                recv_sem=rsem.at[idx], device_id=to, device_id_type=_MESH)

        sent = []
        for j, chip in enumerate(chips):
            for a in range(na):
                cp = rcopy(a, ins[a].at[c], j, c, (*chip, c), j * na + a)
                cp.start()
                sent.append(cp)
        for j, chip in enumerate(chips):
            for a in range(na):
                rcopy(a, ins[a].at[c], j, c, (*chip, c), j * na + a).wait_recv()
                cp = rcopy(a, outs[a].at[j, c], j, c, sibling, (3 + j) * na + a)
                cp.start()
                sent.append(cp)
        for j in range(3):
            for a in range(na):
                rcopy(a, ins[a].at[c], j, 1 - c, sibling, (3 + j) * na + a).wait_recv()
        for cp in sent:
            cp.wait_send()

    out_shape = [jax.ShapeDtypeStruct((3,) + s.shape, s.dtype) for s in shards]
    return pl.pallas_call(
        body, out_shape=out_shape, in_specs=[_HBM] * na, out_specs=[_HBM] * na,
        scratch_shapes=[pltpu.SemaphoreType.DMA((6 * na,)), pltpu.SemaphoreType.DMA((6 * na,))], name=name,
    )(*shards)


def _with_own(own, others, k):
    xi, yi, _ = _my_place()
    d = k ^ (2 * xi + yi)
    j = jnp.where(d == 2, 0, jnp.where(d == 1, 1, 2))
    return jnp.where(d == 0, own, lax.dynamic_index_in_dim(others, j, axis=0, keepdims=False))


def _swap_layers(parts, name):
    na = len(parts)

    def body(*refs):
        ins, outs = refs[:2 * na], refs[2 * na:3 * na]
        ssem, rsem = refs[3 * na:]
        x, y, c = _my_place()

        def copy(a, layer):
            return pltpu.make_async_remote_copy(src_ref=ins[2 * a + layer], dst_ref=outs[a], send_sem=ssem.at[a], recv_sem=rsem.at[a],
                                                device_id=(x, y, 1 - c), device_id_type=_MESH)

        for layer in range(DEPTH):
            @pl.when(c == 1 - layer)
            def _():
                for a in range(na):
                    copy(a, layer).start()
        for a in range(na):
            copy(a, 0).wait()

    flat = [p for pair in parts for p in pair]
    return pl.pallas_call(
        body, out_shape=[jax.ShapeDtypeStruct(p0.shape, p0.dtype) for p0, _ in parts], in_specs=[_HBM] * (2 * na),
        out_specs=[_HBM] * na, scratch_shapes=[pltpu.SemaphoreType.DMA((na,)), pltpu.SemaphoreType.DMA((na,))], name=name,
    )(*flat)


def _scatter_shards(sums, name):
    na = len(sums)

    def body(*refs):
        ins, outs = refs[:na], refs[na:2 * na]
        ssem, rsem = refs[2 * na:]
        x, y, c = _my_place()
        cps = []
        for j, chip in enumerate(_other_chips(x, y)):
            kj = 2 * chip[0] + chip[1]
            for a in range(na):
                cps.append(pltpu.make_async_remote_copy(
                    src_ref=ins[a].at[kj], dst_ref=outs[a].at[j], send_sem=ssem.at[j * na + a], recv_sem=rsem.at[j * na + a],
                    device_id=(*chip, c), device_id_type=_MESH))
        for cp in cps:
            cp.start()
        for cp in cps:
            cp.wait()

    return pl.pallas_call(
        body, out_shape=[jax.ShapeDtypeStruct((3,) + p.shape[1:], p.dtype) for p in sums], in_specs=[_HBM] * na,
        out_specs=[_HBM] * na, scratch_shapes=[pltpu.SemaphoreType.DMA((3 * na,)), pltpu.SemaphoreType.DMA((3 * na,))], name=name,
    )(*sums)


def _share_layers(finals, name):
    na = len(finals)

    def body(*refs):
        ins, outs = refs[:na], refs[na:2 * na]
        ssem, rsem = refs[2 * na:]
        x, y, c = _my_place()
        cps = [pltpu.make_async_remote_copy(src_ref=ins[a], dst_ref=outs[a], send_sem=ssem.at[a], recv_sem=rsem.at[a],
                                            device_id=(x, y, 1 - c), device_id_type=_MESH) for a in range(na)]
        for cp in cps:
            cp.start()
        for cp in cps:
            cp.wait()

    return pl.pallas_call(
        body, out_shape=[jax.ShapeDtypeStruct(p.shape, p.dtype) for p in finals], in_specs=[_HBM] * na, out_specs=[_HBM] * na,
        scratch_shapes=[pltpu.SemaphoreType.DMA((na,)), pltpu.SemaphoreType.DMA((na,))], name=name,
    )(*finals)


def _add_own_layer(part0, part1, recv, c_idx, name):
    ns, r, c = recv.shape
    tr, tc = _tile2(r, c)

    def body(ci_ref, p0_ref, p1_ref, r_ref, o_ref, ob_ref):
        def add(p_ref):
            t = p_ref[...] + r_ref[...]
            o_ref[...] = t
            ob_ref[...] = t.astype(bf16)

        @pl.when(ci_ref[0] == 0)
        def _():
            add(p0_ref)

        @pl.when(ci_ref[0] == 1)
        def _():
            add(p1_ref)

    blk = pl.BlockSpec((1, tr, tc), lambda k, i, j, ci: (k, i, j))
    blk0 = pl.BlockSpec((1, tr, tc), lambda k, i, j, ci: (k * (1 - ci[0]), i * (1 - ci[0]), j * (1 - ci[0])))
    blk1 = pl.BlockSpec((1, tr, tc), lambda k, i, j, ci: (k * ci[0], i * ci[0], j * ci[0]))
    return pl.pallas_call(
        body,
        grid_spec=pltpu.PrefetchScalarGridSpec(num_scalar_prefetch=1, grid=(ns, r // tr, c // tc), in_specs=[blk0, blk1, blk],
                                               out_specs=[blk, blk]),
        out_shape=[jax.ShapeDtypeStruct((ns, r, c), f32), jax.ShapeDtypeStruct((ns, r, c), bf16)], name=name,
        compiler_params=_cp(("arbitrary", "arbitrary", "arbitrary")),
    )(c_idx, part0, part1, recv)


def _add_own_shard(sums, recv, k_idx, name):
    _, r, c = sums.shape
    tr, tc = _tile2(r, c)

    def body(ki_ref, s_ref, r_ref, o_ref):
        o_ref[...] = ((s_ref[0] + r_ref[0].astype(f32)) + r_ref[1].astype(f32)) + r_ref[2].astype(f32)

    return pl.pallas_call(
        body,
        grid_spec=pltpu.PrefetchScalarGridSpec(
            num_scalar_prefetch=1, grid=(r // tr, c // tc),
            in_specs=[pl.BlockSpec((1, tr, tc), lambda i, j, ki: (ki[0], i, j)), pl.BlockSpec((3, tr, tc), lambda i, j, ki: (0, i, j))],
            out_specs=pl.BlockSpec((tr, tc), lambda i, j, ki: (i, j))),
        out_shape=jax.ShapeDtypeStruct((r, c), f32), name=name, compiler_params=_cp(("parallel", "parallel")),
    )(k_idx, sums, recv)


def _reduce_scatter(parts, tag):
    x, y, c = _my_place()
    c_idx = jnp.reshape(c, (1,)).astype(jnp.int32)
    k_idx = jnp.reshape(2 * x + y, (1,)).astype(jnp.int32)
    na = len(parts)
    recv = _swap_layers(parts, f"rs_swap{tag}")
    sums = [_add_own_layer(parts[a][0], parts[a][1], recv[a], c_idx, f"rs_add_layer{tag}_{a}") for a in range(na)]
    recv = _scatter_shards([sb for _, sb in sums], f"rs_scatter{tag}")
    finals = [_add_own_shard(sums[a][0], recv[a], k_idx, f"rs_add_shard{tag}_{a}") for a in range(na)]
    return finals, _share_layers(finals, f"rs_share{tag}"), c_idx


_SMALL = [("ada_b", (DEPTH, 3 * D)), ("norm_w", (DEPTH, D)), ("gm_ln_w", (DEPTH, D)), ("gm_ln_b", (DEPTH, D)),
          ("gm_ws", (DEPTH, NG, CH, CH)), ("gm_bs", (DEPTH, NG, CH)), ("conv_b", (DEPTH, CONVD)), ("dt_bias", (DEPTH, NH)),
          ("a_log", (DEPTH, NH)), ("d_skip", (DEPTH, NH)), ("ssm_norm_w", (DEPTH, DIN)), ("final_norm_w", (D,))]


def _rows_of(shape):
    n = 1
    for d in shape:
        n *= d
    return -(-n // 1024) * 8


def _pack(arrays):
    rows = []
    for a in arrays:
        flat = a.reshape(-1)
        r = _rows_of(a.shape)
        rows.append(jnp.pad(flat, (0, r * 128 - flat.shape[0])).reshape(r, 128))
    return jnp.concatenate(rows, axis=0)


def _unpack(buf, shapes):
    out, at = [], 0
    for shp in shapes:
        r = _rows_of(shp)
        n = 1
        for d in shp:
            n *= d
        out.append(buf[at:at + r].reshape(-1)[:n].reshape(shp))
        at += r
    return out


def kernel(x, c, ada_w, ada_b, norm_w, w_in, gm_ln_w, gm_ln_b, gm_ws, gm_bs, conv_w, conv_b, dt_bias, a_log, d_skip, ssm_norm_w, w_proj_a, w_proj_b, w_out, final_norm_w, loss_target, m_ada_w, m_ada_b, m_norm_w, m_w_in, m_gm_ln_w, m_gm_ln_b, m_gm_ws, m_gm_bs, m_conv_w, m_conv_b, m_dt_bias, m_a_log, m_d_skip, m_ssm_norm_w, m_w_proj_a, m_w_proj_b, m_w_out, m_final_norm_w, v_ada_w, v_ada_b, v_norm_w, v_w_in, v_gm_ln_w, v_gm_ln_b, v_gm_ws, v_gm_bs, v_conv_w, v_conv_b, v_dt_bias, v_a_log, v_d_skip, v_ssm_norm_w, v_w_proj_a, v_w_proj_b, v_w_out, v_final_norm_w):
    xi, yi, ci = _my_place()
    k_me = 2 * xi + yi
    b_me = 4 * xi + 2 * yi + ci
    w = dict(ada_b=ada_b, norm_w=norm_w, gm_ln_w=gm_ln_w, gm_ln_b=gm_ln_b, gm_ws=gm_ws, gm_bs=gm_bs, conv_b=conv_b, dt_bias=dt_bias,
             a_log=a_log, d_skip=d_skip, ssm_norm_w=ssm_norm_w, final_norm_w=final_norm_w)
    m = dict(ada_b=m_ada_b, norm_w=m_norm_w, gm_ln_w=m_gm_ln_w, gm_ln_b=m_gm_ln_b, gm_ws=m_gm_ws, gm_bs=m_gm_bs, conv_b=m_conv_b,
             dt_bias=m_dt_bias, a_log=m_a_log, d_skip=m_d_skip, ssm_norm_w=m_ssm_norm_w, final_norm_w=m_final_norm_w)
    v = dict(ada_b=v_ada_b, norm_w=v_norm_w, gm_ln_w=v_gm_ln_w, gm_ln_b=v_gm_ln_b, gm_ws=v_gm_ws, gm_bs=v_gm_bs, conv_b=v_conv_b,
             dt_bias=v_dt_bias, a_log=v_a_log, d_skip=v_d_skip, ssm_norm_w=v_ssm_norm_w, final_norm_w=v_final_norm_w)

    c_slot = lax.dynamic_update_slice(jnp.zeros((8, D), f32), c, (b_me, 0))
    c_all = _allreduce(c_slot.reshape(64, 128), _AXES, "gather_c").reshape(8, D)
    ada_b_cols = lax.dynamic_slice(ada_b, (0, k_me * ADA_COLS), (DEPTH, ADA_COLS)).reshape(DEPTH, 1, ADA_COLS)
    mod_cols = _ada_fwd(c_all, ada_w, ada_b_cols, "ada_fwd")
    mod_slot = lax.dynamic_update_slice(jnp.zeros((DEPTH, 8, 3 * D), f32), mod_cols, (0, 0, k_me * ADA_COLS))
    mod_all = _allreduce(mod_slot.reshape(-1, 128), ("x", "y"), "gather_mod").reshape(DEPTH, 8, 3 * D)
    mod_me = lax.dynamic_slice(mod_all, (0, b_me, 0), (DEPTH, 1, 3 * D))
    mods = [(mod_me[l, :, 0:D], mod_me[l, :, D:2 * D], mod_me[l, :, 2 * D:3 * D]) for l in range(DEPTH)]

    rows_sh = jnp.concatenate([w_proj_a, w_proj_b, w_out], axis=1).astype(bf16)
    win_sh = w_in.astype(bf16)
    win_oth, rows_oth, conv_oth = _gather_weights([win_sh, rows_sh, conv_w], "gather_weights")
    ops = []
    for l in range(DEPTH):
        rows_l = [_with_own(rows_sh[l], rows_oth[:, l], k) for k in range(NSHARD)]
        w_in_l = jnp.concatenate([_with_own(win_sh[l], win_oth[:, l], k) for k in range(NSHARD)], axis=1)
        wa_l = jnp.concatenate([t[0:256] for t in rows_l], axis=0)
        wb_l = jnp.concatenate([t[256:768] for t in rows_l], axis=0)
        wo_l = jnp.concatenate([t[768:1024] for t in rows_l], axis=0)
        cw_l = jnp.concatenate([_with_own(conv_w[l], conv_oth[:, l], k) for k in range(NSHARD)], axis=1)
        ops.append(_layer_operands(w_in_l, cw_l, wa_l, wb_l, wo_l, norm_w[l], gm_ln_w[l], gm_ln_b[l], gm_ws[l], gm_bs[l], conv_b[l],
                                   dt_bias[l], a_log[l], d_skip[l], ssm_norm_w[l]))

    loss_parts, dx, grads, dfnw = _local_step(x[0], loss_target[0], mods, ops, final_norm_w)

    s_in = N_IN // NSHARD
    tr_ = lambda t: jnp.swapaxes(t, 1, 2)
    g_in = [grads[l]["w_in_t"].reshape(NSHARD, s_in, D) for l in range(DEPTH)]
    g_rows = [jnp.concatenate([grads[l]["w_proj_a"].reshape(NSHARD, 256, D), grads[l]["w_proj_b"].reshape(NSHARD, 512, D),
                               grads[l]["w_out"].reshape(NSHARD, 256, D)], axis=1) for l in range(DEPTH)]
    g_conv = [grads[l]["conv_w"].reshape(4, NSHARD, D).transpose(1, 0, 2) for l in range(DEPTH)]
    (f_in, f_rows, f_conv), (o_in, o_rows, o_conv), c_idx = _reduce_scatter([g_in, g_rows, g_conv], "")
    gr_in, d_in, nm_in, nv_in = [tr_(t) for t in _adamw_layers(tr_(w_in), f_in, o_in, tr_(m_w_in), tr_(v_w_in), c_idx, "adamw_w_in")]
    cat = lambda a, b, c_: jnp.concatenate([a, b, c_], axis=1)
    gr_rows, d_rows, nm_rows, nv_rows = _adamw_layers(cat(w_proj_a, w_proj_b, w_out), f_rows, o_rows, cat(m_w_proj_a, m_w_proj_b, m_w_out),
                                                      cat(v_w_proj_a, v_w_proj_b, v_w_out), c_idx, "adamw_rows")
    gr_conv = jnp.where(ci == 0, jnp.stack([f_conv, o_conv]), jnp.stack([o_conv, f_conv]))
    split = lambda t: (t[:, 0:256], t[:, 256:768], t[:, 768:1024])

    dmod_slot = lax.dynamic_update_slice(jnp.zeros((DEPTH, 8, 3 * D), f32),
                                         jnp.stack([grads[l]["mod"] for l in range(DEPTH)]).reshape(DEPTH, 1, 3 * D), (0, b_me, 0))
    small_g = {n: (dfnw if n == "final_norm_w" else jnp.stack([grads[l]["mod" if n == "ada_b" else n] for l in range(DEPTH)]))
               for n, _ in _SMALL}
    packed = _allreduce(_pack([small_g[n] for n, _ in _SMALL] + [dmod_slot, loss_parts]), _AXES, "allreduce_small")
    n_small = sum(_rows_of(s) for _, s in _SMALL)
    n_dmod = _rows_of(dmod_slot.shape)
    g_small = packed[0:n_small]
    dmod_all = packed[n_small:n_small + n_dmod].reshape(DEPTH, 8, 3 * D)
    loss = jnp.sum(packed[n_small + n_dmod:])
    small_gw = jnp.concatenate([g_small, _pack([gr_conv])], axis=0)
    d_s, nm_s, nv_s = _adamw(_pack([w[n] for n, _ in _SMALL] + [conv_w]), small_gw,
                             _pack([m[n] for n, _ in _SMALL] + [m_conv_w]), _pack([v[n] for n, _ in _SMALL] + [v_conv_w]), "adamw_small")
    shapes = [s for _, s in _SMALL] + [conv_w.shape]
    names = [n for n, _ in _SMALL] + ["conv_w"]
    g_d = dict(zip(names, _unpack(small_gw, shapes)))
    d_d = dict(zip(names, _unpack(d_s, shapes)))
    nm_d = dict(zip(names, _unpack(nm_s, shapes)))
    nv_d = dict(zip(names, _unpack(nv_s, shapes)))

    dmod_cols = lax.dynamic_slice(dmod_all, (0, 0, k_me * ADA_COLS), (DEPTH, 8, ADA_COLS))
    g_ada, d_ada, nm_ada, nv_ada = _ada_bwd_adamw(c_all, dmod_cols, ada_w, m_ada_w, v_ada_w, "ada_bwd_adamw")

    def by_name(big, small):
        ga, gb, go = split(big[1])
        return dict(small, ada_w=big[2], w_in=big[0], w_proj_a=ga, w_proj_b=gb, w_out=go)

    order = ["ada_w", "ada_b", "norm_w", "w_in", "gm_ln_w", "gm_ln_b", "gm_ws", "gm_bs", "conv_w", "conv_b", "dt_bias", "a_log",
             "d_skip", "ssm_norm_w", "w_proj_a", "w_proj_b", "w_out", "final_norm_w"]
    outs = []
    for big, small in (((gr_in, gr_rows, g_ada), g_d), ((d_in, d_rows, d_ada), d_d), ((nm_in, nm_rows, nm_ada), nm_d),
                       ((nv_in, nv_rows, nv_ada), nv_d)):
        t = by_name(big, small)
        outs += [t[n] for n in order]
    return (loss, dx.reshape(1, -1, D), *outs)
```

```python
import jax
import jax.numpy as jnp
from jax import lax
from jax.experimental import pallas as pl
from jax.experimental.pallas import tpu as pltpu
from jax.experimental.pallas import tpu_sc as plsc

f32 = jnp.float32
bf16 = jnp.bfloat16

D = 1024
DEPTH = 2
EPS = 1e-6
CH = 128
NG = 8
HPG = 4
HD = 64
NH = NG * HPG
NST = 128
DIN = 2048
CONVD = 4096
GW = DIN // NG
PB = CONVD + DIN + 256
PA = 3 * D
PG = 2 * D
N_IN = 11296
NSHARD = 4
V7X_VMEM_BYTES = 64 * 2 ** 20
VMEM_LIMIT = V7X_VMEM_BYTES - 8 * 2 ** 20

ADAM_LR, ADAM_B1, ADAM_B2, ADAM_EPS, ADAM_WD, ADAM_STEP = 0.001, 0.9, 0.999, 1e-08, 0.01, 10

_HI = lax.Precision.HIGHEST


def _cp(sem):
    return pltpu.CompilerParams(dimension_semantics=sem, vmem_limit_bytes=VMEM_LIMIT)


def _sigmoid(x):
    return 0.5 * jnp.tanh(0.5 * x) + 0.5


def _silu_and_grad(x):
    s = _sigmoid(x)
    return x * s, s * (1.0 + x * (1.0 - s))


_GELU_K = 0.7978845608028654
_GELU_C = 0.044715


def _gelu_and_grad(x):
    x2 = x * x
    t = jnp.tanh(_GELU_K * (x + _GELU_C * x * x2))
    g = 0.5 * x * (1.0 + t)
    dg = 0.5 * (1.0 + t) + 0.5 * x * (1.0 - t * t) * _GELU_K * (1.0 + 3.0 * _GELU_C * x2)
    return g, dg


def _gelu(x):
    t = jnp.tanh(_GELU_K * (x + _GELU_C * x * x * x))
    return 0.5 * x * (1.0 + t)


def _softplus(x):
    return jnp.maximum(x, 0.0) + jnp.log(1.0 + jnp.exp(-jnp.abs(x)))


def _dot(a, b):
    return jnp.dot(a, b, preferred_element_type=f32)


def _dot_nt(a, b):
    return lax.dot_general(a, b, (((1,), (1,)), ((), ())), preferred_element_type=f32)


def _dot_tn(a, b):
    return lax.dot_general(a, b, (((0,), (0,)), ((), ())), preferred_element_type=f32)


def _dot_hi(a, b):
    return jnp.dot(a, b, precision=_HI, preferred_element_type=f32)


def _rmsmod_fwd(x, nw, scale, shift, name):
    s = x.shape[0]
    ts = min(512, s)

    def body(x_ref, nw_ref, sc_ref, sh_ref, h_ref):
        xv = x_ref[...]
        r = lax.rsqrt(jnp.mean(xv * xv, axis=-1, keepdims=True) + EPS)
        h_ref[...] = ((xv * r) * nw_ref[...] * (1.0 + sc_ref[...]) + sh_ref[...]).astype(bf16)

    row = pl.BlockSpec((1, D), lambda i: (0, 0))
    tile = pl.BlockSpec((ts, D), lambda i: (i, 0))
    return pl.pallas_call(
        body, grid=(s // ts,), in_specs=[tile, row, row, row], out_specs=tile,
        out_shape=jax.ShapeDtypeStruct((s, D), bf16), name=name, compiler_params=_cp(("parallel",)),
    )(x, nw, scale, shift)


def _rmsmod_bwd(dh, x, dres, nw, scale, name):
    s = x.shape[0]
    ts = min(512, s)

    def body(dh_ref, x_ref, dres_ref, nw_ref, sc_ref, dx_ref, dsc_ref, dsh_ref, dnw_ref):
        @pl.when(pl.program_id(0) == 0)
        def _():
            dsc_ref[...] = jnp.zeros_like(dsc_ref)
            dsh_ref[...] = jnp.zeros_like(dsh_ref)
            dnw_ref[...] = jnp.zeros_like(dnw_ref)
        xv = x_ref[...]
        dhv = dh_ref[...]
        r = lax.rsqrt(jnp.mean(xv * xv, axis=-1, keepdims=True) + EPS)
        xn = xv * r
        one_sc = 1.0 + sc_ref[...]
        dsc_ref[...] += jnp.sum(dhv * xn * nw_ref[...], axis=0, keepdims=True)
        dsh_ref[...] += jnp.sum(dhv, axis=0, keepdims=True)
        dnw_ref[...] += jnp.sum(dhv * xn * one_sc, axis=0, keepdims=True)
        dxn = dhv * (nw_ref[...] * one_sc)
        dx_ref[...] = r * (dxn - xn * jnp.mean(dxn * xn, axis=-1, keepdims=True)) + dres_ref[...]

    row = pl.BlockSpec((1, D), lambda i: (0, 0))
    tile = pl.BlockSpec((ts, D), lambda i: (i, 0))
    vec = jax.ShapeDtypeStruct((1, D), f32)
    return pl.pallas_call(
        body, grid=(s // ts,), in_specs=[tile, tile, tile, row, row], out_specs=[tile, row, row, row],
        out_shape=[jax.ShapeDtypeStruct((s, D), f32), vec, vec, vec], name=name, compiler_params=_cp(("arbitrary",)),
    )(dh, x, dres, nw, scale)


def _pick(n, prefs):
    for p in prefs:
        if n % p == 0:
            return p
    return n


def _mm(a, b, out_dtype, name, c_in=None):
    m, k = a.shape
    n = b.shape[1]
    tm = _pick(m, (1024, 512, 256))
    tn = _pick(n, (1280, 1024, 512))
    tk = _pick(k, (1280, 1024, 512))
    nk = k // tk

    def body(*refs):
        if c_in is not None:
            a_ref, b_ref, c_ref, o_ref, acc = refs
        else:
            a_ref, b_ref, o_ref, acc = refs
        kk = pl.program_id(2)

        @pl.when(kk == 0)
        def _():
            if c_in is not None:
                acc[...] = c_ref[...]
            else:
                acc[...] = jnp.zeros_like(acc)
        acc[...] += _dot(a_ref[...], b_ref[...])

        @pl.when(kk == nk - 1)
        def _():
            o_ref[...] = acc[...].astype(out_dtype)

    in_specs = [pl.BlockSpec((tm, tk), lambda j, i, kk: (i, kk)), pl.BlockSpec((tk, tn), lambda j, i, kk: (kk, j))]
    args = [a, b]
    if c_in is not None:
        in_specs.append(pl.BlockSpec((tm, tn), lambda j, i, kk: (i, j)))
        args.append(c_in)
    return pl.pallas_call(
        body, grid=(n // tn, m // tm, nk), in_specs=in_specs, out_specs=pl.BlockSpec((tm, tn), lambda j, i, kk: (i, j)),
        out_shape=jax.ShapeDtypeStruct((m, n), out_dtype), scratch_shapes=[pltpu.VMEM((tm, tn), f32)], name=name,
        compiler_params=_cp(("parallel", "parallel", "arbitrary")),
    )(*args)


def _mm_tn(a, b, name):
    t, k1 = a.shape
    n = b.shape[1]
    t1 = _pick(k1, (1280, 1024, 512))
    tn = _pick(n, (1280, 1024, 512))
    tt = _pick(t, (2048, 1024, 512, 256))
    nt = t // tt

    def body(a_ref, b_ref, o_ref):
        tt_i = pl.program_id(2)

        @pl.when(tt_i == 0)
        def _():
            o_ref[...] = jnp.zeros_like(o_ref)
        o_ref[...] += _dot_tn(a_ref[...], b_ref[...])

    return pl.pallas_call(
        body, grid=(k1 // t1, n // tn, nt),
        in_specs=[pl.BlockSpec((tt, t1), lambda i, j, tt_i: (tt_i, i)), pl.BlockSpec((tt, tn), lambda i, j, tt_i: (tt_i, j))],
        out_specs=pl.BlockSpec((t1, tn), lambda i, j, tt_i: (i, j)),
        out_shape=jax.ShapeDtypeStruct((k1, n), f32), name=name,
        compiler_params=_cp(("parallel", "parallel", "arbitrary")),
    )(a, b)


def _ln_stats(v):
    mu = jnp.mean(v, axis=-1, keepdims=True)
    vc = v - mu
    rstd = lax.rsqrt(jnp.mean(vc * vc, axis=-1, keepdims=True) + EPS)
    return vc * rstd, rstd


def _mix(w_ref, vl):
    return jnp.concatenate([_dot(w_ref[g], vl[:, g * CH:(g + 1) * CH]) for g in range(NG)], axis=1)


def _branch_a_fwd(proj_a, lnw, lnb, wsm, bsf, name):
    s = proj_a.shape[0]
    ta = min(256, s)

    def body(pu_ref, pv_ref, pz_ref, lnw_ref, lnb_ref, w_ref, bs_ref, ya_ref):
        for c in range(ta // CH):
            rows = pl.ds(c * CH, CH)
            vh, _ = _ln_stats(_gelu(pv_ref[rows, :]))
            vl = (vh * lnw_ref[...] + lnb_ref[...]).astype(bf16)
            mixed = _mix(w_ref, vl) + bs_ref[...]
            pz = pz_ref[rows, :]
            ya_ref[rows, :] = (_gelu(pu_ref[rows, :]) * mixed * (pz * _sigmoid(pz))).astype(bf16)

    row = pl.BlockSpec((1, D), lambda i: (0, 0))
    return pl.pallas_call(
        body, grid=(s // ta,),
        in_specs=[pl.BlockSpec((ta, D), lambda i: (i, 0)), pl.BlockSpec((ta, D), lambda i: (i, 1)),
                  pl.BlockSpec((ta, D), lambda i: (i, 2)), row, row,
                  pl.BlockSpec((NG, CH, CH), lambda i: (0, 0, 0)), pl.BlockSpec((CH, D), lambda i: (0, 0))],
        out_specs=pl.BlockSpec((ta, D), lambda i: (i, 0)),
        out_shape=jax.ShapeDtypeStruct((s, D), bf16), name=name, compiler_params=_cp(("parallel",)),
    )(proj_a, proj_a, proj_a, lnw, lnb, wsm, bsf)


def _branch_a_bwd(proj_a, dya, lnw, lnb, wsm, wsm_t, bsf, name):
    s = proj_a.shape[0]
    ta = min(256, s)

    def body(pu_ref, pv_ref, pz_ref, dya_ref, lnw_ref, lnb_ref, w_ref, wt_ref, bs_ref,
             dp_ref, dws_ref, dbs_ref, dlnw_ref, dlnb_ref):
        @pl.when(pl.program_id(0) == 0)
        def _():
            dws_ref[...] = jnp.zeros_like(dws_ref)
            dbs_ref[...] = jnp.zeros_like(dbs_ref)
            dlnw_ref[...] = jnp.zeros_like(dlnw_ref)
            dlnb_ref[...] = jnp.zeros_like(dlnb_ref)
        for c in range(ta // CH):
            rows = pl.ds(c * CH, CH)
            u, du = _gelu_and_grad(pu_ref[rows, :])
            v, dv_act = _gelu_and_grad(pv_ref[rows, :])
            zg, dzg = _silu_and_grad(pz_ref[rows, :])
            vh, rstd = _ln_stats(v)
            vl = (vh * lnw_ref[...] + lnb_ref[...]).astype(bf16)
            mixed = _mix(w_ref, vl) + bs_ref[...]
            dy = dya_ref[rows, :].astype(f32)
            dmixed = dy * u * zg
            dp_ref[rows, 0:D] = (dy * mixed * zg * du).astype(bf16)
            dp_ref[rows, 2 * D:3 * D] = (dy * u * mixed * dzg).astype(bf16)
            dmb = dmixed.astype(bf16)
            dbs_ref[...] += dmixed
            dvl = _mix(wt_ref, dmb)
            for g in range(NG):
                cols = slice(g * CH, (g + 1) * CH)
                dws_ref[g] += _dot_nt(dmb[:, cols], vl[:, cols])
            dlnw_ref[...] += jnp.sum(dvl * vh, axis=0, keepdims=True)
            dlnb_ref[...] += jnp.sum(dvl, axis=0, keepdims=True)
            dvh = dvl * lnw_ref[...]
            dv = rstd * (dvh - jnp.mean(dvh, axis=-1, keepdims=True) - vh * jnp.mean(dvh * vh, axis=-1, keepdims=True))
            dp_ref[rows, D:2 * D] = (dv * dv_act).astype(bf16)

    row = pl.BlockSpec((1, D), lambda i: (0, 0))
    wspec = pl.BlockSpec((NG, CH, CH), lambda i: (0, 0, 0))
    full = pl.BlockSpec((CH, D), lambda i: (0, 0))
    return pl.pallas_call(
        body, grid=(s // ta,),
        in_specs=[pl.BlockSpec((ta, D), lambda i: (i, 0)), pl.BlockSpec((ta, D), lambda i: (i, 1)),
                  pl.BlockSpec((ta, D), lambda i: (i, 2)), pl.BlockSpec((ta, D), lambda i: (i, 0)),
                  row, row, wspec, wspec, full],
        out_specs=[pl.BlockSpec((ta, PA), lambda i: (i, 0)), wspec, full, row, row],
        out_shape=[jax.ShapeDtypeStruct((s, PA), bf16), jax.ShapeDtypeStruct((NG, CH, CH), f32),
                   jax.ShapeDtypeStruct((CH, D), f32), jax.ShapeDtypeStruct((1, D), f32), jax.ShapeDtypeStruct((1, D), f32)],
        name=name, compiler_params=_cp(("arbitrary",)),
    )(proj_a, proj_a, proj_a, dya, lnw, lnb, wsm, wsm_t, bsf)


GB = GW + 2 * NST


def _group_major(xs, b, c):
    lead = xs.shape[:-1]
    return jnp.concatenate([xs.reshape(lead + (NG, GW)), b.reshape(lead + (NG, NST)), c.reshape(lead + (NG, NST))],
                           axis=-1).reshape(lead + (CONVD,))


def _from_group_major(t):
    lead = t.shape[:-1]
    t = t.reshape(lead + (NG, GB))
    return jnp.concatenate([t[..., 0:GW].reshape(lead + (DIN,)), t[..., GW:GW + NST].reshape(lead + (NG * NST,)),
                            t[..., GW + NST:GB].reshape(lead + (NG * NST,))], axis=-1)


def _rows_from_group_major(t):
    t = t.reshape(NG, GB, t.shape[-1])
    return jnp.concatenate([t[:, 0:GW].reshape(DIN, -1), t[:, GW:GW + NST].reshape(NG * NST, -1),
                            t[:, GW + NST:GB].reshape(NG * NST, -1)], axis=0)


def _shift_rows(x, prev8, j):
    xr = pltpu.roll(x, j, 0)
    fix = pltpu.roll(prev8, j, 0)
    rid = lax.broadcasted_iota(jnp.int32, (8, x.shape[1]), 0)
    top = jnp.where(rid < j, fix, xr[0:8])
    return jnp.concatenate([top, xr[8:]], axis=0)


def _shift_rows_up(d, next8, j):
    dr = pltpu.roll(d, CH - j, 0)
    fix = pltpu.roll(next8, 8 - j, 0)
    rid = lax.broadcasted_iota(jnp.int32, (8, d.shape[1]), 0)
    bot = jnp.where(rid >= 8 - j, fix, dr[CH - 8:CH])
    return jnp.concatenate([dr[0:CH - 8], bot], axis=0)


def _conv_pre(x, prev8, cw_ref, cb_ref, cols):
    shifted = [_shift_rows(x, prev8, j) for j in (1, 2, 3)]
    conv = cb_ref[:, cols] + cw_ref[3:4, cols] * x
    for j in (1, 2, 3):
        conv = conv + cw_ref[3 - j:4 - j, cols] * shifted[j - 1]
    return conv, shifted


def _tril_mask():
    return lax.broadcasted_iota(jnp.int32, (CH, CH), 0) >= lax.broadcasted_iota(jnp.int32, (CH, CH), 1)


def _sum_all(v):
    return jnp.sum(jnp.sum(v, axis=0, keepdims=True), axis=1, keepdims=True)


def _lanes(g, width, base=0):
    return pl.ds(pl.multiple_of(base + g * width, width), width)


def _branch_b_fwd(proj_b, cw, cb, dtb, a_row, dkc, snw, name):
    s = proj_b.shape[0]
    nc = s // CH

    def body(xbc_ref, sz_ref, dtr_ref, cw_ref, cb_ref, dtb_ref, a_ref, dkc_ref, snw_ref,
             yb_ref, y_ref, st_ref, cv_ref, prev8, state, acst_s):
        @pl.when(pl.program_id(0) == 0)
        def _():
            prev8[...] = jnp.zeros_like(prev8)
            state[...] = jnp.zeros_like(state)
        st_ref[0] = state[...].astype(bf16)
        mask = _tril_mask()
        dt_all = _softplus(dtr_ref[:, 0:CH] + dtb_ref[...])
        acs_all = _dot_hi(mask.astype(f32), dt_all * a_ref[...])
        acst_s[...] = acs_all.T

        def group(g, carry):
            cols = _lanes(g, GB)
            gcols = _lanes(g, GW)
            x = xbc_ref[:, cols]
            conv, _ = _conv_pre(x, prev8[:, cols], cw_ref, cb_ref, cols)
            prev8[:, cols] = x[CH - 8:CH]
            cv_ref[:, cols] = conv.astype(bf16)
            xc = conv * _sigmoid(conv)
            xs = xc[:, 0:GW]
            bg = xc[:, GW:GW + NST].astype(bf16)
            cg = xc[:, GW + NST:GB].astype(bf16)
            back = lax.rem(CH - HPG * g, CH)
            dt = pltpu.roll(dt_all, back, 1)
            acs = pltpu.roll(acs_all, back, 1)
            cbm = _dot_nt(cg, bg)
            dkc_g = dkc_ref[:, gcols]
            y_parts = []
            for r in range(HPG):
                colb = jnp.broadcast_to(acs[:, r:r + 1], (CH, CH))
                row = acst_s[pl.ds(g * HPG + r, 1), :]
                lmat = jnp.exp(jnp.where(mask, colb - row, -jnp.inf))
                xr = xs[:, r * HD:(r + 1) * HD]
                xd = xr * dt[:, r:r + 1]
                sp = state[g * HPG + r]
                col = colb[:, 0:HD]
                alast = colb[CH - 1:CH, 0:HD]
                y_r = _dot((cbm * lmat).astype(bf16), xd.astype(bf16))
                y_r = y_r + jnp.exp(col) * _dot_nt(cg, sp.astype(bf16))
                y_parts.append(y_r + xr * dkc_g[:, r * HD:(r + 1) * HD])
                cs = _dot_tn((xd * jnp.exp(alast - col)).astype(bf16), bg)
                state[g * HPG + r] = jnp.exp(colb[CH - 1:CH, :]) * sp + cs
            y = jnp.concatenate(y_parts, axis=1)
            szv = sz_ref[:, gcols]
            yz = y * (szv * _sigmoid(szv))
            rr = lax.rsqrt(jnp.mean(yz * yz, axis=-1, keepdims=True) + EPS)
            yb_ref[:, gcols] = (yz * rr * snw_ref[:, gcols]).astype(bf16)
            y_ref[:, gcols] = y.astype(bf16)
            return carry

        lax.fori_loop(0, NG, group, 0)

    const2 = lambda c: (0, 0)
    return pl.pallas_call(
        body, grid=(nc,),
        in_specs=[pl.BlockSpec((CH, CONVD), lambda c: (c, 0)), pl.BlockSpec((CH, DIN), lambda c: (c, CONVD // DIN)),
                  pl.BlockSpec((CH, 256), lambda c: (c, (CONVD + DIN) // 256)),
                  pl.BlockSpec((4, CONVD), const2), pl.BlockSpec((1, CONVD), const2),
                  pl.BlockSpec((1, CH), const2), pl.BlockSpec((1, CH), const2),
                  pl.BlockSpec((1, DIN), const2), pl.BlockSpec((1, DIN), const2)],
        out_specs=[pl.BlockSpec((CH, DIN), lambda c: (c, 0)), pl.BlockSpec((CH, DIN), lambda c: (c, 0)),
                   pl.BlockSpec((1, NH, HD, NST), lambda c: (c, 0, 0, 0)), pl.BlockSpec((CH, CONVD), lambda c: (c, 0))],
        out_shape=[jax.ShapeDtypeStruct((s, DIN), bf16), jax.ShapeDtypeStruct((s, DIN), bf16),
                   jax.ShapeDtypeStruct((nc, NH, HD, NST), bf16), jax.ShapeDtypeStruct((s, CONVD), bf16)],
        scratch_shapes=[pltpu.VMEM((8, CONVD), f32), pltpu.VMEM((NH, HD, NST), f32), pltpu.VMEM((CH, CH), f32)],
        name=name, compiler_params=_cp(("arbitrary",)),
    )(proj_b, proj_b, proj_b, cw, cb, dtb, a_row, dkc, snw)


def _branch_b_bwd(proj_b, conv_sv, dyb, y_sv, states, cw, dtb, a_row, dkc, snw, ind, name):
    s = proj_b.shape[0]
    nc = s // CH

    def body(xbc_ref, cv_ref, sz_ref, dtr_ref, dyb_ref, y_ref, st_ref, cw_ref, dtb_ref, a_ref, dkc_ref,
             snw_ref, ind_ref, dp_ref, dcw_ref, dcb_ref, ddtb_ref, dal_ref, ddk_ref, dsnw_ref,
             dstate, dnext8, acst_s, dacs_acc, q2_acc):
        @pl.when(pl.program_id(0) == 0)
        def _():
            dstate[...] = jnp.zeros_like(dstate)
            dnext8[...] = jnp.zeros_like(dnext8)
            dcw_ref[...] = jnp.zeros_like(dcw_ref)
            dcb_ref[...] = jnp.zeros_like(dcb_ref)
            ddtb_ref[...] = jnp.zeros_like(ddtb_ref)
            dal_ref[...] = jnp.zeros_like(dal_ref)
            ddk_ref[...] = jnp.zeros_like(ddk_ref)
            dsnw_ref[...] = jnp.zeros_like(dsnw_ref)

        dacs_acc[...] = jnp.zeros_like(dacs_acc)
        q2_acc[...] = jnp.zeros_like(q2_acc)
        mask = _tril_mask()
        tri_t = (lax.broadcasted_iota(jnp.int32, (CH, CH), 0) <= lax.broadcasted_iota(jnp.int32, (CH, CH), 1)).astype(f32)
        lane1 = lax.broadcasted_iota(jnp.int32, (1, CH), 1)
        is_last = lax.broadcasted_iota(jnp.int32, (CH, 1), 0) == CH - 1
        z_all = dtr_ref[:, 0:CH] + dtb_ref[...]
        dt_all = _softplus(z_all)
        adt_all = dt_all * a_ref[...]
        acs_all = _dot_hi(mask.astype(f32), adt_all)
        acst_s[...] = acs_all.T

        def ind_sum(v):
            hi = v.astype(bf16)
            lo = (v - hi.astype(f32)).astype(bf16)
            return _dot(hi, ind_ref[...]) + _dot(lo, ind_ref[...])

        def group(g, carry):
            cols = _lanes(g, GB)
            gcols = _lanes(g, GW)
            conv = cv_ref[:, cols].astype(f32)
            sg = _sigmoid(conv)
            xc = conv * sg
            xs = xc[:, 0:GW]
            bg = xc[:, GW:GW + NST].astype(bf16)
            cg = xc[:, GW + NST:GB].astype(bf16)

            y = y_ref[:, gcols].astype(f32)
            silu_sz, dsilu_sz = _silu_and_grad(sz_ref[:, gcols])
            yz = y * silu_sz
            rr = lax.rsqrt(jnp.mean(yz * yz, axis=-1, keepdims=True) + EPS)
            dyb_g = dyb_ref[:, gcols].astype(f32)
            w = dyb_g * snw_ref[:, gcols]
            dsnw_ref[:, gcols] += jnp.sum(dyb_g * yz * rr, axis=0, keepdims=True)
            dyz = rr * w - yz * (rr * rr * rr) * jnp.mean(w * yz, axis=-1, keepdims=True)
            dp_ref[:, _lanes(g, GW, CONVD)] = (dyz * y * dsilu_sz).astype(bf16)
            dy_g = dyz * silu_sz
            ddk_ref[:, gcols] += jnp.sum(dy_g * xs, axis=0, keepdims=True)

            back = lax.rem(CH - HPG * g, CH)
            dt = pltpu.roll(dt_all, back, 1)
            acs = pltpu.roll(acs_all, back, 1)
            cbm = _dot_nt(cg, bg)
            d_cb = jnp.zeros((CH, CH), f32)
            d_bg = jnp.zeros((CH, NST), f32)
            d_cg = jnp.zeros((CH, NST), f32)
            lastrow = jnp.zeros((1, CH), f32)
            dxd_parts, dxs_parts, t_parts = [], [], []
            for r in range(HPG):
                h = g * HPG + r
                colb = jnp.broadcast_to(acs[:, r:r + 1], (CH, CH))
                row = acst_s[pl.ds(h, 1), :]
                lmat = jnp.exp(jnp.where(mask, colb - row, -jnp.inf))
                mmat_b = (cbm * lmat).astype(bf16)
                dtc = dt[:, r:r + 1]
                xd = xs[:, r * HD:(r + 1) * HD] * dtc
                col = colb[:, 0:HD]
                alast = colb[CH - 1:CH, 0:HD]
                dte = jnp.exp(alast - col)
                ea = jnp.exp(col)
                cd = jnp.exp(colb[CH - 1:CH, :])
                sp = st_ref[0, h]
                dsn = dstate[h]
                dsn_b = dsn.astype(bf16)
                dyr = dy_g[:, r * HD:(r + 1) * HD]
                dyr_b = dyr.astype(bf16)
                dye_b = (dyr * ea).astype(bf16)
                d_cg = d_cg + _dot(dye_b, sp)
                dxde = _dot_nt(bg, dsn_b)
                xdte = xd * dte
                xd_b = xd.astype(bf16)
                d_bg = d_bg + _dot(xdte.astype(bf16), dsn_b)
                dxd_diag = _dot_tn(mmat_b, dyr_b)
                dxd = dxde * dte + dxd_diag
                d_cb = d_cb + _dot_nt(dyr_b, xd_b) * lmat
                t_parts.append(dyr_b.astype(f32) * _dot(mmat_b, xd_b) + dyr * (ea * _dot_nt(cg, sp))
                               - xd_b.astype(f32) * dxd_diag - dxde * xdte)
                lastrow = jnp.where(lane1 == r, _sum_all(dsn * sp.astype(f32)) * cd + _sum_all(dxde * xdte), lastrow)
                dstate[h] = cd * dsn + _dot_tn(dye_b, cg)
                dxd_parts.append(dxd)
                dxs_parts.append(dxd * dtc)
            d_cb_b = d_cb.astype(bf16)
            d_bg = d_bg + _dot_tn(d_cb_b, cg)
            d_cg = d_cg + _dot(d_cb_b, bg)
            q2 = ind_sum(jnp.concatenate(dxd_parts, axis=1) * xs)
            dacs = ind_sum(jnp.concatenate(t_parts, axis=1)) + jnp.where(is_last, lastrow, 0.0)
            dacs_acc[...] += pltpu.roll(dacs, HPG * g, 1)
            q2_acc[...] += pltpu.roll(q2, HPG * g, 1)
            dxs = jnp.concatenate(dxs_parts, axis=1) + dy_g * dkc_ref[:, gcols]

            dconv = jnp.concatenate([dxs, d_bg, d_cg], axis=1) * (sg * (1.0 + conv * (1.0 - sg)))
            x = xbc_ref[:, cols]
            dcb_ref[:, cols] += jnp.sum(dconv, axis=0, keepdims=True)
            dcw_ref[3:4, cols] += jnp.sum(dconv * x, axis=0, keepdims=True)
            dx = cw_ref[3:4, cols] * dconv
            nxt = dnext8[:, cols]
            for j in (1, 2, 3):
                up = _shift_rows_up(dconv, nxt, j)
                dcw_ref[3 - j:4 - j, cols] += jnp.sum(up * x, axis=0, keepdims=True)
                dx = dx + cw_ref[3 - j:4 - j, cols] * up
            dnext8[:, cols] = dconv[0:8]
            dp_ref[:, cols] = dx.astype(bf16)
            return carry

        lax.fori_loop(0, NG, group, 0)

        dadt = _dot_hi(tri_t, dacs_acc[...])
        dal_ref[...] += jnp.sum(dadt * adt_all, axis=0, keepdims=True)
        ddz = (dadt * a_ref[...] + q2_acc[...]) * _sigmoid(z_all)
        ddtb_ref[...] += jnp.sum(ddz, axis=0, keepdims=True)
        dp_ref[:, CONVD + DIN:CONVD + DIN + CH] = ddz.astype(bf16)
        dp_ref[:, CONVD + DIN + CH:PB] = jnp.zeros((CH, PB - CONVD - DIN - CH), bf16)

    const2 = lambda c: (0, 0)
    rev = lambda c: (nc - 1 - c, 0)
    return pl.pallas_call(
        body, grid=(nc,),
        in_specs=[pl.BlockSpec((CH, CONVD), rev), pl.BlockSpec((CH, CONVD), rev),
                  pl.BlockSpec((CH, DIN), lambda c: (nc - 1 - c, CONVD // DIN)),
                  pl.BlockSpec((CH, 256), lambda c: (nc - 1 - c, (CONVD + DIN) // 256)),
                  pl.BlockSpec((CH, DIN), rev), pl.BlockSpec((CH, DIN), rev),
                  pl.BlockSpec((1, NH, HD, NST), lambda c: (nc - 1 - c, 0, 0, 0)),
                  pl.BlockSpec((4, CONVD), const2), pl.BlockSpec((1, CH), const2), pl.BlockSpec((1, CH), const2),
                  pl.BlockSpec((1, DIN), const2), pl.BlockSpec((1, DIN), const2), pl.BlockSpec((GW, CH), const2)],
        out_specs=[pl.BlockSpec((CH, PB), rev), pl.BlockSpec((4, CONVD), const2), pl.BlockSpec((1, CONVD), const2),
                   pl.BlockSpec((1, CH), const2), pl.BlockSpec((1, CH), const2), pl.BlockSpec((1, DIN), const2),
                   pl.BlockSpec((1, DIN), const2)],
        out_shape=[jax.ShapeDtypeStruct((s, PB), bf16), jax.ShapeDtypeStruct((4, CONVD), f32),
                   jax.ShapeDtypeStruct((1, CONVD), f32), jax.ShapeDtypeStruct((1, CH), f32),
                   jax.ShapeDtypeStruct((1, CH), f32), jax.ShapeDtypeStruct((1, DIN), f32),
                   jax.ShapeDtypeStruct((1, DIN), f32)],
        scratch_shapes=[pltpu.VMEM((NH, HD, NST), f32), pltpu.VMEM((8, CONVD), f32), pltpu.VMEM((CH, CH), f32),
                        pltpu.VMEM((CH, CH), f32), pltpu.VMEM((CH, CH), f32)],
        name=name, compiler_params=_cp(("arbitrary",)),
    )(proj_b, conv_sv, proj_b, proj_b, dyb, y_sv, states, cw, dtb, a_row, dkc, snw, ind)


def _merge_fwd(ya, yb, proj_g, x, gate, wa, wb, wo, name):
    s = x.shape[0]
    ts = min(512, s)

    def body(ya_ref, yb_ref, ga_ref, gb_ref, x_ref, gate_ref, wa_ref, wb_ref, wo_ref, xo_ref, pa_ref, pb_ref, mg_ref, o_ref):
        pa = _dot(ya_ref[...], wa_ref[...])
        pb = _dot(yb_ref[...], wb_ref[...])
        mg = (_sigmoid(ga_ref[...]) * pa + _sigmoid(gb_ref[...]) * pb).astype(bf16)
        o = _dot(mg, wo_ref[...])
        xo_ref[...] = x_ref[...] + gate_ref[...] * o
        pa_ref[...] = pa.astype(bf16)
        pb_ref[...] = pb.astype(bf16)
        mg_ref[...] = mg
        o_ref[...] = o.astype(bf16)

    tile = pl.BlockSpec((ts, D), lambda i: (i, 0))
    const = lambda i: (0, 0)
    act = jax.ShapeDtypeStruct((s, D), bf16)
    return pl.pallas_call(
        body, grid=(s // ts,),
        in_specs=[tile, pl.BlockSpec((ts, DIN), lambda i: (i, 0)), tile, pl.BlockSpec((ts, D), lambda i: (i, 1)), tile,
                  pl.BlockSpec((1, D), const), pl.BlockSpec((D, D), const), pl.BlockSpec((DIN, D), const),
                  pl.BlockSpec((D, D), const)],
        out_specs=[tile, tile, tile, tile, tile],
        out_shape=[jax.ShapeDtypeStruct((s, D), f32), act, act, act, act],
        name=name, compiler_params=_cp(("parallel",)),
    )(ya, yb, proj_g, proj_g, x, gate, wa, wb, wo)


def _merge_bwd(dxo, gate, o_sv, pa_sv, pb_sv, proj_g, wo_t, wa_t, wb_t, name):
    s = dxo.shape[0]
    ts = min(512, s)

    def body(dxo_ref, gate_ref, o_ref, pa_ref, pb_ref, ga_ref, gb_ref, wot_ref, wat_ref, wbt_ref,
             do_ref, dpa_ref, dpb_ref, dg_ref, dya_ref, dyb_ref, dgate_ref):
        @pl.when(pl.program_id(0) == 0)
        def _():
            dgate_ref[...] = jnp.zeros_like(dgate_ref)
        dxo_v = dxo_ref[...]
        dgate_ref[...] += jnp.sum(dxo_v * o_ref[...].astype(f32), axis=0, keepdims=True)
        do = (dxo_v * gate_ref[...]).astype(bf16)
        do_ref[...] = do
        dmg = _dot(do, wot_ref[...])
        sa = _sigmoid(ga_ref[...])
        sb = _sigmoid(gb_ref[...])
        dpa = (dmg * sa).astype(bf16)
        dpb = (dmg * sb).astype(bf16)
        dpa_ref[...] = dpa
        dpb_ref[...] = dpb
        dg_ref[:, 0:D] = (dmg * pa_ref[...].astype(f32) * sa * (1.0 - sa)).astype(bf16)
        dg_ref[:, D:2 * D] = (dmg * pb_ref[...].astype(f32) * sb * (1.0 - sb)).astype(bf16)
        dya_ref[...] = _dot(dpa, wat_ref[...]).astype(bf16)
        dyb_ref[...] = _dot(dpb, wbt_ref[...]).astype(bf16)

    tile = pl.BlockSpec((ts, D), lambda i: (i, 0))
    const = lambda i: (0, 0)
    act = jax.ShapeDtypeStruct((s, D), bf16)
    return pl.pallas_call(
        body, grid=(s // ts,),
        in_specs=[tile, pl.BlockSpec((1, D), const), tile, tile, tile, tile, pl.BlockSpec((ts, D), lambda i: (i, 1)),
                  pl.BlockSpec((D, D), const), pl.BlockSpec((D, D), const), pl.BlockSpec((D, DIN), const)],
        out_specs=[tile, tile, tile, pl.BlockSpec((ts, PG), lambda i: (i, 0)), tile, pl.BlockSpec((ts, DIN), lambda i: (i, 0)),
                   pl.BlockSpec((1, D), const)],
        out_shape=[act, act, act, jax.ShapeDtypeStruct((s, PG), bf16), act, jax.ShapeDtypeStruct((s, DIN), bf16),
                   jax.ShapeDtypeStruct((1, D), f32)],
        name=name, compiler_params=_cp(("arbitrary",)),
    )(dxo, gate, o_sv, pa_sv, pb_sv, proj_g, proj_g, wo_t, wa_t, wb_t)


def _final_loss(x, target, fnw, name):
    s = x.shape[0]
    ts = min(512, s)

    def body(x_ref, t_ref, w_ref, loss_ref, dx_ref, dw_ref):
        @pl.when(pl.program_id(0) == 0)
        def _():
            loss_ref[...] = jnp.zeros_like(loss_ref)
            dw_ref[...] = jnp.zeros_like(dw_ref)
        xv = x_ref[...]
        r = lax.rsqrt(jnp.mean(xv * xv, axis=-1, keepdims=True) + EPS)
        xn = xv * r
        err = xn * w_ref[...] - t_ref[...]
        part = jnp.sum(err * err, axis=0, keepdims=True)
        acc = part[:, 0:128]
        for k in range(1, D // 128):
            acc = acc + part[:, k * 128:(k + 1) * 128]
        loss_ref[0:1, :] += acc * (0.5 / D)
        dy = err * (1.0 / D)
        dw_ref[...] += jnp.sum(dy * xn, axis=0, keepdims=True)
        dxn = dy * w_ref[...]
        dx_ref[...] = r * (dxn - xn * jnp.mean(dxn * xn, axis=-1, keepdims=True))

    tile = pl.BlockSpec((ts, D), lambda i: (i, 0))
    row = pl.BlockSpec((1, D), lambda i: (0, 0))
    return pl.pallas_call(
        body, grid=(s // ts,), in_specs=[tile, tile, row],
        out_specs=[pl.BlockSpec((8, 128), lambda i: (0, 0)), tile, row],
        out_shape=[jax.ShapeDtypeStruct((8, 128), f32), jax.ShapeDtypeStruct((s, D), f32), jax.ShapeDtypeStruct((1, D), f32)],
        name=name, compiler_params=_cp(("arbitrary",)),
    )(x, target, fnw)


def _layer_operands(w_in, conv_w, wa, wb, wo, norm_w, gm_ln_w, gm_ln_b, gm_ws, gm_bs, conv_b, dt_bias, a_log, d_skip, ssm_norm_w):
    w_xbc = _group_major(w_in[:, 5120:7168], w_in[:, 7168:8192], w_in[:, 8192:9216])
    w_b = jnp.concatenate([w_xbc, w_in[:, 3072:5120], w_in[:, 9216:9248], jnp.zeros((D, 224), bf16)], axis=1)
    w_a = w_in[:, 0:3072]
    w_g = w_in[:, 9248:N_IN]
    tril = jnp.tril(jnp.ones((CH, CH), bool))
    wsm = jnp.where(tril[None], gm_ws, 0.0).astype(bf16)

    def heads_row(v):
        return jnp.pad(v, (0, CH - NH)).reshape(1, CH)

    return dict(
        w_b=w_b, w_a=w_a, w_g=w_g, w_b_t=w_b.T, w_a_t=w_a.T, w_g_t=w_g.T,
        wa=wa, wb=wb, wo=wo, wa_t=wa.T, wb_t=wb.T, wo_t=wo.T,
        norm_w=norm_w.reshape(1, D), lnw=gm_ln_w.reshape(1, D), lnb=gm_ln_b.reshape(1, D),
        wsm=wsm, wsm_t=jnp.swapaxes(wsm, 1, 2), bsf=jnp.repeat(gm_bs.T, CH, axis=1),
        cw=_group_major(conv_w[:, 0:DIN], conv_w[:, DIN:DIN + NG * NST], conv_w[:, DIN + NG * NST:CONVD]),
        cb=_group_major(conv_b[0:DIN], conv_b[DIN:DIN + NG * NST], conv_b[DIN + NG * NST:CONVD]).reshape(1, CONVD),
        dtb=heads_row(dt_bias), a_row=heads_row(-jnp.exp(a_log)),
        snw=ssm_norm_w.reshape(1, DIN), dkc=jnp.repeat(d_skip, HD).reshape(1, DIN),
        ind=(jnp.arange(GW)[:, None] // HD == jnp.arange(CH)[None, :]).astype(bf16),
    )


def _layer_fwd(x, shift, scale, gate, p, tag):
    h = _rmsmod_fwd(x, p["norm_w"], scale, shift, f"rmsmod_fwd{tag}")
    proj_b = _mm(h, p["w_b"], f32, f"proj_b{tag}")
    proj_a = _mm(h, p["w_a"], f32, f"proj_a{tag}")
    proj_g = _mm(h, p["w_g"], f32, f"proj_g{tag}")
    ya = _branch_a_fwd(proj_a, p["lnw"], p["lnb"], p["wsm"], p["bsf"], f"branch_a_fwd{tag}")
    yb, y_sv, states, conv_sv = _branch_b_fwd(proj_b, p["cw"], p["cb"], p["dtb"], p["a_row"], p["dkc"], p["snw"], f"branch_b_fwd{tag}")
    x_out, pa, pb, mg, o = _merge_fwd(ya, yb, proj_g, x, gate, p["wa"], p["wb"], p["wo"], f"merge_fwd{tag}")
    saved = dict(x=x, h=h, proj_b=proj_b, proj_a=proj_a, proj_g=proj_g, ya=ya, yb=yb, y=y_sv, states=states, conv=conv_sv,
                 pa=pa, pb=pb, mg=mg, o=o, scale=scale, gate=gate)
    return x_out, saved


def _layer_bwd(dxo, sv, p, tag):
    do, dpa, dpb, dg, dya, dyb, dgate = _merge_bwd(dxo, sv["gate"], sv["o"], sv["pa"], sv["pb"], sv["proj_g"],
                                                   p["wo_t"], p["wa_t"], p["wb_t"], f"merge_bwd{tag}")
    d_wo = _mm_tn(sv["mg"], do, f"d_wo{tag}")
    d_wa = _mm_tn(sv["ya"], dpa, f"d_wa{tag}")
    d_wb = _mm_tn(sv["yb"], dpb, f"d_wb{tag}")
    da, dws, dbs, dlnw, dlnb = _branch_a_bwd(sv["proj_a"], dya, p["lnw"], p["lnb"], p["wsm"], p["wsm_t"], p["bsf"],
                                             f"branch_a_bwd{tag}")
    db, dcw, dcb, ddtb, dal, ddk, dsnw = _branch_b_bwd(sv["proj_b"], sv["conv"], dyb, sv["y"], sv["states"], p["cw"], p["dtb"],
                                                       p["a_row"], p["dkc"], p["snw"], p["ind"], f"branch_b_bwd{tag}")
    dh = _mm(db, p["w_b_t"], f32, f"dh_b{tag}")
    dh = _mm(da, p["w_a_t"], f32, f"dh_a{tag}", c_in=dh)
    dh = _mm(dg, p["w_g_t"], f32, f"dh_g{tag}", c_in=dh)
    d_w_b_t = _mm_tn(db, sv["h"], f"d_w_b{tag}")
    d_w_a_t = _mm_tn(da, sv["h"], f"d_w_a{tag}")
    d_w_g_t = _mm_tn(dg, sv["h"], f"d_w_g{tag}")
    dx, dscale, dshift, dnw = _rmsmod_bwd(dh, sv["x"], dxo, p["norm_w"], sv["scale"], f"rmsmod_bwd{tag}")
    d_w_in_t = jnp.concatenate([d_w_a_t, d_w_b_t[CONVD:CONVD + DIN], _rows_from_group_major(d_w_b_t[0:CONVD]),
                                d_w_b_t[CONVD + DIN:CONVD + DIN + NH], d_w_g_t], axis=0)
    tril = jnp.tril(jnp.ones((CH, CH), bool))
    heads = lambda v: v[0, 0:NH]
    grads = dict(
        w_in_t=d_w_in_t, w_proj_a=d_wa, w_proj_b=d_wb, w_out=d_wo, conv_w=_from_group_major(dcw), conv_b=_from_group_major(dcb).reshape(CONVD),
        norm_w=dnw.reshape(D), gm_ln_w=dlnw.reshape(D), gm_ln_b=dlnb.reshape(D),
        gm_ws=jnp.where(tril[None], dws, 0.0), gm_bs=dbs.reshape(CH, NG, CH).sum(-1).T,
        dt_bias=heads(ddtb), a_log=heads(dal), d_skip=ddk.reshape(NH, HD).sum(-1), ssm_norm_w=dsnw.reshape(DIN),
        mod=jnp.concatenate([dshift, dscale, dgate], axis=1).reshape(3 * D),
    )
    return dx, grads


def _local_step(x, target, mods, operands_of, fnw):
    saved, layer_ops = [], []
    for l in range(DEPTH):
        shift, scale, gate = mods[l]
        p, x = operands_of(l, x)
        layer_ops.append(p)
        x, sv = _layer_fwd(x, shift, scale, gate, p, f"_l{l}")
        saved.append(sv)
    loss_parts, dx, dfnw = _final_loss(x, target, fnw.reshape(1, D), "final_loss")
    grads = [None] * DEPTH
    for l in reversed(range(DEPTH)):
        dx, grads[l] = _layer_bwd(dx, saved[l], layer_ops[l], f"_l{l}")
    return loss_parts, dx, grads, dfnw.reshape(D)


ADA_COLS = 3 * D // NSHARD


def _ada_fwd(c_all, ada_w, ada_b_cols, name):
    def body(c_ref, w_ref, b_ref, o_ref):
        cv = c_ref[...]
        sc = cv * _sigmoid(cv)
        for l in range(DEPTH):
            o_ref[l] = _dot_hi(sc, w_ref[l]) + b_ref[l]

    return pl.pallas_call(body, out_shape=jax.ShapeDtypeStruct((DEPTH, 8, ADA_COLS), f32), name=name,
                          compiler_params=_cp(None))(c_all, ada_w, ada_b_cols)


def _adam_math(w, g, m, v):
    m = ADAM_B1 * m + (1.0 - ADAM_B1) * g
    v = ADAM_B2 * v + (1.0 - ADAM_B2) * (g * g)
    m_hat = m / (1.0 - ADAM_B1 ** ADAM_STEP)
    v_hat = v / (1.0 - ADAM_B2 ** ADAM_STEP)
    delta = -ADAM_LR * (m_hat / (jnp.sqrt(v_hat) + ADAM_EPS) + ADAM_WD * w)
    return delta, m, v


def _ada_bwd_adamw(c_all, dmod_cols, w, m, v, name):
    tr = 256

    def body(c_ref, dm_ref, w_ref, m_ref, v_ref, g_ref, d_ref, nm_ref, nv_ref):
        cv = c_ref[...]
        sc = cv * _sigmoid(cv)
        g = lax.dot_general(sc, dm_ref[0], (((0,), (0,)), ((), ())), precision=_HI, preferred_element_type=f32)
        g_ref[0] = g
        d_ref[0], nm_ref[0], nv_ref[0] = _adam_math(w_ref[0], g, m_ref[0], v_ref[0])

    blk = pl.BlockSpec((1, tr, ADA_COLS), lambda l, i: (l, i, 0))
    shp = jax.ShapeDtypeStruct((DEPTH, D, ADA_COLS), f32)
    return pl.pallas_call(
        body, grid=(DEPTH, D // tr),
        in_specs=[pl.BlockSpec((8, tr), lambda l, i: (0, i)), pl.BlockSpec((1, 8, ADA_COLS), lambda l, i: (l, 0, 0)), blk, blk, blk],
        out_specs=[blk, blk, blk, blk], out_shape=[shp, shp, shp, shp], name=name, compiler_params=_cp(("parallel", "parallel")),
    )(c_all, dmod_cols, w, m, v)


def _adamw(w, g, m, v, name):
    def body(w_ref, g_ref, m_ref, v_ref, d_ref, nm_ref, nv_ref):
        d_ref[...], nm_ref[...], nv_ref[...] = _adam_math(w_ref[...], g_ref[...], m_ref[...], v_ref[...])

    shp = jax.ShapeDtypeStruct(w.shape, f32)
    return pl.pallas_call(body, out_shape=[shp] * 3, name=name, compiler_params=_cp(None))(w, g, m, v)


def _tile2(r, c):
    if r <= 256 or r % 256 == 0:
        return _pick(r, (256,)), _pick(c, (1024,))
    return r, 128


def _adamw_layers(w, g_mine, g_other, m, v, c_idx, name):
    _, r, c = w.shape
    tr, tc = _tile2(r, c)

    def body(ci_ref, w_ref, gm_ref, go_ref, m_ref, v_ref, g_ref, d_ref, nm_ref, nv_ref):
        def update(g):
            g_ref[0] = g
            d_ref[0], nm_ref[0], nv_ref[0] = _adam_math(w_ref[0], g, m_ref[0], v_ref[0])

        mine = pl.program_id(0) == ci_ref[0]

        @pl.when(mine)
        def _():
            update(gm_ref[...])

        @pl.when(jnp.logical_not(mine))
        def _():
            update(go_ref[...])

    blk = pl.BlockSpec((1, tr, tc), lambda l, i, j, ci: (l, i, j))
    gblk = pl.BlockSpec((tr, tc), lambda l, i, j, ci: (i, j))
    shp = jax.ShapeDtypeStruct(w.shape, f32)
    return pl.pallas_call(
        body,
        grid_spec=pltpu.PrefetchScalarGridSpec(num_scalar_prefetch=1, grid=(DEPTH, r // tr, c // tc),
                                               in_specs=[blk, gblk, gblk, blk, blk], out_specs=[blk, blk, blk, blk]),
        out_shape=[shp, shp, shp, shp], name=name, compiler_params=_cp(("parallel", "parallel", "parallel")),
    )(c_idx, w, g_mine, g_other, m, v)


_MESH = pl.DeviceIdType.MESH
_AXES = ("x", "y", "c")
_HBM = pl.BlockSpec(memory_space=pltpu.HBM)


def _my_place():
    return tuple(lax.axis_index(a) for a in _AXES)


def _allreduce(buf, axes, name):
    r = buf.shape[0]
    n = len(axes)

    def body(x_ref, o_ref, rbuf, ssem, rsem):
        me = dict(zip(_AXES, _my_place()))
        o_ref[...] = x_ref[...]
        for k, ax in enumerate(axes):
            peer = tuple(1 - me[a] if a == ax else me[a] for a in _AXES)
            cp = pltpu.make_async_remote_copy(src_ref=o_ref, dst_ref=rbuf.at[k], send_sem=ssem.at[k], recv_sem=rsem.at[k],
                                              device_id=peer, device_id_type=_MESH)
            cp.start()
            cp.wait()
            o_ref[...] = o_ref[...] + rbuf[k]

    vm = pl.BlockSpec(memory_space=pltpu.VMEM)
    return pl.pallas_call(
        body, out_shape=jax.ShapeDtypeStruct((r, 128), f32), in_specs=[vm], out_specs=vm,
        scratch_shapes=[pltpu.VMEM((n, r, 128), f32), pltpu.SemaphoreType.DMA((n,)), pltpu.SemaphoreType.DMA((n,))],
        name=name, compiler_params=pltpu.CompilerParams(vmem_limit_bytes=VMEM_LIMIT),
    )(buf)


def _other_chips(x, y):
    return [(1 - x, y), (x, 1 - y), (1 - x, 1 - y)]


def _gather_body(ins, outs, ssem, rsem):
    na = len(ins)
    x, y, c = _my_place()
    sibling = (x, y, 1 - c)
    chips = _other_chips(x, y)
    half = [r.shape[0] // 2 if r.shape[0] % 32 == 0 else None for r in ins]

    def part(ref, a, core):
        return ref if half[a] is None else ref.at[pl.ds(core * half[a], half[a])]

    def rcopy(a, src, j, core, to, idx):
        return pltpu.make_async_remote_copy(src_ref=src, dst_ref=part(outs[a].at[j], a, core), send_sem=ssem.at[idx],
                                            recv_sem=rsem.at[idx], device_id=to, device_id_type=_MESH)

    sent = []
    for j, chip in enumerate(chips):
        for a in range(na):
            cp = rcopy(a, part(ins[a], a, c), j, c, (*chip, c), j * na + a)
            cp.start()
            sent.append(cp)
    for j, chip in enumerate(chips):
        for a in range(na):
            rcopy(a, part(ins[a], a, c), j, c, (*chip, c), j * na + a).wait_recv()
            if half[a] is not None:
                cp = rcopy(a, part(outs[a].at[j], a, c), j, c, sibling, (3 + j) * na + a)
                cp.start()
                sent.append(cp)
    for j in range(3):
        for a in range(na):
            if half[a] is not None:
                rcopy(a, part(ins[a], a, c), j, 1 - c, sibling, (3 + j) * na + a).wait_recv()
    for cp in sent:
        cp.wait_send()


def _gather_layer(shards, name):
    na = len(shards)

    def body(*refs):
        _gather_body(refs[:na], refs[na:2 * na], refs[2 * na], refs[2 * na + 1])

    return pl.pallas_call(
        body, out_shape=[jax.ShapeDtypeStruct((3,) + s.shape, s.dtype) for s in shards], in_specs=[_HBM] * na, out_specs=[_HBM] * na,
        scratch_shapes=[pltpu.SemaphoreType.DMA((6 * na,)), pltpu.SemaphoreType.DMA((6 * na,))], name=name,
    )(*shards)


def _gather_layer_behind(shards, name):
    na = len(shards)
    hbm = pltpu.MemorySpace.HBM
    ins = [jax.new_ref(s, memory_space=hbm) for s in shards]
    outs = [jax.empty_ref(jax.ShapeDtypeStruct((3,) + s.shape, s.dtype), memory_space=hbm) for s in shards]

    @pl.kernel(mesh=plsc.ScalarSubcoreMesh(axis_name="sequencer", num_cores=1), name=name,
               scratch_types=(pltpu.SemaphoreType.DMA((6 * na,)), pltpu.SemaphoreType.DMA((6 * na,))),
               compiler_params=pltpu.CompilerParams(collective_id=1))
    def launch(ssem, rsem):
        x, y, c = _my_place()
        barrier = pltpu.get_barrier_semaphore()
        peers = [(*chip, c) for chip in _other_chips(x, y)] + [(x, y, 1 - c)]
        for peer in peers:
            pl.semaphore_signal(barrier, inc=1, device_id=peer, device_id_type=_MESH)
        pl.semaphore_wait(barrier, len(peers))
        _gather_body(ins, outs, ssem, rsem)

    launch()
    return [o[...] for o in outs]


def _with_own(own, others, k):
    xi, yi, _ = _my_place()
    d = k ^ (2 * xi + yi)
    j = jnp.where(d == 2, 0, jnp.where(d == 1, 1, 2))
    return jnp.where(d == 0, own, lax.dynamic_index_in_dim(others, j, axis=0, keepdims=False))


def _swap_layers(parts, name):
    na = len(parts)

    def body(*refs):
        ins, outs = refs[:2 * na], refs[2 * na:3 * na]
        ssem, rsem = refs[3 * na:]
        x, y, c = _my_place()

        def copy(a, layer):
            return pltpu.make_async_remote_copy(src_ref=ins[2 * a + layer], dst_ref=outs[a], send_sem=ssem.at[a], recv_sem=rsem.at[a],
                                                device_id=(x, y, 1 - c), device_id_type=_MESH)

        for layer in range(DEPTH):
            @pl.when(c == 1 - layer)
            def _():
                for a in range(na):
                    copy(a, layer).start()
        for a in range(na):
            copy(a, 0).wait()

    flat = [p for pair in parts for p in pair]
    return pl.pallas_call(
        body, out_shape=[jax.ShapeDtypeStruct(p0.shape, p0.dtype) for p0, _ in parts], in_specs=[_HBM] * (2 * na),
        out_specs=[_HBM] * na, scratch_shapes=[pltpu.SemaphoreType.DMA((na,)), pltpu.SemaphoreType.DMA((na,))], name=name,
    )(*flat)


def _scatter_shards(sums, name):
    na = len(sums)

    def body(*refs):
        ins, outs = refs[:na], refs[na:2 * na]
        ssem, rsem = refs[2 * na:]
        x, y, c = _my_place()
        cps = []
        for j, chip in enumerate(_other_chips(x, y)):
            kj = 2 * chip[0] + chip[1]
            for a in range(na):
                cps.append(pltpu.make_async_remote_copy(
                    src_ref=ins[a].at[kj], dst_ref=outs[a].at[j], send_sem=ssem.at[j * na + a], recv_sem=rsem.at[j * na + a],
                    device_id=(*chip, c), device_id_type=_MESH))
        for cp in cps:
            cp.start()
        for cp in cps:
            cp.wait()

    return pl.pallas_call(
        body, out_shape=[jax.ShapeDtypeStruct((3,) + p.shape[1:], p.dtype) for p in sums], in_specs=[_HBM] * na,
        out_specs=[_HBM] * na, scratch_shapes=[pltpu.SemaphoreType.DMA((3 * na,)), pltpu.SemaphoreType.DMA((3 * na,))], name=name,
    )(*sums)


def _share_layers(finals, name):
    na = len(finals)

    def body(*refs):
        ins, outs = refs[:na], refs[na:2 * na]
        ssem, rsem = refs[2 * na:]
        x, y, c = _my_place()
        cps = [pltpu.make_async_remote_copy(src_ref=ins[a], dst_ref=outs[a], send_sem=ssem.at[a], recv_sem=rsem.at[a],
                                            device_id=(x, y, 1 - c), device_id_type=_MESH) for a in range(na)]
        for cp in cps:
            cp.start()
        for cp in cps:
            cp.wait()

    return pl.pallas_call(
        body, out_shape=[jax.ShapeDtypeStruct(p.shape, p.dtype) for p in finals], in_specs=[_HBM] * na, out_specs=[_HBM] * na,
        scratch_shapes=[pltpu.SemaphoreType.DMA((na,)), pltpu.SemaphoreType.DMA((na,))], name=name,
    )(*finals)


def _add_own_layer(part0, part1, recv, c_idx, name):
    ns, r, c = recv.shape
    tr, tc = _tile2(r, c)

    def body(ci_ref, p0_ref, p1_ref, r_ref, o_ref, ob_ref):
        def add(p_ref):
            t = p_ref[...] + r_ref[...]
            o_ref[...] = t
            ob_ref[...] = t.astype(bf16)

        @pl.when(ci_ref[0] == 0)
        def _():
            add(p0_ref)

        @pl.when(ci_ref[0] == 1)
        def _():
            add(p1_ref)

    blk = pl.BlockSpec((1, tr, tc), lambda k, i, j, ci: (k, i, j))
    blk0 = pl.BlockSpec((1, tr, tc), lambda k, i, j, ci: (k * (1 - ci[0]), i * (1 - ci[0]), j * (1 - ci[0])))
    blk1 = pl.BlockSpec((1, tr, tc), lambda k, i, j, ci: (k * ci[0], i * ci[0], j * ci[0]))
    return pl.pallas_call(
        body,
        grid_spec=pltpu.PrefetchScalarGridSpec(num_scalar_prefetch=1, grid=(ns, r // tr, c // tc), in_specs=[blk0, blk1, blk],
                                               out_specs=[blk, blk]),
        out_shape=[jax.ShapeDtypeStruct((ns, r, c), f32), jax.ShapeDtypeStruct((ns, r, c), bf16)], name=name,
        compiler_params=_cp(("arbitrary", "arbitrary", "arbitrary")),
    )(c_idx, part0, part1, recv)


def _add_own_shard(sums, recv, k_idx, name):
    _, r, c = sums.shape
    tr, tc = _tile2(r, c)

    def body(ki_ref, s_ref, r_ref, o_ref):
        o_ref[...] = ((s_ref[0] + r_ref[0].astype(f32)) + r_ref[1].astype(f32)) + r_ref[2].astype(f32)

    return pl.pallas_call(
        body,
        grid_spec=pltpu.PrefetchScalarGridSpec(
            num_scalar_prefetch=1, grid=(r // tr, c // tc),
            in_specs=[pl.BlockSpec((1, tr, tc), lambda i, j, ki: (ki[0], i, j)), pl.BlockSpec((3, tr, tc), lambda i, j, ki: (0, i, j))],
            out_specs=pl.BlockSpec((tr, tc), lambda i, j, ki: (i, j))),
        out_shape=jax.ShapeDtypeStruct((r, c), f32), name=name, compiler_params=_cp(("parallel", "parallel")),
    )(k_idx, sums, recv)


def _reduce_scatter(parts, tag):
    x, y, c = _my_place()
    c_idx = jnp.reshape(c, (1,)).astype(jnp.int32)
    k_idx = jnp.reshape(2 * x + y, (1,)).astype(jnp.int32)
    na = len(parts)
    recv = _swap_layers(parts, f"rs_swap{tag}")
    sums = [_add_own_layer(parts[a][0], parts[a][1], recv[a], c_idx, f"rs_add_layer{tag}_{a}") for a in range(na)]
    recv = _scatter_shards([sb for _, sb in sums], f"rs_scatter{tag}")
    finals = [_add_own_shard(sums[a][0], recv[a], k_idx, f"rs_add_shard{tag}_{a}") for a in range(na)]
    return finals, _share_layers(finals, f"rs_share{tag}"), c_idx


_SMALL = [("ada_b", (DEPTH, 3 * D)), ("norm_w", (DEPTH, D)), ("gm_ln_w", (DEPTH, D)), ("gm_ln_b", (DEPTH, D)),
          ("gm_ws", (DEPTH, NG, CH, CH)), ("gm_bs", (DEPTH, NG, CH)), ("conv_b", (DEPTH, CONVD)), ("dt_bias", (DEPTH, NH)),
          ("a_log", (DEPTH, NH)), ("d_skip", (DEPTH, NH)), ("ssm_norm_w", (DEPTH, DIN)), ("final_norm_w", (D,))]


def _rows_of(shape):
    n = 1
    for d in shape:
        n *= d
    return -(-n // 1024) * 8


def _pack(arrays):
    rows = []
    for a in arrays:
        flat = a.reshape(-1)
        r = _rows_of(a.shape)
        rows.append(jnp.pad(flat, (0, r * 128 - flat.shape[0])).reshape(r, 128))
    return jnp.concatenate(rows, axis=0)


def _unpack(buf, shapes):
    out, at = [], 0
    for shp in shapes:
        r = _rows_of(shp)
        n = 1
        for d in shp:
            n *= d
        out.append(buf[at:at + r].reshape(-1)[:n].reshape(shp))
        at += r
    return out


def kernel(x, c, ada_w, ada_b, norm_w, w_in, gm_ln_w, gm_ln_b, gm_ws, gm_bs, conv_w, conv_b, dt_bias, a_log, d_skip, ssm_norm_w, w_proj_a, w_proj_b, w_out, final_norm_w, loss_target, m_ada_w, m_ada_b, m_norm_w, m_w_in, m_gm_ln_w, m_gm_ln_b, m_gm_ws, m_gm_bs, m_conv_w, m_conv_b, m_dt_bias, m_a_log, m_d_skip, m_ssm_norm_w, m_w_proj_a, m_w_proj_b, m_w_out, m_final_norm_w, v_ada_w, v_ada_b, v_norm_w, v_w_in, v_gm_ln_w, v_gm_ln_b, v_gm_ws, v_gm_bs, v_conv_w, v_conv_b, v_dt_bias, v_a_log, v_d_skip, v_ssm_norm_w, v_w_proj_a, v_w_proj_b, v_w_out, v_final_norm_w):
    xi, yi, ci = _my_place()
    k_me = 2 * xi + yi
    b_me = 4 * xi + 2 * yi + ci
    w = dict(ada_b=ada_b, norm_w=norm_w, gm_ln_w=gm_ln_w, gm_ln_b=gm_ln_b, gm_ws=gm_ws, gm_bs=gm_bs, conv_b=conv_b, dt_bias=dt_bias,
             a_log=a_log, d_skip=d_skip, ssm_norm_w=ssm_norm_w, final_norm_w=final_norm_w)
    m = dict(ada_b=m_ada_b, norm_w=m_norm_w, gm_ln_w=m_gm_ln_w, gm_ln_b=m_gm_ln_b, gm_ws=m_gm_ws, gm_bs=m_gm_bs, conv_b=m_conv_b,
             dt_bias=m_dt_bias, a_log=m_a_log, d_skip=m_d_skip, ssm_norm_w=m_ssm_norm_w, final_norm_w=m_final_norm_w)
    v = dict(ada_b=v_ada_b, norm_w=v_norm_w, gm_ln_w=v_gm_ln_w, gm_ln_b=v_gm_ln_b, gm_ws=v_gm_ws, gm_bs=v_gm_bs, conv_b=v_conv_b,
             dt_bias=v_dt_bias, a_log=v_a_log, d_skip=v_d_skip, ssm_norm_w=v_ssm_norm_w, final_norm_w=v_final_norm_w)

    c_slot = lax.dynamic_update_slice(jnp.zeros((8, D), f32), c, (b_me, 0))
    c_all = _allreduce(c_slot.reshape(64, 128), _AXES, "gather_c").reshape(8, D)
    ada_b_cols = lax.dynamic_slice(ada_b, (0, k_me * ADA_COLS), (DEPTH, ADA_COLS)).reshape(DEPTH, 1, ADA_COLS)
    mod_cols = _ada_fwd(c_all, ada_w, ada_b_cols, "ada_fwd")
    mod_slot = lax.dynamic_update_slice(jnp.zeros((DEPTH, 8, 3 * D), f32), mod_cols, (0, 0, k_me * ADA_COLS))
    mod_all = _allreduce(mod_slot.reshape(-1, 128), ("x", "y"), "gather_mod").reshape(DEPTH, 8, 3 * D)
    mod_me = lax.dynamic_slice(mod_all, (0, b_me, 0), (DEPTH, 1, 3 * D))
    mods = [(mod_me[l, :, 0:D], mod_me[l, :, D:2 * D], mod_me[l, :, 2 * D:3 * D]) for l in range(DEPTH)]

    rows_sh = jnp.concatenate([w_proj_a, w_proj_b, w_out], axis=1).astype(bf16)
    win_sh = w_in.astype(bf16)
    first = _gather_layer([win_sh[0], rows_sh[0], conv_w[0]], "gather_l0")
    first, later = lax.optimization_barrier((first, [win_sh[1], rows_sh[1], conv_w[1]]))
    others = [first, _gather_layer_behind(later, "gather_l1")]

    def operands_of(l, x_in):
        gathered = others[l]
        if l == 1:
            gathered, x_in = lax.optimization_barrier((gathered, x_in))
        win_oth, rows_oth, conv_oth = gathered
        rows_l = [_with_own(rows_sh[l], rows_oth, k) for k in range(NSHARD)]
        w_in_l = jnp.concatenate([_with_own(win_sh[l], win_oth, k) for k in range(NSHARD)], axis=1)
        wa_l = jnp.concatenate([t[0:256] for t in rows_l], axis=0)
        wb_l = jnp.concatenate([t[256:768] for t in rows_l], axis=0)
        wo_l = jnp.concatenate([t[768:1024] for t in rows_l], axis=0)
        cw_l = jnp.concatenate([_with_own(conv_w[l], conv_oth, k) for k in range(NSHARD)], axis=1)
        return _layer_operands(w_in_l, cw_l, wa_l, wb_l, wo_l, norm_w[l], gm_ln_w[l], gm_ln_b[l], gm_ws[l], gm_bs[l], conv_b[l],
                               dt_bias[l], a_log[l], d_skip[l], ssm_norm_w[l]), x_in

    loss_parts, dx, grads, dfnw = _local_step(x[0], loss_target[0], mods, operands_of, final_norm_w)

    s_in = N_IN // NSHARD
    tr_ = lambda t: jnp.swapaxes(t, 1, 2)
    g_in = [grads[l]["w_in_t"].reshape(NSHARD, s_in, D) for l in range(DEPTH)]
    g_rows = [jnp.concatenate([grads[l]["w_proj_a"].reshape(NSHARD, 256, D), grads[l]["w_proj_b"].reshape(NSHARD, 512, D),
                               grads[l]["w_out"].reshape(NSHARD, 256, D)], axis=1) for l in range(DEPTH)]
    g_conv = [grads[l]["conv_w"].reshape(4, NSHARD, D).transpose(1, 0, 2) for l in range(DEPTH)]
    (f_in, f_rows, f_conv), (o_in, o_rows, o_conv), c_idx = _reduce_scatter([g_in, g_rows, g_conv], "")
    gr_in, d_in, nm_in, nv_in = [tr_(t) for t in _adamw_layers(tr_(w_in), f_in, o_in, tr_(m_w_in), tr_(v_w_in), c_idx, "adamw_w_in")]
    cat = lambda a, b, c_: jnp.concatenate([a, b, c_], axis=1)
    gr_rows, d_rows, nm_rows, nv_rows = _adamw_layers(cat(w_proj_a, w_proj_b, w_out), f_rows, o_rows, cat(m_w_proj_a, m_w_proj_b, m_w_out),
                                                      cat(v_w_proj_a, v_w_proj_b, v_w_out), c_idx, "adamw_rows")
    gr_conv = jnp.where(ci == 0, jnp.stack([f_conv, o_conv]), jnp.stack([o_conv, f_conv]))
    split = lambda t: (t[:, 0:256], t[:, 256:768], t[:, 768:1024])

    dmod_slot = lax.dynamic_update_slice(jnp.zeros((DEPTH, 8, 3 * D), f32),
                                         jnp.stack([grads[l]["mod"] for l in range(DEPTH)]).reshape(DEPTH, 1, 3 * D), (0, b_me, 0))
    small_g = {n: (dfnw if n == "final_norm_w" else jnp.stack([grads[l]["mod" if n == "ada_b" else n] for l in range(DEPTH)]))
               for n, _ in _SMALL}
    packed = _allreduce(_pack([small_g[n] for n, _ in _SMALL] + [dmod_slot, loss_parts]), _AXES, "allreduce_small")
    n_small = sum(_rows_of(s) for _, s in _SMALL)
    n_dmod = _rows_of(dmod_slot.shape)
    g_small = packed[0:n_small]
    dmod_all = packed[n_small:n_small + n_dmod].reshape(DEPTH, 8, 3 * D)
    loss = jnp.sum(packed[n_small + n_dmod:])
    small_gw = jnp.concatenate([g_small, _pack([gr_conv])], axis=0)
    d_s, nm_s, nv_s = _adamw(_pack([w[n] for n, _ in _SMALL] + [conv_w]), small_gw,
                             _pack([m[n] for n, _ in _SMALL] + [m_conv_w]), _pack([v[n] for n, _ in _SMALL] + [v_conv_w]), "adamw_small")
    shapes = [s for _, s in _SMALL] + [conv_w.shape]
    names = [n for n, _ in _SMALL] + ["conv_w"]
    g_d = dict(zip(names, _unpack(small_gw, shapes)))
    d_d = dict(zip(names, _unpack(d_s, shapes)))
    nm_d = dict(zip(names, _unpack(nm_s, shapes)))
    nv_d = dict(zip(names, _unpack(nv_s, shapes)))

    dmod_cols = lax.dynamic_slice(dmod_all, (0, 0, k_me * ADA_COLS), (DEPTH, 8, ADA_COLS))
    g_ada, d_ada, nm_ada, nv_ada = _ada_bwd_adamw(c_all, dmod_cols, ada_w, m_ada_w, v_ada_w, "ada_bwd_adamw")

    def by_name(big, small):
        ga, gb, go = split(big[1])
        return dict(small, ada_w=big[2], w_in=big[0], w_proj_a=ga, w_proj_b=gb, w_out=go)

    order = ["ada_w", "ada_b", "norm_w", "w_in", "gm_ln_w", "gm_ln_b", "gm_ws", "gm_bs", "conv_w", "conv_b", "dt_bias", "a_log",
             "d_skip", "ssm_norm_w", "w_proj_a", "w_proj_b", "w_out", "final_norm_w"]
    outs = []
    for big, small in (((gr_in, gr_rows, g_ada), g_d), ((d_in, d_rows, d_ada), d_d), ((nm_in, nm_rows, nm_ada), nm_d),
                       ((nv_in, nv_rows, nv_ada), nv_d)):
        t = by_name(big, small)
        outs += [t[n] for n in order]
    return (loss, dx.reshape(1, -1, D), *outs)
```

```python
import jax
import jax.numpy as jnp
from jax import lax
from jax.experimental import pallas as pl
from jax.experimental.pallas import tpu as pltpu
from jax.experimental.pallas import tpu_sc as plsc

f32 = jnp.float32
bf16 = jnp.bfloat16

D = 1024
DEPTH = 2
EPS = 1e-6
CH = 128
NG = 8
HPG = 4
HD = 64
NH = NG * HPG
NST = 128
DIN = 2048
CONVD = 4096
GW = DIN // NG
PB = CONVD + DIN + 256
PA = 3 * D
PG = 2 * D
N_IN = 11296
NSHARD = 4
V7X_VMEM_BYTES = 64 * 2 ** 20
VMEM_LIMIT = V7X_VMEM_BYTES - 8 * 2 ** 20

ADAM_LR, ADAM_B1, ADAM_B2, ADAM_EPS, ADAM_WD, ADAM_STEP = 0.001, 0.9, 0.999, 1e-08, 0.01, 10

_HI = lax.Precision.HIGHEST


def _cp(sem):
    return pltpu.CompilerParams(dimension_semantics=sem, vmem_limit_bytes=VMEM_LIMIT)


def _sigmoid(x):
    return 0.5 * jnp.tanh(0.5 * x) + 0.5


def _silu_and_grad(x):
    s = _sigmoid(x)
    return x * s, s * (1.0 + x * (1.0 - s))


_GELU_K = 0.7978845608028654
_GELU_C = 0.044715


def _gelu_and_grad(x):
    x2 = x * x
    t = jnp.tanh(_GELU_K * (x + _GELU_C * x * x2))
    g = 0.5 * x * (1.0 + t)
    dg = 0.5 * (1.0 + t) + 0.5 * x * (1.0 - t * t) * _GELU_K * (1.0 + 3.0 * _GELU_C * x2)
    return g, dg


def _gelu(x):
    t = jnp.tanh(_GELU_K * (x + _GELU_C * x * x * x))
    return 0.5 * x * (1.0 + t)


def _softplus(x):
    return jnp.maximum(x, 0.0) + jnp.log(1.0 + jnp.exp(-jnp.abs(x)))


def _dot(a, b):
    return jnp.dot(a, b, preferred_element_type=f32)


def _dot_nt(a, b):
    return lax.dot_general(a, b, (((1,), (1,)), ((), ())), preferred_element_type=f32)


def _dot_tn(a, b):
    return lax.dot_general(a, b, (((0,), (0,)), ((), ())), preferred_element_type=f32)


def _dot_hi(a, b):
    return jnp.dot(a, b, precision=_HI, preferred_element_type=f32)


def _rmsmod_fwd(x, nw, scale, shift, name):
    s = x.shape[0]
    ts = min(512, s)

    def body(x_ref, nw_ref, sc_ref, sh_ref, h_ref):
        xv = x_ref[...]
        r = lax.rsqrt(jnp.mean(xv * xv, axis=-1, keepdims=True) + EPS)
        h_ref[...] = ((xv * r) * nw_ref[...] * (1.0 + sc_ref[...]) + sh_ref[...]).astype(bf16)

    row = pl.BlockSpec((1, D), lambda i: (0, 0))
    tile = pl.BlockSpec((ts, D), lambda i: (i, 0))
    return pl.pallas_call(
        body, grid=(s // ts,), in_specs=[tile, row, row, row], out_specs=tile,
        out_shape=jax.ShapeDtypeStruct((s, D), bf16), name=name, compiler_params=_cp(("parallel",)),
    )(x, nw, scale, shift)


def _rmsmod_bwd(dh, x, dres, nw, scale, name):
    s = x.shape[0]
    ts = min(512, s)

    def body(dh_ref, x_ref, dres_ref, nw_ref, sc_ref, dx_ref, dsc_ref, dsh_ref, dnw_ref):
        @pl.when(pl.program_id(0) == 0)
        def _():
            dsc_ref[...] = jnp.zeros_like(dsc_ref)
            dsh_ref[...] = jnp.zeros_like(dsh_ref)
            dnw_ref[...] = jnp.zeros_like(dnw_ref)
        xv = x_ref[...]
        dhv = dh_ref[...]
        r = lax.rsqrt(jnp.mean(xv * xv, axis=-1, keepdims=True) + EPS)
        xn = xv * r
        one_sc = 1.0 + sc_ref[...]
        dsc_ref[...] += jnp.sum(dhv * xn * nw_ref[...], axis=0, keepdims=True)
        dsh_ref[...] += jnp.sum(dhv, axis=0, keepdims=True)
        dnw_ref[...] += jnp.sum(dhv * xn * one_sc, axis=0, keepdims=True)
        dxn = dhv * (nw_ref[...] * one_sc)
        dx_ref[...] = r * (dxn - xn * jnp.mean(dxn * xn, axis=-1, keepdims=True)) + dres_ref[...]

    row = pl.BlockSpec((1, D), lambda i: (0, 0))
    tile = pl.BlockSpec((ts, D), lambda i: (i, 0))
    vec = jax.ShapeDtypeStruct((1, D), f32)
    return pl.pallas_call(
        body, grid=(s // ts,), in_specs=[tile, tile, tile, row, row], out_specs=[tile, row, row, row],
        out_shape=[jax.ShapeDtypeStruct((s, D), f32), vec, vec, vec], name=name, compiler_params=_cp(("arbitrary",)),
    )(dh, x, dres, nw, scale)


def _pick(n, prefs):
    for p in prefs:
        if n % p == 0:
            return p
    return n


def _mm(a, b, out_dtype, name, c_in=None):
    m, k = a.shape
    n = b.shape[1]
    tm = _pick(m, (1024, 512, 256))
    tn = _pick(n, (1280, 1024, 512))
    tk = _pick(k, (1280, 1024, 512))
    nk = k // tk

    def body(*refs):
        if c_in is not None:
            a_ref, b_ref, c_ref, o_ref, acc = refs
        else:
            a_ref, b_ref, o_ref, acc = refs
        kk = pl.program_id(2)

        @pl.when(kk == 0)
        def _():
            if c_in is not None:
                acc[...] = c_ref[...]
            else:
                acc[...] = jnp.zeros_like(acc)
        acc[...] += _dot(a_ref[...], b_ref[...])

        @pl.when(kk == nk - 1)
        def _():
            o_ref[...] = acc[...].astype(out_dtype)

    in_specs = [pl.BlockSpec((tm, tk), lambda j, i, kk: (i, kk)), pl.BlockSpec((tk, tn), lambda j, i, kk: (kk, j))]
    args = [a, b]
    if c_in is not None:
        in_specs.append(pl.BlockSpec((tm, tn), lambda j, i, kk: (i, j)))
        args.append(c_in)
    return pl.pallas_call(
        body, grid=(n // tn, m // tm, nk), in_specs=in_specs, out_specs=pl.BlockSpec((tm, tn), lambda j, i, kk: (i, j)),
        out_shape=jax.ShapeDtypeStruct((m, n), out_dtype), scratch_shapes=[pltpu.VMEM((tm, tn), f32)], name=name,
        compiler_params=_cp(("parallel", "parallel", "arbitrary")),
    )(*args)


def _mm_tn(a, b, name):
    t, k1 = a.shape
    n = b.shape[1]
    t1 = _pick(k1, (1280, 1024, 512))
    tn = _pick(n, (1280, 1024, 512))
    tt = _pick(t, (2048, 1024, 512, 256))
    nt = t // tt

    def body(a_ref, b_ref, o_ref):
        tt_i = pl.program_id(2)

        @pl.when(tt_i == 0)
        def _():
            o_ref[...] = jnp.zeros_like(o_ref)
        o_ref[...] += _dot_tn(a_ref[...], b_ref[...])

    return pl.pallas_call(
        body, grid=(k1 // t1, n // tn, nt),
        in_specs=[pl.BlockSpec((tt, t1), lambda i, j, tt_i: (tt_i, i)), pl.BlockSpec((tt, tn), lambda i, j, tt_i: (tt_i, j))],
        out_specs=pl.BlockSpec((t1, tn), lambda i, j, tt_i: (i, j)),
        out_shape=jax.ShapeDtypeStruct((k1, n), f32), name=name,
        compiler_params=_cp(("parallel", "parallel", "arbitrary")),
    )(a, b)


def _ln_stats(v):
    mu = jnp.mean(v, axis=-1, keepdims=True)
    vc = v - mu
    rstd = lax.rsqrt(jnp.mean(vc * vc, axis=-1, keepdims=True) + EPS)
    return vc * rstd, rstd


def _mix(w_ref, vl):
    return jnp.concatenate([_dot(w_ref[g], vl[:, g * CH:(g + 1) * CH]) for g in range(NG)], axis=1)


def _branch_a_fwd(proj_a, lnw, lnb, wsm, bsf, name):
    s = proj_a.shape[0]
    ta = min(256, s)

    def body(pu_ref, pv_ref, pz_ref, lnw_ref, lnb_ref, w_ref, bs_ref, ya_ref):
        for c in range(ta // CH):
            rows = pl.ds(c * CH, CH)
            vh, _ = _ln_stats(_gelu(pv_ref[rows, :]))
            vl = (vh * lnw_ref[...] + lnb_ref[...]).astype(bf16)
            mixed = _mix(w_ref, vl) + bs_ref[...]
            pz = pz_ref[rows, :]
            ya_ref[rows, :] = (_gelu(pu_ref[rows, :]) * mixed * (pz * _sigmoid(pz))).astype(bf16)

    row = pl.BlockSpec((1, D), lambda i: (0, 0))
    return pl.pallas_call(
        body, grid=(s // ta,),
        in_specs=[pl.BlockSpec((ta, D), lambda i: (i, 0)), pl.BlockSpec((ta, D), lambda i: (i, 1)),
                  pl.BlockSpec((ta, D), lambda i: (i, 2)), row, row,
                  pl.BlockSpec((NG, CH, CH), lambda i: (0, 0, 0)), pl.BlockSpec((CH, D), lambda i: (0, 0))],
        out_specs=pl.BlockSpec((ta, D), lambda i: (i, 0)),
        out_shape=jax.ShapeDtypeStruct((s, D), bf16), name=name, compiler_params=_cp(("parallel",)),
    )(proj_a, proj_a, proj_a, lnw, lnb, wsm, bsf)


def _branch_a_bwd(proj_a, dya, lnw, lnb, wsm, wsm_t, bsf, name):
    s = proj_a.shape[0]
    ta = min(256, s)

    def body(pu_ref, pv_ref, pz_ref, dya_ref, lnw_ref, lnb_ref, w_ref, wt_ref, bs_ref,
             dp_ref, dws_ref, dbs_ref, dlnw_ref, dlnb_ref):
        @pl.when(pl.program_id(0) == 0)
        def _():
            dws_ref[...] = jnp.zeros_like(dws_ref)
            dbs_ref[...] = jnp.zeros_like(dbs_ref)
            dlnw_ref[...] = jnp.zeros_like(dlnw_ref)
            dlnb_ref[...] = jnp.zeros_like(dlnb_ref)
        for c in range(ta // CH):
            rows = pl.ds(c * CH, CH)
            u, du = _gelu_and_grad(pu_ref[rows, :])
            v, dv_act = _gelu_and_grad(pv_ref[rows, :])
            zg, dzg = _silu_and_grad(pz_ref[rows, :])
            vh, rstd = _ln_stats(v)
            vl = (vh * lnw_ref[...] + lnb_ref[...]).astype(bf16)
            mixed = _mix(w_ref, vl) + bs_ref[...]
            dy = dya_ref[rows, :].astype(f32)
            dmixed = dy * u * zg
            dp_ref[rows, 0:D] = (dy * mixed * zg * du).astype(bf16)
            dp_ref[rows, 2 * D:3 * D] = (dy * u * mixed * dzg).astype(bf16)
            dmb = dmixed.astype(bf16)
            dbs_ref[...] += dmixed
            dvl = _mix(wt_ref, dmb)
            for g in range(NG):
                cols = slice(g * CH, (g + 1) * CH)
                dws_ref[g] += _dot_nt(dmb[:, cols], vl[:, cols])
            dlnw_ref[...] += jnp.sum(dvl * vh, axis=0, keepdims=True)
            dlnb_ref[...] += jnp.sum(dvl, axis=0, keepdims=True)
            dvh = dvl * lnw_ref[...]
            dv = rstd * (dvh - jnp.mean(dvh, axis=-1, keepdims=True) - vh * jnp.mean(dvh * vh, axis=-1, keepdims=True))
            dp_ref[rows, D:2 * D] = (dv * dv_act).astype(bf16)

    row = pl.BlockSpec((1, D), lambda i: (0, 0))
    wspec = pl.BlockSpec((NG, CH, CH), lambda i: (0, 0, 0))
    full = pl.BlockSpec((CH, D), lambda i: (0, 0))
    return pl.pallas_call(
        body, grid=(s // ta,),
        in_specs=[pl.BlockSpec((ta, D), lambda i: (i, 0)), pl.BlockSpec((ta, D), lambda i: (i, 1)),
                  pl.BlockSpec((ta, D), lambda i: (i, 2)), pl.BlockSpec((ta, D), lambda i: (i, 0)),
                  row, row, wspec, wspec, full],
        out_specs=[pl.BlockSpec((ta, PA), lambda i: (i, 0)), wspec, full, row, row],
        out_shape=[jax.ShapeDtypeStruct((s, PA), bf16), jax.ShapeDtypeStruct((NG, CH, CH), f32),
                   jax.ShapeDtypeStruct((CH, D), f32), jax.ShapeDtypeStruct((1, D), f32), jax.ShapeDtypeStruct((1, D), f32)],
        name=name, compiler_params=_cp(("arbitrary",)),
    )(proj_a, proj_a, proj_a, dya, lnw, lnb, wsm, wsm_t, bsf)


GB = GW + 2 * NST


def _group_major(xs, b, c):
    lead = xs.shape[:-1]
    return jnp.concatenate([xs.reshape(lead + (NG, GW)), b.reshape(lead + (NG, NST)), c.reshape(lead + (NG, NST))],
                           axis=-1).reshape(lead + (CONVD,))


def _from_group_major(t):
    lead = t.shape[:-1]
    t = t.reshape(lead + (NG, GB))
    return jnp.concatenate([t[..., 0:GW].reshape(lead + (DIN,)), t[..., GW:GW + NST].reshape(lead + (NG * NST,)),
                            t[..., GW + NST:GB].reshape(lead + (NG * NST,))], axis=-1)


def _rows_from_group_major(t):
    t = t.reshape(NG, GB, t.shape[-1])
    return jnp.concatenate([t[:, 0:GW].reshape(DIN, -1), t[:, GW:GW + NST].reshape(NG * NST, -1),
                            t[:, GW + NST:GB].reshape(NG * NST, -1)], axis=0)


def _shift_rows(x, prev8, j):
    xr = pltpu.roll(x, j, 0)
    fix = pltpu.roll(prev8, j, 0)
    rid = lax.broadcasted_iota(jnp.int32, (8, x.shape[1]), 0)
    top = jnp.where(rid < j, fix, xr[0:8])
    return jnp.concatenate([top, xr[8:]], axis=0)


def _shift_rows_up(d, next8, j):
    dr = pltpu.roll(d, CH - j, 0)
    fix = pltpu.roll(next8, 8 - j, 0)
    rid = lax.broadcasted_iota(jnp.int32, (8, d.shape[1]), 0)
    bot = jnp.where(rid >= 8 - j, fix, dr[CH - 8:CH])
    return jnp.concatenate([dr[0:CH - 8], bot], axis=0)


def _conv_pre(x, prev8, cw_ref, cb_ref, cols):
    shifted = [_shift_rows(x, prev8, j) for j in (1, 2, 3)]
    conv = cb_ref[:, cols] + cw_ref[3:4, cols] * x
    for j in (1, 2, 3):
        conv = conv + cw_ref[3 - j:4 - j, cols] * shifted[j - 1]
    return conv, shifted


def _tril_mask():
    return lax.broadcasted_iota(jnp.int32, (CH, CH), 0) >= lax.broadcasted_iota(jnp.int32, (CH, CH), 1)


def _sum_all(v):
    return jnp.sum(jnp.sum(v, axis=0, keepdims=True), axis=1, keepdims=True)


def _lanes(g, width, base=0):
    return pl.ds(pl.multiple_of(base + g * width, width), width)


def _branch_b_fwd(proj_b, cw, cb, dtb, a_row, dkc, snw, name):
    s = proj_b.shape[0]
    nc = s // CH

    def body(xbc_ref, sz_ref, dtr_ref, cw_ref, cb_ref, dtb_ref, a_ref, dkc_ref, snw_ref,
             yb_ref, y_ref, st_ref, cv_ref, prev8, state, acst_s):
        @pl.when(pl.program_id(0) == 0)
        def _():
            prev8[...] = jnp.zeros_like(prev8)
            state[...] = jnp.zeros_like(state)
        st_ref[0] = state[...].astype(bf16)
        mask = _tril_mask()
        dt_all = _softplus(dtr_ref[:, 0:CH] + dtb_ref[...])
        acs_all = _dot_hi(mask.astype(f32), dt_all * a_ref[...])
        acst_s[...] = acs_all.T

        def group(g, carry):
            cols = _lanes(g, GB)
            gcols = _lanes(g, GW)
            x = xbc_ref[:, cols]
            conv, _ = _conv_pre(x, prev8[:, cols], cw_ref, cb_ref, cols)
            prev8[:, cols] = x[CH - 8:CH]
            cv_ref[:, cols] = conv.astype(bf16)
            xc = conv * _sigmoid(conv)
            xs = xc[:, 0:GW]
            bg = xc[:, GW:GW + NST].astype(bf16)
            cg = xc[:, GW + NST:GB].astype(bf16)
            back = lax.rem(CH - HPG * g, CH)
            dt = pltpu.roll(dt_all, back, 1)
            acs = pltpu.roll(acs_all, back, 1)
            cbm = _dot_nt(cg, bg)
            dkc_g = dkc_ref[:, gcols]
            y_parts = []
            for r in range(HPG):
                colb = jnp.broadcast_to(acs[:, r:r + 1], (CH, CH))
                row = acst_s[pl.ds(g * HPG + r, 1), :]
                lmat = jnp.exp(jnp.where(mask, colb - row, -jnp.inf))
                xr = xs[:, r * HD:(r + 1) * HD]
                xd = xr * dt[:, r:r + 1]
                sp = state[g * HPG + r]
                col = colb[:, 0:HD]
                alast = colb[CH - 1:CH, 0:HD]
                y_r = _dot((cbm * lmat).astype(bf16), xd.astype(bf16))
                y_r = y_r + jnp.exp(col) * _dot_nt(cg, sp.astype(bf16))
                y_parts.append(y_r + xr * dkc_g[:, r * HD:(r + 1) * HD])
                cs = _dot_tn((xd * jnp.exp(alast - col)).astype(bf16), bg)
                state[g * HPG + r] = jnp.exp(colb[CH - 1:CH, :]) * sp + cs
            y = jnp.concatenate(y_parts, axis=1)
            szv = sz_ref[:, gcols]
            yz = y * (szv * _sigmoid(szv))
            rr = lax.rsqrt(jnp.mean(yz * yz, axis=-1, keepdims=True) + EPS)
            yb_ref[:, gcols] = (yz * rr * snw_ref[:, gcols]).astype(bf16)
            y_ref[:, gcols] = y.astype(bf16)
            return carry

        lax.fori_loop(0, NG, group, 0)

    const2 = lambda c: (0, 0)
    return pl.pallas_call(
        body, grid=(nc,),
        in_specs=[pl.BlockSpec((CH, CONVD), lambda c: (c, 0)), pl.BlockSpec((CH, DIN), lambda c: (c, CONVD // DIN)),
                  pl.BlockSpec((CH, 256), lambda c: (c, (CONVD + DIN) // 256)),
                  pl.BlockSpec((4, CONVD), const2), pl.BlockSpec((1, CONVD), const2),
                  pl.BlockSpec((1, CH), const2), pl.BlockSpec((1, CH), const2),
                  pl.BlockSpec((1, DIN), const2), pl.BlockSpec((1, DIN), const2)],
        out_specs=[pl.BlockSpec((CH, DIN), lambda c: (c, 0)), pl.BlockSpec((CH, DIN), lambda c: (c, 0)),
                   pl.BlockSpec((1, NH, HD, NST), lambda c: (c, 0, 0, 0)), pl.BlockSpec((CH, CONVD), lambda c: (c, 0))],
        out_shape=[jax.ShapeDtypeStruct((s, DIN), bf16), jax.ShapeDtypeStruct((s, DIN), bf16),
                   jax.ShapeDtypeStruct((nc, NH, HD, NST), bf16), jax.ShapeDtypeStruct((s, CONVD), bf16)],
        scratch_shapes=[pltpu.VMEM((8, CONVD), f32), pltpu.VMEM((NH, HD, NST), f32), pltpu.VMEM((CH, CH), f32)],
        name=name, compiler_params=_cp(("arbitrary",)),
    )(proj_b, proj_b, proj_b, cw, cb, dtb, a_row, dkc, snw)


def _branch_b_bwd(proj_b, conv_sv, dyb, y_sv, states, cw, dtb, a_row, dkc, snw, ind, name):
    s = proj_b.shape[0]
    nc = s // CH

    def body(xbc_ref, cv_ref, sz_ref, dtr_ref, dyb_ref, y_ref, st_ref, cw_ref, dtb_ref, a_ref, dkc_ref,
             snw_ref, ind_ref, dp_ref, dcw_ref, dcb_ref, ddtb_ref, dal_ref, ddk_ref, dsnw_ref,
             dstate, dnext8, acst_s, dacs_acc, q2_acc):
        @pl.when(pl.program_id(0) == 0)
        def _():
            dstate[...] = jnp.zeros_like(dstate)
            dnext8[...] = jnp.zeros_like(dnext8)
            dcw_ref[...] = jnp.zeros_like(dcw_ref)
            dcb_ref[...] = jnp.zeros_like(dcb_ref)
            ddtb_ref[...] = jnp.zeros_like(ddtb_ref)
            dal_ref[...] = jnp.zeros_like(dal_ref)
            ddk_ref[...] = jnp.zeros_like(ddk_ref)
            dsnw_ref[...] = jnp.zeros_like(dsnw_ref)

        dacs_acc[...] = jnp.zeros_like(dacs_acc)
        q2_acc[...] = jnp.zeros_like(q2_acc)
        mask = _tril_mask()
        tri_t = (lax.broadcasted_iota(jnp.int32, (CH, CH), 0) <= lax.broadcasted_iota(jnp.int32, (CH, CH), 1)).astype(f32)
        lane1 = lax.broadcasted_iota(jnp.int32, (1, CH), 1)
        is_last = lax.broadcasted_iota(jnp.int32, (CH, 1), 0) == CH - 1
        z_all = dtr_ref[:, 0:CH] + dtb_ref[...]
        dt_all = _softplus(z_all)
        adt_all = dt_all * a_ref[...]
        acs_all = _dot_hi(mask.astype(f32), adt_all)
        acst_s[...] = acs_all.T

        def ind_sum(v):
            hi = v.astype(bf16)
            lo = (v - hi.astype(f32)).astype(bf16)
            return _dot(hi, ind_ref[...]) + _dot(lo, ind_ref[...])

        def group(g, carry):
            cols = _lanes(g, GB)
            gcols = _lanes(g, GW)
            conv = cv_ref[:, cols].astype(f32)
            sg = _sigmoid(conv)
            xc = conv * sg
            xs = xc[:, 0:GW]
            bg = xc[:, GW:GW + NST].astype(bf16)
            cg = xc[:, GW + NST:GB].astype(bf16)

            y = y_ref[:, gcols].astype(f32)
            silu_sz, dsilu_sz = _silu_and_grad(sz_ref[:, gcols])
            yz = y * silu_sz
            rr = lax.rsqrt(jnp.mean(yz * yz, axis=-1, keepdims=True) + EPS)
            dyb_g = dyb_ref[:, gcols].astype(f32)
            w = dyb_g * snw_ref[:, gcols]
            dsnw_ref[:, gcols] += jnp.sum(dyb_g * yz * rr, axis=0, keepdims=True)
            dyz = rr * w - yz * (rr * rr * rr) * jnp.mean(w * yz, axis=-1, keepdims=True)
            dp_ref[:, _lanes(g, GW, CONVD)] = (dyz * y * dsilu_sz).astype(bf16)
            dy_g = dyz * silu_sz
            ddk_ref[:, gcols] += jnp.sum(dy_g * xs, axis=0, keepdims=True)

            back = lax.rem(CH - HPG * g, CH)
            dt = pltpu.roll(dt_all, back, 1)
            acs = pltpu.roll(acs_all, back, 1)
            cbm = _dot_nt(cg, bg)
            d_cb = jnp.zeros((CH, CH), f32)
            d_bg = jnp.zeros((CH, NST), f32)
            d_cg = jnp.zeros((CH, NST), f32)
            lastrow = jnp.zeros((1, CH), f32)
            dxd_parts, dxs_parts, t_parts = [], [], []
            for r in range(HPG):
                h = g * HPG + r
                colb = jnp.broadcast_to(acs[:, r:r + 1], (CH, CH))
                row = acst_s[pl.ds(h, 1), :]
                lmat = jnp.exp(jnp.where(mask, colb - row, -jnp.inf))
                mmat_b = (cbm * lmat).astype(bf16)
                dtc = dt[:, r:r + 1]
                xd = xs[:, r * HD:(r + 1) * HD] * dtc
                col = colb[:, 0:HD]
                alast = colb[CH - 1:CH, 0:HD]
                dte = jnp.exp(alast - col)
                ea = jnp.exp(col)
                cd = jnp.exp(colb[CH - 1:CH, :])
                sp = st_ref[0, h]
                dsn = dstate[h]
                dsn_b = dsn.astype(bf16)
                dyr = dy_g[:, r * HD:(r + 1) * HD]
                dyr_b = dyr.astype(bf16)
                dye_b = (dyr * ea).astype(bf16)
                d_cg = d_cg + _dot(dye_b, sp)
                dxde = _dot_nt(bg, dsn_b)
                xdte = xd * dte
                xd_b = xd.astype(bf16)
                d_bg = d_bg + _dot(xdte.astype(bf16), dsn_b)
                dxd_diag = _dot_tn(mmat_b, dyr_b)
                dxd = dxde * dte + dxd_diag
                d_cb = d_cb + _dot_nt(dyr_b, xd_b) * lmat
                t_parts.append(dyr_b.astype(f32) * _dot(mmat_b, xd_b) + dyr * (ea * _dot_nt(cg, sp))
                               - xd_b.astype(f32) * dxd_diag - dxde * xdte)
                lastrow = jnp.where(lane1 == r, _sum_all(dsn * sp.astype(f32)) * cd + _sum_all(dxde * xdte), lastrow)
                dstate[h] = cd * dsn + _dot_tn(dye_b, cg)
                dxd_parts.append(dxd)
                dxs_parts.append(dxd * dtc)
            d_cb_b = d_cb.astype(bf16)
            d_bg = d_bg + _dot_tn(d_cb_b, cg)
            d_cg = d_cg + _dot(d_cb_b, bg)
            q2 = ind_sum(jnp.concatenate(dxd_parts, axis=1) * xs)
            dacs = ind_sum(jnp.concatenate(t_parts, axis=1)) + jnp.where(is_last, lastrow, 0.0)
            dacs_acc[...] += pltpu.roll(dacs, HPG * g, 1)
            q2_acc[...] += pltpu.roll(q2, HPG * g, 1)
            dxs = jnp.concatenate(dxs_parts, axis=1) + dy_g * dkc_ref[:, gcols]

            dconv = jnp.concatenate([dxs, d_bg, d_cg], axis=1) * (sg * (1.0 + conv * (1.0 - sg)))
            x = xbc_ref[:, cols]
            dcb_ref[:, cols] += jnp.sum(dconv, axis=0, keepdims=True)
            dcw_ref[3:4, cols] += jnp.sum(dconv * x, axis=0, keepdims=True)
            dx = cw_ref[3:4, cols] * dconv
            nxt = dnext8[:, cols]
            for j in (1, 2, 3):
                up = _shift_rows_up(dconv, nxt, j)
                dcw_ref[3 - j:4 - j, cols] += jnp.sum(up * x, axis=0, keepdims=True)
                dx = dx + cw_ref[3 - j:4 - j, cols] * up
            dnext8[:, cols] = dconv[0:8]
            dp_ref[:, cols] = dx.astype(bf16)
            return carry

        lax.fori_loop(0, NG, group, 0)

        dadt = _dot_hi(tri_t, dacs_acc[...])
        dal_ref[...] += jnp.sum(dadt * adt_all, axis=0, keepdims=True)
        ddz = (dadt * a_ref[...] + q2_acc[...]) * _sigmoid(z_all)
        ddtb_ref[...] += jnp.sum(ddz, axis=0, keepdims=True)
        dp_ref[:, CONVD + DIN:CONVD + DIN + CH] = ddz.astype(bf16)
        dp_ref[:, CONVD + DIN + CH:PB] = jnp.zeros((CH, PB - CONVD - DIN - CH), bf16)

    const2 = lambda c: (0, 0)
    rev = lambda c: (nc - 1 - c, 0)
    return pl.pallas_call(
        body, grid=(nc,),
        in_specs=[pl.BlockSpec((CH, CONVD), rev), pl.BlockSpec((CH, CONVD), rev),
                  pl.BlockSpec((CH, DIN), lambda c: (nc - 1 - c, CONVD // DIN)),
                  pl.BlockSpec((CH, 256), lambda c: (nc - 1 - c, (CONVD + DIN) // 256)),
                  pl.BlockSpec((CH, DIN), rev), pl.BlockSpec((CH, DIN), rev),
                  pl.BlockSpec((1, NH, HD, NST), lambda c: (nc - 1 - c, 0, 0, 0)),
                  pl.BlockSpec((4, CONVD), const2), pl.BlockSpec((1, CH), const2), pl.BlockSpec((1, CH), const2),
                  pl.BlockSpec((1, DIN), const2), pl.BlockSpec((1, DIN), const2), pl.BlockSpec((GW, CH), const2)],
        out_specs=[pl.BlockSpec((CH, PB), rev), pl.BlockSpec((4, CONVD), const2), pl.BlockSpec((1, CONVD), const2),
                   pl.BlockSpec((1, CH), const2), pl.BlockSpec((1, CH), const2), pl.BlockSpec((1, DIN), const2),
                   pl.BlockSpec((1, DIN), const2)],
        out_shape=[jax.ShapeDtypeStruct((s, PB), bf16), jax.ShapeDtypeStruct((4, CONVD), f32),
                   jax.ShapeDtypeStruct((1, CONVD), f32), jax.ShapeDtypeStruct((1, CH), f32),
                   jax.ShapeDtypeStruct((1, CH), f32), jax.ShapeDtypeStruct((1, DIN), f32),
                   jax.ShapeDtypeStruct((1, DIN), f32)],
        scratch_shapes=[pltpu.VMEM((NH, HD, NST), f32), pltpu.VMEM((8, CONVD), f32), pltpu.VMEM((CH, CH), f32),
                        pltpu.VMEM((CH, CH), f32), pltpu.VMEM((CH, CH), f32)],
        name=name, compiler_params=_cp(("arbitrary",)),
    )(proj_b, conv_sv, proj_b, proj_b, dyb, y_sv, states, cw, dtb, a_row, dkc, snw, ind)


def _merge_fwd(ya, yb, proj_g, x, gate, wa, wb, wo, name):
    s = x.shape[0]
    ts = min(512, s)

    def body(ya_ref, yb_ref, ga_ref, gb_ref, x_ref, gate_ref, wa_ref, wb_ref, wo_ref, xo_ref, pa_ref, pb_ref, mg_ref, o_ref):
        pa = _dot(ya_ref[...], wa_ref[...])
        pb = _dot(yb_ref[...], wb_ref[...])
        mg = (_sigmoid(ga_ref[...]) * pa + _sigmoid(gb_ref[...]) * pb).astype(bf16)
        o = _dot(mg, wo_ref[...])
        xo_ref[...] = x_ref[...] + gate_ref[...] * o
        pa_ref[...] = pa.astype(bf16)
        pb_ref[...] = pb.astype(bf16)
        mg_ref[...] = mg
        o_ref[...] = o.astype(bf16)

    tile = pl.BlockSpec((ts, D), lambda i: (i, 0))
    const = lambda i: (0, 0)
    act = jax.ShapeDtypeStruct((s, D), bf16)
    return pl.pallas_call(
        body, grid=(s // ts,),
        in_specs=[tile, pl.BlockSpec((ts, DIN), lambda i: (i, 0)), tile, pl.BlockSpec((ts, D), lambda i: (i, 1)), tile,
                  pl.BlockSpec((1, D), const), pl.BlockSpec((D, D), const), pl.BlockSpec((DIN, D), const),
                  pl.BlockSpec((D, D), const)],
        out_specs=[tile, tile, tile, tile, tile],
        out_shape=[jax.ShapeDtypeStruct((s, D), f32), act, act, act, act],
        name=name, compiler_params=_cp(("parallel",)),
    )(ya, yb, proj_g, proj_g, x, gate, wa, wb, wo)


def _merge_bwd(dxo, gate, o_sv, pa_sv, pb_sv, proj_g, wo_t, wa_t, wb_t, name):
    s = dxo.shape[0]
    ts = min(512, s)

    def body(dxo_ref, gate_ref, o_ref, pa_ref, pb_ref, ga_ref, gb_ref, wot_ref, wat_ref, wbt_ref,
             do_ref, dpa_ref, dpb_ref, dg_ref, dya_ref, dyb_ref, dgate_ref):
        @pl.when(pl.program_id(0) == 0)
        def _():
            dgate_ref[...] = jnp.zeros_like(dgate_ref)
        dxo_v = dxo_ref[...]
        dgate_ref[...] += jnp.sum(dxo_v * o_ref[...].astype(f32), axis=0, keepdims=True)
        do = (dxo_v * gate_ref[...]).astype(bf16)
        do_ref[...] = do
        dmg = _dot(do, wot_ref[...])
        sa = _sigmoid(ga_ref[...])
        sb = _sigmoid(gb_ref[...])
        dpa = (dmg * sa).astype(bf16)
        dpb = (dmg * sb).astype(bf16)
        dpa_ref[...] = dpa
        dpb_ref[...] = dpb
        dg_ref[:, 0:D] = (dmg * pa_ref[...].astype(f32) * sa * (1.0 - sa)).astype(bf16)
        dg_ref[:, D:2 * D] = (dmg * pb_ref[...].astype(f32) * sb * (1.0 - sb)).astype(bf16)
        dya_ref[...] = _dot(dpa, wat_ref[...]).astype(bf16)
        dyb_ref[...] = _dot(dpb, wbt_ref[...]).astype(bf16)

    tile = pl.BlockSpec((ts, D), lambda i: (i, 0))
    const = lambda i: (0, 0)
    act = jax.ShapeDtypeStruct((s, D), bf16)
    return pl.pallas_call(
        body, grid=(s // ts,),
        in_specs=[tile, pl.BlockSpec((1, D), const), tile, tile, tile, tile, pl.BlockSpec((ts, D), lambda i: (i, 1)),
                  pl.BlockSpec((D, D), const), pl.BlockSpec((D, D), const), pl.BlockSpec((D, DIN), const)],
        out_specs=[tile, tile, tile, pl.BlockSpec((ts, PG), lambda i: (i, 0)), tile, pl.BlockSpec((ts, DIN), lambda i: (i, 0)),
                   pl.BlockSpec((1, D), const)],
        out_shape=[act, act, act, jax.ShapeDtypeStruct((s, PG), bf16), act, jax.ShapeDtypeStruct((s, DIN), bf16),
                   jax.ShapeDtypeStruct((1, D), f32)],
        name=name, compiler_params=_cp(("arbitrary",)),
    )(dxo, gate, o_sv, pa_sv, pb_sv, proj_g, proj_g, wo_t, wa_t, wb_t)


def _final_loss(x, target, fnw, name):
    s = x.shape[0]
    ts = min(512, s)

    def body(x_ref, t_ref, w_ref, loss_ref, dx_ref, dw_ref):
        @pl.when(pl.program_id(0) == 0)
        def _():
            loss_ref[...] = jnp.zeros_like(loss_ref)
            dw_ref[...] = jnp.zeros_like(dw_ref)
        xv = x_ref[...]
        r = lax.rsqrt(jnp.mean(xv * xv, axis=-1, keepdims=True) + EPS)
        xn = xv * r
        err = xn * w_ref[...] - t_ref[...]
        part = jnp.sum(err * err, axis=0, keepdims=True)
        acc = part[:, 0:128]
        for k in range(1, D // 128):
            acc = acc + part[:, k * 128:(k + 1) * 128]
        loss_ref[0:1, :] += acc * (0.5 / D)
        dy = err * (1.0 / D)
        dw_ref[...] += jnp.sum(dy * xn, axis=0, keepdims=True)
        dxn = dy * w_ref[...]
        dx_ref[...] = r * (dxn - xn * jnp.mean(dxn * xn, axis=-1, keepdims=True))

    tile = pl.BlockSpec((ts, D), lambda i: (i, 0))
    row = pl.BlockSpec((1, D), lambda i: (0, 0))
    return pl.pallas_call(
        body, grid=(s // ts,), in_specs=[tile, tile, row],
        out_specs=[pl.BlockSpec((8, 128), lambda i: (0, 0)), tile, row],
        out_shape=[jax.ShapeDtypeStruct((8, 128), f32), jax.ShapeDtypeStruct((s, D), f32), jax.ShapeDtypeStruct((1, D), f32)],
        name=name, compiler_params=_cp(("arbitrary",)),
    )(x, target, fnw)


def _layer_operands(w_in, conv_w, wa, wb, wo, norm_w, gm_ln_w, gm_ln_b, gm_ws, gm_bs, conv_b, dt_bias, a_log, d_skip, ssm_norm_w):
    w_xbc = _group_major(w_in[:, 5120:7168], w_in[:, 7168:8192], w_in[:, 8192:9216])
    w_b = jnp.concatenate([w_xbc, w_in[:, 3072:5120], w_in[:, 9216:9248], jnp.zeros((D, 224), bf16)], axis=1)
    w_a = w_in[:, 0:3072]
    w_g = w_in[:, 9248:N_IN]
    tril = jnp.tril(jnp.ones((CH, CH), bool))
    wsm = jnp.where(tril[None], gm_ws, 0.0).astype(bf16)

    def heads_row(v):
        return jnp.pad(v, (0, CH - NH)).reshape(1, CH)

    return dict(
        w_b=w_b, w_a=w_a, w_g=w_g, w_b_t=w_b.T, w_a_t=w_a.T, w_g_t=w_g.T,
        wa=wa, wb=wb, wo=wo, wa_t=wa.T, wb_t=wb.T, wo_t=wo.T,
        norm_w=norm_w.reshape(1, D), lnw=gm_ln_w.reshape(1, D), lnb=gm_ln_b.reshape(1, D),
        wsm=wsm, wsm_t=jnp.swapaxes(wsm, 1, 2), bsf=jnp.repeat(gm_bs.T, CH, axis=1),
        cw=_group_major(conv_w[:, 0:DIN], conv_w[:, DIN:DIN + NG * NST], conv_w[:, DIN + NG * NST:CONVD]),
        cb=_group_major(conv_b[0:DIN], conv_b[DIN:DIN + NG * NST], conv_b[DIN + NG * NST:CONVD]).reshape(1, CONVD),
        dtb=heads_row(dt_bias), a_row=heads_row(-jnp.exp(a_log)),
        snw=ssm_norm_w.reshape(1, DIN), dkc=jnp.repeat(d_skip, HD).reshape(1, DIN),
        ind=(jnp.arange(GW)[:, None] // HD == jnp.arange(CH)[None, :]).astype(bf16),
    )


def _layer_fwd(x, shift, scale, gate, p, tag):
    h = _rmsmod_fwd(x, p["norm_w"], scale, shift, f"rmsmod_fwd{tag}")
    proj_b = _mm(h, p["w_b"], f32, f"proj_b{tag}")
    proj_a = _mm(h, p["w_a"], f32, f"proj_a{tag}")
    proj_g = _mm(h, p["w_g"], f32, f"proj_g{tag}")
    ya = _branch_a_fwd(proj_a, p["lnw"], p["lnb"], p["wsm"], p["bsf"], f"branch_a_fwd{tag}")
    yb, y_sv, states, conv_sv = _branch_b_fwd(proj_b, p["cw"], p["cb"], p["dtb"], p["a_row"], p["dkc"], p["snw"], f"branch_b_fwd{tag}")
    x_out, pa, pb, mg, o = _merge_fwd(ya, yb, proj_g, x, gate, p["wa"], p["wb"], p["wo"], f"merge_fwd{tag}")
    saved = dict(x=x, h=h, proj_b=proj_b, proj_a=proj_a, proj_g=proj_g, ya=ya, yb=yb, y=y_sv, states=states, conv=conv_sv,
                 pa=pa, pb=pb, mg=mg, o=o, scale=scale, gate=gate)
    return x_out, saved


def _layer_bwd(dxo, sv, p, tag):
    do, dpa, dpb, dg, dya, dyb, dgate = _merge_bwd(dxo, sv["gate"], sv["o"], sv["pa"], sv["pb"], sv["proj_g"],
                                                   p["wo_t"], p["wa_t"], p["wb_t"], f"merge_bwd{tag}")
    d_wo = _mm_tn(sv["mg"], do, f"d_wo{tag}")
    d_wa = _mm_tn(sv["ya"], dpa, f"d_wa{tag}")
    d_wb = _mm_tn(sv["yb"], dpb, f"d_wb{tag}")
    da, dws, dbs, dlnw, dlnb = _branch_a_bwd(sv["proj_a"], dya, p["lnw"], p["lnb"], p["wsm"], p["wsm_t"], p["bsf"],
                                             f"branch_a_bwd{tag}")
    db, dcw, dcb, ddtb, dal, ddk, dsnw = _branch_b_bwd(sv["proj_b"], sv["conv"], dyb, sv["y"], sv["states"], p["cw"], p["dtb"],
                                                       p["a_row"], p["dkc"], p["snw"], p["ind"], f"branch_b_bwd{tag}")
    dh = _mm(db, p["w_b_t"], f32, f"dh_b{tag}")
    dh = _mm(da, p["w_a_t"], f32, f"dh_a{tag}", c_in=dh)
    dh = _mm(dg, p["w_g_t"], f32, f"dh_g{tag}", c_in=dh)
    d_w_b_t = _mm_tn(db, sv["h"], f"d_w_b{tag}")
    d_w_a_t = _mm_tn(da, sv["h"], f"d_w_a{tag}")
    d_w_g_t = _mm_tn(dg, sv["h"], f"d_w_g{tag}")
    dx, dscale, dshift, dnw = _rmsmod_bwd(dh, sv["x"], dxo, p["norm_w"], sv["scale"], f"rmsmod_bwd{tag}")
    d_w_in_t = jnp.concatenate([d_w_a_t, d_w_b_t[CONVD:CONVD + DIN], _rows_from_group_major(d_w_b_t[0:CONVD]),
                                d_w_b_t[CONVD + DIN:CONVD + DIN + NH], d_w_g_t], axis=0)
    tril = jnp.tril(jnp.ones((CH, CH), bool))
    heads = lambda v: v[0, 0:NH]
    grads = dict(
        w_in_t=d_w_in_t, w_proj_a=d_wa, w_proj_b=d_wb, w_out=d_wo, conv_w=_from_group_major(dcw), conv_b=_from_group_major(dcb).reshape(CONVD),
        norm_w=dnw.reshape(D), gm_ln_w=dlnw.reshape(D), gm_ln_b=dlnb.reshape(D),
        gm_ws=jnp.where(tril[None], dws, 0.0), gm_bs=dbs.reshape(CH, NG, CH).sum(-1).T,
        dt_bias=heads(ddtb), a_log=heads(dal), d_skip=ddk.reshape(NH, HD).sum(-1), ssm_norm_w=dsnw.reshape(DIN),
        mod=jnp.concatenate([dshift, dscale, dgate], axis=1).reshape(3 * D),
    )
    return dx, grads


def _local_step(x, target, mods, operands_of, fnw):
    saved, layer_ops = [], []
    for l in range(DEPTH):
        shift, scale, gate = mods[l]
        p, x = operands_of(l, x)
        layer_ops.append(p)
        x, sv = _layer_fwd(x, shift, scale, gate, p, f"_l{l}")
        saved.append(sv)
    loss_parts, dx, dfnw = _final_loss(x, target, fnw.reshape(1, D), "final_loss")
    grads = [None] * DEPTH
    for l in reversed(range(DEPTH)):
        dx, grads[l] = _layer_bwd(dx, saved[l], layer_ops[l], f"_l{l}")
    return loss_parts, dx, grads, dfnw.reshape(D)


ADA_COLS = 3 * D // NSHARD


def _ada_fwd(c_all, ada_w, ada_b_cols, name):
    def body(c_ref, w_ref, b_ref, o_ref):
        cv = c_ref[...]
        sc = cv * _sigmoid(cv)
        for l in range(DEPTH):
            o_ref[l] = _dot_hi(sc, w_ref[l]) + b_ref[l]

    return pl.pallas_call(body, out_shape=jax.ShapeDtypeStruct((DEPTH, 8, ADA_COLS), f32), name=name,
                          compiler_params=_cp(None))(c_all, ada_w, ada_b_cols)


def _adam_math(w, g, m, v):
    m = ADAM_B1 * m + (1.0 - ADAM_B1) * g
    v = ADAM_B2 * v + (1.0 - ADAM_B2) * (g * g)
    m_hat = m / (1.0 - ADAM_B1 ** ADAM_STEP)
    v_hat = v / (1.0 - ADAM_B2 ** ADAM_STEP)
    delta = -ADAM_LR * (m_hat / (jnp.sqrt(v_hat) + ADAM_EPS) + ADAM_WD * w)
    return delta, m, v


def _ada_bwd_adamw(c_all, dmod_cols, w, m, v, name):
    tr = 256

    def body(c_ref, dm_ref, w_ref, m_ref, v_ref, g_ref, d_ref, nm_ref, nv_ref):
        cv = c_ref[...]
        sc = cv * _sigmoid(cv)
        g = lax.dot_general(sc, dm_ref[0], (((0,), (0,)), ((), ())), precision=_HI, preferred_element_type=f32)
        g_ref[0] = g
        d_ref[0], nm_ref[0], nv_ref[0] = _adam_math(w_ref[0], g, m_ref[0], v_ref[0])

    blk = pl.BlockSpec((1, tr, ADA_COLS), lambda l, i: (l, i, 0))
    shp = jax.ShapeDtypeStruct((DEPTH, D, ADA_COLS), f32)
    return pl.pallas_call(
        body, grid=(DEPTH, D // tr),
        in_specs=[pl.BlockSpec((8, tr), lambda l, i: (0, i)), pl.BlockSpec((1, 8, ADA_COLS), lambda l, i: (l, 0, 0)), blk, blk, blk],
        out_specs=[blk, blk, blk, blk], out_shape=[shp, shp, shp, shp], name=name, compiler_params=_cp(("parallel", "parallel")),
    )(c_all, dmod_cols, w, m, v)


def _adamw(w, g, m, v, name):
    def body(w_ref, g_ref, m_ref, v_ref, d_ref, nm_ref, nv_ref):
        d_ref[...], nm_ref[...], nv_ref[...] = _adam_math(w_ref[...], g_ref[...], m_ref[...], v_ref[...])

    shp = jax.ShapeDtypeStruct(w.shape, f32)
    return pl.pallas_call(body, out_shape=[shp] * 3, name=name, compiler_params=_cp(None))(w, g, m, v)


def _tile2(r, c):
    if r <= 256 or r % 256 == 0:
        return _pick(r, (256,)), _pick(c, (1024,))
    return r, 128


def _adamw_layers(w, g_mine, g_other, m, v, c_idx, name):
    _, r, c = w.shape
    tr, tc = _tile2(r, c)

    def body(ci_ref, w_ref, gm_ref, go_ref, m_ref, v_ref, g_ref, d_ref, nm_ref, nv_ref):
        def update(g):
            g_ref[0] = g
            d_ref[0], nm_ref[0], nv_ref[0] = _adam_math(w_ref[0], g, m_ref[0], v_ref[0])

        mine = pl.program_id(0) == ci_ref[0]

        @pl.when(mine)
        def _():
            update(gm_ref[...])

        @pl.when(jnp.logical_not(mine))
        def _():
            update(go_ref[...])

    blk = pl.BlockSpec((1, tr, tc), lambda l, i, j, ci: (l, i, j))
    gblk = pl.BlockSpec((tr, tc), lambda l, i, j, ci: (i, j))
    shp = jax.ShapeDtypeStruct(w.shape, f32)
    return pl.pallas_call(
        body,
        grid_spec=pltpu.PrefetchScalarGridSpec(num_scalar_prefetch=1, grid=(DEPTH, r // tr, c // tc),
                                               in_specs=[blk, gblk, gblk, blk, blk], out_specs=[blk, blk, blk, blk]),
        out_shape=[shp, shp, shp, shp], name=name, compiler_params=_cp(("parallel", "parallel", "parallel")),
    )(c_idx, w, g_mine, g_other, m, v)


_MESH = pl.DeviceIdType.MESH
_AXES = ("x", "y", "c")
_HBM = pl.BlockSpec(memory_space=pltpu.HBM)


def _my_place():
    return tuple(lax.axis_index(a) for a in _AXES)


def _allreduce(buf, axes, name):
    r = buf.shape[0]
    n = len(axes)

    def body(x_ref, o_ref, rbuf, ssem, rsem):
        me = dict(zip(_AXES, _my_place()))
        o_ref[...] = x_ref[...]
        for k, ax in enumerate(axes):
            peer = tuple(1 - me[a] if a == ax else me[a] for a in _AXES)
            cp = pltpu.make_async_remote_copy(src_ref=o_ref, dst_ref=rbuf.at[k], send_sem=ssem.at[k], recv_sem=rsem.at[k],
                                              device_id=peer, device_id_type=_MESH)
            cp.start()
            cp.wait()
            o_ref[...] = o_ref[...] + rbuf[k]

    vm = pl.BlockSpec(memory_space=pltpu.VMEM)
    return pl.pallas_call(
        body, out_shape=jax.ShapeDtypeStruct((r, 128), f32), in_specs=[vm], out_specs=vm,
        scratch_shapes=[pltpu.VMEM((n, r, 128), f32), pltpu.SemaphoreType.DMA((n,)), pltpu.SemaphoreType.DMA((n,))],
        name=name, compiler_params=pltpu.CompilerParams(vmem_limit_bytes=VMEM_LIMIT),
    )(buf)


def _other_chips(x, y):
    return [(1 - x, y), (x, 1 - y), (1 - x, 1 - y)]


def _gather_body(ins, outs, ssem, rsem):
    na = len(ins)
    x, y, c = _my_place()
    k_me = 2 * x + y
    sibling = (x, y, 1 - c)
    chips = _other_chips(x, y)
    slots = [2 * cx + cy for cx, cy in chips]
    half = [r.shape[0] // 2 if r.shape[0] % 32 == 0 else None for r in ins]

    def part(ref, a, core):
        return ref if half[a] is None else ref.at[pl.ds(core * half[a], half[a])]

    def rcopy(a, src, slot, core, to, idx):
        return pltpu.make_async_remote_copy(src_ref=src, dst_ref=part(outs[a].at[slot], a, core), send_sem=ssem.at[idx],
                                            recv_sem=rsem.at[idx], device_id=to, device_id_type=_MESH)

    sent = []
    for j, chip in enumerate(chips):
        for a in range(na):
            cp = rcopy(a, part(ins[a], a, c), k_me, c, (*chip, c), j * na + a)
            cp.start()
            sent.append(cp)
    for j, chip in enumerate(chips):
        for a in range(na):
            rcopy(a, part(ins[a], a, c), slots[j], c, (*chip, c), j * na + a).wait_recv()
            if half[a] is not None:
                cp = rcopy(a, part(outs[a].at[slots[j]], a, c), slots[j], c, sibling, (3 + j) * na + a)
                cp.start()
                sent.append(cp)
    for j in range(3):
        for a in range(na):
            if half[a] is not None:
                rcopy(a, part(ins[a], a, c), slots[j], 1 - c, sibling, (3 + j) * na + a).wait_recv()
    for cp in sent:
        cp.wait_send()


def _gather_layer_behind(shards, name, collective_id):
    na = len(shards)
    hbm = pltpu.MemorySpace.HBM
    ins = [jax.new_ref(s, memory_space=hbm) for s in shards]
    outs = [jax.empty_ref(jax.ShapeDtypeStruct((NSHARD,) + s.shape, s.dtype), memory_space=hbm) for s in shards]

    @pl.kernel(mesh=plsc.ScalarSubcoreMesh(axis_name="sequencer", num_cores=1), name=name,
               scratch_types=(pltpu.SemaphoreType.DMA((6 * na,)), pltpu.SemaphoreType.DMA((6 * na,))),
               compiler_params=pltpu.CompilerParams(collective_id=collective_id))
    def launch(ssem, rsem):
        x, y, c = _my_place()
        barrier = pltpu.get_barrier_semaphore()
        peers = [(*chip, c) for chip in _other_chips(x, y)] + [(x, y, 1 - c)]
        for peer in peers:
            pl.semaphore_signal(barrier, inc=1, device_id=peer, device_id_type=_MESH)
        pl.semaphore_wait(barrier, len(peers))
        _gather_body(ins, outs, ssem, rsem)

    launch()
    return [o[...] for o in outs]


def _with_own(gathered, own):
    xi, yi, _ = _my_place()
    whole = lax.dynamic_update_index_in_dim(gathered, own, 2 * xi + yi, 0)
    return [whole[k] for k in range(NSHARD)]


def _swap_layers(parts, name):
    na = len(parts)

    def body(*refs):
        ins, outs = refs[:2 * na], refs[2 * na:3 * na]
        ssem, rsem = refs[3 * na:]
        x, y, c = _my_place()

        def copy(a, layer):
            return pltpu.make_async_remote_copy(src_ref=ins[2 * a + layer], dst_ref=outs[a], send_sem=ssem.at[a], recv_sem=rsem.at[a],
                                                device_id=(x, y, 1 - c), device_id_type=_MESH)

        for layer in range(DEPTH):
            @pl.when(c == 1 - layer)
            def _():
                for a in range(na):
                    copy(a, layer).start()
        for a in range(na):
            copy(a, 0).wait()

    flat = [p for pair in parts for p in pair]
    return pl.pallas_call(
        body, out_shape=[jax.ShapeDtypeStruct(p0.shape, p0.dtype) for p0, _ in parts], in_specs=[_HBM] * (2 * na),
        out_specs=[_HBM] * na, scratch_shapes=[pltpu.SemaphoreType.DMA((na,)), pltpu.SemaphoreType.DMA((na,))], name=name,
    )(*flat)


def _scatter_shards(sums, name):
    na = len(sums)

    def body(*refs):
        ins, outs = refs[:na], refs[na:2 * na]
        ssem, rsem = refs[2 * na:]
        x, y, c = _my_place()
        cps = []
        for j, chip in enumerate(_other_chips(x, y)):
            kj = 2 * chip[0] + chip[1]
            for a in range(na):
                cps.append(pltpu.make_async_remote_copy(
                    src_ref=ins[a].at[kj], dst_ref=outs[a].at[j], send_sem=ssem.at[j * na + a], recv_sem=rsem.at[j * na + a],
                    device_id=(*chip, c), device_id_type=_MESH))
        for cp in cps:
            cp.start()
        for cp in cps:
            cp.wait()

    return pl.pallas_call(
        body, out_shape=[jax.ShapeDtypeStruct((3,) + p.shape[1:], p.dtype) for p in sums], in_specs=[_HBM] * na,
        out_specs=[_HBM] * na, scratch_shapes=[pltpu.SemaphoreType.DMA((3 * na,)), pltpu.SemaphoreType.DMA((3 * na,))], name=name,
    )(*sums)


def _share_layers(finals, name):
    na = len(finals)

    def body(*refs):
        ins, outs = refs[:na], refs[na:2 * na]
        ssem, rsem = refs[2 * na:]
        x, y, c = _my_place()
        cps = [pltpu.make_async_remote_copy(src_ref=ins[a], dst_ref=outs[a], send_sem=ssem.at[a], recv_sem=rsem.at[a],
                                            device_id=(x, y, 1 - c), device_id_type=_MESH) for a in range(na)]
        for cp in cps:
            cp.start()
        for cp in cps:
            cp.wait()

    return pl.pallas_call(
        body, out_shape=[jax.ShapeDtypeStruct(p.shape, p.dtype) for p in finals], in_specs=[_HBM] * na, out_specs=[_HBM] * na,
        scratch_shapes=[pltpu.SemaphoreType.DMA((na,)), pltpu.SemaphoreType.DMA((na,))], name=name,
    )(*finals)


def _add_own_layer(part0, part1, recv, c_idx, name):
    ns, r, c = recv.shape
    tr, tc = _tile2(r, c)

    def body(ci_ref, p0_ref, p1_ref, r_ref, o_ref, ob_ref):
        def add(p_ref):
            t = p_ref[...] + r_ref[...]
            o_ref[...] = t
            ob_ref[...] = t.astype(bf16)

        @pl.when(ci_ref[0] == 0)
        def _():
            add(p0_ref)

        @pl.when(ci_ref[0] == 1)
        def _():
            add(p1_ref)

    blk = pl.BlockSpec((1, tr, tc), lambda k, i, j, ci: (k, i, j))
    blk0 = pl.BlockSpec((1, tr, tc), lambda k, i, j, ci: (k * (1 - ci[0]), i * (1 - ci[0]), j * (1 - ci[0])))
    blk1 = pl.BlockSpec((1, tr, tc), lambda k, i, j, ci: (k * ci[0], i * ci[0], j * ci[0]))
    return pl.pallas_call(
        body,
        grid_spec=pltpu.PrefetchScalarGridSpec(num_scalar_prefetch=1, grid=(ns, r // tr, c // tc), in_specs=[blk0, blk1, blk],
                                               out_specs=[blk, blk]),
        out_shape=[jax.ShapeDtypeStruct((ns, r, c), f32), jax.ShapeDtypeStruct((ns, r, c), bf16)], name=name,
        compiler_params=_cp(("arbitrary", "arbitrary", "arbitrary")),
    )(c_idx, part0, part1, recv)


def _add_own_shard(sums, recv, k_idx, name):
    _, r, c = sums.shape
    tr, tc = _tile2(r, c)

    def body(ki_ref, s_ref, r_ref, o_ref):
        o_ref[...] = ((s_ref[0] + r_ref[0].astype(f32)) + r_ref[1].astype(f32)) + r_ref[2].astype(f32)

    return pl.pallas_call(
        body,
        grid_spec=pltpu.PrefetchScalarGridSpec(
            num_scalar_prefetch=1, grid=(r // tr, c // tc),
            in_specs=[pl.BlockSpec((1, tr, tc), lambda i, j, ki: (ki[0], i, j)), pl.BlockSpec((3, tr, tc), lambda i, j, ki: (0, i, j))],
            out_specs=pl.BlockSpec((tr, tc), lambda i, j, ki: (i, j))),
        out_shape=jax.ShapeDtypeStruct((r, c), f32), name=name, compiler_params=_cp(("parallel", "parallel")),
    )(k_idx, sums, recv)


def _reduce_scatter(parts, tag):
    x, y, c = _my_place()
    c_idx = jnp.reshape(c, (1,)).astype(jnp.int32)
    k_idx = jnp.reshape(2 * x + y, (1,)).astype(jnp.int32)
    na = len(parts)
    recv = _swap_layers(parts, f"rs_swap{tag}")
    sums = [_add_own_layer(parts[a][0], parts[a][1], recv[a], c_idx, f"rs_add_layer{tag}_{a}") for a in range(na)]
    recv = _scatter_shards([sb for _, sb in sums], f"rs_scatter{tag}")
    finals = [_add_own_shard(sums[a][0], recv[a], k_idx, f"rs_add_shard{tag}_{a}") for a in range(na)]
    return finals, _share_layers(finals, f"rs_share{tag}"), c_idx


_SMALL = [("ada_b", (DEPTH, 3 * D)), ("norm_w", (DEPTH, D)), ("gm_ln_w", (DEPTH, D)), ("gm_ln_b", (DEPTH, D)),
          ("gm_ws", (DEPTH, NG, CH, CH)), ("gm_bs", (DEPTH, NG, CH)), ("conv_b", (DEPTH, CONVD)), ("dt_bias", (DEPTH, NH)),
          ("a_log", (DEPTH, NH)), ("d_skip", (DEPTH, NH)), ("ssm_norm_w", (DEPTH, DIN)), ("final_norm_w", (D,))]


def _rows_of(shape):
    n = 1
    for d in shape:
        n *= d
    return -(-n // 1024) * 8


def _pack(arrays):
    rows = []
    for a in arrays:
        flat = a.reshape(-1)
        r = _rows_of(a.shape)
        rows.append(jnp.pad(flat, (0, r * 128 - flat.shape[0])).reshape(r, 128))
    return jnp.concatenate(rows, axis=0)


def _unpack(buf, shapes):
    out, at = [], 0
    for shp in shapes:
        r = _rows_of(shp)
        n = 1
        for d in shp:
            n *= d
        out.append(buf[at:at + r].reshape(-1)[:n].reshape(shp))
        at += r
    return out


def kernel(x, c, ada_w, ada_b, norm_w, w_in, gm_ln_w, gm_ln_b, gm_ws, gm_bs, conv_w, conv_b, dt_bias, a_log, d_skip, ssm_norm_w, w_proj_a, w_proj_b, w_out, final_norm_w, loss_target, m_ada_w, m_ada_b, m_norm_w, m_w_in, m_gm_ln_w, m_gm_ln_b, m_gm_ws, m_gm_bs, m_conv_w, m_conv_b, m_dt_bias, m_a_log, m_d_skip, m_ssm_norm_w, m_w_proj_a, m_w_proj_b, m_w_out, m_final_norm_w, v_ada_w, v_ada_b, v_norm_w, v_w_in, v_gm_ln_w, v_gm_ln_b, v_gm_ws, v_gm_bs, v_conv_w, v_conv_b, v_dt_bias, v_a_log, v_d_skip, v_ssm_norm_w, v_w_proj_a, v_w_proj_b, v_w_out, v_final_norm_w):
    xi, yi, ci = _my_place()
    k_me = 2 * xi + yi
    b_me = 4 * xi + 2 * yi + ci
    w = dict(ada_b=ada_b, norm_w=norm_w, gm_ln_w=gm_ln_w, gm_ln_b=gm_ln_b, gm_ws=gm_ws, gm_bs=gm_bs, conv_b=conv_b, dt_bias=dt_bias,
             a_log=a_log, d_skip=d_skip, ssm_norm_w=ssm_norm_w, final_norm_w=final_norm_w)
    m = dict(ada_b=m_ada_b, norm_w=m_norm_w, gm_ln_w=m_gm_ln_w, gm_ln_b=m_gm_ln_b, gm_ws=m_gm_ws, gm_bs=m_gm_bs, conv_b=m_conv_b,
             dt_bias=m_dt_bias, a_log=m_a_log, d_skip=m_d_skip, ssm_norm_w=m_ssm_norm_w, final_norm_w=m_final_norm_w)
    v = dict(ada_b=v_ada_b, norm_w=v_norm_w, gm_ln_w=v_gm_ln_w, gm_ln_b=v_gm_ln_b, gm_ws=v_gm_ws, gm_bs=v_gm_bs, conv_b=v_conv_b,
             dt_bias=v_dt_bias, a_log=v_a_log, d_skip=v_d_skip, ssm_norm_w=v_ssm_norm_w, final_norm_w=v_final_norm_w)

    c_slot = lax.dynamic_update_slice(jnp.zeros((8, D), f32), c, (b_me, 0))
    c_all = _allreduce(c_slot.reshape(64, 128), _AXES, "gather_c").reshape(8, D)
    ada_b_cols = lax.dynamic_slice(ada_b, (0, k_me * ADA_COLS), (DEPTH, ADA_COLS)).reshape(DEPTH, 1, ADA_COLS)
    mod_cols = _ada_fwd(c_all, ada_w, ada_b_cols, "ada_fwd")
    mod_slot = lax.dynamic_update_slice(jnp.zeros((DEPTH, 8, 3 * D), f32), mod_cols, (0, 0, k_me * ADA_COLS))
    mod_all = _allreduce(mod_slot.reshape(-1, 128), ("x", "y"), "gather_mod").reshape(DEPTH, 8, 3 * D)
    mod_me = lax.dynamic_slice(mod_all, (0, b_me, 0), (DEPTH, 1, 3 * D))
    mods = [(mod_me[l, :, 0:D], mod_me[l, :, D:2 * D], mod_me[l, :, 2 * D:3 * D]) for l in range(DEPTH)]

    rows_sh = jnp.concatenate([w_proj_a, w_proj_b, w_out], axis=1).astype(bf16)
    win_sh = w_in.astype(bf16)
    first = _gather_layer_behind([win_sh[0], rows_sh[0], conv_w[0]], "gather_l0", 1)
    first, mods, later = lax.optimization_barrier((first, mods, [win_sh[1], rows_sh[1], conv_w[1]]))
    others = [first, _gather_layer_behind(later, "gather_l1", 2)]

    def operands_of(l, x_in):
        gathered = others[l]
        if l == 1:
            gathered, x_in = lax.optimization_barrier((gathered, x_in))
        win_g, rows_g, conv_g = gathered
        rows_l = _with_own(rows_g, rows_sh[l])
        w_in_l = jnp.concatenate(_with_own(win_g, win_sh[l]), axis=1)
        wa_l = jnp.concatenate([t[0:256] for t in rows_l], axis=0)
        wb_l = jnp.concatenate([t[256:768] for t in rows_l], axis=0)
        wo_l = jnp.concatenate([t[768:1024] for t in rows_l], axis=0)
        cw_l = jnp.concatenate(_with_own(conv_g, conv_w[l]), axis=1)
        return _layer_operands(w_in_l, cw_l, wa_l, wb_l, wo_l, norm_w[l], gm_ln_w[l], gm_ln_b[l], gm_ws[l], gm_bs[l], conv_b[l],
                               dt_bias[l], a_log[l], d_skip[l], ssm_norm_w[l]), x_in

    loss_parts, dx, grads, dfnw = _local_step(x[0], loss_target[0], mods, operands_of, final_norm_w)

    s_in = N_IN // NSHARD
    tr_ = lambda t: jnp.swapaxes(t, 1, 2)
    g_in = [grads[l]["w_in_t"].reshape(NSHARD, s_in, D) for l in range(DEPTH)]
    g_rows = [jnp.concatenate([grads[l]["w_proj_a"].reshape(NSHARD, 256, D), grads[l]["w_proj_b"].reshape(NSHARD, 512, D),
                               grads[l]["w_out"].reshape(NSHARD, 256, D)], axis=1) for l in range(DEPTH)]
    g_conv = [grads[l]["conv_w"].reshape(4, NSHARD, D).transpose(1, 0, 2) for l in range(DEPTH)]
    (f_in, f_rows, f_conv), (o_in, o_rows, o_conv), c_idx = _reduce_scatter([g_in, g_rows, g_conv], "")
    gr_in, d_in, nm_in, nv_in = [tr_(t) for t in _adamw_layers(tr_(w_in), f_in, o_in, tr_(m_w_in), tr_(v_w_in), c_idx, "adamw_w_in")]
    cat = lambda a, b, c_: jnp.concatenate([a, b, c_], axis=1)
    gr_rows, d_rows, nm_rows, nv_rows = _adamw_layers(cat(w_proj_a, w_proj_b, w_out), f_rows, o_rows, cat(m_w_proj_a, m_w_proj_b, m_w_out),
                                                      cat(v_w_proj_a, v_w_proj_b, v_w_out), c_idx, "adamw_rows")
    gr_conv = jnp.where(ci == 0, jnp.stack([f_conv, o_conv]), jnp.stack([o_conv, f_conv]))
    split = lambda t: (t[:, 0:256], t[:, 256:768], t[:, 768:1024])

    dmod_slot = lax.dynamic_update_slice(jnp.zeros((DEPTH, 8, 3 * D), f32),
                                         jnp.stack([grads[l]["mod"] for l in range(DEPTH)]).reshape(DEPTH, 1, 3 * D), (0, b_me, 0))
    small_g = {n: (dfnw if n == "final_norm_w" else jnp.stack([grads[l]["mod" if n == "ada_b" else n] for l in range(DEPTH)]))
               for n, _ in _SMALL}
    packed = _allreduce(_pack([small_g[n] for n, _ in _SMALL] + [dmod_slot, loss_parts]), _AXES, "allreduce_small")
    n_small = sum(_rows_of(s) for _, s in _SMALL)
    n_dmod = _rows_of(dmod_slot.shape)
    g_small = packed[0:n_small]
    dmod_all = packed[n_small:n_small + n_dmod].reshape(DEPTH, 8, 3 * D)
    loss = jnp.sum(packed[n_small + n_dmod:])
    small_gw = jnp.concatenate([g_small, _pack([gr_conv])], axis=0)
    d_s, nm_s, nv_s = _adamw(_pack([w[n] for n, _ in _SMALL] + [conv_w]), small_gw,
                             _pack([m[n] for n, _ in _SMALL] + [m_conv_w]), _pack([v[n] for n, _ in _SMALL] + [v_conv_w]), "adamw_small")
    shapes = [s for _, s in _SMALL] + [conv_w.shape]
    names = [n for n, _ in _SMALL] + ["conv_w"]
    g_d = dict(zip(names, _unpack(small_gw, shapes)))
    d_d = dict(zip(names, _unpack(d_s, shapes)))
    nm_d = dict(zip(names, _unpack(nm_s, shapes)))
    nv_d = dict(zip(names, _unpack(nv_s, shapes)))

    dmod_cols = lax.dynamic_slice(dmod_all, (0, 0, k_me * ADA_COLS), (DEPTH, 8, ADA_COLS))
    g_ada, d_ada, nm_ada, nv_ada = _ada_bwd_adamw(c_all, dmod_cols, ada_w, m_ada_w, v_ada_w, "ada_bwd_adamw")

    def by_name(big, small):
        ga, gb, go = split(big[1])
        return dict(small, ada_w=big[2], w_in=big[0], w_proj_a=ga, w_proj_b=gb, w_out=go)

    order = ["ada_w", "ada_b", "norm_w", "w_in", "gm_ln_w", "gm_ln_b", "gm_ws", "gm_bs", "conv_w", "conv_b", "dt_bias", "a_log",
             "d_skip", "ssm_norm_w", "w_proj_a", "w_proj_b", "w_out", "final_norm_w"]
    outs = []
    for big, small in (((gr_in, gr_rows, g_ada), g_d), ((d_in, d_rows, d_ada), d_d), ((nm_in, nm_rows, nm_ada), nm_d),
                       ((nv_in, nv_rows, nv_ada), nv_d)):
        t = by_name(big, small)
        outs += [t[n] for n in order]
    return (loss, dx.reshape(1, -1, D), *outs)
```

```python
import jax
import jax.numpy as jnp
from jax import lax
from jax.experimental import pallas as pl
from jax.experimental.pallas import tpu as pltpu
from jax.experimental.pallas import tpu_sc as plsc

f32 = jnp.float32
bf16 = jnp.bfloat16

D = 1024
DEPTH = 2
EPS = 1e-6
CH = 128
NG = 8
HPG = 4
HD = 64
NH = NG * HPG
NST = 128
DIN = 2048
CONVD = 4096
GW = DIN // NG
PB = CONVD + DIN + 256
PA = 3 * D
PG = 2 * D
N_IN = 11296
NSHARD = 4
V7X_VMEM_BYTES = 64 * 2 ** 20
VMEM_LIMIT = V7X_VMEM_BYTES - 8 * 2 ** 20

ADAM_LR, ADAM_B1, ADAM_B2, ADAM_EPS, ADAM_WD, ADAM_STEP = 0.001, 0.9, 0.999, 1e-08, 0.01, 10

_HI = lax.Precision.HIGHEST


def _cp(sem):
    return pltpu.CompilerParams(dimension_semantics=sem, vmem_limit_bytes=VMEM_LIMIT)


def _sigmoid(x):
    return 0.5 * jnp.tanh(0.5 * x) + 0.5


def _silu_and_grad(x):
    s = _sigmoid(x)
    return x * s, s * (1.0 + x * (1.0 - s))


_GELU_K = 0.7978845608028654
_GELU_C = 0.044715


def _gelu_and_grad(x):
    x2 = x * x
    t = jnp.tanh(_GELU_K * (x + _GELU_C * x * x2))
    g = 0.5 * x * (1.0 + t)
    dg = 0.5 * (1.0 + t) + 0.5 * x * (1.0 - t * t) * _GELU_K * (1.0 + 3.0 * _GELU_C * x2)
    return g, dg


def _gelu(x):
    t = jnp.tanh(_GELU_K * (x + _GELU_C * x * x * x))
    return 0.5 * x * (1.0 + t)


def _softplus(x):
    return jnp.maximum(x, 0.0) + jnp.log(1.0 + jnp.exp(-jnp.abs(x)))


def _dot(a, b):
    return jnp.dot(a, b, preferred_element_type=f32)


def _dot_nt(a, b):
    return lax.dot_general(a, b, (((1,), (1,)), ((), ())), preferred_element_type=f32)


def _dot_tn(a, b):
    return lax.dot_general(a, b, (((0,), (0,)), ((), ())), preferred_element_type=f32)


def _dot_hi(a, b):
    return jnp.dot(a, b, precision=_HI, preferred_element_type=f32)


def _rmsmod_fwd(x, nw, scale, shift, name):
    s = x.shape[0]
    ts = min(512, s)

    def body(x_ref, nw_ref, sc_ref, sh_ref, h_ref):
        xv = x_ref[...]
        r = lax.rsqrt(jnp.mean(xv * xv, axis=-1, keepdims=True) + EPS)
        h_ref[...] = ((xv * r) * nw_ref[...] * (1.0 + sc_ref[...]) + sh_ref[...]).astype(bf16)

    row = pl.BlockSpec((1, D), lambda i: (0, 0))
    tile = pl.BlockSpec((ts, D), lambda i: (i, 0))
    return pl.pallas_call(
        body, grid=(s // ts,), in_specs=[tile, row, row, row], out_specs=tile,
        out_shape=jax.ShapeDtypeStruct((s, D), bf16), name=name, compiler_params=_cp(("parallel",)),
    )(x, nw, scale, shift)


def _rmsmod_bwd(dh, x, dres, nw, scale, name):
    s = x.shape[0]
    ts = min(512, s)

    def body(dh_ref, x_ref, dres_ref, nw_ref, sc_ref, dx_ref, dsc_ref, dsh_ref, dnw_ref):
        @pl.when(pl.program_id(0) == 0)
        def _():
            dsc_ref[...] = jnp.zeros_like(dsc_ref)
            dsh_ref[...] = jnp.zeros_like(dsh_ref)
            dnw_ref[...] = jnp.zeros_like(dnw_ref)
        xv = x_ref[...]
        dhv = dh_ref[...]
        r = lax.rsqrt(jnp.mean(xv * xv, axis=-1, keepdims=True) + EPS)
        xn = xv * r
        one_sc = 1.0 + sc_ref[...]
        dsc_ref[...] += jnp.sum(dhv * xn * nw_ref[...], axis=0, keepdims=True)
        dsh_ref[...] += jnp.sum(dhv, axis=0, keepdims=True)
        dnw_ref[...] += jnp.sum(dhv * xn * one_sc, axis=0, keepdims=True)
        dxn = dhv * (nw_ref[...] * one_sc)
        dx_ref[...] = r * (dxn - xn * jnp.mean(dxn * xn, axis=-1, keepdims=True)) + dres_ref[...]

    row = pl.BlockSpec((1, D), lambda i: (0, 0))
    tile = pl.BlockSpec((ts, D), lambda i: (i, 0))
    vec = jax.ShapeDtypeStruct((1, D), f32)
    return pl.pallas_call(
        body, grid=(s // ts,), in_specs=[tile, tile, tile, row, row], out_specs=[tile, row, row, row],
        out_shape=[jax.ShapeDtypeStruct((s, D), f32), vec, vec, vec], name=name, compiler_params=_cp(("arbitrary",)),
    )(dh, x, dres, nw, scale)


def _pick(n, prefs):
    for p in prefs:
        if n % p == 0:
            return p
    return n


def _mm(a, b, out_dtype, name, c_in=None, trans_b=False):
    m, k = a.shape
    n = b.shape[0] if trans_b else b.shape[1]
    tm = _pick(m, (1024, 512, 256))
    tn = _pick(n, (1280, 1024, 512))
    tk = _pick(k, (1280, 1024, 512))
    nk = k // tk

    def body(*refs):
        if c_in is not None:
            a_ref, b_ref, c_ref, o_ref, acc = refs
        else:
            a_ref, b_ref, o_ref, acc = refs
        kk = pl.program_id(2)

        @pl.when(kk == 0)
        def _():
            if c_in is not None:
                acc[...] = c_ref[...]
            else:
                acc[...] = jnp.zeros_like(acc)
        acc[...] += (_dot_nt if trans_b else _dot)(a_ref[...], b_ref[...])

        @pl.when(kk == nk - 1)
        def _():
            o_ref[...] = acc[...].astype(out_dtype)

    b_spec = pl.BlockSpec((tn, tk), lambda j, i, kk: (j, kk)) if trans_b else pl.BlockSpec((tk, tn), lambda j, i, kk: (kk, j))
    in_specs = [pl.BlockSpec((tm, tk), lambda j, i, kk: (i, kk)), b_spec]
    args = [a, b]
    if c_in is not None:
        in_specs.append(pl.BlockSpec((tm, tn), lambda j, i, kk: (i, j)))
        args.append(c_in)
    return pl.pallas_call(
        body, grid=(n // tn, m // tm, nk), in_specs=in_specs, out_specs=pl.BlockSpec((tm, tn), lambda j, i, kk: (i, j)),
        out_shape=jax.ShapeDtypeStruct((m, n), out_dtype), scratch_shapes=[pltpu.VMEM((tm, tn), f32)], name=name,
        compiler_params=_cp(("parallel", "parallel", "arbitrary")),
    )(*args)


def _mm_tn(a, b, name):
    t, k1 = a.shape
    n = b.shape[1]
    t1 = _pick(k1, (1280, 1024, 512))
    tn = _pick(n, (1280, 1024, 512))
    tt = _pick(t, (2048, 1024, 512, 256))
    nt = t // tt

    def body(a_ref, b_ref, o_ref):
        tt_i = pl.program_id(2)

        @pl.when(tt_i == 0)
        def _():
            o_ref[...] = jnp.zeros_like(o_ref)
        o_ref[...] += _dot_tn(a_ref[...], b_ref[...])

    return pl.pallas_call(
        body, grid=(k1 // t1, n // tn, nt),
        in_specs=[pl.BlockSpec((tt, t1), lambda i, j, tt_i: (tt_i, i)), pl.BlockSpec((tt, tn), lambda i, j, tt_i: (tt_i, j))],
        out_specs=pl.BlockSpec((t1, tn), lambda i, j, tt_i: (i, j)),
        out_shape=jax.ShapeDtypeStruct((k1, n), f32), name=name,
        compiler_params=_cp(("parallel", "parallel", "arbitrary")),
    )(a, b)


def _ln_stats(v):
    mu = jnp.mean(v, axis=-1, keepdims=True)
    vc = v - mu
    rstd = lax.rsqrt(jnp.mean(vc * vc, axis=-1, keepdims=True) + EPS)
    return vc * rstd, rstd


def _mix(w_ref, vl):
    return jnp.concatenate([_dot(w_ref[g], vl[:, g * CH:(g + 1) * CH]) for g in range(NG)], axis=1)


def _branch_a_fwd(proj_a, lnw, lnb, wsm, bsf, name):
    s = proj_a.shape[0]
    ta = min(256, s)

    def body(pu_ref, pv_ref, pz_ref, lnw_ref, lnb_ref, w_ref, bs_ref, ya_ref):
        for c in range(ta // CH):
            rows = pl.ds(c * CH, CH)
            vh, _ = _ln_stats(_gelu(pv_ref[rows, :]))
            vl = (vh * lnw_ref[...] + lnb_ref[...]).astype(bf16)
            mixed = _mix(w_ref, vl) + bs_ref[...]
            pz = pz_ref[rows, :]
            ya_ref[rows, :] = (_gelu(pu_ref[rows, :]) * mixed * (pz * _sigmoid(pz))).astype(bf16)

    row = pl.BlockSpec((1, D), lambda i: (0, 0))
    return pl.pallas_call(
        body, grid=(s // ta,),
        in_specs=[pl.BlockSpec((ta, D), lambda i: (i, 0)), pl.BlockSpec((ta, D), lambda i: (i, 1)),
                  pl.BlockSpec((ta, D), lambda i: (i, 2)), row, row,
                  pl.BlockSpec((NG, CH, CH), lambda i: (0, 0, 0)), pl.BlockSpec((CH, D), lambda i: (0, 0))],
        out_specs=pl.BlockSpec((ta, D), lambda i: (i, 0)),
        out_shape=jax.ShapeDtypeStruct((s, D), bf16), name=name, compiler_params=_cp(("parallel",)),
    )(proj_a, proj_a, proj_a, lnw, lnb, wsm, bsf)


def _branch_a_bwd(proj_a, dya, lnw, lnb, wsm, wsm_t, bsf, name):
    s = proj_a.shape[0]
    ta = min(256, s)

    def body(pu_ref, pv_ref, pz_ref, dya_ref, lnw_ref, lnb_ref, w_ref, wt_ref, bs_ref,
             dp_ref, dws_ref, dbs_ref, dlnw_ref, dlnb_ref):
        @pl.when(pl.program_id(0) == 0)
        def _():
            dws_ref[...] = jnp.zeros_like(dws_ref)
            dbs_ref[...] = jnp.zeros_like(dbs_ref)
            dlnw_ref[...] = jnp.zeros_like(dlnw_ref)
            dlnb_ref[...] = jnp.zeros_like(dlnb_ref)
        for c in range(ta // CH):
            rows = pl.ds(c * CH, CH)
            u, du = _gelu_and_grad(pu_ref[rows, :])
            v, dv_act = _gelu_and_grad(pv_ref[rows, :])
            zg, dzg = _silu_and_grad(pz_ref[rows, :])
            vh, rstd = _ln_stats(v)
            vl = (vh * lnw_ref[...] + lnb_ref[...]).astype(bf16)
            mixed = _mix(w_ref, vl) + bs_ref[...]
            dy = dya_ref[rows, :].astype(f32)
            dmixed = dy * u * zg
            dp_ref[rows, 0:D] = (dy * mixed * zg * du).astype(bf16)
            dp_ref[rows, 2 * D:3 * D] = (dy * u * mixed * dzg).astype(bf16)
            dmb = dmixed.astype(bf16)
            dbs_ref[...] += dmixed
            dvl = _mix(wt_ref, dmb)
            for g in range(NG):
                cols = slice(g * CH, (g + 1) * CH)
                dws_ref[g] += _dot_nt(dmb[:, cols], vl[:, cols])
            dlnw_ref[...] += jnp.sum(dvl * vh, axis=0, keepdims=True)
            dlnb_ref[...] += jnp.sum(dvl, axis=0, keepdims=True)
            dvh = dvl * lnw_ref[...]
            dv = rstd * (dvh - jnp.mean(dvh, axis=-1, keepdims=True) - vh * jnp.mean(dvh * vh, axis=-1, keepdims=True))
            dp_ref[rows, D:2 * D] = (dv * dv_act).astype(bf16)

    row = pl.BlockSpec((1, D), lambda i: (0, 0))
    wspec = pl.BlockSpec((NG, CH, CH), lambda i: (0, 0, 0))
    full = pl.BlockSpec((CH, D), lambda i: (0, 0))
    return pl.pallas_call(
        body, grid=(s // ta,),
        in_specs=[pl.BlockSpec((ta, D), lambda i: (i, 0)), pl.BlockSpec((ta, D), lambda i: (i, 1)),
                  pl.BlockSpec((ta, D), lambda i: (i, 2)), pl.BlockSpec((ta, D), lambda i: (i, 0)),
                  row, row, wspec, wspec, full],
        out_specs=[pl.BlockSpec((ta, PA), lambda i: (i, 0)), wspec, full, row, row],
        out_shape=[jax.ShapeDtypeStruct((s, PA), bf16), jax.ShapeDtypeStruct((NG, CH, CH), f32),
                   jax.ShapeDtypeStruct((CH, D), f32), jax.ShapeDtypeStruct((1, D), f32), jax.ShapeDtypeStruct((1, D), f32)],
        name=name, compiler_params=_cp(("arbitrary",)),
    )(proj_a, proj_a, proj_a, dya, lnw, lnb, wsm, wsm_t, bsf)


GB = GW + 2 * NST


def _group_major(xs, b, c):
    lead = xs.shape[:-1]
    return jnp.concatenate([xs.reshape(lead + (NG, GW)), b.reshape(lead + (NG, NST)), c.reshape(lead + (NG, NST))],
                           axis=-1).reshape(lead + (CONVD,))


def _from_group_major(t):
    lead = t.shape[:-1]
    t = t.reshape(lead + (NG, GB))
    return jnp.concatenate([t[..., 0:GW].reshape(lead + (DIN,)), t[..., GW:GW + NST].reshape(lead + (NG * NST,)),
                            t[..., GW + NST:GB].reshape(lead + (NG * NST,))], axis=-1)


def _rows_from_group_major(t):
    t = t.reshape(NG, GB, t.shape[-1])
    return jnp.concatenate([t[:, 0:GW].reshape(DIN, -1), t[:, GW:GW + NST].reshape(NG * NST, -1),
                            t[:, GW + NST:GB].reshape(NG * NST, -1)], axis=0)


def _shift_rows(x, prev8, j):
    xr = pltpu.roll(x, j, 0)
    fix = pltpu.roll(prev8, j, 0)
    rid = lax.broadcasted_iota(jnp.int32, (8, x.shape[1]), 0)
    top = jnp.where(rid < j, fix, xr[0:8])
    return jnp.concatenate([top, xr[8:]], axis=0)


def _shift_rows_up(d, next8, j):
    dr = pltpu.roll(d, CH - j, 0)
    fix = pltpu.roll(next8, 8 - j, 0)
    rid = lax.broadcasted_iota(jnp.int32, (8, d.shape[1]), 0)
    bot = jnp.where(rid >= 8 - j, fix, dr[CH - 8:CH])
    return jnp.concatenate([dr[0:CH - 8], bot], axis=0)


def _conv_pre(x, prev8, cw_ref, cb_ref, cols):
    shifted = [_shift_rows(x, prev8, j) for j in (1, 2, 3)]
    conv = cb_ref[:, cols] + cw_ref[3:4, cols] * x
    for j in (1, 2, 3):
        conv = conv + cw_ref[3 - j:4 - j, cols] * shifted[j - 1]
    return conv, shifted


def _tril_mask():
    return lax.broadcasted_iota(jnp.int32, (CH, CH), 0) >= lax.broadcasted_iota(jnp.int32, (CH, CH), 1)


def _sum_all(v):
    return jnp.sum(jnp.sum(v, axis=0, keepdims=True), axis=1, keepdims=True)


def _lanes(g, width, base=0):
    return pl.ds(pl.multiple_of(base + g * width, width), width)


def _branch_b_fwd(proj_b, cw, cb, dtb, a_row, dkc, snw, name):
    s = proj_b.shape[0]
    nc = s // CH

    def body(xbc_ref, sz_ref, dtr_ref, cw_ref, cb_ref, dtb_ref, a_ref, dkc_ref, snw_ref,
             yb_ref, y_ref, st_ref, cv_ref, prev8, state, acst_s):
        @pl.when(pl.program_id(0) == 0)
        def _():
            prev8[...] = jnp.zeros_like(prev8)
            state[...] = jnp.zeros_like(state)
        st_ref[0] = state[...].astype(bf16)
        mask = _tril_mask()
        dt_all = _softplus(dtr_ref[:, 0:CH] + dtb_ref[...])
        acs_all = _dot_hi(mask.astype(f32), dt_all * a_ref[...])
        acst_s[...] = acs_all.T

        def group(g, carry):
            cols = _lanes(g, GB)
            gcols = _lanes(g, GW)
            x = xbc_ref[:, cols]
            conv, _ = _conv_pre(x, prev8[:, cols], cw_ref, cb_ref, cols)
            prev8[:, cols] = x[CH - 8:CH]
            cv_ref[:, cols] = conv.astype(bf16)
            xc = conv * _sigmoid(conv)
            xs = xc[:, 0:GW]
            bg = xc[:, GW:GW + NST].astype(bf16)
            cg = xc[:, GW + NST:GB].astype(bf16)
            back = lax.rem(CH - HPG * g, CH)
            dt = pltpu.roll(dt_all, back, 1)
            acs = pltpu.roll(acs_all, back, 1)
            cbm = _dot_nt(cg, bg)
            dkc_g = dkc_ref[:, gcols]
            y_parts = []
            for r in range(HPG):
                colb = jnp.broadcast_to(acs[:, r:r + 1], (CH, CH))
                row = acst_s[pl.ds(g * HPG + r, 1), :]
                lmat = jnp.exp(jnp.where(mask, colb - row, -jnp.inf))
                xr = xs[:, r * HD:(r + 1) * HD]
                xd = xr * dt[:, r:r + 1]
                sp = state[g * HPG + r]
                col = colb[:, 0:HD]
                alast = colb[CH - 1:CH, 0:HD]
                y_r = _dot((cbm * lmat).astype(bf16), xd.astype(bf16))
                y_r = y_r + jnp.exp(col) * _dot_nt(cg, sp.astype(bf16))
                y_parts.append(y_r + xr * dkc_g[:, r * HD:(r + 1) * HD])
                cs = _dot_tn((xd * jnp.exp(alast - col)).astype(bf16), bg)
                state[g * HPG + r] = jnp.exp(colb[CH - 1:CH, :]) * sp + cs
            y = jnp.concatenate(y_parts, axis=1)
            szv = sz_ref[:, gcols]
            yz = y * (szv * _sigmoid(szv))
            rr = lax.rsqrt(jnp.mean(yz * yz, axis=-1, keepdims=True) + EPS)
            yb_ref[:, gcols] = (yz * rr * snw_ref[:, gcols]).astype(bf16)
            y_ref[:, gcols] = y.astype(bf16)
            return carry

        lax.fori_loop(0, NG, group, 0)

    const2 = lambda c: (0, 0)
    return pl.pallas_call(
        body, grid=(nc,),
        in_specs=[pl.BlockSpec((CH, CONVD), lambda c: (c, 0)), pl.BlockSpec((CH, DIN), lambda c: (c, CONVD // DIN)),
                  pl.BlockSpec((CH, 256), lambda c: (c, (CONVD + DIN) // 256)),
                  pl.BlockSpec((4, CONVD), const2), pl.BlockSpec((1, CONVD), const2),
                  pl.BlockSpec((1, CH), const2), pl.BlockSpec((1, CH), const2),
                  pl.BlockSpec((1, DIN), const2), pl.BlockSpec((1, DIN), const2)],
        out_specs=[pl.BlockSpec((CH, DIN), lambda c: (c, 0)), pl.BlockSpec((CH, DIN), lambda c: (c, 0)),
                   pl.BlockSpec((1, NH, HD, NST), lambda c: (c, 0, 0, 0)), pl.BlockSpec((CH, CONVD), lambda c: (c, 0))],
        out_shape=[jax.ShapeDtypeStruct((s, DIN), bf16), jax.ShapeDtypeStruct((s, DIN), bf16),
                   jax.ShapeDtypeStruct((nc, NH, HD, NST), bf16), jax.ShapeDtypeStruct((s, CONVD), bf16)],
        scratch_shapes=[pltpu.VMEM((8, CONVD), f32), pltpu.VMEM((NH, HD, NST), f32), pltpu.VMEM((CH, CH), f32)],
        name=name, compiler_params=_cp(("arbitrary",)),
    )(proj_b, proj_b, proj_b, cw, cb, dtb, a_row, dkc, snw)


def _branch_b_bwd(proj_b, conv_sv, dyb, y_sv, states, cw, dtb, a_row, dkc, snw, ind, name):
    s = proj_b.shape[0]
    nc = s // CH

    def body(xbc_ref, cv_ref, sz_ref, dtr_ref, dyb_ref, y_ref, st_ref, cw_ref, dtb_ref, a_ref, dkc_ref,
             snw_ref, ind_ref, dp_ref, dcw_ref, dcb_ref, ddtb_ref, dal_ref, ddk_ref, dsnw_ref,
             dstate, dnext8, acst_s, dacs_acc, q2_acc):
        @pl.when(pl.program_id(0) == 0)
        def _():
            dstate[...] = jnp.zeros_like(dstate)
            dnext8[...] = jnp.zeros_like(dnext8)
            dcw_ref[...] = jnp.zeros_like(dcw_ref)
            dcb_ref[...] = jnp.zeros_like(dcb_ref)
            ddtb_ref[...] = jnp.zeros_like(ddtb_ref)
            dal_ref[...] = jnp.zeros_like(dal_ref)
            ddk_ref[...] = jnp.zeros_like(ddk_ref)
            dsnw_ref[...] = jnp.zeros_like(dsnw_ref)

        dacs_acc[...] = jnp.zeros_like(dacs_acc)
        q2_acc[...] = jnp.zeros_like(q2_acc)
        mask = _tril_mask()
        tri_t = (lax.broadcasted_iota(jnp.int32, (CH, CH), 0) <= lax.broadcasted_iota(jnp.int32, (CH, CH), 1)).astype(f32)
        lane1 = lax.broadcasted_iota(jnp.int32, (1, CH), 1)
        is_last = lax.broadcasted_iota(jnp.int32, (CH, 1), 0) == CH - 1
        z_all = dtr_ref[:, 0:CH] + dtb_ref[...]
        dt_all = _softplus(z_all)
        adt_all = dt_all * a_ref[...]
        acs_all = _dot_hi(mask.astype(f32), adt_all)
        acst_s[...] = acs_all.T

        def ind_sum(v):
            hi = v.astype(bf16)
            lo = (v - hi.astype(f32)).astype(bf16)
            return _dot(hi, ind_ref[...]) + _dot(lo, ind_ref[...])

        def group(g, carry):
            cols = _lanes(g, GB)
            gcols = _lanes(g, GW)
            conv = cv_ref[:, cols].astype(f32)
            sg = _sigmoid(conv)
            xc = conv * sg
            xs = xc[:, 0:GW]
            bg = xc[:, GW:GW + NST].astype(bf16)
            cg = xc[:, GW + NST:GB].astype(bf16)

            y = y_ref[:, gcols].astype(f32)
            silu_sz, dsilu_sz = _silu_and_grad(sz_ref[:, gcols])
            yz = y * silu_sz
            rr = lax.rsqrt(jnp.mean(yz * yz, axis=-1, keepdims=True) + EPS)
            dyb_g = dyb_ref[:, gcols].astype(f32)
            w = dyb_g * snw_ref[:, gcols]
            dsnw_ref[:, gcols] += jnp.sum(dyb_g * yz * rr, axis=0, keepdims=True)
            dyz = rr * w - yz * (rr * rr * rr) * jnp.mean(w * yz, axis=-1, keepdims=True)
            dp_ref[:, _lanes(g, GW, CONVD)] = (dyz * y * dsilu_sz).astype(bf16)
            dy_g = dyz * silu_sz
            ddk_ref[:, gcols] += jnp.sum(dy_g * xs, axis=0, keepdims=True)

            back = lax.rem(CH - HPG * g, CH)
            dt = pltpu.roll(dt_all, back, 1)
            acs = pltpu.roll(acs_all, back, 1)
            cbm = _dot_nt(cg, bg)
            d_cb = jnp.zeros((CH, CH), f32)
            d_bg = jnp.zeros((CH, NST), f32)
            d_cg = jnp.zeros((CH, NST), f32)
            lastrow = jnp.zeros((1, CH), f32)
            dxd_parts, dxs_parts, t_parts = [], [], []
            for r in range(HPG):
                h = g * HPG + r
                colb = jnp.broadcast_to(acs[:, r:r + 1], (CH, CH))
                row = acst_s[pl.ds(h, 1), :]
                lmat = jnp.exp(jnp.where(mask, colb - row, -jnp.inf))
                mmat_b = (cbm * lmat).astype(bf16)
                dtc = dt[:, r:r + 1]
                xd = xs[:, r * HD:(r + 1) * HD] * dtc
                col = colb[:, 0:HD]
                alast = colb[CH - 1:CH, 0:HD]
                dte = jnp.exp(alast - col)
                ea = jnp.exp(col)
                cd = jnp.exp(colb[CH - 1:CH, :])
                sp = st_ref[0, h]
                dsn = dstate[h]
                dsn_b = dsn.astype(bf16)
                dyr = dy_g[:, r * HD:(r + 1) * HD]
                dyr_b = dyr.astype(bf16)
                dye_b = (dyr * ea).astype(bf16)
                d_cg = d_cg + _dot(dye_b, sp)
                dxde = _dot_nt(bg, dsn_b)
                xdte = xd * dte
                xd_b = xd.astype(bf16)
                d_bg = d_bg + _dot(xdte.astype(bf16), dsn_b)
                dxd_diag = _dot_tn(mmat_b, dyr_b)
                dxd = dxde * dte + dxd_diag
                d_cb = d_cb + _dot_nt(dyr_b, xd_b) * lmat
                t_parts.append(dyr_b.astype(f32) * _dot(mmat_b, xd_b) + dyr * (ea * _dot_nt(cg, sp))
                               - xd_b.astype(f32) * dxd_diag - dxde * xdte)
                lastrow = jnp.where(lane1 == r, _sum_all(dsn * sp.astype(f32)) * cd + _sum_all(dxde * xdte), lastrow)
                dstate[h] = cd * dsn + _dot_tn(dye_b, cg)
                dxd_parts.append(dxd)
                dxs_parts.append(dxd * dtc)
            d_cb_b = d_cb.astype(bf16)
            d_bg = d_bg + _dot_tn(d_cb_b, cg)
            d_cg = d_cg + _dot(d_cb_b, bg)
            q2 = ind_sum(jnp.concatenate(dxd_parts, axis=1) * xs)
            dacs = ind_sum(jnp.concatenate(t_parts, axis=1)) + jnp.where(is_last, lastrow, 0.0)
            dacs_acc[...] += pltpu.roll(dacs, HPG * g, 1)
            q2_acc[...] += pltpu.roll(q2, HPG * g, 1)
            dxs = jnp.concatenate(dxs_parts, axis=1) + dy_g * dkc_ref[:, gcols]

            dconv = jnp.concatenate([dxs, d_bg, d_cg], axis=1) * (sg * (1.0 + conv * (1.0 - sg)))
            x = xbc_ref[:, cols]
            dcb_ref[:, cols] += jnp.sum(dconv, axis=0, keepdims=True)
            dcw_ref[3:4, cols] += jnp.sum(dconv * x, axis=0, keepdims=True)
            dx = cw_ref[3:4, cols] * dconv
            nxt = dnext8[:, cols]
            for j in (1, 2, 3):
                up = _shift_rows_up(dconv, nxt, j)
                dcw_ref[3 - j:4 - j, cols] += jnp.sum(up * x, axis=0, keepdims=True)
                dx = dx + cw_ref[3 - j:4 - j, cols] * up
            dnext8[:, cols] = dconv[0:8]
            dp_ref[:, cols] = dx.astype(bf16)
            return carry

        lax.fori_loop(0, NG, group, 0)

        dadt = _dot_hi(tri_t, dacs_acc[...])
        dal_ref[...] += jnp.sum(dadt * adt_all, axis=0, keepdims=True)
        ddz = (dadt * a_ref[...] + q2_acc[...]) * _sigmoid(z_all)
        ddtb_ref[...] += jnp.sum(ddz, axis=0, keepdims=True)
        dp_ref[:, CONVD + DIN:CONVD + DIN + CH] = ddz.astype(bf16)
        dp_ref[:, CONVD + DIN + CH:PB] = jnp.zeros((CH, PB - CONVD - DIN - CH), bf16)

    const2 = lambda c: (0, 0)
    rev = lambda c: (nc - 1 - c, 0)
    return pl.pallas_call(
        body, grid=(nc,),
        in_specs=[pl.BlockSpec((CH, CONVD), rev), pl.BlockSpec((CH, CONVD), rev),
                  pl.BlockSpec((CH, DIN), lambda c: (nc - 1 - c, CONVD // DIN)),
                  pl.BlockSpec((CH, 256), lambda c: (nc - 1 - c, (CONVD + DIN) // 256)),
                  pl.BlockSpec((CH, DIN), rev), pl.BlockSpec((CH, DIN), rev),
                  pl.BlockSpec((1, NH, HD, NST), lambda c: (nc - 1 - c, 0, 0, 0)),
                  pl.BlockSpec((4, CONVD), const2), pl.BlockSpec((1, CH), const2), pl.BlockSpec((1, CH), const2),
                  pl.BlockSpec((1, DIN), const2), pl.BlockSpec((1, DIN), const2), pl.BlockSpec((GW, CH), const2)],
        out_specs=[pl.BlockSpec((CH, PB), rev), pl.BlockSpec((4, CONVD), const2), pl.BlockSpec((1, CONVD), const2),
                   pl.BlockSpec((1, CH), const2), pl.BlockSpec((1, CH), const2), pl.BlockSpec((1, DIN), const2),
                   pl.BlockSpec((1, DIN), const2)],
        out_shape=[jax.ShapeDtypeStruct((s, PB), bf16), jax.ShapeDtypeStruct((4, CONVD), f32),
                   jax.ShapeDtypeStruct((1, CONVD), f32), jax.ShapeDtypeStruct((1, CH), f32),
                   jax.ShapeDtypeStruct((1, CH), f32), jax.ShapeDtypeStruct((1, DIN), f32),
                   jax.ShapeDtypeStruct((1, DIN), f32)],
        scratch_shapes=[pltpu.VMEM((NH, HD, NST), f32), pltpu.VMEM((8, CONVD), f32), pltpu.VMEM((CH, CH), f32),
                        pltpu.VMEM((CH, CH), f32), pltpu.VMEM((CH, CH), f32)],
        name=name, compiler_params=_cp(("arbitrary",)),
    )(proj_b, conv_sv, proj_b, proj_b, dyb, y_sv, states, cw, dtb, a_row, dkc, snw, ind)


def _merge_fwd(ya, yb, proj_g, x, gate, wa, wb, wo, name):
    s = x.shape[0]
    ts = min(512, s)

    def body(ya_ref, yb_ref, ga_ref, gb_ref, x_ref, gate_ref, wa_ref, wb_ref, wo_ref, xo_ref, pa_ref, pb_ref, mg_ref, o_ref):
        pa = _dot(ya_ref[...], wa_ref[...])
        pb = _dot(yb_ref[...], wb_ref[...])
        mg = (_sigmoid(ga_ref[...]) * pa + _sigmoid(gb_ref[...]) * pb).astype(bf16)
        o = _dot(mg, wo_ref[...])
        xo_ref[...] = x_ref[...] + gate_ref[...] * o
        pa_ref[...] = pa.astype(bf16)
        pb_ref[...] = pb.astype(bf16)
        mg_ref[...] = mg
        o_ref[...] = o.astype(bf16)

    tile = pl.BlockSpec((ts, D), lambda i: (i, 0))
    const = lambda i: (0, 0)
    act = jax.ShapeDtypeStruct((s, D), bf16)
    return pl.pallas_call(
        body, grid=(s // ts,),
        in_specs=[tile, pl.BlockSpec((ts, DIN), lambda i: (i, 0)), tile, pl.BlockSpec((ts, D), lambda i: (i, 1)), tile,
                  pl.BlockSpec((1, D), const), pl.BlockSpec((D, D), const), pl.BlockSpec((DIN, D), const),
                  pl.BlockSpec((D, D), const)],
        out_specs=[tile, tile, tile, tile, tile],
        out_shape=[jax.ShapeDtypeStruct((s, D), f32), act, act, act, act],
        name=name, compiler_params=_cp(("parallel",)),
    )(ya, yb, proj_g, proj_g, x, gate, wa, wb, wo)


def _merge_bwd(dxo, gate, o_sv, pa_sv, pb_sv, proj_g, wo, wa, wb, name):
    s = dxo.shape[0]
    ts = min(512, s)

    def body(dxo_ref, gate_ref, o_ref, pa_ref, pb_ref, ga_ref, gb_ref, wo_ref, wa_ref, wb_ref,
             do_ref, dpa_ref, dpb_ref, dg_ref, dya_ref, dyb_ref, dgate_ref):
        @pl.when(pl.program_id(0) == 0)
        def _():
            dgate_ref[...] = jnp.zeros_like(dgate_ref)
        dxo_v = dxo_ref[...]
        dgate_ref[...] += jnp.sum(dxo_v * o_ref[...].astype(f32), axis=0, keepdims=True)
        do = (dxo_v * gate_ref[...]).astype(bf16)
        do_ref[...] = do
        dmg = _dot_nt(do, wo_ref[...])
        sa = _sigmoid(ga_ref[...])
        sb = _sigmoid(gb_ref[...])
        dpa = (dmg * sa).astype(bf16)
        dpb = (dmg * sb).astype(bf16)
        dpa_ref[...] = dpa
        dpb_ref[...] = dpb
        dg_ref[:, 0:D] = (dmg * pa_ref[...].astype(f32) * sa * (1.0 - sa)).astype(bf16)
        dg_ref[:, D:2 * D] = (dmg * pb_ref[...].astype(f32) * sb * (1.0 - sb)).astype(bf16)
        dya_ref[...] = _dot_nt(dpa, wa_ref[...]).astype(bf16)
        dyb_ref[...] = _dot_nt(dpb, wb_ref[...]).astype(bf16)

    tile = pl.BlockSpec((ts, D), lambda i: (i, 0))
    const = lambda i: (0, 0)
    act = jax.ShapeDtypeStruct((s, D), bf16)
    return pl.pallas_call(
        body, grid=(s // ts,),
        in_specs=[tile, pl.BlockSpec((1, D), const), tile, tile, tile, tile, pl.BlockSpec((ts, D), lambda i: (i, 1)),
                  pl.BlockSpec((D, D), const), pl.BlockSpec((D, D), const), pl.BlockSpec((DIN, D), const)],
        out_specs=[tile, tile, tile, pl.BlockSpec((ts, PG), lambda i: (i, 0)), tile, pl.BlockSpec((ts, DIN), lambda i: (i, 0)),
                   pl.BlockSpec((1, D), const)],
        out_shape=[act, act, act, jax.ShapeDtypeStruct((s, PG), bf16), act, jax.ShapeDtypeStruct((s, DIN), bf16),
                   jax.ShapeDtypeStruct((1, D), f32)],
        name=name, compiler_params=_cp(("arbitrary",)),
    )(dxo, gate, o_sv, pa_sv, pb_sv, proj_g, proj_g, wo, wa, wb)


def _final_loss(x, target, fnw, name):
    s = x.shape[0]
    ts = min(512, s)

    def body(x_ref, t_ref, w_ref, loss_ref, dx_ref, dw_ref):
        @pl.when(pl.program_id(0) == 0)
        def _():
            loss_ref[...] = jnp.zeros_like(loss_ref)
            dw_ref[...] = jnp.zeros_like(dw_ref)
        xv = x_ref[...]
        r = lax.rsqrt(jnp.mean(xv * xv, axis=-1, keepdims=True) + EPS)
        xn = xv * r
        err = xn * w_ref[...] - t_ref[...]
        part = jnp.sum(err * err, axis=0, keepdims=True)
        acc = part[:, 0:128]
        for k in range(1, D // 128):
            acc = acc + part[:, k * 128:(k + 1) * 128]
        loss_ref[0:1, :] += acc * (0.5 / D)
        dy = err * (1.0 / D)
        dw_ref[...] += jnp.sum(dy * xn, axis=0, keepdims=True)
        dxn = dy * w_ref[...]
        dx_ref[...] = r * (dxn - xn * jnp.mean(dxn * xn, axis=-1, keepdims=True))

    tile = pl.BlockSpec((ts, D), lambda i: (i, 0))
    row = pl.BlockSpec((1, D), lambda i: (0, 0))
    return pl.pallas_call(
        body, grid=(s // ts,), in_specs=[tile, tile, row],
        out_specs=[pl.BlockSpec((8, 128), lambda i: (0, 0)), tile, row],
        out_shape=[jax.ShapeDtypeStruct((8, 128), f32), jax.ShapeDtypeStruct((s, D), f32), jax.ShapeDtypeStruct((1, D), f32)],
        name=name, compiler_params=_cp(("arbitrary",)),
    )(x, target, fnw)


def _layer_operands(w_in, conv_w, wa, wb, wo, norm_w, gm_ln_w, gm_ln_b, gm_ws, gm_bs, conv_b, dt_bias, a_log, d_skip, ssm_norm_w):
    w_xbc = _group_major(w_in[:, 5120:7168], w_in[:, 7168:8192], w_in[:, 8192:9216])
    w_b = jnp.concatenate([w_xbc, w_in[:, 3072:5120], w_in[:, 9216:9248], jnp.zeros((D, 224), bf16)], axis=1)
    w_a = w_in[:, 0:3072]
    w_g = w_in[:, 9248:N_IN]
    tril = jnp.tril(jnp.ones((CH, CH), bool))
    wsm = jnp.where(tril[None], gm_ws, 0.0).astype(bf16)

    def heads_row(v):
        return jnp.pad(v, (0, CH - NH)).reshape(1, CH)

    return dict(
        w_b=w_b, w_a=w_a, w_g=w_g, wa=wa, wb=wb, wo=wo,
        norm_w=norm_w.reshape(1, D), lnw=gm_ln_w.reshape(1, D), lnb=gm_ln_b.reshape(1, D),
        wsm=wsm, wsm_t=jnp.swapaxes(wsm, 1, 2), bsf=jnp.repeat(gm_bs.T, CH, axis=1),
        cw=_group_major(conv_w[:, 0:DIN], conv_w[:, DIN:DIN + NG * NST], conv_w[:, DIN + NG * NST:CONVD]),
        cb=_group_major(conv_b[0:DIN], conv_b[DIN:DIN + NG * NST], conv_b[DIN + NG * NST:CONVD]).reshape(1, CONVD),
        dtb=heads_row(dt_bias), a_row=heads_row(-jnp.exp(a_log)),
        snw=ssm_norm_w.reshape(1, DIN), dkc=jnp.repeat(d_skip, HD).reshape(1, DIN),
        ind=(jnp.arange(GW)[:, None] // HD == jnp.arange(CH)[None, :]).astype(bf16),
    )


def _layer_fwd(x, shift, scale, gate, p, tag):
    h = _rmsmod_fwd(x, p["norm_w"], scale, shift, f"rmsmod_fwd{tag}")
    proj_b = _mm(h, p["w_b"], f32, f"proj_b{tag}")
    proj_a = _mm(h, p["w_a"], f32, f"proj_a{tag}")
    proj_g = _mm(h, p["w_g"], f32, f"proj_g{tag}")
    ya = _branch_a_fwd(proj_a, p["lnw"], p["lnb"], p["wsm"], p["bsf"], f"branch_a_fwd{tag}")
    yb, y_sv, states, conv_sv = _branch_b_fwd(proj_b, p["cw"], p["cb"], p["dtb"], p["a_row"], p["dkc"], p["snw"], f"branch_b_fwd{tag}")
    x_out, pa, pb, mg, o = _merge_fwd(ya, yb, proj_g, x, gate, p["wa"], p["wb"], p["wo"], f"merge_fwd{tag}")
    saved = dict(x=x, h=h, proj_b=proj_b, proj_a=proj_a, proj_g=proj_g, ya=ya, yb=yb, y=y_sv, states=states, conv=conv_sv,
                 pa=pa, pb=pb, mg=mg, o=o, scale=scale, gate=gate)
    return x_out, saved


def _layer_bwd(dxo, sv, p, tag):
    do, dpa, dpb, dg, dya, dyb, dgate = _merge_bwd(dxo, sv["gate"], sv["o"], sv["pa"], sv["pb"], sv["proj_g"],
                                                   p["wo"], p["wa"], p["wb"], f"merge_bwd{tag}")
    d_wo = _mm_tn(sv["mg"], do, f"d_wo{tag}")
    d_wa = _mm_tn(sv["ya"], dpa, f"d_wa{tag}")
    d_wb = _mm_tn(sv["yb"], dpb, f"d_wb{tag}")
    da, dws, dbs, dlnw, dlnb = _branch_a_bwd(sv["proj_a"], dya, p["lnw"], p["lnb"], p["wsm"], p["wsm_t"], p["bsf"],
                                             f"branch_a_bwd{tag}")
    db, dcw, dcb, ddtb, dal, ddk, dsnw = _branch_b_bwd(sv["proj_b"], sv["conv"], dyb, sv["y"], sv["states"], p["cw"], p["dtb"],
                                                       p["a_row"], p["dkc"], p["snw"], p["ind"], f"branch_b_bwd{tag}")
    dh = _mm(db, p["w_b"], f32, f"dh_b{tag}", trans_b=True)
    dh = _mm(da, p["w_a"], f32, f"dh_a{tag}", c_in=dh, trans_b=True)
    dh = _mm(dg, p["w_g"], f32, f"dh_g{tag}", c_in=dh, trans_b=True)
    d_w_b_t = _mm_tn(db, sv["h"], f"d_w_b{tag}")
    d_w_a_t = _mm_tn(da, sv["h"], f"d_w_a{tag}")
    d_w_g_t = _mm_tn(dg, sv["h"], f"d_w_g{tag}")
    dx, dscale, dshift, dnw = _rmsmod_bwd(dh, sv["x"], dxo, p["norm_w"], sv["scale"], f"rmsmod_bwd{tag}")
    d_w_in_t = jnp.concatenate([d_w_a_t, d_w_b_t[CONVD:CONVD + DIN], _rows_from_group_major(d_w_b_t[0:CONVD]),
                                d_w_b_t[CONVD + DIN:CONVD + DIN + NH], d_w_g_t], axis=0)
    tril = jnp.tril(jnp.ones((CH, CH), bool))
    heads = lambda v: v[0, 0:NH]
    grads = dict(
        w_in_t=d_w_in_t, w_proj_a=d_wa, w_proj_b=d_wb, w_out=d_wo, conv_w=_from_group_major(dcw), conv_b=_from_group_major(dcb).reshape(CONVD),
        norm_w=dnw.reshape(D), gm_ln_w=dlnw.reshape(D), gm_ln_b=dlnb.reshape(D),
        gm_ws=jnp.where(tril[None], dws, 0.0), gm_bs=dbs.reshape(CH, NG, CH).sum(-1).T,
        dt_bias=heads(ddtb), a_log=heads(dal), d_skip=ddk.reshape(NH, HD).sum(-1), ssm_norm_w=dsnw.reshape(DIN),
        mod=jnp.concatenate([dshift, dscale, dgate], axis=1).reshape(3 * D),
    )
    return dx, grads


def _local_step(x, target, mods, operands_of, fnw):
    saved, layer_ops = [], []
    for l in range(DEPTH):
        shift, scale, gate = mods[l]
        p, x = operands_of(l, x)
        layer_ops.append(p)
        x, sv = _layer_fwd(x, shift, scale, gate, p, f"_l{l}")
        saved.append(sv)
    loss_parts, dx, dfnw = _final_loss(x, target, fnw.reshape(1, D), "final_loss")
    grads = [None] * DEPTH
    for l in reversed(range(DEPTH)):
        dx, grads[l] = _layer_bwd(dx, saved[l], layer_ops[l], f"_l{l}")
    return loss_parts, dx, grads, dfnw.reshape(D)


ADA_COLS = 3 * D // NSHARD


def _ada_fwd(c_all, ada_w, ada_b_cols, name):
    def body(c_ref, w_ref, b_ref, o_ref):
        cv = c_ref[...]
        sc = cv * _sigmoid(cv)
        for l in range(DEPTH):
            o_ref[l] = _dot_hi(sc, w_ref[l]) + b_ref[l]

    return pl.pallas_call(body, out_shape=jax.ShapeDtypeStruct((DEPTH, 8, ADA_COLS), f32), name=name,
                          compiler_params=_cp(None))(c_all, ada_w, ada_b_cols)


def _adam_math(w, g, m, v):
    m = ADAM_B1 * m + (1.0 - ADAM_B1) * g
    v = ADAM_B2 * v + (1.0 - ADAM_B2) * (g * g)
    m_hat = m / (1.0 - ADAM_B1 ** ADAM_STEP)
    v_hat = v / (1.0 - ADAM_B2 ** ADAM_STEP)
    delta = -ADAM_LR * (m_hat / (jnp.sqrt(v_hat) + ADAM_EPS) + ADAM_WD * w)
    return delta, m, v


def _ada_bwd_adamw(c_all, dmod_cols, w, m, v, name):
    tr = 256

    def body(c_ref, dm_ref, w_ref, m_ref, v_ref, g_ref, d_ref, nm_ref, nv_ref):
        cv = c_ref[...]
        sc = cv * _sigmoid(cv)
        g = lax.dot_general(sc, dm_ref[0], (((0,), (0,)), ((), ())), precision=_HI, preferred_element_type=f32)
        g_ref[0] = g
        d_ref[0], nm_ref[0], nv_ref[0] = _adam_math(w_ref[0], g, m_ref[0], v_ref[0])

    blk = pl.BlockSpec((1, tr, ADA_COLS), lambda l, i: (l, i, 0))
    shp = jax.ShapeDtypeStruct((DEPTH, D, ADA_COLS), f32)
    return pl.pallas_call(
        body, grid=(DEPTH, D // tr),
        in_specs=[pl.BlockSpec((8, tr), lambda l, i: (0, i)), pl.BlockSpec((1, 8, ADA_COLS), lambda l, i: (l, 0, 0)), blk, blk, blk],
        out_specs=[blk, blk, blk, blk], out_shape=[shp, shp, shp, shp], name=name, compiler_params=_cp(("parallel", "parallel")),
    )(c_all, dmod_cols, w, m, v)


def _adamw(w, g, m, v, name):
    def body(w_ref, g_ref, m_ref, v_ref, d_ref, nm_ref, nv_ref):
        d_ref[...], nm_ref[...], nv_ref[...] = _adam_math(w_ref[...], g_ref[...], m_ref[...], v_ref[...])

    shp = jax.ShapeDtypeStruct(w.shape, f32)
    return pl.pallas_call(body, out_shape=[shp] * 3, name=name, compiler_params=_cp(None))(w, g, m, v)


def _tile2(r, c):
    if r <= 256 or r % 256 == 0:
        return _pick(r, (256,)), _pick(c, (1024,))
    return r, 128


def _adamw_layers(w, g_mine, g_other, m, v, c_idx, name):
    _, r, c = w.shape
    tr, tc = _tile2(r, c)

    def body(ci_ref, w_ref, gm_ref, go_ref, m_ref, v_ref, g_ref, d_ref, nm_ref, nv_ref):
        def update(g):
            g_ref[0] = g
            d_ref[0], nm_ref[0], nv_ref[0] = _adam_math(w_ref[0], g, m_ref[0], v_ref[0])

        mine = pl.program_id(0) == ci_ref[0]

        @pl.when(mine)
        def _():
            update(gm_ref[...])

        @pl.when(jnp.logical_not(mine))
        def _():
            update(go_ref[...])

    blk = pl.BlockSpec((1, tr, tc), lambda l, i, j, ci: (l, i, j))
    gblk = pl.BlockSpec((tr, tc), lambda l, i, j, ci: (i, j))
    shp = jax.ShapeDtypeStruct(w.shape, f32)
    return pl.pallas_call(
        body,
        grid_spec=pltpu.PrefetchScalarGridSpec(num_scalar_prefetch=1, grid=(DEPTH, r // tr, c // tc),
                                               in_specs=[blk, gblk, gblk, blk, blk], out_specs=[blk, blk, blk, blk]),
        out_shape=[shp, shp, shp, shp], name=name, compiler_params=_cp(("parallel", "parallel", "parallel")),
    )(c_idx, w, g_mine, g_other, m, v)


_MESH = pl.DeviceIdType.MESH
_AXES = ("x", "y", "c")
_HBM = pl.BlockSpec(memory_space=pltpu.HBM)


def _my_place():
    return tuple(lax.axis_index(a) for a in _AXES)


def _allreduce(buf, axes, name):
    r = buf.shape[0]
    n = len(axes)

    def body(x_ref, o_ref, rbuf, ssem, rsem):
        me = dict(zip(_AXES, _my_place()))
        o_ref[...] = x_ref[...]
        for k, ax in enumerate(axes):
            peer = tuple(1 - me[a] if a == ax else me[a] for a in _AXES)
            cp = pltpu.make_async_remote_copy(src_ref=o_ref, dst_ref=rbuf.at[k], send_sem=ssem.at[k], recv_sem=rsem.at[k],
                                              device_id=peer, device_id_type=_MESH)
            cp.start()
            cp.wait()
            o_ref[...] = o_ref[...] + rbuf[k]

    vm = pl.BlockSpec(memory_space=pltpu.VMEM)
    return pl.pallas_call(
        body, out_shape=jax.ShapeDtypeStruct((r, 128), f32), in_specs=[vm], out_specs=vm,
        scratch_shapes=[pltpu.VMEM((n, r, 128), f32), pltpu.SemaphoreType.DMA((n,)), pltpu.SemaphoreType.DMA((n,))],
        name=name, compiler_params=pltpu.CompilerParams(vmem_limit_bytes=VMEM_LIMIT),
    )(buf)


def _other_chips(x, y):
    return [(1 - x, y), (x, 1 - y), (1 - x, 1 - y)]


def _gather_body(ins, outs, ssem, rsem):
    na = len(ins)
    x, y, c = _my_place()
    k_me = 2 * x + y
    sibling = (x, y, 1 - c)
    chips = _other_chips(x, y)
    slots = [2 * cx + cy for cx, cy in chips]
    half = [r.shape[0] // 2 if r.shape[0] % 32 == 0 else None for r in ins]

    def part(ref, a, core):
        return ref if half[a] is None else ref.at[pl.ds(core * half[a], half[a])]

    def rcopy(a, src, slot, core, to, idx):
        return pltpu.make_async_remote_copy(src_ref=src, dst_ref=part(outs[a].at[slot], a, core), send_sem=ssem.at[idx],
                                            recv_sem=rsem.at[idx], device_id=to, device_id_type=_MESH)

    sent = []
    for j, chip in enumerate(chips):
        for a in range(na):
            cp = rcopy(a, part(ins[a], a, c), k_me, c, (*chip, c), j * na + a)
            cp.start()
            sent.append(cp)
    for j, chip in enumerate(chips):
        for a in range(na):
            rcopy(a, part(ins[a], a, c), slots[j], c, (*chip, c), j * na + a).wait_recv()
            if half[a] is not None:
                cp = rcopy(a, part(outs[a].at[slots[j]], a, c), slots[j], c, sibling, (3 + j) * na + a)
                cp.start()
                sent.append(cp)
    for j in range(3):
        for a in range(na):
            if half[a] is not None:
                rcopy(a, part(ins[a], a, c), slots[j], 1 - c, sibling, (3 + j) * na + a).wait_recv()
    for cp in sent:
        cp.wait_send()


def _gather_layer_behind(shards, name, collective_id):
    na = len(shards)
    hbm = pltpu.MemorySpace.HBM
    ins = [jax.new_ref(s, memory_space=hbm) for s in shards]
    outs = [jax.empty_ref(jax.ShapeDtypeStruct((NSHARD,) + s.shape, s.dtype), memory_space=hbm) for s in shards]

    @pl.kernel(mesh=plsc.ScalarSubcoreMesh(axis_name="sequencer", num_cores=1), name=name,
               scratch_types=(pltpu.SemaphoreType.DMA((6 * na,)), pltpu.SemaphoreType.DMA((6 * na,))),
               compiler_params=pltpu.CompilerParams(collective_id=collective_id))
    def launch(ssem, rsem):
        x, y, c = _my_place()
        barrier = pltpu.get_barrier_semaphore()
        peers = [(*chip, c) for chip in _other_chips(x, y)] + [(x, y, 1 - c)]
        for peer in peers:
            pl.semaphore_signal(barrier, inc=1, device_id=peer, device_id_type=_MESH)
        pl.semaphore_wait(barrier, len(peers))
        _gather_body(ins, outs, ssem, rsem)

    launch()
    return [o[...] for o in outs]


def _with_own(gathered, own):
    xi, yi, _ = _my_place()
    whole = lax.dynamic_update_index_in_dim(gathered, own, 2 * xi + yi, 0)
    return [whole[k] for k in range(NSHARD)]


def _swap_layers(parts, name):
    na = len(parts)

    def body(*refs):
        ins, outs = refs[:2 * na], refs[2 * na:3 * na]
        ssem, rsem = refs[3 * na:]
        x, y, c = _my_place()

        def copy(a, layer):
            return pltpu.make_async_remote_copy(src_ref=ins[2 * a + layer], dst_ref=outs[a], send_sem=ssem.at[a], recv_sem=rsem.at[a],
                                                device_id=(x, y, 1 - c), device_id_type=_MESH)

        for layer in range(DEPTH):
            @pl.when(c == 1 - layer)
            def _():
                for a in range(na):
                    copy(a, layer).start()
        for a in range(na):
            copy(a, 0).wait()

    flat = [p for pair in parts for p in pair]
    return pl.pallas_call(
        body, out_shape=[jax.ShapeDtypeStruct(p0.shape, p0.dtype) for p0, _ in parts], in_specs=[_HBM] * (2 * na),
        out_specs=[_HBM] * na, scratch_shapes=[pltpu.SemaphoreType.DMA((na,)), pltpu.SemaphoreType.DMA((na,))], name=name,
    )(*flat)


def _scatter_shards(sums, name):
    na = len(sums)

    def body(*refs):
        ins, outs = refs[:na], refs[na:2 * na]
        ssem, rsem = refs[2 * na:]
        x, y, c = _my_place()
        cps = []
        for j, chip in enumerate(_other_chips(x, y)):
            kj = 2 * chip[0] + chip[1]
            for a in range(na):
                cps.append(pltpu.make_async_remote_copy(
                    src_ref=ins[a].at[kj], dst_ref=outs[a].at[j], send_sem=ssem.at[j * na + a], recv_sem=rsem.at[j * na + a],
                    device_id=(*chip, c), device_id_type=_MESH))
        for cp in cps:
            cp.start()
        for cp in cps:
            cp.wait()

    return pl.pallas_call(
        body, out_shape=[jax.ShapeDtypeStruct((3,) + p.shape[1:], p.dtype) for p in sums], in_specs=[_HBM] * na,
        out_specs=[_HBM] * na, scratch_shapes=[pltpu.SemaphoreType.DMA((3 * na,)), pltpu.SemaphoreType.DMA((3 * na,))], name=name,
    )(*sums)


def _share_layers(finals, name):
    na = len(finals)

    def body(*refs):
        ins, outs = refs[:na], refs[na:2 * na]
        ssem, rsem = refs[2 * na:]
        x, y, c = _my_place()
        cps = [pltpu.make_async_remote_copy(src_ref=ins[a], dst_ref=outs[a], send_sem=ssem.at[a], recv_sem=rsem.at[a],
                                            device_id=(x, y, 1 - c), device_id_type=_MESH) for a in range(na)]
        for cp in cps:
            cp.start()
        for cp in cps:
            cp.wait()

    return pl.pallas_call(
        body, out_shape=[jax.ShapeDtypeStruct(p.shape, p.dtype) for p in finals], in_specs=[_HBM] * na, out_specs=[_HBM] * na,
        scratch_shapes=[pltpu.SemaphoreType.DMA((na,)), pltpu.SemaphoreType.DMA((na,))], name=name,
    )(*finals)


def _add_own_layer(part0, part1, recv, c_idx, name):
    ns, r, c = recv.shape
    tr, tc = _tile2(r, c)

    def body(ci_ref, p0_ref, p1_ref, r_ref, o_ref, ob_ref):
        def add(p_ref):
            t = p_ref[...] + r_ref[...]
            o_ref[...] = t
            ob_ref[...] = t.astype(bf16)

        @pl.when(ci_ref[0] == 0)
        def _():
            add(p0_ref)

        @pl.when(ci_ref[0] == 1)
        def _():
            add(p1_ref)

    blk = pl.BlockSpec((1, tr, tc), lambda k, i, j, ci: (k, i, j))
    blk0 = pl.BlockSpec((1, tr, tc), lambda k, i, j, ci: (k * (1 - ci[0]), i * (1 - ci[0]), j * (1 - ci[0])))
    blk1 = pl.BlockSpec((1, tr, tc), lambda k, i, j, ci: (k * ci[0], i * ci[0], j * ci[0]))
    return pl.pallas_call(
        body,
        grid_spec=pltpu.PrefetchScalarGridSpec(num_scalar_prefetch=1, grid=(ns, r // tr, c // tc), in_specs=[blk0, blk1, blk],
                                               out_specs=[blk, blk]),
        out_shape=[jax.ShapeDtypeStruct((ns, r, c), f32), jax.ShapeDtypeStruct((ns, r, c), bf16)], name=name,
        compiler_params=_cp(("arbitrary", "arbitrary", "arbitrary")),
    )(c_idx, part0, part1, recv)


def _add_own_shard(sums, recv, k_idx, name):
    _, r, c = sums.shape
    tr, tc = _tile2(r, c)

    def body(ki_ref, s_ref, r_ref, o_ref):
        o_ref[...] = ((s_ref[0] + r_ref[0].astype(f32)) + r_ref[1].astype(f32)) + r_ref[2].astype(f32)

    return pl.pallas_call(
        body,
        grid_spec=pltpu.PrefetchScalarGridSpec(
            num_scalar_prefetch=1, grid=(r // tr, c // tc),
            in_specs=[pl.BlockSpec((1, tr, tc), lambda i, j, ki: (ki[0], i, j)), pl.BlockSpec((3, tr, tc), lambda i, j, ki: (0, i, j))],
            out_specs=pl.BlockSpec((tr, tc), lambda i, j, ki: (i, j))),
        out_shape=jax.ShapeDtypeStruct((r, c), f32), name=name, compiler_params=_cp(("parallel", "parallel")),
    )(k_idx, sums, recv)


def _reduce_scatter(parts, tag):
    x, y, c = _my_place()
    c_idx = jnp.reshape(c, (1,)).astype(jnp.int32)
    k_idx = jnp.reshape(2 * x + y, (1,)).astype(jnp.int32)
    na = len(parts)
    recv = _swap_layers(parts, f"rs_swap{tag}")
    sums = [_add_own_layer(parts[a][0], parts[a][1], recv[a], c_idx, f"rs_add_layer{tag}_{a}") for a in range(na)]
    recv = _scatter_shards([sb for _, sb in sums], f"rs_scatter{tag}")
    finals = [_add_own_shard(sums[a][0], recv[a], k_idx, f"rs_add_shard{tag}_{a}") for a in range(na)]
    return finals, _share_layers(finals, f"rs_share{tag}"), c_idx


_SMALL = [("ada_b", (DEPTH, 3 * D)), ("norm_w", (DEPTH, D)), ("gm_ln_w", (DEPTH, D)), ("gm_ln_b", (DEPTH, D)),
          ("gm_ws", (DEPTH, NG, CH, CH)), ("gm_bs", (DEPTH, NG, CH)), ("conv_b", (DEPTH, CONVD)), ("dt_bias", (DEPTH, NH)),
          ("a_log", (DEPTH, NH)), ("d_skip", (DEPTH, NH)), ("ssm_norm_w", (DEPTH, DIN)), ("final_norm_w", (D,))]


def _rows_of(shape):
    n = 1
    for d in shape:
        n *= d
    return -(-n // 1024) * 8


def _pack(arrays):
    rows = []
    for a in arrays:
        flat = a.reshape(-1)
        r = _rows_of(a.shape)
        rows.append(jnp.pad(flat, (0, r * 128 - flat.shape[0])).reshape(r, 128))
    return jnp.concatenate(rows, axis=0)


def _unpack(buf, shapes):
    out, at = [], 0
    for shp in shapes:
        r = _rows_of(shp)
        n = 1
        for d in shp:
            n *= d
        out.append(buf[at:at + r].reshape(-1)[:n].reshape(shp))
        at += r
    return out


def kernel(x, c, ada_w, ada_b, norm_w, w_in, gm_ln_w, gm_ln_b, gm_ws, gm_bs, conv_w, conv_b, dt_bias, a_log, d_skip, ssm_norm_w, w_proj_a, w_proj_b, w_out, final_norm_w, loss_target, m_ada_w, m_ada_b, m_norm_w, m_w_in, m_gm_ln_w, m_gm_ln_b, m_gm_ws, m_gm_bs, m_conv_w, m_conv_b, m_dt_bias, m_a_log, m_d_skip, m_ssm_norm_w, m_w_proj_a, m_w_proj_b, m_w_out, m_final_norm_w, v_ada_w, v_ada_b, v_norm_w, v_w_in, v_gm_ln_w, v_gm_ln_b, v_gm_ws, v_gm_bs, v_conv_w, v_conv_b, v_dt_bias, v_a_log, v_d_skip, v_ssm_norm_w, v_w_proj_a, v_w_proj_b, v_w_out, v_final_norm_w):
    xi, yi, ci = _my_place()
    k_me = 2 * xi + yi
    b_me = 4 * xi + 2 * yi + ci
    w = dict(ada_b=ada_b, norm_w=norm_w, gm_ln_w=gm_ln_w, gm_ln_b=gm_ln_b, gm_ws=gm_ws, gm_bs=gm_bs, conv_b=conv_b, dt_bias=dt_bias,
             a_log=a_log, d_skip=d_skip, ssm_norm_w=ssm_norm_w, final_norm_w=final_norm_w)
    m = dict(ada_b=m_ada_b, norm_w=m_norm_w, gm_ln_w=m_gm_ln_w, gm_ln_b=m_gm_ln_b, gm_ws=m_gm_ws, gm_bs=m_gm_bs, conv_b=m_conv_b,
             dt_bias=m_dt_bias, a_log=m_a_log, d_skip=m_d_skip, ssm_norm_w=m_ssm_norm_w, final_norm_w=m_final_norm_w)
    v = dict(ada_b=v_ada_b, norm_w=v_norm_w, gm_ln_w=v_gm_ln_w, gm_ln_b=v_gm_ln_b, gm_ws=v_gm_ws, gm_bs=v_gm_bs, conv_b=v_conv_b,
             dt_bias=v_dt_bias, a_log=v_a_log, d_skip=v_d_skip, ssm_norm_w=v_ssm_norm_w, final_norm_w=v_final_norm_w)

    c_slot = lax.dynamic_update_slice(jnp.zeros((8, D), f32), c, (b_me, 0))
    c_all = _allreduce(c_slot.reshape(64, 128), _AXES, "gather_c").reshape(8, D)
    ada_b_cols = lax.dynamic_slice(ada_b, (0, k_me * ADA_COLS), (DEPTH, ADA_COLS)).reshape(DEPTH, 1, ADA_COLS)
    mod_cols = _ada_fwd(c_all, ada_w, ada_b_cols, "ada_fwd")
    mod_slot = lax.dynamic_update_slice(jnp.zeros((DEPTH, 8, 3 * D), f32), mod_cols, (0, 0, k_me * ADA_COLS))
    mod_all = _allreduce(mod_slot.reshape(-1, 128), ("x", "y"), "gather_mod").reshape(DEPTH, 8, 3 * D)
    mod_me = lax.dynamic_slice(mod_all, (0, b_me, 0), (DEPTH, 1, 3 * D))
    mods = [(mod_me[l, :, 0:D], mod_me[l, :, D:2 * D], mod_me[l, :, 2 * D:3 * D]) for l in range(DEPTH)]

    rows_sh = jnp.concatenate([w_proj_a, w_proj_b, w_out], axis=1).astype(bf16)
    win_sh = w_in.astype(bf16)
    first = _gather_layer_behind([win_sh[0], rows_sh[0], conv_w[0]], "gather_l0", 1)
    first, mods, later = lax.optimization_barrier((first, mods, [win_sh[1], rows_sh[1], conv_w[1]]))
    others = [first, _gather_layer_behind(later, "gather_l1", 2)]

    def operands_of(l, x_in):
        gathered = others[l]
        if l == 1:
            gathered, x_in = lax.optimization_barrier((gathered, x_in))
        win_g, rows_g, conv_g = gathered
        rows_l = _with_own(rows_g, rows_sh[l])
        w_in_l = jnp.concatenate(_with_own(win_g, win_sh[l]), axis=1)
        wa_l = jnp.concatenate([t[0:256] for t in rows_l], axis=0)
        wb_l = jnp.concatenate([t[256:768] for t in rows_l], axis=0)
        wo_l = jnp.concatenate([t[768:1024] for t in rows_l], axis=0)
        cw_l = jnp.concatenate(_with_own(conv_g, conv_w[l]), axis=1)
        return _layer_operands(w_in_l, cw_l, wa_l, wb_l, wo_l, norm_w[l], gm_ln_w[l], gm_ln_b[l], gm_ws[l], gm_bs[l], conv_b[l],
                               dt_bias[l], a_log[l], d_skip[l], ssm_norm_w[l]), x_in

    loss_parts, dx, grads, dfnw = _local_step(x[0], loss_target[0], mods, operands_of, final_norm_w)

    s_in = N_IN // NSHARD
    tr_ = lambda t: jnp.swapaxes(t, 1, 2)
    g_in = [grads[l]["w_in_t"].reshape(NSHARD, s_in, D) for l in range(DEPTH)]
    g_rows = [jnp.concatenate([grads[l]["w_proj_a"].reshape(NSHARD, 256, D), grads[l]["w_proj_b"].reshape(NSHARD, 512, D),
                               grads[l]["w_out"].reshape(NSHARD, 256, D)], axis=1) for l in range(DEPTH)]
    g_conv = [grads[l]["conv_w"].reshape(4, NSHARD, D).transpose(1, 0, 2) for l in range(DEPTH)]
    (f_in, f_rows, f_conv), (o_in, o_rows, o_conv), c_idx = _reduce_scatter([g_in, g_rows, g_conv], "")
    gr_in, d_in, nm_in, nv_in = [tr_(t) for t in _adamw_layers(tr_(w_in), f_in, o_in, tr_(m_w_in), tr_(v_w_in), c_idx, "adamw_w_in")]
    cat = lambda a, b, c_: jnp.concatenate([a, b, c_], axis=1)
    gr_rows, d_rows, nm_rows, nv_rows = _adamw_layers(cat(w_proj_a, w_proj_b, w_out), f_rows, o_rows, cat(m_w_proj_a, m_w_proj_b, m_w_out),
                                                      cat(v_w_proj_a, v_w_proj_b, v_w_out), c_idx, "adamw_rows")
    gr_conv = jnp.where(ci == 0, jnp.stack([f_conv, o_conv]), jnp.stack([o_conv, f_conv]))
    split = lambda t: (t[:, 0:256], t[:, 256:768], t[:, 768:1024])

    dmod_slot = lax.dynamic_update_slice(jnp.zeros((DEPTH, 8, 3 * D), f32),
                                         jnp.stack([grads[l]["mod"] for l in range(DEPTH)]).reshape(DEPTH, 1, 3 * D), (0, b_me, 0))
    small_g = {n: (dfnw if n == "final_norm_w" else jnp.stack([grads[l]["mod" if n == "ada_b" else n] for l in range(DEPTH)]))
               for n, _ in _SMALL}
    packed = _allreduce(_pack([small_g[n] for n, _ in _SMALL] + [dmod_slot, loss_parts]), _AXES, "allreduce_small")
    n_small = sum(_rows_of(s) for _, s in _SMALL)
    n_dmod = _rows_of(dmod_slot.shape)
    g_small = packed[0:n_small]
    dmod_all = packed[n_small:n_small + n_dmod].reshape(DEPTH, 8, 3 * D)
    loss = jnp.sum(packed[n_small + n_dmod:])
    small_gw = jnp.concatenate([g_small, _pack([gr_conv])], axis=0)
    d_s, nm_s, nv_s = _adamw(_pack([w[n] for n, _ in _SMALL] + [conv_w]), small_gw,
                             _pack([m[n] for n, _ in _SMALL] + [m_conv_w]), _pack([v[n] for n, _ in _SMALL] + [v_conv_w]), "adamw_small")
    shapes = [s for _, s in _SMALL] + [conv_w.shape]
    names = [n for n, _ in _SMALL] + ["conv_w"]
    g_d = dict(zip(names, _unpack(small_gw, shapes)))
    d_d = dict(zip(names, _unpack(d_s, shapes)))
    nm_d = dict(zip(names, _unpack(nm_s, shapes)))
    nv_d = dict(zip(names, _unpack(nv_s, shapes)))

    dmod_cols = lax.dynamic_slice(dmod_all, (0, 0, k_me * ADA_COLS), (DEPTH, 8, ADA_COLS))
    g_ada, d_ada, nm_ada, nv_ada = _ada_bwd_adamw(c_all, dmod_cols, ada_w, m_ada_w, v_ada_w, "ada_bwd_adamw")

    def by_name(big, small):
        ga, gb, go = split(big[1])
        return dict(small, ada_w=big[2], w_in=big[0], w_proj_a=ga, w_proj_b=gb, w_out=go)

    order = ["ada_w", "ada_b", "norm_w", "w_in", "gm_ln_w", "gm_ln_b", "gm_ws", "gm_bs", "conv_w", "conv_b", "dt_bias", "a_log",
             "d_skip", "ssm_norm_w", "w_proj_a", "w_proj_b", "w_out", "final_norm_w"]
    outs = []
    for big, small in (((gr_in, gr_rows, g_ada), g_d), ((d_in, d_rows, d_ada), d_d), ((nm_in, nm_rows, nm_ada), nm_d),
                       ((nv_in, nv_rows, nv_ada), nv_d)):
        t = by_name(big, small)
        outs += [t[n] for n in order]
    return (loss, dx.reshape(1, -1, D), *outs)
```

```python
import jax
import jax.numpy as jnp
from jax import lax
from jax.experimental import pallas as pl
from jax.experimental.pallas import tpu as pltpu
from jax.experimental.pallas import tpu_sc as plsc

f32 = jnp.float32
bf16 = jnp.bfloat16

D = 1024
DEPTH = 2
EPS = 1e-6
CH = 128
NG = 8
HPG = 4
HD = 64
NH = NG * HPG
NST = 128
DIN = 2048
CONVD = 4096
GW = DIN // NG
PB = CONVD + DIN + 256
PA = 3 * D
PG = 2 * D
N_IN = 11296
NSHARD = 4
V7X_VMEM_BYTES = 64 * 2 ** 20
VMEM_LIMIT = V7X_VMEM_BYTES - 8 * 2 ** 20

ADAM_LR, ADAM_B1, ADAM_B2, ADAM_EPS, ADAM_WD, ADAM_STEP = 0.001, 0.9, 0.999, 1e-08, 0.01, 10

_HI = lax.Precision.HIGHEST


def _cp(sem):
    return pltpu.CompilerParams(dimension_semantics=sem, vmem_limit_bytes=VMEM_LIMIT)


def _sigmoid(x):
    return 0.5 * jnp.tanh(0.5 * x) + 0.5


def _silu_and_grad(x):
    s = _sigmoid(x)
    return x * s, s * (1.0 + x * (1.0 - s))


_GELU_K = 0.7978845608028654
_GELU_C = 0.044715


def _gelu_and_grad(x):
    x2 = x * x
    t = jnp.tanh(_GELU_K * (x + _GELU_C * x * x2))
    g = 0.5 * x * (1.0 + t)
    dg = 0.5 * (1.0 + t) + 0.5 * x * (1.0 - t * t) * _GELU_K * (1.0 + 3.0 * _GELU_C * x2)
    return g, dg


def _gelu(x):
    t = jnp.tanh(_GELU_K * (x + _GELU_C * x * x * x))
    return 0.5 * x * (1.0 + t)


def _softplus(x):
    return jnp.maximum(x, 0.0) + jnp.log(1.0 + jnp.exp(-jnp.abs(x)))


def _dot(a, b):
    return jnp.dot(a, b, preferred_element_type=f32)


def _dot_nt(a, b):
    return lax.dot_general(a, b, (((1,), (1,)), ((), ())), preferred_element_type=f32)


def _dot_tn(a, b):
    return lax.dot_general(a, b, (((0,), (0,)), ((), ())), preferred_element_type=f32)


def _dot_hi(a, b):
    return jnp.dot(a, b, precision=_HI, preferred_element_type=f32)


def _rmsmod_fwd(x, nw, scale, shift, name):
    s = x.shape[0]
    ts = min(512, s)

    def body(x_ref, nw_ref, sc_ref, sh_ref, h_ref):
        xv = x_ref[...]
        r = lax.rsqrt(jnp.mean(xv * xv, axis=-1, keepdims=True) + EPS)
        h_ref[...] = ((xv * r) * nw_ref[...] * (1.0 + sc_ref[...]) + sh_ref[...]).astype(bf16)

    row = pl.BlockSpec((1, D), lambda i: (0, 0))
    tile = pl.BlockSpec((ts, D), lambda i: (i, 0))
    return pl.pallas_call(
        body, grid=(s // ts,), in_specs=[tile, row, row, row], out_specs=tile,
        out_shape=jax.ShapeDtypeStruct((s, D), bf16), name=name, compiler_params=_cp(("parallel",)),
    )(x, nw, scale, shift)


def _rmsmod_bwd(dh, x, dres, nw, scale, name):
    s = x.shape[0]
    ts = min(512, s)

    def body(dh_ref, x_ref, dres_ref, nw_ref, sc_ref, dx_ref, dsc_ref, dsh_ref, dnw_ref):
        @pl.when(pl.program_id(0) == 0)
        def _():
            dsc_ref[...] = jnp.zeros_like(dsc_ref)
            dsh_ref[...] = jnp.zeros_like(dsh_ref)
            dnw_ref[...] = jnp.zeros_like(dnw_ref)
        xv = x_ref[...]
        dhv = dh_ref[...]
        r = lax.rsqrt(jnp.mean(xv * xv, axis=-1, keepdims=True) + EPS)
        xn = xv * r
        one_sc = 1.0 + sc_ref[...]
        dsc_ref[...] += jnp.sum(dhv * xn * nw_ref[...], axis=0, keepdims=True)
        dsh_ref[...] += jnp.sum(dhv, axis=0, keepdims=True)
        dnw_ref[...] += jnp.sum(dhv * xn * one_sc, axis=0, keepdims=True)
        dxn = dhv * (nw_ref[...] * one_sc)
        dx_ref[...] = r * (dxn - xn * jnp.mean(dxn * xn, axis=-1, keepdims=True)) + dres_ref[...]

    row = pl.BlockSpec((1, D), lambda i: (0, 0))
    tile = pl.BlockSpec((ts, D), lambda i: (i, 0))
    vec = jax.ShapeDtypeStruct((1, D), f32)
    return pl.pallas_call(
        body, grid=(s // ts,), in_specs=[tile, tile, tile, row, row], out_specs=[tile, row, row, row],
        out_shape=[jax.ShapeDtypeStruct((s, D), f32), vec, vec, vec], name=name, compiler_params=_cp(("arbitrary",)),
    )(dh, x, dres, nw, scale)


def _pick(n, prefs):
    for p in prefs:
        if n % p == 0:
            return p
    return n


def _mm(a, b, out_dtype, name, c_in=None, trans_b=False):
    m, k = a.shape
    n = b.shape[0] if trans_b else b.shape[1]
    tm = _pick(m, (1024, 512, 256))
    tn = _pick(n, (1280, 1024, 512))
    tk = _pick(k, (1280, 1024, 512))
    nk = k // tk

    def body(*refs):
        if c_in is not None:
            a_ref, b_ref, c_ref, o_ref, acc = refs
        else:
            a_ref, b_ref, o_ref, acc = refs
        kk = pl.program_id(2)

        @pl.when(kk == 0)
        def _():
            if c_in is not None:
                acc[...] = c_ref[...]
            else:
                acc[...] = jnp.zeros_like(acc)
        acc[...] += (_dot_nt if trans_b else _dot)(a_ref[...], b_ref[...])

        @pl.when(kk == nk - 1)
        def _():
            o_ref[...] = acc[...].astype(out_dtype)

    b_spec = pl.BlockSpec((tn, tk), lambda j, i, kk: (j, kk)) if trans_b else pl.BlockSpec((tk, tn), lambda j, i, kk: (kk, j))
    in_specs = [pl.BlockSpec((tm, tk), lambda j, i, kk: (i, kk)), b_spec]
    args = [a, b]
    if c_in is not None:
        in_specs.append(pl.BlockSpec((tm, tn), lambda j, i, kk: (i, j)))
        args.append(c_in)
    return pl.pallas_call(
        body, grid=(n // tn, m // tm, nk), in_specs=in_specs, out_specs=pl.BlockSpec((tm, tn), lambda j, i, kk: (i, j)),
        out_shape=jax.ShapeDtypeStruct((m, n), out_dtype), scratch_shapes=[pltpu.VMEM((tm, tn), f32)], name=name,
        compiler_params=_cp(("parallel", "parallel", "arbitrary")),
    )(*args)


def _mm_tn(a, b, name):
    t, k1 = a.shape
    n = b.shape[1]
    t1 = _pick(k1, (1280, 1024, 512))
    tn = _pick(n, (1280, 1024, 512))
    tt = _pick(t, (2048, 1024, 512, 256))
    nt = t // tt

    def body(a_ref, b_ref, o_ref):
        tt_i = pl.program_id(2)

        @pl.when(tt_i == 0)
        def _():
            o_ref[...] = jnp.zeros_like(o_ref)
        o_ref[...] += _dot_tn(a_ref[...], b_ref[...])

    return pl.pallas_call(
        body, grid=(k1 // t1, n // tn, nt),
        in_specs=[pl.BlockSpec((tt, t1), lambda i, j, tt_i: (tt_i, i)), pl.BlockSpec((tt, tn), lambda i, j, tt_i: (tt_i, j))],
        out_specs=pl.BlockSpec((t1, tn), lambda i, j, tt_i: (i, j)),
        out_shape=jax.ShapeDtypeStruct((k1, n), f32), name=name,
        compiler_params=_cp(("parallel", "parallel", "arbitrary")),
    )(a, b)


def _ln_stats(v):
    mu = jnp.mean(v, axis=-1, keepdims=True)
    vc = v - mu
    rstd = lax.rsqrt(jnp.mean(vc * vc, axis=-1, keepdims=True) + EPS)
    return vc * rstd, rstd


def _mix(w_ref, vl):
    return jnp.concatenate([_dot(w_ref[g], vl[:, g * CH:(g + 1) * CH]) for g in range(NG)], axis=1)


def _branch_a_fwd(proj_a, lnw, lnb, wsm, bsf, name):
    s = proj_a.shape[0]
    ta = min(256, s)

    def body(pu_ref, pv_ref, pz_ref, lnw_ref, lnb_ref, w_ref, bs_ref, ya_ref):
        for c in range(ta // CH):
            rows = pl.ds(c * CH, CH)
            vh, _ = _ln_stats(_gelu(pv_ref[rows, :]))
            vl = (vh * lnw_ref[...] + lnb_ref[...]).astype(bf16)
            mixed = _mix(w_ref, vl) + bs_ref[...]
            pz = pz_ref[rows, :]
            ya_ref[rows, :] = (_gelu(pu_ref[rows, :]) * mixed * (pz * _sigmoid(pz))).astype(bf16)

    row = pl.BlockSpec((1, D), lambda i: (0, 0))
    return pl.pallas_call(
        body, grid=(s // ta,),
        in_specs=[pl.BlockSpec((ta, D), lambda i: (i, 0)), pl.BlockSpec((ta, D), lambda i: (i, 1)),
                  pl.BlockSpec((ta, D), lambda i: (i, 2)), row, row,
                  pl.BlockSpec((NG, CH, CH), lambda i: (0, 0, 0)), pl.BlockSpec((CH, D), lambda i: (0, 0))],
        out_specs=pl.BlockSpec((ta, D), lambda i: (i, 0)),
        out_shape=jax.ShapeDtypeStruct((s, D), bf16), name=name, compiler_params=_cp(("parallel",)),
    )(proj_a, proj_a, proj_a, lnw, lnb, wsm, bsf)


def _branch_a_bwd(proj_a, dya, lnw, lnb, wsm, wsm_t, bsf, name):
    s = proj_a.shape[0]
    ta = min(256, s)

    def body(pu_ref, pv_ref, pz_ref, dya_ref, lnw_ref, lnb_ref, w_ref, wt_ref, bs_ref,
             dp_ref, dws_ref, dbs_ref, dlnw_ref, dlnb_ref):
        @pl.when(pl.program_id(0) == 0)
        def _():
            dws_ref[...] = jnp.zeros_like(dws_ref)
            dbs_ref[...] = jnp.zeros_like(dbs_ref)
            dlnw_ref[...] = jnp.zeros_like(dlnw_ref)
            dlnb_ref[...] = jnp.zeros_like(dlnb_ref)
        for c in range(ta // CH):
            rows = pl.ds(c * CH, CH)
            u, du = _gelu_and_grad(pu_ref[rows, :])
            v, dv_act = _gelu_and_grad(pv_ref[rows, :])
            zg, dzg = _silu_and_grad(pz_ref[rows, :])
            vh, rstd = _ln_stats(v)
            vl = (vh * lnw_ref[...] + lnb_ref[...]).astype(bf16)
            mixed = _mix(w_ref, vl) + bs_ref[...]
            dy = dya_ref[rows, :].astype(f32)
            dmixed = dy * u * zg
            dp_ref[rows, 0:D] = (dy * mixed * zg * du).astype(bf16)
            dp_ref[rows, 2 * D:3 * D] = (dy * u * mixed * dzg).astype(bf16)
            dmb = dmixed.astype(bf16)
            dbs_ref[...] += dmixed
            dvl = _mix(wt_ref, dmb)
            for g in range(NG):
                cols = slice(g * CH, (g + 1) * CH)
                dws_ref[g] += _dot_nt(dmb[:, cols], vl[:, cols])
            dlnw_ref[...] += jnp.sum(dvl * vh, axis=0, keepdims=True)
            dlnb_ref[...] += jnp.sum(dvl, axis=0, keepdims=True)
            dvh = dvl * lnw_ref[...]
            dv = rstd * (dvh - jnp.mean(dvh, axis=-1, keepdims=True) - vh * jnp.mean(dvh * vh, axis=-1, keepdims=True))
            dp_ref[rows, D:2 * D] = (dv * dv_act).astype(bf16)

    row = pl.BlockSpec((1, D), lambda i: (0, 0))
    wspec = pl.BlockSpec((NG, CH, CH), lambda i: (0, 0, 0))
    full = pl.BlockSpec((CH, D), lambda i: (0, 0))
    return pl.pallas_call(
        body, grid=(s // ta,),
        in_specs=[pl.BlockSpec((ta, D), lambda i: (i, 0)), pl.BlockSpec((ta, D), lambda i: (i, 1)),
                  pl.BlockSpec((ta, D), lambda i: (i, 2)), pl.BlockSpec((ta, D), lambda i: (i, 0)),
                  row, row, wspec, wspec, full],
        out_specs=[pl.BlockSpec((ta, PA), lambda i: (i, 0)), wspec, full, row, row],
        out_shape=[jax.ShapeDtypeStruct((s, PA), bf16), jax.ShapeDtypeStruct((NG, CH, CH), f32),
                   jax.ShapeDtypeStruct((CH, D), f32), jax.ShapeDtypeStruct((1, D), f32), jax.ShapeDtypeStruct((1, D), f32)],
        name=name, compiler_params=_cp(("arbitrary",)),
    )(proj_a, proj_a, proj_a, dya, lnw, lnb, wsm, wsm_t, bsf)


GB = GW + 2 * NST


def _group_major(xs, b, c):
    lead = xs.shape[:-1]
    return jnp.concatenate([xs.reshape(lead + (NG, GW)), b.reshape(lead + (NG, NST)), c.reshape(lead + (NG, NST))],
                           axis=-1).reshape(lead + (CONVD,))


def _from_group_major(t):
    lead = t.shape[:-1]
    t = t.reshape(lead + (NG, GB))
    return jnp.concatenate([t[..., 0:GW].reshape(lead + (DIN,)), t[..., GW:GW + NST].reshape(lead + (NG * NST,)),
                            t[..., GW + NST:GB].reshape(lead + (NG * NST,))], axis=-1)


def _rows_from_group_major(t):
    t = t.reshape(NG, GB, t.shape[-1])
    return jnp.concatenate([t[:, 0:GW].reshape(DIN, -1), t[:, GW:GW + NST].reshape(NG * NST, -1),
                            t[:, GW + NST:GB].reshape(NG * NST, -1)], axis=0)


def _shift_rows(x, prev8, j):
    xr = pltpu.roll(x, j, 0)
    fix = pltpu.roll(prev8, j, 0)
    rid = lax.broadcasted_iota(jnp.int32, (8, x.shape[1]), 0)
    top = jnp.where(rid < j, fix, xr[0:8])
    return jnp.concatenate([top, xr[8:]], axis=0)


def _shift_rows_up(d, next8, j):
    dr = pltpu.roll(d, CH - j, 0)
    fix = pltpu.roll(next8, 8 - j, 0)
    rid = lax.broadcasted_iota(jnp.int32, (8, d.shape[1]), 0)
    bot = jnp.where(rid >= 8 - j, fix, dr[CH - 8:CH])
    return jnp.concatenate([dr[0:CH - 8], bot], axis=0)


def _conv_pre(x, prev8, cw_ref, cb_ref, cols):
    shifted = [_shift_rows(x, prev8, j) for j in (1, 2, 3)]
    conv = cb_ref[:, cols] + cw_ref[3:4, cols] * x
    for j in (1, 2, 3):
        conv = conv + cw_ref[3 - j:4 - j, cols] * shifted[j - 1]
    return conv, shifted


def _tril_mask():
    return lax.broadcasted_iota(jnp.int32, (CH, CH), 0) >= lax.broadcasted_iota(jnp.int32, (CH, CH), 1)


def _sum_all(v):
    return jnp.sum(jnp.sum(v, axis=0, keepdims=True), axis=1, keepdims=True)


def _lanes(g, width, base=0):
    return pl.ds(pl.multiple_of(base + g * width, width), width)


def _branch_b_fwd(proj_b, cw, cb, dtb, a_row, dkc, snw, name):
    s = proj_b.shape[0]
    nc = s // CH

    def body(xbc_ref, sz_ref, dtr_ref, cw_ref, cb_ref, dtb_ref, a_ref, dkc_ref, snw_ref,
             yb_ref, y_ref, st_ref, cv_ref, prev8, state, acst_s):
        @pl.when(pl.program_id(0) == 0)
        def _():
            prev8[...] = jnp.zeros_like(prev8)
            state[...] = jnp.zeros_like(state)
        st_ref[0] = state[...].astype(bf16)
        mask = _tril_mask()
        dt_all = _softplus(dtr_ref[:, 0:CH] + dtb_ref[...])
        acs_all = _dot_hi(mask.astype(f32), dt_all * a_ref[...])
        acst_s[...] = acs_all.T

        def group(g, carry):
            cols = _lanes(g, GB)
            gcols = _lanes(g, GW)
            x = xbc_ref[:, cols]
            conv, _ = _conv_pre(x, prev8[:, cols], cw_ref, cb_ref, cols)
            prev8[:, cols] = x[CH - 8:CH]
            cv_ref[:, cols] = conv.astype(bf16)
            xc = conv * _sigmoid(conv)
            xs = xc[:, 0:GW]
            bg = xc[:, GW:GW + NST].astype(bf16)
            cg = xc[:, GW + NST:GB].astype(bf16)
            back = lax.rem(CH - HPG * g, CH)
            dt = pltpu.roll(dt_all, back, 1)
            acs = pltpu.roll(acs_all, back, 1)
            cbm = _dot_nt(cg, bg)
            dkc_g = dkc_ref[:, gcols]
            y_parts = []
            for r in range(HPG):
                colb = jnp.broadcast_to(acs[:, r:r + 1], (CH, CH))
                row = acst_s[pl.ds(g * HPG + r, 1), :]
                lmat = jnp.exp(jnp.where(mask, colb - row, -jnp.inf))
                xr = xs[:, r * HD:(r + 1) * HD]
                xd = xr * dt[:, r:r + 1]
                sp = state[g * HPG + r]
                col = colb[:, 0:HD]
                alast = colb[CH - 1:CH, 0:HD]
                y_r = _dot((cbm * lmat).astype(bf16), xd.astype(bf16))
                y_r = y_r + jnp.exp(col) * _dot_nt(cg, sp.astype(bf16))
                y_parts.append(y_r + xr * dkc_g[:, r * HD:(r + 1) * HD])
                cs = _dot_tn((xd * jnp.exp(alast - col)).astype(bf16), bg)
                state[g * HPG + r] = jnp.exp(colb[CH - 1:CH, :]) * sp + cs
            y = jnp.concatenate(y_parts, axis=1)
            szv = sz_ref[:, gcols]
            yz = y * (szv * _sigmoid(szv))
            rr = lax.rsqrt(jnp.mean(yz * yz, axis=-1, keepdims=True) + EPS)
            yb_ref[:, gcols] = (yz * rr * snw_ref[:, gcols]).astype(bf16)
            y_ref[:, gcols] = y.astype(bf16)
            return carry

        lax.fori_loop(0, NG, group, 0)

    const2 = lambda c: (0, 0)
    return pl.pallas_call(
        body, grid=(nc,),
        in_specs=[pl.BlockSpec((CH, CONVD), lambda c: (c, 0)), pl.BlockSpec((CH, DIN), lambda c: (c, CONVD // DIN)),
                  pl.BlockSpec((CH, 256), lambda c: (c, (CONVD + DIN) // 256)),
                  pl.BlockSpec((4, CONVD), const2), pl.BlockSpec((1, CONVD), const2),
                  pl.BlockSpec((1, CH), const2), pl.BlockSpec((1, CH), const2),
                  pl.BlockSpec((1, DIN), const2), pl.BlockSpec((1, DIN), const2)],
        out_specs=[pl.BlockSpec((CH, DIN), lambda c: (c, 0)), pl.BlockSpec((CH, DIN), lambda c: (c, 0)),
                   pl.BlockSpec((1, NH, HD, NST), lambda c: (c, 0, 0, 0)), pl.BlockSpec((CH, CONVD), lambda c: (c, 0))],
        out_shape=[jax.ShapeDtypeStruct((s, DIN), bf16), jax.ShapeDtypeStruct((s, DIN), bf16),
                   jax.ShapeDtypeStruct((nc, NH, HD, NST), bf16), jax.ShapeDtypeStruct((s, CONVD), bf16)],
        scratch_shapes=[pltpu.VMEM((8, CONVD), f32), pltpu.VMEM((NH, HD, NST), f32), pltpu.VMEM((CH, CH), f32)],
        name=name, compiler_params=_cp(("arbitrary",)),
    )(proj_b, proj_b, proj_b, cw, cb, dtb, a_row, dkc, snw)


def _branch_b_bwd(proj_b, conv_sv, dyb, y_sv, states, cw, dtb, a_row, dkc, snw, ind, name):
    s = proj_b.shape[0]
    nc = s // CH

    def body(xbc_ref, cv_ref, sz_ref, dtr_ref, dyb_ref, y_ref, st_ref, cw_ref, dtb_ref, a_ref, dkc_ref,
             snw_ref, ind_ref, dp_ref, dcw_ref, dcb_ref, ddtb_ref, dal_ref, ddk_ref, dsnw_ref,
             dstate, dnext8, acst_s, dacs_acc, q2_acc):
        @pl.when(pl.program_id(0) == 0)
        def _():
            dstate[...] = jnp.zeros_like(dstate)
            dnext8[...] = jnp.zeros_like(dnext8)
            dcw_ref[...] = jnp.zeros_like(dcw_ref)
            dcb_ref[...] = jnp.zeros_like(dcb_ref)
            ddtb_ref[...] = jnp.zeros_like(ddtb_ref)
            dal_ref[...] = jnp.zeros_like(dal_ref)
            ddk_ref[...] = jnp.zeros_like(ddk_ref)
            dsnw_ref[...] = jnp.zeros_like(dsnw_ref)

        dacs_acc[...] = jnp.zeros_like(dacs_acc)
        q2_acc[...] = jnp.zeros_like(q2_acc)
        mask = _tril_mask()
        tri_t = (lax.broadcasted_iota(jnp.int32, (CH, CH), 0) <= lax.broadcasted_iota(jnp.int32, (CH, CH), 1)).astype(f32)
        lane1 = lax.broadcasted_iota(jnp.int32, (1, CH), 1)
        is_last = lax.broadcasted_iota(jnp.int32, (CH, 1), 0) == CH - 1
        z_all = dtr_ref[:, 0:CH] + dtb_ref[...]
        dt_all = _softplus(z_all)
        adt_all = dt_all * a_ref[...]
        acs_all = _dot_hi(mask.astype(f32), adt_all)
        acst_s[...] = acs_all.T

        def ind_sum(v):
            hi = v.astype(bf16)
            lo = (v - hi.astype(f32)).astype(bf16)
            return _dot(hi, ind_ref[...]) + _dot(lo, ind_ref[...])

        def group(g, carry):
            cols = _lanes(g, GB)
            gcols = _lanes(g, GW)
            conv = cv_ref[:, cols].astype(f32)
            sg = _sigmoid(conv)
            xc = conv * sg
            xs = xc[:, 0:GW]
            bg = xc[:, GW:GW + NST].astype(bf16)
            cg = xc[:, GW + NST:GB].astype(bf16)

            y = y_ref[:, gcols].astype(f32)
            silu_sz, dsilu_sz = _silu_and_grad(sz_ref[:, gcols])
            yz = y * silu_sz
            rr = lax.rsqrt(jnp.mean(yz * yz, axis=-1, keepdims=True) + EPS)
            dyb_g = dyb_ref[:, gcols].astype(f32)
            w = dyb_g * snw_ref[:, gcols]
            dsnw_ref[:, gcols] += jnp.sum(dyb_g * yz * rr, axis=0, keepdims=True)
            dyz = rr * w - yz * (rr * rr * rr) * jnp.mean(w * yz, axis=-1, keepdims=True)
            dp_ref[:, _lanes(g, GW, CONVD)] = (dyz * y * dsilu_sz).astype(bf16)
            dy_g = dyz * silu_sz
            ddk_ref[:, gcols] += jnp.sum(dy_g * xs, axis=0, keepdims=True)

            back = lax.rem(CH - HPG * g, CH)
            dt = pltpu.roll(dt_all, back, 1)
            acs = pltpu.roll(acs_all, back, 1)
            cbm = _dot_nt(cg, bg)
            d_cb = jnp.zeros((CH, CH), f32)
            d_bg = jnp.zeros((CH, NST), f32)
            d_cg = jnp.zeros((CH, NST), f32)
            lastrow = jnp.zeros((1, CH), f32)
            dxd_parts, dxs_parts, t_parts = [], [], []
            for r in range(HPG):
                h = g * HPG + r
                colb = jnp.broadcast_to(acs[:, r:r + 1], (CH, CH))
                row = acst_s[pl.ds(h, 1), :]
                lmat = jnp.exp(jnp.where(mask, colb - row, -jnp.inf))
                mmat_b = (cbm * lmat).astype(bf16)
                dtc = dt[:, r:r + 1]
                xd = xs[:, r * HD:(r + 1) * HD] * dtc
                col = colb[:, 0:HD]
                alast = colb[CH - 1:CH, 0:HD]
                dte = jnp.exp(alast - col)
                ea = jnp.exp(col)
                cd = jnp.exp(colb[CH - 1:CH, :])
                sp = st_ref[0, h]
                dsn = dstate[h]
                dsn_b = dsn.astype(bf16)
                dyr = dy_g[:, r * HD:(r + 1) * HD]
                dyr_b = dyr.astype(bf16)
                dye_b = (dyr * ea).astype(bf16)
                d_cg = d_cg + _dot(dye_b, sp)
                dxde = _dot_nt(bg, dsn_b)
                xdte = xd * dte
                xd_b = xd.astype(bf16)
                d_bg = d_bg + _dot(xdte.astype(bf16), dsn_b)
                dxd_diag = _dot_tn(mmat_b, dyr_b)
                dxd = dxde * dte + dxd_diag
                d_cb = d_cb + _dot_nt(dyr_b, xd_b) * lmat
                t_parts.append(dyr_b.astype(f32) * _dot(mmat_b, xd_b) + dyr * (ea * _dot_nt(cg, sp))
                               - xd_b.astype(f32) * dxd_diag - dxde * xdte)
                lastrow = jnp.where(lane1 == r, _sum_all(dsn * sp.astype(f32)) * cd + _sum_all(dxde * xdte), lastrow)
                dstate[h] = cd * dsn + _dot_tn(dye_b, cg)
                dxd_parts.append(dxd)
                dxs_parts.append(dxd * dtc)
            d_cb_b = d_cb.astype(bf16)
            d_bg = d_bg + _dot_tn(d_cb_b, cg)
            d_cg = d_cg + _dot(d_cb_b, bg)
            q2 = ind_sum(jnp.concatenate(dxd_parts, axis=1) * xs)
            dacs = ind_sum(jnp.concatenate(t_parts, axis=1)) + jnp.where(is_last, lastrow, 0.0)
            dacs_acc[...] += pltpu.roll(dacs, HPG * g, 1)
            q2_acc[...] += pltpu.roll(q2, HPG * g, 1)
            dxs = jnp.concatenate(dxs_parts, axis=1) + dy_g * dkc_ref[:, gcols]

            dconv = jnp.concatenate([dxs, d_bg, d_cg], axis=1) * (sg * (1.0 + conv * (1.0 - sg)))
            x = xbc_ref[:, cols]
            dcb_ref[:, cols] += jnp.sum(dconv, axis=0, keepdims=True)
            dcw_ref[3:4, cols] += jnp.sum(dconv * x, axis=0, keepdims=True)
            dx = cw_ref[3:4, cols] * dconv
            nxt = dnext8[:, cols]
            for j in (1, 2, 3):
                up = _shift_rows_up(dconv, nxt, j)
                dcw_ref[3 - j:4 - j, cols] += jnp.sum(up * x, axis=0, keepdims=True)
                dx = dx + cw_ref[3 - j:4 - j, cols] * up
            dnext8[:, cols] = dconv[0:8]
            dp_ref[:, cols] = dx.astype(bf16)
            return carry

        lax.fori_loop(0, NG, group, 0)

        dadt = _dot_hi(tri_t, dacs_acc[...])
        dal_ref[...] += jnp.sum(dadt * adt_all, axis=0, keepdims=True)
        ddz = (dadt * a_ref[...] + q2_acc[...]) * _sigmoid(z_all)
        ddtb_ref[...] += jnp.sum(ddz, axis=0, keepdims=True)
        dp_ref[:, CONVD + DIN:CONVD + DIN + CH] = ddz.astype(bf16)
        dp_ref[:, CONVD + DIN + CH:PB] = jnp.zeros((CH, PB - CONVD - DIN - CH), bf16)

    const2 = lambda c: (0, 0)
    rev = lambda c: (nc - 1 - c, 0)
    return pl.pallas_call(
        body, grid=(nc,),
        in_specs=[pl.BlockSpec((CH, CONVD), rev), pl.BlockSpec((CH, CONVD), rev),
                  pl.BlockSpec((CH, DIN), lambda c: (nc - 1 - c, CONVD // DIN)),
                  pl.BlockSpec((CH, 256), lambda c: (nc - 1 - c, (CONVD + DIN) // 256)),
                  pl.BlockSpec((CH, DIN), rev), pl.BlockSpec((CH, DIN), rev),
                  pl.BlockSpec((1, NH, HD, NST), lambda c: (nc - 1 - c, 0, 0, 0)),
                  pl.BlockSpec((4, CONVD), const2), pl.BlockSpec((1, CH), const2), pl.BlockSpec((1, CH), const2),
                  pl.BlockSpec((1, DIN), const2), pl.BlockSpec((1, DIN), const2), pl.BlockSpec((GW, CH), const2)],
        out_specs=[pl.BlockSpec((CH, PB), rev), pl.BlockSpec((4, CONVD), const2), pl.BlockSpec((1, CONVD), const2),
                   pl.BlockSpec((1, CH), const2), pl.BlockSpec((1, CH), const2), pl.BlockSpec((1, DIN), const2),
                   pl.BlockSpec((1, DIN), const2)],
        out_shape=[jax.ShapeDtypeStruct((s, PB), bf16), jax.ShapeDtypeStruct((4, CONVD), f32),
                   jax.ShapeDtypeStruct((1, CONVD), f32), jax.ShapeDtypeStruct((1, CH), f32),
                   jax.ShapeDtypeStruct((1, CH), f32), jax.ShapeDtypeStruct((1, DIN), f32),
                   jax.ShapeDtypeStruct((1, DIN), f32)],
        scratch_shapes=[pltpu.VMEM((NH, HD, NST), f32), pltpu.VMEM((8, CONVD), f32), pltpu.VMEM((CH, CH), f32),
                        pltpu.VMEM((CH, CH), f32), pltpu.VMEM((CH, CH), f32)],
        name=name, compiler_params=_cp(("arbitrary",)),
    )(proj_b, conv_sv, proj_b, proj_b, dyb, y_sv, states, cw, dtb, a_row, dkc, snw, ind)


def _merge_fwd(ya, yb, proj_g, x, gate, wa, wb, wo, name):
    s = x.shape[0]
    ts = min(512, s)

    def body(ya_ref, yb_ref, ga_ref, gb_ref, x_ref, gate_ref, wa_ref, wb_ref, wo_ref, xo_ref, pa_ref, pb_ref, mg_ref, o_ref):
        pa = _dot(ya_ref[...], wa_ref[...])
        pb = _dot(yb_ref[...], wb_ref[...])
        mg = (_sigmoid(ga_ref[...]) * pa + _sigmoid(gb_ref[...]) * pb).astype(bf16)
        o = _dot(mg, wo_ref[...])
        xo_ref[...] = x_ref[...] + gate_ref[...] * o
        pa_ref[...] = pa.astype(bf16)
        pb_ref[...] = pb.astype(bf16)
        mg_ref[...] = mg
        o_ref[...] = o.astype(bf16)

    tile = pl.BlockSpec((ts, D), lambda i: (i, 0))
    const = lambda i: (0, 0)
    act = jax.ShapeDtypeStruct((s, D), bf16)
    return pl.pallas_call(
        body, grid=(s // ts,),
        in_specs=[tile, pl.BlockSpec((ts, DIN), lambda i: (i, 0)), tile, pl.BlockSpec((ts, D), lambda i: (i, 1)), tile,
                  pl.BlockSpec((1, D), const), pl.BlockSpec((D, D), const), pl.BlockSpec((DIN, D), const),
                  pl.BlockSpec((D, D), const)],
        out_specs=[tile, tile, tile, tile, tile],
        out_shape=[jax.ShapeDtypeStruct((s, D), f32), act, act, act, act],
        name=name, compiler_params=_cp(("parallel",)),
    )(ya, yb, proj_g, proj_g, x, gate, wa, wb, wo)


def _merge_bwd(dxo, gate, o_sv, pa_sv, pb_sv, proj_g, wo, wa, wb, name):
    s = dxo.shape[0]
    ts = min(512, s)

    def body(dxo_ref, gate_ref, o_ref, pa_ref, pb_ref, ga_ref, gb_ref, wo_ref, wa_ref, wb_ref,
             do_ref, dpa_ref, dpb_ref, dg_ref, dya_ref, dyb_ref, dgate_ref):
        @pl.when(pl.program_id(0) == 0)
        def _():
            dgate_ref[...] = jnp.zeros_like(dgate_ref)
        dxo_v = dxo_ref[...]
        dgate_ref[...] += jnp.sum(dxo_v * o_ref[...].astype(f32), axis=0, keepdims=True)
        do = (dxo_v * gate_ref[...]).astype(bf16)
        do_ref[...] = do
        dmg = _dot_nt(do, wo_ref[...])
        sa = _sigmoid(ga_ref[...])
        sb = _sigmoid(gb_ref[...])
        dpa = (dmg * sa).astype(bf16)
        dpb = (dmg * sb).astype(bf16)
        dpa_ref[...] = dpa
        dpb_ref[...] = dpb
        dg_ref[:, 0:D] = (dmg * pa_ref[...].astype(f32) * sa * (1.0 - sa)).astype(bf16)
        dg_ref[:, D:2 * D] = (dmg * pb_ref[...].astype(f32) * sb * (1.0 - sb)).astype(bf16)
        dya_ref[...] = _dot_nt(dpa, wa_ref[...]).astype(bf16)
        dyb_ref[...] = _dot_nt(dpb, wb_ref[...]).astype(bf16)

    tile = pl.BlockSpec((ts, D), lambda i: (i, 0))
    const = lambda i: (0, 0)
    act = jax.ShapeDtypeStruct((s, D), bf16)
    return pl.pallas_call(
        body, grid=(s // ts,),
        in_specs=[tile, pl.BlockSpec((1, D), const), tile, tile, tile, tile, pl.BlockSpec((ts, D), lambda i: (i, 1)),
                  pl.BlockSpec((D, D), const), pl.BlockSpec((D, D), const), pl.BlockSpec((DIN, D), const)],
        out_specs=[tile, tile, tile, pl.BlockSpec((ts, PG), lambda i: (i, 0)), tile, pl.BlockSpec((ts, DIN), lambda i: (i, 0)),
                   pl.BlockSpec((1, D), const)],
        out_shape=[act, act, act, jax.ShapeDtypeStruct((s, PG), bf16), act, jax.ShapeDtypeStruct((s, DIN), bf16),
                   jax.ShapeDtypeStruct((1, D), f32)],
        name=name, compiler_params=_cp(("arbitrary",)),
    )(dxo, gate, o_sv, pa_sv, pb_sv, proj_g, proj_g, wo, wa, wb)


def _final_loss(x, target, fnw, name):
    s = x.shape[0]
    ts = min(512, s)

    def body(x_ref, t_ref, w_ref, loss_ref, dx_ref, dw_ref):
        @pl.when(pl.program_id(0) == 0)
        def _():
            loss_ref[...] = jnp.zeros_like(loss_ref)
            dw_ref[...] = jnp.zeros_like(dw_ref)
        xv = x_ref[...]
        r = lax.rsqrt(jnp.mean(xv * xv, axis=-1, keepdims=True) + EPS)
        xn = xv * r
        err = xn * w_ref[...] - t_ref[...]
        part = jnp.sum(err * err, axis=0, keepdims=True)
        acc = part[:, 0:128]
        for k in range(1, D // 128):
            acc = acc + part[:, k * 128:(k + 1) * 128]
        loss_ref[0:1, :] += acc * (0.5 / D)
        dy = err * (1.0 / D)
        dw_ref[...] += jnp.sum(dy * xn, axis=0, keepdims=True)
        dxn = dy * w_ref[...]
        dx_ref[...] = r * (dxn - xn * jnp.mean(dxn * xn, axis=-1, keepdims=True))

    tile = pl.BlockSpec((ts, D), lambda i: (i, 0))
    row = pl.BlockSpec((1, D), lambda i: (0, 0))
    return pl.pallas_call(
        body, grid=(s // ts,), in_specs=[tile, tile, row],
        out_specs=[pl.BlockSpec((8, 128), lambda i: (0, 0)), tile, row],
        out_shape=[jax.ShapeDtypeStruct((8, 128), f32), jax.ShapeDtypeStruct((s, D), f32), jax.ShapeDtypeStruct((1, D), f32)],
        name=name, compiler_params=_cp(("arbitrary",)),
    )(x, target, fnw)


def _layer_operands(w_in, conv_w, wa, wb, wo, norm_w, gm_ln_w, gm_ln_b, gm_ws, gm_bs, conv_b, dt_bias, a_log, d_skip, ssm_norm_w):
    w_xbc = _group_major(w_in[:, 5120:7168], w_in[:, 7168:8192], w_in[:, 8192:9216])
    w_b = jnp.concatenate([w_xbc, w_in[:, 3072:5120], w_in[:, 9216:9248], jnp.zeros((D, 224), bf16)], axis=1)
    w_a = w_in[:, 0:3072]
    w_g = w_in[:, 9248:N_IN]
    tril = jnp.tril(jnp.ones((CH, CH), bool))
    wsm = jnp.where(tril[None], gm_ws, 0.0).astype(bf16)

    def heads_row(v):
        return jnp.pad(v, (0, CH - NH)).reshape(1, CH)

    return dict(
        w_b=w_b, w_a=w_a, w_g=w_g, wa=wa, wb=wb, wo=wo,
        norm_w=norm_w.reshape(1, D), lnw=gm_ln_w.reshape(1, D), lnb=gm_ln_b.reshape(1, D),
        wsm=wsm, wsm_t=jnp.swapaxes(wsm, 1, 2), bsf=jnp.repeat(gm_bs.T, CH, axis=1),
        cw=_group_major(conv_w[:, 0:DIN], conv_w[:, DIN:DIN + NG * NST], conv_w[:, DIN + NG * NST:CONVD]),
        cb=_group_major(conv_b[0:DIN], conv_b[DIN:DIN + NG * NST], conv_b[DIN + NG * NST:CONVD]).reshape(1, CONVD),
        dtb=heads_row(dt_bias), a_row=heads_row(-jnp.exp(a_log)),
        snw=ssm_norm_w.reshape(1, DIN), dkc=jnp.repeat(d_skip, HD).reshape(1, DIN),
        ind=(jnp.arange(GW)[:, None] // HD == jnp.arange(CH)[None, :]).astype(bf16),
    )


def _layer_fwd(x, shift, scale, gate, p, tag):
    h = _rmsmod_fwd(x, p["norm_w"], scale, shift, f"rmsmod_fwd{tag}")
    proj_b = _mm(h, p["w_b"], f32, f"proj_b{tag}")
    proj_a = _mm(h, p["w_a"], f32, f"proj_a{tag}")
    proj_g = _mm(h, p["w_g"], f32, f"proj_g{tag}")
    ya = _branch_a_fwd(proj_a, p["lnw"], p["lnb"], p["wsm"], p["bsf"], f"branch_a_fwd{tag}")
    yb, y_sv, states, conv_sv = _branch_b_fwd(proj_b, p["cw"], p["cb"], p["dtb"], p["a_row"], p["dkc"], p["snw"], f"branch_b_fwd{tag}")
    x_out, pa, pb, mg, o = _merge_fwd(ya, yb, proj_g, x, gate, p["wa"], p["wb"], p["wo"], f"merge_fwd{tag}")
    saved = dict(x=x, h=h, proj_b=proj_b, proj_a=proj_a, proj_g=proj_g, ya=ya, yb=yb, y=y_sv, states=states, conv=conv_sv,
                 pa=pa, pb=pb, mg=mg, o=o, scale=scale, gate=gate)
    return x_out, saved


def _layer_bwd(dxo, sv, p, tag):
    do, dpa, dpb, dg, dya, dyb, dgate = _merge_bwd(dxo, sv["gate"], sv["o"], sv["pa"], sv["pb"], sv["proj_g"],
                                                   p["wo"], p["wa"], p["wb"], f"merge_bwd{tag}")
    d_wo = _mm_tn(sv["mg"], do, f"d_wo{tag}")
    d_wa = _mm_tn(sv["ya"], dpa, f"d_wa{tag}")
    d_wb = _mm_tn(sv["yb"], dpb, f"d_wb{tag}")
    da, dws, dbs, dlnw, dlnb = _branch_a_bwd(sv["proj_a"], dya, p["lnw"], p["lnb"], p["wsm"], p["wsm_t"], p["bsf"],
                                             f"branch_a_bwd{tag}")
    db, dcw, dcb, ddtb, dal, ddk, dsnw = _branch_b_bwd(sv["proj_b"], sv["conv"], dyb, sv["y"], sv["states"], p["cw"], p["dtb"],
                                                       p["a_row"], p["dkc"], p["snw"], p["ind"], f"branch_b_bwd{tag}")
    dh = _mm(db, p["w_b"], f32, f"dh_b{tag}", trans_b=True)
    dh = _mm(da, p["w_a"], f32, f"dh_a{tag}", c_in=dh, trans_b=True)
    dh = _mm(dg, p["w_g"], f32, f"dh_g{tag}", c_in=dh, trans_b=True)
    d_w_b_t = _mm_tn(db, sv["h"], f"d_w_b{tag}")
    d_w_a_t = _mm_tn(da, sv["h"], f"d_w_a{tag}")
    d_w_g_t = _mm_tn(dg, sv["h"], f"d_w_g{tag}")
    dx, dscale, dshift, dnw = _rmsmod_bwd(dh, sv["x"], dxo, p["norm_w"], sv["scale"], f"rmsmod_bwd{tag}")
    d_w_in_t = jnp.concatenate([d_w_a_t, d_w_b_t[CONVD:CONVD + DIN], _rows_from_group_major(d_w_b_t[0:CONVD]),
                                d_w_b_t[CONVD + DIN:CONVD + DIN + NH], d_w_g_t], axis=0)
    tril = jnp.tril(jnp.ones((CH, CH), bool))
    heads = lambda v: v[0, 0:NH]
    grads = dict(
        w_in_t=d_w_in_t, w_proj_a=d_wa, w_proj_b=d_wb, w_out=d_wo, conv_w=_from_group_major(dcw), conv_b=_from_group_major(dcb).reshape(CONVD),
        norm_w=dnw.reshape(D), gm_ln_w=dlnw.reshape(D), gm_ln_b=dlnb.reshape(D),
        gm_ws=jnp.where(tril[None], dws, 0.0), gm_bs=dbs.reshape(CH, NG, CH).sum(-1).T,
        dt_bias=heads(ddtb), a_log=heads(dal), d_skip=ddk.reshape(NH, HD).sum(-1), ssm_norm_w=dsnw.reshape(DIN),
        mod=jnp.concatenate([dshift, dscale, dgate], axis=1).reshape(3 * D),
    )
    return dx, grads


def _local_step(x, target, mods, operands_of, fnw):
    saved, layer_ops = [], []
    for l in range(DEPTH):
        shift, scale, gate = mods[l]
        p, x = operands_of(l, x)
        layer_ops.append(p)
        x, sv = _layer_fwd(x, shift, scale, gate, p, f"_l{l}")
        saved.append(sv)
    loss_parts, dx, dfnw = _final_loss(x, target, fnw.reshape(1, D), "final_loss")
    grads = [None] * DEPTH
    for l in reversed(range(DEPTH)):
        dx, grads[l] = _layer_bwd(dx, saved[l], layer_ops[l], f"_l{l}")
    return loss_parts, dx, grads, dfnw.reshape(D)


ADA_COLS = 3 * D // NSHARD


def _ada_fwd(c_all, ada_w, ada_b_cols, name):
    def body(c_ref, w_ref, b_ref, o_ref):
        cv = c_ref[...]
        sc = cv * _sigmoid(cv)
        for l in range(DEPTH):
            o_ref[l] = _dot_hi(sc, w_ref[l]) + b_ref[l]

    return pl.pallas_call(body, out_shape=jax.ShapeDtypeStruct((DEPTH, 8, ADA_COLS), f32), name=name,
                          compiler_params=_cp(None))(c_all, ada_w, ada_b_cols)


def _adam_math(w, g, m, v):
    m = ADAM_B1 * m + (1.0 - ADAM_B1) * g
    v = ADAM_B2 * v + (1.0 - ADAM_B2) * (g * g)
    m_hat = m / (1.0 - ADAM_B1 ** ADAM_STEP)
    v_hat = v / (1.0 - ADAM_B2 ** ADAM_STEP)
    delta = -ADAM_LR * (m_hat / (jnp.sqrt(v_hat) + ADAM_EPS) + ADAM_WD * w)
    return delta, m, v


def _ada_bwd_adamw(c_all, dmod_cols, w, m, v, name):
    tr = 256

    def body(c_ref, dm_ref, w_ref, m_ref, v_ref, g_ref, d_ref, nm_ref, nv_ref):
        cv = c_ref[...]
        sc = cv * _sigmoid(cv)
        g = lax.dot_general(sc, dm_ref[0], (((0,), (0,)), ((), ())), precision=_HI, preferred_element_type=f32)
        g_ref[0] = g
        d_ref[0], nm_ref[0], nv_ref[0] = _adam_math(w_ref[0], g, m_ref[0], v_ref[0])

    blk = pl.BlockSpec((1, tr, ADA_COLS), lambda l, i: (l, i, 0))
    shp = jax.ShapeDtypeStruct((DEPTH, D, ADA_COLS), f32)
    return pl.pallas_call(
        body, grid=(DEPTH, D // tr),
        in_specs=[pl.BlockSpec((8, tr), lambda l, i: (0, i)), pl.BlockSpec((1, 8, ADA_COLS), lambda l, i: (l, 0, 0)), blk, blk, blk],
        out_specs=[blk, blk, blk, blk], out_shape=[shp, shp, shp, shp], name=name, compiler_params=_cp(("parallel", "parallel")),
    )(c_all, dmod_cols, w, m, v)


def _adamw(w, g, m, v, name):
    def body(w_ref, g_ref, m_ref, v_ref, d_ref, nm_ref, nv_ref):
        d_ref[...], nm_ref[...], nv_ref[...] = _adam_math(w_ref[...], g_ref[...], m_ref[...], v_ref[...])

    shp = jax.ShapeDtypeStruct(w.shape, f32)
    return pl.pallas_call(body, out_shape=[shp] * 3, name=name, compiler_params=_cp(None))(w, g, m, v)


def _tile2(r, c):
    if r <= 256 or r % 256 == 0:
        return _pick(r, (256,)), _pick(c, (1024,))
    return r, 128


def _adamw_layers(w, g_mine, g_other, m, v, c_idx, name):
    _, r, c = w.shape
    tr, tc = _tile2(r, c)

    def body(ci_ref, w_ref, gm_ref, go_ref, m_ref, v_ref, g_ref, d_ref, nm_ref, nv_ref):
        def update(g):
            g_ref[0] = g
            d_ref[0], nm_ref[0], nv_ref[0] = _adam_math(w_ref[0], g, m_ref[0], v_ref[0])

        mine = pl.program_id(0) == ci_ref[0]

        @pl.when(mine)
        def _():
            update(gm_ref[...])

        @pl.when(jnp.logical_not(mine))
        def _():
            update(go_ref[...])

    blk = pl.BlockSpec((1, tr, tc), lambda l, i, j, ci: (l, i, j))
    gblk = pl.BlockSpec((tr, tc), lambda l, i, j, ci: (i, j))
    shp = jax.ShapeDtypeStruct(w.shape, f32)
    return pl.pallas_call(
        body,
        grid_spec=pltpu.PrefetchScalarGridSpec(num_scalar_prefetch=1, grid=(DEPTH, r // tr, c // tc),
                                               in_specs=[blk, gblk, gblk, blk, blk], out_specs=[blk, blk, blk, blk]),
        out_shape=[shp, shp, shp, shp], name=name, compiler_params=_cp(("parallel", "parallel", "parallel")),
    )(c_idx, w, g_mine, g_other, m, v)


_MESH = pl.DeviceIdType.MESH
_AXES = ("x", "y", "c")
_HBM = pl.BlockSpec(memory_space=pltpu.HBM)


def _my_place():
    return tuple(lax.axis_index(a) for a in _AXES)


def _allreduce(buf, axes, name):
    r = buf.shape[0]
    n = len(axes)

    def body(x_ref, o_ref, rbuf, ssem, rsem):
        me = dict(zip(_AXES, _my_place()))
        o_ref[...] = x_ref[...]
        for k, ax in enumerate(axes):
            peer = tuple(1 - me[a] if a == ax else me[a] for a in _AXES)
            cp = pltpu.make_async_remote_copy(src_ref=o_ref, dst_ref=rbuf.at[k], send_sem=ssem.at[k], recv_sem=rsem.at[k],
                                              device_id=peer, device_id_type=_MESH)
            cp.start()
            cp.wait()
            o_ref[...] = o_ref[...] + rbuf[k]

    vm = pl.BlockSpec(memory_space=pltpu.VMEM)
    return pl.pallas_call(
        body, out_shape=jax.ShapeDtypeStruct((r, 128), f32), in_specs=[vm], out_specs=vm,
        scratch_shapes=[pltpu.VMEM((n, r, 128), f32), pltpu.SemaphoreType.DMA((n,)), pltpu.SemaphoreType.DMA((n,))],
        name=name, compiler_params=pltpu.CompilerParams(vmem_limit_bytes=VMEM_LIMIT),
    )(buf)


def _other_chips(x, y):
    return [(1 - x, y), (x, 1 - y), (1 - x, 1 - y)]


def _gather_body(ins, outs, ssem, rsem):
    na = len(ins)
    x, y, c = _my_place()
    k_me = 2 * x + y
    sibling = (x, y, 1 - c)
    chips = _other_chips(x, y)
    slots = [2 * cx + cy for cx, cy in chips]
    half = [r.shape[0] // 2 if r.shape[0] % 32 == 0 else None for r in ins]

    def part(ref, a, core):
        return ref if half[a] is None else ref.at[pl.ds(core * half[a], half[a])]

    def rcopy(a, src, slot, core, to, idx):
        return pltpu.make_async_remote_copy(src_ref=src, dst_ref=part(outs[a].at[slot], a, core), send_sem=ssem.at[idx],
                                            recv_sem=rsem.at[idx], device_id=to, device_id_type=_MESH)

    sent = []
    for j, chip in enumerate(chips):
        for a in range(na):
            cp = rcopy(a, part(ins[a], a, c), k_me, c, (*chip, c), j * na + a)
            cp.start()
            sent.append(cp)
    for j, chip in enumerate(chips):
        for a in range(na):
            rcopy(a, part(ins[a], a, c), slots[j], c, (*chip, c), j * na + a).wait_recv()
            if half[a] is not None:
                cp = rcopy(a, part(outs[a].at[slots[j]], a, c), slots[j], c, sibling, (3 + j) * na + a)
                cp.start()
                sent.append(cp)
    for j in range(3):
        for a in range(na):
            if half[a] is not None:
                rcopy(a, part(ins[a], a, c), slots[j], 1 - c, sibling, (3 + j) * na + a).wait_recv()
    for cp in sent:
        cp.wait_send()


def _gather_layer_behind(shards, name, collective_id):
    na = len(shards)
    hbm = pltpu.MemorySpace.HBM
    ins = [jax.new_ref(s, memory_space=hbm) for s in shards]
    outs = [jax.empty_ref(jax.ShapeDtypeStruct((NSHARD,) + s.shape, s.dtype), memory_space=hbm) for s in shards]

    @pl.kernel(mesh=plsc.ScalarSubcoreMesh(axis_name="sequencer", num_cores=1), name=name,
               scratch_types=(pltpu.SemaphoreType.DMA((6 * na,)), pltpu.SemaphoreType.DMA((6 * na,))),
               compiler_params=pltpu.CompilerParams(collective_id=collective_id))
    def launch(ssem, rsem):
        x, y, c = _my_place()
        barrier = pltpu.get_barrier_semaphore()
        peers = [(*chip, c) for chip in _other_chips(x, y)] + [(x, y, 1 - c)]
        for peer in peers:
            pl.semaphore_signal(barrier, inc=1, device_id=peer, device_id_type=_MESH)
        pl.semaphore_wait(barrier, len(peers))
        _gather_body(ins, outs, ssem, rsem)

    launch()
    return [o[...] for o in outs]


def _with_own(gathered, own):
    xi, yi, _ = _my_place()
    whole = lax.dynamic_update_index_in_dim(gathered, own, 2 * xi + yi, 0)
    return [whole[k] for k in range(NSHARD)]


def _swap_layers(parts, name):
    na = len(parts)

    def body(*refs):
        ins, outs = refs[:2 * na], refs[2 * na:3 * na]
        ssem, rsem = refs[3 * na:]
        x, y, c = _my_place()

        def copy(a, layer):
            return pltpu.make_async_remote_copy(src_ref=ins[2 * a + layer], dst_ref=outs[a], send_sem=ssem.at[a], recv_sem=rsem.at[a],
                                                device_id=(x, y, 1 - c), device_id_type=_MESH)

        for layer in range(DEPTH):
            @pl.when(c == 1 - layer)
            def _():
                for a in range(na):
                    copy(a, layer).start()
        for a in range(na):
            copy(a, 0).wait()

    flat = [p for pair in parts for p in pair]
    return pl.pallas_call(
        body, out_shape=[jax.ShapeDtypeStruct(p0.shape, p0.dtype) for p0, _ in parts], in_specs=[_HBM] * (2 * na),
        out_specs=[_HBM] * na, scratch_shapes=[pltpu.SemaphoreType.DMA((na,)), pltpu.SemaphoreType.DMA((na,))], name=name,
    )(*flat)


def _scatter_shards_behind(sums, name, collective_id):
    na = len(sums)
    hbm = pltpu.MemorySpace.HBM
    ins = [jax.new_ref(p, memory_space=hbm) for p in sums]
    outs = [jax.empty_ref(jax.ShapeDtypeStruct((3,) + p.shape[1:], p.dtype), memory_space=hbm) for p in sums]

    @pl.kernel(mesh=plsc.ScalarSubcoreMesh(axis_name="sequencer", num_cores=1), name=name,
               scratch_types=(pltpu.SemaphoreType.DMA((3 * na,)), pltpu.SemaphoreType.DMA((3 * na,))),
               compiler_params=pltpu.CompilerParams(collective_id=collective_id))
    def launch(ssem, rsem):
        x, y, c = _my_place()
        barrier = pltpu.get_barrier_semaphore()
        chips = _other_chips(x, y)
        for chip in chips:
            pl.semaphore_signal(barrier, inc=1, device_id=(*chip, c), device_id_type=_MESH)
        pl.semaphore_wait(barrier, len(chips))
        cps = []
        for j, chip in enumerate(chips):
            kj = 2 * chip[0] + chip[1]
            for a in range(na):
                cps.append(pltpu.make_async_remote_copy(
                    src_ref=ins[a].at[kj], dst_ref=outs[a].at[j], send_sem=ssem.at[j * na + a], recv_sem=rsem.at[j * na + a],
                    device_id=(*chip, c), device_id_type=_MESH))
        for cp in cps:
            cp.start()
        for cp in cps:
            cp.wait()

    launch()
    return [o[...] for o in outs]


def _share_layers(finals, name):
    na = len(finals)

    def body(*refs):
        ins, outs = refs[:na], refs[na:2 * na]
        ssem, rsem = refs[2 * na:]
        x, y, c = _my_place()
        cps = [pltpu.make_async_remote_copy(src_ref=ins[a], dst_ref=outs[a], send_sem=ssem.at[a], recv_sem=rsem.at[a],
                                            device_id=(x, y, 1 - c), device_id_type=_MESH) for a in range(na)]
        for cp in cps:
            cp.start()
        for cp in cps:
            cp.wait()

    return pl.pallas_call(
        body, out_shape=[jax.ShapeDtypeStruct(p.shape, p.dtype) for p in finals], in_specs=[_HBM] * na, out_specs=[_HBM] * na,
        scratch_shapes=[pltpu.SemaphoreType.DMA((na,)), pltpu.SemaphoreType.DMA((na,))], name=name,
    )(*finals)


def _add_own_layer(part0, part1, recv, c_idx, name):
    ns, r, c = recv.shape
    tr, tc = _tile2(r, c)

    def body(ci_ref, p0_ref, p1_ref, r_ref, o_ref, ob_ref):
        def add(p_ref):
            t = p_ref[...] + r_ref[...]
            o_ref[...] = t
            ob_ref[...] = t.astype(bf16)

        @pl.when(ci_ref[0] == 0)
        def _():
            add(p0_ref)

        @pl.when(ci_ref[0] == 1)
        def _():
            add(p1_ref)

    blk = pl.BlockSpec((1, tr, tc), lambda k, i, j, ci: (k, i, j))
    blk0 = pl.BlockSpec((1, tr, tc), lambda k, i, j, ci: (k * (1 - ci[0]), i * (1 - ci[0]), j * (1 - ci[0])))
    blk1 = pl.BlockSpec((1, tr, tc), lambda k, i, j, ci: (k * ci[0], i * ci[0], j * ci[0]))
    return pl.pallas_call(
        body,
        grid_spec=pltpu.PrefetchScalarGridSpec(num_scalar_prefetch=1, grid=(ns, r // tr, c // tc), in_specs=[blk0, blk1, blk],
                                               out_specs=[blk, blk]),
        out_shape=[jax.ShapeDtypeStruct((ns, r, c), f32), jax.ShapeDtypeStruct((ns, r, c), bf16)], name=name,
        compiler_params=_cp(("arbitrary", "arbitrary", "arbitrary")),
    )(c_idx, part0, part1, recv)


def _add_own_shard(sums, recv, k_idx, name):
    _, r, c = sums.shape
    tr, tc = _tile2(r, c)

    def body(ki_ref, s_ref, r_ref, o_ref):
        o_ref[...] = ((s_ref[0] + r_ref[0].astype(f32)) + r_ref[1].astype(f32)) + r_ref[2].astype(f32)

    return pl.pallas_call(
        body,
        grid_spec=pltpu.PrefetchScalarGridSpec(
            num_scalar_prefetch=1, grid=(r // tr, c // tc),
            in_specs=[pl.BlockSpec((1, tr, tc), lambda i, j, ki: (ki[0], i, j)), pl.BlockSpec((3, tr, tc), lambda i, j, ki: (0, i, j))],
            out_specs=pl.BlockSpec((tr, tc), lambda i, j, ki: (i, j))),
        out_shape=jax.ShapeDtypeStruct((r, c), f32), name=name, compiler_params=_cp(("parallel", "parallel")),
    )(k_idx, sums, recv)


def _reduce_scatter(parts, tag, meanwhile):
    x, y, c = _my_place()
    c_idx = jnp.reshape(c, (1,)).astype(jnp.int32)
    k_idx = jnp.reshape(2 * x + y, (1,)).astype(jnp.int32)
    na = len(parts)
    recv = _swap_layers(parts, f"rs_swap{tag}")
    sums = [_add_own_layer(parts[a][0], parts[a][1], recv[a], c_idx, f"rs_add_layer{tag}_{a}") for a in range(na)]
    recv = _scatter_shards_behind([sb for _, sb in sums], f"rs_scatter{tag}", 3)
    recv, side = lax.optimization_barrier((recv, meanwhile()))
    finals = [_add_own_shard(sums[a][0], recv[a], k_idx, f"rs_add_shard{tag}_{a}") for a in range(na)]
    return finals, _share_layers(finals, f"rs_share{tag}"), c_idx, side


_SMALL = [("ada_b", (DEPTH, 3 * D)), ("norm_w", (DEPTH, D)), ("gm_ln_w", (DEPTH, D)), ("gm_ln_b", (DEPTH, D)),
          ("gm_ws", (DEPTH, NG, CH, CH)), ("gm_bs", (DEPTH, NG, CH)), ("conv_b", (DEPTH, CONVD)), ("dt_bias", (DEPTH, NH)),
          ("a_log", (DEPTH, NH)), ("d_skip", (DEPTH, NH)), ("ssm_norm_w", (DEPTH, DIN)), ("final_norm_w", (D,))]


def _rows_of(shape):
    n = 1
    for d in shape:
        n *= d
    return -(-n // 1024) * 8


def _pack(arrays):
    rows = []
    for a in arrays:
        flat = a.reshape(-1)
        r = _rows_of(a.shape)
        rows.append(jnp.pad(flat, (0, r * 128 - flat.shape[0])).reshape(r, 128))
    return jnp.concatenate(rows, axis=0)


def _unpack(buf, shapes):
    out, at = [], 0
    for shp in shapes:
        r = _rows_of(shp)
        n = 1
        for d in shp:
            n *= d
        out.append(buf[at:at + r].reshape(-1)[:n].reshape(shp))
        at += r
    return out


def kernel(x, c, ada_w, ada_b, norm_w, w_in, gm_ln_w, gm_ln_b, gm_ws, gm_bs, conv_w, conv_b, dt_bias, a_log, d_skip, ssm_norm_w, w_proj_a, w_proj_b, w_out, final_norm_w, loss_target, m_ada_w, m_ada_b, m_norm_w, m_w_in, m_gm_ln_w, m_gm_ln_b, m_gm_ws, m_gm_bs, m_conv_w, m_conv_b, m_dt_bias, m_a_log, m_d_skip, m_ssm_norm_w, m_w_proj_a, m_w_proj_b, m_w_out, m_final_norm_w, v_ada_w, v_ada_b, v_norm_w, v_w_in, v_gm_ln_w, v_gm_ln_b, v_gm_ws, v_gm_bs, v_conv_w, v_conv_b, v_dt_bias, v_a_log, v_d_skip, v_ssm_norm_w, v_w_proj_a, v_w_proj_b, v_w_out, v_final_norm_w):
    xi, yi, ci = _my_place()
    k_me = 2 * xi + yi
    b_me = 4 * xi + 2 * yi + ci
    w = dict(ada_b=ada_b, norm_w=norm_w, gm_ln_w=gm_ln_w, gm_ln_b=gm_ln_b, gm_ws=gm_ws, gm_bs=gm_bs, conv_b=conv_b, dt_bias=dt_bias,
             a_log=a_log, d_skip=d_skip, ssm_norm_w=ssm_norm_w, final_norm_w=final_norm_w)
    m = dict(ada_b=m_ada_b, norm_w=m_norm_w, gm_ln_w=m_gm_ln_w, gm_ln_b=m_gm_ln_b, gm_ws=m_gm_ws, gm_bs=m_gm_bs, conv_b=m_conv_b,
             dt_bias=m_dt_bias, a_log=m_a_log, d_skip=m_d_skip, ssm_norm_w=m_ssm_norm_w, final_norm_w=m_final_norm_w)
    v = dict(ada_b=v_ada_b, norm_w=v_norm_w, gm_ln_w=v_gm_ln_w, gm_ln_b=v_gm_ln_b, gm_ws=v_gm_ws, gm_bs=v_gm_bs, conv_b=v_conv_b,
             dt_bias=v_dt_bias, a_log=v_a_log, d_skip=v_d_skip, ssm_norm_w=v_ssm_norm_w, final_norm_w=v_final_norm_w)

    c_slot = lax.dynamic_update_slice(jnp.zeros((8, D), f32), c, (b_me, 0))
    c_all = _allreduce(c_slot.reshape(64, 128), _AXES, "gather_c").reshape(8, D)
    ada_b_cols = lax.dynamic_slice(ada_b, (0, k_me * ADA_COLS), (DEPTH, ADA_COLS)).reshape(DEPTH, 1, ADA_COLS)
    mod_cols = _ada_fwd(c_all, ada_w, ada_b_cols, "ada_fwd")
    mod_slot = lax.dynamic_update_slice(jnp.zeros((DEPTH, 8, 3 * D), f32), mod_cols, (0, 0, k_me * ADA_COLS))
    mod_all = _allreduce(mod_slot.reshape(-1, 128), ("x", "y"), "gather_mod").reshape(DEPTH, 8, 3 * D)
    mod_me = lax.dynamic_slice(mod_all, (0, b_me, 0), (DEPTH, 1, 3 * D))
    mods = [(mod_me[l, :, 0:D], mod_me[l, :, D:2 * D], mod_me[l, :, 2 * D:3 * D]) for l in range(DEPTH)]

    rows_sh = jnp.concatenate([w_proj_a, w_proj_b, w_out], axis=1).astype(bf16)
    win_sh = w_in.astype(bf16)
    first = _gather_layer_behind([win_sh[0], rows_sh[0], conv_w[0]], "gather_l0", 1)
    first, mods, later = lax.optimization_barrier((first, mods, [win_sh[1], rows_sh[1], conv_w[1]]))
    others = [first, _gather_layer_behind(later, "gather_l1", 2)]

    def operands_of(l, x_in):
        gathered = others[l]
        if l == 1:
            gathered, x_in = lax.optimization_barrier((gathered, x_in))
        win_g, rows_g, conv_g = gathered
        rows_l = _with_own(rows_g, rows_sh[l])
        w_in_l = jnp.concatenate(_with_own(win_g, win_sh[l]), axis=1)
        wa_l = jnp.concatenate([t[0:256] for t in rows_l], axis=0)
        wb_l = jnp.concatenate([t[256:768] for t in rows_l], axis=0)
        wo_l = jnp.concatenate([t[768:1024] for t in rows_l], axis=0)
        cw_l = jnp.concatenate(_with_own(conv_g, conv_w[l]), axis=1)
        return _layer_operands(w_in_l, cw_l, wa_l, wb_l, wo_l, norm_w[l], gm_ln_w[l], gm_ln_b[l], gm_ws[l], gm_bs[l], conv_b[l],
                               dt_bias[l], a_log[l], d_skip[l], ssm_norm_w[l]), x_in

    loss_parts, dx, grads, dfnw = _local_step(x[0], loss_target[0], mods, operands_of, final_norm_w)

    s_in = N_IN // NSHARD
    tr_ = lambda t: jnp.swapaxes(t, 1, 2)
    g_in = [grads[l]["w_in_t"].reshape(NSHARD, s_in, D) for l in range(DEPTH)]
    g_rows = [jnp.concatenate([grads[l]["w_proj_a"].reshape(NSHARD, 256, D), grads[l]["w_proj_b"].reshape(NSHARD, 512, D),
                               grads[l]["w_out"].reshape(NSHARD, 256, D)], axis=1) for l in range(DEPTH)]
    g_conv = [grads[l]["conv_w"].reshape(4, NSHARD, D).transpose(1, 0, 2) for l in range(DEPTH)]

    dmod_slot = lax.dynamic_update_slice(jnp.zeros((DEPTH, 8, 3 * D), f32),
                                         jnp.stack([grads[l]["mod"] for l in range(DEPTH)]).reshape(DEPTH, 1, 3 * D), (0, b_me, 0))
    small_g = {n: (dfnw if n == "final_norm_w" else jnp.stack([grads[l]["mod" if n == "ada_b" else n] for l in range(DEPTH)]))
               for n, _ in _SMALL}
    n_small = sum(_rows_of(s) for _, s in _SMALL)
    n_dmod = _rows_of(dmod_slot.shape)

    def small_work():
        packed = _allreduce(_pack([small_g[n] for n, _ in _SMALL] + [dmod_slot, loss_parts]), _AXES, "allreduce_small")
        dmod_all = packed[n_small:n_small + n_dmod].reshape(DEPTH, 8, 3 * D)
        dmod_cols = lax.dynamic_slice(dmod_all, (0, 0, k_me * ADA_COLS), (DEPTH, 8, ADA_COLS))
        return packed, _ada_bwd_adamw(c_all, dmod_cols, ada_w, m_ada_w, v_ada_w, "ada_bwd_adamw")

    (f_in, f_rows, f_conv), (o_in, o_rows, o_conv), c_idx, (packed, ada_out) = _reduce_scatter([g_in, g_rows, g_conv], "", small_work)
    g_ada, d_ada, nm_ada, nv_ada = ada_out
    gr_in, d_in, nm_in, nv_in = [tr_(t) for t in _adamw_layers(tr_(w_in), f_in, o_in, tr_(m_w_in), tr_(v_w_in), c_idx, "adamw_w_in")]
    cat = lambda a, b, c_: jnp.concatenate([a, b, c_], axis=1)
    gr_rows, d_rows, nm_rows, nv_rows = _adamw_layers(cat(w_proj_a, w_proj_b, w_out), f_rows, o_rows, cat(m_w_proj_a, m_w_proj_b, m_w_out),
                                                      cat(v_w_proj_a, v_w_proj_b, v_w_out), c_idx, "adamw_rows")
    gr_conv = jnp.where(ci == 0, jnp.stack([f_conv, o_conv]), jnp.stack([o_conv, f_conv]))
    split = lambda t: (t[:, 0:256], t[:, 256:768], t[:, 768:1024])

    g_small = packed[0:n_small]
    loss = jnp.sum(packed[n_small + n_dmod:])
    small_gw = jnp.concatenate([g_small, _pack([gr_conv])], axis=0)
    d_s, nm_s, nv_s = _adamw(_pack([w[n] for n, _ in _SMALL] + [conv_w]), small_gw,
                             _pack([m[n] for n, _ in _SMALL] + [m_conv_w]), _pack([v[n] for n, _ in _SMALL] + [v_conv_w]), "adamw_small")
    shapes = [s for _, s in _SMALL] + [conv_w.shape]
    names = [n for n, _ in _SMALL] + ["conv_w"]
    g_d = dict(zip(names, _unpack(small_gw, shapes)))
    d_d = dict(zip(names, _unpack(d_s, shapes)))
    nm_d = dict(zip(names, _unpack(nm_s, shapes)))
    nv_d = dict(zip(names, _unpack(nv_s, shapes)))

    def by_name(big, small):
        ga, gb, go = split(big[1])
        return dict(small, ada_w=big[2], w_in=big[0], w_proj_a=ga, w_proj_b=gb, w_out=go)

    order = ["ada_w", "ada_b", "norm_w", "w_in", "gm_ln_w", "gm_ln_b", "gm_ws", "gm_bs", "conv_w", "conv_b", "dt_bias", "a_log",
             "d_skip", "ssm_norm_w", "w_proj_a", "w_proj_b", "w_out", "final_norm_w"]
    outs = []
    for big, small in (((gr_in, gr_rows, g_ada), g_d), ((d_in, d_rows, d_ada), d_d), ((nm_in, nm_rows, nm_ada), nm_d),
                       ((nv_in, nv_rows, nv_ada), nv_d)):
        t = by_name(big, small)
        outs += [t[n] for n in order]
    return (loss, dx.reshape(1, -1, D), *outs)
```

```python
import jax
import jax.numpy as jnp
from jax import lax
from jax.experimental import pallas as pl
from jax.experimental.pallas import tpu as pltpu
from jax.experimental.pallas import tpu_sc as plsc

f32 = jnp.float32
bf16 = jnp.bfloat16

D = 1024
DEPTH = 2
EPS = 1e-6
CH = 128
NG = 8
HPG = 4
HD = 64
NH = NG * HPG
NST = 128
DIN = 2048
CONVD = 4096
GW = DIN // NG
PB = CONVD + DIN + 256
PA = 3 * D
PG = 2 * D
N_IN = 11296
NSHARD = 4
V7X_VMEM_BYTES = 64 * 2 ** 20
VMEM_LIMIT = V7X_VMEM_BYTES - 8 * 2 ** 20

ADAM_LR, ADAM_B1, ADAM_B2, ADAM_EPS, ADAM_WD, ADAM_STEP = 0.001, 0.9, 0.999, 1e-08, 0.01, 10

_HI = lax.Precision.HIGHEST


def _cp(sem):
    return pltpu.CompilerParams(dimension_semantics=sem, vmem_limit_bytes=VMEM_LIMIT)


def _sigmoid(x):
    return 0.5 * jnp.tanh(0.5 * x) + 0.5


def _silu_and_grad(x):
    s = _sigmoid(x)
    return x * s, s * (1.0 + x * (1.0 - s))


_GELU_K = 0.7978845608028654
_GELU_C = 0.044715


def _gelu_and_grad(x):
    x2 = x * x
    t = jnp.tanh(_GELU_K * (x + _GELU_C * x * x2))
    g = 0.5 * x * (1.0 + t)
    dg = 0.5 * (1.0 + t) + 0.5 * x * (1.0 - t * t) * _GELU_K * (1.0 + 3.0 * _GELU_C * x2)
    return g, dg


def _gelu(x):
    t = jnp.tanh(_GELU_K * (x + _GELU_C * x * x * x))
    return 0.5 * x * (1.0 + t)


def _softplus(x):
    return jnp.maximum(x, 0.0) + jnp.log(1.0 + jnp.exp(-jnp.abs(x)))


def _dot(a, b):
    return jnp.dot(a, b, preferred_element_type=f32)


def _dot_nt(a, b):
    return lax.dot_general(a, b, (((1,), (1,)), ((), ())), preferred_element_type=f32)


def _dot_tn(a, b):
    return lax.dot_general(a, b, (((0,), (0,)), ((), ())), preferred_element_type=f32)


def _dot_hi(a, b):
    return jnp.dot(a, b, precision=_HI, preferred_element_type=f32)


def _rmsmod_fwd(x, nw, scale, shift, name):
    s = x.shape[0]
    ts = min(512, s)

    def body(x_ref, nw_ref, sc_ref, sh_ref, h_ref):
        xv = x_ref[...]
        r = lax.rsqrt(jnp.mean(xv * xv, axis=-1, keepdims=True) + EPS)
        h_ref[...] = ((xv * r) * nw_ref[...] * (1.0 + sc_ref[...]) + sh_ref[...]).astype(bf16)

    row = pl.BlockSpec((1, D), lambda i: (0, 0))
    tile = pl.BlockSpec((ts, D), lambda i: (i, 0))
    return pl.pallas_call(
        body, grid=(s // ts,), in_specs=[tile, row, row, row], out_specs=tile,
        out_shape=jax.ShapeDtypeStruct((s, D), bf16), name=name, compiler_params=_cp(("parallel",)),
    )(x, nw, scale, shift)


def _rmsmod_bwd(dh, x, dres, nw, scale, name):
    s = x.shape[0]
    ts = min(512, s)

    def body(dh_ref, x_ref, dres_ref, nw_ref, sc_ref, dx_ref, dsc_ref, dsh_ref, dnw_ref):
        @pl.when(pl.program_id(0) == 0)
        def _():
            dsc_ref[...] = jnp.zeros_like(dsc_ref)
            dsh_ref[...] = jnp.zeros_like(dsh_ref)
            dnw_ref[...] = jnp.zeros_like(dnw_ref)
        xv = x_ref[...]
        dhv = dh_ref[...]
        r = lax.rsqrt(jnp.mean(xv * xv, axis=-1, keepdims=True) + EPS)
        xn = xv * r
        one_sc = 1.0 + sc_ref[...]
        dsc_ref[...] += jnp.sum(dhv * xn * nw_ref[...], axis=0, keepdims=True)
        dsh_ref[...] += jnp.sum(dhv, axis=0, keepdims=True)
        dnw_ref[...] += jnp.sum(dhv * xn * one_sc, axis=0, keepdims=True)
        dxn = dhv * (nw_ref[...] * one_sc)
        dx_ref[...] = r * (dxn - xn * jnp.mean(dxn * xn, axis=-1, keepdims=True)) + dres_ref[...]

    row = pl.BlockSpec((1, D), lambda i: (0, 0))
    tile = pl.BlockSpec((ts, D), lambda i: (i, 0))
    vec = jax.ShapeDtypeStruct((1, D), f32)
    return pl.pallas_call(
        body, grid=(s // ts,), in_specs=[tile, tile, tile, row, row], out_specs=[tile, row, row, row],
        out_shape=[jax.ShapeDtypeStruct((s, D), f32), vec, vec, vec], name=name, compiler_params=_cp(("arbitrary",)),
    )(dh, x, dres, nw, scale)


def _pick(n, prefs):
    for p in prefs:
        if n % p == 0:
            return p
    return n


def _mm(a, b, out_dtype, name, c_in=None, trans_b=False):
    m, k = a.shape
    n = b.shape[0] if trans_b else b.shape[1]
    tn = _pick(n, (1280, 1024, 512))
    tk = _pick(k, (1280, 1024, 512))
    nk = k // tk
    one_pass = nk == 1 and c_in is None
    tm = _pick(m, (2048, 1024, 512, 256) if one_pass else (1024, 512, 256))
    dot = _dot_nt if trans_b else _dot

    def body(*refs):
        if one_pass:
            a_ref, b_ref, o_ref = refs
            o_ref[...] = dot(a_ref[...], b_ref[...]).astype(out_dtype)
            return
        if c_in is not None:
            a_ref, b_ref, c_ref, o_ref, acc = refs
        else:
            a_ref, b_ref, o_ref, acc = refs
        kk = pl.program_id(2)

        @pl.when(kk == 0)
        def _():
            if c_in is not None:
                acc[...] = c_ref[...]
            else:
                acc[...] = jnp.zeros_like(acc)
        acc[...] += dot(a_ref[...], b_ref[...])

        @pl.when(kk == nk - 1)
        def _():
            o_ref[...] = acc[...].astype(out_dtype)

    b_spec = pl.BlockSpec((tn, tk), lambda j, i, kk: (j, kk)) if trans_b else pl.BlockSpec((tk, tn), lambda j, i, kk: (kk, j))
    in_specs = [pl.BlockSpec((tm, tk), lambda j, i, kk: (i, kk)), b_spec]
    args = [a, b]
    if c_in is not None:
        in_specs.append(pl.BlockSpec((tm, tn), lambda j, i, kk: (i, j)))
        args.append(c_in)
    return pl.pallas_call(
        body, grid=(n // tn, m // tm, nk), in_specs=in_specs, out_specs=pl.BlockSpec((tm, tn), lambda j, i, kk: (i, j)),
        out_shape=jax.ShapeDtypeStruct((m, n), out_dtype), scratch_shapes=[] if one_pass else [pltpu.VMEM((tm, tn), f32)],
        name=name, compiler_params=_cp(("parallel", "parallel", "arbitrary")),
    )(*args)


def _mm_tn(a, b, name):
    t, k1 = a.shape
    n = b.shape[1]
    t1 = _pick(k1, (1280, 1024, 512))
    tn = _pick(n, (1280, 1024, 512))
    tt = _pick(t, (2048, 1024, 512, 256))
    nt = t // tt

    def body(a_ref, b_ref, o_ref):
        tt_i = pl.program_id(2)

        @pl.when(tt_i == 0)
        def _():
            o_ref[...] = jnp.zeros_like(o_ref)
        o_ref[...] += _dot_tn(a_ref[...], b_ref[...])

    return pl.pallas_call(
        body, grid=(k1 // t1, n // tn, nt),
        in_specs=[pl.BlockSpec((tt, t1), lambda i, j, tt_i: (tt_i, i)), pl.BlockSpec((tt, tn), lambda i, j, tt_i: (tt_i, j))],
        out_specs=pl.BlockSpec((t1, tn), lambda i, j, tt_i: (i, j)),
        out_shape=jax.ShapeDtypeStruct((k1, n), f32), name=name,
        compiler_params=_cp(("parallel", "parallel", "arbitrary")),
    )(a, b)


def _ln_stats(v):
    mu = jnp.mean(v, axis=-1, keepdims=True)
    vc = v - mu
    rstd = lax.rsqrt(jnp.mean(vc * vc, axis=-1, keepdims=True) + EPS)
    return vc * rstd, rstd


def _mix(w_ref, vl):
    return jnp.concatenate([_dot(w_ref[g], vl[:, g * CH:(g + 1) * CH]) for g in range(NG)], axis=1)


def _branch_a_fwd(proj_a, lnw, lnb, wsm, bsf, name):
    s = proj_a.shape[0]
    ta = min(256, s)

    def body(pu_ref, pv_ref, pz_ref, lnw_ref, lnb_ref, w_ref, bs_ref, ya_ref):
        for c in range(ta // CH):
            rows = pl.ds(c * CH, CH)
            vh, _ = _ln_stats(_gelu(pv_ref[rows, :]))
            vl = (vh * lnw_ref[...] + lnb_ref[...]).astype(bf16)
            mixed = _mix(w_ref, vl) + bs_ref[...]
            pz = pz_ref[rows, :]
            ya_ref[rows, :] = (_gelu(pu_ref[rows, :]) * mixed * (pz * _sigmoid(pz))).astype(bf16)

    row = pl.BlockSpec((1, D), lambda i: (0, 0))
    return pl.pallas_call(
        body, grid=(s // ta,),
        in_specs=[pl.BlockSpec((ta, D), lambda i: (i, 0)), pl.BlockSpec((ta, D), lambda i: (i, 1)),
                  pl.BlockSpec((ta, D), lambda i: (i, 2)), row, row,
                  pl.BlockSpec((NG, CH, CH), lambda i: (0, 0, 0)), pl.BlockSpec((CH, D), lambda i: (0, 0))],
        out_specs=pl.BlockSpec((ta, D), lambda i: (i, 0)),
        out_shape=jax.ShapeDtypeStruct((s, D), bf16), name=name, compiler_params=_cp(("parallel",)),
    )(proj_a, proj_a, proj_a, lnw, lnb, wsm, bsf)


def _branch_a_bwd(proj_a, dya, lnw, lnb, wsm, wsm_t, bsf, name):
    s = proj_a.shape[0]
    ta = min(256, s)

    def body(pu_ref, pv_ref, pz_ref, dya_ref, lnw_ref, lnb_ref, w_ref, wt_ref, bs_ref,
             dp_ref, dws_ref, dbs_ref, dlnw_ref, dlnb_ref):
        @pl.when(pl.program_id(0) == 0)
        def _():
            dws_ref[...] = jnp.zeros_like(dws_ref)
            dbs_ref[...] = jnp.zeros_like(dbs_ref)
            dlnw_ref[...] = jnp.zeros_like(dlnw_ref)
            dlnb_ref[...] = jnp.zeros_like(dlnb_ref)
        for c in range(ta // CH):
            rows = pl.ds(c * CH, CH)
            u, du = _gelu_and_grad(pu_ref[rows, :])
            v, dv_act = _gelu_and_grad(pv_ref[rows, :])
            zg, dzg = _silu_and_grad(pz_ref[rows, :])
            vh, rstd = _ln_stats(v)
            vl = (vh * lnw_ref[...] + lnb_ref[...]).astype(bf16)
            mixed = _mix(w_ref, vl) + bs_ref[...]
            dy = dya_ref[rows, :].astype(f32)
            dmixed = dy * u * zg
            dp_ref[rows, 0:D] = (dy * mixed * zg * du).astype(bf16)
            dp_ref[rows, 2 * D:3 * D] = (dy * u * mixed * dzg).astype(bf16)
            dmb = dmixed.astype(bf16)
            dbs_ref[...] += dmixed
            dvl = _mix(wt_ref, dmb)
            for g in range(NG):
                cols = slice(g * CH, (g + 1) * CH)
                dws_ref[g] += _dot_nt(dmb[:, cols], vl[:, cols])
            dlnw_ref[...] += jnp.sum(dvl * vh, axis=0, keepdims=True)
            dlnb_ref[...] += jnp.sum(dvl, axis=0, keepdims=True)
            dvh = dvl * lnw_ref[...]
            dv = rstd * (dvh - jnp.mean(dvh, axis=-1, keepdims=True) - vh * jnp.mean(dvh * vh, axis=-1, keepdims=True))
            dp_ref[rows, D:2 * D] = (dv * dv_act).astype(bf16)

    row = pl.BlockSpec((1, D), lambda i: (0, 0))
    wspec = pl.BlockSpec((NG, CH, CH), lambda i: (0, 0, 0))
    full = pl.BlockSpec((CH, D), lambda i: (0, 0))
    return pl.pallas_call(
        body, grid=(s // ta,),
        in_specs=[pl.BlockSpec((ta, D), lambda i: (i, 0)), pl.BlockSpec((ta, D), lambda i: (i, 1)),
                  pl.BlockSpec((ta, D), lambda i: (i, 2)), pl.BlockSpec((ta, D), lambda i: (i, 0)),
                  row, row, wspec, wspec, full],
        out_specs=[pl.BlockSpec((ta, PA), lambda i: (i, 0)), wspec, full, row, row],
        out_shape=[jax.ShapeDtypeStruct((s, PA), bf16), jax.ShapeDtypeStruct((NG, CH, CH), f32),
                   jax.ShapeDtypeStruct((CH, D), f32), jax.ShapeDtypeStruct((1, D), f32), jax.ShapeDtypeStruct((1, D), f32)],
        name=name, compiler_params=_cp(("arbitrary",)),
    )(proj_a, proj_a, proj_a, dya, lnw, lnb, wsm, wsm_t, bsf)


GB = GW + 2 * NST


def _group_major(xs, b, c):
    lead = xs.shape[:-1]
    return jnp.concatenate([xs.reshape(lead + (NG, GW)), b.reshape(lead + (NG, NST)), c.reshape(lead + (NG, NST))],
                           axis=-1).reshape(lead + (CONVD,))


def _from_group_major(t):
    lead = t.shape[:-1]
    t = t.reshape(lead + (NG, GB))
    return jnp.concatenate([t[..., 0:GW].reshape(lead + (DIN,)), t[..., GW:GW + NST].reshape(lead + (NG * NST,)),
                            t[..., GW + NST:GB].reshape(lead + (NG * NST,))], axis=-1)


def _rows_from_group_major(t):
    t = t.reshape(NG, GB, t.shape[-1])
    return jnp.concatenate([t[:, 0:GW].reshape(DIN, -1), t[:, GW:GW + NST].reshape(NG * NST, -1),
                            t[:, GW + NST:GB].reshape(NG * NST, -1)], axis=0)


def _shift_rows(x, prev8, j):
    xr = pltpu.roll(x, j, 0)
    fix = pltpu.roll(prev8, j, 0)
    rid = lax.broadcasted_iota(jnp.int32, (8, x.shape[1]), 0)
    top = jnp.where(rid < j, fix, xr[0:8])
    return jnp.concatenate([top, xr[8:]], axis=0)


def _shift_rows_up(d, next8, j):
    dr = pltpu.roll(d, CH - j, 0)
    fix = pltpu.roll(next8, 8 - j, 0)
    rid = lax.broadcasted_iota(jnp.int32, (8, d.shape[1]), 0)
    bot = jnp.where(rid >= 8 - j, fix, dr[CH - 8:CH])
    return jnp.concatenate([dr[0:CH - 8], bot], axis=0)


def _conv_pre(x, prev8, cw_ref, cb_ref, cols):
    shifted = [_shift_rows(x, prev8, j) for j in (1, 2, 3)]
    conv = cb_ref[:, cols] + cw_ref[3:4, cols] * x
    for j in (1, 2, 3):
        conv = conv + cw_ref[3 - j:4 - j, cols] * shifted[j - 1]
    return conv, shifted


def _tril_mask():
    return lax.broadcasted_iota(jnp.int32, (CH, CH), 0) >= lax.broadcasted_iota(jnp.int32, (CH, CH), 1)


def _sum_all(v):
    return jnp.sum(jnp.sum(v, axis=0, keepdims=True), axis=1, keepdims=True)


def _lanes(g, width, base=0):
    return pl.ds(pl.multiple_of(base + g * width, width), width)


def _branch_b_fwd(proj_b, cw, cb, dtb, a_row, dkc, snw, name):
    s = proj_b.shape[0]
    nc = s // CH

    def body(xbc_ref, sz_ref, dtr_ref, cw_ref, cb_ref, dtb_ref, a_ref, dkc_ref, snw_ref,
             yb_ref, y_ref, st_ref, cv_ref, prev8, state, acst_s):
        @pl.when(pl.program_id(0) == 0)
        def _():
            prev8[...] = jnp.zeros_like(prev8)
            state[...] = jnp.zeros_like(state)
        st_ref[0] = state[...].astype(bf16)
        mask = _tril_mask()
        dt_all = _softplus(dtr_ref[:, 0:CH] + dtb_ref[...])
        acs_all = _dot_hi(mask.astype(f32), dt_all * a_ref[...])
        acst_s[...] = acs_all.T

        def group(g, carry):
            cols = _lanes(g, GB)
            gcols = _lanes(g, GW)
            x = xbc_ref[:, cols]
            conv, _ = _conv_pre(x, prev8[:, cols], cw_ref, cb_ref, cols)
            prev8[:, cols] = x[CH - 8:CH]
            cv_ref[:, cols] = conv.astype(bf16)
            xc = conv * _sigmoid(conv)
            xs = xc[:, 0:GW]
            bg = xc[:, GW:GW + NST].astype(bf16)
            cg = xc[:, GW + NST:GB].astype(bf16)
            back = lax.rem(CH - HPG * g, CH)
            dt = pltpu.roll(dt_all, back, 1)
            acs = pltpu.roll(acs_all, back, 1)
            cbm = _dot_nt(cg, bg)
            dkc_g = dkc_ref[:, gcols]
            y_parts = []
            for r in range(HPG):
                colb = jnp.broadcast_to(acs[:, r:r + 1], (CH, CH))
                row = acst_s[pl.ds(g * HPG + r, 1), :]
                lmat = jnp.exp(jnp.where(mask, colb - row, -jnp.inf))
                xr = xs[:, r * HD:(r + 1) * HD]
                xd = xr * dt[:, r:r + 1]
                sp = state[g * HPG + r]
                col = colb[:, 0:HD]
                alast = colb[CH - 1:CH, 0:HD]
                y_r = _dot((cbm * lmat).astype(bf16), xd.astype(bf16))
                y_r = y_r + jnp.exp(col) * _dot_nt(cg, sp.astype(bf16))
                y_parts.append(y_r + xr * dkc_g[:, r * HD:(r + 1) * HD])
                cs = _dot_tn((xd * jnp.exp(alast - col)).astype(bf16), bg)
                state[g * HPG + r] = jnp.exp(colb[CH - 1:CH, :]) * sp + cs
            y = jnp.concatenate(y_parts, axis=1)
            szv = sz_ref[:, gcols]
            yz = y * (szv * _sigmoid(szv))
            rr = lax.rsqrt(jnp.mean(yz * yz, axis=-1, keepdims=True) + EPS)
            yb_ref[:, gcols] = (yz * rr * snw_ref[:, gcols]).astype(bf16)
            y_ref[:, gcols] = y.astype(bf16)
            return carry

        lax.fori_loop(0, NG, group, 0)

    const2 = lambda c: (0, 0)
    return pl.pallas_call(
        body, grid=(nc,),
        in_specs=[pl.BlockSpec((CH, CONVD), lambda c: (c, 0)), pl.BlockSpec((CH, DIN), lambda c: (c, CONVD // DIN)),
                  pl.BlockSpec((CH, 256), lambda c: (c, (CONVD + DIN) // 256)),
                  pl.BlockSpec((4, CONVD), const2), pl.BlockSpec((1, CONVD), const2),
                  pl.BlockSpec((1, CH), const2), pl.BlockSpec((1, CH), const2),
                  pl.BlockSpec((1, DIN), const2), pl.BlockSpec((1, DIN), const2)],
        out_specs=[pl.BlockSpec((CH, DIN), lambda c: (c, 0)), pl.BlockSpec((CH, DIN), lambda c: (c, 0)),
                   pl.BlockSpec((1, NH, HD, NST), lambda c: (c, 0, 0, 0)), pl.BlockSpec((CH, CONVD), lambda c: (c, 0))],
        out_shape=[jax.ShapeDtypeStruct((s, DIN), bf16), jax.ShapeDtypeStruct((s, DIN), bf16),
                   jax.ShapeDtypeStruct((nc, NH, HD, NST), bf16), jax.ShapeDtypeStruct((s, CONVD), bf16)],
        scratch_shapes=[pltpu.VMEM((8, CONVD), f32), pltpu.VMEM((NH, HD, NST), f32), pltpu.VMEM((CH, CH), f32)],
        name=name, compiler_params=_cp(("arbitrary",)),
    )(proj_b, proj_b, proj_b, cw, cb, dtb, a_row, dkc, snw)


def _branch_b_bwd(proj_b, conv_sv, dyb, y_sv, states, cw, dtb, a_row, dkc, snw, ind, name):
    s = proj_b.shape[0]
    nc = s // CH

    def body(xbc_ref, cv_ref, sz_ref, dtr_ref, dyb_ref, y_ref, st_ref, cw_ref, dtb_ref, a_ref, dkc_ref,
             snw_ref, ind_ref, dp_ref, dcw_ref, dcb_ref, ddtb_ref, dal_ref, ddk_ref, dsnw_ref,
             dstate, dnext8, acst_s, dacs_acc, q2_acc):
        @pl.when(pl.program_id(0) == 0)
        def _():
            dstate[...] = jnp.zeros_like(dstate)
            dnext8[...] = jnp.zeros_like(dnext8)
            dcw_ref[...] = jnp.zeros_like(dcw_ref)
            dcb_ref[...] = jnp.zeros_like(dcb_ref)
            ddtb_ref[...] = jnp.zeros_like(ddtb_ref)
            dal_ref[...] = jnp.zeros_like(dal_ref)
            ddk_ref[...] = jnp.zeros_like(ddk_ref)
            dsnw_ref[...] = jnp.zeros_like(dsnw_ref)

        dacs_acc[...] = jnp.zeros_like(dacs_acc)
        q2_acc[...] = jnp.zeros_like(q2_acc)
        mask = _tril_mask()
        tri_t = (lax.broadcasted_iota(jnp.int32, (CH, CH), 0) <= lax.broadcasted_iota(jnp.int32, (CH, CH), 1)).astype(f32)
        lane1 = lax.broadcasted_iota(jnp.int32, (1, CH), 1)
        is_last = lax.broadcasted_iota(jnp.int32, (CH, 1), 0) == CH - 1
        z_all = dtr_ref[:, 0:CH] + dtb_ref[...]
        dt_all = _softplus(z_all)
        adt_all = dt_all * a_ref[...]
        acs_all = _dot_hi(mask.astype(f32), adt_all)
        acst_s[...] = acs_all.T

        def ind_sum(v):
            hi = v.astype(bf16)
            lo = (v - hi.astype(f32)).astype(bf16)
            return _dot(hi, ind_ref[...]) + _dot(lo, ind_ref[...])

        def group(g, carry):
            cols = _lanes(g, GB)
            gcols = _lanes(g, GW)
            conv = cv_ref[:, cols].astype(f32)
            sg = _sigmoid(conv)
            xc = conv * sg
            xs = xc[:, 0:GW]
            bg = xc[:, GW:GW + NST].astype(bf16)
            cg = xc[:, GW + NST:GB].astype(bf16)

            y = y_ref[:, gcols].astype(f32)
            silu_sz, dsilu_sz = _silu_and_grad(sz_ref[:, gcols])
            yz = y * silu_sz
            rr = lax.rsqrt(jnp.mean(yz * yz, axis=-1, keepdims=True) + EPS)
            dyb_g = dyb_ref[:, gcols].astype(f32)
            w = dyb_g * snw_ref[:, gcols]
            dsnw_ref[:, gcols] += jnp.sum(dyb_g * yz * rr, axis=0, keepdims=True)
            dyz = rr * w - yz * (rr * rr * rr) * jnp.mean(w * yz, axis=-1, keepdims=True)
            dp_ref[:, _lanes(g, GW, CONVD)] = (dyz * y * dsilu_sz).astype(bf16)
            dy_g = dyz * silu_sz
            ddk_ref[:, gcols] += jnp.sum(dy_g * xs, axis=0, keepdims=True)

            back = lax.rem(CH - HPG * g, CH)
            dt = pltpu.roll(dt_all, back, 1)
            acs = pltpu.roll(acs_all, back, 1)
            cbm = _dot_nt(cg, bg)
            d_cb = jnp.zeros((CH, CH), f32)
            d_bg = jnp.zeros((CH, NST), f32)
            d_cg = jnp.zeros((CH, NST), f32)
            lastrow = jnp.zeros((1, CH), f32)
            dxd_parts, dxs_parts, t_parts = [], [], []
            for r in range(HPG):
                h = g * HPG + r
                colb = jnp.broadcast_to(acs[:, r:r + 1], (CH, CH))
                row = acst_s[pl.ds(h, 1), :]
                lmat = jnp.exp(jnp.where(mask, colb - row, -jnp.inf))
                mmat_b = (cbm * lmat).astype(bf16)
                dtc = dt[:, r:r + 1]
                xd = xs[:, r * HD:(r + 1) * HD] * dtc
                col = colb[:, 0:HD]
                alast = colb[CH - 1:CH, 0:HD]
                dte = jnp.exp(alast - col)
                ea = jnp.exp(col)
                cd = jnp.exp(colb[CH - 1:CH, :])
                sp = st_ref[0, h]
                dsn = dstate[h]
                dsn_b = dsn.astype(bf16)
                dyr = dy_g[:, r * HD:(r + 1) * HD]
                dyr_b = dyr.astype(bf16)
                dye_b = (dyr * ea).astype(bf16)
                d_cg = d_cg + _dot(dye_b, sp)
                dxde = _dot_nt(bg, dsn_b)
                xdte = xd * dte
                xd_b = xd.astype(bf16)
                d_bg = d_bg + _dot(xdte.astype(bf16), dsn_b)
                dxd_diag = _dot_tn(mmat_b, dyr_b)
                dxd = dxde * dte + dxd_diag
                d_cb = d_cb + _dot_nt(dyr_b, xd_b) * lmat
                t_parts.append(dyr_b.astype(f32) * _dot(mmat_b, xd_b) + dyr * (ea * _dot_nt(cg, sp))
                               - xd_b.astype(f32) * dxd_diag - dxde * xdte)
                lastrow = jnp.where(lane1 == r, _sum_all(dsn * sp.astype(f32)) * cd + _sum_all(dxde * xdte), lastrow)
                dstate[h] = cd * dsn + _dot_tn(dye_b, cg)
                dxd_parts.append(dxd)
                dxs_parts.append(dxd * dtc)
            d_cb_b = d_cb.astype(bf16)
            d_bg = d_bg + _dot_tn(d_cb_b, cg)
            d_cg = d_cg + _dot(d_cb_b, bg)
            q2 = ind_sum(jnp.concatenate(dxd_parts, axis=1) * xs)
            dacs = ind_sum(jnp.concatenate(t_parts, axis=1)) + jnp.where(is_last, lastrow, 0.0)
            dacs_acc[...] += pltpu.roll(dacs, HPG * g, 1)
            q2_acc[...] += pltpu.roll(q2, HPG * g, 1)
            dxs = jnp.concatenate(dxs_parts, axis=1) + dy_g * dkc_ref[:, gcols]

            dconv = jnp.concatenate([dxs, d_bg, d_cg], axis=1) * (sg * (1.0 + conv * (1.0 - sg)))
            x = xbc_ref[:, cols]
            dcb_ref[:, cols] += jnp.sum(dconv, axis=0, keepdims=True)
            dcw_ref[3:4, cols] += jnp.sum(dconv * x, axis=0, keepdims=True)
            dx = cw_ref[3:4, cols] * dconv
            nxt = dnext8[:, cols]
            for j in (1, 2, 3):
                up = _shift_rows_up(dconv, nxt, j)
                dcw_ref[3 - j:4 - j, cols] += jnp.sum(up * x, axis=0, keepdims=True)
                dx = dx + cw_ref[3 - j:4 - j, cols] * up
            dnext8[:, cols] = dconv[0:8]
            dp_ref[:, cols] = dx.astype(bf16)
            return carry

        lax.fori_loop(0, NG, group, 0)

        dadt = _dot_hi(tri_t, dacs_acc[...])
        dal_ref[...] += jnp.sum(dadt * adt_all, axis=0, keepdims=True)
        ddz = (dadt * a_ref[...] + q2_acc[...]) * _sigmoid(z_all)
        ddtb_ref[...] += jnp.sum(ddz, axis=0, keepdims=True)
        dp_ref[:, CONVD + DIN:CONVD + DIN + CH] = ddz.astype(bf16)
        dp_ref[:, CONVD + DIN + CH:PB] = jnp.zeros((CH, PB - CONVD - DIN - CH), bf16)

    const2 = lambda c: (0, 0)
    rev = lambda c: (nc - 1 - c, 0)
    return pl.pallas_call(
        body, grid=(nc,),
        in_specs=[pl.BlockSpec((CH, CONVD), rev), pl.BlockSpec((CH, CONVD), rev),
                  pl.BlockSpec((CH, DIN), lambda c: (nc - 1 - c, CONVD // DIN)),
                  pl.BlockSpec((CH, 256), lambda c: (nc - 1 - c, (CONVD + DIN) // 256)),
                  pl.BlockSpec((CH, DIN), rev), pl.BlockSpec((CH, DIN), rev),
                  pl.BlockSpec((1, NH, HD, NST), lambda c: (nc - 1 - c, 0, 0, 0)),
                  pl.BlockSpec((4, CONVD), const2), pl.BlockSpec((1, CH), const2), pl.BlockSpec((1, CH), const2),
                  pl.BlockSpec((1, DIN), const2), pl.BlockSpec((1, DIN), const2), pl.BlockSpec((GW, CH), const2)],
        out_specs=[pl.BlockSpec((CH, PB), rev), pl.BlockSpec((4, CONVD), const2), pl.BlockSpec((1, CONVD), const2),
                   pl.BlockSpec((1, CH), const2), pl.BlockSpec((1, CH), const2), pl.BlockSpec((1, DIN), const2),
                   pl.BlockSpec((1, DIN), const2)],
        out_shape=[jax.ShapeDtypeStruct((s, PB), bf16), jax.ShapeDtypeStruct((4, CONVD), f32),
                   jax.ShapeDtypeStruct((1, CONVD), f32), jax.ShapeDtypeStruct((1, CH), f32),
                   jax.ShapeDtypeStruct((1, CH), f32), jax.ShapeDtypeStruct((1, DIN), f32),
                   jax.ShapeDtypeStruct((1, DIN), f32)],
        scratch_shapes=[pltpu.VMEM((NH, HD, NST), f32), pltpu.VMEM((8, CONVD), f32), pltpu.VMEM((CH, CH), f32),
                        pltpu.VMEM((CH, CH), f32), pltpu.VMEM((CH, CH), f32)],
        name=name, compiler_params=_cp(("arbitrary",)),
    )(proj_b, conv_sv, proj_b, proj_b, dyb, y_sv, states, cw, dtb, a_row, dkc, snw, ind)


def _merge_fwd(ya, yb, proj_g, x, gate, wa, wb, wo, name):
    s = x.shape[0]
    ts = min(512, s)

    def body(ya_ref, yb_ref, ga_ref, gb_ref, x_ref, gate_ref, wa_ref, wb_ref, wo_ref, xo_ref, pa_ref, pb_ref, mg_ref, o_ref):
        pa = _dot(ya_ref[...], wa_ref[...])
        pb = _dot(yb_ref[...], wb_ref[...])
        mg = (_sigmoid(ga_ref[...]) * pa + _sigmoid(gb_ref[...]) * pb).astype(bf16)
        o = _dot(mg, wo_ref[...])
        xo_ref[...] = x_ref[...] + gate_ref[...] * o
        pa_ref[...] = pa.astype(bf16)
        pb_ref[...] = pb.astype(bf16)
        mg_ref[...] = mg
        o_ref[...] = o.astype(bf16)

    tile = pl.BlockSpec((ts, D), lambda i: (i, 0))
    const = lambda i: (0, 0)
    act = jax.ShapeDtypeStruct((s, D), bf16)
    return pl.pallas_call(
        body, grid=(s // ts,),
        in_specs=[tile, pl.BlockSpec((ts, DIN), lambda i: (i, 0)), tile, pl.BlockSpec((ts, D), lambda i: (i, 1)), tile,
                  pl.BlockSpec((1, D), const), pl.BlockSpec((D, D), const), pl.BlockSpec((DIN, D), const),
                  pl.BlockSpec((D, D), const)],
        out_specs=[tile, tile, tile, tile, tile],
        out_shape=[jax.ShapeDtypeStruct((s, D), f32), act, act, act, act],
        name=name, compiler_params=_cp(("parallel",)),
    )(ya, yb, proj_g, proj_g, x, gate, wa, wb, wo)


def _merge_bwd(dxo, gate, o_sv, pa_sv, pb_sv, proj_g, wo, wa, wb, name):
    s = dxo.shape[0]
    ts = min(512, s)

    def body(dxo_ref, gate_ref, o_ref, pa_ref, pb_ref, ga_ref, gb_ref, wo_ref, wa_ref, wb_ref,
             do_ref, dpa_ref, dpb_ref, dg_ref, dya_ref, dyb_ref, dgate_ref):
        @pl.when(pl.program_id(0) == 0)
        def _():
            dgate_ref[...] = jnp.zeros_like(dgate_ref)
        dxo_v = dxo_ref[...]
        dgate_ref[...] += jnp.sum(dxo_v * o_ref[...].astype(f32), axis=0, keepdims=True)
        do = (dxo_v * gate_ref[...]).astype(bf16)
        do_ref[...] = do
        dmg = _dot_nt(do, wo_ref[...])
        sa = _sigmoid(ga_ref[...])
        sb = _sigmoid(gb_ref[...])
        dpa = (dmg * sa).astype(bf16)
        dpb = (dmg * sb).astype(bf16)
        dpa_ref[...] = dpa
        dpb_ref[...] = dpb
        dg_ref[:, 0:D] = (dmg * pa_ref[...].astype(f32) * sa * (1.0 - sa)).astype(bf16)
        dg_ref[:, D:2 * D] = (dmg * pb_ref[...].astype(f32) * sb * (1.0 - sb)).astype(bf16)
        dya_ref[...] = _dot_nt(dpa, wa_ref[...]).astype(bf16)
        dyb_ref[...] = _dot_nt(dpb, wb_ref[...]).astype(bf16)

    tile = pl.BlockSpec((ts, D), lambda i: (i, 0))
    const = lambda i: (0, 0)
    act = jax.ShapeDtypeStruct((s, D), bf16)
    return pl.pallas_call(
        body, grid=(s // ts,),
        in_specs=[tile, pl.BlockSpec((1, D), const), tile, tile, tile, tile, pl.BlockSpec((ts, D), lambda i: (i, 1)),
                  pl.BlockSpec((D, D), const), pl.BlockSpec((D, D), const), pl.BlockSpec((DIN, D), const)],
        out_specs=[tile, tile, tile, pl.BlockSpec((ts, PG), lambda i: (i, 0)), tile, pl.BlockSpec((ts, DIN), lambda i: (i, 0)),
                   pl.BlockSpec((1, D), const)],
        out_shape=[act, act, act, jax.ShapeDtypeStruct((s, PG), bf16), act, jax.ShapeDtypeStruct((s, DIN), bf16),
                   jax.ShapeDtypeStruct((1, D), f32)],
        name=name, compiler_params=_cp(("arbitrary",)),
    )(dxo, gate, o_sv, pa_sv, pb_sv, proj_g, proj_g, wo, wa, wb)


def _final_loss(x, target, fnw, name):
    s = x.shape[0]
    ts = min(512, s)

    def body(x_ref, t_ref, w_ref, loss_ref, dx_ref, dw_ref):
        @pl.when(pl.program_id(0) == 0)
        def _():
            loss_ref[...] = jnp.zeros_like(loss_ref)
            dw_ref[...] = jnp.zeros_like(dw_ref)
        xv = x_ref[...]
        r = lax.rsqrt(jnp.mean(xv * xv, axis=-1, keepdims=True) + EPS)
        xn = xv * r
        err = xn * w_ref[...] - t_ref[...]
        part = jnp.sum(err * err, axis=0, keepdims=True)
        acc = part[:, 0:128]
        for k in range(1, D // 128):
            acc = acc + part[:, k * 128:(k + 1) * 128]
        loss_ref[0:1, :] += acc * (0.5 / D)
        dy = err * (1.0 / D)
        dw_ref[...] += jnp.sum(dy * xn, axis=0, keepdims=True)
        dxn = dy * w_ref[...]
        dx_ref[...] = r * (dxn - xn * jnp.mean(dxn * xn, axis=-1, keepdims=True))

    tile = pl.BlockSpec((ts, D), lambda i: (i, 0))
    row = pl.BlockSpec((1, D), lambda i: (0, 0))
    return pl.pallas_call(
        body, grid=(s // ts,), in_specs=[tile, tile, row],
        out_specs=[pl.BlockSpec((8, 128), lambda i: (0, 0)), tile, row],
        out_shape=[jax.ShapeDtypeStruct((8, 128), f32), jax.ShapeDtypeStruct((s, D), f32), jax.ShapeDtypeStruct((1, D), f32)],
        name=name, compiler_params=_cp(("arbitrary",)),
    )(x, target, fnw)


def _layer_operands(w_in, conv_w, wa, wb, wo, norm_w, gm_ln_w, gm_ln_b, gm_ws, gm_bs, conv_b, dt_bias, a_log, d_skip, ssm_norm_w):
    w_xbc = _group_major(w_in[:, 5120:7168], w_in[:, 7168:8192], w_in[:, 8192:9216])
    w_b = jnp.concatenate([w_xbc, w_in[:, 3072:5120], w_in[:, 9216:9248], jnp.zeros((D, 224), bf16)], axis=1)
    w_a = w_in[:, 0:3072]
    w_g = w_in[:, 9248:N_IN]
    tril = jnp.tril(jnp.ones((CH, CH), bool))
    wsm = jnp.where(tril[None], gm_ws, 0.0).astype(bf16)

    def heads_row(v):
        return jnp.pad(v, (0, CH - NH)).reshape(1, CH)

    return dict(
        w_b=w_b, w_a=w_a, w_g=w_g, wa=wa, wb=wb, wo=wo,
        norm_w=norm_w.reshape(1, D), lnw=gm_ln_w.reshape(1, D), lnb=gm_ln_b.reshape(1, D),
        wsm=wsm, wsm_t=jnp.swapaxes(wsm, 1, 2), bsf=jnp.repeat(gm_bs.T, CH, axis=1),
        cw=_group_major(conv_w[:, 0:DIN], conv_w[:, DIN:DIN + NG * NST], conv_w[:, DIN + NG * NST:CONVD]),
        cb=_group_major(conv_b[0:DIN], conv_b[DIN:DIN + NG * NST], conv_b[DIN + NG * NST:CONVD]).reshape(1, CONVD),
        dtb=heads_row(dt_bias), a_row=heads_row(-jnp.exp(a_log)),
        snw=ssm_norm_w.reshape(1, DIN), dkc=jnp.repeat(d_skip, HD).reshape(1, DIN),
        ind=(jnp.arange(GW)[:, None] // HD == jnp.arange(CH)[None, :]).astype(bf16),
    )


def _layer_fwd(x, shift, scale, gate, p, tag):
    h = _rmsmod_fwd(x, p["norm_w"], scale, shift, f"rmsmod_fwd{tag}")
    proj_b = _mm(h, p["w_b"], f32, f"proj_b{tag}")
    proj_a = _mm(h, p["w_a"], f32, f"proj_a{tag}")
    proj_g = _mm(h, p["w_g"], f32, f"proj_g{tag}")
    ya = _branch_a_fwd(proj_a, p["lnw"], p["lnb"], p["wsm"], p["bsf"], f"branch_a_fwd{tag}")
    yb, y_sv, states, conv_sv = _branch_b_fwd(proj_b, p["cw"], p["cb"], p["dtb"], p["a_row"], p["dkc"], p["snw"], f"branch_b_fwd{tag}")
    x_out, pa, pb, mg, o = _merge_fwd(ya, yb, proj_g, x, gate, p["wa"], p["wb"], p["wo"], f"merge_fwd{tag}")
    saved = dict(x=x, h=h, proj_b=proj_b, proj_a=proj_a, proj_g=proj_g, ya=ya, yb=yb, y=y_sv, states=states, conv=conv_sv,
                 pa=pa, pb=pb, mg=mg, o=o, scale=scale, gate=gate)
    return x_out, saved


def _layer_bwd(dxo, sv, p, tag):
    do, dpa, dpb, dg, dya, dyb, dgate = _merge_bwd(dxo, sv["gate"], sv["o"], sv["pa"], sv["pb"], sv["proj_g"],
                                                   p["wo"], p["wa"], p["wb"], f"merge_bwd{tag}")
    d_wo = _mm_tn(sv["mg"], do, f"d_wo{tag}")
    d_wa = _mm_tn(sv["ya"], dpa, f"d_wa{tag}")
    d_wb = _mm_tn(sv["yb"], dpb, f"d_wb{tag}")
    da, dws, dbs, dlnw, dlnb = _branch_a_bwd(sv["proj_a"], dya, p["lnw"], p["lnb"], p["wsm"], p["wsm_t"], p["bsf"],
                                             f"branch_a_bwd{tag}")
    db, dcw, dcb, ddtb, dal, ddk, dsnw = _branch_b_bwd(sv["proj_b"], sv["conv"], dyb, sv["y"], sv["states"], p["cw"], p["dtb"],
                                                       p["a_row"], p["dkc"], p["snw"], p["ind"], f"branch_b_bwd{tag}")
    dh = _mm(db, p["w_b"], f32, f"dh_b{tag}", trans_b=True)
    dh = _mm(da, p["w_a"], f32, f"dh_a{tag}", c_in=dh, trans_b=True)
    dh = _mm(dg, p["w_g"], f32, f"dh_g{tag}", c_in=dh, trans_b=True)
    d_w_b_t = _mm_tn(db, sv["h"], f"d_w_b{tag}")
    d_w_a_t = _mm_tn(da, sv["h"], f"d_w_a{tag}")
    d_w_g_t = _mm_tn(dg, sv["h"], f"d_w_g{tag}")
    dx, dscale, dshift, dnw = _rmsmod_bwd(dh, sv["x"], dxo, p["norm_w"], sv["scale"], f"rmsmod_bwd{tag}")
    d_w_in_t = jnp.concatenate([d_w_a_t, d_w_b_t[CONVD:CONVD + DIN], _rows_from_group_major(d_w_b_t[0:CONVD]),
                                d_w_b_t[CONVD + DIN:CONVD + DIN + NH], d_w_g_t], axis=0)
    tril = jnp.tril(jnp.ones((CH, CH), bool))
    heads = lambda v: v[0, 0:NH]
    grads = dict(
        w_in_t=d_w_in_t, w_proj_a=d_wa, w_proj_b=d_wb, w_out=d_wo, conv_w=_from_group_major(dcw), conv_b=_from_group_major(dcb).reshape(CONVD),
        norm_w=dnw.reshape(D), gm_ln_w=dlnw.reshape(D), gm_ln_b=dlnb.reshape(D),
        gm_ws=jnp.where(tril[None], dws, 0.0), gm_bs=dbs.reshape(CH, NG, CH).sum(-1).T,
        dt_bias=heads(ddtb), a_log=heads(dal), d_skip=ddk.reshape(NH, HD).sum(-1), ssm_norm_w=dsnw.reshape(DIN),
        mod=jnp.concatenate([dshift, dscale, dgate], axis=1).reshape(3 * D),
    )
    return dx, grads


def _local_step(x, target, mods, operands_of, fnw):
    saved, layer_ops = [], []
    for l in range(DEPTH):
        shift, scale, gate = mods[l]
        p, x = operands_of(l, x)
        layer_ops.append(p)
        x, sv = _layer_fwd(x, shift, scale, gate, p, f"_l{l}")
        saved.append(sv)
    loss_parts, dx, dfnw = _final_loss(x, target, fnw.reshape(1, D), "final_loss")
    grads = [None] * DEPTH
    for l in reversed(range(DEPTH)):
        dx, grads[l] = _layer_bwd(dx, saved[l], layer_ops[l], f"_l{l}")
    return loss_parts, dx, grads, dfnw.reshape(D)


ADA_COLS = 3 * D // NSHARD


def _ada_fwd(c_all, ada_w, ada_b_cols, name):
    def body(c_ref, w_ref, b_ref, o_ref):
        cv = c_ref[...]
        sc = cv * _sigmoid(cv)
        for l in range(DEPTH):
            o_ref[l] = _dot_hi(sc, w_ref[l]) + b_ref[l]

    return pl.pallas_call(body, out_shape=jax.ShapeDtypeStruct((DEPTH, 8, ADA_COLS), f32), name=name,
                          compiler_params=_cp(None))(c_all, ada_w, ada_b_cols)


def _adam_math(w, g, m, v):
    m = ADAM_B1 * m + (1.0 - ADAM_B1) * g
    v = ADAM_B2 * v + (1.0 - ADAM_B2) * (g * g)
    m_hat = m / (1.0 - ADAM_B1 ** ADAM_STEP)
    v_hat = v / (1.0 - ADAM_B2 ** ADAM_STEP)
    delta = -ADAM_LR * (m_hat / (jnp.sqrt(v_hat) + ADAM_EPS) + ADAM_WD * w)
    return delta, m, v


def _ada_bwd_adamw(c_all, dmod_cols, w, m, v, name):
    tr = 256

    def body(c_ref, dm_ref, w_ref, m_ref, v_ref, g_ref, d_ref, nm_ref, nv_ref):
        cv = c_ref[...]
        sc = cv * _sigmoid(cv)
        g = lax.dot_general(sc, dm_ref[0], (((0,), (0,)), ((), ())), precision=_HI, preferred_element_type=f32)
        g_ref[0] = g
        d_ref[0], nm_ref[0], nv_ref[0] = _adam_math(w_ref[0], g, m_ref[0], v_ref[0])

    blk = pl.BlockSpec((1, tr, ADA_COLS), lambda l, i: (l, i, 0))
    shp = jax.ShapeDtypeStruct((DEPTH, D, ADA_COLS), f32)
    return pl.pallas_call(
        body, grid=(DEPTH, D // tr),
        in_specs=[pl.BlockSpec((8, tr), lambda l, i: (0, i)), pl.BlockSpec((1, 8, ADA_COLS), lambda l, i: (l, 0, 0)), blk, blk, blk],
        out_specs=[blk, blk, blk, blk], out_shape=[shp, shp, shp, shp], name=name, compiler_params=_cp(("parallel", "parallel")),
    )(c_all, dmod_cols, w, m, v)


def _adamw(w, g, m, v, name):
    def body(w_ref, g_ref, m_ref, v_ref, d_ref, nm_ref, nv_ref):
        d_ref[...], nm_ref[...], nv_ref[...] = _adam_math(w_ref[...], g_ref[...], m_ref[...], v_ref[...])

    shp = jax.ShapeDtypeStruct(w.shape, f32)
    return pl.pallas_call(body, out_shape=[shp] * 3, name=name, compiler_params=_cp(None))(w, g, m, v)


def _tile2(r, c):
    if r <= 256 or r % 256 == 0:
        return _pick(r, (256,)), _pick(c, (1024,))
    return r, 128


def _adamw_layers(w, g_mine, g_other, m, v, c_idx, name):
    _, r, c = w.shape
    tr, tc = _tile2(r, c)

    def body(ci_ref, w_ref, gm_ref, go_ref, m_ref, v_ref, g_ref, d_ref, nm_ref, nv_ref):
        def update(g):
            g_ref[0] = g
            d_ref[0], nm_ref[0], nv_ref[0] = _adam_math(w_ref[0], g, m_ref[0], v_ref[0])

        mine = pl.program_id(0) == ci_ref[0]

        @pl.when(mine)
        def _():
            update(gm_ref[...])

        @pl.when(jnp.logical_not(mine))
        def _():
            update(go_ref[...])

    blk = pl.BlockSpec((1, tr, tc), lambda l, i, j, ci: (l, i, j))
    gblk = pl.BlockSpec((tr, tc), lambda l, i, j, ci: (i, j))
    shp = jax.ShapeDtypeStruct(w.shape, f32)
    return pl.pallas_call(
        body,
        grid_spec=pltpu.PrefetchScalarGridSpec(num_scalar_prefetch=1, grid=(DEPTH, r // tr, c // tc),
                                               in_specs=[blk, gblk, gblk, blk, blk], out_specs=[blk, blk, blk, blk]),
        out_shape=[shp, shp, shp, shp], name=name, compiler_params=_cp(("parallel", "parallel", "parallel")),
    )(c_idx, w, g_mine, g_other, m, v)


_MESH = pl.DeviceIdType.MESH
_AXES = ("x", "y", "c")
_HBM = pl.BlockSpec(memory_space=pltpu.HBM)


def _my_place():
    return tuple(lax.axis_index(a) for a in _AXES)


def _allreduce(buf, axes, name):
    r = buf.shape[0]
    n = len(axes)

    def body(x_ref, o_ref, rbuf, ssem, rsem):
        me = dict(zip(_AXES, _my_place()))
        o_ref[...] = x_ref[...]
        for k, ax in enumerate(axes):
            peer = tuple(1 - me[a] if a == ax else me[a] for a in _AXES)
            cp = pltpu.make_async_remote_copy(src_ref=o_ref, dst_ref=rbuf.at[k], send_sem=ssem.at[k], recv_sem=rsem.at[k],
                                              device_id=peer, device_id_type=_MESH)
            cp.start()
            cp.wait()
            o_ref[...] = o_ref[...] + rbuf[k]

    vm = pl.BlockSpec(memory_space=pltpu.VMEM)
    return pl.pallas_call(
        body, out_shape=jax.ShapeDtypeStruct((r, 128), f32), in_specs=[vm], out_specs=vm,
        scratch_shapes=[pltpu.VMEM((n, r, 128), f32), pltpu.SemaphoreType.DMA((n,)), pltpu.SemaphoreType.DMA((n,))],
        name=name, compiler_params=pltpu.CompilerParams(vmem_limit_bytes=VMEM_LIMIT),
    )(buf)


def _other_chips(x, y):
    return [(1 - x, y), (x, 1 - y), (1 - x, 1 - y)]


def _gather_body(ins, outs, ssem, rsem):
    na = len(ins)
    x, y, c = _my_place()
    k_me = 2 * x + y
    sibling = (x, y, 1 - c)
    chips = _other_chips(x, y)
    slots = [2 * cx + cy for cx, cy in chips]
    half = [r.shape[0] // 2 if r.shape[0] % 32 == 0 else None for r in ins]

    def part(ref, a, core):
        return ref if half[a] is None else ref.at[pl.ds(core * half[a], half[a])]

    def rcopy(a, src, slot, core, to, idx):
        return pltpu.make_async_remote_copy(src_ref=src, dst_ref=part(outs[a].at[slot], a, core), send_sem=ssem.at[idx],
                                            recv_sem=rsem.at[idx], device_id=to, device_id_type=_MESH)

    sent = []
    for j, chip in enumerate(chips):
        for a in range(na):
            cp = rcopy(a, part(ins[a], a, c), k_me, c, (*chip, c), j * na + a)
            cp.start()
            sent.append(cp)
    for j, chip in enumerate(chips):
        for a in range(na):
            rcopy(a, part(ins[a], a, c), slots[j], c, (*chip, c), j * na + a).wait_recv()
            if half[a] is not None:
                cp = rcopy(a, part(outs[a].at[slots[j]], a, c), slots[j], c, sibling, (3 + j) * na + a)
                cp.start()
                sent.append(cp)
    for j in range(3):
        for a in range(na):
            if half[a] is not None:
                rcopy(a, part(ins[a], a, c), slots[j], 1 - c, sibling, (3 + j) * na + a).wait_recv()
    for cp in sent:
        cp.wait_send()


def _gather_layer_behind(shards, name, collective_id):
    na = len(shards)
    hbm = pltpu.MemorySpace.HBM
    ins = [jax.new_ref(s, memory_space=hbm) for s in shards]
    outs = [jax.empty_ref(jax.ShapeDtypeStruct((NSHARD,) + s.shape, s.dtype), memory_space=hbm) for s in shards]

    @pl.kernel(mesh=plsc.ScalarSubcoreMesh(axis_name="sequencer", num_cores=1), name=name,
               scratch_types=(pltpu.SemaphoreType.DMA((6 * na,)), pltpu.SemaphoreType.DMA((6 * na,))),
               compiler_params=pltpu.CompilerParams(collective_id=collective_id))
    def launch(ssem, rsem):
        x, y, c = _my_place()
        barrier = pltpu.get_barrier_semaphore()
        peers = [(*chip, c) for chip in _other_chips(x, y)] + [(x, y, 1 - c)]
        for peer in peers:
            pl.semaphore_signal(barrier, inc=1, device_id=peer, device_id_type=_MESH)
        pl.semaphore_wait(barrier, len(peers))
        _gather_body(ins, outs, ssem, rsem)

    launch()
    return [o[...] for o in outs]


def _with_own(gathered, own):
    xi, yi, _ = _my_place()
    whole = lax.dynamic_update_index_in_dim(gathered, own, 2 * xi + yi, 0)
    return [whole[k] for k in range(NSHARD)]


def _swap_layers(parts, name):
    na = len(parts)

    def body(*refs):
        ins, outs = refs[:2 * na], refs[2 * na:3 * na]
        ssem, rsem = refs[3 * na:]
        x, y, c = _my_place()

        def copy(a, layer):
            return pltpu.make_async_remote_copy(src_ref=ins[2 * a + layer], dst_ref=outs[a], send_sem=ssem.at[a], recv_sem=rsem.at[a],
                                                device_id=(x, y, 1 - c), device_id_type=_MESH)

        for layer in range(DEPTH):
            @pl.when(c == 1 - layer)
            def _():
                for a in range(na):
                    copy(a, layer).start()
        for a in range(na):
            copy(a, 0).wait()

    flat = [p for pair in parts for p in pair]
    return pl.pallas_call(
        body, out_shape=[jax.ShapeDtypeStruct(p0.shape, p0.dtype) for p0, _ in parts], in_specs=[_HBM] * (2 * na),
        out_specs=[_HBM] * na, scratch_shapes=[pltpu.SemaphoreType.DMA((na,)), pltpu.SemaphoreType.DMA((na,))], name=name,
    )(*flat)


def _scatter_shards_behind(sums, name, collective_id):
    na = len(sums)
    hbm = pltpu.MemorySpace.HBM
    ins = [jax.new_ref(p, memory_space=hbm) for p in sums]
    outs = [jax.empty_ref(jax.ShapeDtypeStruct((3,) + p.shape[1:], p.dtype), memory_space=hbm) for p in sums]

    @pl.kernel(mesh=plsc.ScalarSubcoreMesh(axis_name="sequencer", num_cores=1), name=name,
               scratch_types=(pltpu.SemaphoreType.DMA((3 * na,)), pltpu.SemaphoreType.DMA((3 * na,))),
               compiler_params=pltpu.CompilerParams(collective_id=collective_id))
    def launch(ssem, rsem):
        x, y, c = _my_place()
        barrier = pltpu.get_barrier_semaphore()
        chips = _other_chips(x, y)
        for chip in chips:
            pl.semaphore_signal(barrier, inc=1, device_id=(*chip, c), device_id_type=_MESH)
        pl.semaphore_wait(barrier, len(chips))
        cps = []
        for j, chip in enumerate(chips):
            kj = 2 * chip[0] + chip[1]
            for a in range(na):
                cps.append(pltpu.make_async_remote_copy(
                    src_ref=ins[a].at[kj], dst_ref=outs[a].at[j], send_sem=ssem.at[j * na + a], recv_sem=rsem.at[j * na + a],
                    device_id=(*chip, c), device_id_type=_MESH))
        for cp in cps:
            cp.start()
        for cp in cps:
            cp.wait()

    launch()
    return [o[...] for o in outs]


def _share_layers(finals, name):
    na = len(finals)

    def body(*refs):
        ins, outs = refs[:na], refs[na:2 * na]
        ssem, rsem = refs[2 * na:]
        x, y, c = _my_place()
        cps = [pltpu.make_async_remote_copy(src_ref=ins[a], dst_ref=outs[a], send_sem=ssem.at[a], recv_sem=rsem.at[a],
                                            device_id=(x, y, 1 - c), device_id_type=_MESH) for a in range(na)]
        for cp in cps:
            cp.start()
        for cp in cps:
            cp.wait()

    return pl.pallas_call(
        body, out_shape=[jax.ShapeDtypeStruct(p.shape, p.dtype) for p in finals], in_specs=[_HBM] * na, out_specs=[_HBM] * na,
        scratch_shapes=[pltpu.SemaphoreType.DMA((na,)), pltpu.SemaphoreType.DMA((na,))], name=name,
    )(*finals)


def _add_own_layer(part0, part1, recv, c_idx, name):
    ns, r, c = recv.shape
    tr, tc = _tile2(r, c)

    def body(ci_ref, p0_ref, p1_ref, r_ref, o_ref, ob_ref):
        def add(p_ref):
            t = p_ref[...] + r_ref[...]
            o_ref[...] = t
            ob_ref[...] = t.astype(bf16)

        @pl.when(ci_ref[0] == 0)
        def _():
            add(p0_ref)

        @pl.when(ci_ref[0] == 1)
        def _():
            add(p1_ref)

    blk = pl.BlockSpec((1, tr, tc), lambda k, i, j, ci: (k, i, j))
    blk0 = pl.BlockSpec((1, tr, tc), lambda k, i, j, ci: (k * (1 - ci[0]), i * (1 - ci[0]), j * (1 - ci[0])))
    blk1 = pl.BlockSpec((1, tr, tc), lambda k, i, j, ci: (k * ci[0], i * ci[0], j * ci[0]))
    return pl.pallas_call(
        body,
        grid_spec=pltpu.PrefetchScalarGridSpec(num_scalar_prefetch=1, grid=(ns, r // tr, c // tc), in_specs=[blk0, blk1, blk],
                                               out_specs=[blk, blk]),
        out_shape=[jax.ShapeDtypeStruct((ns, r, c), f32), jax.ShapeDtypeStruct((ns, r, c), bf16)], name=name,
        compiler_params=_cp(("arbitrary", "arbitrary", "arbitrary")),
    )(c_idx, part0, part1, recv)


def _add_own_shard(sums, recv, k_idx, name):
    _, r, c = sums.shape
    tr, tc = _tile2(r, c)

    def body(ki_ref, s_ref, r_ref, o_ref):
        o_ref[...] = ((s_ref[0] + r_ref[0].astype(f32)) + r_ref[1].astype(f32)) + r_ref[2].astype(f32)

    return pl.pallas_call(
        body,
        grid_spec=pltpu.PrefetchScalarGridSpec(
            num_scalar_prefetch=1, grid=(r // tr, c // tc),
            in_specs=[pl.BlockSpec((1, tr, tc), lambda i, j, ki: (ki[0], i, j)), pl.BlockSpec((3, tr, tc), lambda i, j, ki: (0, i, j))],
            out_specs=pl.BlockSpec((tr, tc), lambda i, j, ki: (i, j))),
        out_shape=jax.ShapeDtypeStruct((r, c), f32), name=name, compiler_params=_cp(("parallel", "parallel")),
    )(k_idx, sums, recv)


def _reduce_scatter(parts, tag, meanwhile):
    x, y, c = _my_place()
    c_idx = jnp.reshape(c, (1,)).astype(jnp.int32)
    k_idx = jnp.reshape(2 * x + y, (1,)).astype(jnp.int32)
    na = len(parts)
    recv = _swap_layers(parts, f"rs_swap{tag}")
    sums = [_add_own_layer(parts[a][0], parts[a][1], recv[a], c_idx, f"rs_add_layer{tag}_{a}") for a in range(na)]
    recv = _scatter_shards_behind([sb for _, sb in sums], f"rs_scatter{tag}", 3)
    recv, side = lax.optimization_barrier((recv, meanwhile()))
    finals = [_add_own_shard(sums[a][0], recv[a], k_idx, f"rs_add_shard{tag}_{a}") for a in range(na)]
    return finals, _share_layers(finals, f"rs_share{tag}"), c_idx, side


_SMALL = [("ada_b", (DEPTH, 3 * D)), ("norm_w", (DEPTH, D)), ("gm_ln_w", (DEPTH, D)), ("gm_ln_b", (DEPTH, D)),
          ("gm_ws", (DEPTH, NG, CH, CH)), ("gm_bs", (DEPTH, NG, CH)), ("conv_b", (DEPTH, CONVD)), ("dt_bias", (DEPTH, NH)),
          ("a_log", (DEPTH, NH)), ("d_skip", (DEPTH, NH)), ("ssm_norm_w", (DEPTH, DIN)), ("final_norm_w", (D,))]


def _rows_of(shape):
    n = 1
    for d in shape:
        n *= d
    return -(-n // 1024) * 8


def _pack(arrays):
    rows = []
    for a in arrays:
        flat = a.reshape(-1)
        r = _rows_of(a.shape)
        rows.append(jnp.pad(flat, (0, r * 128 - flat.shape[0])).reshape(r, 128))
    return jnp.concatenate(rows, axis=0)


def _unpack(buf, shapes):
    out, at = [], 0
    for shp in shapes:
        r = _rows_of(shp)
        n = 1
        for d in shp:
            n *= d
        out.append(buf[at:at + r].reshape(-1)[:n].reshape(shp))
        at += r
    return out


def kernel(x, c, ada_w, ada_b, norm_w, w_in, gm_ln_w, gm_ln_b, gm_ws, gm_bs, conv_w, conv_b, dt_bias, a_log, d_skip, ssm_norm_w, w_proj_a, w_proj_b, w_out, final_norm_w, loss_target, m_ada_w, m_ada_b, m_norm_w, m_w_in, m_gm_ln_w, m_gm_ln_b, m_gm_ws, m_gm_bs, m_conv_w, m_conv_b, m_dt_bias, m_a_log, m_d_skip, m_ssm_norm_w, m_w_proj_a, m_w_proj_b, m_w_out, m_final_norm_w, v_ada_w, v_ada_b, v_norm_w, v_w_in, v_gm_ln_w, v_gm_ln_b, v_gm_ws, v_gm_bs, v_conv_w, v_conv_b, v_dt_bias, v_a_log, v_d_skip, v_ssm_norm_w, v_w_proj_a, v_w_proj_b, v_w_out, v_final_norm_w):
    xi, yi, ci = _my_place()
    k_me = 2 * xi + yi
    b_me = 4 * xi + 2 * yi + ci
    w = dict(ada_b=ada_b, norm_w=norm_w, gm_ln_w=gm_ln_w, gm_ln_b=gm_ln_b, gm_ws=gm_ws, gm_bs=gm_bs, conv_b=conv_b, dt_bias=dt_bias,
             a_log=a_log, d_skip=d_skip, ssm_norm_w=ssm_norm_w, final_norm_w=final_norm_w)
    m = dict(ada_b=m_ada_b, norm_w=m_norm_w, gm_ln_w=m_gm_ln_w, gm_ln_b=m_gm_ln_b, gm_ws=m_gm_ws, gm_bs=m_gm_bs, conv_b=m_conv_b,
             dt_bias=m_dt_bias, a_log=m_a_log, d_skip=m_d_skip, ssm_norm_w=m_ssm_norm_w, final_norm_w=m_final_norm_w)
    v = dict(ada_b=v_ada_b, norm_w=v_norm_w, gm_ln_w=v_gm_ln_w, gm_ln_b=v_gm_ln_b, gm_ws=v_gm_ws, gm_bs=v_gm_bs, conv_b=v_conv_b,
             dt_bias=v_dt_bias, a_log=v_a_log, d_skip=v_d_skip, ssm_norm_w=v_ssm_norm_w, final_norm_w=v_final_norm_w)

    c_slot = lax.dynamic_update_slice(jnp.zeros((8, D), f32), c, (b_me, 0))
    c_all = _allreduce(c_slot.reshape(64, 128), _AXES, "gather_c").reshape(8, D)
    ada_b_cols = lax.dynamic_slice(ada_b, (0, k_me * ADA_COLS), (DEPTH, ADA_COLS)).reshape(DEPTH, 1, ADA_COLS)
    mod_cols = _ada_fwd(c_all, ada_w, ada_b_cols, "ada_fwd")
    mod_slot = lax.dynamic_update_slice(jnp.zeros((DEPTH, 8, 3 * D), f32), mod_cols, (0, 0, k_me * ADA_COLS))
    mod_all = _allreduce(mod_slot.reshape(-1, 128), ("x", "y"), "gather_mod").reshape(DEPTH, 8, 3 * D)
    mod_me = lax.dynamic_slice(mod_all, (0, b_me, 0), (DEPTH, 1, 3 * D))
    mods = [(mod_me[l, :, 0:D], mod_me[l, :, D:2 * D], mod_me[l, :, 2 * D:3 * D]) for l in range(DEPTH)]

    rows_sh = jnp.concatenate([w_proj_a, w_proj_b, w_out], axis=1).astype(bf16)
    win_sh = w_in.astype(bf16)
    first = _gather_layer_behind([win_sh[0], rows_sh[0], conv_w[0]], "gather_l0", 1)
    first, mods, later = lax.optimization_barrier((first, mods, [win_sh[1], rows_sh[1], conv_w[1]]))
    others = [first, _gather_layer_behind(later, "gather_l1", 2)]

    def operands_of(l, x_in):
        gathered = others[l]
        if l == 1:
            gathered, x_in = lax.optimization_barrier((gathered, x_in))
        win_g, rows_g, conv_g = gathered
        rows_l = _with_own(rows_g, rows_sh[l])
        w_in_l = jnp.concatenate(_with_own(win_g, win_sh[l]), axis=1)
        wa_l = jnp.concatenate([t[0:256] for t in rows_l], axis=0)
        wb_l = jnp.concatenate([t[256:768] for t in rows_l], axis=0)
        wo_l = jnp.concatenate([t[768:1024] for t in rows_l], axis=0)
        cw_l = jnp.concatenate(_with_own(conv_g, conv_w[l]), axis=1)
        return _layer_operands(w_in_l, cw_l, wa_l, wb_l, wo_l, norm_w[l], gm_ln_w[l], gm_ln_b[l], gm_ws[l], gm_bs[l], conv_b[l],
                               dt_bias[l], a_log[l], d_skip[l], ssm_norm_w[l]), x_in

    loss_parts, dx, grads, dfnw = _local_step(x[0], loss_target[0], mods, operands_of, final_norm_w)

    s_in = N_IN // NSHARD
    tr_ = lambda t: jnp.swapaxes(t, 1, 2)
    g_in = [grads[l]["w_in_t"].reshape(NSHARD, s_in, D) for l in range(DEPTH)]
    g_rows = [jnp.concatenate([grads[l]["w_proj_a"].reshape(NSHARD, 256, D), grads[l]["w_proj_b"].reshape(NSHARD, 512, D),
                               grads[l]["w_out"].reshape(NSHARD, 256, D)], axis=1) for l in range(DEPTH)]
    g_conv = [grads[l]["conv_w"].reshape(4, NSHARD, D).transpose(1, 0, 2) for l in range(DEPTH)]

    dmod_slot = lax.dynamic_update_slice(jnp.zeros((DEPTH, 8, 3 * D), f32),
                                         jnp.stack([grads[l]["mod"] for l in range(DEPTH)]).reshape(DEPTH, 1, 3 * D), (0, b_me, 0))
    small_g = {n: (dfnw if n == "final_norm_w" else jnp.stack([grads[l]["mod" if n == "ada_b" else n] for l in range(DEPTH)]))
               for n, _ in _SMALL}
    n_small = sum(_rows_of(s) for _, s in _SMALL)
    n_dmod = _rows_of(dmod_slot.shape)

    def small_work():
        packed = _allreduce(_pack([small_g[n] for n, _ in _SMALL] + [dmod_slot, loss_parts]), _AXES, "allreduce_small")
        dmod_all = packed[n_small:n_small + n_dmod].reshape(DEPTH, 8, 3 * D)
        dmod_cols = lax.dynamic_slice(dmod_all, (0, 0, k_me * ADA_COLS), (DEPTH, 8, ADA_COLS))
        return packed, _ada_bwd_adamw(c_all, dmod_cols, ada_w, m_ada_w, v_ada_w, "ada_bwd_adamw")

    (f_in, f_rows, f_conv), (o_in, o_rows, o_conv), c_idx, (packed, ada_out) = _reduce_scatter([g_in, g_rows, g_conv], "", small_work)
    g_ada, d_ada, nm_ada, nv_ada = ada_out
    gr_in, d_in, nm_in, nv_in = [tr_(t) for t in _adamw_layers(tr_(w_in), f_in, o_in, tr_(m_w_in), tr_(v_w_in), c_idx, "adamw_w_in")]
    cat = lambda a, b, c_: jnp.concatenate([a, b, c_], axis=1)
    gr_rows, d_rows, nm_rows, nv_rows = _adamw_layers(cat(w_proj_a, w_proj_b, w_out), f_rows, o_rows, cat(m_w_proj_a, m_w_proj_b, m_w_out),
                                                      cat(v_w_proj_a, v_w_proj_b, v_w_out), c_idx, "adamw_rows")
    gr_conv = jnp.where(ci == 0, jnp.stack([f_conv, o_conv]), jnp.stack([o_conv, f_conv]))
    split = lambda t: (t[:, 0:256], t[:, 256:768], t[:, 768:1024])

    g_small = packed[0:n_small]
    loss = jnp.sum(packed[n_small + n_dmod:])
    small_gw = jnp.concatenate([g_small, _pack([gr_conv])], axis=0)
    d_s, nm_s, nv_s = _adamw(_pack([w[n] for n, _ in _SMALL] + [conv_w]), small_gw,
                             _pack([m[n] for n, _ in _SMALL] + [m_conv_w]), _pack([v[n] for n, _ in _SMALL] + [v_conv_w]), "adamw_small")
    shapes = [s for _, s in _SMALL] + [conv_w.shape]
    names = [n for n, _ in _SMALL] + ["conv_w"]
    g_d = dict(zip(names, _unpack(small_gw, shapes)))
    d_d = dict(zip(names, _unpack(d_s, shapes)))
    nm_d = dict(zip(names, _unpack(nm_s, shapes)))
    nv_d = dict(zip(names, _unpack(nv_s, shapes)))

    def by_name(big, small):
        ga, gb, go = split(big[1])
        return dict(small, ada_w=big[2], w_in=big[0], w_proj_a=ga, w_proj_b=gb, w_out=go)

    order = ["ada_w", "ada_b", "norm_w", "w_in", "gm_ln_w", "gm_ln_b", "gm_ws", "gm_bs", "conv_w", "conv_b", "dt_bias", "a_log",
             "d_skip", "ssm_norm_w", "w_proj_a", "w_proj_b", "w_out", "final_norm_w"]
    outs = []
    for big, small in (((gr_in, gr_rows, g_ada), g_d), ((d_in, d_rows, d_ada), d_d), ((nm_in, nm_rows, nm_ada), nm_d),
                       ((nv_in, nv_rows, nv_ada), nv_d)):
        t = by_name(big, small)
        outs += [t[n] for n in order]
    return (loss, dx.reshape(1, -1, D), *outs)
```

```python
import jax
import jax.numpy as jnp
from jax import lax
from jax.experimental import pallas as pl
from jax.experimental.pallas import tpu as pltpu
from jax.experimental.pallas import tpu_sc as plsc

f32 = jnp.float32
bf16 = jnp.bfloat16

D = 1024
DEPTH = 2
EPS = 1e-6
CH = 128
NG = 8
HPG = 4
HD = 64
NH = NG * HPG
NST = 128
DIN = 2048
CONVD = 4096
GW = DIN // NG
PB = CONVD + DIN + 256
PA = 3 * D
PG = 2 * D
N_IN = 11296
NSHARD = 4
RA, RB = D // NSHARD, DIN // NSHARD
V7X_VMEM_BYTES = 64 * 2 ** 20
VMEM_LIMIT = V7X_VMEM_BYTES - 8 * 2 ** 20

ADAM_LR, ADAM_B1, ADAM_B2, ADAM_EPS, ADAM_WD, ADAM_STEP = 0.001, 0.9, 0.999, 1e-08, 0.01, 10

_HI = lax.Precision.HIGHEST


def _cp(sem):
    return pltpu.CompilerParams(dimension_semantics=sem, vmem_limit_bytes=VMEM_LIMIT)


def _sigmoid(x):
    return 0.5 * jnp.tanh(0.5 * x) + 0.5


def _silu_and_grad(x):
    s = _sigmoid(x)
    return x * s, s * (1.0 + x * (1.0 - s))


_GELU_K = 0.7978845608028654
_GELU_C = 0.044715


def _gelu_and_grad(x):
    x2 = x * x
    t = jnp.tanh(_GELU_K * (x + _GELU_C * x * x2))
    g = 0.5 * x * (1.0 + t)
    dg = 0.5 * (1.0 + t) + 0.5 * x * (1.0 - t * t) * _GELU_K * (1.0 + 3.0 * _GELU_C * x2)
    return g, dg


def _gelu(x):
    t = jnp.tanh(_GELU_K * (x + _GELU_C * x * x * x))
    return 0.5 * x * (1.0 + t)


def _softplus(x):
    return jnp.maximum(x, 0.0) + jnp.log(1.0 + jnp.exp(-jnp.abs(x)))


def _dot(a, b):
    return jnp.dot(a, b, preferred_element_type=f32)


def _dot_nt(a, b):
    return lax.dot_general(a, b, (((1,), (1,)), ((), ())), preferred_element_type=f32)


def _dot_tn(a, b):
    return lax.dot_general(a, b, (((0,), (0,)), ((), ())), preferred_element_type=f32)


def _dot_hi(a, b):
    return jnp.dot(a, b, precision=_HI, preferred_element_type=f32)


def _rmsmod_fwd(x, nw, scale, shift, name):
    s = x.shape[0]
    ts = min(512, s)

    def body(x_ref, nw_ref, sc_ref, sh_ref, h_ref):
        xv = x_ref[...]
        r = lax.rsqrt(jnp.mean(xv * xv, axis=-1, keepdims=True) + EPS)
        h_ref[...] = ((xv * r) * nw_ref[...] * (1.0 + sc_ref[...]) + sh_ref[...]).astype(bf16)

    row = pl.BlockSpec((1, D), lambda i: (0, 0))
    tile = pl.BlockSpec((ts, D), lambda i: (i, 0))
    return pl.pallas_call(
        body, grid=(s // ts,), in_specs=[tile, row, row, row], out_specs=tile,
        out_shape=jax.ShapeDtypeStruct((s, D), bf16), name=name, compiler_params=_cp(("parallel",)),
    )(x, nw, scale, shift)


def _rmsmod_bwd(dh, x, dres, nw, scale, name):
    s = x.shape[0]
    ts = min(512, s)

    def body(dh_ref, x_ref, dres_ref, nw_ref, sc_ref, dx_ref, dsc_ref, dsh_ref, dnw_ref):
        @pl.when(pl.program_id(0) == 0)
        def _():
            dsc_ref[...] = jnp.zeros_like(dsc_ref)
            dsh_ref[...] = jnp.zeros_like(dsh_ref)
            dnw_ref[...] = jnp.zeros_like(dnw_ref)
        xv = x_ref[...]
        dhv = dh_ref[...]
        r = lax.rsqrt(jnp.mean(xv * xv, axis=-1, keepdims=True) + EPS)
        xn = xv * r
        one_sc = 1.0 + sc_ref[...]
        dsc_ref[...] += jnp.sum(dhv * xn * nw_ref[...], axis=0, keepdims=True)
        dsh_ref[...] += jnp.sum(dhv, axis=0, keepdims=True)
        dnw_ref[...] += jnp.sum(dhv * xn * one_sc, axis=0, keepdims=True)
        dxn = dhv * (nw_ref[...] * one_sc)
        dx_ref[...] = r * (dxn - xn * jnp.mean(dxn * xn, axis=-1, keepdims=True)) + dres_ref[...]

    row = pl.BlockSpec((1, D), lambda i: (0, 0))
    tile = pl.BlockSpec((ts, D), lambda i: (i, 0))
    vec = jax.ShapeDtypeStruct((1, D), f32)
    return pl.pallas_call(
        body, grid=(s // ts,), in_specs=[tile, tile, tile, row, row], out_specs=[tile, row, row, row],
        out_shape=[jax.ShapeDtypeStruct((s, D), f32), vec, vec, vec], name=name, compiler_params=_cp(("arbitrary",)),
    )(dh, x, dres, nw, scale)


def _pick(n, prefs):
    for p in prefs:
        if n % p == 0:
            return p
    return n


def _mm(a, b, out_dtype, name, c_in=None, trans_b=False):
    m, k = a.shape
    n = b.shape[0] if trans_b else b.shape[1]
    tn = _pick(n, (1280, 1024, 512))
    tk = _pick(k, (1280, 1024, 512))
    nk = k // tk
    one_pass = nk == 1 and c_in is None
    tm = _pick(m, (2048, 1024, 512, 256) if one_pass else (1024, 512, 256))
    dot = _dot_nt if trans_b else _dot

    def body(*refs):
        if one_pass:
            a_ref, b_ref, o_ref = refs
            o_ref[...] = dot(a_ref[...], b_ref[...]).astype(out_dtype)
            return
        if c_in is not None:
            a_ref, b_ref, c_ref, o_ref, acc = refs
        else:
            a_ref, b_ref, o_ref, acc = refs
        kk = pl.program_id(2)

        @pl.when(kk == 0)
        def _():
            if c_in is not None:
                acc[...] = c_ref[...]
            else:
                acc[...] = jnp.zeros_like(acc)
        acc[...] += dot(a_ref[...], b_ref[...])

        @pl.when(kk == nk - 1)
        def _():
            o_ref[...] = acc[...].astype(out_dtype)

    b_spec = pl.BlockSpec((tn, tk), lambda j, i, kk: (j, kk)) if trans_b else pl.BlockSpec((tk, tn), lambda j, i, kk: (kk, j))
    in_specs = [pl.BlockSpec((tm, tk), lambda j, i, kk: (i, kk)), b_spec]
    args = [a, b]
    if c_in is not None:
        in_specs.append(pl.BlockSpec((tm, tn), lambda j, i, kk: (i, j)))
        args.append(c_in)
    return pl.pallas_call(
        body, grid=(n // tn, m // tm, nk), in_specs=in_specs, out_specs=pl.BlockSpec((tm, tn), lambda j, i, kk: (i, j)),
        out_shape=jax.ShapeDtypeStruct((m, n), out_dtype), scratch_shapes=[] if one_pass else [pltpu.VMEM((tm, tn), f32)],
        name=name, compiler_params=_cp(("parallel", "parallel", "arbitrary")),
    )(*args)


def _mm_tn(a, b, name):
    t, k1 = a.shape
    n = b.shape[1]
    t1 = _pick(k1, (1280, 1024, 512))
    tn = _pick(n, (1280, 1024, 512))
    tt = _pick(t, (2048, 1024, 512, 256))
    nt = t // tt

    def body(a_ref, b_ref, o_ref):
        tt_i = pl.program_id(2)

        @pl.when(tt_i == 0)
        def _():
            o_ref[...] = jnp.zeros_like(o_ref)
        o_ref[...] += _dot_tn(a_ref[...], b_ref[...])

    return pl.pallas_call(
        body, grid=(k1 // t1, n // tn, nt),
        in_specs=[pl.BlockSpec((tt, t1), lambda i, j, tt_i: (tt_i, i)), pl.BlockSpec((tt, tn), lambda i, j, tt_i: (tt_i, j))],
        out_specs=pl.BlockSpec((t1, tn), lambda i, j, tt_i: (i, j)),
        out_shape=jax.ShapeDtypeStruct((k1, n), f32), name=name,
        compiler_params=_cp(("parallel", "parallel", "arbitrary")),
    )(a, b)


def _ln_stats(v):
    mu = jnp.mean(v, axis=-1, keepdims=True)
    vc = v - mu
    rstd = lax.rsqrt(jnp.mean(vc * vc, axis=-1, keepdims=True) + EPS)
    return vc * rstd, rstd


def _mix(w_ref, vl):
    return jnp.concatenate([_dot(w_ref[g], vl[:, g * CH:(g + 1) * CH]) for g in range(NG)], axis=1)


def _branch_a_fwd(proj_a, lnw, lnb, wsm, bsf, name):
    s = proj_a.shape[0]
    ta = min(256, s)

    def body(pu_ref, pv_ref, pz_ref, lnw_ref, lnb_ref, w_ref, bs_ref, ya_ref):
        for c in range(ta // CH):
            rows = pl.ds(c * CH, CH)
            vh, _ = _ln_stats(_gelu(pv_ref[rows, :]))
            vl = (vh * lnw_ref[...] + lnb_ref[...]).astype(bf16)
            mixed = _mix(w_ref, vl) + bs_ref[...]
            pz = pz_ref[rows, :]
            ya_ref[rows, :] = (_gelu(pu_ref[rows, :]) * mixed * (pz * _sigmoid(pz))).astype(bf16)

    row = pl.BlockSpec((1, D), lambda i: (0, 0))
    return pl.pallas_call(
        body, grid=(s // ta,),
        in_specs=[pl.BlockSpec((ta, D), lambda i: (i, 0)), pl.BlockSpec((ta, D), lambda i: (i, 1)),
                  pl.BlockSpec((ta, D), lambda i: (i, 2)), row, row,
                  pl.BlockSpec((NG, CH, CH), lambda i: (0, 0, 0)), pl.BlockSpec((CH, D), lambda i: (0, 0))],
        out_specs=pl.BlockSpec((ta, D), lambda i: (i, 0)),
        out_shape=jax.ShapeDtypeStruct((s, D), bf16), name=name, compiler_params=_cp(("parallel",)),
    )(proj_a, proj_a, proj_a, lnw, lnb, wsm, bsf)


def _branch_a_bwd(proj_a, dya, lnw, lnb, wsm, wsm_t, bsf, name):
    s = proj_a.shape[0]
    ta = min(256, s)

    def body(pu_ref, pv_ref, pz_ref, dya_ref, lnw_ref, lnb_ref, w_ref, wt_ref, bs_ref,
             dp_ref, dws_ref, dbs_ref, dlnw_ref, dlnb_ref):
        @pl.when(pl.program_id(0) == 0)
        def _():
            dws_ref[...] = jnp.zeros_like(dws_ref)
            dbs_ref[...] = jnp.zeros_like(dbs_ref)
            dlnw_ref[...] = jnp.zeros_like(dlnw_ref)
            dlnb_ref[...] = jnp.zeros_like(dlnb_ref)
        for c in range(ta // CH):
            rows = pl.ds(c * CH, CH)
            u, du = _gelu_and_grad(pu_ref[rows, :])
            v, dv_act = _gelu_and_grad(pv_ref[rows, :])
            zg, dzg = _silu_and_grad(pz_ref[rows, :])
            vh, rstd = _ln_stats(v)
            vl = (vh * lnw_ref[...] + lnb_ref[...]).astype(bf16)
            mixed = _mix(w_ref, vl) + bs_ref[...]
            dy = dya_ref[rows, :].astype(f32)
            dmixed = dy * u * zg
            dp_ref[rows, 0:D] = (dy * mixed * zg * du).astype(bf16)
            dp_ref[rows, 2 * D:3 * D] = (dy * u * mixed * dzg).astype(bf16)
            dmb = dmixed.astype(bf16)
            dbs_ref[...] += dmixed
            dvl = _mix(wt_ref, dmb)
            for g in range(NG):
                cols = slice(g * CH, (g + 1) * CH)
                dws_ref[g] += _dot_nt(dmb[:, cols], vl[:, cols])
            dlnw_ref[...] += jnp.sum(dvl * vh, axis=0, keepdims=True)
            dlnb_ref[...] += jnp.sum(dvl, axis=0, keepdims=True)
            dvh = dvl * lnw_ref[...]
            dv = rstd * (dvh - jnp.mean(dvh, axis=-1, keepdims=True) - vh * jnp.mean(dvh * vh, axis=-1, keepdims=True))
            dp_ref[rows, D:2 * D] = (dv * dv_act).astype(bf16)

    row = pl.BlockSpec((1, D), lambda i: (0, 0))
    wspec = pl.BlockSpec((NG, CH, CH), lambda i: (0, 0, 0))
    full = pl.BlockSpec((CH, D), lambda i: (0, 0))
    return pl.pallas_call(
        body, grid=(s // ta,),
        in_specs=[pl.BlockSpec((ta, D), lambda i: (i, 0)), pl.BlockSpec((ta, D), lambda i: (i, 1)),
                  pl.BlockSpec((ta, D), lambda i: (i, 2)), pl.BlockSpec((ta, D), lambda i: (i, 0)),
                  row, row, wspec, wspec, full],
        out_specs=[pl.BlockSpec((ta, PA), lambda i: (i, 0)), wspec, full, row, row],
        out_shape=[jax.ShapeDtypeStruct((s, PA), bf16), jax.ShapeDtypeStruct((NG, CH, CH), f32),
                   jax.ShapeDtypeStruct((CH, D), f32), jax.ShapeDtypeStruct((1, D), f32), jax.ShapeDtypeStruct((1, D), f32)],
        name=name, compiler_params=_cp(("arbitrary",)),
    )(proj_a, proj_a, proj_a, dya, lnw, lnb, wsm, wsm_t, bsf)


GB = GW + 2 * NST


def _group_major(xs, b, c):
    lead = xs.shape[:-1]
    return jnp.concatenate([xs.reshape(lead + (NG, GW)), b.reshape(lead + (NG, NST)), c.reshape(lead + (NG, NST))],
                           axis=-1).reshape(lead + (CONVD,))


def _from_group_major(t):
    lead = t.shape[:-1]
    t = t.reshape(lead + (NG, GB))
    return jnp.concatenate([t[..., 0:GW].reshape(lead + (DIN,)), t[..., GW:GW + NST].reshape(lead + (NG * NST,)),
                            t[..., GW + NST:GB].reshape(lead + (NG * NST,))], axis=-1)


def _rows_from_group_major(t):
    t = t.reshape(NG, GB, t.shape[-1])
    return jnp.concatenate([t[:, 0:GW].reshape(DIN, -1), t[:, GW:GW + NST].reshape(NG * NST, -1),
                            t[:, GW + NST:GB].reshape(NG * NST, -1)], axis=0)


def _shift_rows(x, prev8, j):
    xr = pltpu.roll(x, j, 0)
    fix = pltpu.roll(prev8, j, 0)
    rid = lax.broadcasted_iota(jnp.int32, (8, x.shape[1]), 0)
    top = jnp.where(rid < j, fix, xr[0:8])
    return jnp.concatenate([top, xr[8:]], axis=0)


def _shift_rows_up(d, next8, j):
    dr = pltpu.roll(d, CH - j, 0)
    fix = pltpu.roll(next8, 8 - j, 0)
    rid = lax.broadcasted_iota(jnp.int32, (8, d.shape[1]), 0)
    bot = jnp.where(rid >= 8 - j, fix, dr[CH - 8:CH])
    return jnp.concatenate([dr[0:CH - 8], bot], axis=0)


def _conv_pre(x, prev8, cw_ref, cb_ref, cols):
    shifted = [_shift_rows(x, prev8, j) for j in (1, 2, 3)]
    conv = cb_ref[:, cols] + cw_ref[3:4, cols] * x
    for j in (1, 2, 3):
        conv = conv + cw_ref[3 - j:4 - j, cols] * shifted[j - 1]
    return conv, shifted


def _tril_mask():
    return lax.broadcasted_iota(jnp.int32, (CH, CH), 0) >= lax.broadcasted_iota(jnp.int32, (CH, CH), 1)


def _sum_all(v):
    return jnp.sum(jnp.sum(v, axis=0, keepdims=True), axis=1, keepdims=True)


def _lanes(g, width, base=0):
    return pl.ds(pl.multiple_of(base + g * width, width), width)


def _branch_b_fwd(proj_b, cw, cb, dtb, a_row, dkc, snw, name):
    s = proj_b.shape[0]
    nc = s // CH

    def body(xbc_ref, sz_ref, dtr_ref, cw_ref, cb_ref, dtb_ref, a_ref, dkc_ref, snw_ref,
             yb_ref, y_ref, st_ref, cv_ref, prev8, state, acst_s):
        @pl.when(pl.program_id(0) == 0)
        def _():
            prev8[...] = jnp.zeros_like(prev8)
            state[...] = jnp.zeros_like(state)
        st_ref[0] = state[...].astype(bf16)
        mask = _tril_mask()
        dt_all = _softplus(dtr_ref[:, 0:CH] + dtb_ref[...])
        acs_all = _dot_hi(mask.astype(f32), dt_all * a_ref[...])
        acst_s[...] = acs_all.T

        def group(g, carry):
            cols = _lanes(g, GB)
            gcols = _lanes(g, GW)
            x = xbc_ref[:, cols]
            conv, _ = _conv_pre(x, prev8[:, cols], cw_ref, cb_ref, cols)
            prev8[:, cols] = x[CH - 8:CH]
            cv_ref[:, cols] = conv.astype(bf16)
            xc = conv * _sigmoid(conv)
            xs = xc[:, 0:GW]
            bg = xc[:, GW:GW + NST].astype(bf16)
            cg = xc[:, GW + NST:GB].astype(bf16)
            back = lax.rem(CH - HPG * g, CH)
            dt = pltpu.roll(dt_all, back, 1)
            acs = pltpu.roll(acs_all, back, 1)
            cbm = _dot_nt(cg, bg)
            dkc_g = dkc_ref[:, gcols]
            y_parts = []
            for r in range(HPG):
                colb = jnp.broadcast_to(acs[:, r:r + 1], (CH, CH))
                row = acst_s[pl.ds(g * HPG + r, 1), :]
                lmat = jnp.exp(jnp.where(mask, colb - row, -jnp.inf))
                xr = xs[:, r * HD:(r + 1) * HD]
                xd = xr * dt[:, r:r + 1]
                sp = state[g * HPG + r]
                col = colb[:, 0:HD]
                alast = colb[CH - 1:CH, 0:HD]
                y_r = _dot((cbm * lmat).astype(bf16), xd.astype(bf16))
                y_r = y_r + jnp.exp(col) * _dot_nt(cg, sp.astype(bf16))
                y_parts.append(y_r + xr * dkc_g[:, r * HD:(r + 1) * HD])
                cs = _dot_tn((xd * jnp.exp(alast - col)).astype(bf16), bg)
                state[g * HPG + r] = jnp.exp(colb[CH - 1:CH, :]) * sp + cs
            y = jnp.concatenate(y_parts, axis=1)
            szv = sz_ref[:, gcols]
            yz = y * (szv * _sigmoid(szv))
            rr = lax.rsqrt(jnp.mean(yz * yz, axis=-1, keepdims=True) + EPS)
            yb_ref[:, gcols] = (yz * rr * snw_ref[:, gcols]).astype(bf16)
            y_ref[:, gcols] = y.astype(bf16)
            return carry

        lax.fori_loop(0, NG, group, 0)

    const2 = lambda c: (0, 0)
    return pl.pallas_call(
        body, grid=(nc,),
        in_specs=[pl.BlockSpec((CH, CONVD), lambda c: (c, 0)), pl.BlockSpec((CH, DIN), lambda c: (c, CONVD // DIN)),
                  pl.BlockSpec((CH, 256), lambda c: (c, (CONVD + DIN) // 256)),
                  pl.BlockSpec((4, CONVD), const2), pl.BlockSpec((1, CONVD), const2),
                  pl.BlockSpec((1, CH), const2), pl.BlockSpec((1, CH), const2),
                  pl.BlockSpec((1, DIN), const2), pl.BlockSpec((1, DIN), const2)],
        out_specs=[pl.BlockSpec((CH, DIN), lambda c: (c, 0)), pl.BlockSpec((CH, DIN), lambda c: (c, 0)),
                   pl.BlockSpec((1, NH, HD, NST), lambda c: (c, 0, 0, 0)), pl.BlockSpec((CH, CONVD), lambda c: (c, 0))],
        out_shape=[jax.ShapeDtypeStruct((s, DIN), bf16), jax.ShapeDtypeStruct((s, DIN), bf16),
                   jax.ShapeDtypeStruct((nc, NH, HD, NST), bf16), jax.ShapeDtypeStruct((s, CONVD), bf16)],
        scratch_shapes=[pltpu.VMEM((8, CONVD), f32), pltpu.VMEM((NH, HD, NST), f32), pltpu.VMEM((CH, CH), f32)],
        name=name, compiler_params=_cp(("arbitrary",)),
    )(proj_b, proj_b, proj_b, cw, cb, dtb, a_row, dkc, snw)


def _branch_b_bwd(proj_b, conv_sv, dyb, y_sv, states, cw, dtb, a_row, dkc, snw, ind, name):
    s = proj_b.shape[0]
    nc = s // CH

    def body(xbc_ref, cv_ref, sz_ref, dtr_ref, dyb_ref, y_ref, st_ref, cw_ref, dtb_ref, a_ref, dkc_ref,
             snw_ref, ind_ref, dp_ref, dcw_ref, dcb_ref, ddtb_ref, dal_ref, ddk_ref, dsnw_ref,
             dstate, dnext8, acst_s, dacs_acc, q2_acc):
        @pl.when(pl.program_id(0) == 0)
        def _():
            dstate[...] = jnp.zeros_like(dstate)
            dnext8[...] = jnp.zeros_like(dnext8)
            dcw_ref[...] = jnp.zeros_like(dcw_ref)
            dcb_ref[...] = jnp.zeros_like(dcb_ref)
            ddtb_ref[...] = jnp.zeros_like(ddtb_ref)
            dal_ref[...] = jnp.zeros_like(dal_ref)
            ddk_ref[...] = jnp.zeros_like(ddk_ref)
            dsnw_ref[...] = jnp.zeros_like(dsnw_ref)

        dacs_acc[...] = jnp.zeros_like(dacs_acc)
        q2_acc[...] = jnp.zeros_like(q2_acc)
        mask = _tril_mask()
        tri_t = (lax.broadcasted_iota(jnp.int32, (CH, CH), 0) <= lax.broadcasted_iota(jnp.int32, (CH, CH), 1)).astype(f32)
        lane1 = lax.broadcasted_iota(jnp.int32, (1, CH), 1)
        is_last = lax.broadcasted_iota(jnp.int32, (CH, 1), 0) == CH - 1
        z_all = dtr_ref[:, 0:CH] + dtb_ref[...]
        dt_all = _softplus(z_all)
        adt_all = dt_all * a_ref[...]
        acs_all = _dot_hi(mask.astype(f32), adt_all)
        acst_s[...] = acs_all.T

        def ind_sum(v):
            hi = v.astype(bf16)
            lo = (v - hi.astype(f32)).astype(bf16)
            return _dot(hi, ind_ref[...]) + _dot(lo, ind_ref[...])

        def group(g, carry):
            cols = _lanes(g, GB)
            gcols = _lanes(g, GW)
            conv = cv_ref[:, cols].astype(f32)
            sg = _sigmoid(conv)
            xc = conv * sg
            xs = xc[:, 0:GW]
            bg = xc[:, GW:GW + NST].astype(bf16)
            cg = xc[:, GW + NST:GB].astype(bf16)

            y = y_ref[:, gcols].astype(f32)
            silu_sz, dsilu_sz = _silu_and_grad(sz_ref[:, gcols])
            yz = y * silu_sz
            rr = lax.rsqrt(jnp.mean(yz * yz, axis=-1, keepdims=True) + EPS)
            dyb_g = dyb_ref[:, gcols].astype(f32)
            w = dyb_g * snw_ref[:, gcols]
            dsnw_ref[:, gcols] += jnp.sum(dyb_g * yz * rr, axis=0, keepdims=True)
            dyz = rr * w - yz * (rr * rr * rr) * jnp.mean(w * yz, axis=-1, keepdims=True)
            dp_ref[:, _lanes(g, GW, CONVD)] = (dyz * y * dsilu_sz).astype(bf16)
            dy_g = dyz * silu_sz
            ddk_ref[:, gcols] += jnp.sum(dy_g * xs, axis=0, keepdims=True)

            back = lax.rem(CH - HPG * g, CH)
            dt = pltpu.roll(dt_all, back, 1)
            acs = pltpu.roll(acs_all, back, 1)
            cbm = _dot_nt(cg, bg)
            d_cb = jnp.zeros((CH, CH), f32)
            d_bg = jnp.zeros((CH, NST), f32)
            d_cg = jnp.zeros((CH, NST), f32)
            lastrow = jnp.zeros((1, CH), f32)
            dxd_parts, dxs_parts, t_parts = [], [], []
            for r in range(HPG):
                h = g * HPG + r
                colb = jnp.broadcast_to(acs[:, r:r + 1], (CH, CH))
                row = acst_s[pl.ds(h, 1), :]
                lmat = jnp.exp(jnp.where(mask, colb - row, -jnp.inf))
                mmat_b = (cbm * lmat).astype(bf16)
                dtc = dt[:, r:r + 1]
                xd = xs[:, r * HD:(r + 1) * HD] * dtc
                col = colb[:, 0:HD]
                alast = colb[CH - 1:CH, 0:HD]
                dte = jnp.exp(alast - col)
                ea = jnp.exp(col)
                cd = jnp.exp(colb[CH - 1:CH, :])
                sp = st_ref[0, h]
                dsn = dstate[h]
                dsn_b = dsn.astype(bf16)
                dyr = dy_g[:, r * HD:(r + 1) * HD]
                dyr_b = dyr.astype(bf16)
                dye_b = (dyr * ea).astype(bf16)
                d_cg = d_cg + _dot(dye_b, sp)
                dxde = _dot_nt(bg, dsn_b)
                xdte = xd * dte
                xd_b = xd.astype(bf16)
                d_bg = d_bg + _dot(xdte.astype(bf16), dsn_b)
                dxd_diag = _dot_tn(mmat_b, dyr_b)
                dxd = dxde * dte + dxd_diag
                d_cb = d_cb + _dot_nt(dyr_b, xd_b) * lmat
                t_parts.append(dyr_b.astype(f32) * _dot(mmat_b, xd_b) + dyr * (ea * _dot_nt(cg, sp))
                               - xd_b.astype(f32) * dxd_diag - dxde * xdte)
                lastrow = jnp.where(lane1 == r, _sum_all(dsn * sp.astype(f32)) * cd + _sum_all(dxde * xdte), lastrow)
                dstate[h] = cd * dsn + _dot_tn(dye_b, cg)
                dxd_parts.append(dxd)
                dxs_parts.append(dxd * dtc)
            d_cb_b = d_cb.astype(bf16)
            d_bg = d_bg + _dot_tn(d_cb_b, cg)
            d_cg = d_cg + _dot(d_cb_b, bg)
            q2 = ind_sum(jnp.concatenate(dxd_parts, axis=1) * xs)
            dacs = ind_sum(jnp.concatenate(t_parts, axis=1)) + jnp.where(is_last, lastrow, 0.0)
            dacs_acc[...] += pltpu.roll(dacs, HPG * g, 1)
            q2_acc[...] += pltpu.roll(q2, HPG * g, 1)
            dxs = jnp.concatenate(dxs_parts, axis=1) + dy_g * dkc_ref[:, gcols]

            dconv = jnp.concatenate([dxs, d_bg, d_cg], axis=1) * (sg * (1.0 + conv * (1.0 - sg)))
            x = xbc_ref[:, cols]
            dcb_ref[:, cols] += jnp.sum(dconv, axis=0, keepdims=True)
            dcw_ref[3:4, cols] += jnp.sum(dconv * x, axis=0, keepdims=True)
            dx = cw_ref[3:4, cols] * dconv
            nxt = dnext8[:, cols]
            for j in (1, 2, 3):
                up = _shift_rows_up(dconv, nxt, j)
                dcw_ref[3 - j:4 - j, cols] += jnp.sum(up * x, axis=0, keepdims=True)
                dx = dx + cw_ref[3 - j:4 - j, cols] * up
            dnext8[:, cols] = dconv[0:8]
            dp_ref[:, cols] = dx.astype(bf16)
            return carry

        lax.fori_loop(0, NG, group, 0)

        dadt = _dot_hi(tri_t, dacs_acc[...])
        dal_ref[...] += jnp.sum(dadt * adt_all, axis=0, keepdims=True)
        ddz = (dadt * a_ref[...] + q2_acc[...]) * _sigmoid(z_all)
        ddtb_ref[...] += jnp.sum(ddz, axis=0, keepdims=True)
        dp_ref[:, CONVD + DIN:CONVD + DIN + CH] = ddz.astype(bf16)
        dp_ref[:, CONVD + DIN + CH:PB] = jnp.zeros((CH, PB - CONVD - DIN - CH), bf16)

    const2 = lambda c: (0, 0)
    rev = lambda c: (nc - 1 - c, 0)
    return pl.pallas_call(
        body, grid=(nc,),
        in_specs=[pl.BlockSpec((CH, CONVD), rev), pl.BlockSpec((CH, CONVD), rev),
                  pl.BlockSpec((CH, DIN), lambda c: (nc - 1 - c, CONVD // DIN)),
                  pl.BlockSpec((CH, 256), lambda c: (nc - 1 - c, (CONVD + DIN) // 256)),
                  pl.BlockSpec((CH, DIN), rev), pl.BlockSpec((CH, DIN), rev),
                  pl.BlockSpec((1, NH, HD, NST), lambda c: (nc - 1 - c, 0, 0, 0)),
                  pl.BlockSpec((4, CONVD), const2), pl.BlockSpec((1, CH), const2), pl.BlockSpec((1, CH), const2),
                  pl.BlockSpec((1, DIN), const2), pl.BlockSpec((1, DIN), const2), pl.BlockSpec((GW, CH), const2)],
        out_specs=[pl.BlockSpec((CH, PB), rev), pl.BlockSpec((4, CONVD), const2), pl.BlockSpec((1, CONVD), const2),
                   pl.BlockSpec((1, CH), const2), pl.BlockSpec((1, CH), const2), pl.BlockSpec((1, DIN), const2),
                   pl.BlockSpec((1, DIN), const2)],
        out_shape=[jax.ShapeDtypeStruct((s, PB), bf16), jax.ShapeDtypeStruct((4, CONVD), f32),
                   jax.ShapeDtypeStruct((1, CONVD), f32), jax.ShapeDtypeStruct((1, CH), f32),
                   jax.ShapeDtypeStruct((1, CH), f32), jax.ShapeDtypeStruct((1, DIN), f32),
                   jax.ShapeDtypeStruct((1, DIN), f32)],
        scratch_shapes=[pltpu.VMEM((NH, HD, NST), f32), pltpu.VMEM((8, CONVD), f32), pltpu.VMEM((CH, CH), f32),
                        pltpu.VMEM((CH, CH), f32), pltpu.VMEM((CH, CH), f32)],
        name=name, compiler_params=_cp(("arbitrary",)),
    )(proj_b, conv_sv, proj_b, proj_b, dyb, y_sv, states, cw, dtb, a_row, dkc, snw, ind)


def _merge_fwd(ya, yb, proj_g, x, gate, wa, wb, wo, name):
    s = x.shape[0]
    ts = min(512, s)

    def body(ya_ref, yb_ref, ga_ref, gb_ref, x_ref, gate_ref, wa_ref, wb_ref, wo_ref, xo_ref, pa_ref, pb_ref, mg_ref, o_ref):
        pa = _dot(ya_ref[...], wa_ref[...])
        pb = _dot(yb_ref[...], wb_ref[...])
        mg = (_sigmoid(ga_ref[...]) * pa + _sigmoid(gb_ref[...]) * pb).astype(bf16)
        o = _dot(mg, wo_ref[...])
        xo_ref[...] = x_ref[...] + gate_ref[...] * o
        pa_ref[...] = pa.astype(bf16)
        pb_ref[...] = pb.astype(bf16)
        mg_ref[...] = mg
        o_ref[...] = o.astype(bf16)

    tile = pl.BlockSpec((ts, D), lambda i: (i, 0))
    const = lambda i: (0, 0)
    act = jax.ShapeDtypeStruct((s, D), bf16)
    return pl.pallas_call(
        body, grid=(s // ts,),
        in_specs=[tile, pl.BlockSpec((ts, DIN), lambda i: (i, 0)), tile, pl.BlockSpec((ts, D), lambda i: (i, 1)), tile,
                  pl.BlockSpec((1, D), const), pl.BlockSpec((D, D), const), pl.BlockSpec((DIN, D), const),
                  pl.BlockSpec((D, D), const)],
        out_specs=[tile, tile, tile, tile, tile],
        out_shape=[jax.ShapeDtypeStruct((s, D), f32), act, act, act, act],
        name=name, compiler_params=_cp(("parallel",)),
    )(ya, yb, proj_g, proj_g, x, gate, wa, wb, wo)


def _merge_bwd(dxo, gate, o_sv, pa_sv, pb_sv, proj_g, wo, wa, wb, name):
    s = dxo.shape[0]
    ts = min(512, s)

    def body(dxo_ref, gate_ref, o_ref, pa_ref, pb_ref, ga_ref, gb_ref, wo_ref, wa_ref, wb_ref,
             do_ref, dpa_ref, dpb_ref, dg_ref, dya_ref, dyb_ref, dgate_ref):
        @pl.when(pl.program_id(0) == 0)
        def _():
            dgate_ref[...] = jnp.zeros_like(dgate_ref)
        dxo_v = dxo_ref[...]
        dgate_ref[...] += jnp.sum(dxo_v * o_ref[...].astype(f32), axis=0, keepdims=True)
        do = (dxo_v * gate_ref[...]).astype(bf16)
        do_ref[...] = do
        dmg = _dot_nt(do, wo_ref[...])
        sa = _sigmoid(ga_ref[...])
        sb = _sigmoid(gb_ref[...])
        dpa = (dmg * sa).astype(bf16)
        dpb = (dmg * sb).astype(bf16)
        dpa_ref[...] = dpa
        dpb_ref[...] = dpb
        dg_ref[:, 0:D] = (dmg * pa_ref[...].astype(f32) * sa * (1.0 - sa)).astype(bf16)
        dg_ref[:, D:2 * D] = (dmg * pb_ref[...].astype(f32) * sb * (1.0 - sb)).astype(bf16)
        dya_ref[...] = _dot_nt(dpa, wa_ref[...]).astype(bf16)
        dyb_ref[...] = _dot_nt(dpb, wb_ref[...]).astype(bf16)

    tile = pl.BlockSpec((ts, D), lambda i: (i, 0))
    const = lambda i: (0, 0)
    act = jax.ShapeDtypeStruct((s, D), bf16)
    return pl.pallas_call(
        body, grid=(s // ts,),
        in_specs=[tile, pl.BlockSpec((1, D), const), tile, tile, tile, tile, pl.BlockSpec((ts, D), lambda i: (i, 1)),
                  pl.BlockSpec((D, D), const), pl.BlockSpec((D, D), const), pl.BlockSpec((DIN, D), const)],
        out_specs=[tile, tile, tile, pl.BlockSpec((ts, PG), lambda i: (i, 0)), tile, pl.BlockSpec((ts, DIN), lambda i: (i, 0)),
                   pl.BlockSpec((1, D), const)],
        out_shape=[act, act, act, jax.ShapeDtypeStruct((s, PG), bf16), act, jax.ShapeDtypeStruct((s, DIN), bf16),
                   jax.ShapeDtypeStruct((1, D), f32)],
        name=name, compiler_params=_cp(("arbitrary",)),
    )(dxo, gate, o_sv, pa_sv, pb_sv, proj_g, proj_g, wo, wa, wb)


def _final_loss(x, target, fnw, name):
    s = x.shape[0]
    ts = min(512, s)

    def body(x_ref, t_ref, w_ref, loss_ref, dx_ref, dw_ref):
        @pl.when(pl.program_id(0) == 0)
        def _():
            loss_ref[...] = jnp.zeros_like(loss_ref)
            dw_ref[...] = jnp.zeros_like(dw_ref)
        xv = x_ref[...]
        r = lax.rsqrt(jnp.mean(xv * xv, axis=-1, keepdims=True) + EPS)
        xn = xv * r
        err = xn * w_ref[...] - t_ref[...]
        part = jnp.sum(err * err, axis=0, keepdims=True)
        acc = part[:, 0:128]
        for k in range(1, D // 128):
            acc = acc + part[:, k * 128:(k + 1) * 128]
        loss_ref[0:1, :] += acc * (0.5 / D)
        dy = err * (1.0 / D)
        dw_ref[...] += jnp.sum(dy * xn, axis=0, keepdims=True)
        dxn = dy * w_ref[...]
        dx_ref[...] = r * (dxn - xn * jnp.mean(dxn * xn, axis=-1, keepdims=True))

    tile = pl.BlockSpec((ts, D), lambda i: (i, 0))
    row = pl.BlockSpec((1, D), lambda i: (0, 0))
    return pl.pallas_call(
        body, grid=(s // ts,), in_specs=[tile, tile, row],
        out_specs=[pl.BlockSpec((8, 128), lambda i: (0, 0)), tile, row],
        out_shape=[jax.ShapeDtypeStruct((8, 128), f32), jax.ShapeDtypeStruct((s, D), f32), jax.ShapeDtypeStruct((1, D), f32)],
        name=name, compiler_params=_cp(("arbitrary",)),
    )(x, target, fnw)


def _layer_operands(w_in, conv_w, wa, wb, wo, norm_w, gm_ln_w, gm_ln_b, gm_ws, gm_bs, conv_b, dt_bias, a_log, d_skip, ssm_norm_w):
    sz0, x0 = PA, PA + DIN
    b0, c0, dt0 = x0 + DIN, x0 + DIN + NG * NST, x0 + CONVD
    w_xbc = _group_major(w_in[:, x0:b0], w_in[:, b0:c0], w_in[:, c0:dt0])
    w_b = jnp.concatenate([w_xbc, w_in[:, sz0:x0], w_in[:, dt0:dt0 + NH], jnp.zeros((D, PB - CONVD - DIN - NH), bf16)], axis=1)
    w_a = w_in[:, 0:PA]
    w_g = w_in[:, dt0 + NH:N_IN]
    tril = jnp.tril(jnp.ones((CH, CH), bool))
    wsm = jnp.where(tril[None], gm_ws, 0.0).astype(bf16)

    def heads_row(v):
        return jnp.pad(v, (0, CH - NH)).reshape(1, CH)

    return dict(
        w_b=w_b, w_a=w_a, w_g=w_g, wa=wa, wb=wb, wo=wo,
        norm_w=norm_w.reshape(1, D), lnw=gm_ln_w.reshape(1, D), lnb=gm_ln_b.reshape(1, D),
        wsm=wsm, wsm_t=jnp.swapaxes(wsm, 1, 2), bsf=jnp.repeat(gm_bs.T, CH, axis=1),
        cw=_group_major(conv_w[:, 0:DIN], conv_w[:, DIN:DIN + NG * NST], conv_w[:, DIN + NG * NST:CONVD]),
        cb=_group_major(conv_b[0:DIN], conv_b[DIN:DIN + NG * NST], conv_b[DIN + NG * NST:CONVD]).reshape(1, CONVD),
        dtb=heads_row(dt_bias), a_row=heads_row(-jnp.exp(a_log)),
        snw=ssm_norm_w.reshape(1, DIN), dkc=jnp.repeat(d_skip, HD).reshape(1, DIN),
        ind=(jnp.arange(GW)[:, None] // HD == jnp.arange(CH)[None, :]).astype(bf16),
    )


def _layer_fwd(x, shift, scale, gate, p, tag):
    h = _rmsmod_fwd(x, p["norm_w"], scale, shift, f"rmsmod_fwd{tag}")
    proj_b = _mm(h, p["w_b"], f32, f"proj_b{tag}")
    proj_a = _mm(h, p["w_a"], f32, f"proj_a{tag}")
    proj_g = _mm(h, p["w_g"], f32, f"proj_g{tag}")
    ya = _branch_a_fwd(proj_a, p["lnw"], p["lnb"], p["wsm"], p["bsf"], f"branch_a_fwd{tag}")
    yb, y_sv, states, conv_sv = _branch_b_fwd(proj_b, p["cw"], p["cb"], p["dtb"], p["a_row"], p["dkc"], p["snw"], f"branch_b_fwd{tag}")
    if "merge_weights" in p:
        p["wa"], p["wb"], p["wo"], yb = p.pop("merge_weights")(yb)
    x_out, pa, pb, mg, o = _merge_fwd(ya, yb, proj_g, x, gate, p["wa"], p["wb"], p["wo"], f"merge_fwd{tag}")
    saved = dict(x=x, h=h, proj_b=proj_b, proj_a=proj_a, proj_g=proj_g, ya=ya, yb=yb, y=y_sv, states=states, conv=conv_sv,
                 pa=pa, pb=pb, mg=mg, o=o, scale=scale, gate=gate)
    return x_out, saved


def _layer_bwd(dxo, sv, p, tag):
    do, dpa, dpb, dg, dya, dyb, dgate = _merge_bwd(dxo, sv["gate"], sv["o"], sv["pa"], sv["pb"], sv["proj_g"],
                                                   p["wo"], p["wa"], p["wb"], f"merge_bwd{tag}")
    d_wo = _mm_tn(sv["mg"], do, f"d_wo{tag}")
    d_wa = _mm_tn(sv["ya"], dpa, f"d_wa{tag}")
    d_wb = _mm_tn(sv["yb"], dpb, f"d_wb{tag}")
    da, dws, dbs, dlnw, dlnb = _branch_a_bwd(sv["proj_a"], dya, p["lnw"], p["lnb"], p["wsm"], p["wsm_t"], p["bsf"],
                                             f"branch_a_bwd{tag}")
    db, dcw, dcb, ddtb, dal, ddk, dsnw = _branch_b_bwd(sv["proj_b"], sv["conv"], dyb, sv["y"], sv["states"], p["cw"], p["dtb"],
                                                       p["a_row"], p["dkc"], p["snw"], p["ind"], f"branch_b_bwd{tag}")
    dh = _mm(db, p["w_b"], f32, f"dh_b{tag}", trans_b=True)
    dh = _mm(da, p["w_a"], f32, f"dh_a{tag}", c_in=dh, trans_b=True)
    dh = _mm(dg, p["w_g"], f32, f"dh_g{tag}", c_in=dh, trans_b=True)
    d_w_b_t = _mm_tn(db, sv["h"], f"d_w_b{tag}")
    d_w_a_t = _mm_tn(da, sv["h"], f"d_w_a{tag}")
    d_w_g_t = _mm_tn(dg, sv["h"], f"d_w_g{tag}")
    dx, dscale, dshift, dnw = _rmsmod_bwd(dh, sv["x"], dxo, p["norm_w"], sv["scale"], f"rmsmod_bwd{tag}")
    d_w_in_t = jnp.concatenate([d_w_a_t, d_w_b_t[CONVD:CONVD + DIN], _rows_from_group_major(d_w_b_t[0:CONVD]),
                                d_w_b_t[CONVD + DIN:CONVD + DIN + NH], d_w_g_t], axis=0)
    tril = jnp.tril(jnp.ones((CH, CH), bool))
    heads = lambda v: v[0, 0:NH]
    grads = dict(
        w_in_t=d_w_in_t, w_proj_a=d_wa, w_proj_b=d_wb, w_out=d_wo, conv_w=_from_group_major(dcw), conv_b=_from_group_major(dcb).reshape(CONVD),
        norm_w=dnw.reshape(D), gm_ln_w=dlnw.reshape(D), gm_ln_b=dlnb.reshape(D),
        gm_ws=jnp.where(tril[None], dws, 0.0), gm_bs=dbs.reshape(CH, NG, CH).sum(-1).T,
        dt_bias=heads(ddtb), a_log=heads(dal), d_skip=ddk.reshape(NH, HD).sum(-1), ssm_norm_w=dsnw.reshape(DIN),
        mod=jnp.concatenate([dshift, dscale, dgate], axis=1).reshape(3 * D),
    )
    return dx, grads


def _local_step(x, target, mods, operands_of, fnw):
    saved, layer_ops = [], []
    for l in range(DEPTH):
        shift, scale, gate = mods[l]
        p, x = operands_of(l, x)
        layer_ops.append(p)
        x, sv = _layer_fwd(x, shift, scale, gate, p, f"_l{l}")
        saved.append(sv)
    loss_parts, dx, dfnw = _final_loss(x, target, fnw.reshape(1, D), "final_loss")
    grads = [None] * DEPTH
    for l in reversed(range(DEPTH)):
        dx, grads[l] = _layer_bwd(dx, saved[l], layer_ops[l], f"_l{l}")
    return loss_parts, dx, grads, dfnw.reshape(D)


ADA_COLS = 3 * D // NSHARD


def _ada_fwd(c_all, ada_w, ada_b_cols, name):
    def body(c_ref, w_ref, b_ref, o_ref):
        cv = c_ref[...]
        sc = cv * _sigmoid(cv)
        for l in range(DEPTH):
            o_ref[l] = _dot_hi(sc, w_ref[l]) + b_ref[l]

    return pl.pallas_call(body, out_shape=jax.ShapeDtypeStruct((DEPTH, 8, ADA_COLS), f32), name=name,
                          compiler_params=_cp(None))(c_all, ada_w, ada_b_cols)


def _adam_math(w, g, m, v):
    m = ADAM_B1 * m + (1.0 - ADAM_B1) * g
    v = ADAM_B2 * v + (1.0 - ADAM_B2) * (g * g)
    m_hat = m / (1.0 - ADAM_B1 ** ADAM_STEP)
    v_hat = v / (1.0 - ADAM_B2 ** ADAM_STEP)
    delta = -ADAM_LR * (m_hat / (jnp.sqrt(v_hat) + ADAM_EPS) + ADAM_WD * w)
    return delta, m, v


def _ada_bwd_adamw(c_all, dmod_cols, w, m, v, name):
    tr = 256

    def body(c_ref, dm_ref, w_ref, m_ref, v_ref, g_ref, d_ref, nm_ref, nv_ref):
        cv = c_ref[...]
        sc = cv * _sigmoid(cv)
        g = lax.dot_general(sc, dm_ref[0], (((0,), (0,)), ((), ())), precision=_HI, preferred_element_type=f32)
        g_ref[0] = g
        d_ref[0], nm_ref[0], nv_ref[0] = _adam_math(w_ref[0], g, m_ref[0], v_ref[0])

    blk = pl.BlockSpec((1, tr, ADA_COLS), lambda l, i: (l, i, 0))
    shp = jax.ShapeDtypeStruct((DEPTH, D, ADA_COLS), f32)
    return pl.pallas_call(
        body, grid=(DEPTH, D // tr),
        in_specs=[pl.BlockSpec((8, tr), lambda l, i: (0, i)), pl.BlockSpec((1, 8, ADA_COLS), lambda l, i: (l, 0, 0)), blk, blk, blk],
        out_specs=[blk, blk, blk, blk], out_shape=[shp, shp, shp, shp], name=name, compiler_params=_cp(("parallel", "parallel")),
    )(c_all, dmod_cols, w, m, v)


def _adamw(w, g, m, v, name):
    def body(w_ref, g_ref, m_ref, v_ref, d_ref, nm_ref, nv_ref):
        d_ref[...], nm_ref[...], nv_ref[...] = _adam_math(w_ref[...], g_ref[...], m_ref[...], v_ref[...])

    shp = jax.ShapeDtypeStruct(w.shape, f32)
    return pl.pallas_call(body, out_shape=[shp] * 3, name=name, compiler_params=_cp(None))(w, g, m, v)


def _tile2(r, c):
    if r <= 256 or r % 256 == 0:
        return _pick(r, (256,)), _pick(c, (1024,))
    return r, 128


def _adamw_layers(w, g_mine, g_other, m, v, c_idx, name):
    _, r, c = w.shape
    tr, tc = _tile2(r, c)

    def body(ci_ref, w_ref, gm_ref, go_ref, m_ref, v_ref, g_ref, d_ref, nm_ref, nv_ref):
        def update(g):
            g_ref[0] = g
            d_ref[0], nm_ref[0], nv_ref[0] = _adam_math(w_ref[0], g, m_ref[0], v_ref[0])

        mine = pl.program_id(0) == ci_ref[0]

        @pl.when(mine)
        def _():
            update(gm_ref[...])

        @pl.when(jnp.logical_not(mine))
        def _():
            update(go_ref[...])

    blk = pl.BlockSpec((1, tr, tc), lambda l, i, j, ci: (l, i, j))
    gblk = pl.BlockSpec((tr, tc), lambda l, i, j, ci: (i, j))
    shp = jax.ShapeDtypeStruct(w.shape, f32)
    return pl.pallas_call(
        body,
        grid_spec=pltpu.PrefetchScalarGridSpec(num_scalar_prefetch=1, grid=(DEPTH, r // tr, c // tc),
                                               in_specs=[blk, gblk, gblk, blk, blk], out_specs=[blk, blk, blk, blk]),
        out_shape=[shp, shp, shp, shp], name=name, compiler_params=_cp(("parallel", "parallel", "parallel")),
    )(c_idx, w, g_mine, g_other, m, v)


_MESH = pl.DeviceIdType.MESH
_AXES = ("x", "y", "c")
_HBM = pl.BlockSpec(memory_space=pltpu.HBM)


def _my_place():
    return tuple(lax.axis_index(a) for a in _AXES)


def _allreduce(buf, axes, name):
    r = buf.shape[0]
    n = len(axes)

    def body(x_ref, o_ref, rbuf, ssem, rsem):
        me = dict(zip(_AXES, _my_place()))
        o_ref[...] = x_ref[...]
        for k, ax in enumerate(axes):
            peer = tuple(1 - me[a] if a == ax else me[a] for a in _AXES)
            cp = pltpu.make_async_remote_copy(src_ref=o_ref, dst_ref=rbuf.at[k], send_sem=ssem.at[k], recv_sem=rsem.at[k],
                                              device_id=peer, device_id_type=_MESH)
            cp.start()
            cp.wait()
            o_ref[...] = o_ref[...] + rbuf[k]

    vm = pl.BlockSpec(memory_space=pltpu.VMEM)
    return pl.pallas_call(
        body, out_shape=jax.ShapeDtypeStruct((r, 128), f32), in_specs=[vm], out_specs=vm,
        scratch_shapes=[pltpu.VMEM((n, r, 128), f32), pltpu.SemaphoreType.DMA((n,)), pltpu.SemaphoreType.DMA((n,))],
        name=name, compiler_params=pltpu.CompilerParams(vmem_limit_bytes=VMEM_LIMIT),
    )(buf)


def _other_chips(x, y):
    return [(1 - x, y), (x, 1 - y), (1 - x, 1 - y)]


def _gather_body(ins, outs, ssem, rsem):
    na = len(ins)
    x, y, c = _my_place()
    k_me = 2 * x + y
    sibling = (x, y, 1 - c)
    chips = _other_chips(x, y)
    slots = [2 * cx + cy for cx, cy in chips]
    half = [r.shape[0] // 2 if r.shape[0] % 32 == 0 else None for r in ins]

    def part(ref, a, core):
        return ref if half[a] is None else ref.at[pl.ds(core * half[a], half[a])]

    def rcopy(a, src, slot, core, to, idx):
        return pltpu.make_async_remote_copy(src_ref=src, dst_ref=part(outs[a].at[slot], a, core), send_sem=ssem.at[idx],
                                            recv_sem=rsem.at[idx], device_id=to, device_id_type=_MESH)

    sent = []
    for j, chip in enumerate(chips):
        for a in range(na):
            cp = rcopy(a, part(ins[a], a, c), k_me, c, (*chip, c), j * na + a)
            cp.start()
            sent.append(cp)
    for j, chip in enumerate(chips):
        for a in range(na):
            rcopy(a, part(ins[a], a, c), slots[j], c, (*chip, c), j * na + a).wait_recv()
            if half[a] is not None:
                cp = rcopy(a, part(outs[a].at[slots[j]], a, c), slots[j], c, sibling, (3 + j) * na + a)
                cp.start()
                sent.append(cp)
    for j in range(3):
        for a in range(na):
            if half[a] is not None:
                rcopy(a, part(ins[a], a, c), slots[j], 1 - c, sibling, (3 + j) * na + a).wait_recv()
    for cp in sent:
        cp.wait_send()


def _gather_layer_behind(shards, name, collective_id):
    na = len(shards)
    hbm = pltpu.MemorySpace.HBM
    ins = [jax.new_ref(s, memory_space=hbm) for s in shards]
    outs = [jax.empty_ref(jax.ShapeDtypeStruct((NSHARD,) + s.shape, s.dtype), memory_space=hbm) for s in shards]

    @pl.kernel(mesh=plsc.ScalarSubcoreMesh(axis_name="sequencer", num_cores=1), name=name,
               scratch_types=(pltpu.SemaphoreType.DMA((6 * na,)), pltpu.SemaphoreType.DMA((6 * na,))),
               compiler_params=pltpu.CompilerParams(collective_id=collective_id))
    def launch(ssem, rsem):
        x, y, c = _my_place()
        barrier = pltpu.get_barrier_semaphore()
        peers = [(*chip, c) for chip in _other_chips(x, y)] + [(x, y, 1 - c)]
        for peer in peers:
            pl.semaphore_signal(barrier, inc=1, device_id=peer, device_id_type=_MESH)
        pl.semaphore_wait(barrier, len(peers))
        _gather_body(ins, outs, ssem, rsem)

    launch()
    return [o[...] for o in outs]


def _with_own(gathered, own):
    xi, yi, _ = _my_place()
    whole = lax.dynamic_update_index_in_dim(gathered, own, 2 * xi + yi, 0)
    return [whole[k] for k in range(NSHARD)]


def _swap_layers(parts, name):
    na = len(parts)

    def body(*refs):
        ins, outs = refs[:2 * na], refs[2 * na:3 * na]
        ssem, rsem = refs[3 * na:]
        x, y, c = _my_place()

        def copy(a, layer):
            return pltpu.make_async_remote_copy(src_ref=ins[2 * a + layer], dst_ref=outs[a], send_sem=ssem.at[a], recv_sem=rsem.at[a],
                                                device_id=(x, y, 1 - c), device_id_type=_MESH)

        for layer in range(DEPTH):
            @pl.when(c == 1 - layer)
            def _():
                for a in range(na):
                    copy(a, layer).start()
        for a in range(na):
            copy(a, 0).wait()

    flat = [p for pair in parts for p in pair]
    return pl.pallas_call(
        body, out_shape=[jax.ShapeDtypeStruct(p0.shape, p0.dtype) for p0, _ in parts], in_specs=[_HBM] * (2 * na),
        out_specs=[_HBM] * na, scratch_shapes=[pltpu.SemaphoreType.DMA((na,)), pltpu.SemaphoreType.DMA((na,))], name=name,
    )(*flat)


def _scatter_shards_behind(sums, name, collective_id):
    na = len(sums)
    hbm = pltpu.MemorySpace.HBM
    ins = [jax.new_ref(p, memory_space=hbm) for p in sums]
    outs = [jax.empty_ref(jax.ShapeDtypeStruct((3,) + p.shape[1:], p.dtype), memory_space=hbm) for p in sums]

    @pl.kernel(mesh=plsc.ScalarSubcoreMesh(axis_name="sequencer", num_cores=1), name=name,
               scratch_types=(pltpu.SemaphoreType.DMA((3 * na,)), pltpu.SemaphoreType.DMA((3 * na,))),
               compiler_params=pltpu.CompilerParams(collective_id=collective_id))
    def launch(ssem, rsem):
        x, y, c = _my_place()
        barrier = pltpu.get_barrier_semaphore()
        chips = _other_chips(x, y)
        for chip in chips:
            pl.semaphore_signal(barrier, inc=1, device_id=(*chip, c), device_id_type=_MESH)
        pl.semaphore_wait(barrier, len(chips))
        cps = []
        for j, chip in enumerate(chips):
            kj = 2 * chip[0] + chip[1]
            for a in range(na):
                cps.append(pltpu.make_async_remote_copy(
                    src_ref=ins[a].at[kj], dst_ref=outs[a].at[j], send_sem=ssem.at[j * na + a], recv_sem=rsem.at[j * na + a],
                    device_id=(*chip, c), device_id_type=_MESH))
        for cp in cps:
            cp.start()
        for cp in cps:
            cp.wait()

    launch()
    return [o[...] for o in outs]


def _share_layers(finals, name):
    na = len(finals)

    def body(*refs):
        ins, outs = refs[:na], refs[na:2 * na]
        ssem, rsem = refs[2 * na:]
        x, y, c = _my_place()
        cps = [pltpu.make_async_remote_copy(src_ref=ins[a], dst_ref=outs[a], send_sem=ssem.at[a], recv_sem=rsem.at[a],
                                            device_id=(x, y, 1 - c), device_id_type=_MESH) for a in range(na)]
        for cp in cps:
            cp.start()
        for cp in cps:
            cp.wait()

    return pl.pallas_call(
        body, out_shape=[jax.ShapeDtypeStruct(p.shape, p.dtype) for p in finals], in_specs=[_HBM] * na, out_specs=[_HBM] * na,
        scratch_shapes=[pltpu.SemaphoreType.DMA((na,)), pltpu.SemaphoreType.DMA((na,))], name=name,
    )(*finals)


def _add_own_layer(part0, part1, recv, c_idx, name):
    ns, r, c = recv.shape
    tr, tc = _tile2(r, c)

    def body(ci_ref, p0_ref, p1_ref, r_ref, o_ref, ob_ref):
        def add(p_ref):
            t = p_ref[...] + r_ref[...]
            o_ref[...] = t
            ob_ref[...] = t.astype(bf16)

        @pl.when(ci_ref[0] == 0)
        def _():
            add(p0_ref)

        @pl.when(ci_ref[0] == 1)
        def _():
            add(p1_ref)

    blk = pl.BlockSpec((1, tr, tc), lambda k, i, j, ci: (k, i, j))
    blk0 = pl.BlockSpec((1, tr, tc), lambda k, i, j, ci: (k * (1 - ci[0]), i * (1 - ci[0]), j * (1 - ci[0])))
    blk1 = pl.BlockSpec((1, tr, tc), lambda k, i, j, ci: (k * ci[0], i * ci[0], j * ci[0]))
    return pl.pallas_call(
        body,
        grid_spec=pltpu.PrefetchScalarGridSpec(num_scalar_prefetch=1, grid=(ns, r // tr, c // tc), in_specs=[blk0, blk1, blk],
                                               out_specs=[blk, blk]),
        out_shape=[jax.ShapeDtypeStruct((ns, r, c), f32), jax.ShapeDtypeStruct((ns, r, c), bf16)], name=name,
        compiler_params=_cp(("arbitrary", "arbitrary", "arbitrary")),
    )(c_idx, part0, part1, recv)


def _add_own_shard(sums, recv, k_idx, name):
    _, r, c = sums.shape
    tr, tc = _tile2(r, c)

    def body(ki_ref, s_ref, r_ref, o_ref):
        o_ref[...] = ((s_ref[0] + r_ref[0].astype(f32)) + r_ref[1].astype(f32)) + r_ref[2].astype(f32)

    return pl.pallas_call(
        body,
        grid_spec=pltpu.PrefetchScalarGridSpec(
            num_scalar_prefetch=1, grid=(r // tr, c // tc),
            in_specs=[pl.BlockSpec((1, tr, tc), lambda i, j, ki: (ki[0], i, j)), pl.BlockSpec((3, tr, tc), lambda i, j, ki: (0, i, j))],
            out_specs=pl.BlockSpec((tr, tc), lambda i, j, ki: (i, j))),
        out_shape=jax.ShapeDtypeStruct((r, c), f32), name=name, compiler_params=_cp(("parallel", "parallel")),
    )(k_idx, sums, recv)


def _reduce_scatter(parts, tag, meanwhile):
    x, y, c = _my_place()
    c_idx = jnp.reshape(c, (1,)).astype(jnp.int32)
    k_idx = jnp.reshape(2 * x + y, (1,)).astype(jnp.int32)
    na = len(parts)
    recv = _swap_layers(parts, f"rs_swap{tag}")
    sums = [_add_own_layer(parts[a][0], parts[a][1], recv[a], c_idx, f"rs_add_layer{tag}_{a}") for a in range(na)]
    recv = _scatter_shards_behind([sb for _, sb in sums], f"rs_scatter{tag}", 3)
    recv, side = lax.optimization_barrier((recv, meanwhile()))
    finals = [_add_own_shard(sums[a][0], recv[a], k_idx, f"rs_add_shard{tag}_{a}") for a in range(na)]
    return finals, _share_layers(finals, f"rs_share{tag}"), c_idx, side


_SMALL = [("ada_b", (DEPTH, 3 * D)), ("norm_w", (DEPTH, D)), ("gm_ln_w", (DEPTH, D)), ("gm_ln_b", (DEPTH, D)),
          ("gm_ws", (DEPTH, NG, CH, CH)), ("gm_bs", (DEPTH, NG, CH)), ("conv_b", (DEPTH, CONVD)), ("dt_bias", (DEPTH, NH)),
          ("a_log", (DEPTH, NH)), ("d_skip", (DEPTH, NH)), ("ssm_norm_w", (DEPTH, DIN)), ("final_norm_w", (D,))]


def _rows_of(shape):
    n = 1
    for d in shape:
        n *= d
    return -(-n // 1024) * 8


def _pack(arrays):
    rows = []
    for a in arrays:
        flat = a.reshape(-1)
        r = _rows_of(a.shape)
        rows.append(jnp.pad(flat, (0, r * 128 - flat.shape[0])).reshape(r, 128))
    return jnp.concatenate(rows, axis=0)


def _unpack(buf, shapes):
    out, at = [], 0
    for shp in shapes:
        r = _rows_of(shp)
        n = 1
        for d in shp:
            n *= d
        out.append(buf[at:at + r].reshape(-1)[:n].reshape(shp))
        at += r
    return out


def kernel(x, c, ada_w, ada_b, norm_w, w_in, gm_ln_w, gm_ln_b, gm_ws, gm_bs, conv_w, conv_b, dt_bias, a_log, d_skip, ssm_norm_w, w_proj_a, w_proj_b, w_out, final_norm_w, loss_target, m_ada_w, m_ada_b, m_norm_w, m_w_in, m_gm_ln_w, m_gm_ln_b, m_gm_ws, m_gm_bs, m_conv_w, m_conv_b, m_dt_bias, m_a_log, m_d_skip, m_ssm_norm_w, m_w_proj_a, m_w_proj_b, m_w_out, m_final_norm_w, v_ada_w, v_ada_b, v_norm_w, v_w_in, v_gm_ln_w, v_gm_ln_b, v_gm_ws, v_gm_bs, v_conv_w, v_conv_b, v_dt_bias, v_a_log, v_d_skip, v_ssm_norm_w, v_w_proj_a, v_w_proj_b, v_w_out, v_final_norm_w):
    xi, yi, ci = _my_place()
    k_me = 2 * xi + yi
    b_me = 4 * xi + 2 * yi + ci
    w = dict(ada_b=ada_b, norm_w=norm_w, gm_ln_w=gm_ln_w, gm_ln_b=gm_ln_b, gm_ws=gm_ws, gm_bs=gm_bs, conv_b=conv_b, dt_bias=dt_bias,
             a_log=a_log, d_skip=d_skip, ssm_norm_w=ssm_norm_w, final_norm_w=final_norm_w)
    m = dict(ada_b=m_ada_b, norm_w=m_norm_w, gm_ln_w=m_gm_ln_w, gm_ln_b=m_gm_ln_b, gm_ws=m_gm_ws, gm_bs=m_gm_bs, conv_b=m_conv_b,
             dt_bias=m_dt_bias, a_log=m_a_log, d_skip=m_d_skip, ssm_norm_w=m_ssm_norm_w, final_norm_w=m_final_norm_w)
    v = dict(ada_b=v_ada_b, norm_w=v_norm_w, gm_ln_w=v_gm_ln_w, gm_ln_b=v_gm_ln_b, gm_ws=v_gm_ws, gm_bs=v_gm_bs, conv_b=v_conv_b,
             dt_bias=v_dt_bias, a_log=v_a_log, d_skip=v_d_skip, ssm_norm_w=v_ssm_norm_w, final_norm_w=v_final_norm_w)

    c_slot = lax.dynamic_update_slice(jnp.zeros((8, D), f32), c, (b_me, 0))
    c_all = _allreduce(c_slot.reshape(64, 128), _AXES, "gather_c").reshape(8, D)
    ada_b_cols = lax.dynamic_slice(ada_b, (0, k_me * ADA_COLS), (DEPTH, ADA_COLS)).reshape(DEPTH, 1, ADA_COLS)
    mod_cols = _ada_fwd(c_all, ada_w, ada_b_cols, "ada_fwd")
    mod_slot = lax.dynamic_update_slice(jnp.zeros((DEPTH, 8, 3 * D), f32), mod_cols, (0, 0, k_me * ADA_COLS))
    mod_all = _allreduce(mod_slot.reshape(-1, 128), ("x", "y"), "gather_mod").reshape(DEPTH, 8, 3 * D)
    mod_me = lax.dynamic_slice(mod_all, (0, b_me, 0), (DEPTH, 1, 3 * D))
    mods = [(mod_me[l, :, 0:D], mod_me[l, :, D:2 * D], mod_me[l, :, 2 * D:3 * D]) for l in range(DEPTH)]

    rows_sh = jnp.concatenate([w_proj_a, w_proj_b, w_out], axis=1).astype(bf16)
    win_sh = w_in.astype(bf16)
    first = _gather_layer_behind([win_sh[0], conv_w[0]], "gather_l0", 1)
    first, mods, next_in, later = lax.optimization_barrier((first, mods, [rows_sh[0]], [win_sh[1], conv_w[1], rows_sh[1]]))
    first_rows = _gather_layer_behind(next_in, "gather_l0_merge", 5)
    second = _gather_layer_behind(later, "gather_l1", 2)
    others = [(first[0], first[1], first_rows[0]), second]

    def merge_matrices(rows_g, l):
        rows_l = _with_own(rows_g, rows_sh[l])
        return (jnp.concatenate([t[0:RA] for t in rows_l], axis=0), jnp.concatenate([t[RA:RA + RB] for t in rows_l], axis=0),
                jnp.concatenate([t[RA + RB:] for t in rows_l], axis=0))

    def operands_of(l, x_in):
        win_g, conv_g, rows_g = others[l]
        if l == 1:
            (win_g, conv_g, rows_g), x_in = lax.optimization_barrier(((win_g, conv_g, rows_g), x_in))
        w_in_l = jnp.concatenate(_with_own(win_g, win_sh[l]), axis=1)
        cw_l = jnp.concatenate(_with_own(conv_g, conv_w[l]), axis=1)
        wa_l, wb_l, wo_l = merge_matrices(rows_g, l) if l == 1 else (None, None, None)
        p = _layer_operands(w_in_l, cw_l, wa_l, wb_l, wo_l, norm_w[l], gm_ln_w[l], gm_ln_b[l], gm_ws[l], gm_bs[l], conv_b[l],
                            dt_bias[l], a_log[l], d_skip[l], ssm_norm_w[l])
        if l == 0:
            def late(yb):
                rows_now, yb = lax.optimization_barrier((rows_g, yb))
                return (*merge_matrices(rows_now, l), yb)
            p["merge_weights"] = late
        return p, x_in

    loss_parts, dx, grads, dfnw = _local_step(x[0], loss_target[0], mods, operands_of, final_norm_w)

    s_in = N_IN // NSHARD
    tr_ = lambda t: jnp.swapaxes(t, 1, 2)
    g_in = [grads[l]["w_in_t"].reshape(NSHARD, s_in, D) for l in range(DEPTH)]
    g_rows = [jnp.concatenate([grads[l]["w_proj_a"].reshape(NSHARD, RA, D), grads[l]["w_proj_b"].reshape(NSHARD, RB, D),
                               grads[l]["w_out"].reshape(NSHARD, RA, D)], axis=1) for l in range(DEPTH)]
    g_conv = [grads[l]["conv_w"].reshape(4, NSHARD, D).transpose(1, 0, 2) for l in range(DEPTH)]

    dmod_slot = lax.dynamic_update_slice(jnp.zeros((DEPTH, 8, 3 * D), f32),
                                         jnp.stack([grads[l]["mod"] for l in range(DEPTH)]).reshape(DEPTH, 1, 3 * D), (0, b_me, 0))
    small_g = {n: (dfnw if n == "final_norm_w" else jnp.stack([grads[l]["mod" if n == "ada_b" else n] for l in range(DEPTH)]))
               for n, _ in _SMALL}
    n_small = sum(_rows_of(s) for _, s in _SMALL)
    n_dmod = _rows_of(dmod_slot.shape)

    def small_work():
        packed = _allreduce(_pack([small_g[n] for n, _ in _SMALL] + [dmod_slot, loss_parts]), _AXES, "allreduce_small")
        dmod_all = packed[n_small:n_small + n_dmod].reshape(DEPTH, 8, 3 * D)
        dmod_cols = lax.dynamic_slice(dmod_all, (0, 0, k_me * ADA_COLS), (DEPTH, 8, ADA_COLS))
        return packed, _ada_bwd_adamw(c_all, dmod_cols, ada_w, m_ada_w, v_ada_w, "ada_bwd_adamw")

    (f_in, f_rows, f_conv), (o_in, o_rows, o_conv), c_idx, (packed, ada_out) = _reduce_scatter([g_in, g_rows, g_conv], "", small_work)
    g_ada, d_ada, nm_ada, nv_ada = ada_out
    gr_in, d_in, nm_in, nv_in = [tr_(t) for t in _adamw_layers(tr_(w_in), f_in, o_in, tr_(m_w_in), tr_(v_w_in), c_idx, "adamw_w_in")]
    cat = lambda a, b, c_: jnp.concatenate([a, b, c_], axis=1)
    gr_rows, d_rows, nm_rows, nv_rows = _adamw_layers(cat(w_proj_a, w_proj_b, w_out), f_rows, o_rows, cat(m_w_proj_a, m_w_proj_b, m_w_out),
                                                      cat(v_w_proj_a, v_w_proj_b, v_w_out), c_idx, "adamw_rows")
    gr_conv = jnp.where(ci == 0, jnp.stack([f_conv, o_conv]), jnp.stack([o_conv, f_conv]))
    split = lambda t: (t[:, 0:RA], t[:, RA:RA + RB], t[:, RA + RB:])

    g_small = packed[0:n_small]
    loss = jnp.sum(packed[n_small + n_dmod:])
    small_gw = jnp.concatenate([g_small, _pack([gr_conv])], axis=0)
    d_s, nm_s, nv_s = _adamw(_pack([w[n] for n, _ in _SMALL] + [conv_w]), small_gw,
                             _pack([m[n] for n, _ in _SMALL] + [m_conv_w]), _pack([v[n] for n, _ in _SMALL] + [v_conv_w]), "adamw_small")
    shapes = [s for _, s in _SMALL] + [conv_w.shape]
    names = [n for n, _ in _SMALL] + ["conv_w"]
    g_d = dict(zip(names, _unpack(small_gw, shapes)))
    d_d = dict(zip(names, _unpack(d_s, shapes)))
    nm_d = dict(zip(names, _unpack(nm_s, shapes)))
    nv_d = dict(zip(names, _unpack(nv_s, shapes)))

    def by_name(big, small):
        ga, gb, go = split(big[1])
        return dict(small, ada_w=big[2], w_in=big[0], w_proj_a=ga, w_proj_b=gb, w_out=go)

    order = ["ada_w", "ada_b", "norm_w", "w_in", "gm_ln_w", "gm_ln_b", "gm_ws", "gm_bs", "conv_w", "conv_b", "dt_bias", "a_log",
             "d_skip", "ssm_norm_w", "w_proj_a", "w_proj_b", "w_out", "final_norm_w"]
    outs = []
    for big, small in (((gr_in, gr_rows, g_ada), g_d), ((d_in, d_rows, d_ada), d_d), ((nm_in, nm_rows, nm_ada), nm_d),
                       ((nv_in, nv_rows, nv_ada), nv_d)):
        t = by_name(big, small)
        outs += [t[n] for n in order]
    return (loss, dx.reshape(1, -1, D), *outs)
```

```python
import jax
import jax.numpy as jnp
from jax import lax
from jax.experimental import pallas as pl
from jax.experimental.pallas import tpu as pltpu
from jax.experimental.pallas import tpu_sc as plsc

f32 = jnp.float32
bf16 = jnp.bfloat16

D = 1024
DEPTH = 2
EPS = 1e-6
CH = 128
NG = 8
HPG = 4
HD = 64
NH = NG * HPG
NST = 128
DIN = 2048
CONVD = 4096
GW = DIN // NG
PB = CONVD + DIN + 256
PA = 3 * D
PG = 2 * D
N_IN = 11296
NSHARD = 4
RA, RB = D // NSHARD, DIN // NSHARD
V7X_VMEM_BYTES = 64 * 2 ** 20
VMEM_LIMIT = V7X_VMEM_BYTES - 8 * 2 ** 20

ADAM_LR, ADAM_B1, ADAM_B2, ADAM_EPS, ADAM_WD, ADAM_STEP = 0.001, 0.9, 0.999, 1e-08, 0.01, 10

_HI = lax.Precision.HIGHEST


def _cp(sem):
    return pltpu.CompilerParams(dimension_semantics=sem, vmem_limit_bytes=VMEM_LIMIT)


def _sigmoid(x):
    return 0.5 * jnp.tanh(0.5 * x) + 0.5


def _silu_and_grad(x):
    s = _sigmoid(x)
    return x * s, s * (1.0 + x * (1.0 - s))


_GELU_K = 0.7978845608028654
_GELU_C = 0.044715


def _gelu_and_grad(x):
    x2 = x * x
    t = jnp.tanh(_GELU_K * (x + _GELU_C * x * x2))
    g = 0.5 * x * (1.0 + t)
    dg = 0.5 * (1.0 + t) + 0.5 * x * (1.0 - t * t) * _GELU_K * (1.0 + 3.0 * _GELU_C * x2)
    return g, dg


def _gelu(x):
    t = jnp.tanh(_GELU_K * (x + _GELU_C * x * x * x))
    return 0.5 * x * (1.0 + t)


def _softplus(x):
    return jnp.maximum(x, 0.0) + jnp.log(1.0 + jnp.exp(-jnp.abs(x)))


def _dot(a, b):
    return jnp.dot(a, b, preferred_element_type=f32)


def _dot_nt(a, b):
    return lax.dot_general(a, b, (((1,), (1,)), ((), ())), preferred_element_type=f32)


def _dot_tn(a, b):
    return lax.dot_general(a, b, (((0,), (0,)), ((), ())), preferred_element_type=f32)


def _dot_hi(a, b):
    return jnp.dot(a, b, precision=_HI, preferred_element_type=f32)


def _rmsmod_fwd(x, nw, scale, shift, name):
    s = x.shape[0]
    ts = min(512, s)

    def body(x_ref, nw_ref, sc_ref, sh_ref, h_ref):
        xv = x_ref[...]
        r = lax.rsqrt(jnp.mean(xv * xv, axis=-1, keepdims=True) + EPS)
        h_ref[...] = ((xv * r) * nw_ref[...] * (1.0 + sc_ref[...]) + sh_ref[...]).astype(bf16)

    row = pl.BlockSpec((1, D), lambda i: (0, 0))
    tile = pl.BlockSpec((ts, D), lambda i: (i, 0))
    return pl.pallas_call(
        body, grid=(s // ts,), in_specs=[tile, row, row, row], out_specs=tile,
        out_shape=jax.ShapeDtypeStruct((s, D), bf16), name=name, compiler_params=_cp(("parallel",)),
    )(x, nw, scale, shift)


def _rmsmod_bwd(dh, x, dres, nw, scale, name):
    s = x.shape[0]
    ts = min(512, s)

    def body(dh_ref, x_ref, dres_ref, nw_ref, sc_ref, dx_ref, dsc_ref, dsh_ref, dnw_ref):
        @pl.when(pl.program_id(0) == 0)
        def _():
            dsc_ref[...] = jnp.zeros_like(dsc_ref)
            dsh_ref[...] = jnp.zeros_like(dsh_ref)
            dnw_ref[...] = jnp.zeros_like(dnw_ref)
        xv = x_ref[...]
        dhv = dh_ref[...]
        r = lax.rsqrt(jnp.mean(xv * xv, axis=-1, keepdims=True) + EPS)
        xn = xv * r
        one_sc = 1.0 + sc_ref[...]
        dsc_ref[...] += jnp.sum(dhv * xn * nw_ref[...], axis=0, keepdims=True)
        dsh_ref[...] += jnp.sum(dhv, axis=0, keepdims=True)
        dnw_ref[...] += jnp.sum(dhv * xn * one_sc, axis=0, keepdims=True)
        dxn = dhv * (nw_ref[...] * one_sc)
        dx_ref[...] = r * (dxn - xn * jnp.mean(dxn * xn, axis=-1, keepdims=True)) + dres_ref[...]

    row = pl.BlockSpec((1, D), lambda i: (0, 0))
    tile = pl.BlockSpec((ts, D), lambda i: (i, 0))
    vec = jax.ShapeDtypeStruct((1, D), f32)
    return pl.pallas_call(
        body, grid=(s // ts,), in_specs=[tile, tile, tile, row, row], out_specs=[tile, row, row, row],
        out_shape=[jax.ShapeDtypeStruct((s, D), f32), vec, vec, vec], name=name, compiler_params=_cp(("arbitrary",)),
    )(dh, x, dres, nw, scale)


def _pick(n, prefs):
    for p in prefs:
        if n % p == 0:
            return p
    return n


def _mm(a, b, out_dtype, name, c_in=None, trans_b=False):
    m, k = a.shape
    n = b.shape[0] if trans_b else b.shape[1]
    tn = _pick(n, (1280, 1024, 512))
    tk = _pick(k, (1280, 1024, 512))
    nk = k // tk
    one_pass = nk == 1 and c_in is None
    tm = _pick(m, (2048, 1024, 512, 256) if one_pass else (1024, 512, 256))
    dot = _dot_nt if trans_b else _dot

    def body(*refs):
        if one_pass:
            a_ref, b_ref, o_ref = refs
            o_ref[...] = dot(a_ref[...], b_ref[...]).astype(out_dtype)
            return
        if c_in is not None:
            a_ref, b_ref, c_ref, o_ref, acc = refs
        else:
            a_ref, b_ref, o_ref, acc = refs
        kk = pl.program_id(2)

        @pl.when(kk == 0)
        def _():
            if c_in is not None:
                acc[...] = c_ref[...]
            else:
                acc[...] = jnp.zeros_like(acc)
        acc[...] += dot(a_ref[...], b_ref[...])

        @pl.when(kk == nk - 1)
        def _():
            o_ref[...] = acc[...].astype(out_dtype)

    b_spec = pl.BlockSpec((tn, tk), lambda j, i, kk: (j, kk)) if trans_b else pl.BlockSpec((tk, tn), lambda j, i, kk: (kk, j))
    in_specs = [pl.BlockSpec((tm, tk), lambda j, i, kk: (i, kk)), b_spec]
    args = [a, b]
    if c_in is not None:
        in_specs.append(pl.BlockSpec((tm, tn), lambda j, i, kk: (i, j)))
        args.append(c_in)
    return pl.pallas_call(
        body, grid=(n // tn, m // tm, nk), in_specs=in_specs, out_specs=pl.BlockSpec((tm, tn), lambda j, i, kk: (i, j)),
        out_shape=jax.ShapeDtypeStruct((m, n), out_dtype), scratch_shapes=[] if one_pass else [pltpu.VMEM((tm, tn), f32)],
        name=name, compiler_params=_cp(("parallel", "parallel", "arbitrary")),
    )(*args)


def _mm_tn(a, b, name):
    t, k1 = a.shape
    n = b.shape[1]
    t1 = _pick(k1, (1280, 1024, 512))
    tn = _pick(n, (1280, 1024, 512))
    tt = _pick(t, (2048, 1024, 512, 256))
    nt = t // tt

    def body(a_ref, b_ref, o_ref):
        tt_i = pl.program_id(2)

        @pl.when(tt_i == 0)
        def _():
            o_ref[...] = jnp.zeros_like(o_ref)
        o_ref[...] += _dot_tn(a_ref[...], b_ref[...])

    return pl.pallas_call(
        body, grid=(k1 // t1, n // tn, nt),
        in_specs=[pl.BlockSpec((tt, t1), lambda i, j, tt_i: (tt_i, i)), pl.BlockSpec((tt, tn), lambda i, j, tt_i: (tt_i, j))],
        out_specs=pl.BlockSpec((t1, tn), lambda i, j, tt_i: (i, j)),
        out_shape=jax.ShapeDtypeStruct((k1, n), f32), name=name,
        compiler_params=_cp(("parallel", "parallel", "arbitrary")),
    )(a, b)


def _ln_stats(v):
    mu = jnp.mean(v, axis=-1, keepdims=True)
    vc = v - mu
    rstd = lax.rsqrt(jnp.mean(vc * vc, axis=-1, keepdims=True) + EPS)
    return vc * rstd, rstd


def _mix(w_ref, vl):
    return jnp.concatenate([_dot(w_ref[g], vl[:, g * CH:(g + 1) * CH]) for g in range(NG)], axis=1)


def _branch_a_fwd(proj_a, lnw, lnb, wsm, bsf, name):
    s = proj_a.shape[0]
    ta = min(256, s)

    def body(pu_ref, pv_ref, pz_ref, lnw_ref, lnb_ref, w_ref, bs_ref, ya_ref):
        for c in range(ta // CH):
            rows = pl.ds(c * CH, CH)
            vh, _ = _ln_stats(_gelu(pv_ref[rows, :]))
            vl = (vh * lnw_ref[...] + lnb_ref[...]).astype(bf16)
            mixed = _mix(w_ref, vl) + bs_ref[...]
            pz = pz_ref[rows, :]
            ya_ref[rows, :] = (_gelu(pu_ref[rows, :]) * mixed * (pz * _sigmoid(pz))).astype(bf16)

    row = pl.BlockSpec((1, D), lambda i: (0, 0))
    return pl.pallas_call(
        body, grid=(s // ta,),
        in_specs=[pl.BlockSpec((ta, D), lambda i: (i, 0)), pl.BlockSpec((ta, D), lambda i: (i, 1)),
                  pl.BlockSpec((ta, D), lambda i: (i, 2)), row, row,
                  pl.BlockSpec((NG, CH, CH), lambda i: (0, 0, 0)), pl.BlockSpec((CH, D), lambda i: (0, 0))],
        out_specs=pl.BlockSpec((ta, D), lambda i: (i, 0)),
        out_shape=jax.ShapeDtypeStruct((s, D), bf16), name=name, compiler_params=_cp(("parallel",)),
    )(proj_a, proj_a, proj_a, lnw, lnb, wsm, bsf)


def _branch_a_bwd(proj_a, dya, lnw, lnb, wsm, wsm_t, bsf, name):
    s = proj_a.shape[0]
    ta = min(256, s)

    def body(pu_ref, pv_ref, pz_ref, dya_ref, lnw_ref, lnb_ref, w_ref, wt_ref, bs_ref,
             dp_ref, dws_ref, dbs_ref, dlnw_ref, dlnb_ref):
        @pl.when(pl.program_id(0) == 0)
        def _():
            dws_ref[...] = jnp.zeros_like(dws_ref)
            dbs_ref[...] = jnp.zeros_like(dbs_ref)
            dlnw_ref[...] = jnp.zeros_like(dlnw_ref)
            dlnb_ref[...] = jnp.zeros_like(dlnb_ref)
        for c in range(ta // CH):
            rows = pl.ds(c * CH, CH)
            u, du = _gelu_and_grad(pu_ref[rows, :])
            v, dv_act = _gelu_and_grad(pv_ref[rows, :])
            zg, dzg = _silu_and_grad(pz_ref[rows, :])
            vh, rstd = _ln_stats(v)
            vl = (vh * lnw_ref[...] + lnb_ref[...]).astype(bf16)
            mixed = _mix(w_ref, vl) + bs_ref[...]
            dy = dya_ref[rows, :].astype(f32)
            dmixed = dy * u * zg
            dp_ref[rows, 0:D] = (dy * mixed * zg * du).astype(bf16)
            dp_ref[rows, 2 * D:3 * D] = (dy * u * mixed * dzg).astype(bf16)
            dmb = dmixed.astype(bf16)
            dbs_ref[...] += dmixed
            dvl = _mix(wt_ref, dmb)
            for g in range(NG):
                cols = slice(g * CH, (g + 1) * CH)
                dws_ref[g] += _dot_nt(dmb[:, cols], vl[:, cols])
            dlnw_ref[...] += jnp.sum(dvl * vh, axis=0, keepdims=True)
            dlnb_ref[...] += jnp.sum(dvl, axis=0, keepdims=True)
            dvh = dvl * lnw_ref[...]
            dv = rstd * (dvh - jnp.mean(dvh, axis=-1, keepdims=True) - vh * jnp.mean(dvh * vh, axis=-1, keepdims=True))
            dp_ref[rows, D:2 * D] = (dv * dv_act).astype(bf16)

    row = pl.BlockSpec((1, D), lambda i: (0, 0))
    wspec = pl.BlockSpec((NG, CH, CH), lambda i: (0, 0, 0))
    full = pl.BlockSpec((CH, D), lambda i: (0, 0))
    return pl.pallas_call(
        body, grid=(s // ta,),
        in_specs=[pl.BlockSpec((ta, D), lambda i: (i, 0)), pl.BlockSpec((ta, D), lambda i: (i, 1)),
                  pl.BlockSpec((ta, D), lambda i: (i, 2)), pl.BlockSpec((ta, D), lambda i: (i, 0)),
                  row, row, wspec, wspec, full],
        out_specs=[pl.BlockSpec((ta, PA), lambda i: (i, 0)), wspec, full, row, row],
        out_shape=[jax.ShapeDtypeStruct((s, PA), bf16), jax.ShapeDtypeStruct((NG, CH, CH), f32),
                   jax.ShapeDtypeStruct((CH, D), f32), jax.ShapeDtypeStruct((1, D), f32), jax.ShapeDtypeStruct((1, D), f32)],
        name=name, compiler_params=_cp(("arbitrary",)),
    )(proj_a, proj_a, proj_a, dya, lnw, lnb, wsm, wsm_t, bsf)


GB = GW + 2 * NST


def _group_major(xs, b, c):
    lead = xs.shape[:-1]
    return jnp.concatenate([xs.reshape(lead + (NG, GW)), b.reshape(lead + (NG, NST)), c.reshape(lead + (NG, NST))],
                           axis=-1).reshape(lead + (CONVD,))


def _from_group_major(t):
    lead = t.shape[:-1]
    t = t.reshape(lead + (NG, GB))
    return jnp.concatenate([t[..., 0:GW].reshape(lead + (DIN,)), t[..., GW:GW + NST].reshape(lead + (NG * NST,)),
                            t[..., GW + NST:GB].reshape(lead + (NG * NST,))], axis=-1)


def _rows_from_group_major(t):
    t = t.reshape(NG, GB, t.shape[-1])
    return jnp.concatenate([t[:, 0:GW].reshape(DIN, -1), t[:, GW:GW + NST].reshape(NG * NST, -1),
                            t[:, GW + NST:GB].reshape(NG * NST, -1)], axis=0)


def _shift_rows(x, prev8, j):
    xr = pltpu.roll(x, j, 0)
    fix = pltpu.roll(prev8, j, 0)
    rid = lax.broadcasted_iota(jnp.int32, (8, x.shape[1]), 0)
    top = jnp.where(rid < j, fix, xr[0:8])
    return jnp.concatenate([top, xr[8:]], axis=0)


def _shift_rows_up(d, next8, j):
    dr = pltpu.roll(d, CH - j, 0)
    fix = pltpu.roll(next8, 8 - j, 0)
    rid = lax.broadcasted_iota(jnp.int32, (8, d.shape[1]), 0)
    bot = jnp.where(rid >= 8 - j, fix, dr[CH - 8:CH])
    return jnp.concatenate([dr[0:CH - 8], bot], axis=0)


def _conv_pre(x, prev8, cw_ref, cb_ref, cols):
    shifted = [_shift_rows(x, prev8, j) for j in (1, 2, 3)]
    conv = cb_ref[:, cols] + cw_ref[3:4, cols] * x
    for j in (1, 2, 3):
        conv = conv + cw_ref[3 - j:4 - j, cols] * shifted[j - 1]
    return conv, shifted


def _tril_mask():
    return lax.broadcasted_iota(jnp.int32, (CH, CH), 0) >= lax.broadcasted_iota(jnp.int32, (CH, CH), 1)


def _sum_all(v):
    return jnp.sum(jnp.sum(v, axis=0, keepdims=True), axis=1, keepdims=True)


def _lanes(g, width, base=0):
    return pl.ds(pl.multiple_of(base + g * width, width), width)


def _branch_b_fwd(proj_b, cw, cb, dtb, a_row, dkc, snw, name):
    s = proj_b.shape[0]
    nc = s // CH

    def body(xbc_ref, sz_ref, dtr_ref, cw_ref, cb_ref, dtb_ref, a_ref, dkc_ref, snw_ref,
             yb_ref, y_ref, st_ref, cv_ref, prev8, state, acst_s):
        @pl.when(pl.program_id(0) == 0)
        def _():
            prev8[...] = jnp.zeros_like(prev8)
            state[...] = jnp.zeros_like(state)
        st_ref[0] = state[...].astype(bf16)
        mask = _tril_mask()
        dt_all = _softplus(dtr_ref[:, 0:CH] + dtb_ref[...])
        acs_all = _dot_hi(mask.astype(f32), dt_all * a_ref[...])
        acst_s[...] = acs_all.T

        def group(g, carry):
            cols = _lanes(g, GB)
            gcols = _lanes(g, GW)
            x = xbc_ref[:, cols]
            conv, _ = _conv_pre(x, prev8[:, cols], cw_ref, cb_ref, cols)
            prev8[:, cols] = x[CH - 8:CH]
            cv_ref[:, cols] = conv.astype(bf16)
            xc = conv * _sigmoid(conv)
            xs = xc[:, 0:GW]
            bg = xc[:, GW:GW + NST].astype(bf16)
            cg = xc[:, GW + NST:GB].astype(bf16)
            back = lax.rem(CH - HPG * g, CH)
            dt = pltpu.roll(dt_all, back, 1)
            acs = pltpu.roll(acs_all, back, 1)
            cbm = _dot_nt(cg, bg)
            dkc_g = dkc_ref[:, gcols]
            y_parts = []
            for r in range(HPG):
                colb = jnp.broadcast_to(acs[:, r:r + 1], (CH, CH))
                row = acst_s[pl.ds(g * HPG + r, 1), :]
                lmat = jnp.exp(jnp.where(mask, colb - row, -jnp.inf))
                xr = xs[:, r * HD:(r + 1) * HD]
                xd = xr * dt[:, r:r + 1]
                sp = state[g * HPG + r]
                col = colb[:, 0:HD]
                alast = colb[CH - 1:CH, 0:HD]
                y_r = _dot((cbm * lmat).astype(bf16), xd.astype(bf16))
                y_r = y_r + jnp.exp(col) * _dot_nt(cg, sp.astype(bf16))
                y_parts.append(y_r + xr * dkc_g[:, r * HD:(r + 1) * HD])
                cs = _dot_tn((xd * jnp.exp(alast - col)).astype(bf16), bg)
                state[g * HPG + r] = jnp.exp(colb[CH - 1:CH, :]) * sp + cs
            y = jnp.concatenate(y_parts, axis=1)
            szv = sz_ref[:, gcols]
            yz = y * (szv * _sigmoid(szv))
            rr = lax.rsqrt(jnp.mean(yz * yz, axis=-1, keepdims=True) + EPS)
            yb_ref[:, gcols] = (yz * rr * snw_ref[:, gcols]).astype(bf16)
            y_ref[:, gcols] = y.astype(bf16)
            return carry

        lax.fori_loop(0, NG, group, 0)

    const2 = lambda c: (0, 0)
    return pl.pallas_call(
        body, grid=(nc,),
        in_specs=[pl.BlockSpec((CH, CONVD), lambda c: (c, 0)), pl.BlockSpec((CH, DIN), lambda c: (c, CONVD // DIN)),
                  pl.BlockSpec((CH, 256), lambda c: (c, (CONVD + DIN) // 256)),
                  pl.BlockSpec((4, CONVD), const2), pl.BlockSpec((1, CONVD), const2),
                  pl.BlockSpec((1, CH), const2), pl.BlockSpec((1, CH), const2),
                  pl.BlockSpec((1, DIN), const2), pl.BlockSpec((1, DIN), const2)],
        out_specs=[pl.BlockSpec((CH, DIN), lambda c: (c, 0)), pl.BlockSpec((CH, DIN), lambda c: (c, 0)),
                   pl.BlockSpec((1, NH, HD, NST), lambda c: (c, 0, 0, 0)), pl.BlockSpec((CH, CONVD), lambda c: (c, 0))],
        out_shape=[jax.ShapeDtypeStruct((s, DIN), bf16), jax.ShapeDtypeStruct((s, DIN), bf16),
                   jax.ShapeDtypeStruct((nc, NH, HD, NST), bf16), jax.ShapeDtypeStruct((s, CONVD), bf16)],
        scratch_shapes=[pltpu.VMEM((8, CONVD), f32), pltpu.VMEM((NH, HD, NST), f32), pltpu.VMEM((CH, CH), f32)],
        name=name, compiler_params=_cp(("arbitrary",)),
    )(proj_b, proj_b, proj_b, cw, cb, dtb, a_row, dkc, snw)


def _branch_b_bwd(proj_b, conv_sv, dyb, y_sv, states, cw, dtb, a_row, dkc, snw, ind, name):
    s = proj_b.shape[0]
    nc = s // CH

    def body(xbc_ref, cv_ref, sz_ref, dtr_ref, dyb_ref, y_ref, st_ref, cw_ref, dtb_ref, a_ref, dkc_ref,
             snw_ref, ind_ref, dp_ref, dcw_ref, dcb_ref, ddtb_ref, dal_ref, ddk_ref, dsnw_ref,
             dstate, dnext8, acst_s, dacs_acc, q2_acc):
        @pl.when(pl.program_id(0) == 0)
        def _():
            dstate[...] = jnp.zeros_like(dstate)
            dnext8[...] = jnp.zeros_like(dnext8)
            dcw_ref[...] = jnp.zeros_like(dcw_ref)
            dcb_ref[...] = jnp.zeros_like(dcb_ref)
            ddtb_ref[...] = jnp.zeros_like(ddtb_ref)
            dal_ref[...] = jnp.zeros_like(dal_ref)
            ddk_ref[...] = jnp.zeros_like(ddk_ref)
            dsnw_ref[...] = jnp.zeros_like(dsnw_ref)

        dacs_acc[...] = jnp.zeros_like(dacs_acc)
        q2_acc[...] = jnp.zeros_like(q2_acc)
        mask = _tril_mask()
        tri_t = (lax.broadcasted_iota(jnp.int32, (CH, CH), 0) <= lax.broadcasted_iota(jnp.int32, (CH, CH), 1)).astype(f32)
        lane1 = lax.broadcasted_iota(jnp.int32, (1, CH), 1)
        is_last = lax.broadcasted_iota(jnp.int32, (CH, 1), 0) == CH - 1
        z_all = dtr_ref[:, 0:CH] + dtb_ref[...]
        dt_all = _softplus(z_all)
        adt_all = dt_all * a_ref[...]
        acs_all = _dot_hi(mask.astype(f32), adt_all)
        acst_s[...] = acs_all.T

        def ind_sum(v):
            hi = v.astype(bf16)
            lo = (v - hi.astype(f32)).astype(bf16)
            return _dot(hi, ind_ref[...]) + _dot(lo, ind_ref[...])

        def group(g, carry):
            cols = _lanes(g, GB)
            gcols = _lanes(g, GW)
            conv = cv_ref[:, cols].astype(f32)
            sg = _sigmoid(conv)
            xc = conv * sg
            xs = xc[:, 0:GW]
            bg = xc[:, GW:GW + NST].astype(bf16)
            cg = xc[:, GW + NST:GB].astype(bf16)

            y = y_ref[:, gcols].astype(f32)
            silu_sz, dsilu_sz = _silu_and_grad(sz_ref[:, gcols])
            yz = y * silu_sz
            rr = lax.rsqrt(jnp.mean(yz * yz, axis=-1, keepdims=True) + EPS)
            dyb_g = dyb_ref[:, gcols].astype(f32)
            w = dyb_g * snw_ref[:, gcols]
            dsnw_ref[:, gcols] += jnp.sum(dyb_g * yz * rr, axis=0, keepdims=True)
            dyz = rr * w - yz * (rr * rr * rr) * jnp.mean(w * yz, axis=-1, keepdims=True)
            dp_ref[:, _lanes(g, GW, CONVD)] = (dyz * y * dsilu_sz).astype(bf16)
            dy_g = dyz * silu_sz
            ddk_ref[:, gcols] += jnp.sum(dy_g * xs, axis=0, keepdims=True)

            back = lax.rem(CH - HPG * g, CH)
            dt = pltpu.roll(dt_all, back, 1)
            acs = pltpu.roll(acs_all, back, 1)
            cbm = _dot_nt(cg, bg)
            d_cb = jnp.zeros((CH, CH), f32)
            d_bg = jnp.zeros((CH, NST), f32)
            d_cg = jnp.zeros((CH, NST), f32)
            lastrow = jnp.zeros((1, CH), f32)
            dxd_parts, dxs_parts, t_parts = [], [], []
            for r in range(HPG):
                h = g * HPG + r
                colb = jnp.broadcast_to(acs[:, r:r + 1], (CH, CH))
                row = acst_s[pl.ds(h, 1), :]
                lmat = jnp.exp(jnp.where(mask, colb - row, -jnp.inf))
                mmat_b = (cbm * lmat).astype(bf16)
                dtc = dt[:, r:r + 1]
                xd = xs[:, r * HD:(r + 1) * HD] * dtc
                col = colb[:, 0:HD]
                alast = colb[CH - 1:CH, 0:HD]
                dte = jnp.exp(alast - col)
                ea = jnp.exp(col)
                cd = jnp.exp(colb[CH - 1:CH, :])
                sp = st_ref[0, h]
                dsn = dstate[h]
                dsn_b = dsn.astype(bf16)
                dyr = dy_g[:, r * HD:(r + 1) * HD]
                dyr_b = dyr.astype(bf16)
                dye_b = (dyr * ea).astype(bf16)
                d_cg = d_cg + _dot(dye_b, sp)
                dxde = _dot_nt(bg, dsn_b)
                xdte = xd * dte
                xd_b = xd.astype(bf16)
                d_bg = d_bg + _dot(xdte.astype(bf16), dsn_b)
                dxd_diag = _dot_tn(mmat_b, dyr_b)
                dxd = dxde * dte + dxd_diag
                d_cb = d_cb + _dot_nt(dyr_b, xd_b) * lmat
                t_parts.append(dyr_b.astype(f32) * _dot(mmat_b, xd_b) + dyr * (ea * _dot_nt(cg, sp))
                               - xd_b.astype(f32) * dxd_diag - dxde * xdte)
                lastrow = jnp.where(lane1 == r, _sum_all(dsn * sp.astype(f32)) * cd + _sum_all(dxde * xdte), lastrow)
                dstate[h] = cd * dsn + _dot_tn(dye_b, cg)
                dxd_parts.append(dxd)
                dxs_parts.append(dxd * dtc)
            d_cb_b = d_cb.astype(bf16)
            d_bg = d_bg + _dot_tn(d_cb_b, cg)
            d_cg = d_cg + _dot(d_cb_b, bg)
            q2 = ind_sum(jnp.concatenate(dxd_parts, axis=1) * xs)
            dacs = ind_sum(jnp.concatenate(t_parts, axis=1)) + jnp.where(is_last, lastrow, 0.0)
            dacs_acc[...] += pltpu.roll(dacs, HPG * g, 1)
            q2_acc[...] += pltpu.roll(q2, HPG * g, 1)
            dxs = jnp.concatenate(dxs_parts, axis=1) + dy_g * dkc_ref[:, gcols]

            dconv = jnp.concatenate([dxs, d_bg, d_cg], axis=1) * (sg * (1.0 + conv * (1.0 - sg)))
            x = xbc_ref[:, cols]
            dcb_ref[:, cols] += jnp.sum(dconv, axis=0, keepdims=True)
            dcw_ref[3:4, cols] += jnp.sum(dconv * x, axis=0, keepdims=True)
            dx = cw_ref[3:4, cols] * dconv
            nxt = dnext8[:, cols]
            for j in (1, 2, 3):
                up = _shift_rows_up(dconv, nxt, j)
                dcw_ref[3 - j:4 - j, cols] += jnp.sum(up * x, axis=0, keepdims=True)
                dx = dx + cw_ref[3 - j:4 - j, cols] * up
            dnext8[:, cols] = dconv[0:8]
            dp_ref[:, cols] = dx.astype(bf16)
            return carry

        lax.fori_loop(0, NG, group, 0)

        dadt = _dot_hi(tri_t, dacs_acc[...])
        dal_ref[...] += jnp.sum(dadt * adt_all, axis=0, keepdims=True)
        ddz = (dadt * a_ref[...] + q2_acc[...]) * _sigmoid(z_all)
        ddtb_ref[...] += jnp.sum(ddz, axis=0, keepdims=True)
        dp_ref[:, CONVD + DIN:CONVD + DIN + CH] = ddz.astype(bf16)
        dp_ref[:, CONVD + DIN + CH:PB] = jnp.zeros((CH, PB - CONVD - DIN - CH), bf16)

    const2 = lambda c: (0, 0)
    rev = lambda c: (nc - 1 - c, 0)
    return pl.pallas_call(
        body, grid=(nc,),
        in_specs=[pl.BlockSpec((CH, CONVD), rev), pl.BlockSpec((CH, CONVD), rev),
                  pl.BlockSpec((CH, DIN), lambda c: (nc - 1 - c, CONVD // DIN)),
                  pl.BlockSpec((CH, 256), lambda c: (nc - 1 - c, (CONVD + DIN) // 256)),
                  pl.BlockSpec((CH, DIN), rev), pl.BlockSpec((CH, DIN), rev),
                  pl.BlockSpec((1, NH, HD, NST), lambda c: (nc - 1 - c, 0, 0, 0)),
                  pl.BlockSpec((4, CONVD), const2), pl.BlockSpec((1, CH), const2), pl.BlockSpec((1, CH), const2),
                  pl.BlockSpec((1, DIN), const2), pl.BlockSpec((1, DIN), const2), pl.BlockSpec((GW, CH), const2)],
        out_specs=[pl.BlockSpec((CH, PB), rev), pl.BlockSpec((4, CONVD), const2), pl.BlockSpec((1, CONVD), const2),
                   pl.BlockSpec((1, CH), const2), pl.BlockSpec((1, CH), const2), pl.BlockSpec((1, DIN), const2),
                   pl.BlockSpec((1, DIN), const2)],
        out_shape=[jax.ShapeDtypeStruct((s, PB), bf16), jax.ShapeDtypeStruct((4, CONVD), f32),
                   jax.ShapeDtypeStruct((1, CONVD), f32), jax.ShapeDtypeStruct((1, CH), f32),
                   jax.ShapeDtypeStruct((1, CH), f32), jax.ShapeDtypeStruct((1, DIN), f32),
                   jax.ShapeDtypeStruct((1, DIN), f32)],
        scratch_shapes=[pltpu.VMEM((NH, HD, NST), f32), pltpu.VMEM((8, CONVD), f32), pltpu.VMEM((CH, CH), f32),
                        pltpu.VMEM((CH, CH), f32), pltpu.VMEM((CH, CH), f32)],
        name=name, compiler_params=_cp(("arbitrary",)),
    )(proj_b, conv_sv, proj_b, proj_b, dyb, y_sv, states, cw, dtb, a_row, dkc, snw, ind)


def _merge_fwd(ya, yb, proj_g, x, gate, wa, wb, wo, name):
    s = x.shape[0]
    ts = min(512, s)

    def body(ya_ref, yb_ref, ga_ref, gb_ref, x_ref, gate_ref, wa_ref, wb_ref, wo_ref, xo_ref, pa_ref, pb_ref, mg_ref, o_ref):
        pa = _dot(ya_ref[...], wa_ref[...])
        pb = _dot(yb_ref[...], wb_ref[...])
        mg = (_sigmoid(ga_ref[...]) * pa + _sigmoid(gb_ref[...]) * pb).astype(bf16)
        o = _dot(mg, wo_ref[...])
        xo_ref[...] = x_ref[...] + gate_ref[...] * o
        pa_ref[...] = pa.astype(bf16)
        pb_ref[...] = pb.astype(bf16)
        mg_ref[...] = mg
        o_ref[...] = o.astype(bf16)

    tile = pl.BlockSpec((ts, D), lambda i: (i, 0))
    const = lambda i: (0, 0)
    act = jax.ShapeDtypeStruct((s, D), bf16)
    return pl.pallas_call(
        body, grid=(s // ts,),
        in_specs=[tile, pl.BlockSpec((ts, DIN), lambda i: (i, 0)), tile, pl.BlockSpec((ts, D), lambda i: (i, 1)), tile,
                  pl.BlockSpec((1, D), const), pl.BlockSpec((D, D), const), pl.BlockSpec((DIN, D), const),
                  pl.BlockSpec((D, D), const)],
        out_specs=[tile, tile, tile, tile, tile],
        out_shape=[jax.ShapeDtypeStruct((s, D), f32), act, act, act, act],
        name=name, compiler_params=_cp(("parallel",)),
    )(ya, yb, proj_g, proj_g, x, gate, wa, wb, wo)


def _merge_bwd(dxo, gate, o_sv, pa_sv, pb_sv, proj_g, wo, wa, wb, name):
    s = dxo.shape[0]
    ts = min(512, s)

    def body(dxo_ref, gate_ref, o_ref, pa_ref, pb_ref, ga_ref, gb_ref, wo_ref, wa_ref, wb_ref,
             do_ref, dpa_ref, dpb_ref, dg_ref, dya_ref, dyb_ref, dgate_ref):
        @pl.when(pl.program_id(0) == 0)
        def _():
            dgate_ref[...] = jnp.zeros_like(dgate_ref)
        dxo_v = dxo_ref[...]
        dgate_ref[...] += jnp.sum(dxo_v * o_ref[...].astype(f32), axis=0, keepdims=True)
        do = (dxo_v * gate_ref[...]).astype(bf16)
        do_ref[...] = do
        dmg = _dot_nt(do, wo_ref[...])
        sa = _sigmoid(ga_ref[...])
        sb = _sigmoid(gb_ref[...])
        dpa = (dmg * sa).astype(bf16)
        dpb = (dmg * sb).astype(bf16)
        dpa_ref[...] = dpa
        dpb_ref[...] = dpb
        dg_ref[:, 0:D] = (dmg * pa_ref[...].astype(f32) * sa * (1.0 - sa)).astype(bf16)
        dg_ref[:, D:2 * D] = (dmg * pb_ref[...].astype(f32) * sb * (1.0 - sb)).astype(bf16)
        dya_ref[...] = _dot_nt(dpa, wa_ref[...]).astype(bf16)
        dyb_ref[...] = _dot_nt(dpb, wb_ref[...]).astype(bf16)

    tile = pl.BlockSpec((ts, D), lambda i: (i, 0))
    const = lambda i: (0, 0)
    act = jax.ShapeDtypeStruct((s, D), bf16)
    return pl.pallas_call(
        body, grid=(s // ts,),
        in_specs=[tile, pl.BlockSpec((1, D), const), tile, tile, tile, tile, pl.BlockSpec((ts, D), lambda i: (i, 1)),
                  pl.BlockSpec((D, D), const), pl.BlockSpec((D, D), const), pl.BlockSpec((DIN, D), const)],
        out_specs=[tile, tile, tile, pl.BlockSpec((ts, PG), lambda i: (i, 0)), tile, pl.BlockSpec((ts, DIN), lambda i: (i, 0)),
                   pl.BlockSpec((1, D), const)],
        out_shape=[act, act, act, jax.ShapeDtypeStruct((s, PG), bf16), act, jax.ShapeDtypeStruct((s, DIN), bf16),
                   jax.ShapeDtypeStruct((1, D), f32)],
        name=name, compiler_params=_cp(("arbitrary",)),
    )(dxo, gate, o_sv, pa_sv, pb_sv, proj_g, proj_g, wo, wa, wb)


def _final_loss(x, target, fnw, name):
    s = x.shape[0]
    ts = min(512, s)

    def body(x_ref, t_ref, w_ref, loss_ref, dx_ref, dw_ref):
        @pl.when(pl.program_id(0) == 0)
        def _():
            loss_ref[...] = jnp.zeros_like(loss_ref)
            dw_ref[...] = jnp.zeros_like(dw_ref)
        xv = x_ref[...]
        r = lax.rsqrt(jnp.mean(xv * xv, axis=-1, keepdims=True) + EPS)
        xn = xv * r
        err = xn * w_ref[...] - t_ref[...]
        part = jnp.sum(err * err, axis=0, keepdims=True)
        acc = part[:, 0:128]
        for k in range(1, D // 128):
            acc = acc + part[:, k * 128:(k + 1) * 128]
        loss_ref[0:1, :] += acc * (0.5 / D)
        dy = err * (1.0 / D)
        dw_ref[...] += jnp.sum(dy * xn, axis=0, keepdims=True)
        dxn = dy * w_ref[...]
        dx_ref[...] = r * (dxn - xn * jnp.mean(dxn * xn, axis=-1, keepdims=True))

    tile = pl.BlockSpec((ts, D), lambda i: (i, 0))
    row = pl.BlockSpec((1, D), lambda i: (0, 0))
    return pl.pallas_call(
        body, grid=(s // ts,), in_specs=[tile, tile, row],
        out_specs=[pl.BlockSpec((8, 128), lambda i: (0, 0)), tile, row],
        out_shape=[jax.ShapeDtypeStruct((8, 128), f32), jax.ShapeDtypeStruct((s, D), f32), jax.ShapeDtypeStruct((1, D), f32)],
        name=name, compiler_params=_cp(("arbitrary",)),
    )(x, target, fnw)


def _layer_operands(w_in, conv_w, wa, wb, wo, norm_w, gm_ln_w, gm_ln_b, gm_ws, gm_bs, conv_b, dt_bias, a_log, d_skip, ssm_norm_w):
    sz0, x0 = PA, PA + DIN
    b0, c0, dt0 = x0 + DIN, x0 + DIN + NG * NST, x0 + CONVD
    w_xbc = _group_major(w_in[:, x0:b0], w_in[:, b0:c0], w_in[:, c0:dt0])
    w_b = jnp.concatenate([w_xbc, w_in[:, sz0:x0], w_in[:, dt0:dt0 + NH], jnp.zeros((D, PB - CONVD - DIN - NH), bf16)], axis=1)
    w_a = w_in[:, 0:PA]
    w_g = w_in[:, dt0 + NH:N_IN]
    tril = jnp.tril(jnp.ones((CH, CH), bool))
    wsm = jnp.where(tril[None], gm_ws, 0.0).astype(bf16)

    def heads_row(v):
        return jnp.pad(v, (0, CH - NH)).reshape(1, CH)

    return dict(
        w_b=w_b, w_a=w_a, w_g=w_g, wa=wa, wb=wb, wo=wo,
        norm_w=norm_w.reshape(1, D), lnw=gm_ln_w.reshape(1, D), lnb=gm_ln_b.reshape(1, D),
        wsm=wsm, wsm_t=jnp.swapaxes(wsm, 1, 2), bsf=jnp.repeat(gm_bs.T, CH, axis=1),
        cw=_group_major(conv_w[:, 0:DIN], conv_w[:, DIN:DIN + NG * NST], conv_w[:, DIN + NG * NST:CONVD]),
        cb=_group_major(conv_b[0:DIN], conv_b[DIN:DIN + NG * NST], conv_b[DIN + NG * NST:CONVD]).reshape(1, CONVD),
        dtb=heads_row(dt_bias), a_row=heads_row(-jnp.exp(a_log)),
        snw=ssm_norm_w.reshape(1, DIN), dkc=jnp.repeat(d_skip, HD).reshape(1, DIN),
        ind=(jnp.arange(GW)[:, None] // HD == jnp.arange(CH)[None, :]).astype(bf16),
    )


def _layer_fwd(x, shift, scale, gate, p, tag):
    h = _rmsmod_fwd(x, p["norm_w"], scale, shift, f"rmsmod_fwd{tag}")
    proj_b = _mm(h, p["w_b"], f32, f"proj_b{tag}")
    proj_a = _mm(h, p["w_a"], f32, f"proj_a{tag}")
    proj_g = _mm(h, p["w_g"], f32, f"proj_g{tag}")
    ya = _branch_a_fwd(proj_a, p["lnw"], p["lnb"], p["wsm"], p["bsf"], f"branch_a_fwd{tag}")
    yb, y_sv, states, conv_sv = _branch_b_fwd(proj_b, p["cw"], p["cb"], p["dtb"], p["a_row"], p["dkc"], p["snw"], f"branch_b_fwd{tag}")
    if "merge_weights" in p:
        p["wa"], p["wb"], p["wo"], yb = p.pop("merge_weights")(yb)
    x_out, pa, pb, mg, o = _merge_fwd(ya, yb, proj_g, x, gate, p["wa"], p["wb"], p["wo"], f"merge_fwd{tag}")
    saved = dict(x=x, h=h, proj_b=proj_b, proj_a=proj_a, proj_g=proj_g, ya=ya, yb=yb, y=y_sv, states=states, conv=conv_sv,
                 pa=pa, pb=pb, mg=mg, o=o, scale=scale, gate=gate)
    return x_out, saved


def _layer_bwd(dxo, sv, p, tag):
    do, dpa, dpb, dg, dya, dyb, dgate = _merge_bwd(dxo, sv["gate"], sv["o"], sv["pa"], sv["pb"], sv["proj_g"],
                                                   p["wo"], p["wa"], p["wb"], f"merge_bwd{tag}")
    d_wo = _mm_tn(sv["mg"], do, f"d_wo{tag}")
    d_wa = _mm_tn(sv["ya"], dpa, f"d_wa{tag}")
    d_wb = _mm_tn(sv["yb"], dpb, f"d_wb{tag}")
    da, dws, dbs, dlnw, dlnb = _branch_a_bwd(sv["proj_a"], dya, p["lnw"], p["lnb"], p["wsm"], p["wsm_t"], p["bsf"],
                                             f"branch_a_bwd{tag}")
    db, dcw, dcb, ddtb, dal, ddk, dsnw = _branch_b_bwd(sv["proj_b"], sv["conv"], dyb, sv["y"], sv["states"], p["cw"], p["dtb"],
                                                       p["a_row"], p["dkc"], p["snw"], p["ind"], f"branch_b_bwd{tag}")
    dh = _mm(db, p["w_b"], f32, f"dh_b{tag}", trans_b=True)
    dh = _mm(da, p["w_a"], f32, f"dh_a{tag}", c_in=dh, trans_b=True)
    dh = _mm(dg, p["w_g"], f32, f"dh_g{tag}", c_in=dh, trans_b=True)
    d_w_b_t = _mm_tn(db, sv["h"], f"d_w_b{tag}")
    d_w_a_t = _mm_tn(da, sv["h"], f"d_w_a{tag}")
    d_w_g_t = _mm_tn(dg, sv["h"], f"d_w_g{tag}")
    dx, dscale, dshift, dnw = _rmsmod_bwd(dh, sv["x"], dxo, p["norm_w"], sv["scale"], f"rmsmod_bwd{tag}")
    d_w_in_t = jnp.concatenate([d_w_a_t, d_w_b_t[CONVD:CONVD + DIN], _rows_from_group_major(d_w_b_t[0:CONVD]),
                                d_w_b_t[CONVD + DIN:CONVD + DIN + NH], d_w_g_t], axis=0)
    tril = jnp.tril(jnp.ones((CH, CH), bool))
    heads = lambda v: v[0, 0:NH]
    grads = dict(
        w_in_t=d_w_in_t, w_proj_a=d_wa, w_proj_b=d_wb, w_out=d_wo, conv_w=_from_group_major(dcw), conv_b=_from_group_major(dcb).reshape(CONVD),
        norm_w=dnw.reshape(D), gm_ln_w=dlnw.reshape(D), gm_ln_b=dlnb.reshape(D),
        gm_ws=jnp.where(tril[None], dws, 0.0), gm_bs=dbs.reshape(CH, NG, CH).sum(-1).T,
        dt_bias=heads(ddtb), a_log=heads(dal), d_skip=ddk.reshape(NH, HD).sum(-1), ssm_norm_w=dsnw.reshape(DIN),
        mod=jnp.concatenate([dshift, dscale, dgate], axis=1).reshape(3 * D),
    )
    return dx, grads


def _local_step(x, target, mods, operands_of, fnw):
    saved, layer_ops = [], []
    for l in range(DEPTH):
        shift, scale, gate = mods[l]
        p, x = operands_of(l, x)
        layer_ops.append(p)
        x, sv = _layer_fwd(x, shift, scale, gate, p, f"_l{l}")
        saved.append(sv)
    loss_parts, dx, dfnw = _final_loss(x, target, fnw.reshape(1, D), "final_loss")
    grads = [None] * DEPTH
    for l in reversed(range(DEPTH)):
        dx, grads[l] = _layer_bwd(dx, saved[l], layer_ops[l], f"_l{l}")
    return loss_parts, dx, grads, dfnw.reshape(D)


ADA_COLS = 3 * D // NSHARD


def _ada_fwd(c_all, ada_w, ada_b_cols, name):
    def body(c_ref, w_ref, b_ref, o_ref):
        cv = c_ref[...]
        sc = cv * _sigmoid(cv)
        for l in range(DEPTH):
            o_ref[l] = _dot_hi(sc, w_ref[l]) + b_ref[l]

    return pl.pallas_call(body, out_shape=jax.ShapeDtypeStruct((DEPTH, 8, ADA_COLS), f32), name=name,
                          compiler_params=_cp(None))(c_all, ada_w, ada_b_cols)


def _adam_math(w, g, m, v):
    m = ADAM_B1 * m + (1.0 - ADAM_B1) * g
    v = ADAM_B2 * v + (1.0 - ADAM_B2) * (g * g)
    m_hat = m / (1.0 - ADAM_B1 ** ADAM_STEP)
    v_hat = v / (1.0 - ADAM_B2 ** ADAM_STEP)
    delta = -ADAM_LR * (m_hat / (jnp.sqrt(v_hat) + ADAM_EPS) + ADAM_WD * w)
    return delta, m, v


def _ada_bwd_adamw(c_all, dmod_cols, w, m, v, name):
    tr = 256

    def body(c_ref, dm_ref, w_ref, m_ref, v_ref, g_ref, d_ref, nm_ref, nv_ref):
        cv = c_ref[...]
        sc = cv * _sigmoid(cv)
        g = lax.dot_general(sc, dm_ref[0], (((0,), (0,)), ((), ())), precision=_HI, preferred_element_type=f32)
        g_ref[0] = g
        d_ref[0], nm_ref[0], nv_ref[0] = _adam_math(w_ref[0], g, m_ref[0], v_ref[0])

    blk = pl.BlockSpec((1, tr, ADA_COLS), lambda l, i: (l, i, 0))
    shp = jax.ShapeDtypeStruct((DEPTH, D, ADA_COLS), f32)
    return pl.pallas_call(
        body, grid=(DEPTH, D // tr),
        in_specs=[pl.BlockSpec((8, tr), lambda l, i: (0, i)), pl.BlockSpec((1, 8, ADA_COLS), lambda l, i: (l, 0, 0)), blk, blk, blk],
        out_specs=[blk, blk, blk, blk], out_shape=[shp, shp, shp, shp], name=name, compiler_params=_cp(("parallel", "parallel")),
    )(c_all, dmod_cols, w, m, v)


def _adamw(w, g, m, v, name):
    def body(w_ref, g_ref, m_ref, v_ref, d_ref, nm_ref, nv_ref):
        d_ref[...], nm_ref[...], nv_ref[...] = _adam_math(w_ref[...], g_ref[...], m_ref[...], v_ref[...])

    shp = jax.ShapeDtypeStruct(w.shape, f32)
    return pl.pallas_call(body, out_shape=[shp] * 3, name=name, compiler_params=_cp(None))(w, g, m, v)


def _tile2(r, c):
    if r <= 256 or r % 256 == 0:
        return _pick(r, (256,)), _pick(c, (1024,))
    return r, 128


def _adamw_layers(w, g_mine, g_other, m, v, c_idx, name):
    _, r, c = w.shape
    tr, tc = _tile2(r, c)

    def body(ci_ref, w_ref, gm_ref, go_ref, m_ref, v_ref, g_ref, d_ref, nm_ref, nv_ref):
        def update(g):
            g_ref[0] = g
            d_ref[0], nm_ref[0], nv_ref[0] = _adam_math(w_ref[0], g, m_ref[0], v_ref[0])

        mine = pl.program_id(0) == ci_ref[0]

        @pl.when(mine)
        def _():
            update(gm_ref[...])

        @pl.when(jnp.logical_not(mine))
        def _():
            update(go_ref[...])

    blk = pl.BlockSpec((1, tr, tc), lambda l, i, j, ci: (l, i, j))
    gblk = pl.BlockSpec((tr, tc), lambda l, i, j, ci: (i, j))
    shp = jax.ShapeDtypeStruct(w.shape, f32)
    return pl.pallas_call(
        body,
        grid_spec=pltpu.PrefetchScalarGridSpec(num_scalar_prefetch=1, grid=(DEPTH, r // tr, c // tc),
                                               in_specs=[blk, gblk, gblk, blk, blk], out_specs=[blk, blk, blk, blk]),
        out_shape=[shp, shp, shp, shp], name=name, compiler_params=_cp(("parallel", "parallel", "parallel")),
    )(c_idx, w, g_mine, g_other, m, v)


_MESH = pl.DeviceIdType.MESH
_AXES = ("x", "y", "c")
_HBM = pl.BlockSpec(memory_space=pltpu.HBM)


def _my_place():
    return tuple(lax.axis_index(a) for a in _AXES)


def _allreduce(buf, axes, name):
    r = buf.shape[0]
    n = len(axes)

    def body(x_ref, o_ref, rbuf, ssem, rsem):
        me = dict(zip(_AXES, _my_place()))
        o_ref[...] = x_ref[...]
        for k, ax in enumerate(axes):
            peer = tuple(1 - me[a] if a == ax else me[a] for a in _AXES)
            cp = pltpu.make_async_remote_copy(src_ref=o_ref, dst_ref=rbuf.at[k], send_sem=ssem.at[k], recv_sem=rsem.at[k],
                                              device_id=peer, device_id_type=_MESH)
            cp.start()
            cp.wait()
            o_ref[...] = o_ref[...] + rbuf[k]

    vm = pl.BlockSpec(memory_space=pltpu.VMEM)
    return pl.pallas_call(
        body, out_shape=jax.ShapeDtypeStruct((r, 128), f32), in_specs=[vm], out_specs=vm,
        scratch_shapes=[pltpu.VMEM((n, r, 128), f32), pltpu.SemaphoreType.DMA((n,)), pltpu.SemaphoreType.DMA((n,))],
        name=name, compiler_params=pltpu.CompilerParams(vmem_limit_bytes=VMEM_LIMIT),
    )(buf)


def _allreduce_halves(buf, name):
    r = buf.shape[0]
    h = r // 2
    assert r % 16 == 0

    def body(x_ref, o_ref, rbuf, ssem, rsem):
        x, y, c = _my_place()
        mine = pl.ds(pl.multiple_of(c * h, 8), h)
        other = pl.ds(pl.multiple_of((1 - c) * h, 8), h)
        o_ref[...] = x_ref[...]
        steps = [(other, (x, y, 1 - c)), (mine, (1 - x, y, c)), (mine, (x, 1 - y, c))]
        for k, (rows, peer) in enumerate(steps):
            cp = pltpu.make_async_remote_copy(src_ref=o_ref.at[rows], dst_ref=rbuf.at[k], send_sem=ssem.at[k], recv_sem=rsem.at[k],
                                              device_id=peer, device_id_type=_MESH)
            cp.start()
            cp.wait()
            o_ref[mine, :] = o_ref[mine, :] + rbuf[k]
        cp = pltpu.make_async_remote_copy(src_ref=o_ref.at[mine], dst_ref=rbuf.at[3], send_sem=ssem.at[3], recv_sem=rsem.at[3],
                                          device_id=(x, y, 1 - c), device_id_type=_MESH)
        cp.start()
        cp.wait()
        o_ref[other, :] = rbuf[3]

    vm = pl.BlockSpec(memory_space=pltpu.VMEM)
    return pl.pallas_call(
        body, out_shape=jax.ShapeDtypeStruct((r, 128), f32), in_specs=[vm], out_specs=vm,
        scratch_shapes=[pltpu.VMEM((4, h, 128), f32), pltpu.SemaphoreType.DMA((4,)), pltpu.SemaphoreType.DMA((4,))],
        name=name, compiler_params=pltpu.CompilerParams(vmem_limit_bytes=VMEM_LIMIT),
    )(buf)


def _other_chips(x, y):
    return [(1 - x, y), (x, 1 - y), (1 - x, 1 - y)]


def _gather_body(ins, outs, ssem, rsem):
    na = len(ins)
    x, y, c = _my_place()
    k_me = 2 * x + y
    sibling = (x, y, 1 - c)
    chips = _other_chips(x, y)
    slots = [2 * cx + cy for cx, cy in chips]
    half = [r.shape[0] // 2 if r.shape[0] % 32 == 0 else None for r in ins]

    def part(ref, a, core):
        return ref if half[a] is None else ref.at[pl.ds(core * half[a], half[a])]

    def rcopy(a, src, slot, core, to, idx):
        return pltpu.make_async_remote_copy(src_ref=src, dst_ref=part(outs[a].at[slot], a, core), send_sem=ssem.at[idx],
                                            recv_sem=rsem.at[idx], device_id=to, device_id_type=_MESH)

    sent = []
    for j, chip in enumerate(chips):
        for a in range(na):
            cp = rcopy(a, part(ins[a], a, c), k_me, c, (*chip, c), j * na + a)
            cp.start()
            sent.append(cp)
    for j, chip in enumerate(chips):
        for a in range(na):
            rcopy(a, part(ins[a], a, c), slots[j], c, (*chip, c), j * na + a).wait_recv()
            if half[a] is not None:
                cp = rcopy(a, part(outs[a].at[slots[j]], a, c), slots[j], c, sibling, (3 + j) * na + a)
                cp.start()
                sent.append(cp)
    for j in range(3):
        for a in range(na):
            if half[a] is not None:
                rcopy(a, part(ins[a], a, c), slots[j], 1 - c, sibling, (3 + j) * na + a).wait_recv()
    for cp in sent:
        cp.wait_send()


def _gather_layer_behind(shards, name, collective_id):
    na = len(shards)
    hbm = pltpu.MemorySpace.HBM
    ins = [jax.new_ref(s, memory_space=hbm) for s in shards]
    outs = [jax.empty_ref(jax.ShapeDtypeStruct((NSHARD,) + s.shape, s.dtype), memory_space=hbm) for s in shards]

    @pl.kernel(mesh=plsc.ScalarSubcoreMesh(axis_name="sequencer", num_cores=1), name=name,
               scratch_types=(pltpu.SemaphoreType.DMA((6 * na,)), pltpu.SemaphoreType.DMA((6 * na,))),
               compiler_params=pltpu.CompilerParams(collective_id=collective_id))
    def launch(ssem, rsem):
        x, y, c = _my_place()
        barrier = pltpu.get_barrier_semaphore()
        peers = [(*chip, c) for chip in _other_chips(x, y)] + [(x, y, 1 - c)]
        for peer in peers:
            pl.semaphore_signal(barrier, inc=1, device_id=peer, device_id_type=_MESH)
        pl.semaphore_wait(barrier, len(peers))
        _gather_body(ins, outs, ssem, rsem)

    launch()
    return [o[...] for o in outs]


def _with_own(gathered, own):
    xi, yi, _ = _my_place()
    whole = lax.dynamic_update_index_in_dim(gathered, own, 2 * xi + yi, 0)
    return [whole[k] for k in range(NSHARD)]


def _swap_layers(parts, name):
    na = len(parts)

    def body(*refs):
        ins, outs = refs[:2 * na], refs[2 * na:3 * na]
        ssem, rsem = refs[3 * na:]
        x, y, c = _my_place()

        def copy(a, layer):
            return pltpu.make_async_remote_copy(src_ref=ins[2 * a + layer], dst_ref=outs[a], send_sem=ssem.at[a], recv_sem=rsem.at[a],
                                                device_id=(x, y, 1 - c), device_id_type=_MESH)

        for layer in range(DEPTH):
            @pl.when(c == 1 - layer)
            def _():
                for a in range(na):
                    copy(a, layer).start()
        for a in range(na):
            copy(a, 0).wait()

    flat = [p for pair in parts for p in pair]
    return pl.pallas_call(
        body, out_shape=[jax.ShapeDtypeStruct(p0.shape, p0.dtype) for p0, _ in parts], in_specs=[_HBM] * (2 * na),
        out_specs=[_HBM] * na, scratch_shapes=[pltpu.SemaphoreType.DMA((na,)), pltpu.SemaphoreType.DMA((na,))], name=name,
    )(*flat)


def _scatter_shards_behind(sums, name, collective_id):
    na = len(sums)
    hbm = pltpu.MemorySpace.HBM
    ins = [jax.new_ref(p, memory_space=hbm) for p in sums]
    outs = [jax.empty_ref(jax.ShapeDtypeStruct((3,) + p.shape[1:], p.dtype), memory_space=hbm) for p in sums]

    @pl.kernel(mesh=plsc.ScalarSubcoreMesh(axis_name="sequencer", num_cores=1), name=name,
               scratch_types=(pltpu.SemaphoreType.DMA((3 * na,)), pltpu.SemaphoreType.DMA((3 * na,))),
               compiler_params=pltpu.CompilerParams(collective_id=collective_id))
    def launch(ssem, rsem):
        x, y, c = _my_place()
        barrier = pltpu.get_barrier_semaphore()
        chips = _other_chips(x, y)
        for chip in chips:
            pl.semaphore_signal(barrier, inc=1, device_id=(*chip, c), device_id_type=_MESH)
        pl.semaphore_wait(barrier, len(chips))
        cps = []
        for j, chip in enumerate(chips):
            kj = 2 * chip[0] + chip[1]
            for a in range(na):
                cps.append(pltpu.make_async_remote_copy(
                    src_ref=ins[a].at[kj], dst_ref=outs[a].at[j], send_sem=ssem.at[j * na + a], recv_sem=rsem.at[j * na + a],
                    device_id=(*chip, c), device_id_type=_MESH))
        for cp in cps:
            cp.start()
        for cp in cps:
            cp.wait()

    launch()
    return [o[...] for o in outs]


def _share_layers(finals, name):
    na = len(finals)

    def body(*refs):
        ins, outs = refs[:na], refs[na:2 * na]
        ssem, rsem = refs[2 * na:]
        x, y, c = _my_place()
        cps = [pltpu.make_async_remote_copy(src_ref=ins[a], dst_ref=outs[a], send_sem=ssem.at[a], recv_sem=rsem.at[a],
                                            device_id=(x, y, 1 - c), device_id_type=_MESH) for a in range(na)]
        for cp in cps:
            cp.start()
        for cp in cps:
            cp.wait()

    return pl.pallas_call(
        body, out_shape=[jax.ShapeDtypeStruct(p.shape, p.dtype) for p in finals], in_specs=[_HBM] * na, out_specs=[_HBM] * na,
        scratch_shapes=[pltpu.SemaphoreType.DMA((na,)), pltpu.SemaphoreType.DMA((na,))], name=name,
    )(*finals)


def _add_own_layer(part0, part1, recv, c_idx, name):
    ns, r, c = recv.shape
    tr, tc = _tile2(r, c)

    def body(ci_ref, p0_ref, p1_ref, r_ref, o_ref, ob_ref):
        def add(p_ref):
            t = p_ref[...] + r_ref[...]
            o_ref[...] = t
            ob_ref[...] = t.astype(bf16)

        @pl.when(ci_ref[0] == 0)
        def _():
            add(p0_ref)

        @pl.when(ci_ref[0] == 1)
        def _():
            add(p1_ref)

    blk = pl.BlockSpec((1, tr, tc), lambda k, i, j, ci: (k, i, j))
    blk0 = pl.BlockSpec((1, tr, tc), lambda k, i, j, ci: (k * (1 - ci[0]), i * (1 - ci[0]), j * (1 - ci[0])))
    blk1 = pl.BlockSpec((1, tr, tc), lambda k, i, j, ci: (k * ci[0], i * ci[0], j * ci[0]))
    return pl.pallas_call(
        body,
        grid_spec=pltpu.PrefetchScalarGridSpec(num_scalar_prefetch=1, grid=(ns, r // tr, c // tc), in_specs=[blk0, blk1, blk],
                                               out_specs=[blk, blk]),
        out_shape=[jax.ShapeDtypeStruct((ns, r, c), f32), jax.ShapeDtypeStruct((ns, r, c), bf16)], name=name,
        compiler_params=_cp(("arbitrary", "arbitrary", "arbitrary")),
    )(c_idx, part0, part1, recv)


def _add_own_shard(sums, recv, k_idx, name):
    _, r, c = sums.shape
    tr, tc = _tile2(r, c)

    def body(ki_ref, s_ref, r_ref, o_ref):
        o_ref[...] = ((s_ref[0] + r_ref[0].astype(f32)) + r_ref[1].astype(f32)) + r_ref[2].astype(f32)

    return pl.pallas_call(
        body,
        grid_spec=pltpu.PrefetchScalarGridSpec(
            num_scalar_prefetch=1, grid=(r // tr, c // tc),
            in_specs=[pl.BlockSpec((1, tr, tc), lambda i, j, ki: (ki[0], i, j)), pl.BlockSpec((3, tr, tc), lambda i, j, ki: (0, i, j))],
            out_specs=pl.BlockSpec((tr, tc), lambda i, j, ki: (i, j))),
        out_shape=jax.ShapeDtypeStruct((r, c), f32), name=name, compiler_params=_cp(("parallel", "parallel")),
    )(k_idx, sums, recv)


def _reduce_scatter(parts, tag, meanwhile):
    x, y, c = _my_place()
    c_idx = jnp.reshape(c, (1,)).astype(jnp.int32)
    k_idx = jnp.reshape(2 * x + y, (1,)).astype(jnp.int32)
    na = len(parts)
    recv = _swap_layers(parts, f"rs_swap{tag}")
    sums = [_add_own_layer(parts[a][0], parts[a][1], recv[a], c_idx, f"rs_add_layer{tag}_{a}") for a in range(na)]
    recv = _scatter_shards_behind([sb for _, sb in sums], f"rs_scatter{tag}", 3)
    recv, side = lax.optimization_barrier((recv, meanwhile()))
    finals = [_add_own_shard(sums[a][0], recv[a], k_idx, f"rs_add_shard{tag}_{a}") for a in range(na)]
    return finals, _share_layers(finals, f"rs_share{tag}"), c_idx, side


_SMALL = [("ada_b", (DEPTH, 3 * D)), ("norm_w", (DEPTH, D)), ("gm_ln_w", (DEPTH, D)), ("gm_ln_b", (DEPTH, D)),
          ("gm_ws", (DEPTH, NG, CH, CH)), ("gm_bs", (DEPTH, NG, CH)), ("conv_b", (DEPTH, CONVD)), ("dt_bias", (DEPTH, NH)),
          ("a_log", (DEPTH, NH)), ("d_skip", (DEPTH, NH)), ("ssm_norm_w", (DEPTH, DIN)), ("final_norm_w", (D,))]


def _rows_of(shape):
    n = 1
    for d in shape:
        n *= d
    return -(-n // 1024) * 8


def _pack(arrays):
    rows = []
    for a in arrays:
        flat = a.reshape(-1)
        r = _rows_of(a.shape)
        rows.append(jnp.pad(flat, (0, r * 128 - flat.shape[0])).reshape(r, 128))
    return jnp.concatenate(rows, axis=0)


def _unpack(buf, shapes):
    out, at = [], 0
    for shp in shapes:
        r = _rows_of(shp)
        n = 1
        for d in shp:
            n *= d
        out.append(buf[at:at + r].reshape(-1)[:n].reshape(shp))
        at += r
    return out


def kernel(x, c, ada_w, ada_b, norm_w, w_in, gm_ln_w, gm_ln_b, gm_ws, gm_bs, conv_w, conv_b, dt_bias, a_log, d_skip, ssm_norm_w, w_proj_a, w_proj_b, w_out, final_norm_w, loss_target, m_ada_w, m_ada_b, m_norm_w, m_w_in, m_gm_ln_w, m_gm_ln_b, m_gm_ws, m_gm_bs, m_conv_w, m_conv_b, m_dt_bias, m_a_log, m_d_skip, m_ssm_norm_w, m_w_proj_a, m_w_proj_b, m_w_out, m_final_norm_w, v_ada_w, v_ada_b, v_norm_w, v_w_in, v_gm_ln_w, v_gm_ln_b, v_gm_ws, v_gm_bs, v_conv_w, v_conv_b, v_dt_bias, v_a_log, v_d_skip, v_ssm_norm_w, v_w_proj_a, v_w_proj_b, v_w_out, v_final_norm_w):
    xi, yi, ci = _my_place()
    k_me = 2 * xi + yi
    b_me = 4 * xi + 2 * yi + ci
    w = dict(ada_b=ada_b, norm_w=norm_w, gm_ln_w=gm_ln_w, gm_ln_b=gm_ln_b, gm_ws=gm_ws, gm_bs=gm_bs, conv_b=conv_b, dt_bias=dt_bias,
             a_log=a_log, d_skip=d_skip, ssm_norm_w=ssm_norm_w, final_norm_w=final_norm_w)
    m = dict(ada_b=m_ada_b, norm_w=m_norm_w, gm_ln_w=m_gm_ln_w, gm_ln_b=m_gm_ln_b, gm_ws=m_gm_ws, gm_bs=m_gm_bs, conv_b=m_conv_b,
             dt_bias=m_dt_bias, a_log=m_a_log, d_skip=m_d_skip, ssm_norm_w=m_ssm_norm_w, final_norm_w=m_final_norm_w)
    v = dict(ada_b=v_ada_b, norm_w=v_norm_w, gm_ln_w=v_gm_ln_w, gm_ln_b=v_gm_ln_b, gm_ws=v_gm_ws, gm_bs=v_gm_bs, conv_b=v_conv_b,
             dt_bias=v_dt_bias, a_log=v_a_log, d_skip=v_d_skip, ssm_norm_w=v_ssm_norm_w, final_norm_w=v_final_norm_w)

    c_slot = lax.dynamic_update_slice(jnp.zeros((8, D), f32), c, (b_me, 0))
    c_all = _allreduce(c_slot.reshape(64, 128), _AXES, "gather_c").reshape(8, D)
    ada_b_cols = lax.dynamic_slice(ada_b, (0, k_me * ADA_COLS), (DEPTH, ADA_COLS)).reshape(DEPTH, 1, ADA_COLS)
    mod_cols = _ada_fwd(c_all, ada_w, ada_b_cols, "ada_fwd")
    mod_slot = lax.dynamic_update_slice(jnp.zeros((DEPTH, 8, 3 * D), f32), mod_cols, (0, 0, k_me * ADA_COLS))
    mod_all = _allreduce(mod_slot.reshape(-1, 128), ("x", "y"), "gather_mod").reshape(DEPTH, 8, 3 * D)
    mod_me = lax.dynamic_slice(mod_all, (0, b_me, 0), (DEPTH, 1, 3 * D))
    mods = [(mod_me[l, :, 0:D], mod_me[l, :, D:2 * D], mod_me[l, :, 2 * D:3 * D]) for l in range(DEPTH)]

    rows_sh = jnp.concatenate([w_proj_a, w_proj_b, w_out], axis=1).astype(bf16)
    win_sh = w_in.astype(bf16)
    first = _gather_layer_behind([win_sh[0], conv_w[0]], "gather_l0", 1)
    first, mods, next_in, later = lax.optimization_barrier((first, mods, [rows_sh[0]], [win_sh[1], conv_w[1], rows_sh[1]]))
    first_rows = _gather_layer_behind(next_in, "gather_l0_merge", 5)
    second = _gather_layer_behind(later, "gather_l1", 2)
    others = [(first[0], first[1], first_rows[0]), second]

    def merge_matrices(rows_g, l):
        rows_l = _with_own(rows_g, rows_sh[l])
        return (jnp.concatenate([t[0:RA] for t in rows_l], axis=0), jnp.concatenate([t[RA:RA + RB] for t in rows_l], axis=0),
                jnp.concatenate([t[RA + RB:] for t in rows_l], axis=0))

    def operands_of(l, x_in):
        win_g, conv_g, rows_g = others[l]
        if l == 1:
            (win_g, conv_g, rows_g), x_in = lax.optimization_barrier(((win_g, conv_g, rows_g), x_in))
        w_in_l = jnp.concatenate(_with_own(win_g, win_sh[l]), axis=1)
        cw_l = jnp.concatenate(_with_own(conv_g, conv_w[l]), axis=1)
        wa_l, wb_l, wo_l = merge_matrices(rows_g, l) if l == 1 else (None, None, None)
        p = _layer_operands(w_in_l, cw_l, wa_l, wb_l, wo_l, norm_w[l], gm_ln_w[l], gm_ln_b[l], gm_ws[l], gm_bs[l], conv_b[l],
                            dt_bias[l], a_log[l], d_skip[l], ssm_norm_w[l])
        if l == 0:
            def late(yb):
                rows_now, yb = lax.optimization_barrier((rows_g, yb))
                return (*merge_matrices(rows_now, l), yb)
            p["merge_weights"] = late
        return p, x_in

    loss_parts, dx, grads, dfnw = _local_step(x[0], loss_target[0], mods, operands_of, final_norm_w)

    s_in = N_IN // NSHARD
    tr_ = lambda t: jnp.swapaxes(t, 1, 2)
    g_in = [grads[l]["w_in_t"].reshape(NSHARD, s_in, D) for l in range(DEPTH)]
    g_rows = [jnp.concatenate([grads[l]["w_proj_a"].reshape(NSHARD, RA, D), grads[l]["w_proj_b"].reshape(NSHARD, RB, D),
                               grads[l]["w_out"].reshape(NSHARD, RA, D)], axis=1) for l in range(DEPTH)]
    g_conv = [grads[l]["conv_w"].reshape(4, NSHARD, D).transpose(1, 0, 2) for l in range(DEPTH)]

    dmod_slot = lax.dynamic_update_slice(jnp.zeros((DEPTH, 8, 3 * D), f32),
                                         jnp.stack([grads[l]["mod"] for l in range(DEPTH)]).reshape(DEPTH, 1, 3 * D), (0, b_me, 0))
    small_g = {n: (dfnw if n == "final_norm_w" else jnp.stack([grads[l]["mod" if n == "ada_b" else n] for l in range(DEPTH)]))
               for n, _ in _SMALL}
    n_small = sum(_rows_of(s) for _, s in _SMALL)
    n_dmod = _rows_of(dmod_slot.shape)

    def small_work():
        tail = [jnp.zeros((8, 128), f32)] if (n_small + n_dmod + 8) % 16 else []
        packed = _allreduce_halves(_pack([small_g[n] for n, _ in _SMALL] + [dmod_slot, loss_parts] + tail), "allreduce_small")
        dmod_all = packed[n_small:n_small + n_dmod].reshape(DEPTH, 8, 3 * D)
        dmod_cols = lax.dynamic_slice(dmod_all, (0, 0, k_me * ADA_COLS), (DEPTH, 8, ADA_COLS))
        return packed, _ada_bwd_adamw(c_all, dmod_cols, ada_w, m_ada_w, v_ada_w, "ada_bwd_adamw")

    (f_in, f_rows, f_conv), (o_in, o_rows, o_conv), c_idx, (packed, ada_out) = _reduce_scatter([g_in, g_rows, g_conv], "", small_work)
    g_ada, d_ada, nm_ada, nv_ada = ada_out
    gr_in, d_in, nm_in, nv_in = [tr_(t) for t in _adamw_layers(tr_(w_in), f_in, o_in, tr_(m_w_in), tr_(v_w_in), c_idx, "adamw_w_in")]
    cat = lambda a, b, c_: jnp.concatenate([a, b, c_], axis=1)
    gr_rows, d_rows, nm_rows, nv_rows = _adamw_layers(cat(w_proj_a, w_proj_b, w_out), f_rows, o_rows, cat(m_w_proj_a, m_w_proj_b, m_w_out),
                                                      cat(v_w_proj_a, v_w_proj_b, v_w_out), c_idx, "adamw_rows")
    gr_conv = jnp.where(ci == 0, jnp.stack([f_conv, o_conv]), jnp.stack([o_conv, f_conv]))
    split = lambda t: (t[:, 0:RA], t[:, RA:RA + RB], t[:, RA + RB:])

    g_small = packed[0:n_small]
    loss = jnp.sum(packed[n_small + n_dmod:])
    small_gw = jnp.concatenate([g_small, _pack([gr_conv])], axis=0)
    d_s, nm_s, nv_s = _adamw(_pack([w[n] for n, _ in _SMALL] + [conv_w]), small_gw,
                             _pack([m[n] for n, _ in _SMALL] + [m_conv_w]), _pack([v[n] for n, _ in _SMALL] + [v_conv_w]), "adamw_small")
    shapes = [s for _, s in _SMALL] + [conv_w.shape]
    names = [n for n, _ in _SMALL] + ["conv_w"]
    g_d = dict(zip(names, _unpack(small_gw, shapes)))
    d_d = dict(zip(names, _unpack(d_s, shapes)))
    nm_d = dict(zip(names, _unpack(nm_s, shapes)))
    nv_d = dict(zip(names, _unpack(nv_s, shapes)))

    def by_name(big, small):
        ga, gb, go = split(big[1])
        return dict(small, ada_w=big[2], w_in=big[0], w_proj_a=ga, w_proj_b=gb, w_out=go)

    order = ["ada_w", "ada_b", "norm_w", "w_in", "gm_ln_w", "gm_ln_b", "gm_ws", "gm_bs", "conv_w", "conv_b", "dt_bias", "a_log",
             "d_skip", "ssm_norm_w", "w_proj_a", "w_proj_b", "w_out", "final_norm_w"]
    outs = []
    for big, small in (((gr_in, gr_rows, g_ada), g_d), ((d_in, d_rows, d_ada), d_d), ((nm_in, nm_rows, nm_ada), nm_d),
                       ((nv_in, nv_rows, nv_ada), nv_d)):
        t = by_name(big, small)
        outs += [t[n] for n in order]
    return (loss, dx.reshape(1, -1, D), *outs)
```

```python
import jax
import jax.numpy as jnp
from jax import lax
from jax.experimental import pallas as pl
from jax.experimental.pallas import tpu as pltpu
from jax.experimental.pallas import tpu_sc as plsc

f32 = jnp.float32
bf16 = jnp.bfloat16

D = 1024
DEPTH = 2
EPS = 1e-6
CH = 128
NG = 8
HPG = 4
HD = 64
NH = NG * HPG
NST = 128
DIN = 2048
CONVD = 4096
GW = DIN // NG
PB = CONVD + DIN + 256
PA = 3 * D
PG = 2 * D
N_IN = 11296
NSHARD = 4
RA, RB = D // NSHARD, DIN // NSHARD
V7X_VMEM_BYTES = 64 * 2 ** 20
VMEM_LIMIT = V7X_VMEM_BYTES - 8 * 2 ** 20

ADAM_LR, ADAM_B1, ADAM_B2, ADAM_EPS, ADAM_WD, ADAM_STEP = 0.001, 0.9, 0.999, 1e-08, 0.01, 10

_HI = lax.Precision.HIGHEST


def _cp(sem):
    return pltpu.CompilerParams(dimension_semantics=sem, vmem_limit_bytes=VMEM_LIMIT)


def _sigmoid(x):
    return 0.5 * jnp.tanh(0.5 * x) + 0.5


def _silu_and_grad(x):
    s = _sigmoid(x)
    return x * s, s * (1.0 + x * (1.0 - s))


_GELU_K = 0.7978845608028654
_GELU_C = 0.044715


def _gelu_and_grad(x):
    x2 = x * x
    t = jnp.tanh(_GELU_K * (x + _GELU_C * x * x2))
    g = 0.5 * x * (1.0 + t)
    dg = 0.5 * (1.0 + t) + 0.5 * x * (1.0 - t * t) * _GELU_K * (1.0 + 3.0 * _GELU_C * x2)
    return g, dg


def _gelu(x):
    t = jnp.tanh(_GELU_K * (x + _GELU_C * x * x * x))
    return 0.5 * x * (1.0 + t)


def _softplus(x):
    return jnp.maximum(x, 0.0) + jnp.log(1.0 + jnp.exp(-jnp.abs(x)))


def _dot(a, b):
    return jnp.dot(a, b, preferred_element_type=f32)


def _dot_nt(a, b):
    return lax.dot_general(a, b, (((1,), (1,)), ((), ())), preferred_element_type=f32)


def _dot_tn(a, b):
    return lax.dot_general(a, b, (((0,), (0,)), ((), ())), preferred_element_type=f32)


def _dot_hi(a, b):
    return jnp.dot(a, b, precision=_HI, preferred_element_type=f32)


def _rmsmod_fwd(x, nw, scale, shift, name):
    s = x.shape[0]
    ts = min(1024, s)

    def body(x_ref, nw_ref, sc_ref, sh_ref, h_ref):
        xv = x_ref[...]
        r = lax.rsqrt(jnp.mean(xv * xv, axis=-1, keepdims=True) + EPS)
        h_ref[...] = ((xv * r) * nw_ref[...] * (1.0 + sc_ref[...]) + sh_ref[...]).astype(bf16)

    row = pl.BlockSpec((1, D), lambda i: (0, 0))
    tile = pl.BlockSpec((ts, D), lambda i: (i, 0))
    return pl.pallas_call(
        body, grid=(s // ts,), in_specs=[tile, row, row, row], out_specs=tile,
        out_shape=jax.ShapeDtypeStruct((s, D), bf16), name=name, compiler_params=_cp(("parallel",)),
    )(x, nw, scale, shift)


def _rmsmod_bwd(dh, x, dres, nw, scale, name):
    s = x.shape[0]
    ts = min(1024, s)

    def body(dh_ref, x_ref, dres_ref, nw_ref, sc_ref, dx_ref, dsc_ref, dsh_ref, dnw_ref):
        @pl.when(pl.program_id(0) == 0)
        def _():
            dsc_ref[...] = jnp.zeros_like(dsc_ref)
            dsh_ref[...] = jnp.zeros_like(dsh_ref)
            dnw_ref[...] = jnp.zeros_like(dnw_ref)
        xv = x_ref[...]
        dhv = dh_ref[...]
        r = lax.rsqrt(jnp.mean(xv * xv, axis=-1, keepdims=True) + EPS)
        xn = xv * r
        one_sc = 1.0 + sc_ref[...]
        dsc_ref[...] += jnp.sum(dhv * xn * nw_ref[...], axis=0, keepdims=True)
        dsh_ref[...] += jnp.sum(dhv, axis=0, keepdims=True)
        dnw_ref[...] += jnp.sum(dhv * xn * one_sc, axis=0, keepdims=True)
        dxn = dhv * (nw_ref[...] * one_sc)
        dx_ref[...] = r * (dxn - xn * jnp.mean(dxn * xn, axis=-1, keepdims=True)) + dres_ref[...]

    row = pl.BlockSpec((1, D), lambda i: (0, 0))
    tile = pl.BlockSpec((ts, D), lambda i: (i, 0))
    vec = jax.ShapeDtypeStruct((1, D), f32)
    return pl.pallas_call(
        body, grid=(s // ts,), in_specs=[tile, tile, tile, row, row], out_specs=[tile, row, row, row],
        out_shape=[jax.ShapeDtypeStruct((s, D), f32), vec, vec, vec], name=name, compiler_params=_cp(("arbitrary",)),
    )(dh, x, dres, nw, scale)


def _pick(n, prefs):
    for p in prefs:
        if n % p == 0:
            return p
    return n


def _mm(a, b, out_dtype, name, c_in=None, trans_b=False):
    m, k = a.shape
    n = b.shape[0] if trans_b else b.shape[1]
    tn = _pick(n, (1280, 1024, 512))
    tk = _pick(k, (1280, 1024, 512))
    nk = k // tk
    one_pass = nk == 1 and c_in is None
    tm = _pick(m, (2048, 1024, 512, 256) if one_pass else (1024, 512, 256))
    dot = _dot_nt if trans_b else _dot

    def body(*refs):
        if one_pass:
            a_ref, b_ref, o_ref = refs
            o_ref[...] = dot(a_ref[...], b_ref[...]).astype(out_dtype)
            return
        if c_in is not None:
            a_ref, b_ref, c_ref, o_ref, acc = refs
        else:
            a_ref, b_ref, o_ref, acc = refs
        kk = pl.program_id(2)

        @pl.when(kk == 0)
        def _():
            if c_in is not None:
                acc[...] = c_ref[...]
            else:
                acc[...] = jnp.zeros_like(acc)
        acc[...] += dot(a_ref[...], b_ref[...])

        @pl.when(kk == nk - 1)
        def _():
            o_ref[...] = acc[...].astype(out_dtype)

    b_spec = pl.BlockSpec((tn, tk), lambda j, i, kk: (j, kk)) if trans_b else pl.BlockSpec((tk, tn), lambda j, i, kk: (kk, j))
    in_specs = [pl.BlockSpec((tm, tk), lambda j, i, kk: (i, kk)), b_spec]
    args = [a, b]
    if c_in is not None:
        in_specs.append(pl.BlockSpec((tm, tn), lambda j, i, kk: (i, j)))
        args.append(c_in)
    return pl.pallas_call(
        body, grid=(n // tn, m // tm, nk), in_specs=in_specs, out_specs=pl.BlockSpec((tm, tn), lambda j, i, kk: (i, j)),
        out_shape=jax.ShapeDtypeStruct((m, n), out_dtype), scratch_shapes=[] if one_pass else [pltpu.VMEM((tm, tn), f32)],
        name=name, compiler_params=_cp(("parallel", "parallel", "arbitrary")),
    )(*args)


def _mm_tn(a, b, name):
    t, k1 = a.shape
    n = b.shape[1]
    t1 = _pick(k1, (1280, 1024, 512))
    tn = _pick(n, (1280, 1024, 512))
    tt = _pick(t, (2048, 1024, 512, 256))
    nt = t // tt

    def body(a_ref, b_ref, o_ref):
        tt_i = pl.program_id(2)

        @pl.when(tt_i == 0)
        def _():
            o_ref[...] = jnp.zeros_like(o_ref)
        o_ref[...] += _dot_tn(a_ref[...], b_ref[...])

    return pl.pallas_call(
        body, grid=(k1 // t1, n // tn, nt),
        in_specs=[pl.BlockSpec((tt, t1), lambda i, j, tt_i: (tt_i, i)), pl.BlockSpec((tt, tn), lambda i, j, tt_i: (tt_i, j))],
        out_specs=pl.BlockSpec((t1, tn), lambda i, j, tt_i: (i, j)),
        out_shape=jax.ShapeDtypeStruct((k1, n), f32), name=name,
        compiler_params=_cp(("parallel", "parallel", "arbitrary")),
    )(a, b)


def _ln_stats(v):
    mu = jnp.mean(v, axis=-1, keepdims=True)
    vc = v - mu
    rstd = lax.rsqrt(jnp.mean(vc * vc, axis=-1, keepdims=True) + EPS)
    return vc * rstd, rstd


def _mix(w_ref, vl):
    return jnp.concatenate([_dot(w_ref[g], vl[:, g * CH:(g + 1) * CH]) for g in range(NG)], axis=1)


def _branch_a_fwd(proj_a, lnw, lnb, wsm, bsf, name):
    s = proj_a.shape[0]
    ta = min(512, s)

    def body(pu_ref, pv_ref, pz_ref, lnw_ref, lnb_ref, w_ref, bs_ref, ya_ref):
        for c in range(ta // CH):
            rows = pl.ds(c * CH, CH)
            vh, _ = _ln_stats(_gelu(pv_ref[rows, :]))
            vl = (vh * lnw_ref[...] + lnb_ref[...]).astype(bf16)
            mixed = _mix(w_ref, vl) + bs_ref[...]
            pz = pz_ref[rows, :]
            ya_ref[rows, :] = (_gelu(pu_ref[rows, :]) * mixed * (pz * _sigmoid(pz))).astype(bf16)

    row = pl.BlockSpec((1, D), lambda i: (0, 0))
    return pl.pallas_call(
        body, grid=(s // ta,),
        in_specs=[pl.BlockSpec((ta, D), lambda i: (i, 0)), pl.BlockSpec((ta, D), lambda i: (i, 1)),
                  pl.BlockSpec((ta, D), lambda i: (i, 2)), row, row,
                  pl.BlockSpec((NG, CH, CH), lambda i: (0, 0, 0)), pl.BlockSpec((CH, D), lambda i: (0, 0))],
        out_specs=pl.BlockSpec((ta, D), lambda i: (i, 0)),
        out_shape=jax.ShapeDtypeStruct((s, D), bf16), name=name, compiler_params=_cp(("parallel",)),
    )(proj_a, proj_a, proj_a, lnw, lnb, wsm, bsf)


def _branch_a_bwd(proj_a, dya, lnw, lnb, wsm, wsm_t, bsf, name):
    s = proj_a.shape[0]
    ta = min(512, s)

    def body(pu_ref, pv_ref, pz_ref, dya_ref, lnw_ref, lnb_ref, w_ref, wt_ref, bs_ref,
             dp_ref, dws_ref, dbs_ref, dlnw_ref, dlnb_ref):
        @pl.when(pl.program_id(0) == 0)
        def _():
            dws_ref[...] = jnp.zeros_like(dws_ref)
            dbs_ref[...] = jnp.zeros_like(dbs_ref)
            dlnw_ref[...] = jnp.zeros_like(dlnw_ref)
            dlnb_ref[...] = jnp.zeros_like(dlnb_ref)
        for c in range(ta // CH):
            rows = pl.ds(c * CH, CH)
            u, du = _gelu_and_grad(pu_ref[rows, :])
            v, dv_act = _gelu_and_grad(pv_ref[rows, :])
            zg, dzg = _silu_and_grad(pz_ref[rows, :])
            vh, rstd = _ln_stats(v)
            vl = (vh * lnw_ref[...] + lnb_ref[...]).astype(bf16)
            mixed = _mix(w_ref, vl) + bs_ref[...]
            dy = dya_ref[rows, :].astype(f32)
            dmixed = dy * u * zg
            dp_ref[rows, 0:D] = (dy * mixed * zg * du).astype(bf16)
            dp_ref[rows, 2 * D:3 * D] = (dy * u * mixed * dzg).astype(bf16)
            dmb = dmixed.astype(bf16)
            dbs_ref[...] += dmixed
            dvl = _mix(wt_ref, dmb)
            for g in range(NG):
                cols = slice(g * CH, (g + 1) * CH)
                dws_ref[g] += _dot_nt(dmb[:, cols], vl[:, cols])
            dlnw_ref[...] += jnp.sum(dvl * vh, axis=0, keepdims=True)
            dlnb_ref[...] += jnp.sum(dvl, axis=0, keepdims=True)
            dvh = dvl * lnw_ref[...]
            dv = rstd * (dvh - jnp.mean(dvh, axis=-1, keepdims=True) - vh * jnp.mean(dvh * vh, axis=-1, keepdims=True))
            dp_ref[rows, D:2 * D] = (dv * dv_act).astype(bf16)

    row = pl.BlockSpec((1, D), lambda i: (0, 0))
    wspec = pl.BlockSpec((NG, CH, CH), lambda i: (0, 0, 0))
    full = pl.BlockSpec((CH, D), lambda i: (0, 0))
    return pl.pallas_call(
        body, grid=(s // ta,),
        in_specs=[pl.BlockSpec((ta, D), lambda i: (i, 0)), pl.BlockSpec((ta, D), lambda i: (i, 1)),
                  pl.BlockSpec((ta, D), lambda i: (i, 2)), pl.BlockSpec((ta, D), lambda i: (i, 0)),
                  row, row, wspec, wspec, full],
        out_specs=[pl.BlockSpec((ta, PA), lambda i: (i, 0)), wspec, full, row, row],
        out_shape=[jax.ShapeDtypeStruct((s, PA), bf16), jax.ShapeDtypeStruct((NG, CH, CH), f32),
                   jax.ShapeDtypeStruct((CH, D), f32), jax.ShapeDtypeStruct((1, D), f32), jax.ShapeDtypeStruct((1, D), f32)],
        name=name, compiler_params=_cp(("arbitrary",)),
    )(proj_a, proj_a, proj_a, dya, lnw, lnb, wsm, wsm_t, bsf)


GB = GW + 2 * NST


def _group_major(xs, b, c):
    lead = xs.shape[:-1]
    return jnp.concatenate([xs.reshape(lead + (NG, GW)), b.reshape(lead + (NG, NST)), c.reshape(lead + (NG, NST))],
                           axis=-1).reshape(lead + (CONVD,))


def _from_group_major(t):
    lead = t.shape[:-1]
    t = t.reshape(lead + (NG, GB))
    return jnp.concatenate([t[..., 0:GW].reshape(lead + (DIN,)), t[..., GW:GW + NST].reshape(lead + (NG * NST,)),
                            t[..., GW + NST:GB].reshape(lead + (NG * NST,))], axis=-1)


def _rows_from_group_major(t):
    t = t.reshape(NG, GB, t.shape[-1])
    return jnp.concatenate([t[:, 0:GW].reshape(DIN, -1), t[:, GW:GW + NST].reshape(NG * NST, -1),
                            t[:, GW + NST:GB].reshape(NG * NST, -1)], axis=0)


def _shift_rows(x, prev8, j):
    xr = pltpu.roll(x, j, 0)
    fix = pltpu.roll(prev8, j, 0)
    rid = lax.broadcasted_iota(jnp.int32, (8, x.shape[1]), 0)
    top = jnp.where(rid < j, fix, xr[0:8])
    return jnp.concatenate([top, xr[8:]], axis=0)


def _shift_rows_up(d, next8, j):
    dr = pltpu.roll(d, CH - j, 0)
    fix = pltpu.roll(next8, 8 - j, 0)
    rid = lax.broadcasted_iota(jnp.int32, (8, d.shape[1]), 0)
    bot = jnp.where(rid >= 8 - j, fix, dr[CH - 8:CH])
    return jnp.concatenate([dr[0:CH - 8], bot], axis=0)


def _conv_pre(x, prev8, cw_ref, cb_ref, cols):
    shifted = [_shift_rows(x, prev8, j) for j in (1, 2, 3)]
    conv = cb_ref[:, cols] + cw_ref[3:4, cols] * x
    for j in (1, 2, 3):
        conv = conv + cw_ref[3 - j:4 - j, cols] * shifted[j - 1]
    return conv, shifted


def _tril_mask():
    return lax.broadcasted_iota(jnp.int32, (CH, CH), 0) >= lax.broadcasted_iota(jnp.int32, (CH, CH), 1)


def _sum_all(v):
    return jnp.sum(jnp.sum(v, axis=0, keepdims=True), axis=1, keepdims=True)


def _lanes(g, width, base=0):
    return pl.ds(pl.multiple_of(base + g * width, width), width)


def _branch_b_fwd(proj_b, cw, cb, dtb, a_row, dkc, snw, name):
    s = proj_b.shape[0]
    nc = s // CH

    def body(xbc_ref, sz_ref, dtr_ref, cw_ref, cb_ref, dtb_ref, a_ref, dkc_ref, snw_ref,
             yb_ref, y_ref, st_ref, cv_ref, prev8, state, acst_s):
        @pl.when(pl.program_id(0) == 0)
        def _():
            prev8[...] = jnp.zeros_like(prev8)
            state[...] = jnp.zeros_like(state)
        st_ref[0] = state[...].astype(bf16)
        mask = _tril_mask()
        dt_all = _softplus(dtr_ref[:, 0:CH] + dtb_ref[...])
        acs_all = _dot_hi(mask.astype(f32), dt_all * a_ref[...])
        acst_s[...] = acs_all.T

        def group(g, carry):
            cols = _lanes(g, GB)
            gcols = _lanes(g, GW)
            x = xbc_ref[:, cols]
            conv, _ = _conv_pre(x, prev8[:, cols], cw_ref, cb_ref, cols)
            prev8[:, cols] = x[CH - 8:CH]
            cv_ref[:, cols] = conv.astype(bf16)
            xc = conv * _sigmoid(conv)
            xs = xc[:, 0:GW]
            bg = xc[:, GW:GW + NST].astype(bf16)
            cg = xc[:, GW + NST:GB].astype(bf16)
            back = lax.rem(CH - HPG * g, CH)
            dt = pltpu.roll(dt_all, back, 1)
            acs = pltpu.roll(acs_all, back, 1)
            cbm = _dot_nt(cg, bg)
            dkc_g = dkc_ref[:, gcols]
            y_parts = []
            for r in range(HPG):
                colb = jnp.broadcast_to(acs[:, r:r + 1], (CH, CH))
                row = acst_s[pl.ds(g * HPG + r, 1), :]
                lmat = jnp.exp(jnp.where(mask, colb - row, -jnp.inf))
                xr = xs[:, r * HD:(r + 1) * HD]
                xd = xr * dt[:, r:r + 1]
                sp = state[g * HPG + r]
                col = colb[:, 0:HD]
                alast = colb[CH - 1:CH, 0:HD]
                y_r = _dot((cbm * lmat).astype(bf16), xd.astype(bf16))
                y_r = y_r + jnp.exp(col) * _dot_nt(cg, sp.astype(bf16))
                y_parts.append(y_r + xr * dkc_g[:, r * HD:(r + 1) * HD])
                cs = _dot_tn((xd * jnp.exp(alast - col)).astype(bf16), bg)
                state[g * HPG + r] = jnp.exp(colb[CH - 1:CH, :]) * sp + cs
            y = jnp.concatenate(y_parts, axis=1)
            szv = sz_ref[:, gcols]
            yz = y * (szv * _sigmoid(szv))
            rr = lax.rsqrt(jnp.mean(yz * yz, axis=-1, keepdims=True) + EPS)
            yb_ref[:, gcols] = (yz * rr * snw_ref[:, gcols]).astype(bf16)
            y_ref[:, gcols] = y.astype(bf16)
            return carry

        lax.fori_loop(0, NG, group, 0)

    const2 = lambda c: (0, 0)
    return pl.pallas_call(
        body, grid=(nc,),
        in_specs=[pl.BlockSpec((CH, CONVD), lambda c: (c, 0)), pl.BlockSpec((CH, DIN), lambda c: (c, CONVD // DIN)),
                  pl.BlockSpec((CH, 256), lambda c: (c, (CONVD + DIN) // 256)),
                  pl.BlockSpec((4, CONVD), const2), pl.BlockSpec((1, CONVD), const2),
                  pl.BlockSpec((1, CH), const2), pl.BlockSpec((1, CH), const2),
                  pl.BlockSpec((1, DIN), const2), pl.BlockSpec((1, DIN), const2)],
        out_specs=[pl.BlockSpec((CH, DIN), lambda c: (c, 0)), pl.BlockSpec((CH, DIN), lambda c: (c, 0)),
                   pl.BlockSpec((1, NH, HD, NST), lambda c: (c, 0, 0, 0)), pl.BlockSpec((CH, CONVD), lambda c: (c, 0))],
        out_shape=[jax.ShapeDtypeStruct((s, DIN), bf16), jax.ShapeDtypeStruct((s, DIN), bf16),
                   jax.ShapeDtypeStruct((nc, NH, HD, NST), bf16), jax.ShapeDtypeStruct((s, CONVD), bf16)],
        scratch_shapes=[pltpu.VMEM((8, CONVD), f32), pltpu.VMEM((NH, HD, NST), f32), pltpu.VMEM((CH, CH), f32)],
        name=name, compiler_params=_cp(("arbitrary",)),
    )(proj_b, proj_b, proj_b, cw, cb, dtb, a_row, dkc, snw)


def _branch_b_bwd(proj_b, conv_sv, dyb, y_sv, states, cw, dtb, a_row, dkc, snw, ind, name):
    s = proj_b.shape[0]
    nc = s // CH

    def body(xbc_ref, cv_ref, sz_ref, dtr_ref, dyb_ref, y_ref, st_ref, cw_ref, dtb_ref, a_ref, dkc_ref,
             snw_ref, ind_ref, dp_ref, dcw_ref, dcb_ref, ddtb_ref, dal_ref, ddk_ref, dsnw_ref,
             dstate, dnext8, acst_s, dacs_acc, q2_acc):
        @pl.when(pl.program_id(0) == 0)
        def _():
            dstate[...] = jnp.zeros_like(dstate)
            dnext8[...] = jnp.zeros_like(dnext8)
            dcw_ref[...] = jnp.zeros_like(dcw_ref)
            dcb_ref[...] = jnp.zeros_like(dcb_ref)
            ddtb_ref[...] = jnp.zeros_like(ddtb_ref)
            dal_ref[...] = jnp.zeros_like(dal_ref)
            ddk_ref[...] = jnp.zeros_like(ddk_ref)
            dsnw_ref[...] = jnp.zeros_like(dsnw_ref)

        dacs_acc[...] = jnp.zeros_like(dacs_acc)
        q2_acc[...] = jnp.zeros_like(q2_acc)
        mask = _tril_mask()
        tri_t = (lax.broadcasted_iota(jnp.int32, (CH, CH), 0) <= lax.broadcasted_iota(jnp.int32, (CH, CH), 1)).astype(f32)
        lane1 = lax.broadcasted_iota(jnp.int32, (1, CH), 1)
        is_last = lax.broadcasted_iota(jnp.int32, (CH, 1), 0) == CH - 1
        z_all = dtr_ref[:, 0:CH] + dtb_ref[...]
        dt_all = _softplus(z_all)
        adt_all = dt_all * a_ref[...]
        acs_all = _dot_hi(mask.astype(f32), adt_all)
        acst_s[...] = acs_all.T

        def ind_sum(v):
            hi = v.astype(bf16)
            lo = (v - hi.astype(f32)).astype(bf16)
            return _dot(hi, ind_ref[...]) + _dot(lo, ind_ref[...])

        def group(g, carry):
            cols = _lanes(g, GB)
            gcols = _lanes(g, GW)
            conv = cv_ref[:, cols].astype(f32)
            sg = _sigmoid(conv)
            xc = conv * sg
            xs = xc[:, 0:GW]
            bg = xc[:, GW:GW + NST].astype(bf16)
            cg = xc[:, GW + NST:GB].astype(bf16)

            y = y_ref[:, gcols].astype(f32)
            silu_sz, dsilu_sz = _silu_and_grad(sz_ref[:, gcols])
            yz = y * silu_sz
            rr = lax.rsqrt(jnp.mean(yz * yz, axis=-1, keepdims=True) + EPS)
            dyb_g = dyb_ref[:, gcols].astype(f32)
            w = dyb_g * snw_ref[:, gcols]
            dsnw_ref[:, gcols] += jnp.sum(dyb_g * yz * rr, axis=0, keepdims=True)
            dyz = rr * w - yz * (rr * rr * rr) * jnp.mean(w * yz, axis=-1, keepdims=True)
            dp_ref[:, _lanes(g, GW, CONVD)] = (dyz * y * dsilu_sz).astype(bf16)
            dy_g = dyz * silu_sz
            ddk_ref[:, gcols] += jnp.sum(dy_g * xs, axis=0, keepdims=True)

            back = lax.rem(CH - HPG * g, CH)
            dt = pltpu.roll(dt_all, back, 1)
            acs = pltpu.roll(acs_all, back, 1)
            cbm = _dot_nt(cg, bg)
            d_cb = jnp.zeros((CH, CH), f32)
            d_bg = jnp.zeros((CH, NST), f32)
            d_cg = jnp.zeros((CH, NST), f32)
            lastrow = jnp.zeros((1, CH), f32)
            dxd_parts, dxs_parts, t_parts = [], [], []
            for r in range(HPG):
                h = g * HPG + r
                colb = jnp.broadcast_to(acs[:, r:r + 1], (CH, CH))
                row = acst_s[pl.ds(h, 1), :]
                lmat = jnp.exp(jnp.where(mask, colb - row, -jnp.inf))
                mmat_b = (cbm * lmat).astype(bf16)
                dtc = dt[:, r:r + 1]
                xd = xs[:, r * HD:(r + 1) * HD] * dtc
                col = colb[:, 0:HD]
                alast = colb[CH - 1:CH, 0:HD]
                dte = jnp.exp(alast - col)
                ea = jnp.exp(col)
                cd = jnp.exp(colb[CH - 1:CH, :])
                sp = st_ref[0, h]
                dsn = dstate[h]
                dsn_b = dsn.astype(bf16)
                dyr = dy_g[:, r * HD:(r + 1) * HD]
                dyr_b = dyr.astype(bf16)
                dye_b = (dyr * ea).astype(bf16)
                d_cg = d_cg + _dot(dye_b, sp)
                dxde = _dot_nt(bg, dsn_b)
                xdte = xd * dte
                xd_b = xd.astype(bf16)
                d_bg = d_bg + _dot(xdte.astype(bf16), dsn_b)
                dxd_diag = _dot_tn(mmat_b, dyr_b)
                dxd = dxde * dte + dxd_diag
                d_cb = d_cb + _dot_nt(dyr_b, xd_b) * lmat
                t_parts.append(dyr_b.astype(f32) * _dot(mmat_b, xd_b) + dyr * (ea * _dot_nt(cg, sp))
                               - xd_b.astype(f32) * dxd_diag - dxde * xdte)
                lastrow = jnp.where(lane1 == r, _sum_all(dsn * sp.astype(f32)) * cd + _sum_all(dxde * xdte), lastrow)
                dstate[h] = cd * dsn + _dot_tn(dye_b, cg)
                dxd_parts.append(dxd)
                dxs_parts.append(dxd * dtc)
            d_cb_b = d_cb.astype(bf16)
            d_bg = d_bg + _dot_tn(d_cb_b, cg)
            d_cg = d_cg + _dot(d_cb_b, bg)
            q2 = ind_sum(jnp.concatenate(dxd_parts, axis=1) * xs)
            dacs = ind_sum(jnp.concatenate(t_parts, axis=1)) + jnp.where(is_last, lastrow, 0.0)
            dacs_acc[...] += pltpu.roll(dacs, HPG * g, 1)
            q2_acc[...] += pltpu.roll(q2, HPG * g, 1)
            dxs = jnp.concatenate(dxs_parts, axis=1) + dy_g * dkc_ref[:, gcols]

            dconv = jnp.concatenate([dxs, d_bg, d_cg], axis=1) * (sg * (1.0 + conv * (1.0 - sg)))
            x = xbc_ref[:, cols]
            dcb_ref[:, cols] += jnp.sum(dconv, axis=0, keepdims=True)
            dcw_ref[3:4, cols] += jnp.sum(dconv * x, axis=0, keepdims=True)
            dx = cw_ref[3:4, cols] * dconv
            nxt = dnext8[:, cols]
            for j in (1, 2, 3):
                up = _shift_rows_up(dconv, nxt, j)
                dcw_ref[3 - j:4 - j, cols] += jnp.sum(up * x, axis=0, keepdims=True)
                dx = dx + cw_ref[3 - j:4 - j, cols] * up
            dnext8[:, cols] = dconv[0:8]
            dp_ref[:, cols] = dx.astype(bf16)
            return carry

        lax.fori_loop(0, NG, group, 0)

        dadt = _dot_hi(tri_t, dacs_acc[...])
        dal_ref[...] += jnp.sum(dadt * adt_all, axis=0, keepdims=True)
        ddz = (dadt * a_ref[...] + q2_acc[...]) * _sigmoid(z_all)
        ddtb_ref[...] += jnp.sum(ddz, axis=0, keepdims=True)
        dp_ref[:, CONVD + DIN:CONVD + DIN + CH] = ddz.astype(bf16)
        dp_ref[:, CONVD + DIN + CH:PB] = jnp.zeros((CH, PB - CONVD - DIN - CH), bf16)

    const2 = lambda c: (0, 0)
    rev = lambda c: (nc - 1 - c, 0)
    return pl.pallas_call(
        body, grid=(nc,),
        in_specs=[pl.BlockSpec((CH, CONVD), rev), pl.BlockSpec((CH, CONVD), rev),
                  pl.BlockSpec((CH, DIN), lambda c: (nc - 1 - c, CONVD // DIN)),
                  pl.BlockSpec((CH, 256), lambda c: (nc - 1 - c, (CONVD + DIN) // 256)),
                  pl.BlockSpec((CH, DIN), rev), pl.BlockSpec((CH, DIN), rev),
                  pl.BlockSpec((1, NH, HD, NST), lambda c: (nc - 1 - c, 0, 0, 0)),
                  pl.BlockSpec((4, CONVD), const2), pl.BlockSpec((1, CH), const2), pl.BlockSpec((1, CH), const2),
                  pl.BlockSpec((1, DIN), const2), pl.BlockSpec((1, DIN), const2), pl.BlockSpec((GW, CH), const2)],
        out_specs=[pl.BlockSpec((CH, PB), rev), pl.BlockSpec((4, CONVD), const2), pl.BlockSpec((1, CONVD), const2),
                   pl.BlockSpec((1, CH), const2), pl.BlockSpec((1, CH), const2), pl.BlockSpec((1, DIN), const2),
                   pl.BlockSpec((1, DIN), const2)],
        out_shape=[jax.ShapeDtypeStruct((s, PB), bf16), jax.ShapeDtypeStruct((4, CONVD), f32),
                   jax.ShapeDtypeStruct((1, CONVD), f32), jax.ShapeDtypeStruct((1, CH), f32),
                   jax.ShapeDtypeStruct((1, CH), f32), jax.ShapeDtypeStruct((1, DIN), f32),
                   jax.ShapeDtypeStruct((1, DIN), f32)],
        scratch_shapes=[pltpu.VMEM((NH, HD, NST), f32), pltpu.VMEM((8, CONVD), f32), pltpu.VMEM((CH, CH), f32),
                        pltpu.VMEM((CH, CH), f32), pltpu.VMEM((CH, CH), f32)],
        name=name, compiler_params=_cp(("arbitrary",)),
    )(proj_b, conv_sv, proj_b, proj_b, dyb, y_sv, states, cw, dtb, a_row, dkc, snw, ind)


def _merge_fwd(ya, yb, proj_g, x, gate, wa, wb, wo, name):
    s = x.shape[0]
    ts = min(512, s)

    def body(ya_ref, yb_ref, ga_ref, gb_ref, x_ref, gate_ref, wa_ref, wb_ref, wo_ref, xo_ref, pa_ref, pb_ref, mg_ref, o_ref):
        pa = _dot(ya_ref[...], wa_ref[...])
        pb = _dot(yb_ref[...], wb_ref[...])
        mg = (_sigmoid(ga_ref[...]) * pa + _sigmoid(gb_ref[...]) * pb).astype(bf16)
        o = _dot(mg, wo_ref[...])
        xo_ref[...] = x_ref[...] + gate_ref[...] * o
        pa_ref[...] = pa.astype(bf16)
        pb_ref[...] = pb.astype(bf16)
        mg_ref[...] = mg
        o_ref[...] = o.astype(bf16)

    tile = pl.BlockSpec((ts, D), lambda i: (i, 0))
    const = lambda i: (0, 0)
    act = jax.ShapeDtypeStruct((s, D), bf16)
    return pl.pallas_call(
        body, grid=(s // ts,),
        in_specs=[tile, pl.BlockSpec((ts, DIN), lambda i: (i, 0)), tile, pl.BlockSpec((ts, D), lambda i: (i, 1)), tile,
                  pl.BlockSpec((1, D), const), pl.BlockSpec((D, D), const), pl.BlockSpec((DIN, D), const),
                  pl.BlockSpec((D, D), const)],
        out_specs=[tile, tile, tile, tile, tile],
        out_shape=[jax.ShapeDtypeStruct((s, D), f32), act, act, act, act],
        name=name, compiler_params=_cp(("parallel",)),
    )(ya, yb, proj_g, proj_g, x, gate, wa, wb, wo)


def _merge_bwd(dxo, gate, o_sv, pa_sv, pb_sv, proj_g, wo, wa, wb, name):
    s = dxo.shape[0]
    ts = min(512, s)

    def body(dxo_ref, gate_ref, o_ref, pa_ref, pb_ref, ga_ref, gb_ref, wo_ref, wa_ref, wb_ref,
             do_ref, dpa_ref, dpb_ref, dg_ref, dya_ref, dyb_ref, dgate_ref):
        @pl.when(pl.program_id(0) == 0)
        def _():
            dgate_ref[...] = jnp.zeros_like(dgate_ref)
        dxo_v = dxo_ref[...]
        dgate_ref[...] += jnp.sum(dxo_v * o_ref[...].astype(f32), axis=0, keepdims=True)
        do = (dxo_v * gate_ref[...]).astype(bf16)
        do_ref[...] = do
        dmg = _dot_nt(do, wo_ref[...])
        sa = _sigmoid(ga_ref[...])
        sb = _sigmoid(gb_ref[...])
        dpa = (dmg * sa).astype(bf16)
        dpb = (dmg * sb).astype(bf16)
        dpa_ref[...] = dpa
        dpb_ref[...] = dpb
        dg_ref[:, 0:D] = (dmg * pa_ref[...].astype(f32) * sa * (1.0 - sa)).astype(bf16)
        dg_ref[:, D:2 * D] = (dmg * pb_ref[...].astype(f32) * sb * (1.0 - sb)).astype(bf16)
        dya_ref[...] = _dot_nt(dpa, wa_ref[...]).astype(bf16)
        dyb_ref[...] = _dot_nt(dpb, wb_ref[...]).astype(bf16)

    tile = pl.BlockSpec((ts, D), lambda i: (i, 0))
    const = lambda i: (0, 0)
    act = jax.ShapeDtypeStruct((s, D), bf16)
    return pl.pallas_call(
        body, grid=(s // ts,),
        in_specs=[tile, pl.BlockSpec((1, D), const), tile, tile, tile, tile, pl.BlockSpec((ts, D), lambda i: (i, 1)),
                  pl.BlockSpec((D, D), const), pl.BlockSpec((D, D), const), pl.BlockSpec((DIN, D), const)],
        out_specs=[tile, tile, tile, pl.BlockSpec((ts, PG), lambda i: (i, 0)), tile, pl.BlockSpec((ts, DIN), lambda i: (i, 0)),
                   pl.BlockSpec((1, D), const)],
        out_shape=[act, act, act, jax.ShapeDtypeStruct((s, PG), bf16), act, jax.ShapeDtypeStruct((s, DIN), bf16),
                   jax.ShapeDtypeStruct((1, D), f32)],
        name=name, compiler_params=_cp(("arbitrary",)),
    )(dxo, gate, o_sv, pa_sv, pb_sv, proj_g, proj_g, wo, wa, wb)


def _final_loss(x, target, fnw, name):
    s = x.shape[0]
    ts = min(512, s)

    def body(x_ref, t_ref, w_ref, loss_ref, dx_ref, dw_ref):
        @pl.when(pl.program_id(0) == 0)
        def _():
            loss_ref[...] = jnp.zeros_like(loss_ref)
            dw_ref[...] = jnp.zeros_like(dw_ref)
        xv = x_ref[...]
        r = lax.rsqrt(jnp.mean(xv * xv, axis=-1, keepdims=True) + EPS)
        xn = xv * r
        err = xn * w_ref[...] - t_ref[...]
        part = jnp.sum(err * err, axis=0, keepdims=True)
        acc = part[:, 0:128]
        for k in range(1, D // 128):
            acc = acc + part[:, k * 128:(k + 1) * 128]
        loss_ref[0:1, :] += acc * (0.5 / D)
        dy = err * (1.0 / D)
        dw_ref[...] += jnp.sum(dy * xn, axis=0, keepdims=True)
        dxn = dy * w_ref[...]
        dx_ref[...] = r * (dxn - xn * jnp.mean(dxn * xn, axis=-1, keepdims=True))

    tile = pl.BlockSpec((ts, D), lambda i: (i, 0))
    row = pl.BlockSpec((1, D), lambda i: (0, 0))
    return pl.pallas_call(
        body, grid=(s // ts,), in_specs=[tile, tile, row],
        out_specs=[pl.BlockSpec((8, 128), lambda i: (0, 0)), tile, row],
        out_shape=[jax.ShapeDtypeStruct((8, 128), f32), jax.ShapeDtypeStruct((s, D), f32), jax.ShapeDtypeStruct((1, D), f32)],
        name=name, compiler_params=_cp(("arbitrary",)),
    )(x, target, fnw)


def _layer_operands(w_in, conv_w, wa, wb, wo, norm_w, gm_ln_w, gm_ln_b, gm_ws, gm_bs, conv_b, dt_bias, a_log, d_skip, ssm_norm_w):
    sz0, x0 = PA, PA + DIN
    b0, c0, dt0 = x0 + DIN, x0 + DIN + NG * NST, x0 + CONVD
    w_xbc = _group_major(w_in[:, x0:b0], w_in[:, b0:c0], w_in[:, c0:dt0])
    w_b = jnp.concatenate([w_xbc, w_in[:, sz0:x0], w_in[:, dt0:dt0 + NH], jnp.zeros((D, PB - CONVD - DIN - NH), bf16)], axis=1)
    w_a = w_in[:, 0:PA]
    w_g = w_in[:, dt0 + NH:N_IN]
    tril = jnp.tril(jnp.ones((CH, CH), bool))
    wsm = jnp.where(tril[None], gm_ws, 0.0).astype(bf16)

    def heads_row(v):
        return jnp.pad(v, (0, CH - NH)).reshape(1, CH)

    return dict(
        w_b=w_b, w_a=w_a, w_g=w_g, wa=wa, wb=wb, wo=wo,
        norm_w=norm_w.reshape(1, D), lnw=gm_ln_w.reshape(1, D), lnb=gm_ln_b.reshape(1, D),
        wsm=wsm, wsm_t=jnp.swapaxes(wsm, 1, 2), bsf=jnp.repeat(gm_bs.T, CH, axis=1),
        cw=_group_major(conv_w[:, 0:DIN], conv_w[:, DIN:DIN + NG * NST], conv_w[:, DIN + NG * NST:CONVD]),
        cb=_group_major(conv_b[0:DIN], conv_b[DIN:DIN + NG * NST], conv_b[DIN + NG * NST:CONVD]).reshape(1, CONVD),
        dtb=heads_row(dt_bias), a_row=heads_row(-jnp.exp(a_log)),
        snw=ssm_norm_w.reshape(1, DIN), dkc=jnp.repeat(d_skip, HD).reshape(1, DIN),
        ind=(jnp.arange(GW)[:, None] // HD == jnp.arange(CH)[None, :]).astype(bf16),
    )


def _layer_fwd(x, shift, scale, gate, p, tag):
    h = _rmsmod_fwd(x, p["norm_w"], scale, shift, f"rmsmod_fwd{tag}")
    proj_b = _mm(h, p["w_b"], f32, f"proj_b{tag}")
    proj_a = _mm(h, p["w_a"], f32, f"proj_a{tag}")
    proj_g = _mm(h, p["w_g"], f32, f"proj_g{tag}")
    ya = _branch_a_fwd(proj_a, p["lnw"], p["lnb"], p["wsm"], p["bsf"], f"branch_a_fwd{tag}")
    yb, y_sv, states, conv_sv = _branch_b_fwd(proj_b, p["cw"], p["cb"], p["dtb"], p["a_row"], p["dkc"], p["snw"], f"branch_b_fwd{tag}")
    if "merge_weights" in p:
        p["wa"], p["wb"], p["wo"], yb = p.pop("merge_weights")(yb)
    x_out, pa, pb, mg, o = _merge_fwd(ya, yb, proj_g, x, gate, p["wa"], p["wb"], p["wo"], f"merge_fwd{tag}")
    saved = dict(x=x, h=h, proj_b=proj_b, proj_a=proj_a, proj_g=proj_g, ya=ya, yb=yb, y=y_sv, states=states, conv=conv_sv,
                 pa=pa, pb=pb, mg=mg, o=o, scale=scale, gate=gate)
    return x_out, saved


def _layer_bwd(dxo, sv, p, tag):
    do, dpa, dpb, dg, dya, dyb, dgate = _merge_bwd(dxo, sv["gate"], sv["o"], sv["pa"], sv["pb"], sv["proj_g"],
                                                   p["wo"], p["wa"], p["wb"], f"merge_bwd{tag}")
    d_wo = _mm_tn(sv["mg"], do, f"d_wo{tag}")
    d_wa = _mm_tn(sv["ya"], dpa, f"d_wa{tag}")
    d_wb = _mm_tn(sv["yb"], dpb, f"d_wb{tag}")
    da, dws, dbs, dlnw, dlnb = _branch_a_bwd(sv["proj_a"], dya, p["lnw"], p["lnb"], p["wsm"], p["wsm_t"], p["bsf"],
                                             f"branch_a_bwd{tag}")
    db, dcw, dcb, ddtb, dal, ddk, dsnw = _branch_b_bwd(sv["proj_b"], sv["conv"], dyb, sv["y"], sv["states"], p["cw"], p["dtb"],
                                                       p["a_row"], p["dkc"], p["snw"], p["ind"], f"branch_b_bwd{tag}")
    dh = _mm(db, p["w_b"], f32, f"dh_b{tag}", trans_b=True)
    dh = _mm(da, p["w_a"], f32, f"dh_a{tag}", c_in=dh, trans_b=True)
    dh = _mm(dg, p["w_g"], f32, f"dh_g{tag}", c_in=dh, trans_b=True)
    d_w_b_t = _mm_tn(db, sv["h"], f"d_w_b{tag}")
    d_w_a_t = _mm_tn(da, sv["h"], f"d_w_a{tag}")
    d_w_g_t = _mm_tn(dg, sv["h"], f"d_w_g{tag}")
    dx, dscale, dshift, dnw = _rmsmod_bwd(dh, sv["x"], dxo, p["norm_w"], sv["scale"], f"rmsmod_bwd{tag}")
    d_w_in_t = jnp.concatenate([d_w_a_t, d_w_b_t[CONVD:CONVD + DIN], _rows_from_group_major(d_w_b_t[0:CONVD]),
                                d_w_b_t[CONVD + DIN:CONVD + DIN + NH], d_w_g_t], axis=0)
    tril = jnp.tril(jnp.ones((CH, CH), bool))
    heads = lambda v: v[0, 0:NH]
    grads = dict(
        w_in_t=d_w_in_t, w_proj_a=d_wa, w_proj_b=d_wb, w_out=d_wo, conv_w=_from_group_major(dcw), conv_b=_from_group_major(dcb).reshape(CONVD),
        norm_w=dnw.reshape(D), gm_ln_w=dlnw.reshape(D), gm_ln_b=dlnb.reshape(D),
        gm_ws=jnp.where(tril[None], dws, 0.0), gm_bs=dbs.reshape(CH, NG, CH).sum(-1).T,
        dt_bias=heads(ddtb), a_log=heads(dal), d_skip=ddk.reshape(NH, HD).sum(-1), ssm_norm_w=dsnw.reshape(DIN),
        mod=jnp.concatenate([dshift, dscale, dgate], axis=1).reshape(3 * D),
    )
    return dx, grads


def _local_step(x, target, mods, operands_of, fnw):
    saved, layer_ops = [], []
    for l in range(DEPTH):
        shift, scale, gate = mods[l]
        p, x = operands_of(l, x)
        layer_ops.append(p)
        x, sv = _layer_fwd(x, shift, scale, gate, p, f"_l{l}")
        saved.append(sv)
    loss_parts, dx, dfnw = _final_loss(x, target, fnw.reshape(1, D), "final_loss")
    grads = [None] * DEPTH
    for l in reversed(range(DEPTH)):
        dx, grads[l] = _layer_bwd(dx, saved[l], layer_ops[l], f"_l{l}")
    return loss_parts, dx, grads, dfnw.reshape(D)


ADA_COLS = 3 * D // NSHARD


def _ada_fwd(c_all, ada_w, ada_b_cols, name):
    def body(c_ref, w_ref, b_ref, o_ref):
        cv = c_ref[...]
        sc = cv * _sigmoid(cv)
        for l in range(DEPTH):
            o_ref[l] = _dot_hi(sc, w_ref[l]) + b_ref[l]

    return pl.pallas_call(body, out_shape=jax.ShapeDtypeStruct((DEPTH, 8, ADA_COLS), f32), name=name,
                          compiler_params=_cp(None))(c_all, ada_w, ada_b_cols)


def _adam_math(w, g, m, v):
    m = ADAM_B1 * m + (1.0 - ADAM_B1) * g
    v = ADAM_B2 * v + (1.0 - ADAM_B2) * (g * g)
    m_hat = m / (1.0 - ADAM_B1 ** ADAM_STEP)
    v_hat = v / (1.0 - ADAM_B2 ** ADAM_STEP)
    delta = -ADAM_LR * (m_hat / (jnp.sqrt(v_hat) + ADAM_EPS) + ADAM_WD * w)
    return delta, m, v


def _ada_bwd_adamw(c_all, dmod_cols, w, m, v, name):
    tr = 256

    def body(c_ref, dm_ref, w_ref, m_ref, v_ref, g_ref, d_ref, nm_ref, nv_ref):
        cv = c_ref[...]
        sc = cv * _sigmoid(cv)
        g = lax.dot_general(sc, dm_ref[0], (((0,), (0,)), ((), ())), precision=_HI, preferred_element_type=f32)
        g_ref[0] = g
        d_ref[0], nm_ref[0], nv_ref[0] = _adam_math(w_ref[0], g, m_ref[0], v_ref[0])

    blk = pl.BlockSpec((1, tr, ADA_COLS), lambda l, i: (l, i, 0))
    shp = jax.ShapeDtypeStruct((DEPTH, D, ADA_COLS), f32)
    return pl.pallas_call(
        body, grid=(DEPTH, D // tr),
        in_specs=[pl.BlockSpec((8, tr), lambda l, i: (0, i)), pl.BlockSpec((1, 8, ADA_COLS), lambda l, i: (l, 0, 0)), blk, blk, blk],
        out_specs=[blk, blk, blk, blk], out_shape=[shp, shp, shp, shp], name=name, compiler_params=_cp(("parallel", "parallel")),
    )(c_all, dmod_cols, w, m, v)


def _adamw(w, g, m, v, name):
    def body(w_ref, g_ref, m_ref, v_ref, d_ref, nm_ref, nv_ref):
        d_ref[...], nm_ref[...], nv_ref[...] = _adam_math(w_ref[...], g_ref[...], m_ref[...], v_ref[...])

    shp = jax.ShapeDtypeStruct(w.shape, f32)
    return pl.pallas_call(body, out_shape=[shp] * 3, name=name, compiler_params=_cp(None))(w, g, m, v)


def _tile2(r, c):
    if r <= 256 or r % 256 == 0:
        return _pick(r, (256,)), _pick(c, (1024,))
    return r, 128


def _adamw_layers(w, g_mine, g_other, m, v, c_idx, name):
    _, r, c = w.shape
    tr, tc = _tile2(r, c)

    def body(ci_ref, w_ref, gm_ref, go_ref, m_ref, v_ref, g_ref, d_ref, nm_ref, nv_ref):
        def update(g):
            g_ref[0] = g
            d_ref[0], nm_ref[0], nv_ref[0] = _adam_math(w_ref[0], g, m_ref[0], v_ref[0])

        mine = pl.program_id(0) == ci_ref[0]

        @pl.when(mine)
        def _():
            update(gm_ref[...])

        @pl.when(jnp.logical_not(mine))
        def _():
            update(go_ref[...])

    blk = pl.BlockSpec((1, tr, tc), lambda l, i, j, ci: (l, i, j))
    gblk = pl.BlockSpec((tr, tc), lambda l, i, j, ci: (i, j))
    shp = jax.ShapeDtypeStruct(w.shape, f32)
    return pl.pallas_call(
        body,
        grid_spec=pltpu.PrefetchScalarGridSpec(num_scalar_prefetch=1, grid=(DEPTH, r // tr, c // tc),
                                               in_specs=[blk, gblk, gblk, blk, blk], out_specs=[blk, blk, blk, blk]),
        out_shape=[shp, shp, shp, shp], name=name, compiler_params=_cp(("parallel", "parallel", "parallel")),
    )(c_idx, w, g_mine, g_other, m, v)


_MESH = pl.DeviceIdType.MESH
_AXES = ("x", "y", "c")
_HBM = pl.BlockSpec(memory_space=pltpu.HBM)


def _my_place():
    return tuple(lax.axis_index(a) for a in _AXES)


def _allreduce(buf, axes, name):
    r = buf.shape[0]
    n = len(axes)

    def body(x_ref, o_ref, rbuf, ssem, rsem):
        me = dict(zip(_AXES, _my_place()))
        o_ref[...] = x_ref[...]
        for k, ax in enumerate(axes):
            peer = tuple(1 - me[a] if a == ax else me[a] for a in _AXES)
            cp = pltpu.make_async_remote_copy(src_ref=o_ref, dst_ref=rbuf.at[k], send_sem=ssem.at[k], recv_sem=rsem.at[k],
                                              device_id=peer, device_id_type=_MESH)
            cp.start()
            cp.wait()
            o_ref[...] = o_ref[...] + rbuf[k]

    vm = pl.BlockSpec(memory_space=pltpu.VMEM)
    return pl.pallas_call(
        body, out_shape=jax.ShapeDtypeStruct((r, 128), f32), in_specs=[vm], out_specs=vm,
        scratch_shapes=[pltpu.VMEM((n, r, 128), f32), pltpu.SemaphoreType.DMA((n,)), pltpu.SemaphoreType.DMA((n,))],
        name=name, compiler_params=pltpu.CompilerParams(vmem_limit_bytes=VMEM_LIMIT),
    )(buf)


def _allreduce_halves(buf, name):
    r = buf.shape[0]
    h = r // 2
    assert r % 16 == 0

    def body(x_ref, o_ref, rbuf, ssem, rsem):
        x, y, c = _my_place()
        mine = pl.ds(pl.multiple_of(c * h, 8), h)
        other = pl.ds(pl.multiple_of((1 - c) * h, 8), h)
        o_ref[...] = x_ref[...]
        steps = [(other, (x, y, 1 - c)), (mine, (1 - x, y, c)), (mine, (x, 1 - y, c))]
        for k, (rows, peer) in enumerate(steps):
            cp = pltpu.make_async_remote_copy(src_ref=o_ref.at[rows], dst_ref=rbuf.at[k], send_sem=ssem.at[k], recv_sem=rsem.at[k],
                                              device_id=peer, device_id_type=_MESH)
            cp.start()
            cp.wait()
            o_ref[mine, :] = o_ref[mine, :] + rbuf[k]
        cp = pltpu.make_async_remote_copy(src_ref=o_ref.at[mine], dst_ref=rbuf.at[3], send_sem=ssem.at[3], recv_sem=rsem.at[3],
                                          device_id=(x, y, 1 - c), device_id_type=_MESH)
        cp.start()
        cp.wait()
        o_ref[other, :] = rbuf[3]

    vm = pl.BlockSpec(memory_space=pltpu.VMEM)
    return pl.pallas_call(
        body, out_shape=jax.ShapeDtypeStruct((r, 128), f32), in_specs=[vm], out_specs=vm,
        scratch_shapes=[pltpu.VMEM((4, h, 128), f32), pltpu.SemaphoreType.DMA((4,)), pltpu.SemaphoreType.DMA((4,))],
        name=name, compiler_params=pltpu.CompilerParams(vmem_limit_bytes=VMEM_LIMIT),
    )(buf)


def _other_chips(x, y):
    return [(1 - x, y), (x, 1 - y), (1 - x, 1 - y)]


def _gather_body(ins, outs, ssem, rsem):
    na = len(ins)
    x, y, c = _my_place()
    k_me = 2 * x + y
    sibling = (x, y, 1 - c)
    chips = _other_chips(x, y)
    slots = [2 * cx + cy for cx, cy in chips]
    half = [r.shape[0] // 2 if r.shape[0] % 32 == 0 else None for r in ins]

    def part(ref, a, core):
        return ref if half[a] is None else ref.at[pl.ds(core * half[a], half[a])]

    def rcopy(a, src, slot, core, to, idx):
        return pltpu.make_async_remote_copy(src_ref=src, dst_ref=part(outs[a].at[slot], a, core), send_sem=ssem.at[idx],
                                            recv_sem=rsem.at[idx], device_id=to, device_id_type=_MESH)

    sent = []
    for j, chip in enumerate(chips):
        for a in range(na):
            cp = rcopy(a, part(ins[a], a, c), k_me, c, (*chip, c), j * na + a)
            cp.start()
            sent.append(cp)
    for j, chip in enumerate(chips):
        for a in range(na):
            rcopy(a, part(ins[a], a, c), slots[j], c, (*chip, c), j * na + a).wait_recv()
            if half[a] is not None:
                cp = rcopy(a, part(outs[a].at[slots[j]], a, c), slots[j], c, sibling, (3 + j) * na + a)
                cp.start()
                sent.append(cp)
    for j in range(3):
        for a in range(na):
            if half[a] is not None:
                rcopy(a, part(ins[a], a, c), slots[j], 1 - c, sibling, (3 + j) * na + a).wait_recv()
    for cp in sent:
        cp.wait_send()


def _gather_layer_behind(shards, name, collective_id):
    na = len(shards)
    hbm = pltpu.MemorySpace.HBM
    ins = [jax.new_ref(s, memory_space=hbm) for s in shards]
    outs = [jax.empty_ref(jax.ShapeDtypeStruct((NSHARD,) + s.shape, s.dtype), memory_space=hbm) for s in shards]

    @pl.kernel(mesh=plsc.ScalarSubcoreMesh(axis_name="sequencer", num_cores=1), name=name,
               scratch_types=(pltpu.SemaphoreType.DMA((6 * na,)), pltpu.SemaphoreType.DMA((6 * na,))),
               compiler_params=pltpu.CompilerParams(collective_id=collective_id))
    def launch(ssem, rsem):
        x, y, c = _my_place()
        barrier = pltpu.get_barrier_semaphore()
        peers = [(*chip, c) for chip in _other_chips(x, y)] + [(x, y, 1 - c)]
        for peer in peers:
            pl.semaphore_signal(barrier, inc=1, device_id=peer, device_id_type=_MESH)
        pl.semaphore_wait(barrier, len(peers))
        _gather_body(ins, outs, ssem, rsem)

    launch()
    return [o[...] for o in outs]


def _with_own(gathered, own):
    xi, yi, _ = _my_place()
    whole = lax.dynamic_update_index_in_dim(gathered, own, 2 * xi + yi, 0)
    return [whole[k] for k in range(NSHARD)]


def _swap_layers(parts, name):
    na = len(parts)

    def body(*refs):
        ins, outs = refs[:2 * na], refs[2 * na:3 * na]
        ssem, rsem = refs[3 * na:]
        x, y, c = _my_place()

        def copy(a, layer):
            return pltpu.make_async_remote_copy(src_ref=ins[2 * a + layer], dst_ref=outs[a], send_sem=ssem.at[a], recv_sem=rsem.at[a],
                                                device_id=(x, y, 1 - c), device_id_type=_MESH)

        for layer in range(DEPTH):
            @pl.when(c == 1 - layer)
            def _():
                for a in range(na):
                    copy(a, layer).start()
        for a in range(na):
            copy(a, 0).wait()

    flat = [p for pair in parts for p in pair]
    return pl.pallas_call(
        body, out_shape=[jax.ShapeDtypeStruct(p0.shape, p0.dtype) for p0, _ in parts], in_specs=[_HBM] * (2 * na),
        out_specs=[_HBM] * na, scratch_shapes=[pltpu.SemaphoreType.DMA((na,)), pltpu.SemaphoreType.DMA((na,))], name=name,
    )(*flat)


def _scatter_shards_behind(sums, name, collective_id):
    na = len(sums)
    hbm = pltpu.MemorySpace.HBM
    ins = [jax.new_ref(p, memory_space=hbm) for p in sums]
    outs = [jax.empty_ref(jax.ShapeDtypeStruct((3,) + p.shape[1:], p.dtype), memory_space=hbm) for p in sums]

    @pl.kernel(mesh=plsc.ScalarSubcoreMesh(axis_name="sequencer", num_cores=1), name=name,
               scratch_types=(pltpu.SemaphoreType.DMA((3 * na,)), pltpu.SemaphoreType.DMA((3 * na,))),
               compiler_params=pltpu.CompilerParams(collective_id=collective_id))
    def launch(ssem, rsem):
        x, y, c = _my_place()
        barrier = pltpu.get_barrier_semaphore()
        chips = _other_chips(x, y)
        for chip in chips:
            pl.semaphore_signal(barrier, inc=1, device_id=(*chip, c), device_id_type=_MESH)
        pl.semaphore_wait(barrier, len(chips))
        cps = []
        for j, chip in enumerate(chips):
            kj = 2 * chip[0] + chip[1]
            for a in range(na):
                cps.append(pltpu.make_async_remote_copy(
                    src_ref=ins[a].at[kj], dst_ref=outs[a].at[j], send_sem=ssem.at[j * na + a], recv_sem=rsem.at[j * na + a],
                    device_id=(*chip, c), device_id_type=_MESH))
        for cp in cps:
            cp.start()
        for cp in cps:
            cp.wait()

    launch()
    return [o[...] for o in outs]


def _share_layers(finals, name):
    na = len(finals)

    def body(*refs):
        ins, outs = refs[:na], refs[na:2 * na]
        ssem, rsem = refs[2 * na:]
        x, y, c = _my_place()
        cps = [pltpu.make_async_remote_copy(src_ref=ins[a], dst_ref=outs[a], send_sem=ssem.at[a], recv_sem=rsem.at[a],
                                            device_id=(x, y, 1 - c), device_id_type=_MESH) for a in range(na)]
        for cp in cps:
            cp.start()
        for cp in cps:
            cp.wait()

    return pl.pallas_call(
        body, out_shape=[jax.ShapeDtypeStruct(p.shape, p.dtype) for p in finals], in_specs=[_HBM] * na, out_specs=[_HBM] * na,
        scratch_shapes=[pltpu.SemaphoreType.DMA((na,)), pltpu.SemaphoreType.DMA((na,))], name=name,
    )(*finals)


def _add_own_layer(part0, part1, recv, c_idx, name):
    ns, r, c = recv.shape
    tr, tc = _tile2(r, c)

    def body(ci_ref, p0_ref, p1_ref, r_ref, o_ref, ob_ref):
        def add(p_ref):
            t = p_ref[...] + r_ref[...]
            o_ref[...] = t
            ob_ref[...] = t.astype(bf16)

        @pl.when(ci_ref[0] == 0)
        def _():
            add(p0_ref)

        @pl.when(ci_ref[0] == 1)
        def _():
            add(p1_ref)

    blk = pl.BlockSpec((1, tr, tc), lambda k, i, j, ci: (k, i, j))
    blk0 = pl.BlockSpec((1, tr, tc), lambda k, i, j, ci: (k * (1 - ci[0]), i * (1 - ci[0]), j * (1 - ci[0])))
    blk1 = pl.BlockSpec((1, tr, tc), lambda k, i, j, ci: (k * ci[0], i * ci[0], j * ci[0]))
    return pl.pallas_call(
        body,
        grid_spec=pltpu.PrefetchScalarGridSpec(num_scalar_prefetch=1, grid=(ns, r // tr, c // tc), in_specs=[blk0, blk1, blk],
                                               out_specs=[blk, blk]),
        out_shape=[jax.ShapeDtypeStruct((ns, r, c), f32), jax.ShapeDtypeStruct((ns, r, c), bf16)], name=name,
        compiler_params=_cp(("arbitrary", "arbitrary", "arbitrary")),
    )(c_idx, part0, part1, recv)


def _add_own_shard(sums, recv, k_idx, name):
    _, r, c = sums.shape
    tr, tc = _tile2(r, c)

    def body(ki_ref, s_ref, r_ref, o_ref):
        o_ref[...] = ((s_ref[0] + r_ref[0].astype(f32)) + r_ref[1].astype(f32)) + r_ref[2].astype(f32)

    return pl.pallas_call(
        body,
        grid_spec=pltpu.PrefetchScalarGridSpec(
            num_scalar_prefetch=1, grid=(r // tr, c // tc),
            in_specs=[pl.BlockSpec((1, tr, tc), lambda i, j, ki: (ki[0], i, j)), pl.BlockSpec((3, tr, tc), lambda i, j, ki: (0, i, j))],
            out_specs=pl.BlockSpec((tr, tc), lambda i, j, ki: (i, j))),
        out_shape=jax.ShapeDtypeStruct((r, c), f32), name=name, compiler_params=_cp(("parallel", "parallel")),
    )(k_idx, sums, recv)


def _reduce_scatter(parts, tag, meanwhile):
    x, y, c = _my_place()
    c_idx = jnp.reshape(c, (1,)).astype(jnp.int32)
    k_idx = jnp.reshape(2 * x + y, (1,)).astype(jnp.int32)
    na = len(parts)
    recv = _swap_layers(parts, f"rs_swap{tag}")
    sums = [_add_own_layer(parts[a][0], parts[a][1], recv[a], c_idx, f"rs_add_layer{tag}_{a}") for a in range(na)]
    recv = _scatter_shards_behind([sb for _, sb in sums], f"rs_scatter{tag}", 3)
    recv, side = lax.optimization_barrier((recv, meanwhile()))
    finals = [_add_own_shard(sums[a][0], recv[a], k_idx, f"rs_add_shard{tag}_{a}") for a in range(na)]
    return finals, _share_layers(finals, f"rs_share{tag}"), c_idx, side


_SMALL = [("ada_b", (DEPTH, 3 * D)), ("norm_w", (DEPTH, D)), ("gm_ln_w", (DEPTH, D)), ("gm_ln_b", (DEPTH, D)),
          ("gm_ws", (DEPTH, NG, CH, CH)), ("gm_bs", (DEPTH, NG, CH)), ("conv_b", (DEPTH, CONVD)), ("dt_bias", (DEPTH, NH)),
          ("a_log", (DEPTH, NH)), ("d_skip", (DEPTH, NH)), ("ssm_norm_w", (DEPTH, DIN)), ("final_norm_w", (D,))]


def _rows_of(shape):
    n = 1
    for d in shape:
        n *= d
    return -(-n // 1024) * 8


def _pack(arrays):
    rows = []
    for a in arrays:
        flat = a.reshape(-1)
        r = _rows_of(a.shape)
        rows.append(jnp.pad(flat, (0, r * 128 - flat.shape[0])).reshape(r, 128))
    return jnp.concatenate(rows, axis=0)


def _unpack(buf, shapes):
    out, at = [], 0
    for shp in shapes:
        r = _rows_of(shp)
        n = 1
        for d in shp:
            n *= d
        out.append(buf[at:at + r].reshape(-1)[:n].reshape(shp))
        at += r
    return out


def kernel(x, c, ada_w, ada_b, norm_w, w_in, gm_ln_w, gm_ln_b, gm_ws, gm_bs, conv_w, conv_b, dt_bias, a_log, d_skip, ssm_norm_w, w_proj_a, w_proj_b, w_out, final_norm_w, loss_target, m_ada_w, m_ada_b, m_norm_w, m_w_in, m_gm_ln_w, m_gm_ln_b, m_gm_ws, m_gm_bs, m_conv_w, m_conv_b, m_dt_bias, m_a_log, m_d_skip, m_ssm_norm_w, m_w_proj_a, m_w_proj_b, m_w_out, m_final_norm_w, v_ada_w, v_ada_b, v_norm_w, v_w_in, v_gm_ln_w, v_gm_ln_b, v_gm_ws, v_gm_bs, v_conv_w, v_conv_b, v_dt_bias, v_a_log, v_d_skip, v_ssm_norm_w, v_w_proj_a, v_w_proj_b, v_w_out, v_final_norm_w):
    xi, yi, ci = _my_place()
    k_me = 2 * xi + yi
    b_me = 4 * xi + 2 * yi + ci
    w = dict(ada_b=ada_b, norm_w=norm_w, gm_ln_w=gm_ln_w, gm_ln_b=gm_ln_b, gm_ws=gm_ws, gm_bs=gm_bs, conv_b=conv_b, dt_bias=dt_bias,
             a_log=a_log, d_skip=d_skip, ssm_norm_w=ssm_norm_w, final_norm_w=final_norm_w)
    m = dict(ada_b=m_ada_b, norm_w=m_norm_w, gm_ln_w=m_gm_ln_w, gm_ln_b=m_gm_ln_b, gm_ws=m_gm_ws, gm_bs=m_gm_bs, conv_b=m_conv_b,
             dt_bias=m_dt_bias, a_log=m_a_log, d_skip=m_d_skip, ssm_norm_w=m_ssm_norm_w, final_norm_w=m_final_norm_w)
    v = dict(ada_b=v_ada_b, norm_w=v_norm_w, gm_ln_w=v_gm_ln_w, gm_ln_b=v_gm_ln_b, gm_ws=v_gm_ws, gm_bs=v_gm_bs, conv_b=v_conv_b,
             dt_bias=v_dt_bias, a_log=v_a_log, d_skip=v_d_skip, ssm_norm_w=v_ssm_norm_w, final_norm_w=v_final_norm_w)

    c_slot = lax.dynamic_update_slice(jnp.zeros((8, D), f32), c, (b_me, 0))
    c_all = _allreduce(c_slot.reshape(64, 128), _AXES, "gather_c").reshape(8, D)
    ada_b_cols = lax.dynamic_slice(ada_b, (0, k_me * ADA_COLS), (DEPTH, ADA_COLS)).reshape(DEPTH, 1, ADA_COLS)
    mod_cols = _ada_fwd(c_all, ada_w, ada_b_cols, "ada_fwd")
    mod_slot = lax.dynamic_update_slice(jnp.zeros((DEPTH, 8, 3 * D), f32), mod_cols, (0, 0, k_me * ADA_COLS))
    mod_all = _allreduce(mod_slot.reshape(-1, 128), ("x", "y"), "gather_mod").reshape(DEPTH, 8, 3 * D)
    mod_me = lax.dynamic_slice(mod_all, (0, b_me, 0), (DEPTH, 1, 3 * D))
    mods = [(mod_me[l, :, 0:D], mod_me[l, :, D:2 * D], mod_me[l, :, 2 * D:3 * D]) for l in range(DEPTH)]

    rows_sh = jnp.concatenate([w_proj_a, w_proj_b, w_out], axis=1).astype(bf16)
    win_sh = w_in.astype(bf16)
    first = _gather_layer_behind([win_sh[0], conv_w[0]], "gather_l0", 1)
    first, mods, next_in, later = lax.optimization_barrier((first, mods, [rows_sh[0]], [win_sh[1], conv_w[1], rows_sh[1]]))
    first_rows = _gather_layer_behind(next_in, "gather_l0_merge", 5)
    second = _gather_layer_behind(later, "gather_l1", 2)
    others = [(first[0], first[1], first_rows[0]), second]

    def merge_matrices(rows_g, l):
        rows_l = _with_own(rows_g, rows_sh[l])
        return (jnp.concatenate([t[0:RA] for t in rows_l], axis=0), jnp.concatenate([t[RA:RA + RB] for t in rows_l], axis=0),
                jnp.concatenate([t[RA + RB:] for t in rows_l], axis=0))

    def operands_of(l, x_in):
        win_g, conv_g, rows_g = others[l]
        if l == 1:
            (win_g, conv_g, rows_g), x_in = lax.optimization_barrier(((win_g, conv_g, rows_g), x_in))
        w_in_l = jnp.concatenate(_with_own(win_g, win_sh[l]), axis=1)
        cw_l = jnp.concatenate(_with_own(conv_g, conv_w[l]), axis=1)
        wa_l, wb_l, wo_l = merge_matrices(rows_g, l) if l == 1 else (None, None, None)
        p = _layer_operands(w_in_l, cw_l, wa_l, wb_l, wo_l, norm_w[l], gm_ln_w[l], gm_ln_b[l], gm_ws[l], gm_bs[l], conv_b[l],
                            dt_bias[l], a_log[l], d_skip[l], ssm_norm_w[l])
        if l == 0:
            def late(yb):
                rows_now, yb = lax.optimization_barrier((rows_g, yb))
                return (*merge_matrices(rows_now, l), yb)
            p["merge_weights"] = late
        return p, x_in

    loss_parts, dx, grads, dfnw = _local_step(x[0], loss_target[0], mods, operands_of, final_norm_w)

    s_in = N_IN // NSHARD
    tr_ = lambda t: jnp.swapaxes(t, 1, 2)
    g_in = [grads[l]["w_in_t"].reshape(NSHARD, s_in, D) for l in range(DEPTH)]
    g_rows = [jnp.concatenate([grads[l]["w_proj_a"].reshape(NSHARD, RA, D), grads[l]["w_proj_b"].reshape(NSHARD, RB, D),
                               grads[l]["w_out"].reshape(NSHARD, RA, D)], axis=1) for l in range(DEPTH)]
    g_conv = [grads[l]["conv_w"].reshape(4, NSHARD, D).transpose(1, 0, 2) for l in range(DEPTH)]

    dmod_slot = lax.dynamic_update_slice(jnp.zeros((DEPTH, 8, 3 * D), f32),
                                         jnp.stack([grads[l]["mod"] for l in range(DEPTH)]).reshape(DEPTH, 1, 3 * D), (0, b_me, 0))
    small_g = {n: (dfnw if n == "final_norm_w" else jnp.stack([grads[l]["mod" if n == "ada_b" else n] for l in range(DEPTH)]))
               for n, _ in _SMALL}
    n_small = sum(_rows_of(s) for _, s in _SMALL)
    n_dmod = _rows_of(dmod_slot.shape)

    def small_work():
        tail = [jnp.zeros((8, 128), f32)] if (n_small + n_dmod + 8) % 16 else []
        packed = _allreduce_halves(_pack([small_g[n] for n, _ in _SMALL] + [dmod_slot, loss_parts] + tail), "allreduce_small")
        dmod_all = packed[n_small:n_small + n_dmod].reshape(DEPTH, 8, 3 * D)
        dmod_cols = lax.dynamic_slice(dmod_all, (0, 0, k_me * ADA_COLS), (DEPTH, 8, ADA_COLS))
        return packed, _ada_bwd_adamw(c_all, dmod_cols, ada_w, m_ada_w, v_ada_w, "ada_bwd_adamw")

    (f_in, f_rows, f_conv), (o_in, o_rows, o_conv), c_idx, (packed, ada_out) = _reduce_scatter([g_in, g_rows, g_conv], "", small_work)
    g_ada, d_ada, nm_ada, nv_ada = ada_out
    gr_in, d_in, nm_in, nv_in = [tr_(t) for t in _adamw_layers(tr_(w_in), f_in, o_in, tr_(m_w_in), tr_(v_w_in), c_idx, "adamw_w_in")]
    cat = lambda a, b, c_: jnp.concatenate([a, b, c_], axis=1)
    gr_rows, d_rows, nm_rows, nv_rows = _adamw_layers(cat(w_proj_a, w_proj_b, w_out), f_rows, o_rows, cat(m_w_proj_a, m_w_proj_b, m_w_out),
                                                      cat(v_w_proj_a, v_w_proj_b, v_w_out), c_idx, "adamw_rows")
    gr_conv = jnp.where(ci == 0, jnp.stack([f_conv, o_conv]), jnp.stack([o_conv, f_conv]))
    split = lambda t: (t[:, 0:RA], t[:, RA:RA + RB], t[:, RA + RB:])

    g_small = packed[0:n_small]
    loss = jnp.sum(packed[n_small + n_dmod:])
    small_gw = jnp.concatenate([g_small, _pack([gr_conv])], axis=0)
    d_s, nm_s, nv_s = _adamw(_pack([w[n] for n, _ in _SMALL] + [conv_w]), small_gw,
                             _pack([m[n] for n, _ in _SMALL] + [m_conv_w]), _pack([v[n] for n, _ in _SMALL] + [v_conv_w]), "adamw_small")
    shapes = [s for _, s in _SMALL] + [conv_w.shape]
    names = [n for n, _ in _SMALL] + ["conv_w"]
    g_d = dict(zip(names, _unpack(small_gw, shapes)))
    d_d = dict(zip(names, _unpack(d_s, shapes)))
    nm_d = dict(zip(names, _unpack(nm_s, shapes)))
    nv_d = dict(zip(names, _unpack(nv_s, shapes)))

    def by_name(big, small):
        ga, gb, go = split(big[1])
        return dict(small, ada_w=big[2], w_in=big[0], w_proj_a=ga, w_proj_b=gb, w_out=go)

    order = ["ada_w", "ada_b", "norm_w", "w_in", "gm_ln_w", "gm_ln_b", "gm_ws", "gm_bs", "conv_w", "conv_b", "dt_bias", "a_log",
             "d_skip", "ssm_norm_w", "w_proj_a", "w_proj_b", "w_out", "final_norm_w"]
    outs = []
    for big, small in (((gr_in, gr_rows, g_ada), g_d), ((d_in, d_rows, d_ada), d_d), ((nm_in, nm_rows, nm_ada), nm_d),
                       ((nv_in, nv_rows, nv_ada), nv_d)):
        t = by_name(big, small)
        outs += [t[n] for n in order]
    return (loss, dx.reshape(1, -1, D), *outs)
```

```python
import jax
import jax.numpy as jnp
from jax import lax
from jax.experimental import pallas as pl
from jax.experimental.pallas import tpu as pltpu
from jax.experimental.pallas import tpu_sc as plsc

f32 = jnp.float32
bf16 = jnp.bfloat16

D = 1024
DEPTH = 2
EPS = 1e-6
CH = 128
NG = 8
HPG = 4
HD = 64
NH = NG * HPG
NST = 128
DIN = 2048
CONVD = 4096
GW = DIN // NG
PB = CONVD + DIN + 256
PA = 3 * D
PG = 2 * D
N_IN = 11296
NSHARD = 4
RA, RB = D // NSHARD, DIN // NSHARD
V7X_VMEM_BYTES = 64 * 2 ** 20
VMEM_LIMIT = V7X_VMEM_BYTES - 8 * 2 ** 20

ADAM_LR, ADAM_B1, ADAM_B2, ADAM_EPS, ADAM_WD, ADAM_STEP = 0.001, 0.9, 0.999, 1e-08, 0.01, 10

_HI = lax.Precision.HIGHEST


def _cp(sem):
    return pltpu.CompilerParams(dimension_semantics=sem, vmem_limit_bytes=VMEM_LIMIT)


def _sigmoid(x):
    return 0.5 * jnp.tanh(0.5 * x) + 0.5


def _silu_and_grad(x):
    s = _sigmoid(x)
    return x * s, s * (1.0 + x * (1.0 - s))


_GELU_K = 0.7978845608028654
_GELU_C = 0.044715


def _gelu_and_grad(x):
    x2 = x * x
    t = jnp.tanh(_GELU_K * (x + _GELU_C * x * x2))
    g = 0.5 * x * (1.0 + t)
    dg = 0.5 * (1.0 + t) + 0.5 * x * (1.0 - t * t) * _GELU_K * (1.0 + 3.0 * _GELU_C * x2)
    return g, dg


def _gelu(x):
    t = jnp.tanh(_GELU_K * (x + _GELU_C * x * x * x))
    return 0.5 * x * (1.0 + t)


def _softplus(x):
    return jnp.maximum(x, 0.0) + jnp.log(1.0 + jnp.exp(-jnp.abs(x)))


def _dot(a, b):
    return jnp.dot(a, b, preferred_element_type=f32)


def _dot_nt(a, b):
    return lax.dot_general(a, b, (((1,), (1,)), ((), ())), preferred_element_type=f32)


def _dot_tn(a, b):
    return lax.dot_general(a, b, (((0,), (0,)), ((), ())), preferred_element_type=f32)


def _dot_hi(a, b):
    return jnp.dot(a, b, precision=_HI, preferred_element_type=f32)


def _rmsmod_fwd(x, nw, scale, shift, name):
    s = x.shape[0]
    ts = min(1024, s)

    def body(x_ref, nw_ref, sc_ref, sh_ref, h_ref):
        xv = x_ref[...]
        r = lax.rsqrt(jnp.mean(xv * xv, axis=-1, keepdims=True) + EPS)
        h_ref[...] = ((xv * r) * nw_ref[...] * (1.0 + sc_ref[...]) + sh_ref[...]).astype(bf16)

    row = pl.BlockSpec((1, D), lambda i: (0, 0))
    tile = pl.BlockSpec((ts, D), lambda i: (i, 0))
    return pl.pallas_call(
        body, grid=(s // ts,), in_specs=[tile, row, row, row], out_specs=tile,
        out_shape=jax.ShapeDtypeStruct((s, D), bf16), name=name, compiler_params=_cp(("parallel",)),
    )(x, nw, scale, shift)


def _rmsmod_bwd(dh, x, dres, nw, scale, name):
    s = x.shape[0]
    ts = min(1024, s)

    def body(dh_ref, x_ref, dres_ref, nw_ref, sc_ref, dx_ref, dsc_ref, dsh_ref, dnw_ref):
        @pl.when(pl.program_id(0) == 0)
        def _():
            dsc_ref[...] = jnp.zeros_like(dsc_ref)
            dsh_ref[...] = jnp.zeros_like(dsh_ref)
            dnw_ref[...] = jnp.zeros_like(dnw_ref)
        xv = x_ref[...]
        dhv = dh_ref[...]
        r = lax.rsqrt(jnp.mean(xv * xv, axis=-1, keepdims=True) + EPS)
        xn = xv * r
        one_sc = 1.0 + sc_ref[...]
        dsc_ref[...] += jnp.sum(dhv * xn * nw_ref[...], axis=0, keepdims=True)
        dsh_ref[...] += jnp.sum(dhv, axis=0, keepdims=True)
        dnw_ref[...] += jnp.sum(dhv * xn * one_sc, axis=0, keepdims=True)
        dxn = dhv * (nw_ref[...] * one_sc)
        dx_ref[...] = r * (dxn - xn * jnp.mean(dxn * xn, axis=-1, keepdims=True)) + dres_ref[...]

    row = pl.BlockSpec((1, D), lambda i: (0, 0))
    tile = pl.BlockSpec((ts, D), lambda i: (i, 0))
    vec = jax.ShapeDtypeStruct((1, D), f32)
    return pl.pallas_call(
        body, grid=(s // ts,), in_specs=[tile, tile, tile, row, row], out_specs=[tile, row, row, row],
        out_shape=[jax.ShapeDtypeStruct((s, D), f32), vec, vec, vec], name=name, compiler_params=_cp(("arbitrary",)),
    )(dh, x, dres, nw, scale)


def _pick(n, prefs):
    for p in prefs:
        if n % p == 0:
            return p
    return n


def _mm(a, b, out_dtype, name, c_in=None, trans_b=False):
    m, k = a.shape
    n = b.shape[0] if trans_b else b.shape[1]
    tn = _pick(n, (1280, 1024, 512))
    tk = _pick(k, (1280, 1024, 512))
    nk = k // tk
    one_pass = nk == 1 and c_in is None
    tm = _pick(m, (2048, 1024, 512, 256) if one_pass else (1024, 512, 256))
    dot = _dot_nt if trans_b else _dot

    def body(*refs):
        if one_pass:
            a_ref, b_ref, o_ref = refs
            o_ref[...] = dot(a_ref[...], b_ref[...]).astype(out_dtype)
            return
        if c_in is not None:
            a_ref, b_ref, c_ref, o_ref, acc = refs
        else:
            a_ref, b_ref, o_ref, acc = refs
        kk = pl.program_id(2)

        @pl.when(kk == 0)
        def _():
            if c_in is not None:
                acc[...] = c_ref[...]
            else:
                acc[...] = jnp.zeros_like(acc)
        acc[...] += dot(a_ref[...], b_ref[...])

        @pl.when(kk == nk - 1)
        def _():
            o_ref[...] = acc[...].astype(out_dtype)

    b_spec = pl.BlockSpec((tn, tk), lambda j, i, kk: (j, kk)) if trans_b else pl.BlockSpec((tk, tn), lambda j, i, kk: (kk, j))
    in_specs = [pl.BlockSpec((tm, tk), lambda j, i, kk: (i, kk)), b_spec]
    args = [a, b]
    if c_in is not None:
        in_specs.append(pl.BlockSpec((tm, tn), lambda j, i, kk: (i, j)))
        args.append(c_in)
    return pl.pallas_call(
        body, grid=(n // tn, m // tm, nk), in_specs=in_specs, out_specs=pl.BlockSpec((tm, tn), lambda j, i, kk: (i, j)),
        out_shape=jax.ShapeDtypeStruct((m, n), out_dtype), scratch_shapes=[] if one_pass else [pltpu.VMEM((tm, tn), f32)],
        name=name, compiler_params=_cp(("parallel", "parallel", "arbitrary")),
    )(*args)


def _mm_tn(a, b, name):
    t, k1 = a.shape
    n = b.shape[1]
    t1 = _pick(k1, (1280, 1024, 512))
    tn = _pick(n, (1280, 1024, 512))
    tt = _pick(t, (2048, 1024, 512, 256))
    nt = t // tt

    def body(a_ref, b_ref, o_ref):
        tt_i = pl.program_id(2)

        @pl.when(tt_i == 0)
        def _():
            o_ref[...] = jnp.zeros_like(o_ref)
        o_ref[...] += _dot_tn(a_ref[...], b_ref[...])

    return pl.pallas_call(
        body, grid=(k1 // t1, n // tn, nt),
        in_specs=[pl.BlockSpec((tt, t1), lambda i, j, tt_i: (tt_i, i)), pl.BlockSpec((tt, tn), lambda i, j, tt_i: (tt_i, j))],
        out_specs=pl.BlockSpec((t1, tn), lambda i, j, tt_i: (i, j)),
        out_shape=jax.ShapeDtypeStruct((k1, n), f32), name=name,
        compiler_params=_cp(("parallel", "parallel", "arbitrary")),
    )(a, b)


def _ln_stats(v):
    mu = jnp.mean(v, axis=-1, keepdims=True)
    vc = v - mu
    rstd = lax.rsqrt(jnp.mean(vc * vc, axis=-1, keepdims=True) + EPS)
    return vc * rstd, rstd


def _mix(w_ref, vl):
    return jnp.concatenate([_dot(w_ref[g], vl[:, g * CH:(g + 1) * CH]) for g in range(NG)], axis=1)


def _branch_a_fwd(proj_a, lnw, lnb, wsm, bsf, name):
    s = proj_a.shape[0]
    ta = min(512, s)

    def body(pu_ref, pv_ref, pz_ref, lnw_ref, lnb_ref, w_ref, bs_ref, ya_ref):
        for c in range(ta // CH):
            rows = pl.ds(c * CH, CH)
            vh, _ = _ln_stats(_gelu(pv_ref[rows, :]))
            vl = (vh * lnw_ref[...] + lnb_ref[...]).astype(bf16)
            mixed = _mix(w_ref, vl) + bs_ref[...]
            pz = pz_ref[rows, :]
            ya_ref[rows, :] = (_gelu(pu_ref[rows, :]) * mixed * (pz * _sigmoid(pz))).astype(bf16)

    row = pl.BlockSpec((1, D), lambda i: (0, 0))
    return pl.pallas_call(
        body, grid=(s // ta,),
        in_specs=[pl.BlockSpec((ta, D), lambda i: (i, 0)), pl.BlockSpec((ta, D), lambda i: (i, 1)),
                  pl.BlockSpec((ta, D), lambda i: (i, 2)), row, row,
                  pl.BlockSpec((NG, CH, CH), lambda i: (0, 0, 0)), pl.BlockSpec((CH, D), lambda i: (0, 0))],
        out_specs=pl.BlockSpec((ta, D), lambda i: (i, 0)),
        out_shape=jax.ShapeDtypeStruct((s, D), bf16), name=name, compiler_params=_cp(("parallel",)),
    )(proj_a, proj_a, proj_a, lnw, lnb, wsm, bsf)


def _branch_a_bwd(proj_a, dya, lnw, lnb, wsm, wsm_t, bsf, name):
    s = proj_a.shape[0]
    ta = min(512, s)

    def body(pu_ref, pv_ref, pz_ref, dya_ref, lnw_ref, lnb_ref, w_ref, wt_ref, bs_ref,
             dp_ref, dws_ref, dbs_ref, dlnw_ref, dlnb_ref):
        @pl.when(pl.program_id(0) == 0)
        def _():
            dws_ref[...] = jnp.zeros_like(dws_ref)
            dbs_ref[...] = jnp.zeros_like(dbs_ref)
            dlnw_ref[...] = jnp.zeros_like(dlnw_ref)
            dlnb_ref[...] = jnp.zeros_like(dlnb_ref)
        for c in range(ta // CH):
            rows = pl.ds(c * CH, CH)
            u, du = _gelu_and_grad(pu_ref[rows, :])
            v, dv_act = _gelu_and_grad(pv_ref[rows, :])
            zg, dzg = _silu_and_grad(pz_ref[rows, :])
            vh, rstd = _ln_stats(v)
            vl = (vh * lnw_ref[...] + lnb_ref[...]).astype(bf16)
            mixed = _mix(w_ref, vl) + bs_ref[...]
            dy = dya_ref[rows, :].astype(f32)
            dmixed = dy * u * zg
            dp_ref[rows, 0:D] = (dy * mixed * zg * du).astype(bf16)
            dp_ref[rows, 2 * D:3 * D] = (dy * u * mixed * dzg).astype(bf16)
            dmb = dmixed.astype(bf16)
            dbs_ref[...] += dmixed
            dvl = _mix(wt_ref, dmb)
            for g in range(NG):
                cols = slice(g * CH, (g + 1) * CH)
                dws_ref[g] += _dot_nt(dmb[:, cols], vl[:, cols])
            dlnw_ref[...] += jnp.sum(dvl * vh, axis=0, keepdims=True)
            dlnb_ref[...] += jnp.sum(dvl, axis=0, keepdims=True)
            dvh = dvl * lnw_ref[...]
            dv = rstd * (dvh - jnp.mean(dvh, axis=-1, keepdims=True) - vh * jnp.mean(dvh * vh, axis=-1, keepdims=True))
            dp_ref[rows, D:2 * D] = (dv * dv_act).astype(bf16)

    row = pl.BlockSpec((1, D), lambda i: (0, 0))
    wspec = pl.BlockSpec((NG, CH, CH), lambda i: (0, 0, 0))
    full = pl.BlockSpec((CH, D), lambda i: (0, 0))
    return pl.pallas_call(
        body, grid=(s // ta,),
        in_specs=[pl.BlockSpec((ta, D), lambda i: (i, 0)), pl.BlockSpec((ta, D), lambda i: (i, 1)),
                  pl.BlockSpec((ta, D), lambda i: (i, 2)), pl.BlockSpec((ta, D), lambda i: (i, 0)),
                  row, row, wspec, wspec, full],
        out_specs=[pl.BlockSpec((ta, PA), lambda i: (i, 0)), wspec, full, row, row],
        out_shape=[jax.ShapeDtypeStruct((s, PA), bf16), jax.ShapeDtypeStruct((NG, CH, CH), f32),
                   jax.ShapeDtypeStruct((CH, D), f32), jax.ShapeDtypeStruct((1, D), f32), jax.ShapeDtypeStruct((1, D), f32)],
        name=name, compiler_params=_cp(("arbitrary",)),
    )(proj_a, proj_a, proj_a, dya, lnw, lnb, wsm, wsm_t, bsf)


GB = GW + 2 * NST


def _group_major(xs, b, c):
    lead = xs.shape[:-1]
    return jnp.concatenate([xs.reshape(lead + (NG, GW)), b.reshape(lead + (NG, NST)), c.reshape(lead + (NG, NST))],
                           axis=-1).reshape(lead + (CONVD,))


def _from_group_major(t):
    lead = t.shape[:-1]
    t = t.reshape(lead + (NG, GB))
    return jnp.concatenate([t[..., 0:GW].reshape(lead + (DIN,)), t[..., GW:GW + NST].reshape(lead + (NG * NST,)),
                            t[..., GW + NST:GB].reshape(lead + (NG * NST,))], axis=-1)


def _rows_from_group_major(t):
    t = t.reshape(NG, GB, t.shape[-1])
    return jnp.concatenate([t[:, 0:GW].reshape(DIN, -1), t[:, GW:GW + NST].reshape(NG * NST, -1),
                            t[:, GW + NST:GB].reshape(NG * NST, -1)], axis=0)


def _shift_rows(x, prev8, j):
    xr = pltpu.roll(x, j, 0)
    fix = pltpu.roll(prev8, j, 0)
    rid = lax.broadcasted_iota(jnp.int32, (8, x.shape[1]), 0)
    top = jnp.where(rid < j, fix, xr[0:8])
    return jnp.concatenate([top, xr[8:]], axis=0)


def _shift_rows_up(d, next8, j):
    dr = pltpu.roll(d, CH - j, 0)
    fix = pltpu.roll(next8, 8 - j, 0)
    rid = lax.broadcasted_iota(jnp.int32, (8, d.shape[1]), 0)
    bot = jnp.where(rid >= 8 - j, fix, dr[CH - 8:CH])
    return jnp.concatenate([dr[0:CH - 8], bot], axis=0)


def _conv_pre(x, prev8, cw_ref, cb_ref, cols):
    shifted = [_shift_rows(x, prev8, j) for j in (1, 2, 3)]
    conv = cb_ref[:, cols] + cw_ref[3:4, cols] * x
    for j in (1, 2, 3):
        conv = conv + cw_ref[3 - j:4 - j, cols] * shifted[j - 1]
    return conv, shifted


def _tril_mask():
    return lax.broadcasted_iota(jnp.int32, (CH, CH), 0) >= lax.broadcasted_iota(jnp.int32, (CH, CH), 1)


def _sum_all(v):
    return jnp.sum(jnp.sum(v, axis=0, keepdims=True), axis=1, keepdims=True)


def _lanes(g, width, base=0):
    return pl.ds(pl.multiple_of(base + g * width, width), width)


def _branch_b_fwd(proj_b, cw, cb, dtb, a_row, dkc, snw, name):
    s = proj_b.shape[0]
    nc = s // CH

    def body(xbc_ref, sz_ref, dtr_ref, cw_ref, cb_ref, dtb_ref, a_ref, dkc_ref, snw_ref,
             yb_ref, y_ref, st_ref, cv_ref, prev8, state, acst_s):
        @pl.when(pl.program_id(0) == 0)
        def _():
            prev8[...] = jnp.zeros_like(prev8)
            state[...] = jnp.zeros_like(state)
        st_ref[0] = state[...].astype(bf16)
        mask = _tril_mask()
        dt_all = _softplus(dtr_ref[:, 0:CH] + dtb_ref[...])
        acs_all = _dot_hi(mask.astype(f32), dt_all * a_ref[...])
        acst_s[...] = acs_all.T

        def group(g, carry):
            cols = _lanes(g, GB)
            gcols = _lanes(g, GW)
            x = xbc_ref[:, cols]
            conv, _ = _conv_pre(x, prev8[:, cols], cw_ref, cb_ref, cols)
            prev8[:, cols] = x[CH - 8:CH]
            cv_ref[:, cols] = conv.astype(bf16)
            xc = conv * _sigmoid(conv)
            xs = xc[:, 0:GW]
            bg = xc[:, GW:GW + NST].astype(bf16)
            cg = xc[:, GW + NST:GB].astype(bf16)
            back = lax.rem(CH - HPG * g, CH)
            dt = pltpu.roll(dt_all, back, 1)
            acs = pltpu.roll(acs_all, back, 1)
            cbm = _dot_nt(cg, bg)
            dkc_g = dkc_ref[:, gcols]
            y_parts = []
            for r in range(HPG):
                colb = jnp.broadcast_to(acs[:, r:r + 1], (CH, CH))
                row = acst_s[pl.ds(g * HPG + r, 1), :]
                lmat = jnp.exp(jnp.where(mask, colb - row, -jnp.inf))
                xr = xs[:, r * HD:(r + 1) * HD]
                xd = xr * dt[:, r:r + 1]
                sp = state[g * HPG + r]
                col = colb[:, 0:HD]
                alast = colb[CH - 1:CH, 0:HD]
                y_r = _dot((cbm * lmat).astype(bf16), xd.astype(bf16))
                y_r = y_r + jnp.exp(col) * _dot_nt(cg, sp.astype(bf16))
                y_parts.append(y_r + xr * dkc_g[:, r * HD:(r + 1) * HD])
                cs = _dot_tn((xd * jnp.exp(alast - col)).astype(bf16), bg)
                state[g * HPG + r] = jnp.exp(colb[CH - 1:CH, :]) * sp + cs
            y = jnp.concatenate(y_parts, axis=1)
            szv = sz_ref[:, gcols]
            yz = y * (szv * _sigmoid(szv))
            rr = lax.rsqrt(jnp.mean(yz * yz, axis=-1, keepdims=True) + EPS)
            yb_ref[:, gcols] = (yz * rr * snw_ref[:, gcols]).astype(bf16)
            y_ref[:, gcols] = y.astype(bf16)
            return carry

        lax.fori_loop(0, NG, group, 0)

    const2 = lambda c: (0, 0)
    return pl.pallas_call(
        body, grid=(nc,),
        in_specs=[pl.BlockSpec((CH, CONVD), lambda c: (c, 0)), pl.BlockSpec((CH, DIN), lambda c: (c, CONVD // DIN)),
                  pl.BlockSpec((CH, 256), lambda c: (c, (CONVD + DIN) // 256)),
                  pl.BlockSpec((4, CONVD), const2), pl.BlockSpec((1, CONVD), const2),
                  pl.BlockSpec((1, CH), const2), pl.BlockSpec((1, CH), const2),
                  pl.BlockSpec((1, DIN), const2), pl.BlockSpec((1, DIN), const2)],
        out_specs=[pl.BlockSpec((CH, DIN), lambda c: (c, 0)), pl.BlockSpec((CH, DIN), lambda c: (c, 0)),
                   pl.BlockSpec((1, NH, HD, NST), lambda c: (c, 0, 0, 0)), pl.BlockSpec((CH, CONVD), lambda c: (c, 0))],
        out_shape=[jax.ShapeDtypeStruct((s, DIN), bf16), jax.ShapeDtypeStruct((s, DIN), bf16),
                   jax.ShapeDtypeStruct((nc, NH, HD, NST), bf16), jax.ShapeDtypeStruct((s, CONVD), bf16)],
        scratch_shapes=[pltpu.VMEM((8, CONVD), f32), pltpu.VMEM((NH, HD, NST), f32), pltpu.VMEM((CH, CH), f32)],
        name=name, compiler_params=_cp(("arbitrary",)),
    )(proj_b, proj_b, proj_b, cw, cb, dtb, a_row, dkc, snw)


def _branch_b_bwd(proj_b, conv_sv, dyb, y_sv, states, cw, dtb, a_row, dkc, snw, ind, name):
    s = proj_b.shape[0]
    nc = s // CH

    def body(xbc_ref, cv_ref, sz_ref, dtr_ref, dyb_ref, y_ref, st_ref, cw_ref, dtb_ref, a_ref, dkc_ref,
             snw_ref, ind_ref, dp_ref, dcw_ref, dcb_ref, ddtb_ref, dal_ref, ddk_ref, dsnw_ref,
             dstate, dnext8, acst_s, dacs_acc, q2_acc):
        @pl.when(pl.program_id(0) == 0)
        def _():
            dstate[...] = jnp.zeros_like(dstate)
            dnext8[...] = jnp.zeros_like(dnext8)
            dcw_ref[...] = jnp.zeros_like(dcw_ref)
            dcb_ref[...] = jnp.zeros_like(dcb_ref)
            ddtb_ref[...] = jnp.zeros_like(ddtb_ref)
            dal_ref[...] = jnp.zeros_like(dal_ref)
            ddk_ref[...] = jnp.zeros_like(ddk_ref)
            dsnw_ref[...] = jnp.zeros_like(dsnw_ref)

        dacs_acc[...] = jnp.zeros_like(dacs_acc)
        q2_acc[...] = jnp.zeros_like(q2_acc)
        mask = _tril_mask()
        tri_t = (lax.broadcasted_iota(jnp.int32, (CH, CH), 0) <= lax.broadcasted_iota(jnp.int32, (CH, CH), 1)).astype(f32)
        lane1 = lax.broadcasted_iota(jnp.int32, (1, CH), 1)
        is_last = lax.broadcasted_iota(jnp.int32, (CH, 1), 0) == CH - 1
        z_all = dtr_ref[:, 0:CH] + dtb_ref[...]
        dt_all = _softplus(z_all)
        adt_all = dt_all * a_ref[...]
        acs_all = _dot_hi(mask.astype(f32), adt_all)
        acst_s[...] = acs_all.T

        def ind_sum(v):
            hi = v.astype(bf16)
            lo = (v - hi.astype(f32)).astype(bf16)
            return _dot(hi, ind_ref[...]) + _dot(lo, ind_ref[...])

        def group(g, carry):
            cols = _lanes(g, GB)
            gcols = _lanes(g, GW)
            conv = cv_ref[:, cols].astype(f32)
            sg = _sigmoid(conv)
            xc = conv * sg
            xs = xc[:, 0:GW]
            bg = xc[:, GW:GW + NST].astype(bf16)
            cg = xc[:, GW + NST:GB].astype(bf16)

            y = y_ref[:, gcols].astype(f32)
            silu_sz, dsilu_sz = _silu_and_grad(sz_ref[:, gcols])
            yz = y * silu_sz
            rr = lax.rsqrt(jnp.mean(yz * yz, axis=-1, keepdims=True) + EPS)
            dyb_g = dyb_ref[:, gcols].astype(f32)
            w = dyb_g * snw_ref[:, gcols]
            dsnw_ref[:, gcols] += jnp.sum(dyb_g * yz * rr, axis=0, keepdims=True)
            dyz = rr * w - yz * (rr * rr * rr) * jnp.mean(w * yz, axis=-1, keepdims=True)
            dp_ref[:, _lanes(g, GW, CONVD)] = (dyz * y * dsilu_sz).astype(bf16)
            dy_g = dyz * silu_sz
            ddk_ref[:, gcols] += jnp.sum(dy_g * xs, axis=0, keepdims=True)

            back = lax.rem(CH - HPG * g, CH)
            dt = pltpu.roll(dt_all, back, 1)
            acs = pltpu.roll(acs_all, back, 1)
            cbm = _dot_nt(cg, bg)
            d_cb = jnp.zeros((CH, CH), f32)
            d_bg = jnp.zeros((CH, NST), f32)
            d_cg = jnp.zeros((CH, NST), f32)
            lastrow = jnp.zeros((1, CH), f32)
            dxd_parts, dxs_parts, t_parts = [], [], []
            for r in range(HPG):
                h = g * HPG + r
                colb = jnp.broadcast_to(acs[:, r:r + 1], (CH, CH))
                row = acst_s[pl.ds(h, 1), :]
                lmat = jnp.exp(jnp.where(mask, colb - row, -jnp.inf))
                mmat_b = (cbm * lmat).astype(bf16)
                dtc = dt[:, r:r + 1]
                xd = xs[:, r * HD:(r + 1) * HD] * dtc
                col = colb[:, 0:HD]
                alast = colb[CH - 1:CH, 0:HD]
                dte = jnp.exp(alast - col)
                ea = jnp.exp(col)
                cd = jnp.exp(colb[CH - 1:CH, :])
                sp = st_ref[0, h]
                dsn = dstate[h]
                dsn_b = dsn.astype(bf16)
                dyr = dy_g[:, r * HD:(r + 1) * HD]
                dyr_b = dyr.astype(bf16)
                dye_b = (dyr * ea).astype(bf16)
                d_cg = d_cg + _dot(dye_b, sp)
                dxde = _dot_nt(bg, dsn_b)
                xdte = xd * dte
                xd_b = xd.astype(bf16)
                d_bg = d_bg + _dot(xdte.astype(bf16), dsn_b)
                dxd_diag = _dot_tn(mmat_b, dyr_b)
                dxd = dxde * dte + dxd_diag
                d_cb = d_cb + _dot_nt(dyr_b, xd_b) * lmat
                t_parts.append(dyr_b.astype(f32) * _dot(mmat_b, xd_b) + dyr * (ea * _dot_nt(cg, sp))
                               - xd_b.astype(f32) * dxd_diag - dxde * xdte)
                lastrow = jnp.where(lane1 == r, _sum_all(dsn * sp.astype(f32)) * cd + _sum_all(dxde * xdte), lastrow)
                dstate[h] = cd * dsn + _dot_tn(dye_b, cg)
                dxd_parts.append(dxd)
                dxs_parts.append(dxd * dtc)
            d_cb_b = d_cb.astype(bf16)
            d_bg = d_bg + _dot_tn(d_cb_b, cg)
            d_cg = d_cg + _dot(d_cb_b, bg)
            q2 = ind_sum(jnp.concatenate(dxd_parts, axis=1) * xs)
            dacs = ind_sum(jnp.concatenate(t_parts, axis=1)) + jnp.where(is_last, lastrow, 0.0)
            dacs_acc[...] += pltpu.roll(dacs, HPG * g, 1)
            q2_acc[...] += pltpu.roll(q2, HPG * g, 1)
            dxs = jnp.concatenate(dxs_parts, axis=1) + dy_g * dkc_ref[:, gcols]

            dconv = jnp.concatenate([dxs, d_bg, d_cg], axis=1) * (sg * (1.0 + conv * (1.0 - sg)))
            x = xbc_ref[:, cols]
            dcb_ref[:, cols] += jnp.sum(dconv, axis=0, keepdims=True)
            dcw_ref[3:4, cols] += jnp.sum(dconv * x, axis=0, keepdims=True)
            dx = cw_ref[3:4, cols] * dconv
            nxt = dnext8[:, cols]
            for j in (1, 2, 3):
                up = _shift_rows_up(dconv, nxt, j)
                dcw_ref[3 - j:4 - j, cols] += jnp.sum(up * x, axis=0, keepdims=True)
                dx = dx + cw_ref[3 - j:4 - j, cols] * up
            dnext8[:, cols] = dconv[0:8]
            dp_ref[:, cols] = dx.astype(bf16)
            return carry

        lax.fori_loop(0, NG, group, 0)

        dadt = _dot_hi(tri_t, dacs_acc[...])
        dal_ref[...] += jnp.sum(dadt * adt_all, axis=0, keepdims=True)
        ddz = (dadt * a_ref[...] + q2_acc[...]) * _sigmoid(z_all)
        ddtb_ref[...] += jnp.sum(ddz, axis=0, keepdims=True)
        dp_ref[:, CONVD + DIN:CONVD + DIN + CH] = ddz.astype(bf16)
        dp_ref[:, CONVD + DIN + CH:PB] = jnp.zeros((CH, PB - CONVD - DIN - CH), bf16)

    const2 = lambda c: (0, 0)
    rev = lambda c: (nc - 1 - c, 0)
    return pl.pallas_call(
        body, grid=(nc,),
        in_specs=[pl.BlockSpec((CH, CONVD), rev), pl.BlockSpec((CH, CONVD), rev),
                  pl.BlockSpec((CH, DIN), lambda c: (nc - 1 - c, CONVD // DIN)),
                  pl.BlockSpec((CH, 256), lambda c: (nc - 1 - c, (CONVD + DIN) // 256)),
                  pl.BlockSpec((CH, DIN), rev), pl.BlockSpec((CH, DIN), rev),
                  pl.BlockSpec((1, NH, HD, NST), lambda c: (nc - 1 - c, 0, 0, 0)),
                  pl.BlockSpec((4, CONVD), const2), pl.BlockSpec((1, CH), const2), pl.BlockSpec((1, CH), const2),
                  pl.BlockSpec((1, DIN), const2), pl.BlockSpec((1, DIN), const2), pl.BlockSpec((GW, CH), const2)],
        out_specs=[pl.BlockSpec((CH, PB), rev), pl.BlockSpec((4, CONVD), const2), pl.BlockSpec((1, CONVD), const2),
                   pl.BlockSpec((1, CH), const2), pl.BlockSpec((1, CH), const2), pl.BlockSpec((1, DIN), const2),
                   pl.BlockSpec((1, DIN), const2)],
        out_shape=[jax.ShapeDtypeStruct((s, PB), bf16), jax.ShapeDtypeStruct((4, CONVD), f32),
                   jax.ShapeDtypeStruct((1, CONVD), f32), jax.ShapeDtypeStruct((1, CH), f32),
                   jax.ShapeDtypeStruct((1, CH), f32), jax.ShapeDtypeStruct((1, DIN), f32),
                   jax.ShapeDtypeStruct((1, DIN), f32)],
        scratch_shapes=[pltpu.VMEM((NH, HD, NST), f32), pltpu.VMEM((8, CONVD), f32), pltpu.VMEM((CH, CH), f32),
                        pltpu.VMEM((CH, CH), f32), pltpu.VMEM((CH, CH), f32)],
        name=name, compiler_params=_cp(("arbitrary",)),
    )(proj_b, conv_sv, proj_b, proj_b, dyb, y_sv, states, cw, dtb, a_row, dkc, snw, ind)


def _merge_fwd(ya, yb, proj_g, x, gate, wa, wb, wo, name):
    s = x.shape[0]
    ts = min(512, s)

    def body(ya_ref, yb_ref, ga_ref, gb_ref, x_ref, gate_ref, wa_ref, wb_ref, wo_ref, xo_ref, pa_ref, pb_ref, mg_ref, o_ref):
        pa = _dot(ya_ref[...], wa_ref[...])
        pb = _dot(yb_ref[...], wb_ref[...])
        mg = (_sigmoid(ga_ref[...]) * pa + _sigmoid(gb_ref[...]) * pb).astype(bf16)
        o = _dot(mg, wo_ref[...])
        xo_ref[...] = x_ref[...] + gate_ref[...] * o
        pa_ref[...] = pa.astype(bf16)
        pb_ref[...] = pb.astype(bf16)
        mg_ref[...] = mg
        o_ref[...] = o.astype(bf16)

    tile = pl.BlockSpec((ts, D), lambda i: (i, 0))
    const = lambda i: (0, 0)
    act = jax.ShapeDtypeStruct((s, D), bf16)
    return pl.pallas_call(
        body, grid=(s // ts,),
        in_specs=[tile, pl.BlockSpec((ts, DIN), lambda i: (i, 0)), tile, pl.BlockSpec((ts, D), lambda i: (i, 1)), tile,
                  pl.BlockSpec((1, D), const), pl.BlockSpec((D, D), const), pl.BlockSpec((DIN, D), const),
                  pl.BlockSpec((D, D), const)],
        out_specs=[tile, tile, tile, tile, tile],
        out_shape=[jax.ShapeDtypeStruct((s, D), f32), act, act, act, act],
        name=name, compiler_params=_cp(("parallel",)),
    )(ya, yb, proj_g, proj_g, x, gate, wa, wb, wo)


def _merge_bwd(dxo, gate, o_sv, pa_sv, pb_sv, proj_g, wo, wa, wb, name):
    s = dxo.shape[0]
    ts = min(512, s)

    def body(dxo_ref, gate_ref, o_ref, pa_ref, pb_ref, ga_ref, gb_ref, wo_ref, wa_ref, wb_ref,
             do_ref, dpa_ref, dpb_ref, dg_ref, dya_ref, dyb_ref, dgate_ref):
        @pl.when(pl.program_id(0) == 0)
        def _():
            dgate_ref[...] = jnp.zeros_like(dgate_ref)
        dxo_v = dxo_ref[...]
        dgate_ref[...] += jnp.sum(dxo_v * o_ref[...].astype(f32), axis=0, keepdims=True)
        do = (dxo_v * gate_ref[...]).astype(bf16)
        do_ref[...] = do
        dmg = _dot_nt(do, wo_ref[...])
        sa = _sigmoid(ga_ref[...])
        sb = _sigmoid(gb_ref[...])
        dpa = (dmg * sa).astype(bf16)
        dpb = (dmg * sb).astype(bf16)
        dpa_ref[...] = dpa
        dpb_ref[...] = dpb
        dg_ref[:, 0:D] = (dmg * pa_ref[...].astype(f32) * sa * (1.0 - sa)).astype(bf16)
        dg_ref[:, D:2 * D] = (dmg * pb_ref[...].astype(f32) * sb * (1.0 - sb)).astype(bf16)
        dya_ref[...] = _dot_nt(dpa, wa_ref[...]).astype(bf16)
        dyb_ref[...] = _dot_nt(dpb, wb_ref[...]).astype(bf16)

    tile = pl.BlockSpec((ts, D), lambda i: (i, 0))
    const = lambda i: (0, 0)
    act = jax.ShapeDtypeStruct((s, D), bf16)
    return pl.pallas_call(
        body, grid=(s // ts,),
        in_specs=[tile, pl.BlockSpec((1, D), const), tile, tile, tile, tile, pl.BlockSpec((ts, D), lambda i: (i, 1)),
                  pl.BlockSpec((D, D), const), pl.BlockSpec((D, D), const), pl.BlockSpec((DIN, D), const)],
        out_specs=[tile, tile, tile, pl.BlockSpec((ts, PG), lambda i: (i, 0)), tile, pl.BlockSpec((ts, DIN), lambda i: (i, 0)),
                   pl.BlockSpec((1, D), const)],
        out_shape=[act, act, act, jax.ShapeDtypeStruct((s, PG), bf16), act, jax.ShapeDtypeStruct((s, DIN), bf16),
                   jax.ShapeDtypeStruct((1, D), f32)],
        name=name, compiler_params=_cp(("arbitrary",)),
    )(dxo, gate, o_sv, pa_sv, pb_sv, proj_g, proj_g, wo, wa, wb)


def _final_loss(x, target, fnw, name):
    s = x.shape[0]
    ts = min(512, s)

    def body(x_ref, t_ref, w_ref, loss_ref, dx_ref, dw_ref):
        @pl.when(pl.program_id(0) == 0)
        def _():
            loss_ref[...] = jnp.zeros_like(loss_ref)
            dw_ref[...] = jnp.zeros_like(dw_ref)
        xv = x_ref[...]
        r = lax.rsqrt(jnp.mean(xv * xv, axis=-1, keepdims=True) + EPS)
        xn = xv * r
        err = xn * w_ref[...] - t_ref[...]
        part = jnp.sum(err * err, axis=0, keepdims=True)
        acc = part[:, 0:128]
        for k in range(1, D // 128):
            acc = acc + part[:, k * 128:(k + 1) * 128]
        loss_ref[0:1, :] += acc * (0.5 / D)
        dy = err * (1.0 / D)
        dw_ref[...] += jnp.sum(dy * xn, axis=0, keepdims=True)
        dxn = dy * w_ref[...]
        dx_ref[...] = r * (dxn - xn * jnp.mean(dxn * xn, axis=-1, keepdims=True))

    tile = pl.BlockSpec((ts, D), lambda i: (i, 0))
    row = pl.BlockSpec((1, D), lambda i: (0, 0))
    return pl.pallas_call(
        body, grid=(s // ts,), in_specs=[tile, tile, row],
        out_specs=[pl.BlockSpec((8, 128), lambda i: (0, 0)), tile, row],
        out_shape=[jax.ShapeDtypeStruct((8, 128), f32), jax.ShapeDtypeStruct((s, D), f32), jax.ShapeDtypeStruct((1, D), f32)],
        name=name, compiler_params=_cp(("arbitrary",)),
    )(x, target, fnw)


def _layer_operands(w_in, conv_w, wa, wb, wo, norm_w, gm_ln_w, gm_ln_b, gm_ws, gm_bs, conv_b, dt_bias, a_log, d_skip, ssm_norm_w):
    sz0, x0 = PA, PA + DIN
    b0, c0, dt0 = x0 + DIN, x0 + DIN + NG * NST, x0 + CONVD
    w_xbc = _group_major(w_in[:, x0:b0], w_in[:, b0:c0], w_in[:, c0:dt0])
    w_b = jnp.concatenate([w_xbc, w_in[:, sz0:x0], w_in[:, dt0:dt0 + NH], jnp.zeros((D, PB - CONVD - DIN - NH), bf16)], axis=1)
    w_a = w_in[:, 0:PA]
    w_g = w_in[:, dt0 + NH:N_IN]
    tril = jnp.tril(jnp.ones((CH, CH), bool))
    wsm = jnp.where(tril[None], gm_ws, 0.0).astype(bf16)

    def heads_row(v):
        return jnp.pad(v, (0, CH - NH)).reshape(1, CH)

    return dict(
        w_b=w_b, w_a=w_a, w_g=w_g, wa=wa, wb=wb, wo=wo,
        norm_w=norm_w.reshape(1, D), lnw=gm_ln_w.reshape(1, D), lnb=gm_ln_b.reshape(1, D),
        wsm=wsm, wsm_t=jnp.swapaxes(wsm, 1, 2), bsf=jnp.repeat(gm_bs.T, CH, axis=1),
        cw=_group_major(conv_w[:, 0:DIN], conv_w[:, DIN:DIN + NG * NST], conv_w[:, DIN + NG * NST:CONVD]),
        cb=_group_major(conv_b[0:DIN], conv_b[DIN:DIN + NG * NST], conv_b[DIN + NG * NST:CONVD]).reshape(1, CONVD),
        dtb=heads_row(dt_bias), a_row=heads_row(-jnp.exp(a_log)),
        snw=ssm_norm_w.reshape(1, DIN), dkc=jnp.repeat(d_skip, HD).reshape(1, DIN),
        ind=(jnp.arange(GW)[:, None] // HD == jnp.arange(CH)[None, :]).astype(bf16),
    )


def _layer_fwd(x, shift, scale, gate, p, tag):
    h = _rmsmod_fwd(x, p["norm_w"], scale, shift, f"rmsmod_fwd{tag}")
    proj_b = _mm(h, p["w_b"], f32, f"proj_b{tag}")
    proj_a = _mm(h, p["w_a"], f32, f"proj_a{tag}")
    proj_g = _mm(h, p["w_g"], f32, f"proj_g{tag}")
    ya = _branch_a_fwd(proj_a, p["lnw"], p["lnb"], p["wsm"], p["bsf"], f"branch_a_fwd{tag}")
    yb, y_sv, states, conv_sv = _branch_b_fwd(proj_b, p["cw"], p["cb"], p["dtb"], p["a_row"], p["dkc"], p["snw"], f"branch_b_fwd{tag}")
    if "merge_weights" in p:
        p["wa"], p["wb"], p["wo"], yb = p.pop("merge_weights")(yb)
    x_out, pa, pb, mg, o = _merge_fwd(ya, yb, proj_g, x, gate, p["wa"], p["wb"], p["wo"], f"merge_fwd{tag}")
    saved = dict(x=x, h=h, proj_b=proj_b, proj_a=proj_a, proj_g=proj_g, ya=ya, yb=yb, y=y_sv, states=states, conv=conv_sv,
                 pa=pa, pb=pb, mg=mg, o=o, scale=scale, gate=gate)
    return x_out, saved


def _layer_bwd(dxo, sv, p, tag):
    do, dpa, dpb, dg, dya, dyb, dgate = _merge_bwd(dxo, sv["gate"], sv["o"], sv["pa"], sv["pb"], sv["proj_g"],
                                                   p["wo"], p["wa"], p["wb"], f"merge_bwd{tag}")
    d_wo = _mm_tn(sv["mg"], do, f"d_wo{tag}")
    d_wa = _mm_tn(sv["ya"], dpa, f"d_wa{tag}")
    d_wb = _mm_tn(sv["yb"], dpb, f"d_wb{tag}")
    da, dws, dbs, dlnw, dlnb = _branch_a_bwd(sv["proj_a"], dya, p["lnw"], p["lnb"], p["wsm"], p["wsm_t"], p["bsf"],
                                             f"branch_a_bwd{tag}")
    db, dcw, dcb, ddtb, dal, ddk, dsnw = _branch_b_bwd(sv["proj_b"], sv["conv"], dyb, sv["y"], sv["states"], p["cw"], p["dtb"],
                                                       p["a_row"], p["dkc"], p["snw"], p["ind"], f"branch_b_bwd{tag}")
    dh = _mm(db, p["w_b"], f32, f"dh_b{tag}", trans_b=True)
    dh = _mm(da, p["w_a"], f32, f"dh_a{tag}", c_in=dh, trans_b=True)
    dh = _mm(dg, p["w_g"], f32, f"dh_g{tag}", c_in=dh, trans_b=True)
    d_w_b_t = _mm_tn(db, sv["h"], f"d_w_b{tag}")
    d_w_a_t = _mm_tn(da, sv["h"], f"d_w_a{tag}")
    d_w_g_t = _mm_tn(dg, sv["h"], f"d_w_g{tag}")
    dx, dscale, dshift, dnw = _rmsmod_bwd(dh, sv["x"], dxo, p["norm_w"], sv["scale"], f"rmsmod_bwd{tag}")
    d_w_in_t = jnp.concatenate([d_w_a_t, d_w_b_t[CONVD:CONVD + DIN], _rows_from_group_major(d_w_b_t[0:CONVD]),
                                d_w_b_t[CONVD + DIN:CONVD + DIN + NH], d_w_g_t], axis=0)
    tril = jnp.tril(jnp.ones((CH, CH), bool))
    heads = lambda v: v[0, 0:NH]
    grads = dict(
        w_in_t=d_w_in_t, w_proj_a=d_wa, w_proj_b=d_wb, w_out=d_wo, conv_w=_from_group_major(dcw), conv_b=_from_group_major(dcb).reshape(CONVD),
        norm_w=dnw.reshape(D), gm_ln_w=dlnw.reshape(D), gm_ln_b=dlnb.reshape(D),
        gm_ws=jnp.where(tril[None], dws, 0.0), gm_bs=dbs.reshape(CH, NG, CH).sum(-1).T,
        dt_bias=heads(ddtb), a_log=heads(dal), d_skip=ddk.reshape(NH, HD).sum(-1), ssm_norm_w=dsnw.reshape(DIN),
        mod=jnp.concatenate([dshift, dscale, dgate], axis=1).reshape(3 * D),
    )
    return dx, grads


def _local_step(x, target, mods, operands_of, fnw):
    saved, layer_ops = [], []
    for l in range(DEPTH):
        shift, scale, gate = mods[l]
        p, x = operands_of(l, x)
        layer_ops.append(p)
        x, sv = _layer_fwd(x, shift, scale, gate, p, f"_l{l}")
        saved.append(sv)
    loss_parts, dx, dfnw = _final_loss(x, target, fnw.reshape(1, D), "final_loss")
    grads = [None] * DEPTH
    for l in reversed(range(DEPTH)):
        dx, grads[l] = _layer_bwd(dx, saved[l], layer_ops[l], f"_l{l}")
    return loss_parts, dx, grads, dfnw.reshape(D)


ADA_COLS = 3 * D // NSHARD


def _ada_fwd(c_all, ada_w, ada_b_cols, name):
    def body(c_ref, w_ref, b_ref, o_ref):
        cv = c_ref[...]
        sc = cv * _sigmoid(cv)
        for l in range(DEPTH):
            o_ref[l] = _dot_hi(sc, w_ref[l]) + b_ref[l]

    return pl.pallas_call(body, out_shape=jax.ShapeDtypeStruct((DEPTH, 8, ADA_COLS), f32), name=name,
                          compiler_params=_cp(None))(c_all, ada_w, ada_b_cols)


def _adam_math(w, g, m, v):
    m = ADAM_B1 * m + (1.0 - ADAM_B1) * g
    v = ADAM_B2 * v + (1.0 - ADAM_B2) * (g * g)
    m_hat = m / (1.0 - ADAM_B1 ** ADAM_STEP)
    v_hat = v / (1.0 - ADAM_B2 ** ADAM_STEP)
    delta = -ADAM_LR * (m_hat / (jnp.sqrt(v_hat) + ADAM_EPS) + ADAM_WD * w)
    return delta, m, v


def _ada_bwd_adamw(c_all, dmod_cols, w, m, v, name):
    tr = 256

    def body(c_ref, dm_ref, w_ref, m_ref, v_ref, g_ref, d_ref, nm_ref, nv_ref):
        cv = c_ref[...]
        sc = cv * _sigmoid(cv)
        g = lax.dot_general(sc, dm_ref[0], (((0,), (0,)), ((), ())), precision=_HI, preferred_element_type=f32)
        g_ref[0] = g
        d_ref[0], nm_ref[0], nv_ref[0] = _adam_math(w_ref[0], g, m_ref[0], v_ref[0])

    blk = pl.BlockSpec((1, tr, ADA_COLS), lambda l, i: (l, i, 0))
    shp = jax.ShapeDtypeStruct((DEPTH, D, ADA_COLS), f32)
    return pl.pallas_call(
        body, grid=(DEPTH, D // tr),
        in_specs=[pl.BlockSpec((8, tr), lambda l, i: (0, i)), pl.BlockSpec((1, 8, ADA_COLS), lambda l, i: (l, 0, 0)), blk, blk, blk],
        out_specs=[blk, blk, blk, blk], out_shape=[shp, shp, shp, shp], name=name, compiler_params=_cp(("parallel", "parallel")),
    )(c_all, dmod_cols, w, m, v)


def _adamw(w, g, m, v, name):
    def body(w_ref, g_ref, m_ref, v_ref, d_ref, nm_ref, nv_ref):
        d_ref[...], nm_ref[...], nv_ref[...] = _adam_math(w_ref[...], g_ref[...], m_ref[...], v_ref[...])

    shp = jax.ShapeDtypeStruct(w.shape, f32)
    return pl.pallas_call(body, out_shape=[shp] * 3, name=name, compiler_params=_cp(None))(w, g, m, v)


def _tile2(r, c):
    if r <= 256 or r % 256 == 0:
        return _pick(r, (256,)), _pick(c, (1024,))
    return r, 128


def _adamw_layers(w, g_mine, g_other, m, v, c_idx, name):
    _, r, c = w.shape
    tr, tc = _tile2(r, c)

    def body(ci_ref, w_ref, gm_ref, go_ref, m_ref, v_ref, g_ref, d_ref, nm_ref, nv_ref):
        def update(g):
            g_ref[0] = g
            d_ref[0], nm_ref[0], nv_ref[0] = _adam_math(w_ref[0], g, m_ref[0], v_ref[0])

        mine = pl.program_id(0) == ci_ref[0]

        @pl.when(mine)
        def _():
            update(gm_ref[...])

        @pl.when(jnp.logical_not(mine))
        def _():
            update(go_ref[...])

    blk = pl.BlockSpec((1, tr, tc), lambda l, i, j, ci: (l, i, j))
    gblk = pl.BlockSpec((tr, tc), lambda l, i, j, ci: (i, j))
    shp = jax.ShapeDtypeStruct(w.shape, f32)
    return pl.pallas_call(
        body,
        grid_spec=pltpu.PrefetchScalarGridSpec(num_scalar_prefetch=1, grid=(DEPTH, r // tr, c // tc),
                                               in_specs=[blk, gblk, gblk, blk, blk], out_specs=[blk, blk, blk, blk]),
        out_shape=[shp, shp, shp, shp], name=name, compiler_params=_cp(("parallel", "parallel", "parallel")),
    )(c_idx, w, g_mine, g_other, m, v)


_MESH = pl.DeviceIdType.MESH
_AXES = ("x", "y", "c")
_HBM = pl.BlockSpec(memory_space=pltpu.HBM)


def _my_place():
    return tuple(lax.axis_index(a) for a in _AXES)


def _allreduce(buf, axes, name):
    r = buf.shape[0]
    n = len(axes)

    def body(x_ref, o_ref, rbuf, ssem, rsem):
        me = dict(zip(_AXES, _my_place()))
        o_ref[...] = x_ref[...]
        for k, ax in enumerate(axes):
            peer = tuple(1 - me[a] if a == ax else me[a] for a in _AXES)
            cp = pltpu.make_async_remote_copy(src_ref=o_ref, dst_ref=rbuf.at[k], send_sem=ssem.at[k], recv_sem=rsem.at[k],
                                              device_id=peer, device_id_type=_MESH)
            cp.start()
            cp.wait()
            o_ref[...] = o_ref[...] + rbuf[k]

    vm = pl.BlockSpec(memory_space=pltpu.VMEM)
    return pl.pallas_call(
        body, out_shape=jax.ShapeDtypeStruct((r, 128), f32), in_specs=[vm], out_specs=vm,
        scratch_shapes=[pltpu.VMEM((n, r, 128), f32), pltpu.SemaphoreType.DMA((n,)), pltpu.SemaphoreType.DMA((n,))],
        name=name, compiler_params=pltpu.CompilerParams(vmem_limit_bytes=VMEM_LIMIT),
    )(buf)


def _allreduce_halves(buf, name):
    r = buf.shape[0]
    h = r // 2
    assert r % 16 == 0

    def body(x_ref, o_ref, rbuf, ssem, rsem):
        x, y, c = _my_place()
        mine = pl.ds(pl.multiple_of(c * h, 8), h)
        other = pl.ds(pl.multiple_of((1 - c) * h, 8), h)
        o_ref[...] = x_ref[...]
        steps = [(other, (x, y, 1 - c)), (mine, (1 - x, y, c)), (mine, (x, 1 - y, c))]
        for k, (rows, peer) in enumerate(steps):
            cp = pltpu.make_async_remote_copy(src_ref=o_ref.at[rows], dst_ref=rbuf.at[k], send_sem=ssem.at[k], recv_sem=rsem.at[k],
                                              device_id=peer, device_id_type=_MESH)
            cp.start()
            cp.wait()
            o_ref[mine, :] = o_ref[mine, :] + rbuf[k]
        cp = pltpu.make_async_remote_copy(src_ref=o_ref.at[mine], dst_ref=rbuf.at[3], send_sem=ssem.at[3], recv_sem=rsem.at[3],
                                          device_id=(x, y, 1 - c), device_id_type=_MESH)
        cp.start()
        cp.wait()
        o_ref[other, :] = rbuf[3]

    vm = pl.BlockSpec(memory_space=pltpu.VMEM)
    return pl.pallas_call(
        body, out_shape=jax.ShapeDtypeStruct((r, 128), f32), in_specs=[vm], out_specs=vm,
        scratch_shapes=[pltpu.VMEM((4, h, 128), f32), pltpu.SemaphoreType.DMA((4,)), pltpu.SemaphoreType.DMA((4,))],
        name=name, compiler_params=pltpu.CompilerParams(vmem_limit_bytes=VMEM_LIMIT),
    )(buf)


def _other_chips(x, y):
    return [(1 - x, y), (x, 1 - y), (1 - x, 1 - y)]


def _gather_body(ins, outs, ssem, rsem):
    na = len(ins)
    x, y, c = _my_place()
    k_me = 2 * x + y
    sibling = (x, y, 1 - c)
    chips = _other_chips(x, y)
    slots = [2 * cx + cy for cx, cy in chips]
    half = [r.shape[0] // 2 if r.shape[0] % 32 == 0 else None for r in ins]

    def part(ref, a, core):
        return ref if half[a] is None else ref.at[pl.ds(core * half[a], half[a])]

    def rcopy(a, src, slot, core, to, idx):
        return pltpu.make_async_remote_copy(src_ref=src, dst_ref=part(outs[a].at[slot], a, core), send_sem=ssem.at[idx],
                                            recv_sem=rsem.at[idx], device_id=to, device_id_type=_MESH)

    sent = []
    for j, chip in enumerate(chips):
        for a in range(na):
            cp = rcopy(a, part(ins[a], a, c), k_me, c, (*chip, c), j * na + a)
            cp.start()
            sent.append(cp)
    for j, chip in enumerate(chips):
        for a in range(na):
            rcopy(a, part(ins[a], a, c), slots[j], c, (*chip, c), j * na + a).wait_recv()
            if half[a] is not None:
                cp = rcopy(a, part(outs[a].at[slots[j]], a, c), slots[j], c, sibling, (3 + j) * na + a)
                cp.start()
                sent.append(cp)
    for j in range(3):
        for a in range(na):
            if half[a] is not None:
                rcopy(a, part(ins[a], a, c), slots[j], 1 - c, sibling, (3 + j) * na + a).wait_recv()
    for cp in sent:
        cp.wait_send()


def _gather_layer_behind(shards, name, collective_id):
    na = len(shards)
    hbm = pltpu.MemorySpace.HBM
    ins = [jax.new_ref(s, memory_space=hbm) for s in shards]
    outs = [jax.empty_ref(jax.ShapeDtypeStruct((NSHARD,) + s.shape, s.dtype), memory_space=hbm) for s in shards]

    @pl.kernel(mesh=plsc.ScalarSubcoreMesh(axis_name="sequencer", num_cores=1), name=name,
               scratch_types=(pltpu.SemaphoreType.DMA((6 * na,)), pltpu.SemaphoreType.DMA((6 * na,))),
               compiler_params=pltpu.CompilerParams(collective_id=collective_id))
    def launch(ssem, rsem):
        x, y, c = _my_place()
        barrier = pltpu.get_barrier_semaphore()
        peers = [(*chip, c) for chip in _other_chips(x, y)] + [(x, y, 1 - c)]
        for peer in peers:
            pl.semaphore_signal(barrier, inc=1, device_id=peer, device_id_type=_MESH)
        pl.semaphore_wait(barrier, len(peers))
        _gather_body(ins, outs, ssem, rsem)

    launch()
    return [o[...] for o in outs]


def _with_own(gathered, own):
    xi, yi, _ = _my_place()
    whole = lax.dynamic_update_index_in_dim(gathered, own, 2 * xi + yi, 0)
    return [whole[k] for k in range(NSHARD)]


def _swap_layers(parts, name):
    na = len(parts)

    def body(*refs):
        ins, outs = refs[:2 * na], refs[2 * na:3 * na]
        ssem, rsem = refs[3 * na:]
        x, y, c = _my_place()

        def copy(a, layer):
            return pltpu.make_async_remote_copy(src_ref=ins[2 * a + layer], dst_ref=outs[a], send_sem=ssem.at[a], recv_sem=rsem.at[a],
                                                device_id=(x, y, 1 - c), device_id_type=_MESH)

        for layer in range(DEPTH):
            @pl.when(c == 1 - layer)
            def _():
                for a in range(na):
                    copy(a, layer).start()
        for a in range(na):
            copy(a, 0).wait()

    flat = [p for pair in parts for p in pair]
    return pl.pallas_call(
        body, out_shape=[jax.ShapeDtypeStruct(p0.shape, p0.dtype) for p0, _ in parts], in_specs=[_HBM] * (2 * na),
        out_specs=[_HBM] * na, scratch_shapes=[pltpu.SemaphoreType.DMA((na,)), pltpu.SemaphoreType.DMA((na,))], name=name,
    )(*flat)


def _scatter_shards_behind(sums, name, collective_id):
    na = len(sums)
    hbm = pltpu.MemorySpace.HBM
    ins = [jax.new_ref(p, memory_space=hbm) for p in sums]
    outs = [jax.empty_ref(jax.ShapeDtypeStruct((3,) + p.shape[1:], p.dtype), memory_space=hbm) for p in sums]

    @pl.kernel(mesh=plsc.ScalarSubcoreMesh(axis_name="sequencer", num_cores=1), name=name,
               scratch_types=(pltpu.SemaphoreType.DMA((3 * na,)), pltpu.SemaphoreType.DMA((3 * na,))),
               compiler_params=pltpu.CompilerParams(collective_id=collective_id))
    def launch(ssem, rsem):
        x, y, c = _my_place()
        barrier = pltpu.get_barrier_semaphore()
        chips = _other_chips(x, y)
        for chip in chips:
            pl.semaphore_signal(barrier, inc=1, device_id=(*chip, c), device_id_type=_MESH)
        pl.semaphore_wait(barrier, len(chips))
        cps = []
        for j, chip in enumerate(chips):
            kj = 2 * chip[0] + chip[1]
            for a in range(na):
                cps.append(pltpu.make_async_remote_copy(
                    src_ref=ins[a].at[kj], dst_ref=outs[a].at[j], send_sem=ssem.at[j * na + a], recv_sem=rsem.at[j * na + a],
                    device_id=(*chip, c), device_id_type=_MESH))
        for cp in cps:
            cp.start()
        for cp in cps:
            cp.wait()

    launch()
    return [o[...] for o in outs]


def _share_layers(finals, name):
    na = len(finals)

    def body(*refs):
        ins, outs = refs[:na], refs[na:2 * na]
        ssem, rsem = refs[2 * na:]
        x, y, c = _my_place()
        cps = [pltpu.make_async_remote_copy(src_ref=ins[a], dst_ref=outs[a], send_sem=ssem.at[a], recv_sem=rsem.at[a],
                                            device_id=(x, y, 1 - c), device_id_type=_MESH) for a in range(na)]
        for cp in cps:
            cp.start()
        for cp in cps:
            cp.wait()

    return pl.pallas_call(
        body, out_shape=[jax.ShapeDtypeStruct(p.shape, p.dtype) for p in finals], in_specs=[_HBM] * na, out_specs=[_HBM] * na,
        scratch_shapes=[pltpu.SemaphoreType.DMA((na,)), pltpu.SemaphoreType.DMA((na,))], name=name,
    )(*finals)


def _add_own_layer(part0, part1, recv, c_idx, name):
    ns, r, c = recv.shape
    tr, tc = _tile2(r, c)

    def body(ci_ref, p0_ref, p1_ref, r_ref, o_ref, ob_ref):
        def add(p_ref):
            t = p_ref[...] + r_ref[...]
            o_ref[...] = t
            ob_ref[...] = t.astype(bf16)

        @pl.when(ci_ref[0] == 0)
        def _():
            add(p0_ref)

        @pl.when(ci_ref[0] == 1)
        def _():
            add(p1_ref)

    blk = pl.BlockSpec((1, tr, tc), lambda k, i, j, ci: (k, i, j))
    blk0 = pl.BlockSpec((1, tr, tc), lambda k, i, j, ci: (k * (1 - ci[0]), i * (1 - ci[0]), j * (1 - ci[0])))
    blk1 = pl.BlockSpec((1, tr, tc), lambda k, i, j, ci: (k * ci[0], i * ci[0], j * ci[0]))
    return pl.pallas_call(
        body,
        grid_spec=pltpu.PrefetchScalarGridSpec(num_scalar_prefetch=1, grid=(ns, r // tr, c // tc), in_specs=[blk0, blk1, blk],
                                               out_specs=[blk, blk]),
        out_shape=[jax.ShapeDtypeStruct((ns, r, c), f32), jax.ShapeDtypeStruct((ns, r, c), bf16)], name=name,
        compiler_params=_cp(("arbitrary", "arbitrary", "arbitrary")),
    )(c_idx, part0, part1, recv)


def _add_own_shard(sums, recv, k_idx, name):
    _, r, c = sums.shape
    tr, tc = _tile2(r, c)

    def body(ki_ref, s_ref, r_ref, o_ref):
        o_ref[...] = ((s_ref[0] + r_ref[0].astype(f32)) + r_ref[1].astype(f32)) + r_ref[2].astype(f32)

    return pl.pallas_call(
        body,
        grid_spec=pltpu.PrefetchScalarGridSpec(
            num_scalar_prefetch=1, grid=(r // tr, c // tc),
            in_specs=[pl.BlockSpec((1, tr, tc), lambda i, j, ki: (ki[0], i, j)), pl.BlockSpec((3, tr, tc), lambda i, j, ki: (0, i, j))],
            out_specs=pl.BlockSpec((tr, tc), lambda i, j, ki: (i, j))),
        out_shape=jax.ShapeDtypeStruct((r, c), f32), name=name, compiler_params=_cp(("parallel", "parallel")),
    )(k_idx, sums, recv)


def _reduce_scatter(parts, tag, meanwhile):
    x, y, c = _my_place()
    c_idx = jnp.reshape(c, (1,)).astype(jnp.int32)
    k_idx = jnp.reshape(2 * x + y, (1,)).astype(jnp.int32)
    na = len(parts)
    recv = _swap_layers(parts, f"rs_swap{tag}")
    sums = [_add_own_layer(parts[a][0], parts[a][1], recv[a], c_idx, f"rs_add_layer{tag}_{a}") for a in range(na)]
    recv = _scatter_shards_behind([sb for _, sb in sums], f"rs_scatter{tag}", 3)
    recv, side = lax.optimization_barrier((recv, meanwhile()))
    finals = [_add_own_shard(sums[a][0], recv[a], k_idx, f"rs_add_shard{tag}_{a}") for a in range(na)]
    return finals, _share_layers(finals, f"rs_share{tag}"), c_idx, side


_SMALL = [("ada_b", (DEPTH, 3 * D)), ("norm_w", (DEPTH, D)), ("gm_ln_w", (DEPTH, D)), ("gm_ln_b", (DEPTH, D)),
          ("gm_ws", (DEPTH, NG, CH, CH)), ("gm_bs", (DEPTH, NG, CH)), ("conv_b", (DEPTH, CONVD)), ("dt_bias", (DEPTH, NH)),
          ("a_log", (DEPTH, NH)), ("d_skip", (DEPTH, NH)), ("ssm_norm_w", (DEPTH, DIN)), ("final_norm_w", (D,))]


def _rows_of(shape):
    n = 1
    for d in shape:
        n *= d
    return -(-n // 1024) * 8


def _pack(arrays):
    rows = []
    for a in arrays:
        flat = a.reshape(-1)
        r = _rows_of(a.shape)
        rows.append(jnp.pad(flat, (0, r * 128 - flat.shape[0])).reshape(r, 128))
    return jnp.concatenate(rows, axis=0)


def _unpack(buf, shapes):
    out, at = [], 0
    for shp in shapes:
        r = _rows_of(shp)
        n = 1
        for d in shp:
            n *= d
        out.append(buf[at:at + r].reshape(-1)[:n].reshape(shp))
        at += r
    return out


def kernel(x, c, ada_w, ada_b, norm_w, w_in, gm_ln_w, gm_ln_b, gm_ws, gm_bs, conv_w, conv_b, dt_bias, a_log, d_skip, ssm_norm_w, w_proj_a, w_proj_b, w_out, final_norm_w, loss_target, m_ada_w, m_ada_b, m_norm_w, m_w_in, m_gm_ln_w, m_gm_ln_b, m_gm_ws, m_gm_bs, m_conv_w, m_conv_b, m_dt_bias, m_a_log, m_d_skip, m_ssm_norm_w, m_w_proj_a, m_w_proj_b, m_w_out, m_final_norm_w, v_ada_w, v_ada_b, v_norm_w, v_w_in, v_gm_ln_w, v_gm_ln_b, v_gm_ws, v_gm_bs, v_conv_w, v_conv_b, v_dt_bias, v_a_log, v_d_skip, v_ssm_norm_w, v_w_proj_a, v_w_proj_b, v_w_out, v_final_norm_w):
    xi, yi, ci = _my_place()
    k_me = 2 * xi + yi
    b_me = 4 * xi + 2 * yi + ci
    w = dict(ada_b=ada_b, norm_w=norm_w, gm_ln_w=gm_ln_w, gm_ln_b=gm_ln_b, gm_ws=gm_ws, gm_bs=gm_bs, conv_b=conv_b, dt_bias=dt_bias,
             a_log=a_log, d_skip=d_skip, ssm_norm_w=ssm_norm_w, final_norm_w=final_norm_w)
    m = dict(ada_b=m_ada_b, norm_w=m_norm_w, gm_ln_w=m_gm_ln_w, gm_ln_b=m_gm_ln_b, gm_ws=m_gm_ws, gm_bs=m_gm_bs, conv_b=m_conv_b,
             dt_bias=m_dt_bias, a_log=m_a_log, d_skip=m_d_skip, ssm_norm_w=m_ssm_norm_w, final_norm_w=m_final_norm_w)
    v = dict(ada_b=v_ada_b, norm_w=v_norm_w, gm_ln_w=v_gm_ln_w, gm_ln_b=v_gm_ln_b, gm_ws=v_gm_ws, gm_bs=v_gm_bs, conv_b=v_conv_b,
             dt_bias=v_dt_bias, a_log=v_a_log, d_skip=v_d_skip, ssm_norm_w=v_ssm_norm_w, final_norm_w=v_final_norm_w)

    c_slot = lax.dynamic_update_slice(jnp.zeros((8, D), f32), c, (b_me, 0))
    c_all = _allreduce(c_slot.reshape(64, 128), _AXES, "gather_c").reshape(8, D)
    ada_b_cols = lax.dynamic_slice(ada_b, (0, k_me * ADA_COLS), (DEPTH, ADA_COLS)).reshape(DEPTH, 1, ADA_COLS)
    mod_cols = _ada_fwd(c_all, ada_w, ada_b_cols, "ada_fwd")
    mod_slot = lax.dynamic_update_slice(jnp.zeros((DEPTH, 8, 3 * D), f32), mod_cols, (0, 0, k_me * ADA_COLS))
    mod_all = _allreduce(mod_slot.reshape(-1, 128), ("x", "y"), "gather_mod").reshape(DEPTH, 8, 3 * D)
    mod_me = lax.dynamic_slice(mod_all, (0, b_me, 0), (DEPTH, 1, 3 * D))
    mods = [(mod_me[l, :, 0:D], mod_me[l, :, D:2 * D], mod_me[l, :, 2 * D:3 * D]) for l in range(DEPTH)]

    rows_sh = jnp.concatenate([w_proj_a, w_proj_b, w_out], axis=1).astype(bf16)
    win_sh = w_in.astype(bf16)
    first = _gather_layer_behind([win_sh[0], conv_w[0]], "gather_l0", 1)
    first, mods, next_in, later = lax.optimization_barrier((first, mods, [rows_sh[0]], [win_sh[1], conv_w[1], rows_sh[1]]))
    first_rows = _gather_layer_behind(next_in, "gather_l0_merge", 5)
    second = _gather_layer_behind(later, "gather_l1", 2)
    others = [(first[0], first[1], first_rows[0]), second]

    def merge_matrices(rows_g, l):
        rows_l = _with_own(rows_g, rows_sh[l])
        return (jnp.concatenate([t[0:RA] for t in rows_l], axis=0), jnp.concatenate([t[RA:RA + RB] for t in rows_l], axis=0),
                jnp.concatenate([t[RA + RB:] for t in rows_l], axis=0))

    def operands_of(l, x_in):
        win_g, conv_g, rows_g = others[l]
        if l == 1:
            (win_g, conv_g, rows_g), x_in = lax.optimization_barrier(((win_g, conv_g, rows_g), x_in))
        w_in_l = jnp.concatenate(_with_own(win_g, win_sh[l]), axis=1)
        cw_l = jnp.concatenate(_with_own(conv_g, conv_w[l]), axis=1)
        wa_l, wb_l, wo_l = merge_matrices(rows_g, l) if l == 1 else (None, None, None)
        p = _layer_operands(w_in_l, cw_l, wa_l, wb_l, wo_l, norm_w[l], gm_ln_w[l], gm_ln_b[l], gm_ws[l], gm_bs[l], conv_b[l],
                            dt_bias[l], a_log[l], d_skip[l], ssm_norm_w[l])
        if l == 0:
            def late(yb):
                rows_now, yb = lax.optimization_barrier((rows_g, yb))
                return (*merge_matrices(rows_now, l), yb)
            p["merge_weights"] = late
        return p, x_in

    loss_parts, dx, grads, dfnw = _local_step(x[0], loss_target[0], mods, operands_of, final_norm_w)

    s_in = N_IN // NSHARD
    tr_ = lambda t: jnp.swapaxes(t, 1, 2)
    g_in = [grads[l]["w_in_t"].reshape(NSHARD, s_in, D) for l in range(DEPTH)]
    g_rows = [jnp.concatenate([grads[l]["w_proj_a"].reshape(NSHARD, RA, D), grads[l]["w_proj_b"].reshape(NSHARD, RB, D),
                               grads[l]["w_out"].reshape(NSHARD, RA, D)], axis=1) for l in range(DEPTH)]
    g_conv = [grads[l]["conv_w"].reshape(4, NSHARD, D).transpose(1, 0, 2) for l in range(DEPTH)]

    dmod_slot = lax.dynamic_update_slice(jnp.zeros((DEPTH, 8, 3 * D), f32),
                                         jnp.stack([grads[l]["mod"] for l in range(DEPTH)]).reshape(DEPTH, 1, 3 * D), (0, b_me, 0))
    small_g = {n: (dfnw if n == "final_norm_w" else jnp.stack([grads[l]["mod" if n == "ada_b" else n] for l in range(DEPTH)]))
               for n, _ in _SMALL}
    n_small = sum(_rows_of(s) for _, s in _SMALL)
    n_dmod = _rows_of(dmod_slot.shape)

    def small_work():
        tail = [jnp.zeros((8, 128), f32)] if (n_small + n_dmod + 8) % 16 else []
        packed = _allreduce_halves(_pack([small_g[n] for n, _ in _SMALL] + [dmod_slot, loss_parts] + tail), "allreduce_small")
        dmod_all = packed[n_small:n_small + n_dmod].reshape(DEPTH, 8, 3 * D)
        dmod_cols = lax.dynamic_slice(dmod_all, (0, 0, k_me * ADA_COLS), (DEPTH, 8, ADA_COLS))
        return packed, _ada_bwd_adamw(c_all, dmod_cols, ada_w, m_ada_w, v_ada_w, "ada_bwd_adamw")

    (f_in, f_rows, f_conv), (o_in, o_rows, o_conv), c_idx, (packed, ada_out) = _reduce_scatter([g_in, g_rows, g_conv], "", small_work)
    g_ada, d_ada, nm_ada, nv_ada = ada_out
    gr_in, d_in, nm_in, nv_in = [tr_(t) for t in _adamw_layers(tr_(w_in), f_in, o_in, tr_(m_w_in), tr_(v_w_in), c_idx, "adamw_w_in")]
    rows_of = dict(w_proj_a=(0, RA), w_proj_b=(RA, RA + RB), w_out=(RA + RB, 2 * RA + RB))
    rows_upd = {n: _adamw_layers(wv, f_rows[lo:hi], o_rows[lo:hi], mv, vv, c_idx, f"adamw_{n}")
                for (n, (lo, hi)), wv, mv, vv in zip(rows_of.items(), (w_proj_a, w_proj_b, w_out), (m_w_proj_a, m_w_proj_b, m_w_out),
                                                     (v_w_proj_a, v_w_proj_b, v_w_out))}
    gr_conv = jnp.where(ci == 0, jnp.stack([f_conv, o_conv]), jnp.stack([o_conv, f_conv]))

    g_small = packed[0:n_small]
    loss = jnp.sum(packed[n_small + n_dmod:])
    small_gw = jnp.concatenate([g_small, _pack([gr_conv])], axis=0)
    d_s, nm_s, nv_s = _adamw(_pack([w[n] for n, _ in _SMALL] + [conv_w]), small_gw,
                             _pack([m[n] for n, _ in _SMALL] + [m_conv_w]), _pack([v[n] for n, _ in _SMALL] + [v_conv_w]), "adamw_small")
    shapes = [s for _, s in _SMALL] + [conv_w.shape]
    names = [n for n, _ in _SMALL] + ["conv_w"]
    g_d = dict(zip(names, _unpack(small_gw, shapes)))
    d_d = dict(zip(names, _unpack(d_s, shapes)))
    nm_d = dict(zip(names, _unpack(nm_s, shapes)))
    nv_d = dict(zip(names, _unpack(nv_s, shapes)))

    order = ["ada_w", "ada_b", "norm_w", "w_in", "gm_ln_w", "gm_ln_b", "gm_ws", "gm_bs", "conv_w", "conv_b", "dt_bias", "a_log",
             "d_skip", "ssm_norm_w", "w_proj_a", "w_proj_b", "w_out", "final_norm_w"]
    outs = []
    for i, (w_in_out, ada_out_i, small) in enumerate(((gr_in, g_ada, g_d), (d_in, d_ada, d_d), (nm_in, nm_ada, nm_d),
                                                      (nv_in, nv_ada, nv_d))):
        t = dict(small, ada_w=ada_out_i, w_in=w_in_out, **{n: rows_upd[n][i] for n in rows_of})
        outs += [t[n] for n in order]
    return (loss, dx.reshape(1, -1, D), *outs)
```

```python
import jax
import jax.numpy as jnp
from jax import lax
from jax.experimental import pallas as pl
from jax.experimental.pallas import tpu as pltpu
from jax.experimental.pallas import tpu_sc as plsc

f32 = jnp.float32
bf16 = jnp.bfloat16

D = 1024
DEPTH = 2
EPS = 1e-6
CH = 128
NG = 8
HPG = 4
HD = 64
NH = NG * HPG
NST = 128
DIN = 2048
CONVD = 4096
GW = DIN // NG
PB = CONVD + DIN + 256
PA = 3 * D
PG = 2 * D
N_IN = 11296
NSHARD = 4
RA, RB = D // NSHARD, DIN // NSHARD
V7X_VMEM_BYTES = 64 * 2 ** 20
VMEM_LIMIT = V7X_VMEM_BYTES - 8 * 2 ** 20

ADAM_LR, ADAM_B1, ADAM_B2, ADAM_EPS, ADAM_WD, ADAM_STEP = 0.001, 0.9, 0.999, 1e-08, 0.01, 10

_HI = lax.Precision.HIGHEST


def _cp(sem):
    return pltpu.CompilerParams(dimension_semantics=sem, vmem_limit_bytes=VMEM_LIMIT)


def _sigmoid(x):
    return 0.5 * jnp.tanh(0.5 * x) + 0.5


def _silu_and_grad(x):
    s = _sigmoid(x)
    return x * s, s * (1.0 + x * (1.0 - s))


_GELU_K = 0.7978845608028654
_GELU_C = 0.044715


def _gelu_and_grad(x):
    x2 = x * x
    t = jnp.tanh(_GELU_K * (x + _GELU_C * x * x2))
    g = 0.5 * x * (1.0 + t)
    dg = 0.5 * (1.0 + t) + 0.5 * x * (1.0 - t * t) * _GELU_K * (1.0 + 3.0 * _GELU_C * x2)
    return g, dg


def _gelu(x):
    t = jnp.tanh(_GELU_K * (x + _GELU_C * x * x * x))
    return 0.5 * x * (1.0 + t)


def _softplus(x):
    return jnp.maximum(x, 0.0) + jnp.log(1.0 + jnp.exp(-jnp.abs(x)))


def _dot(a, b):
    return jnp.dot(a, b, preferred_element_type=f32)


def _dot_nt(a, b):
    return lax.dot_general(a, b, (((1,), (1,)), ((), ())), preferred_element_type=f32)


def _dot_tn(a, b):
    return lax.dot_general(a, b, (((0,), (0,)), ((), ())), preferred_element_type=f32)


def _dot_hi(a, b):
    return jnp.dot(a, b, precision=_HI, preferred_element_type=f32)


def _rmsmod_fwd(x, nw, scale, shift, name):
    s = x.shape[0]
    ts = min(1024, s)

    def body(x_ref, nw_ref, sc_ref, sh_ref, h_ref):
        xv = x_ref[...]
        r = lax.rsqrt(jnp.mean(xv * xv, axis=-1, keepdims=True) + EPS)
        h_ref[...] = ((xv * r) * nw_ref[...] * (1.0 + sc_ref[...]) + sh_ref[...]).astype(bf16)

    row = pl.BlockSpec((1, D), lambda i: (0, 0))
    tile = pl.BlockSpec((ts, D), lambda i: (i, 0))
    return pl.pallas_call(
        body, grid=(s // ts,), in_specs=[tile, row, row, row], out_specs=tile,
        out_shape=jax.ShapeDtypeStruct((s, D), bf16), name=name, compiler_params=_cp(("parallel",)),
    )(x, nw, scale, shift)


def _rmsmod_bwd(dh, x, dres, nw, scale, name):
    s = x.shape[0]
    ts = min(1024, s)

    def body(dh_ref, x_ref, dres_ref, nw_ref, sc_ref, dx_ref, dsc_ref, dsh_ref, dnw_ref):
        @pl.when(pl.program_id(0) == 0)
        def _():
            dsc_ref[...] = jnp.zeros_like(dsc_ref)
            dsh_ref[...] = jnp.zeros_like(dsh_ref)
            dnw_ref[...] = jnp.zeros_like(dnw_ref)
        xv = x_ref[...]
        dhv = dh_ref[...]
        r = lax.rsqrt(jnp.mean(xv * xv, axis=-1, keepdims=True) + EPS)
        xn = xv * r
        one_sc = 1.0 + sc_ref[...]
        dsc_ref[...] += jnp.sum(dhv * xn * nw_ref[...], axis=0, keepdims=True)
        dsh_ref[...] += jnp.sum(dhv, axis=0, keepdims=True)
        dnw_ref[...] += jnp.sum(dhv * xn * one_sc, axis=0, keepdims=True)
        dxn = dhv * (nw_ref[...] * one_sc)
        dx_ref[...] = r * (dxn - xn * jnp.mean(dxn * xn, axis=-1, keepdims=True)) + dres_ref[...]

    row = pl.BlockSpec((1, D), lambda i: (0, 0))
    tile = pl.BlockSpec((ts, D), lambda i: (i, 0))
    vec = jax.ShapeDtypeStruct((1, D), f32)
    return pl.pallas_call(
        body, grid=(s // ts,), in_specs=[tile, tile, tile, row, row], out_specs=[tile, row, row, row],
        out_shape=[jax.ShapeDtypeStruct((s, D), f32), vec, vec, vec], name=name, compiler_params=_cp(("arbitrary",)),
    )(dh, x, dres, nw, scale)


def _pick(n, prefs):
    for p in prefs:
        if n % p == 0:
            return p
    return n


def _mm(a, b, out_dtype, name, c_in=None, trans_b=False):
    m, k = a.shape
    n = b.shape[0] if trans_b else b.shape[1]
    tn = _pick(n, (1280, 1024, 512))
    tk = _pick(k, (1280, 1024, 512))
    nk = k // tk
    one_pass = nk == 1 and c_in is None
    tm = _pick(m, (2048, 1024, 512, 256) if one_pass else (1024, 512, 256))
    dot = _dot_nt if trans_b else _dot

    def body(*refs):
        if one_pass:
            a_ref, b_ref, o_ref = refs
            o_ref[...] = dot(a_ref[...], b_ref[...]).astype(out_dtype)
            return
        if c_in is not None:
            a_ref, b_ref, c_ref, o_ref, acc = refs
        else:
            a_ref, b_ref, o_ref, acc = refs
        kk = pl.program_id(2)

        @pl.when(kk == 0)
        def _():
            if c_in is not None:
                acc[...] = c_ref[...]
            else:
                acc[...] = jnp.zeros_like(acc)
        acc[...] += dot(a_ref[...], b_ref[...])

        @pl.when(kk == nk - 1)
        def _():
            o_ref[...] = acc[...].astype(out_dtype)

    b_spec = pl.BlockSpec((tn, tk), lambda j, i, kk: (j, kk)) if trans_b else pl.BlockSpec((tk, tn), lambda j, i, kk: (kk, j))
    in_specs = [pl.BlockSpec((tm, tk), lambda j, i, kk: (i, kk)), b_spec]
    args = [a, b]
    if c_in is not None:
        in_specs.append(pl.BlockSpec((tm, tn), lambda j, i, kk: (i, j)))
        args.append(c_in)
    return pl.pallas_call(
        body, grid=(n // tn, m // tm, nk), in_specs=in_specs, out_specs=pl.BlockSpec((tm, tn), lambda j, i, kk: (i, j)),
        out_shape=jax.ShapeDtypeStruct((m, n), out_dtype), scratch_shapes=[] if one_pass else [pltpu.VMEM((tm, tn), f32)],
        name=name, compiler_params=_cp(("parallel", "parallel", "arbitrary")),
    )(*args)


def _norm_mm(x, nw, scale, shift, b, name, emit_h=False):
    m = x.shape[0]
    n = b.shape[1]
    tm = _pick(m, (1024, 512, 256))
    tn = n if emit_h else _pick(n, (1280, 1024, 512))

    def body(x_ref, nw_ref, sc_ref, sh_ref, b_ref, o_ref, *h_ref):
        xv = x_ref[...]
        r = lax.rsqrt(jnp.mean(xv * xv, axis=-1, keepdims=True) + EPS)
        h = ((xv * r) * nw_ref[...] * (1.0 + sc_ref[...]) + sh_ref[...]).astype(bf16)
        o_ref[...] = _dot(h, b_ref[...])
        if emit_h:
            h_ref[0][...] = h

    row = pl.BlockSpec((1, D), lambda j, i: (0, 0))
    out_specs = [pl.BlockSpec((tm, tn), lambda j, i: (i, j))]
    out_shape = [jax.ShapeDtypeStruct((m, n), f32)]
    if emit_h:
        out_specs.append(pl.BlockSpec((tm, D), lambda j, i: (i, 0)))
        out_shape.append(jax.ShapeDtypeStruct((m, D), bf16))
    return pl.pallas_call(
        body, grid=(n // tn, m // tm),
        in_specs=[pl.BlockSpec((tm, D), lambda j, i: (i, 0)), row, row, row, pl.BlockSpec((D, tn), lambda j, i: (0, j))],
        out_specs=out_specs, out_shape=out_shape, name=name, compiler_params=_cp(("arbitrary", "arbitrary")),
    )(x, nw, scale, shift, b)


def _mm_tn(a, b, name):
    t, k1 = a.shape
    n = b.shape[1]
    t1 = _pick(k1, (1280, 1024, 512))
    tn = _pick(n, (1280, 1024, 512))
    tt = _pick(t, (2048, 1024, 512, 256))
    nt = t // tt

    def body(a_ref, b_ref, o_ref):
        tt_i = pl.program_id(2)

        @pl.when(tt_i == 0)
        def _():
            o_ref[...] = jnp.zeros_like(o_ref)
        o_ref[...] += _dot_tn(a_ref[...], b_ref[...])

    return pl.pallas_call(
        body, grid=(k1 // t1, n // tn, nt),
        in_specs=[pl.BlockSpec((tt, t1), lambda i, j, tt_i: (tt_i, i)), pl.BlockSpec((tt, tn), lambda i, j, tt_i: (tt_i, j))],
        out_specs=pl.BlockSpec((t1, tn), lambda i, j, tt_i: (i, j)),
        out_shape=jax.ShapeDtypeStruct((k1, n), f32), name=name,
        compiler_params=_cp(("parallel", "parallel", "arbitrary")),
    )(a, b)


def _ln_stats(v):
    mu = jnp.mean(v, axis=-1, keepdims=True)
    vc = v - mu
    rstd = lax.rsqrt(jnp.mean(vc * vc, axis=-1, keepdims=True) + EPS)
    return vc * rstd, rstd


def _mix(w_ref, vl):
    return jnp.concatenate([_dot(w_ref[g], vl[:, g * CH:(g + 1) * CH]) for g in range(NG)], axis=1)


def _branch_a_fwd(proj_a, lnw, lnb, wsm, bsf, name):
    s = proj_a.shape[0]
    ta = min(512, s)

    def body(pu_ref, pv_ref, pz_ref, lnw_ref, lnb_ref, w_ref, bs_ref, ya_ref):
        for c in range(ta // CH):
            rows = pl.ds(c * CH, CH)
            vh, _ = _ln_stats(_gelu(pv_ref[rows, :]))
            vl = (vh * lnw_ref[...] + lnb_ref[...]).astype(bf16)
            mixed = _mix(w_ref, vl) + bs_ref[...]
            pz = pz_ref[rows, :]
            ya_ref[rows, :] = (_gelu(pu_ref[rows, :]) * mixed * (pz * _sigmoid(pz))).astype(bf16)

    row = pl.BlockSpec((1, D), lambda i: (0, 0))
    return pl.pallas_call(
        body, grid=(s // ta,),
        in_specs=[pl.BlockSpec((ta, D), lambda i: (i, 0)), pl.BlockSpec((ta, D), lambda i: (i, 1)),
                  pl.BlockSpec((ta, D), lambda i: (i, 2)), row, row,
                  pl.BlockSpec((NG, CH, CH), lambda i: (0, 0, 0)), pl.BlockSpec((CH, D), lambda i: (0, 0))],
        out_specs=pl.BlockSpec((ta, D), lambda i: (i, 0)),
        out_shape=jax.ShapeDtypeStruct((s, D), bf16), name=name, compiler_params=_cp(("parallel",)),
    )(proj_a, proj_a, proj_a, lnw, lnb, wsm, bsf)


def _branch_a_bwd(proj_a, dya, lnw, lnb, wsm, wsm_t, bsf, name):
    s = proj_a.shape[0]
    ta = min(512, s)

    def body(pu_ref, pv_ref, pz_ref, dya_ref, lnw_ref, lnb_ref, w_ref, wt_ref, bs_ref,
             dp_ref, dws_ref, dbs_ref, dlnw_ref, dlnb_ref):
        @pl.when(pl.program_id(0) == 0)
        def _():
            dws_ref[...] = jnp.zeros_like(dws_ref)
            dbs_ref[...] = jnp.zeros_like(dbs_ref)
            dlnw_ref[...] = jnp.zeros_like(dlnw_ref)
            dlnb_ref[...] = jnp.zeros_like(dlnb_ref)
        for c in range(ta // CH):
            rows = pl.ds(c * CH, CH)
            u, du = _gelu_and_grad(pu_ref[rows, :])
            v, dv_act = _gelu_and_grad(pv_ref[rows, :])
            zg, dzg = _silu_and_grad(pz_ref[rows, :])
            vh, rstd = _ln_stats(v)
            vl = (vh * lnw_ref[...] + lnb_ref[...]).astype(bf16)
            mixed = _mix(w_ref, vl) + bs_ref[...]
            dy = dya_ref[rows, :].astype(f32)
            dmixed = dy * u * zg
            dp_ref[rows, 0:D] = (dy * mixed * zg * du).astype(bf16)
            dp_ref[rows, 2 * D:3 * D] = (dy * u * mixed * dzg).astype(bf16)
            dmb = dmixed.astype(bf16)
            dbs_ref[...] += dmixed
            dvl = _mix(wt_ref, dmb)
            for g in range(NG):
                cols = slice(g * CH, (g + 1) * CH)
                dws_ref[g] += _dot_nt(dmb[:, cols], vl[:, cols])
            dlnw_ref[...] += jnp.sum(dvl * vh, axis=0, keepdims=True)
            dlnb_ref[...] += jnp.sum(dvl, axis=0, keepdims=True)
            dvh = dvl * lnw_ref[...]
            dv = rstd * (dvh - jnp.mean(dvh, axis=-1, keepdims=True) - vh * jnp.mean(dvh * vh, axis=-1, keepdims=True))
            dp_ref[rows, D:2 * D] = (dv * dv_act).astype(bf16)

    row = pl.BlockSpec((1, D), lambda i: (0, 0))
    wspec = pl.BlockSpec((NG, CH, CH), lambda i: (0, 0, 0))
    full = pl.BlockSpec((CH, D), lambda i: (0, 0))
    return pl.pallas_call(
        body, grid=(s // ta,),
        in_specs=[pl.BlockSpec((ta, D), lambda i: (i, 0)), pl.BlockSpec((ta, D), lambda i: (i, 1)),
                  pl.BlockSpec((ta, D), lambda i: (i, 2)), pl.BlockSpec((ta, D), lambda i: (i, 0)),
                  row, row, wspec, wspec, full],
        out_specs=[pl.BlockSpec((ta, PA), lambda i: (i, 0)), wspec, full, row, row],
        out_shape=[jax.ShapeDtypeStruct((s, PA), bf16), jax.ShapeDtypeStruct((NG, CH, CH), f32),
                   jax.ShapeDtypeStruct((CH, D), f32), jax.ShapeDtypeStruct((1, D), f32), jax.ShapeDtypeStruct((1, D), f32)],
        name=name, compiler_params=_cp(("arbitrary",)),
    )(proj_a, proj_a, proj_a, dya, lnw, lnb, wsm, wsm_t, bsf)


GB = GW + 2 * NST


def _group_major(xs, b, c):
    lead = xs.shape[:-1]
    return jnp.concatenate([xs.reshape(lead + (NG, GW)), b.reshape(lead + (NG, NST)), c.reshape(lead + (NG, NST))],
                           axis=-1).reshape(lead + (CONVD,))


def _from_group_major(t):
    lead = t.shape[:-1]
    t = t.reshape(lead + (NG, GB))
    return jnp.concatenate([t[..., 0:GW].reshape(lead + (DIN,)), t[..., GW:GW + NST].reshape(lead + (NG * NST,)),
                            t[..., GW + NST:GB].reshape(lead + (NG * NST,))], axis=-1)


def _rows_from_group_major(t):
    t = t.reshape(NG, GB, t.shape[-1])
    return jnp.concatenate([t[:, 0:GW].reshape(DIN, -1), t[:, GW:GW + NST].reshape(NG * NST, -1),
                            t[:, GW + NST:GB].reshape(NG * NST, -1)], axis=0)


def _shift_rows(x, prev8, j):
    xr = pltpu.roll(x, j, 0)
    fix = pltpu.roll(prev8, j, 0)
    rid = lax.broadcasted_iota(jnp.int32, (8, x.shape[1]), 0)
    top = jnp.where(rid < j, fix, xr[0:8])
    return jnp.concatenate([top, xr[8:]], axis=0)


def _shift_rows_up(d, next8, j):
    dr = pltpu.roll(d, CH - j, 0)
    fix = pltpu.roll(next8, 8 - j, 0)
    rid = lax.broadcasted_iota(jnp.int32, (8, d.shape[1]), 0)
    bot = jnp.where(rid >= 8 - j, fix, dr[CH - 8:CH])
    return jnp.concatenate([dr[0:CH - 8], bot], axis=0)


def _conv_pre(x, prev8, cw_ref, cb_ref, cols):
    shifted = [_shift_rows(x, prev8, j) for j in (1, 2, 3)]
    conv = cb_ref[:, cols] + cw_ref[3:4, cols] * x
    for j in (1, 2, 3):
        conv = conv + cw_ref[3 - j:4 - j, cols] * shifted[j - 1]
    return conv, shifted


def _tril_mask():
    return lax.broadcasted_iota(jnp.int32, (CH, CH), 0) >= lax.broadcasted_iota(jnp.int32, (CH, CH), 1)


def _sum_all(v):
    return jnp.sum(jnp.sum(v, axis=0, keepdims=True), axis=1, keepdims=True)


def _lanes(g, width, base=0):
    return pl.ds(pl.multiple_of(base + g * width, width), width)


def _branch_b_fwd(proj_b, cw, cb, dtb, a_row, dkc, snw, name):
    s = proj_b.shape[0]
    nc = s // CH

    def body(xbc_ref, sz_ref, dtr_ref, cw_ref, cb_ref, dtb_ref, a_ref, dkc_ref, snw_ref,
             yb_ref, y_ref, st_ref, cv_ref, prev8, state, acst_s):
        @pl.when(pl.program_id(0) == 0)
        def _():
            prev8[...] = jnp.zeros_like(prev8)
            state[...] = jnp.zeros_like(state)
        st_ref[0] = state[...].astype(bf16)
        mask = _tril_mask()
        dt_all = _softplus(dtr_ref[:, 0:CH] + dtb_ref[...])
        acs_all = _dot_hi(mask.astype(f32), dt_all * a_ref[...])
        acst_s[...] = acs_all.T

        def group(g, carry):
            cols = _lanes(g, GB)
            gcols = _lanes(g, GW)
            x = xbc_ref[:, cols]
            conv, _ = _conv_pre(x, prev8[:, cols], cw_ref, cb_ref, cols)
            prev8[:, cols] = x[CH - 8:CH]
            cv_ref[:, cols] = conv.astype(bf16)
            xc = conv * _sigmoid(conv)
            xs = xc[:, 0:GW]
            bg = xc[:, GW:GW + NST].astype(bf16)
            cg = xc[:, GW + NST:GB].astype(bf16)
            back = lax.rem(CH - HPG * g, CH)
            dt = pltpu.roll(dt_all, back, 1)
            acs = pltpu.roll(acs_all, back, 1)
            cbm = _dot_nt(cg, bg)
            dkc_g = dkc_ref[:, gcols]
            y_parts = []
            for r in range(HPG):
                colb = jnp.broadcast_to(acs[:, r:r + 1], (CH, CH))
                row = acst_s[pl.ds(g * HPG + r, 1), :]
                lmat = jnp.exp(jnp.where(mask, colb - row, -jnp.inf))
                xr = xs[:, r * HD:(r + 1) * HD]
                xd = xr * dt[:, r:r + 1]
                sp = state[g * HPG + r]
                col = colb[:, 0:HD]
                alast = colb[CH - 1:CH, 0:HD]
                y_r = _dot((cbm * lmat).astype(bf16), xd.astype(bf16))
                y_r = y_r + jnp.exp(col) * _dot_nt(cg, sp.astype(bf16))
                y_parts.append(y_r + xr * dkc_g[:, r * HD:(r + 1) * HD])
                cs = _dot_tn((xd * jnp.exp(alast - col)).astype(bf16), bg)
                state[g * HPG + r] = jnp.exp(colb[CH - 1:CH, :]) * sp + cs
            y = jnp.concatenate(y_parts, axis=1)
            szv = sz_ref[:, gcols]
            yz = y * (szv * _sigmoid(szv))
            rr = lax.rsqrt(jnp.mean(yz * yz, axis=-1, keepdims=True) + EPS)
            yb_ref[:, gcols] = (yz * rr * snw_ref[:, gcols]).astype(bf16)
            y_ref[:, gcols] = y.astype(bf16)
            return carry

        lax.fori_loop(0, NG, group, 0)

    const2 = lambda c: (0, 0)
    return pl.pallas_call(
        body, grid=(nc,),
        in_specs=[pl.BlockSpec((CH, CONVD), lambda c: (c, 0)), pl.BlockSpec((CH, DIN), lambda c: (c, CONVD // DIN)),
                  pl.BlockSpec((CH, 256), lambda c: (c, (CONVD + DIN) // 256)),
                  pl.BlockSpec((4, CONVD), const2), pl.BlockSpec((1, CONVD), const2),
                  pl.BlockSpec((1, CH), const2), pl.BlockSpec((1, CH), const2),
                  pl.BlockSpec((1, DIN), const2), pl.BlockSpec((1, DIN), const2)],
        out_specs=[pl.BlockSpec((CH, DIN), lambda c: (c, 0)), pl.BlockSpec((CH, DIN), lambda c: (c, 0)),
                   pl.BlockSpec((1, NH, HD, NST), lambda c: (c, 0, 0, 0)), pl.BlockSpec((CH, CONVD), lambda c: (c, 0))],
        out_shape=[jax.ShapeDtypeStruct((s, DIN), bf16), jax.ShapeDtypeStruct((s, DIN), bf16),
                   jax.ShapeDtypeStruct((nc, NH, HD, NST), bf16), jax.ShapeDtypeStruct((s, CONVD), bf16)],
        scratch_shapes=[pltpu.VMEM((8, CONVD), f32), pltpu.VMEM((NH, HD, NST), f32), pltpu.VMEM((CH, CH), f32)],
        name=name, compiler_params=_cp(("arbitrary",)),
    )(proj_b, proj_b, proj_b, cw, cb, dtb, a_row, dkc, snw)


def _branch_b_bwd(proj_b, conv_sv, dyb, y_sv, states, cw, dtb, a_row, dkc, snw, ind, name):
    s = proj_b.shape[0]
    nc = s // CH

    def body(xbc_ref, cv_ref, sz_ref, dtr_ref, dyb_ref, y_ref, st_ref, cw_ref, dtb_ref, a_ref, dkc_ref,
             snw_ref, ind_ref, dp_ref, dcw_ref, dcb_ref, ddtb_ref, dal_ref, ddk_ref, dsnw_ref,
             dstate, dnext8, acst_s, dacs_acc, q2_acc):
        @pl.when(pl.program_id(0) == 0)
        def _():
            dstate[...] = jnp.zeros_like(dstate)
            dnext8[...] = jnp.zeros_like(dnext8)
            dcw_ref[...] = jnp.zeros_like(dcw_ref)
            dcb_ref[...] = jnp.zeros_like(dcb_ref)
            ddtb_ref[...] = jnp.zeros_like(ddtb_ref)
            dal_ref[...] = jnp.zeros_like(dal_ref)
            ddk_ref[...] = jnp.zeros_like(ddk_ref)
            dsnw_ref[...] = jnp.zeros_like(dsnw_ref)

        dacs_acc[...] = jnp.zeros_like(dacs_acc)
        q2_acc[...] = jnp.zeros_like(q2_acc)
        mask = _tril_mask()
        tri_t = (lax.broadcasted_iota(jnp.int32, (CH, CH), 0) <= lax.broadcasted_iota(jnp.int32, (CH, CH), 1)).astype(f32)
        lane1 = lax.broadcasted_iota(jnp.int32, (1, CH), 1)
        is_last = lax.broadcasted_iota(jnp.int32, (CH, 1), 0) == CH - 1
        z_all = dtr_ref[:, 0:CH] + dtb_ref[...]
        dt_all = _softplus(z_all)
        adt_all = dt_all * a_ref[...]
        acs_all = _dot_hi(mask.astype(f32), adt_all)
        acst_s[...] = acs_all.T

        def ind_sum(v):
            hi = v.astype(bf16)
            lo = (v - hi.astype(f32)).astype(bf16)
            return _dot(hi, ind_ref[...]) + _dot(lo, ind_ref[...])

        def group(g, carry):
            cols = _lanes(g, GB)
            gcols = _lanes(g, GW)
            conv = cv_ref[:, cols].astype(f32)
            sg = _sigmoid(conv)
            xc = conv * sg
            xs = xc[:, 0:GW]
            bg = xc[:, GW:GW + NST].astype(bf16)
            cg = xc[:, GW + NST:GB].astype(bf16)

            y = y_ref[:, gcols].astype(f32)
            silu_sz, dsilu_sz = _silu_and_grad(sz_ref[:, gcols])
            yz = y * silu_sz
            rr = lax.rsqrt(jnp.mean(yz * yz, axis=-1, keepdims=True) + EPS)
            dyb_g = dyb_ref[:, gcols].astype(f32)
            w = dyb_g * snw_ref[:, gcols]
            dsnw_ref[:, gcols] += jnp.sum(dyb_g * yz * rr, axis=0, keepdims=True)
            dyz = rr * w - yz * (rr * rr * rr) * jnp.mean(w * yz, axis=-1, keepdims=True)
            dp_ref[:, _lanes(g, GW, CONVD)] = (dyz * y * dsilu_sz).astype(bf16)
            dy_g = dyz * silu_sz
            ddk_ref[:, gcols] += jnp.sum(dy_g * xs, axis=0, keepdims=True)

            back = lax.rem(CH - HPG * g, CH)
            dt = pltpu.roll(dt_all, back, 1)
            acs = pltpu.roll(acs_all, back, 1)
            cbm = _dot_nt(cg, bg)
            d_cb = jnp.zeros((CH, CH), f32)
            d_bg = jnp.zeros((CH, NST), f32)
            d_cg = jnp.zeros((CH, NST), f32)
            lastrow = jnp.zeros((1, CH), f32)
            dxd_parts, dxs_parts, t_parts = [], [], []
            for r in range(HPG):
                h = g * HPG + r
                colb = jnp.broadcast_to(acs[:, r:r + 1], (CH, CH))
                row = acst_s[pl.ds(h, 1), :]
                lmat = jnp.exp(jnp.where(mask, colb - row, -jnp.inf))
                mmat_b = (cbm * lmat).astype(bf16)
                dtc = dt[:, r:r + 1]
                xd = xs[:, r * HD:(r + 1) * HD] * dtc
                col = colb[:, 0:HD]
                alast = colb[CH - 1:CH, 0:HD]
                dte = jnp.exp(alast - col)
                ea = jnp.exp(col)
                cd = jnp.exp(colb[CH - 1:CH, :])
                sp = st_ref[0, h]
                dsn = dstate[h]
                dsn_b = dsn.astype(bf16)
                dyr = dy_g[:, r * HD:(r + 1) * HD]
                dyr_b = dyr.astype(bf16)
                dye_b = (dyr * ea).astype(bf16)
                d_cg = d_cg + _dot(dye_b, sp)
                dxde = _dot_nt(bg, dsn_b)
                xdte = xd * dte
                xd_b = xd.astype(bf16)
                d_bg = d_bg + _dot(xdte.astype(bf16), dsn_b)
                dxd_diag = _dot_tn(mmat_b, dyr_b)
                dxd = dxde * dte + dxd_diag
                d_cb = d_cb + _dot_nt(dyr_b, xd_b) * lmat
                t_parts.append(dyr_b.astype(f32) * _dot(mmat_b, xd_b) + dyr * (ea * _dot_nt(cg, sp))
                               - xd_b.astype(f32) * dxd_diag - dxde * xdte)
                lastrow = jnp.where(lane1 == r, _sum_all(dsn * sp.astype(f32)) * cd + _sum_all(dxde * xdte), lastrow)
                dstate[h] = cd * dsn + _dot_tn(dye_b, cg)
                dxd_parts.append(dxd)
                dxs_parts.append(dxd * dtc)
            d_cb_b = d_cb.astype(bf16)
            d_bg = d_bg + _dot_tn(d_cb_b, cg)
            d_cg = d_cg + _dot(d_cb_b, bg)
            q2 = ind_sum(jnp.concatenate(dxd_parts, axis=1) * xs)
            dacs = ind_sum(jnp.concatenate(t_parts, axis=1)) + jnp.where(is_last, lastrow, 0.0)
            dacs_acc[...] += pltpu.roll(dacs, HPG * g, 1)
            q2_acc[...] += pltpu.roll(q2, HPG * g, 1)
            dxs = jnp.concatenate(dxs_parts, axis=1) + dy_g * dkc_ref[:, gcols]

            dconv = jnp.concatenate([dxs, d_bg, d_cg], axis=1) * (sg * (1.0 + conv * (1.0 - sg)))
            x = xbc_ref[:, cols]
            dcb_ref[:, cols] += jnp.sum(dconv, axis=0, keepdims=True)
            dcw_ref[3:4, cols] += jnp.sum(dconv * x, axis=0, keepdims=True)
            dx = cw_ref[3:4, cols] * dconv
            nxt = dnext8[:, cols]
            for j in (1, 2, 3):
                up = _shift_rows_up(dconv, nxt, j)
                dcw_ref[3 - j:4 - j, cols] += jnp.sum(up * x, axis=0, keepdims=True)
                dx = dx + cw_ref[3 - j:4 - j, cols] * up
            dnext8[:, cols] = dconv[0:8]
            dp_ref[:, cols] = dx.astype(bf16)
            return carry

        lax.fori_loop(0, NG, group, 0)

        dadt = _dot_hi(tri_t, dacs_acc[...])
        dal_ref[...] += jnp.sum(dadt * adt_all, axis=0, keepdims=True)
        ddz = (dadt * a_ref[...] + q2_acc[...]) * _sigmoid(z_all)
        ddtb_ref[...] += jnp.sum(ddz, axis=0, keepdims=True)
        dp_ref[:, CONVD + DIN:CONVD + DIN + CH] = ddz.astype(bf16)
        dp_ref[:, CONVD + DIN + CH:PB] = jnp.zeros((CH, PB - CONVD - DIN - CH), bf16)

    const2 = lambda c: (0, 0)
    rev = lambda c: (nc - 1 - c, 0)
    return pl.pallas_call(
        body, grid=(nc,),
        in_specs=[pl.BlockSpec((CH, CONVD), rev), pl.BlockSpec((CH, CONVD), rev),
                  pl.BlockSpec((CH, DIN), lambda c: (nc - 1 - c, CONVD // DIN)),
                  pl.BlockSpec((CH, 256), lambda c: (nc - 1 - c, (CONVD + DIN) // 256)),
                  pl.BlockSpec((CH, DIN), rev), pl.BlockSpec((CH, DIN), rev),
                  pl.BlockSpec((1, NH, HD, NST), lambda c: (nc - 1 - c, 0, 0, 0)),
                  pl.BlockSpec((4, CONVD), const2), pl.BlockSpec((1, CH), const2), pl.BlockSpec((1, CH), const2),
                  pl.BlockSpec((1, DIN), const2), pl.BlockSpec((1, DIN), const2), pl.BlockSpec((GW, CH), const2)],
        out_specs=[pl.BlockSpec((CH, PB), rev), pl.BlockSpec((4, CONVD), const2), pl.BlockSpec((1, CONVD), const2),
                   pl.BlockSpec((1, CH), const2), pl.BlockSpec((1, CH), const2), pl.BlockSpec((1, DIN), const2),
                   pl.BlockSpec((1, DIN), const2)],
        out_shape=[jax.ShapeDtypeStruct((s, PB), bf16), jax.ShapeDtypeStruct((4, CONVD), f32),
                   jax.ShapeDtypeStruct((1, CONVD), f32), jax.ShapeDtypeStruct((1, CH), f32),
                   jax.ShapeDtypeStruct((1, CH), f32), jax.ShapeDtypeStruct((1, DIN), f32),
                   jax.ShapeDtypeStruct((1, DIN), f32)],
        scratch_shapes=[pltpu.VMEM((NH, HD, NST), f32), pltpu.VMEM((8, CONVD), f32), pltpu.VMEM((CH, CH), f32),
                        pltpu.VMEM((CH, CH), f32), pltpu.VMEM((CH, CH), f32)],
        name=name, compiler_params=_cp(("arbitrary",)),
    )(proj_b, conv_sv, proj_b, proj_b, dyb, y_sv, states, cw, dtb, a_row, dkc, snw, ind)


def _merge_fwd(ya, yb, proj_g, x, gate, wa, wb, wo, name):
    s = x.shape[0]
    ts = min(512, s)

    def body(ya_ref, yb_ref, ga_ref, gb_ref, x_ref, gate_ref, wa_ref, wb_ref, wo_ref, xo_ref, pa_ref, pb_ref, mg_ref, o_ref):
        pa = _dot(ya_ref[...], wa_ref[...])
        pb = _dot(yb_ref[...], wb_ref[...])
        mg = (_sigmoid(ga_ref[...]) * pa + _sigmoid(gb_ref[...]) * pb).astype(bf16)
        o = _dot(mg, wo_ref[...])
        xo_ref[...] = x_ref[...] + gate_ref[...] * o
        pa_ref[...] = pa.astype(bf16)
        pb_ref[...] = pb.astype(bf16)
        mg_ref[...] = mg
        o_ref[...] = o.astype(bf16)

    tile = pl.BlockSpec((ts, D), lambda i: (i, 0))
    const = lambda i: (0, 0)
    act = jax.ShapeDtypeStruct((s, D), bf16)
    return pl.pallas_call(
        body, grid=(s // ts,),
        in_specs=[tile, pl.BlockSpec((ts, DIN), lambda i: (i, 0)), tile, pl.BlockSpec((ts, D), lambda i: (i, 1)), tile,
                  pl.BlockSpec((1, D), const), pl.BlockSpec((D, D), const), pl.BlockSpec((DIN, D), const),
                  pl.BlockSpec((D, D), const)],
        out_specs=[tile, tile, tile, tile, tile],
        out_shape=[jax.ShapeDtypeStruct((s, D), f32), act, act, act, act],
        name=name, compiler_params=_cp(("parallel",)),
    )(ya, yb, proj_g, proj_g, x, gate, wa, wb, wo)


def _merge_bwd(dxo, gate, o_sv, pa_sv, pb_sv, proj_g, wo, wa, wb, name):
    s = dxo.shape[0]
    ts = min(512, s)

    def body(dxo_ref, gate_ref, o_ref, pa_ref, pb_ref, ga_ref, gb_ref, wo_ref, wa_ref, wb_ref,
             do_ref, dpa_ref, dpb_ref, dg_ref, dya_ref, dyb_ref, dgate_ref):
        @pl.when(pl.program_id(0) == 0)
        def _():
            dgate_ref[...] = jnp.zeros_like(dgate_ref)
        dxo_v = dxo_ref[...]
        dgate_ref[...] += jnp.sum(dxo_v * o_ref[...].astype(f32), axis=0, keepdims=True)
        do = (dxo_v * gate_ref[...]).astype(bf16)
        do_ref[...] = do
        dmg = _dot_nt(do, wo_ref[...])
        sa = _sigmoid(ga_ref[...])
        sb = _sigmoid(gb_ref[...])
        dpa = (dmg * sa).astype(bf16)
        dpb = (dmg * sb).astype(bf16)
        dpa_ref[...] = dpa
        dpb_ref[...] = dpb
        dg_ref[:, 0:D] = (dmg * pa_ref[...].astype(f32) * sa * (1.0 - sa)).astype(bf16)
        dg_ref[:, D:2 * D] = (dmg * pb_ref[...].astype(f32) * sb * (1.0 - sb)).astype(bf16)
        dya_ref[...] = _dot_nt(dpa, wa_ref[...]).astype(bf16)
        dyb_ref[...] = _dot_nt(dpb, wb_ref[...]).astype(bf16)

    tile = pl.BlockSpec((ts, D), lambda i: (i, 0))
    const = lambda i: (0, 0)
    act = jax.ShapeDtypeStruct((s, D), bf16)
    return pl.pallas_call(
        body, grid=(s // ts,),
        in_specs=[tile, pl.BlockSpec((1, D), const), tile, tile, tile, tile, pl.BlockSpec((ts, D), lambda i: (i, 1)),
                  pl.BlockSpec((D, D), const), pl.BlockSpec((D, D), const), pl.BlockSpec((DIN, D), const)],
        out_specs=[tile, tile, tile, pl.BlockSpec((ts, PG), lambda i: (i, 0)), tile, pl.BlockSpec((ts, DIN), lambda i: (i, 0)),
                   pl.BlockSpec((1, D), const)],
        out_shape=[act, act, act, jax.ShapeDtypeStruct((s, PG), bf16), act, jax.ShapeDtypeStruct((s, DIN), bf16),
                   jax.ShapeDtypeStruct((1, D), f32)],
        name=name, compiler_params=_cp(("arbitrary",)),
    )(dxo, gate, o_sv, pa_sv, pb_sv, proj_g, proj_g, wo, wa, wb)


def _final_loss(x, target, fnw, name):
    s = x.shape[0]
    ts = min(512, s)

    def body(x_ref, t_ref, w_ref, loss_ref, dx_ref, dw_ref):
        @pl.when(pl.program_id(0) == 0)
        def _():
            loss_ref[...] = jnp.zeros_like(loss_ref)
            dw_ref[...] = jnp.zeros_like(dw_ref)
        xv = x_ref[...]
        r = lax.rsqrt(jnp.mean(xv * xv, axis=-1, keepdims=True) + EPS)
        xn = xv * r
        err = xn * w_ref[...] - t_ref[...]
        part = jnp.sum(err * err, axis=0, keepdims=True)
        acc = part[:, 0:128]
        for k in range(1, D // 128):
            acc = acc + part[:, k * 128:(k + 1) * 128]
        loss_ref[0:1, :] += acc * (0.5 / D)
        dy = err * (1.0 / D)
        dw_ref[...] += jnp.sum(dy * xn, axis=0, keepdims=True)
        dxn = dy * w_ref[...]
        dx_ref[...] = r * (dxn - xn * jnp.mean(dxn * xn, axis=-1, keepdims=True))

    tile = pl.BlockSpec((ts, D), lambda i: (i, 0))
    row = pl.BlockSpec((1, D), lambda i: (0, 0))
    return pl.pallas_call(
        body, grid=(s // ts,), in_specs=[tile, tile, row],
        out_specs=[pl.BlockSpec((8, 128), lambda i: (0, 0)), tile, row],
        out_shape=[jax.ShapeDtypeStruct((8, 128), f32), jax.ShapeDtypeStruct((s, D), f32), jax.ShapeDtypeStruct((1, D), f32)],
        name=name, compiler_params=_cp(("arbitrary",)),
    )(x, target, fnw)


def _layer_operands(w_in, conv_w, wa, wb, wo, norm_w, gm_ln_w, gm_ln_b, gm_ws, gm_bs, conv_b, dt_bias, a_log, d_skip, ssm_norm_w):
    sz0, x0 = PA, PA + DIN
    b0, c0, dt0 = x0 + DIN, x0 + DIN + NG * NST, x0 + CONVD
    w_xbc = _group_major(w_in[:, x0:b0], w_in[:, b0:c0], w_in[:, c0:dt0])
    w_b = jnp.concatenate([w_xbc, w_in[:, sz0:x0], w_in[:, dt0:dt0 + NH], jnp.zeros((D, PB - CONVD - DIN - NH), bf16)], axis=1)
    w_a = w_in[:, 0:PA]
    w_g = w_in[:, dt0 + NH:N_IN]
    tril = jnp.tril(jnp.ones((CH, CH), bool))
    wsm = jnp.where(tril[None], gm_ws, 0.0).astype(bf16)

    def heads_row(v):
        return jnp.pad(v, (0, CH - NH)).reshape(1, CH)

    return dict(
        w_b=w_b, w_a=w_a, w_g=w_g, wa=wa, wb=wb, wo=wo,
        norm_w=norm_w.reshape(1, D), lnw=gm_ln_w.reshape(1, D), lnb=gm_ln_b.reshape(1, D),
        wsm=wsm, wsm_t=jnp.swapaxes(wsm, 1, 2), bsf=jnp.repeat(gm_bs.T, CH, axis=1),
        cw=_group_major(conv_w[:, 0:DIN], conv_w[:, DIN:DIN + NG * NST], conv_w[:, DIN + NG * NST:CONVD]),
        cb=_group_major(conv_b[0:DIN], conv_b[DIN:DIN + NG * NST], conv_b[DIN + NG * NST:CONVD]).reshape(1, CONVD),
        dtb=heads_row(dt_bias), a_row=heads_row(-jnp.exp(a_log)),
        snw=ssm_norm_w.reshape(1, DIN), dkc=jnp.repeat(d_skip, HD).reshape(1, DIN),
        ind=(jnp.arange(GW)[:, None] // HD == jnp.arange(CH)[None, :]).astype(bf16),
    )


def _layer_fwd(x, shift, scale, gate, p, tag):
    proj_g, h = _norm_mm(x, p["norm_w"], scale, shift, p["w_g"], f"proj_g{tag}", emit_h=True)
    proj_b, = _norm_mm(x, p["norm_w"], scale, shift, p["w_b"], f"proj_b{tag}")
    proj_a, = _norm_mm(x, p["norm_w"], scale, shift, p["w_a"], f"proj_a{tag}")
    ya = _branch_a_fwd(proj_a, p["lnw"], p["lnb"], p["wsm"], p["bsf"], f"branch_a_fwd{tag}")
    yb, y_sv, states, conv_sv = _branch_b_fwd(proj_b, p["cw"], p["cb"], p["dtb"], p["a_row"], p["dkc"], p["snw"], f"branch_b_fwd{tag}")
    if "merge_weights" in p:
        p["wa"], p["wb"], p["wo"], yb = p.pop("merge_weights")(yb)
    x_out, pa, pb, mg, o = _merge_fwd(ya, yb, proj_g, x, gate, p["wa"], p["wb"], p["wo"], f"merge_fwd{tag}")
    saved = dict(x=x, h=h, proj_b=proj_b, proj_a=proj_a, proj_g=proj_g, ya=ya, yb=yb, y=y_sv, states=states, conv=conv_sv,
                 pa=pa, pb=pb, mg=mg, o=o, scale=scale, gate=gate)
    return x_out, saved


def _layer_bwd(dxo, sv, p, tag):
    do, dpa, dpb, dg, dya, dyb, dgate = _merge_bwd(dxo, sv["gate"], sv["o"], sv["pa"], sv["pb"], sv["proj_g"],
                                                   p["wo"], p["wa"], p["wb"], f"merge_bwd{tag}")
    d_wo = _mm_tn(sv["mg"], do, f"d_wo{tag}")
    d_wa = _mm_tn(sv["ya"], dpa, f"d_wa{tag}")
    d_wb = _mm_tn(sv["yb"], dpb, f"d_wb{tag}")
    da, dws, dbs, dlnw, dlnb = _branch_a_bwd(sv["proj_a"], dya, p["lnw"], p["lnb"], p["wsm"], p["wsm_t"], p["bsf"],
                                             f"branch_a_bwd{tag}")
    db, dcw, dcb, ddtb, dal, ddk, dsnw = _branch_b_bwd(sv["proj_b"], sv["conv"], dyb, sv["y"], sv["states"], p["cw"], p["dtb"],
                                                       p["a_row"], p["dkc"], p["snw"], p["ind"], f"branch_b_bwd{tag}")
    dh = _mm(db, p["w_b"], f32, f"dh_b{tag}", trans_b=True)
    dh = _mm(da, p["w_a"], f32, f"dh_a{tag}", c_in=dh, trans_b=True)
    dh = _mm(dg, p["w_g"], f32, f"dh_g{tag}", c_in=dh, trans_b=True)
    d_w_b_t = _mm_tn(db, sv["h"], f"d_w_b{tag}")
    d_w_a_t = _mm_tn(da, sv["h"], f"d_w_a{tag}")
    d_w_g_t = _mm_tn(dg, sv["h"], f"d_w_g{tag}")
    dx, dscale, dshift, dnw = _rmsmod_bwd(dh, sv["x"], dxo, p["norm_w"], sv["scale"], f"rmsmod_bwd{tag}")
    d_w_in_t = jnp.concatenate([d_w_a_t, d_w_b_t[CONVD:CONVD + DIN], _rows_from_group_major(d_w_b_t[0:CONVD]),
                                d_w_b_t[CONVD + DIN:CONVD + DIN + NH], d_w_g_t], axis=0)
    tril = jnp.tril(jnp.ones((CH, CH), bool))
    heads = lambda v: v[0, 0:NH]
    grads = dict(
        w_in_t=d_w_in_t, w_proj_a=d_wa, w_proj_b=d_wb, w_out=d_wo, conv_w=_from_group_major(dcw), conv_b=_from_group_major(dcb).reshape(CONVD),
        norm_w=dnw.reshape(D), gm_ln_w=dlnw.reshape(D), gm_ln_b=dlnb.reshape(D),
        gm_ws=jnp.where(tril[None], dws, 0.0), gm_bs=dbs.reshape(CH, NG, CH).sum(-1).T,
        dt_bias=heads(ddtb), a_log=heads(dal), d_skip=ddk.reshape(NH, HD).sum(-1), ssm_norm_w=dsnw.reshape(DIN),
        mod=jnp.concatenate([dshift, dscale, dgate], axis=1).reshape(3 * D),
    )
    return dx, grads


def _local_step(x, target, mods, operands_of, fnw):
    saved, layer_ops = [], []
    for l in range(DEPTH):
        shift, scale, gate = mods[l]
        p, x = operands_of(l, x)
        layer_ops.append(p)
        x, sv = _layer_fwd(x, shift, scale, gate, p, f"_l{l}")
        saved.append(sv)
    loss_parts, dx, dfnw = _final_loss(x, target, fnw.reshape(1, D), "final_loss")
    grads = [None] * DEPTH
    for l in reversed(range(DEPTH)):
        dx, grads[l] = _layer_bwd(dx, saved[l], layer_ops[l], f"_l{l}")
    return loss_parts, dx, grads, dfnw.reshape(D)


ADA_COLS = 3 * D // NSHARD


def _ada_fwd(c_all, ada_w, ada_b_cols, name):
    def body(c_ref, w_ref, b_ref, o_ref):
        cv = c_ref[...]
        sc = cv * _sigmoid(cv)
        for l in range(DEPTH):
            o_ref[l] = _dot_hi(sc, w_ref[l]) + b_ref[l]

    return pl.pallas_call(body, out_shape=jax.ShapeDtypeStruct((DEPTH, 8, ADA_COLS), f32), name=name,
                          compiler_params=_cp(None))(c_all, ada_w, ada_b_cols)


def _adam_math(w, g, m, v):
    m = ADAM_B1 * m + (1.0 - ADAM_B1) * g
    v = ADAM_B2 * v + (1.0 - ADAM_B2) * (g * g)
    m_hat = m / (1.0 - ADAM_B1 ** ADAM_STEP)
    v_hat = v / (1.0 - ADAM_B2 ** ADAM_STEP)
    delta = -ADAM_LR * (m_hat / (jnp.sqrt(v_hat) + ADAM_EPS) + ADAM_WD * w)
    return delta, m, v


def _ada_bwd_adamw(c_all, dmod_cols, w, m, v, name):
    tr = 256

    def body(c_ref, dm_ref, w_ref, m_ref, v_ref, g_ref, d_ref, nm_ref, nv_ref):
        cv = c_ref[...]
        sc = cv * _sigmoid(cv)
        g = lax.dot_general(sc, dm_ref[0], (((0,), (0,)), ((), ())), precision=_HI, preferred_element_type=f32)
        g_ref[0] = g
        d_ref[0], nm_ref[0], nv_ref[0] = _adam_math(w_ref[0], g, m_ref[0], v_ref[0])

    blk = pl.BlockSpec((1, tr, ADA_COLS), lambda l, i: (l, i, 0))
    shp = jax.ShapeDtypeStruct((DEPTH, D, ADA_COLS), f32)
    return pl.pallas_call(
        body, grid=(DEPTH, D // tr),
        in_specs=[pl.BlockSpec((8, tr), lambda l, i: (0, i)), pl.BlockSpec((1, 8, ADA_COLS), lambda l, i: (l, 0, 0)), blk, blk, blk],
        out_specs=[blk, blk, blk, blk], out_shape=[shp, shp, shp, shp], name=name, compiler_params=_cp(("parallel", "parallel")),
    )(c_all, dmod_cols, w, m, v)


def _adamw(w, g, m, v, name):
    def body(w_ref, g_ref, m_ref, v_ref, d_ref, nm_ref, nv_ref):
        d_ref[...], nm_ref[...], nv_ref[...] = _adam_math(w_ref[...], g_ref[...], m_ref[...], v_ref[...])

    shp = jax.ShapeDtypeStruct(w.shape, f32)
    return pl.pallas_call(body, out_shape=[shp] * 3, name=name, compiler_params=_cp(None))(w, g, m, v)


def _tile2(r, c):
    if r <= 256 or r % 256 == 0:
        return _pick(r, (256,)), _pick(c, (1024,))
    return r, 128


def _adamw_layers(w, g_mine, g_other, m, v, c_idx, name):
    _, r, c = w.shape
    tr, tc = _tile2(r, c)

    def body(ci_ref, w_ref, gm_ref, go_ref, m_ref, v_ref, g_ref, d_ref, nm_ref, nv_ref):
        def update(g):
            g_ref[0] = g
            d_ref[0], nm_ref[0], nv_ref[0] = _adam_math(w_ref[0], g, m_ref[0], v_ref[0])

        mine = pl.program_id(0) == ci_ref[0]

        @pl.when(mine)
        def _():
            update(gm_ref[...])

        @pl.when(jnp.logical_not(mine))
        def _():
            update(go_ref[...])

    blk = pl.BlockSpec((1, tr, tc), lambda l, i, j, ci: (l, i, j))
    gblk = pl.BlockSpec((tr, tc), lambda l, i, j, ci: (i, j))
    shp = jax.ShapeDtypeStruct(w.shape, f32)
    return pl.pallas_call(
        body,
        grid_spec=pltpu.PrefetchScalarGridSpec(num_scalar_prefetch=1, grid=(DEPTH, r // tr, c // tc),
                                               in_specs=[blk, gblk, gblk, blk, blk], out_specs=[blk, blk, blk, blk]),
        out_shape=[shp, shp, shp, shp], name=name, compiler_params=_cp(("parallel", "parallel", "parallel")),
    )(c_idx, w, g_mine, g_other, m, v)


_MESH = pl.DeviceIdType.MESH
_AXES = ("x", "y", "c")
_HBM = pl.BlockSpec(memory_space=pltpu.HBM)


def _my_place():
    return tuple(lax.axis_index(a) for a in _AXES)


def _allreduce(buf, axes, name):
    r = buf.shape[0]
    n = len(axes)

    def body(x_ref, o_ref, rbuf, ssem, rsem):
        me = dict(zip(_AXES, _my_place()))
        o_ref[...] = x_ref[...]
        for k, ax in enumerate(axes):
            peer = tuple(1 - me[a] if a == ax else me[a] for a in _AXES)
            cp = pltpu.make_async_remote_copy(src_ref=o_ref, dst_ref=rbuf.at[k], send_sem=ssem.at[k], recv_sem=rsem.at[k],
                                              device_id=peer, device_id_type=_MESH)
            cp.start()
            cp.wait()
            o_ref[...] = o_ref[...] + rbuf[k]

    vm = pl.BlockSpec(memory_space=pltpu.VMEM)
    return pl.pallas_call(
        body, out_shape=jax.ShapeDtypeStruct((r, 128), f32), in_specs=[vm], out_specs=vm,
        scratch_shapes=[pltpu.VMEM((n, r, 128), f32), pltpu.SemaphoreType.DMA((n,)), pltpu.SemaphoreType.DMA((n,))],
        name=name, compiler_params=pltpu.CompilerParams(vmem_limit_bytes=VMEM_LIMIT),
    )(buf)


def _allreduce_halves(buf, name):
    r = buf.shape[0]
    h = r // 2
    assert r % 16 == 0

    def body(x_ref, o_ref, rbuf, ssem, rsem):
        x, y, c = _my_place()
        mine = pl.ds(pl.multiple_of(c * h, 8), h)
        other = pl.ds(pl.multiple_of((1 - c) * h, 8), h)
        o_ref[...] = x_ref[...]
        steps = [(other, (x, y, 1 - c)), (mine, (1 - x, y, c)), (mine, (x, 1 - y, c))]
        for k, (rows, peer) in enumerate(steps):
            cp = pltpu.make_async_remote_copy(src_ref=o_ref.at[rows], dst_ref=rbuf.at[k], send_sem=ssem.at[k], recv_sem=rsem.at[k],
                                              device_id=peer, device_id_type=_MESH)
            cp.start()
            cp.wait()
            o_ref[mine, :] = o_ref[mine, :] + rbuf[k]
        cp = pltpu.make_async_remote_copy(src_ref=o_ref.at[mine], dst_ref=rbuf.at[3], send_sem=ssem.at[3], recv_sem=rsem.at[3],
                                          device_id=(x, y, 1 - c), device_id_type=_MESH)
        cp.start()
        cp.wait()
        o_ref[other, :] = rbuf[3]

    vm = pl.BlockSpec(memory_space=pltpu.VMEM)
    return pl.pallas_call(
        body, out_shape=jax.ShapeDtypeStruct((r, 128), f32), in_specs=[vm], out_specs=vm,
        scratch_shapes=[pltpu.VMEM((4, h, 128), f32), pltpu.SemaphoreType.DMA((4,)), pltpu.SemaphoreType.DMA((4,))],
        name=name, compiler_params=pltpu.CompilerParams(vmem_limit_bytes=VMEM_LIMIT),
    )(buf)


def _other_chips(x, y):
    return [(1 - x, y), (x, 1 - y), (1 - x, 1 - y)]


def _gather_body(ins, outs, ssem, rsem):
    na = len(ins)
    x, y, c = _my_place()
    k_me = 2 * x + y
    sibling = (x, y, 1 - c)
    chips = _other_chips(x, y)
    slots = [2 * cx + cy for cx, cy in chips]
    half = [r.shape[0] // 2 if r.shape[0] % 32 == 0 else None for r in ins]

    def part(ref, a, core):
        return ref if half[a] is None else ref.at[pl.ds(core * half[a], half[a])]

    def rcopy(a, src, slot, core, to, idx):
        return pltpu.make_async_remote_copy(src_ref=src, dst_ref=part(outs[a].at[slot], a, core), send_sem=ssem.at[idx],
                                            recv_sem=rsem.at[idx], device_id=to, device_id_type=_MESH)

    sent = []
    for j, chip in enumerate(chips):
        for a in range(na):
            cp = rcopy(a, part(ins[a], a, c), k_me, c, (*chip, c), j * na + a)
            cp.start()
            sent.append(cp)
    for j, chip in enumerate(chips):
        for a in range(na):
            rcopy(a, part(ins[a], a, c), slots[j], c, (*chip, c), j * na + a).wait_recv()
            if half[a] is not None:
                cp = rcopy(a, part(outs[a].at[slots[j]], a, c), slots[j], c, sibling, (3 + j) * na + a)
                cp.start()
                sent.append(cp)
    for j in range(3):
        for a in range(na):
            if half[a] is not None:
                rcopy(a, part(ins[a], a, c), slots[j], 1 - c, sibling, (3 + j) * na + a).wait_recv()
    for cp in sent:
        cp.wait_send()


def _gather_layer_behind(shards, name, collective_id):
    na = len(shards)
    hbm = pltpu.MemorySpace.HBM
    ins = [jax.new_ref(s, memory_space=hbm) for s in shards]
    outs = [jax.empty_ref(jax.ShapeDtypeStruct((NSHARD,) + s.shape, s.dtype), memory_space=hbm) for s in shards]

    @pl.kernel(mesh=plsc.ScalarSubcoreMesh(axis_name="sequencer", num_cores=1), name=name,
               scratch_types=(pltpu.SemaphoreType.DMA((6 * na,)), pltpu.SemaphoreType.DMA((6 * na,))),
               compiler_params=pltpu.CompilerParams(collective_id=collective_id))
    def launch(ssem, rsem):
        x, y, c = _my_place()
        barrier = pltpu.get_barrier_semaphore()
        peers = [(*chip, c) for chip in _other_chips(x, y)] + [(x, y, 1 - c)]
        for peer in peers:
            pl.semaphore_signal(barrier, inc=1, device_id=peer, device_id_type=_MESH)
        pl.semaphore_wait(barrier, len(peers))
        _gather_body(ins, outs, ssem, rsem)

    launch()
    return [o[...] for o in outs]


def _with_own(gathered, own):
    xi, yi, _ = _my_place()
    whole = lax.dynamic_update_index_in_dim(gathered, own, 2 * xi + yi, 0)
    return [whole[k] for k in range(NSHARD)]


def _swap_layers(parts, name):
    na = len(parts)

    def body(*refs):
        ins, outs = refs[:2 * na], refs[2 * na:3 * na]
        ssem, rsem = refs[3 * na:]
        x, y, c = _my_place()

        def copy(a, layer):
            return pltpu.make_async_remote_copy(src_ref=ins[2 * a + layer], dst_ref=outs[a], send_sem=ssem.at[a], recv_sem=rsem.at[a],
                                                device_id=(x, y, 1 - c), device_id_type=_MESH)

        for layer in range(DEPTH):
            @pl.when(c == 1 - layer)
            def _():
                for a in range(na):
                    copy(a, layer).start()
        for a in range(na):
            copy(a, 0).wait()

    flat = [p for pair in parts for p in pair]
    return pl.pallas_call(
        body, out_shape=[jax.ShapeDtypeStruct(p0.shape, p0.dtype) for p0, _ in parts], in_specs=[_HBM] * (2 * na),
        out_specs=[_HBM] * na, scratch_shapes=[pltpu.SemaphoreType.DMA((na,)), pltpu.SemaphoreType.DMA((na,))], name=name,
    )(*flat)


def _scatter_shards_behind(sums, name, collective_id):
    na = len(sums)
    hbm = pltpu.MemorySpace.HBM
    ins = [jax.new_ref(p, memory_space=hbm) for p in sums]
    outs = [jax.empty_ref(jax.ShapeDtypeStruct((3,) + p.shape[1:], p.dtype), memory_space=hbm) for p in sums]

    @pl.kernel(mesh=plsc.ScalarSubcoreMesh(axis_name="sequencer", num_cores=1), name=name,
               scratch_types=(pltpu.SemaphoreType.DMA((3 * na,)), pltpu.SemaphoreType.DMA((3 * na,))),
               compiler_params=pltpu.CompilerParams(collective_id=collective_id))
    def launch(ssem, rsem):
        x, y, c = _my_place()
        barrier = pltpu.get_barrier_semaphore()
        chips = _other_chips(x, y)
        for chip in chips:
            pl.semaphore_signal(barrier, inc=1, device_id=(*chip, c), device_id_type=_MESH)
        pl.semaphore_wait(barrier, len(chips))
        cps = []
        for j, chip in enumerate(chips):
            kj = 2 * chip[0] + chip[1]
            for a in range(na):
                cps.append(pltpu.make_async_remote_copy(
                    src_ref=ins[a].at[kj], dst_ref=outs[a].at[j], send_sem=ssem.at[j * na + a], recv_sem=rsem.at[j * na + a],
                    device_id=(*chip, c), device_id_type=_MESH))
        for cp in cps:
            cp.start()
        for cp in cps:
            cp.wait()

    launch()
    return [o[...] for o in outs]


def _share_layers(finals, name):
    na = len(finals)

    def body(*refs):
        ins, outs = refs[:na], refs[na:2 * na]
        ssem, rsem = refs[2 * na:]
        x, y, c = _my_place()
        cps = [pltpu.make_async_remote_copy(src_ref=ins[a], dst_ref=outs[a], send_sem=ssem.at[a], recv_sem=rsem.at[a],
                                            device_id=(x, y, 1 - c), device_id_type=_MESH) for a in range(na)]
        for cp in cps:
            cp.start()
        for cp in cps:
            cp.wait()

    return pl.pallas_call(
        body, out_shape=[jax.ShapeDtypeStruct(p.shape, p.dtype) for p in finals], in_specs=[_HBM] * na, out_specs=[_HBM] * na,
        scratch_shapes=[pltpu.SemaphoreType.DMA((na,)), pltpu.SemaphoreType.DMA((na,))], name=name,
    )(*finals)


def _add_own_layer(part0, part1, recv, c_idx, name):
    ns, r, c = recv.shape
    tr, tc = _tile2(r, c)

    def body(ci_ref, p0_ref, p1_ref, r_ref, o_ref, ob_ref):
        def add(p_ref):
            t = p_ref[...] + r_ref[...]
            o_ref[...] = t
            ob_ref[...] = t.astype(bf16)

        @pl.when(ci_ref[0] == 0)
        def _():
            add(p0_ref)

        @pl.when(ci_ref[0] == 1)
        def _():
            add(p1_ref)

    blk = pl.BlockSpec((1, tr, tc), lambda k, i, j, ci: (k, i, j))
    blk0 = pl.BlockSpec((1, tr, tc), lambda k, i, j, ci: (k * (1 - ci[0]), i * (1 - ci[0]), j * (1 - ci[0])))
    blk1 = pl.BlockSpec((1, tr, tc), lambda k, i, j, ci: (k * ci[0], i * ci[0], j * ci[0]))
    return pl.pallas_call(
        body,
        grid_spec=pltpu.PrefetchScalarGridSpec(num_scalar_prefetch=1, grid=(ns, r // tr, c // tc), in_specs=[blk0, blk1, blk],
                                               out_specs=[blk, blk]),
        out_shape=[jax.ShapeDtypeStruct((ns, r, c), f32), jax.ShapeDtypeStruct((ns, r, c), bf16)], name=name,
        compiler_params=_cp(("arbitrary", "arbitrary", "arbitrary")),
    )(c_idx, part0, part1, recv)


def _add_own_shard(sums, recv, k_idx, name):
    _, r, c = sums.shape
    tr, tc = _tile2(r, c)

    def body(ki_ref, s_ref, r_ref, o_ref):
        o_ref[...] = ((s_ref[0] + r_ref[0].astype(f32)) + r_ref[1].astype(f32)) + r_ref[2].astype(f32)

    return pl.pallas_call(
        body,
        grid_spec=pltpu.PrefetchScalarGridSpec(
            num_scalar_prefetch=1, grid=(r // tr, c // tc),
            in_specs=[pl.BlockSpec((1, tr, tc), lambda i, j, ki: (ki[0], i, j)), pl.BlockSpec((3, tr, tc), lambda i, j, ki: (0, i, j))],
            out_specs=pl.BlockSpec((tr, tc), lambda i, j, ki: (i, j))),
        out_shape=jax.ShapeDtypeStruct((r, c), f32), name=name, compiler_params=_cp(("parallel", "parallel")),
    )(k_idx, sums, recv)


def _reduce_scatter(parts, tag, meanwhile):
    x, y, c = _my_place()
    c_idx = jnp.reshape(c, (1,)).astype(jnp.int32)
    k_idx = jnp.reshape(2 * x + y, (1,)).astype(jnp.int32)
    na = len(parts)
    recv = _swap_layers(parts, f"rs_swap{tag}")
    sums = [_add_own_layer(parts[a][0], parts[a][1], recv[a], c_idx, f"rs_add_layer{tag}_{a}") for a in range(na)]
    recv = _scatter_shards_behind([sb for _, sb in sums], f"rs_scatter{tag}", 3)
    recv, side = lax.optimization_barrier((recv, meanwhile()))
    finals = [_add_own_shard(sums[a][0], recv[a], k_idx, f"rs_add_shard{tag}_{a}") for a in range(na)]
    return finals, _share_layers(finals, f"rs_share{tag}"), c_idx, side


_SMALL = [("ada_b", (DEPTH, 3 * D)), ("norm_w", (DEPTH, D)), ("gm_ln_w", (DEPTH, D)), ("gm_ln_b", (DEPTH, D)),
          ("gm_ws", (DEPTH, NG, CH, CH)), ("gm_bs", (DEPTH, NG, CH)), ("conv_b", (DEPTH, CONVD)), ("dt_bias", (DEPTH, NH)),
          ("a_log", (DEPTH, NH)), ("d_skip", (DEPTH, NH)), ("ssm_norm_w", (DEPTH, DIN)), ("final_norm_w", (D,))]


def _rows_of(shape):
    n = 1
    for d in shape:
        n *= d
    return -(-n // 1024) * 8


def _pack(arrays):
    rows = []
    for a in arrays:
        flat = a.reshape(-1)
        r = _rows_of(a.shape)
        rows.append(jnp.pad(flat, (0, r * 128 - flat.shape[0])).reshape(r, 128))
    return jnp.concatenate(rows, axis=0)


def _unpack(buf, shapes):
    out, at = [], 0
    for shp in shapes:
        r = _rows_of(shp)
        n = 1
        for d in shp:
            n *= d
        out.append(buf[at:at + r].reshape(-1)[:n].reshape(shp))
        at += r
    return out


def kernel(x, c, ada_w, ada_b, norm_w, w_in, gm_ln_w, gm_ln_b, gm_ws, gm_bs, conv_w, conv_b, dt_bias, a_log, d_skip, ssm_norm_w, w_proj_a, w_proj_b, w_out, final_norm_w, loss_target, m_ada_w, m_ada_b, m_norm_w, m_w_in, m_gm_ln_w, m_gm_ln_b, m_gm_ws, m_gm_bs, m_conv_w, m_conv_b, m_dt_bias, m_a_log, m_d_skip, m_ssm_norm_w, m_w_proj_a, m_w_proj_b, m_w_out, m_final_norm_w, v_ada_w, v_ada_b, v_norm_w, v_w_in, v_gm_ln_w, v_gm_ln_b, v_gm_ws, v_gm_bs, v_conv_w, v_conv_b, v_dt_bias, v_a_log, v_d_skip, v_ssm_norm_w, v_w_proj_a, v_w_proj_b, v_w_out, v_final_norm_w):
    xi, yi, ci = _my_place()
    k_me = 2 * xi + yi
    b_me = 4 * xi + 2 * yi + ci
    w = dict(ada_b=ada_b, norm_w=norm_w, gm_ln_w=gm_ln_w, gm_ln_b=gm_ln_b, gm_ws=gm_ws, gm_bs=gm_bs, conv_b=conv_b, dt_bias=dt_bias,
             a_log=a_log, d_skip=d_skip, ssm_norm_w=ssm_norm_w, final_norm_w=final_norm_w)
    m = dict(ada_b=m_ada_b, norm_w=m_norm_w, gm_ln_w=m_gm_ln_w, gm_ln_b=m_gm_ln_b, gm_ws=m_gm_ws, gm_bs=m_gm_bs, conv_b=m_conv_b,
             dt_bias=m_dt_bias, a_log=m_a_log, d_skip=m_d_skip, ssm_norm_w=m_ssm_norm_w, final_norm_w=m_final_norm_w)
    v = dict(ada_b=v_ada_b, norm_w=v_norm_w, gm_ln_w=v_gm_ln_w, gm_ln_b=v_gm_ln_b, gm_ws=v_gm_ws, gm_bs=v_gm_bs, conv_b=v_conv_b,
             dt_bias=v_dt_bias, a_log=v_a_log, d_skip=v_d_skip, ssm_norm_w=v_ssm_norm_w, final_norm_w=v_final_norm_w)

    c_slot = lax.dynamic_update_slice(jnp.zeros((8, D), f32), c, (b_me, 0))
    c_all = _allreduce(c_slot.reshape(64, 128), _AXES, "gather_c").reshape(8, D)
    ada_b_cols = lax.dynamic_slice(ada_b, (0, k_me * ADA_COLS), (DEPTH, ADA_COLS)).reshape(DEPTH, 1, ADA_COLS)
    mod_cols = _ada_fwd(c_all, ada_w, ada_b_cols, "ada_fwd")
    mod_slot = lax.dynamic_update_slice(jnp.zeros((DEPTH, 8, 3 * D), f32), mod_cols, (0, 0, k_me * ADA_COLS))
    mod_all = _allreduce(mod_slot.reshape(-1, 128), ("x", "y"), "gather_mod").reshape(DEPTH, 8, 3 * D)
    mod_me = lax.dynamic_slice(mod_all, (0, b_me, 0), (DEPTH, 1, 3 * D))
    mods = [(mod_me[l, :, 0:D], mod_me[l, :, D:2 * D], mod_me[l, :, 2 * D:3 * D]) for l in range(DEPTH)]

    rows_sh = jnp.concatenate([w_proj_a, w_proj_b, w_out], axis=1).astype(bf16)
    win_sh = w_in.astype(bf16)
    first = _gather_layer_behind([win_sh[0], conv_w[0]], "gather_l0", 1)
    first, mods, next_in, later = lax.optimization_barrier((first, mods, [rows_sh[0]], [win_sh[1], conv_w[1], rows_sh[1]]))
    first_rows = _gather_layer_behind(next_in, "gather_l0_merge", 5)
    second = _gather_layer_behind(later, "gather_l1", 2)
    others = [(first[0], first[1], first_rows[0]), second]

    def merge_matrices(rows_g, l):
        rows_l = _with_own(rows_g, rows_sh[l])
        return (jnp.concatenate([t[0:RA] for t in rows_l], axis=0), jnp.concatenate([t[RA:RA + RB] for t in rows_l], axis=0),
                jnp.concatenate([t[RA + RB:] for t in rows_l], axis=0))

    def operands_of(l, x_in):
        win_g, conv_g, rows_g = others[l]
        if l == 1:
            (win_g, conv_g, rows_g), x_in = lax.optimization_barrier(((win_g, conv_g, rows_g), x_in))
        w_in_l = jnp.concatenate(_with_own(win_g, win_sh[l]), axis=1)
        cw_l = jnp.concatenate(_with_own(conv_g, conv_w[l]), axis=1)
        wa_l, wb_l, wo_l = merge_matrices(rows_g, l) if l == 1 else (None, None, None)
        p = _layer_operands(w_in_l, cw_l, wa_l, wb_l, wo_l, norm_w[l], gm_ln_w[l], gm_ln_b[l], gm_ws[l], gm_bs[l], conv_b[l],
                            dt_bias[l], a_log[l], d_skip[l], ssm_norm_w[l])
        if l == 0:
            def late(yb):
                rows_now, yb = lax.optimization_barrier((rows_g, yb))
                return (*merge_matrices(rows_now, l), yb)
            p["merge_weights"] = late
        return p, x_in

    loss_parts, dx, grads, dfnw = _local_step(x[0], loss_target[0], mods, operands_of, final_norm_w)

    s_in = N_IN // NSHARD
    tr_ = lambda t: jnp.swapaxes(t, 1, 2)
    g_in = [grads[l]["w_in_t"].reshape(NSHARD, s_in, D) for l in range(DEPTH)]
    g_rows = [jnp.concatenate([grads[l]["w_proj_a"].reshape(NSHARD, RA, D), grads[l]["w_proj_b"].reshape(NSHARD, RB, D),
                               grads[l]["w_out"].reshape(NSHARD, RA, D)], axis=1) for l in range(DEPTH)]
    g_conv = [grads[l]["conv_w"].reshape(4, NSHARD, D).transpose(1, 0, 2) for l in range(DEPTH)]

    dmod_slot = lax.dynamic_update_slice(jnp.zeros((DEPTH, 8, 3 * D), f32),
                                         jnp.stack([grads[l]["mod"] for l in range(DEPTH)]).reshape(DEPTH, 1, 3 * D), (0, b_me, 0))
    small_g = {n: (dfnw if n == "final_norm_w" else jnp.stack([grads[l]["mod" if n == "ada_b" else n] for l in range(DEPTH)]))
               for n, _ in _SMALL}
    n_small = sum(_rows_of(s) for _, s in _SMALL)
    n_dmod = _rows_of(dmod_slot.shape)

    def small_work():
        tail = [jnp.zeros((8, 128), f32)] if (n_small + n_dmod + 8) % 16 else []
        packed = _allreduce_halves(_pack([small_g[n] for n, _ in _SMALL] + [dmod_slot, loss_parts] + tail), "allreduce_small")
        dmod_all = packed[n_small:n_small + n_dmod].reshape(DEPTH, 8, 3 * D)
        dmod_cols = lax.dynamic_slice(dmod_all, (0, 0, k_me * ADA_COLS), (DEPTH, 8, ADA_COLS))
        return packed, _ada_bwd_adamw(c_all, dmod_cols, ada_w, m_ada_w, v_ada_w, "ada_bwd_adamw")

    (f_in, f_rows, f_conv), (o_in, o_rows, o_conv), c_idx, (packed, ada_out) = _reduce_scatter([g_in, g_rows, g_conv], "", small_work)
    g_ada, d_ada, nm_ada, nv_ada = ada_out
    gr_in, d_in, nm_in, nv_in = [tr_(t) for t in _adamw_layers(tr_(w_in), f_in, o_in, tr_(m_w_in), tr_(v_w_in), c_idx, "adamw_w_in")]
    rows_of = dict(w_proj_a=(0, RA), w_proj_b=(RA, RA + RB), w_out=(RA + RB, 2 * RA + RB))
    rows_upd = {n: _adamw_layers(wv, f_rows[lo:hi], o_rows[lo:hi], mv, vv, c_idx, f"adamw_{n}")
                for (n, (lo, hi)), wv, mv, vv in zip(rows_of.items(), (w_proj_a, w_proj_b, w_out), (m_w_proj_a, m_w_proj_b, m_w_out),
                                                     (v_w_proj_a, v_w_proj_b, v_w_out))}
    gr_conv = jnp.where(ci == 0, jnp.stack([f_conv, o_conv]), jnp.stack([o_conv, f_conv]))

    g_small = packed[0:n_small]
    loss = jnp.sum(packed[n_small + n_dmod:])
    small_gw = jnp.concatenate([g_small, _pack([gr_conv])], axis=0)
    d_s, nm_s, nv_s = _adamw(_pack([w[n] for n, _ in _SMALL] + [conv_w]), small_gw,
                             _pack([m[n] for n, _ in _SMALL] + [m_conv_w]), _pack([v[n] for n, _ in _SMALL] + [v_conv_w]), "adamw_small")
    shapes = [s for _, s in _SMALL] + [conv_w.shape]
    names = [n for n, _ in _SMALL] + ["conv_w"]
    g_d = dict(zip(names, _unpack(small_gw, shapes)))
    d_d = dict(zip(names, _unpack(d_s, shapes)))
    nm_d = dict(zip(names, _unpack(nm_s, shapes)))
    nv_d = dict(zip(names, _unpack(nv_s, shapes)))

    order = ["ada_w", "ada_b", "norm_w", "w_in", "gm_ln_w", "gm_ln_b", "gm_ws", "gm_bs", "conv_w", "conv_b", "dt_bias", "a_log",
             "d_skip", "ssm_norm_w", "w_proj_a", "w_proj_b", "w_out", "final_norm_w"]
    outs = []
    for i, (w_in_out, ada_out_i, small) in enumerate(((gr_in, g_ada, g_d), (d_in, d_ada, d_d), (nm_in, nm_ada, nm_d),
                                                      (nv_in, nv_ada, nv_d))):
        t = dict(small, ada_w=ada_out_i, w_in=w_in_out, **{n: rows_upd[n][i] for n in rows_of})
        outs += [t[n] for n in order]
    return (loss, dx.reshape(1, -1, D), *outs)
```
